```python
import math
import jax, jax.numpy as jnp
from jax import lax
import numpy as np

D_MODEL = 1024
BATCH = 1
SEQ = 16384
DEPTH = 1
DEC_BATCH = 128
DEC_SEQ = 4
PAST_LEN = 16384
PAGE_SIZE = 128

ATTN_HEADS = 8
KV_HEADS = 2
ATTN_HEAD_DIM = 64
ATTN_GROUP = ATTN_HEADS // KV_HEADS
ATTN_WIDTH = ATTN_HEADS * ATTN_HEAD_DIM
KV_WIDTH = KV_HEADS * ATTN_HEAD_DIM
WINDOW = 128
ATTN_BLOCK = WINDOW
ROPE_THETA = 10000.0
MLSTM_HEADS = 4
MLSTM_HEAD_DIM = 128
MLSTM_WIDTH = MLSTM_HEADS * MLSTM_HEAD_DIM
MLSTM_CHUNK = 64
MIX_WIDTH = ATTN_WIDTH + MLSTM_WIDTH
D_FF = -(-8 * D_MODEL // (3 * 256)) * 256
NORM_EPS = 1e-6
W_BUF = min(WINDOW, PAST_LEN)
SPLITS = [ATTN_WIDTH, KV_WIDTH, KV_WIDTH,
          MLSTM_WIDTH, MLSTM_WIDTH, MLSTM_WIDTH, MLSTM_WIDTH,
          MLSTM_HEADS, MLSTM_HEADS]
SPLIT_IDX = [int(i) for i in np.cumsum(SPLITS)[:-1]]
IN_WIDTH = int(sum(SPLITS))

kernel_name = "hymba_swa_sink_mlstm_step"


def rmsnorm(x, g):
    xf = x.astype(jnp.float32)
    y = xf * lax.rsqrt(jnp.mean(xf * xf, axis=-1, keepdims=True) + NORM_EPS)
    return (y * g.astype(jnp.float32)).astype(x.dtype)


def rope(x, pos):
    half = x.shape[-1] // 2
    inv = ROPE_THETA ** (-jnp.arange(half, dtype=jnp.float32) / half)
    ang = pos[:, None] * inv[None, :]
    cos = jnp.cos(ang)[:, None, :]
    sin = jnp.sin(ang)[:, None, :]
    xf = x.astype(jnp.float32)
    x1, x2 = xf[..., :half], xf[..., half:]
    return jnp.concatenate([x1 * cos - x2 * sin, x2 * cos + x1 * sin], axis=-1).astype(x.dtype)


def sink_attend(q, k, v, mask, sinks):
    s = jnp.einsum('...qkgd,...skd->...kgqs', q.astype(jnp.float32), k.astype(jnp.float32))
    s = s * (ATTN_HEAD_DIM ** -0.5)
    s = jnp.where(mask[..., None, None, :, :], s, -jnp.inf)
    sink = jnp.broadcast_to(sinks.astype(jnp.float32)[:, :, None, None], s.shape[:-1] + (1,))
    p = jax.nn.softmax(jnp.concatenate([s, sink], axis=-1), axis=-1)[..., :-1]
    o = jnp.einsum('...kgqs,...skd->...qkgd', p, v.astype(jnp.float32))
    return o.astype(q.dtype)


def attn_prompt(q, k, v, sinks):
    B, T = q.shape[0], q.shape[1]
    NB = T // ATTN_BLOCK
    qb = q.reshape(B, NB, ATTN_BLOCK, KV_HEADS, ATTN_GROUP, ATTN_HEAD_DIM)
    kb = k.reshape(B, NB, ATTN_BLOCK, KV_HEADS, ATTN_HEAD_DIM)
    vb = v.reshape(B, NB, ATTN_BLOCK, KV_HEADS, ATTN_HEAD_DIM)
    pad = ((0, 0), (1, 0), (0, 0), (0, 0), (0, 0))
    kk = jnp.concatenate([jnp.pad(kb, pad)[:, :-1], kb], axis=2)
    vv = jnp.concatenate([jnp.pad(vb, pad)[:, :-1], vb], axis=2)
    blk = jnp.arange(NB)[:, None] * ATTN_BLOCK
    qpos = blk + jnp.arange(ATTN_BLOCK)[None, :]
    kpos = blk - ATTN_BLOCK + jnp.arange(2 * ATTN_BLOCK)[None, :]
    diff = qpos[:, :, None] - kpos[:, None, :]
    mask = (diff >= 0) & (diff < WINDOW) & (kpos[:, None, :] >= 0)
    o = sink_attend(qb, kk, vv, mask, sinks.reshape(KV_HEADS, ATTN_GROUP))
    return o.reshape(B, T, ATTN_WIDTH)


def attn_sample(q, k, v, cache_k, cache_v, sinks):
    DB, T = q.shape[0], q.shape[1]
    W = cache_k.shape[1]
    kk = jnp.concatenate([cache_k.astype(k.dtype), k], axis=1)
    vv = jnp.concatenate([cache_v.astype(v.dtype), v], axis=1)
    qpos = PAST_LEN + jnp.arange(T)
    kpos = jnp.concatenate([PAST_LEN - W + jnp.arange(W), PAST_LEN + jnp.arange(T)])
    diff = qpos[:, None] - kpos[None, :]
    mask = (diff >= 0) & (diff < WINDOW) & (kpos[None, :] >= 0)
    qb = q.reshape(DB, T, KV_HEADS, ATTN_GROUP, ATTN_HEAD_DIM)
    o = sink_attend(qb, kk, vv, mask, sinks.reshape(KV_HEADS, ATTN_GROUP))
    return o.reshape(DB, T, ATTN_WIDTH), kk[:, -W:], vv[:, -W:]


def mlstm_chunkwise(q, k, v, ig, lf, C0, n0, m0):
    B, T, H, D = q.shape
    L = math.gcd(T, MLSTM_CHUNK)
    NC = T // L
    f32 = jnp.float32
    to_c = lambda a: a.astype(f32).reshape(B, NC, L, H, D).transpose(1, 0, 3, 2, 4)
    to_g = lambda a: a.astype(f32).reshape(B, NC, L, H).transpose(1, 0, 3, 2)
    causal = jnp.tril(jnp.ones((L, L), dtype=bool))

    def step(carry, xs):
        C, n, m = carry
        qc, kc, vc, igc, lfc = xs
        b = jnp.cumsum(lfc, axis=-1)
        dlog = jnp.where(causal, b[..., :, None] - b[..., None, :] + igc[..., None, :], -jnp.inf)
        inter = b + m[..., None]
        m_t = jnp.maximum(inter, jnp.max(dlog, axis=-1))
        s = jnp.einsum('bhld,bhsd->bhls', qc, kc) * jnp.exp(dlog - m_t[..., None])
        a = jnp.exp(inter - m_t)
        num = a[..., None] * jnp.einsum('bhld,bhde->bhle', qc, C) + jnp.einsum('bhls,bhse->bhle', s, vc)
        den = a * jnp.einsum('bhld,bhd->bhl', qc, n) + jnp.sum(s, axis=-1)
        h = num / jnp.maximum(jnp.abs(den), jnp.exp(-m_t))[..., None]
        m_new = m_t[..., -1]
        a_end = jnp.exp(b[..., -1] + m - m_new)
        w = jnp.exp(b[..., -1:] - b + igc - m_new[..., None])
        C_new = a_end[..., None, None] * C + jnp.einsum('bhs,bhsd,bhse->bhde', w, kc, vc)
        n_new = a_end[..., None] * n + jnp.einsum('bhs,bhsd->bhd', w, kc)
        return (C_new, n_new, m_new), h

    carry0 = (C0.astype(f32), n0.astype(f32), m0.astype(f32))
    (C, n, m), hs = lax.scan(step, carry0, (to_c(q), to_c(k), to_c(v), to_g(ig), to_g(lf)))
    h = hs.transpose(1, 0, 3, 2, 4).reshape(B, T, H, D)
    return h, C, n, m


def layer(x, pos, cache_k, cache_v, C0, n0, m0,
          attn_norm, w_in, q_norm, k_norm, attn_sinks, b_ig, b_fg, mlstm_norm,
          w_out, ffn_norm, w_gate, w_up, w_down):
    B, T, _ = x.shape
    h = rmsnorm(x, attn_norm)
    z = h @ w_in
    q_a, k_a, v_a, q_m, k_m, v_m, o_m, ig, fg = jnp.split(z, SPLIT_IDX, axis=-1)
    q_a = rope(rmsnorm(q_a.reshape(B, T, ATTN_HEADS, ATTN_HEAD_DIM), q_norm), pos)
    k_a = rope(rmsnorm(k_a.reshape(B, T, KV_HEADS, ATTN_HEAD_DIM), k_norm), pos)
    v_a = v_a.reshape(B, T, KV_HEADS, ATTN_HEAD_DIM)
    if cache_k is None:
        attn_o = attn_prompt(q_a, k_a, v_a, attn_sinks)
        wk = min(WINDOW, T)
        new_k, new_v = k_a[:, -wk:], v_a[:, -wk:]
    else:
        attn_o, new_k, new_v = attn_sample(q_a, k_a, v_a, cache_k, cache_v, attn_sinks)
    q_m = q_m.reshape(B, T, MLSTM_HEADS, MLSTM_HEAD_DIM)
    k_m = k_m.reshape(B, T, MLSTM_HEADS, MLSTM_HEAD_DIM) * (MLSTM_HEAD_DIM ** -0.5)
    v_m = v_m.reshape(B, T, MLSTM_HEADS, MLSTM_HEAD_DIM)
    ig = ig.astype(jnp.float32) + b_ig.astype(jnp.float32)
    lf = jax.nn.log_sigmoid(fg.astype(jnp.float32) + b_fg.astype(jnp.float32))
    h_m, C, n, m = mlstm_chunkwise(q_m, k_m, v_m, ig, lf, C0, n0, m0)
    h_m = rmsnorm(h_m, mlstm_norm.reshape(MLSTM_HEADS, MLSTM_HEAD_DIM)).reshape(B, T, MLSTM_WIDTH)
    m_o = (h_m * jax.nn.sigmoid(o_m.astype(jnp.float32))).astype(x.dtype)
    x = x + jnp.concatenate([attn_o.astype(x.dtype), m_o], axis=-1) @ w_out
    hf = rmsnorm(x, ffn_norm)
    x = x + (jax.nn.silu(hf @ w_gate) * (hf @ w_up)) @ w_down
    return x, new_k, new_v, C, n, m


def setup_inputs(seed: int = 0) -> dict:
    key = jax.random.key(seed)
    ks = jax.random.split(key, 24)
    nrm = lambda k, s, sc: jax.random.normal(k, s, dtype=jnp.float32) * sc
    gain = lambda k, s: 1.0 + 0.01 * jax.random.normal(k, s, dtype=jnp.float32)
    return {
        "x_prompt": nrm(ks[0], (BATCH, SEQ, D_MODEL), 1.0),
        "x_sample": nrm(ks[1], (DEC_BATCH, DEC_SEQ, D_MODEL), 1.0),
        "cache_k": nrm(ks[2], (DEPTH, DEC_BATCH, W_BUF, KV_HEADS, ATTN_HEAD_DIM), 1.0),
        "cache_v": nrm(ks[3], (DEPTH, DEC_BATCH, W_BUF, KV_HEADS, ATTN_HEAD_DIM), 1.0),
        "state_C": nrm(ks[4], (DEPTH, DEC_BATCH, MLSTM_HEADS, MLSTM_HEAD_DIM, MLSTM_HEAD_DIM), 0.05),
        "state_n": nrm(ks[5], (DEPTH, DEC_BATCH, MLSTM_HEADS, MLSTM_HEAD_DIM), 0.1),
        "state_m": nrm(ks[6], (DEPTH, DEC_BATCH, MLSTM_HEADS), 1.0),
        "attn_norm": gain(ks[7], (DEPTH, D_MODEL)),
        "w_in": nrm(ks[8], (DEPTH, D_MODEL, IN_WIDTH), D_MODEL ** -0.5),
        "q_norm": gain(ks[9], (DEPTH, ATTN_HEAD_DIM)),
        "k_norm": gain(ks[10], (DEPTH, ATTN_HEAD_DIM)),
        "attn_sinks": nrm(ks[11], (DEPTH, ATTN_HEADS), 0.5),
        "b_ig": nrm(ks[12], (DEPTH, MLSTM_HEADS), 0.1),
        "b_fg": jnp.linspace(3.0, 6.0, MLSTM_HEADS, dtype=jnp.float32)[None, :] + nrm(ks[13], (DEPTH, MLSTM_HEADS), 0.1),
        "mlstm_norm": gain(ks[14], (DEPTH, MLSTM_WIDTH)),
        "w_out": nrm(ks[15], (DEPTH, MIX_WIDTH, D_MODEL), MIX_WIDTH ** -0.5),
        "ffn_norm": gain(ks[16], (DEPTH, D_MODEL)),
        "w_gate": nrm(ks[17], (DEPTH, D_MODEL, D_FF), D_MODEL ** -0.5),
        "w_up": nrm(ks[18], (DEPTH, D_MODEL, D_FF), D_MODEL ** -0.5),
        "w_down": nrm(ks[19], (DEPTH, D_FF, D_MODEL), D_FF ** -0.5),
    }


def reference(x_prompt, x_sample, cache_k, cache_v, state_C, state_n, state_m,
              attn_norm, w_in, q_norm, k_norm, attn_sinks, b_ig, b_fg, mlstm_norm,
              w_out, ffn_norm, w_gate, w_up, w_down):
    Bp, Tp = x_prompt.shape[0], x_prompt.shape[1]
    Ts = x_sample.shape[1]
    pos_p = jnp.arange(Tp, dtype=jnp.float32)
    pos_s = jnp.arange(Ts, dtype=jnp.float32) + PAST_LEN
    zC = jnp.zeros((Bp, MLSTM_HEADS, MLSTM_HEAD_DIM, MLSTM_HEAD_DIM), jnp.float32)
    zn = jnp.zeros((Bp, MLSTM_HEADS, MLSTM_HEAD_DIM), jnp.float32)
    zm = jnp.zeros((Bp, MLSTM_HEADS), jnp.float32)
    yp, ys = x_prompt, x_sample
    kp, vp, Cp, np_, mp = [], [], [], [], []
    ks, vs, Cs, ns, ms = [], [], [], [], []
    for l in range(DEPTH):
        w = (attn_norm[l], w_in[l], q_norm[l], k_norm[l], attn_sinks[l], b_ig[l], b_fg[l],
             mlstm_norm[l], w_out[l], ffn_norm[l], w_gate[l], w_up[l], w_down[l])
        yp, a, b, c, d, e = layer(yp, pos_p, None, None, zC, zn, zm, *w)
        kp.append(a); vp.append(b); Cp.append(c); np_.append(d); mp.append(e)
        ys, a, b, c, d, e = layer(ys, pos_s, cache_k[l], cache_v[l], state_C[l], state_n[l], state_m[l], *w)
        ks.append(a); vs.append(b); Cs.append(c); ns.append(d); ms.append(e)
    return (yp, ys,
            jnp.stack(kp), jnp.stack(vp), jnp.stack(Cp), jnp.stack(np_), jnp.stack(mp),
            jnp.stack(ks), jnp.stack(vs), jnp.stack(Cs), jnp.stack(ns), jnp.stack(ms))
```

```python
import functools
import math

import jax
import jax.numpy as jnp
from jax import lax
from jax.experimental import pallas as pl
from jax.experimental.pallas import tpu as pltpu

D_MODEL = 1024
PAST_LEN = 16384
ATTN_HEADS = 8
KV_HEADS = 2
ATTN_HEAD_DIM = 64
ATTN_GROUP = ATTN_HEADS // KV_HEADS
ATTN_WIDTH = ATTN_HEADS * ATTN_HEAD_DIM
KV_WIDTH = KV_HEADS * ATTN_HEAD_DIM
WINDOW = 128
ROPE_THETA = 10000.0
MLSTM_HEADS = 4
MLSTM_HEAD_DIM = 128
MLSTM_WIDTH = MLSTM_HEADS * MLSTM_HEAD_DIM
MIX_WIDTH = ATTN_WIDTH + MLSTM_WIDTH
D_FF = 2816
NORM_EPS = 1e-6

LANES = 128
SUBLANES = 8
VMEM_LIMIT = 56 * 1024 * 1024

COL_QA = 0
COL_KA = COL_QA + ATTN_WIDTH
COL_VA = COL_KA + KV_WIDTH
COL_QM = COL_VA + KV_WIDTH
COL_KM = COL_QM + MLSTM_WIDTH
COL_VM = COL_KM + MLSTM_WIDTH
COL_OM = COL_VM + MLSTM_WIDTH
COL_G = COL_OM + MLSTM_WIDTH
IN_PAD = COL_G + LANES
FG_LANE = SUBLANES

PROMPT_BLOCK = 256
QBLOCK = WINDOW
MCHUNK = 128
FFN_BLOCK = 256
FFN_CHUNK = 256
SAMPLE_BATCH_BLOCK = 16
NEG = -1e30

f32 = jnp.float32
bf16 = jnp.bfloat16


def _rms(x, gain):
    return x * lax.rsqrt(jnp.mean(x * x, axis=-1, keepdims=True) + NORM_EPS) * gain


def _segsum64(s, lane):
    for k in (1, 2, 4, 8, 16, 32):
        s = s + jnp.where((lane & k) != 0, pltpu.roll(s, k, 1), pltpu.roll(s, LANES - k, 1))
    return s


def _headnorm_rope(xs, gain, cos, sin_signed, lane):
    ss = _segsum64(xs * xs, lane)
    y = xs * lax.rsqrt(ss * (1.0 / ATTN_HEAD_DIM) + NORM_EPS) * gain
    partner = jnp.where((lane & 32) != 0, pltpu.roll(y, 32, 1), pltpu.roll(y, LANES - 32, 1))
    return y * cos + partner * sin_signed


def _in_proj(x, anorm, win_ref):
    h = _rms(x, anorm).astype(bf16)
    return jnp.dot(h, win_ref[...], preferred_element_type=f32)


def _prompt_mixer_kernel(sinks_ref, x_ref, cos_ref, sin_ref, win_ref, wout_ref, anorm_ref,
                         qg_ref, kg_ref, gbias_ref, mnorm_ref,
                         x1_ref, kout_ref, vout_ref, cext_ref, mout_ref,
                         z_ref, mix_ref, kprev_ref, vprev_ref, cst_ref, mst_ref):
    step = pl.program_id(0)
    tb = x_ref.shape[0]

    @pl.when(step == 0)
    def _init():
        kprev_ref[...] = jnp.zeros_like(kprev_ref)
        vprev_ref[...] = jnp.zeros_like(vprev_ref)
        cst_ref[...] = jnp.zeros_like(cst_ref)
        mst_ref[...] = jnp.zeros_like(mst_ref)

    x = x_ref[...]
    z_ref[...] = _in_proj(x, anorm_ref[...], win_ref)

    lane = lax.broadcasted_iota(jnp.int32, (QBLOCK, LANES), 1)
    qi = lax.broadcasted_iota(jnp.int32, (ATTN_GROUP * QBLOCK, 2 * QBLOCK), 0) & (QBLOCK - 1)
    kj = lax.broadcasted_iota(jnp.int32, (ATTN_GROUP * QBLOCK, 2 * QBLOCK), 1)
    band = (kj > qi) & (kj <= qi + QBLOCK)
    for qb in range(tb // QBLOCK):
        r0 = qb * QBLOCK
        rows = slice(r0, r0 + QBLOCK)
        cos = cos_ref[rows, :]
        sin = sin_ref[rows, :]
        q_slabs = []
        for j in range(ATTN_WIDTH // LANES):
            qs = _headnorm_rope(z_ref[rows, COL_QA + j * LANES:COL_QA + (j + 1) * LANES],
                                qg_ref[...], cos, sin, lane)
            q_slabs.append((qs * (ATTN_HEAD_DIM ** -0.5)).astype(bf16))
        kcur = _headnorm_rope(z_ref[rows, COL_KA:COL_KA + KV_WIDTH], kg_ref[...], cos, sin, lane)
        vcur = z_ref[rows, COL_VA:COL_VA + KV_WIDTH]
        kout_ref[...] = kcur
        vout_ref[...] = vcur
        kcur_b = kcur.astype(bf16)
        vcur_b = vcur.astype(bf16)
        kprev = kprev_ref[...]
        vprev = vprev_ref[...]
        kmin = jnp.where(step * tb + r0 == 0, QBLOCK, 0)
        valid = band & (kj >= kmin)
        for c in range(KV_HEADS):
            hs = slice(c * ATTN_HEAD_DIM, (c + 1) * ATTN_HEAD_DIM)
            kcat = jnp.concatenate([kprev[:, hs], kcur_b[:, hs]], axis=0)
            vcat = jnp.concatenate([vprev[:, hs], vcur_b[:, hs]], axis=0)
            q_heads = []
            sink_rows = []
            for g in range(ATTN_GROUP):
                head = c * ATTN_GROUP + g
                slab = q_slabs[head // 2]
                off = (head % 2) * ATTN_HEAD_DIM
                q_heads.append(slab[:, off:off + ATTN_HEAD_DIM])
                sink_rows.append(jnp.full((QBLOCK, 1), sinks_ref[head], f32))
            qst = jnp.concatenate(q_heads, axis=0)
            sink = jnp.concatenate(sink_rows, axis=0)
            s = lax.dot_general(qst, kcat, (((1,), (1,)), ((), ())), preferred_element_type=f32)
            s = jnp.where(valid, s, NEG)
            mx = jnp.maximum(jnp.max(s, axis=-1, keepdims=True), sink)
            p = jnp.exp(s - mx)
            den = jnp.sum(p, axis=-1, keepdims=True) + jnp.exp(sink - mx)
            o = jnp.dot(p.astype(bf16), vcat, preferred_element_type=f32) / den
            for g in range(ATTN_GROUP):
                head = c * ATTN_GROUP + g
                mix_ref[rows, head * ATTN_HEAD_DIM:(head + 1) * ATTN_HEAD_DIM] = (
                    o[g * QBLOCK:(g + 1) * QBLOCK, :].astype(bf16))
        kprev_ref[...] = kcur_b
        vprev_ref[...] = vcur_b

    lane_t = lax.broadcasted_iota(jnp.int32, (tb, LANES), 1)
    gcol = z_ref[:, COL_G:COL_G + LANES] + gbias_ref[...]
    acol = jnp.where(lane_t < FG_LANE, gcol, jax.nn.log_sigmoid(gcol))
    arow = acol.T
    lane8 = lax.broadcasted_iota(jnp.int32, (SUBLANES, LANES), 1)
    lane_in = lane8 & (MCHUNK - 1)
    m_prev = mst_ref[:, 0:1]
    stacks = []
    u_rows = []
    for sb in range(tb // LANES):
        ls = slice(sb * LANES, (sb + 1) * LANES)
        ig8 = arow[0:SUBLANES, ls]
        lf8 = arow[FG_LANE:FG_LANE + SUBLANES, ls]
        b8 = lf8
        k = 1
        while k < MCHUNK:
            b8 = b8 + jnp.where(lane_in >= k, pltpu.roll(b8, k, 1), 0.0)
            k *= 2
        u8 = ig8 - b8
        cm8 = u8
        k = 1
        while k < MCHUNK:
            cm8 = jnp.maximum(cm8, jnp.where(lane_in >= k, pltpu.roll(cm8, k, 1), NEG))
            k *= 2
        g8 = jnp.zeros_like(u8)
        mp8 = jnp.zeros_like(u8)
        gl8 = jnp.zeros_like(u8)
        for c in range(LANES // MCHUNK):
            in_chunk = (lane8 // MCHUNK) == c
            gc = jnp.maximum(cm8, m_prev)
            last = c * MCHUNK + MCHUNK - 1
            g_last = jnp.max(jnp.where(lane8 == last, gc, NEG), axis=1, keepdims=True)
            b_last = jnp.max(jnp.where(lane8 == last, b8, NEG), axis=1, keepdims=True)
            g8 = jnp.where(in_chunk, gc, g8)
            mp8 = jnp.where(in_chunk, m_prev, mp8)
            gl8 = jnp.where(in_chunk, g_last, gl8)
            m_prev = b_last + g_last
        a8 = jnp.exp(mp8 - g8)
        emt8 = jnp.exp(-(b8 + g8))
        w8 = jnp.exp(u8 - gl8)
        aend8 = jnp.exp(mp8 - gl8)
        stacks.append(jnp.concatenate(
            [g8, a8, emt8, w8, aend8, jnp.zeros((LANES - 5 * SUBLANES, LANES), f32)], axis=0))
        u_rows.append(u8)
    mst_ref[...] = jnp.broadcast_to(m_prev, mst_ref.shape)
    mout_ref[...] = jnp.broadcast_to(m_prev, mout_ref.shape)
    colform = jnp.concatenate(stacks, axis=1).T

    ti = lax.broadcasted_iota(jnp.int32, (MCHUNK, MCHUNK), 0)
    si = lax.broadcasted_iota(jnp.int32, (MCHUNK, MCHUNK), 1)
    causal = si <= ti
    one_hot0 = (lax.broadcasted_iota(jnp.int32, (MCHUNK, LANES), 1) == 0).astype(bf16)
    for hd in range(MLSTM_HEADS):
        hcols = lambda base: slice(base + hd * MLSTM_HEAD_DIM, base + (hd + 1) * MLSTM_HEAD_DIM)
        cext = cst_ref[hd]
        for c in range(tb // MCHUNK):
            rows = slice(c * MCHUNK, (c + 1) * MCHUNK)
            qb_ = z_ref[rows, hcols(COL_QM)].astype(bf16)
            kf = z_ref[rows, hcols(COL_KM)] * (MLSTM_HEAD_DIM ** -0.5)
            vb = z_ref[rows, hcols(COL_VM)].astype(bf16)
            og = z_ref[rows, hcols(COL_OM)]
            g_c = colform[rows, hd:hd + 1]
            a_c = colform[rows, SUBLANES + hd:SUBLANES + hd + 1]
            emt_c = colform[rows, 2 * SUBLANES + hd:2 * SUBLANES + hd + 1]
            w_c = colform[rows, 3 * SUBLANES + hd:3 * SUBLANES + hd + 1]
            aend = colform[c * MCHUNK:c * MCHUNK + 1, 4 * SUBLANES + hd:4 * SUBLANES + hd + 1]
            sb, off = divmod(c * MCHUNK, LANES)
            u_r = u_rows[sb][hd:hd + 1, off:off + MCHUNK]
            dmat = jnp.exp(jnp.where(causal, u_r - g_c, NEG))
            smat = lax.dot_general(qb_, kf.astype(bf16), (((1,), (1,)), ((), ())),
                                   preferred_element_type=f32) * dmat
            rowsum = jnp.sum(smat, axis=-1, keepdims=True)
            qc = jnp.dot(qb_, cext.astype(bf16), preferred_element_type=f32)
            sv = jnp.dot(smat.astype(bf16), vb, preferred_element_type=f32)
            num = a_c * qc[:, :MLSTM_HEAD_DIM] + sv
            den = a_c * qc[:, MLSTM_HEAD_DIM:MLSTM_HEAD_DIM + 1] + rowsum
            hraw = num / jnp.maximum(jnp.abs(den), emt_c)
            hn = _rms(hraw, mnorm_ref[hd:hd + 1, :])
            mix_ref[rows, ATTN_WIDTH + hd * MLSTM_HEAD_DIM:ATTN_WIDTH + (hd + 1) * MLSTM_HEAD_DIM] = (
                (hn * jax.nn.sigmoid(og)).astype(bf16))
            kw_t = (kf * w_c).T.astype(bf16)
            vext = jnp.concatenate([vb, one_hot0], axis=1)
            cext = aend * cext + jnp.dot(kw_t, vext, preferred_element_type=f32)
        cst_ref[hd] = cext
        cext_ref[hd] = cext

    x1_ref[...] = x + jnp.dot(mix_ref[...], wout_ref[...], preferred_element_type=f32)


def _const_spec(shape):
    nd = len(shape)
    return pl.BlockSpec(shape, lambda i, *_: (0,) * nd)


def _prompt_mixer(x, cos, sin, win_p, wout_b, anorm, qg, kg, sinks, gbias, mnorm):
    t = x.shape[0]
    tb = PROMPT_BLOCK
    grid_spec = pltpu.PrefetchScalarGridSpec(
        num_scalar_prefetch=1,
        grid=(t // tb,),
        in_specs=[
            pl.BlockSpec((tb, D_MODEL), lambda i, *_: (i, 0)),
            pl.BlockSpec((tb, LANES), lambda i, *_: (i, 0)),
            pl.BlockSpec((tb, LANES), lambda i, *_: (i, 0)),
            _const_spec((D_MODEL, IN_PAD)),
            _const_spec((MIX_WIDTH, D_MODEL)),
            _const_spec((1, D_MODEL)),
            _const_spec((1, LANES)),
            _const_spec((1, LANES)),
            _const_spec((1, LANES)),
            _const_spec((MLSTM_HEADS, MLSTM_HEAD_DIM)),
        ],
        out_specs=[
            pl.BlockSpec((tb, D_MODEL), lambda i, *_: (i, 0)),
            _const_spec((WINDOW, KV_WIDTH)),
            _const_spec((WINDOW, KV_WIDTH)),
            _const_spec((MLSTM_HEADS, MLSTM_HEAD_DIM, 2 * MLSTM_HEAD_DIM)),
            _const_spec((SUBLANES, LANES)),
        ],
        scratch_shapes=[
            pltpu.VMEM((tb, IN_PAD), f32),
            pltpu.VMEM((tb, MIX_WIDTH), bf16),
            pltpu.VMEM((WINDOW, KV_WIDTH), bf16),
            pltpu.VMEM((WINDOW, KV_WIDTH), bf16),
            pltpu.VMEM((MLSTM_HEADS, MLSTM_HEAD_DIM, 2 * MLSTM_HEAD_DIM), f32),
            pltpu.VMEM((SUBLANES, LANES), f32),
        ],
    )
    return pl.pallas_call(
        _prompt_mixer_kernel,
        grid_spec=grid_spec,
        out_shape=[
            jax.ShapeDtypeStruct((t, D_MODEL), f32),
            jax.ShapeDtypeStruct((WINDOW, KV_WIDTH), f32),
            jax.ShapeDtypeStruct((WINDOW, KV_WIDTH), f32),
            jax.ShapeDtypeStruct((MLSTM_HEADS, MLSTM_HEAD_DIM, 2 * MLSTM_HEAD_DIM), f32),
            jax.ShapeDtypeStruct((SUBLANES, LANES), f32),
        ],
        compiler_params=pltpu.CompilerParams(
            dimension_semantics=("arbitrary",), vmem_limit_bytes=VMEM_LIMIT),
        name="prompt_mixer",
    )(sinks, x, cos, sin, win_p, wout_b, anorm, qg, kg, gbias, mnorm)


def _ffn_kernel(x_ref, g_ref, wg_ref, wu_ref, wd_ref, o_ref):
    x = x_ref[...]
    hf = _rms(x, g_ref[...]).astype(bf16)
    acc = x
    for c in range(D_FF // FFN_CHUNK):
        cs = slice(c * FFN_CHUNK, (c + 1) * FFN_CHUNK)
        gate = jnp.dot(hf, wg_ref[:, cs], preferred_element_type=f32)
        up = jnp.dot(hf, wu_ref[:, cs], preferred_element_type=f32)
        act = (gate * jax.nn.sigmoid(gate) * up).astype(bf16)
        acc = acc + jnp.dot(act, wd_ref[cs, :], preferred_element_type=f32)
    o_ref[...] = acc


def _ffn(x, fnorm, wg_b, wu_b, wd_b):
    n = x.shape[0]
    tm = FFN_BLOCK
    return pl.pallas_call(
        _ffn_kernel,
        grid=(n // tm,),
        in_specs=[
            pl.BlockSpec((tm, D_MODEL), lambda i: (i, 0)),
            _const_spec((1, D_MODEL)),
            _const_spec((D_MODEL, D_FF)),
            _const_spec((D_MODEL, D_FF)),
            _const_spec((D_FF, D_MODEL)),
        ],
        out_specs=pl.BlockSpec((tm, D_MODEL), lambda i: (i, 0)),
        out_shape=jax.ShapeDtypeStruct((n, D_MODEL), f32),
        compiler_params=pltpu.CompilerParams(
            dimension_semantics=("arbitrary",), vmem_limit_bytes=VMEM_LIMIT),
        name="ffn",
    )(x, fnorm, wg_b, wu_b, wd_b)


def _sample_proj_kernel(x_ref, cos_ref, sin_ref, win_ref, anorm_ref, qg_ref, kg_ref, gbias_ref,
                        q_ref, k_ref, v_ref, zm_ref, gate_ref):
    n = x_ref.shape[0]
    z = _in_proj(x_ref[...], anorm_ref[...], win_ref)
    lane = lax.broadcasted_iota(jnp.int32, (n, LANES), 1)
    cos = cos_ref[...]
    sin = sin_ref[...]
    for j in range(ATTN_WIDTH // LANES):
        qs = _headnorm_rope(z[:, COL_QA + j * LANES:COL_QA + (j + 1) * LANES], qg_ref[...], cos, sin, lane)
        q_ref[:, j * LANES:(j + 1) * LANES] = qs * (ATTN_HEAD_DIM ** -0.5)
    k_ref[...] = _headnorm_rope(z[:, COL_KA:COL_KA + KV_WIDTH], kg_ref[...], cos, sin, lane)
    v_ref[...] = z[:, COL_VA:COL_VA + KV_WIDTH]
    zm_ref[:, 0:MLSTM_WIDTH] = z[:, COL_QM:COL_KM]
    zm_ref[:, MLSTM_WIDTH:2 * MLSTM_WIDTH] = z[:, COL_KM:COL_VM] * (MLSTM_HEAD_DIM ** -0.5)
    zm_ref[:, 2 * MLSTM_WIDTH:4 * MLSTM_WIDTH] = z[:, COL_VM:COL_G]
    gcol = z[:, COL_G:COL_G + LANES] + gbias_ref[...]
    gate_ref[...] = jnp.where(lane < FG_LANE, gcol, jax.nn.log_sigmoid(gcol))


def _sample_proj(x, cos, sin, win_p, anorm, qg, kg, gbias):
    n = x.shape[0]
    full = lambda shape: pl.BlockSpec(shape, lambda i: (0,) * len(shape))
    return pl.pallas_call(
        _sample_proj_kernel,
        grid=(1,),
        in_specs=[full((n, D_MODEL)), full((n, LANES)), full((n, LANES)), full((D_MODEL, IN_PAD)),
                  full((1, D_MODEL)), full((1, LANES)), full((1, LANES)), full((1, LANES))],
        out_specs=[full((n, ATTN_WIDTH)), full((n, KV_WIDTH)), full((n, KV_WIDTH)),
                   full((n, 4 * MLSTM_WIDTH)), full((n, LANES))],
        out_shape=[jax.ShapeDtypeStruct((n, ATTN_WIDTH), f32),
                   jax.ShapeDtypeStruct((n, KV_WIDTH), f32),
                   jax.ShapeDtypeStruct((n, KV_WIDTH), f32),
                   jax.ShapeDtypeStruct((n, 4 * MLSTM_WIDTH), f32),
                   jax.ShapeDtypeStruct((n, LANES), f32)],
        compiler_params=pltpu.CompilerParams(vmem_limit_bytes=VMEM_LIMIT),
        name="sample_proj",
    )(x, cos, sin, win_p, anorm, qg, kg, gbias)


def _sample_core_kernel(qbd_ref, ck_ref, cv_ref, kn_ref, vn_ref, sink_ref,
                        qm_ref, km_ref, vm_ref, gp_ref, gr_ref, c_ref, n_ref, m_ref,
                        o_ref, h_ref, cn_ref, nn_ref, mn_ref):
    bb = qbd_ref.shape[0]
    nrow = qbd_ref.shape[1]
    tpad = kn_ref.shape[1]
    nreal = nrow // ATTN_HEADS
    qbd = qbd_ref[...].astype(bf16)
    zpad_k = jnp.zeros((bb, LANES - tpad, LANES), bf16)
    kk = jnp.concatenate([ck_ref[...].astype(bf16), kn_ref[...].astype(bf16), zpad_k], axis=1)
    vv = jnp.concatenate([cv_ref[...].astype(bf16), vn_ref[...].astype(bf16), zpad_k], axis=1)
    s = jnp.einsum('bqd,bkd->bqk', qbd, kk, preferred_element_type=f32)
    tq = lax.broadcasted_iota(jnp.int32, s.shape, 1) & (nreal - 1)
    kj = lax.broadcasted_iota(jnp.int32, s.shape, 2)
    valid = ((kj < WINDOW) & (kj > tq)) | ((kj >= WINDOW) & ((kj - WINDOW) <= tq))
    s = jnp.where(valid, s, NEG)
    sink = sink_ref[:, 0:1][None]
    mx = jnp.maximum(jnp.max(s, axis=-1, keepdims=True), sink)
    p = jnp.exp(s - mx)
    den = jnp.sum(p, axis=-1, keepdims=True) + jnp.exp(sink - mx)
    o = jnp.einsum('bqk,bkd->bqd', p.astype(bf16), vv, preferred_element_type=f32)
    o_ref[...] = o / den

    q = qm_ref[...]
    k = km_ref[...]
    v = vm_ref[...]
    gp = gp_ref[...]
    gr = gr_ref[...]
    c0 = c_ref[...]
    n0 = n_ref[...]
    m0 = m_ref[...]
    ig_c = gp[:, :, 0:1]
    lf_c = gp[:, :, 1:2]
    ig_r = gr[:, 0:1, :]
    lf_r = gr[:, 1:2, :]
    ti = lax.broadcasted_iota(jnp.int32, q.shape, 1)
    si = lax.broadcasted_iota(jnp.int32, q.shape, 2)
    tri = si <= ti
    b_c = jnp.sum(jnp.where(tri, lf_r, 0.0), axis=2, keepdims=True)
    b_r = jnp.sum(jnp.where(ti <= si, lf_c, 0.0), axis=1, keepdims=True)
    dlog = jnp.where(tri, b_c - b_r + ig_r, NEG)
    inter = b_c + m0
    m_t = jnp.maximum(inter, jnp.max(dlog, axis=2, keepdims=True))
    dmat = jnp.exp(dlog - m_t)
    smat = jnp.zeros_like(q)
    for sx in range(nreal):
        col = jnp.sum(q * k[:, sx:sx + 1, :], axis=2, keepdims=True)
        smat = jnp.where(si == sx, col, smat)
    smat = smat * dmat
    a = jnp.exp(inter - m_t)
    qc = jnp.einsum('gtd,gde->gte', q.astype(bf16), c0.astype(bf16), preferred_element_type=f32)
    intra = jnp.zeros_like(q)
    for sx in range(nreal):
        intra = intra + smat[:, :, sx:sx + 1] * v[:, sx:sx + 1, :]
    num = a * qc + intra
    qn = jnp.sum(q * n0, axis=2, keepdims=True)
    den_m = a * qn + jnp.sum(smat, axis=2, keepdims=True)
    h_ref[...] = num / jnp.maximum(jnp.abs(den_m), jnp.exp(-m_t))
    last = nreal - 1
    m_new = m_t[:, last:last + 1, :]
    b_last = b_c[:, last:last + 1, :]
    a_end = jnp.exp(b_last + m0 - m_new)
    w_c = jnp.exp(b_last - b_c + ig_c - m_new)
    kw = k * w_c
    upd = jnp.einsum('gsd,gse->gde', kw.astype(bf16), v.astype(bf16), preferred_element_type=f32)
    cn_ref[...] = a_end * c0 + upd
    nn_ref[...] = a_end * n0 + jnp.sum(kw, axis=1, keepdims=True)
    mn_ref[...] = m_new


def _sample_core(qbd, ck, cv, kn, vn, sink_rows, qm, km, vm, gp, gr, c0, n0, m0):
    nb = qbd.shape[0]
    bb = SAMPLE_BATCH_BLOCK
    gb = bb * MLSTM_HEADS
    ng = nb * MLSTM_HEADS
    nrow = qbd.shape[1]
    tpad = kn.shape[1]
    blk = lambda shape: pl.BlockSpec(shape, lambda i: (i,) + (0,) * (len(shape) - 1))
    return pl.pallas_call(
        _sample_core_kernel,
        grid=(nb // bb,),
        in_specs=[blk((bb, nrow, LANES)), blk((bb, WINDOW, KV_WIDTH)), blk((bb, WINDOW, KV_WIDTH)),
                  blk((bb, tpad, KV_WIDTH)), blk((bb, tpad, KV_WIDTH)),
                  pl.BlockSpec((nrow, LANES), lambda i: (0, 0)),
                  blk((gb, tpad, MLSTM_HEAD_DIM)), blk((gb, tpad, MLSTM_HEAD_DIM)),
                  blk((gb, tpad, MLSTM_HEAD_DIM)), blk((gb, tpad, LANES)), blk((gb, tpad, LANES)),
                  blk((gb, MLSTM_HEAD_DIM, MLSTM_HEAD_DIM)), blk((gb, 1, MLSTM_HEAD_DIM)),
                  blk((gb, 1, 1))],
        out_specs=[blk((bb, nrow, LANES)), blk((gb, tpad, MLSTM_HEAD_DIM)),
                   blk((gb, MLSTM_HEAD_DIM, MLSTM_HEAD_DIM)), blk((gb, 1, MLSTM_HEAD_DIM)),
                   blk((gb, 1, 1))],
        out_shape=[jax.ShapeDtypeStruct((nb, nrow, LANES), f32),
                   jax.ShapeDtypeStruct((ng, tpad, MLSTM_HEAD_DIM), f32),
                   jax.ShapeDtypeStruct((ng, MLSTM_HEAD_DIM, MLSTM_HEAD_DIM), f32),
                   jax.ShapeDtypeStruct((ng, 1, MLSTM_HEAD_DIM), f32),
                   jax.ShapeDtypeStruct((ng, 1, 1), f32)],
        compiler_params=pltpu.CompilerParams(
            dimension_semantics=("arbitrary",), vmem_limit_bytes=VMEM_LIMIT),
        name="sample_core",
    )(qbd, ck, cv, kn, vn, sink_rows, qm, km, vm, gp, gr, c0, n0, m0)


def _sample_out_kernel(x_ref, ao_ref, hm_ref, om_ref, mnorm_ref, wout_ref, x1_ref):
    parts = [ao_ref[...].astype(bf16)]
    for hd in range(MLSTM_HEADS):
        cs = slice(hd * MLSTM_HEAD_DIM, (hd + 1) * MLSTM_HEAD_DIM)
        hn = _rms(hm_ref[:, cs], mnorm_ref[hd:hd + 1, :])
        parts.append((hn * jax.nn.sigmoid(om_ref[:, cs])).astype(bf16))
    mix = jnp.concatenate(parts, axis=1)
    x1_ref[...] = x_ref[...] + jnp.dot(mix, wout_ref[...], preferred_element_type=f32)


def _sample_out(x, ao, hm, om, mnorm, wout_b):
    n = x.shape[0]
    full = lambda shape: pl.BlockSpec(shape, lambda i: (0,) * len(shape))
    return pl.pallas_call(
        _sample_out_kernel,
        grid=(1,),
        in_specs=[full((n, D_MODEL)), full((n, ATTN_WIDTH)), full((n, MLSTM_WIDTH)),
                  full((n, MLSTM_WIDTH)), full((MLSTM_HEADS, MLSTM_HEAD_DIM)),
                  full((MIX_WIDTH, D_MODEL))],
        out_specs=full((n, D_MODEL)),
        out_shape=jax.ShapeDtypeStruct((n, D_MODEL), f32),
        compiler_params=pltpu.CompilerParams(vmem_limit_bytes=VMEM_LIMIT),
        name="sample_out",
    )(x, ao, hm, om, mnorm, wout_b)


def _rope_tables(pos):
    half = ATTN_HEAD_DIM // 2
    inv = ROPE_THETA ** (-jnp.arange(half, dtype=f32) / half)
    ang = pos[:, None] * inv[None, :]
    c = jnp.cos(ang)
    s = jnp.sin(ang)
    cos = jnp.tile(c, (1, LANES // half))
    sin = jnp.tile(jnp.concatenate([-s, s], axis=1), (1, LANES // ATTN_HEAD_DIM))
    return cos, sin


def _to_head_major(a, nb, nt, nh, tpad, fill=0.0):
    d = a.shape[1] // nh
    a = a.reshape(nb, nt, nh, d).transpose(0, 2, 1, 3).reshape(nb * nh, nt, d)
    return jnp.pad(a, ((0, 0), (0, tpad - nt), (0, 0)), constant_values=fill)


def kernel(x_prompt, x_sample, cache_k, cache_v, state_C, state_n, state_m, attn_norm, w_in, q_norm,
           k_norm, attn_sinks, b_ig, b_fg, mlstm_norm, w_out, ffn_norm, w_gate, w_up, w_down):
    assert w_in.shape[0] == 1 and x_prompt.shape[0] == 1
    tp = x_prompt.shape[1]
    nb, nt = x_sample.shape[0], x_sample.shape[1]
    tpad = SUBLANES
    nh = MLSTM_HEADS

    w = w_in[0]
    gate_w = jnp.zeros((D_MODEL, LANES), f32)
    gate_w = gate_w.at[:, 0:nh].set(w[:, COL_G:COL_G + nh])
    gate_w = gate_w.at[:, FG_LANE:FG_LANE + nh].set(w[:, COL_G + nh:COL_G + 2 * nh])
    win_p = jnp.concatenate([w[:, :COL_G], gate_w], axis=1).astype(bf16)
    gbias = jnp.zeros((1, LANES), f32)
    gbias = gbias.at[0, 0:nh].set(b_ig[0]).at[0, FG_LANE:FG_LANE + nh].set(b_fg[0])
    wout_b = w_out[0].astype(bf16)
    wg_b = w_gate[0].astype(bf16)
    wu_b = w_up[0].astype(bf16)
    wd_b = w_down[0].astype(bf16)
    anorm = attn_norm[0].reshape(1, D_MODEL)
    fnorm = ffn_norm[0].reshape(1, D_MODEL)
    qg = jnp.tile(q_norm[0], LANES // ATTN_HEAD_DIM).reshape(1, LANES)
    kg = jnp.tile(k_norm[0], LANES // ATTN_HEAD_DIM).reshape(1, LANES)
    mnorm = mlstm_norm[0].reshape(nh, MLSTM_HEAD_DIM)
    sinks = attn_sinks[0]

    cos_p, sin_p = _rope_tables(jnp.arange(tp, dtype=f32))
    x1_p, k_p, v_p, cext_p, m_p = _prompt_mixer(
        x_prompt[0], cos_p, sin_p, win_p, wout_b, anorm, qg, kg, sinks, gbias, mnorm)
    y_p = _ffn(x1_p, fnorm, wg_b, wu_b, wd_b)

    xs = x_sample.reshape(nb * nt, D_MODEL)
    cos_s, sin_s = _rope_tables(jnp.arange(nt, dtype=f32) + PAST_LEN)
    cos_s = jnp.tile(cos_s, (nb, 1))
    sin_s = jnp.tile(sin_s, (nb, 1))
    q_s, k_s, v_s, zm_s, gate_s = _sample_proj(xs, cos_s, sin_s, win_p, anorm, qg, kg, gbias)

    qh = q_s.reshape(nb, nt, ATTN_HEADS, ATTN_HEAD_DIM).transpose(0, 2, 1, 3)
    half_rows = ATTN_GROUP * nt
    q0 = qh[:, :ATTN_GROUP].reshape(nb, half_rows, ATTN_HEAD_DIM)
    q1 = qh[:, ATTN_GROUP:].reshape(nb, half_rows, ATTN_HEAD_DIM)
    zq = jnp.zeros_like(q0)
    qbd = jnp.concatenate([jnp.concatenate([q0, zq], axis=2), jnp.concatenate([zq, q1], axis=2)], axis=1)
    kn = jnp.pad(k_s.reshape(nb, nt, KV_WIDTH), ((0, 0), (0, tpad - nt), (0, 0)))
    vn = jnp.pad(v_s.reshape(nb, nt, KV_WIDTH), ((0, 0), (0, tpad - nt), (0, 0)))
    sink_rows = jnp.broadcast_to(jnp.repeat(sinks, nt)[:, None], (ATTN_HEADS * nt, LANES))
    qm = _to_head_major(zm_s[:, 0:MLSTM_WIDTH], nb, nt, nh, tpad)
    km = _to_head_major(zm_s[:, MLSTM_WIDTH:2 * MLSTM_WIDTH], nb, nt, nh, tpad)
    vm = _to_head_major(zm_s[:, 2 * MLSTM_WIDTH:3 * MLSTM_WIDTH], nb, nt, nh, tpad)
    ig = gate_s[:, 0:nh].reshape(nb, nt, nh).transpose(0, 2, 1).reshape(nb * nh, nt)
    lf = gate_s[:, FG_LANE:FG_LANE + nh].reshape(nb, nt, nh).transpose(0, 2, 1).reshape(nb * nh, nt)
    ig_cp = jnp.pad(ig, ((0, 0), (0, tpad - nt)), constant_values=NEG)
    lf_cp = jnp.pad(lf, ((0, 0), (0, tpad - nt)))
    gp = jnp.pad(jnp.stack([ig_cp, lf_cp], axis=2), ((0, 0), (0, 0), (0, LANES - 2)))
    ig_rp = jnp.pad(ig, ((0, 0), (0, LANES - nt)), constant_values=NEG)
    lf_rp = jnp.pad(lf, ((0, 0), (0, LANES - nt)))
    gr = jnp.pad(jnp.stack([ig_rp, lf_rp], axis=1), ((0, 0), (0, tpad - 2), (0, 0)))
    ck = cache_k[0].reshape(nb, WINDOW, KV_WIDTH)
    cv = cache_v[0].reshape(nb, WINDOW, KV_WIDTH)
    c0 = state_C[0].reshape(nb * nh, MLSTM_HEAD_DIM, MLSTM_HEAD_DIM)
    n0 = state_n[0].reshape(nb * nh, 1, MLSTM_HEAD_DIM)
    m0 = state_m[0].reshape(nb * nh, 1, 1)

    o_bd, h_s, c_new, n_new, m_new = _sample_core(qbd, ck, cv, kn, vn, sink_rows, qm, km, vm, gp, gr,
                                                  c0, n0, m0)

    o0 = o_bd[:, :half_rows, :ATTN_HEAD_DIM].reshape(nb, ATTN_GROUP, nt, ATTN_HEAD_DIM)
    o1 = o_bd[:, half_rows:, ATTN_HEAD_DIM:].reshape(nb, ATTN_GROUP, nt, ATTN_HEAD_DIM)
    ao = jnp.concatenate([o0, o1], axis=1).transpose(0, 2, 1, 3).reshape(nb * nt, ATTN_WIDTH)
    hm = h_s[:, :nt].reshape(nb, nh, nt, MLSTM_HEAD_DIM).transpose(0, 2, 1, 3).reshape(nb * nt, MLSTM_WIDTH)
    om = zm_s[:, 3 * MLSTM_WIDTH:]
    x1_s = _sample_out(xs, ao, hm, om, mnorm, wout_b)
    y_s = _ffn(x1_s, fnorm, wg_b, wu_b, wd_b)

    new_k_s = jnp.concatenate([ck[:, nt:], k_s.reshape(nb, nt, KV_WIDTH)], axis=1)
    new_v_s = jnp.concatenate([cv[:, nt:], v_s.reshape(nb, nt, KV_WIDTH)], axis=1)

    kv_shape = (1, 1, WINDOW, KV_HEADS, ATTN_HEAD_DIM)
    return (
        y_p[None],
        y_s.reshape(nb, nt, D_MODEL),
        k_p.reshape(kv_shape),
        v_p.reshape(kv_shape),
        cext_p[None, None, :, :, :MLSTM_HEAD_DIM],
        cext_p[None, None, :, :, MLSTM_HEAD_DIM],
        m_p[None, None, :nh, 0],
        new_k_s.reshape(1, nb, WINDOW, KV_HEADS, ATTN_HEAD_DIM),
        new_v_s.reshape(1, nb, WINDOW, KV_HEADS, ATTN_HEAD_DIM),
        c_new.reshape(1, nb, nh, MLSTM_HEAD_DIM, MLSTM_HEAD_DIM),
        n_new.reshape(1, nb, nh, MLSTM_HEAD_DIM),
        m_new.reshape(1, nb, nh),
    )
```

```python
import jax
import jax.numpy as jnp
from jax import lax
from jax.experimental import pallas as pl
from jax.experimental.pallas import tpu as pltpu

D_MODEL = 1024
PAST_LEN = 16384
ATTN_HEADS = 8
KV_HEADS = 2
ATTN_HEAD_DIM = 64
ATTN_GROUP = ATTN_HEADS // KV_HEADS
ATTN_WIDTH = ATTN_HEADS * ATTN_HEAD_DIM
KV_WIDTH = KV_HEADS * ATTN_HEAD_DIM
WINDOW = 128
ROPE_THETA = 10000.0
MLSTM_HEADS = 4
MLSTM_HEAD_DIM = 128
MLSTM_WIDTH = MLSTM_HEADS * MLSTM_HEAD_DIM
MIX_WIDTH = ATTN_WIDTH + MLSTM_WIDTH
D_FF = 2816
NORM_EPS = 1e-6

LANES = 128
SUBLANES = 8
VMEM_LIMIT = 56 * 1024 * 1024

COL_QA = 0
COL_KA = COL_QA + ATTN_WIDTH
COL_VA = COL_KA + KV_WIDTH
COL_QM = COL_VA + KV_WIDTH
COL_KM = COL_QM + MLSTM_WIDTH
COL_VM = COL_KM + MLSTM_WIDTH
COL_OM = COL_VM + MLSTM_WIDTH
COL_G = COL_OM + MLSTM_WIDTH
IN_PAD = COL_G + LANES
FG_LANE = SUBLANES

PROMPT_BLOCK = 256
QBLOCK = WINDOW
MCHUNK = 128
FFN_BLOCK = 512
FFN_CHUNK = 256
SAMPLE_BATCH_BLOCK = 16
NEG = -1e30

f32 = jnp.float32
bf16 = jnp.bfloat16


def _rms(x, gain):
    return x * lax.rsqrt(jnp.mean(x * x, axis=-1, keepdims=True) + NORM_EPS) * gain


def _segsum64(s, lane):
    for k in (1, 2, 4, 8, 16, 32):
        s = s + jnp.where((lane & k) != 0, pltpu.roll(s, k, 1), pltpu.roll(s, LANES - k, 1))
    return s


def _headnorm_rope(xs, gain, cos, sin_signed, lane):
    ss = _segsum64(xs * xs, lane)
    y = xs * lax.rsqrt(ss * (1.0 / ATTN_HEAD_DIM) + NORM_EPS) * gain
    partner = jnp.where((lane & 32) != 0, pltpu.roll(y, 32, 1), pltpu.roll(y, LANES - 32, 1))
    return y * cos + partner * sin_signed


def _in_proj(x, anorm, win_ref):
    h = _rms(x, anorm).astype(bf16)
    return jnp.dot(h, win_ref[...], preferred_element_type=f32)


PA_Q = 0
PA_K = PA_Q + ATTN_WIDTH
PA_V = PA_K + KV_HEADS * LANES
PA_WIDTH = PA_V + KV_HEADS * LANES
PM_Q = 0
PM_K = PM_Q + MLSTM_WIDTH
PM_V = PM_K + MLSTM_WIDTH
PM_O = PM_V + MLSTM_WIDTH
PM_G = PM_O + MLSTM_WIDTH
PM_WIDTH = PM_G + LANES
QUARTER = ATTN_HEAD_DIM // 2


def _group_sumsq(xs, bd_ref):
    x2 = xs * xs
    hi = x2.astype(bf16)
    lo = (x2 - hi.astype(f32)).astype(bf16)
    return (jnp.dot(hi, bd_ref[...], preferred_element_type=f32)
            + jnp.dot(lo, bd_ref[...], preferred_element_type=f32))


def _norm_rope_quarters(xs, ss, gain, cos, sin_signed):
    y = xs * lax.rsqrt(ss * (1.0 / ATTN_HEAD_DIM) + NORM_EPS) * gain
    return y * cos + pltpu.roll(y, LANES // 2, 1) * sin_signed


def _prompt_mixer_kernel(x_ref, cos_ref, sin_ref, wa_ref, wm_ref, ws_ref, wout_ref, bd_ref, bias_ref,
                         anorm_ref, qg_ref, kg_ref, kgs_ref, coss_ref, sins_ref, gbias_ref, mnorm_ref,
                         x1_ref, kout_ref, vout_ref, cext_ref, mout_ref,
                         za_ref, zm_ref, mix_ref, kprev_ref, vprev_ref, cst_ref, mst_ref):
    step = pl.program_id(0)
    nstep = pl.num_programs(0)
    tb = x_ref.shape[0]

    @pl.when(step == 0)
    def _init():
        kprev_ref[...] = jnp.zeros_like(kprev_ref)
        vprev_ref[...] = jnp.zeros_like(vprev_ref)
        cst_ref[...] = jnp.zeros_like(cst_ref)
        mst_ref[...] = jnp.zeros_like(mst_ref)

    x = x_ref[...]
    h = _rms(x, anorm_ref[...]).astype(bf16)
    za_ref[...] = jnp.dot(h, wa_ref[...], preferred_element_type=f32)
    zm_ref[...] = jnp.dot(h, wm_ref[...], preferred_element_type=f32)

    @pl.when(step == nstep - 1)
    def _window_out():
        zs = jnp.dot(h[tb - WINDOW:, :], ws_ref[...], preferred_element_type=f32)
        lane_s = lax.broadcasted_iota(jnp.int32, (WINDOW, LANES), 1)
        kout_ref[...] = _headnorm_rope(zs[:, :KV_WIDTH], kgs_ref[...], coss_ref[...], sins_ref[...], lane_s)
        vout_ref[...] = zs[:, KV_WIDTH:]

    lane_t = lax.broadcasted_iota(jnp.int32, (tb, LANES), 1)
    gcol = zm_ref[:, PM_G:PM_G + LANES] + gbias_ref[...]
    acol = jnp.where(lane_t < FG_LANE, gcol, jax.nn.log_sigmoid(gcol))
    arow = acol.T
    lane8 = lax.broadcasted_iota(jnp.int32, (SUBLANES, LANES), 1)
    lane_in = lane8 & (MCHUNK - 1)
    m_prev = mst_ref[:, 0:1]
    stacks = []
    u_rows = []
    w_rows = []
    for sb in range(tb // LANES):
        ls = slice(sb * LANES, (sb + 1) * LANES)
        ig8 = arow[0:SUBLANES, ls]
        lf8 = arow[FG_LANE:FG_LANE + SUBLANES, ls]
        b8 = lf8
        k = 1
        while k < MCHUNK:
            b8 = b8 + jnp.where(lane_in >= k, pltpu.roll(b8, k, 1), 0.0)
            k *= 2
        u8 = ig8 - b8
        cm8 = u8
        k = 1
        while k < MCHUNK:
            cm8 = jnp.maximum(cm8, jnp.where(lane_in >= k, pltpu.roll(cm8, k, 1), NEG))
            k *= 2
        g8 = jnp.zeros_like(u8)
        mp8 = jnp.zeros_like(u8)
        gl8 = jnp.zeros_like(u8)
        for c in range(LANES // MCHUNK):
            in_chunk = (lane8 // MCHUNK) == c
            gc = jnp.maximum(cm8, m_prev)
            last = c * MCHUNK + MCHUNK - 1
            g_last = jnp.max(jnp.where(lane8 == last, gc, NEG), axis=1, keepdims=True)
            b_last = jnp.max(jnp.where(lane8 == last, b8, NEG), axis=1, keepdims=True)
            g8 = jnp.where(in_chunk, gc, g8)
            mp8 = jnp.where(in_chunk, m_prev, mp8)
            gl8 = jnp.where(in_chunk, g_last, gl8)
            m_prev = b_last + g_last
        a8 = jnp.exp(mp8 - g8)
        emt8 = jnp.exp(-(b8 + g8))
        aend8 = jnp.exp(mp8 - gl8)
        stacks.append(jnp.concatenate(
            [g8, a8, emt8, aend8, jnp.zeros((LANES - 4 * SUBLANES, LANES), f32)], axis=0))
        u_rows.append(u8)
        w_rows.append(jnp.exp(u8 - gl8))
    mst_ref[...] = jnp.broadcast_to(m_prev, mst_ref.shape)
    mout_ref[...] = jnp.broadcast_to(m_prev, mout_ref.shape)
    colform = jnp.concatenate(stacks, axis=1).T

    lane = lax.broadcasted_iota(jnp.int32, (QBLOCK, LANES), 1)
    head_a = ((lane // QUARTER) & 1) == 0
    low_half = lane < ATTN_HEAD_DIM
    qi = lax.broadcasted_iota(jnp.int32, (ATTN_GROUP * QBLOCK, 2 * QBLOCK), 0) & (QBLOCK - 1)
    kj = lax.broadcasted_iota(jnp.int32, (ATTN_GROUP * QBLOCK, 2 * QBLOCK), 1)
    band = (kj > qi) & (kj <= qi + QBLOCK)
    row0 = lax.broadcasted_iota(jnp.int32, (QBLOCK, LANES), 0) == 0
    ones_slab = jnp.ones((2 * QBLOCK, LANES), bf16)
    for qb in range(tb // QBLOCK):
        r0 = qb * QBLOCK
        rows = slice(r0, r0 + QBLOCK)
        cos = cos_ref[rows, :]
        sin = sin_ref[rows, :]
        ss = [_group_sumsq(za_ref[rows, d * 2 * LANES:(d + 1) * 2 * LANES], bd_ref)
              for d in range(PA_V // (2 * LANES))]
        slabs = []
        for j in range(PA_V // LANES):
            xs = za_ref[rows, j * LANES:(j + 1) * LANES]
            gain = qg_ref[...] if j < ATTN_WIDTH // LANES else kg_ref[...]
            slabs.append(_norm_rope_quarters(xs, ss[j // 2][:, (j % 2) * LANES:(j % 2 + 1) * LANES],
                                             gain, cos, sin))
        kmin = jnp.where(step * tb + r0 == 0, QBLOCK, 0)
        valid = band & (kj >= kmin)
        for c in range(KV_HEADS):
            kcur = slabs[ATTN_WIDTH // LANES + c].astype(bf16)
            vcur = za_ref[rows, PA_V + c * LANES:PA_V + (c + 1) * LANES].astype(bf16)
            kcat = jnp.concatenate([kprev_ref[c], kcur], axis=0)
            vext = jnp.concatenate(
                [jnp.concatenate([vprev_ref[c], vcur], axis=0), ones_slab], axis=1)
            q_heads = []
            for j in (2 * c, 2 * c + 1):
                qs = slabs[j] * (ATTN_HEAD_DIM ** -0.5)
                q_heads.append(jnp.where(head_a, qs, 0.0).astype(bf16))
                q_heads.append(jnp.where(head_a, 0.0, qs).astype(bf16))
            qst = jnp.concatenate(q_heads, axis=0)
            s = lax.dot_general(qst, kcat, (((1,), (1,)), ((), ())), preferred_element_type=f32)
            s = jnp.where(valid, s, bias_ref[c])
            p = jnp.exp(s - jnp.max(s, axis=-1, keepdims=True)).astype(bf16)
            of = jnp.dot(p, vext, preferred_element_type=f32)
            o = of[:, :LANES] / of[:, LANES:]
            for jj in range(2):
                pair = jnp.where(low_half, o[(2 * jj) * QBLOCK:(2 * jj + 1) * QBLOCK],
                                 o[(2 * jj + 1) * QBLOCK:(2 * jj + 2) * QBLOCK])
                col = (2 * c + jj) * LANES
                mix_ref[rows, col:col + LANES] = pair.astype(bf16)
            kprev_ref[c] = kcur
            vprev_ref[c] = jnp.where(row0, jnp.zeros_like(vcur), vcur)

    ti = lax.broadcasted_iota(jnp.int32, (MCHUNK, MCHUNK), 0)
    si = lax.broadcasted_iota(jnp.int32, (MCHUNK, MCHUNK), 1)
    causal = si <= ti
    ones_l = jnp.ones((MCHUNK, LANES), bf16)
    for hd in range(MLSTM_HEADS):
        hcols = lambda base: slice(base + hd * MLSTM_HEAD_DIM, base + (hd + 1) * MLSTM_HEAD_DIM)
        cext = cst_ref[hd]
        for c in range(tb // MCHUNK):
            rows = slice(c * MCHUNK, (c + 1) * MCHUNK)
            qb_ = zm_ref[rows, hcols(PM_Q)].astype(bf16)
            kf = zm_ref[rows, hcols(PM_K)] * (MLSTM_HEAD_DIM ** -0.5)
            vb = zm_ref[rows, hcols(PM_V)].astype(bf16)
            og = zm_ref[rows, hcols(PM_O)]
            g_c = colform[rows, hd:hd + 1]
            a_c = colform[rows, SUBLANES + hd:SUBLANES + hd + 1]
            emt_c = colform[rows, 2 * SUBLANES + hd:2 * SUBLANES + hd + 1]
            aend = colform[c * MCHUNK:c * MCHUNK + 1, 3 * SUBLANES + hd:3 * SUBLANES + hd + 1]
            sb, off = divmod(c * MCHUNK, LANES)
            u_r = u_rows[sb][hd:hd + 1, off:off + MCHUNK]
            w_r = w_rows[sb][hd:hd + 1, off:off + MCHUNK]
            dmat = jnp.exp(jnp.where(causal, u_r - g_c, NEG))
            smat = lax.dot_general(qb_, kf.astype(bf16), (((1,), (1,)), ((), ())),
                                   preferred_element_type=f32) * dmat
            vext = jnp.concatenate([vb, ones_l], axis=1)
            nd = (a_c * jnp.dot(qb_, cext.astype(bf16), preferred_element_type=f32)
                  + jnp.dot(smat.astype(bf16), vext, preferred_element_type=f32))
            hraw = nd[:, :MLSTM_HEAD_DIM] / jnp.maximum(jnp.abs(nd[:, MLSTM_HEAD_DIM:]), emt_c)
            hn = _rms(hraw, mnorm_ref[hd:hd + 1, :])
            mix_ref[rows, ATTN_WIDTH + hd * MLSTM_HEAD_DIM:ATTN_WIDTH + (hd + 1) * MLSTM_HEAD_DIM] = (
                (hn * jax.nn.sigmoid(og)).astype(bf16))
            kw_t = (kf.T * w_r).astype(bf16)
            cext = aend * cext + jnp.dot(kw_t, vext, preferred_element_type=f32)
        cst_ref[hd] = cext
        cext_ref[hd] = cext

    x1_ref[...] = x + jnp.dot(mix_ref[...], wout_ref[...], preferred_element_type=f32)


def _const_spec(shape):
    nd = len(shape)
    return pl.BlockSpec(shape, lambda i, *_: (0,) * nd)


def _prompt_mixer(x, cos, sin, wa, wm, ws, wout_b, bd, bias, anorm, qg, kg, kgs, coss, sins, gbias, mnorm):
    t = x.shape[0]
    tb = PROMPT_BLOCK
    state_shape = (MLSTM_HEADS, MLSTM_HEAD_DIM, 2 * MLSTM_HEAD_DIM)
    return pl.pallas_call(
        _prompt_mixer_kernel,
        grid=(t // tb,),
        in_specs=[
            pl.BlockSpec((tb, D_MODEL), lambda i: (i, 0)),
            pl.BlockSpec((tb, LANES), lambda i: (i, 0)),
            pl.BlockSpec((tb, LANES), lambda i: (i, 0)),
            _const_spec((D_MODEL, PA_WIDTH)),
            _const_spec((D_MODEL, PM_WIDTH)),
            _const_spec((D_MODEL, 2 * KV_WIDTH)),
            _const_spec((MIX_WIDTH, D_MODEL)),
            _const_spec((2 * LANES, 2 * LANES)),
            _const_spec((KV_HEADS, ATTN_GROUP * QBLOCK, 2 * QBLOCK)),
            _const_spec((1, D_MODEL)),
            _const_spec((1, LANES)),
            _const_spec((1, LANES)),
            _const_spec((1, LANES)),
            _const_spec((WINDOW, LANES)),
            _const_spec((WINDOW, LANES)),
            _const_spec((1, LANES)),
            _const_spec((MLSTM_HEADS, MLSTM_HEAD_DIM)),
        ],
        out_specs=[
            pl.BlockSpec((tb, D_MODEL), lambda i: (i, 0)),
            _const_spec((WINDOW, KV_WIDTH)),
            _const_spec((WINDOW, KV_WIDTH)),
            _const_spec(state_shape),
            _const_spec((SUBLANES, LANES)),
        ],
        out_shape=[
            jax.ShapeDtypeStruct((t, D_MODEL), f32),
            jax.ShapeDtypeStruct((WINDOW, KV_WIDTH), f32),
            jax.ShapeDtypeStruct((WINDOW, KV_WIDTH), f32),
            jax.ShapeDtypeStruct(state_shape, f32),
            jax.ShapeDtypeStruct((SUBLANES, LANES), f32),
        ],
        scratch_shapes=[
            pltpu.VMEM((tb, PA_WIDTH), f32),
            pltpu.VMEM((tb, PM_WIDTH), f32),
            pltpu.VMEM((tb, MIX_WIDTH), bf16),
            pltpu.VMEM((KV_HEADS, WINDOW, LANES), bf16),
            pltpu.VMEM((KV_HEADS, WINDOW, LANES), bf16),
            pltpu.VMEM(state_shape, f32),
            pltpu.VMEM((SUBLANES, LANES), f32),
        ],
        compiler_params=pltpu.CompilerParams(
            dimension_semantics=("arbitrary",), vmem_limit_bytes=VMEM_LIMIT),
        name="prompt_mixer",
    )(x, cos, sin, wa, wm, ws, wout_b, bd, bias, anorm, qg, kg, kgs, coss, sins, gbias, mnorm)


def _ffn_kernel(x_ref, g_ref, wg_ref, wu_ref, wd_ref, o_ref):
    x = x_ref[...]
    hf = _rms(x, g_ref[...]).astype(bf16)
    acc = x
    for c in range(D_FF // FFN_CHUNK):
        cs = slice(c * FFN_CHUNK, (c + 1) * FFN_CHUNK)
        gate = jnp.dot(hf, wg_ref[:, cs], preferred_element_type=f32)
        up = jnp.dot(hf, wu_ref[:, cs], preferred_element_type=f32)
        act = (gate * jax.nn.sigmoid(gate) * up).astype(bf16)
        acc = acc + jnp.dot(act, wd_ref[cs, :], preferred_element_type=f32)
    o_ref[...] = acc


def _ffn(x, fnorm, wg_b, wu_b, wd_b):
    n = x.shape[0]
    tm = FFN_BLOCK
    return pl.pallas_call(
        _ffn_kernel,
        grid=(n // tm,),
        in_specs=[
            pl.BlockSpec((tm, D_MODEL), lambda i: (i, 0)),
            _const_spec((1, D_MODEL)),
            _const_spec((D_MODEL, D_FF)),
            _const_spec((D_MODEL, D_FF)),
            _const_spec((D_FF, D_MODEL)),
        ],
        out_specs=pl.BlockSpec((tm, D_MODEL), lambda i: (i, 0)),
        out_shape=jax.ShapeDtypeStruct((n, D_MODEL), f32),
        compiler_params=pltpu.CompilerParams(
            dimension_semantics=("arbitrary",), vmem_limit_bytes=VMEM_LIMIT),
        name="ffn",
    )(x, fnorm, wg_b, wu_b, wd_b)


def _sample_proj_kernel(x_ref, cos_ref, sin_ref, win_ref, anorm_ref, qg_ref, kg_ref, gbias_ref,
                        q_ref, k_ref, v_ref, zm_ref, gate_ref):
    n = x_ref.shape[0]
    z = _in_proj(x_ref[...], anorm_ref[...], win_ref)
    lane = lax.broadcasted_iota(jnp.int32, (n, LANES), 1)
    cos = cos_ref[...]
    sin = sin_ref[...]
    for j in range(ATTN_WIDTH // LANES):
        qs = _headnorm_rope(z[:, COL_QA + j * LANES:COL_QA + (j + 1) * LANES], qg_ref[...], cos, sin, lane)
        q_ref[:, j * LANES:(j + 1) * LANES] = qs * (ATTN_HEAD_DIM ** -0.5)
    k_ref[...] = _headnorm_rope(z[:, COL_KA:COL_KA + KV_WIDTH], kg_ref[...], cos, sin, lane)
    v_ref[...] = z[:, COL_VA:COL_VA + KV_WIDTH]
    zm_ref[:, 0:MLSTM_WIDTH] = z[:, COL_QM:COL_KM]
    zm_ref[:, MLSTM_WIDTH:2 * MLSTM_WIDTH] = z[:, COL_KM:COL_VM] * (MLSTM_HEAD_DIM ** -0.5)
    zm_ref[:, 2 * MLSTM_WIDTH:4 * MLSTM_WIDTH] = z[:, COL_VM:COL_G]
    gcol = z[:, COL_G:COL_G + LANES] + gbias_ref[...]
    gate_ref[...] = jnp.where(lane < FG_LANE, gcol, jax.nn.log_sigmoid(gcol))


def _sample_proj(x, cos, sin, win_p, anorm, qg, kg, gbias):
    n = x.shape[0]
    full = lambda shape: pl.BlockSpec(shape, lambda i: (0,) * len(shape))
    return pl.pallas_call(
        _sample_proj_kernel,
        grid=(1,),
        in_specs=[full((n, D_MODEL)), full((n, LANES)), full((n, LANES)), full((D_MODEL, IN_PAD)),
                  full((1, D_MODEL)), full((1, LANES)), full((1, LANES)), full((1, LANES))],
        out_specs=[full((n, ATTN_WIDTH)), full((n, KV_WIDTH)), full((n, KV_WIDTH)),
                   full((n, 4 * MLSTM_WIDTH)), full((n, LANES))],
        out_shape=[jax.ShapeDtypeStruct((n, ATTN_WIDTH), f32),
                   jax.ShapeDtypeStruct((n, KV_WIDTH), f32),
                   jax.ShapeDtypeStruct((n, KV_WIDTH), f32),
                   jax.ShapeDtypeStruct((n, 4 * MLSTM_WIDTH), f32),
                   jax.ShapeDtypeStruct((n, LANES), f32)],
        compiler_params=pltpu.CompilerParams(vmem_limit_bytes=VMEM_LIMIT),
        name="sample_proj",
    )(x, cos, sin, win_p, anorm, qg, kg, gbias)


def _sample_core_kernel(qbd_ref, ck_ref, cv_ref, kn_ref, vn_ref, sink_ref,
                        qm_ref, km_ref, vm_ref, gp_ref, gr_ref, c_ref, n_ref, m_ref,
                        o_ref, h_ref, cn_ref, nn_ref, mn_ref):
    bb = qbd_ref.shape[0]
    nrow = qbd_ref.shape[1]
    tpad = kn_ref.shape[1]
    nreal = nrow // ATTN_HEADS
    qbd = qbd_ref[...].astype(bf16)
    zpad_k = jnp.zeros((bb, LANES - tpad, LANES), bf16)
    kk = jnp.concatenate([ck_ref[...].astype(bf16), kn_ref[...].astype(bf16), zpad_k], axis=1)
    vv = jnp.concatenate([cv_ref[...].astype(bf16), vn_ref[...].astype(bf16), zpad_k], axis=1)
    s = jnp.einsum('bqd,bkd->bqk', qbd, kk, preferred_element_type=f32)
    tq = lax.broadcasted_iota(jnp.int32, s.shape, 1) & (nreal - 1)
    kj = lax.broadcasted_iota(jnp.int32, s.shape, 2)
    valid = ((kj < WINDOW) & (kj > tq)) | ((kj >= WINDOW) & ((kj - WINDOW) <= tq))
    s = jnp.where(valid, s, NEG)
    sink = sink_ref[:, 0:1][None]
    mx = jnp.maximum(jnp.max(s, axis=-1, keepdims=True), sink)
    p = jnp.exp(s - mx)
    den = jnp.sum(p, axis=-1, keepdims=True) + jnp.exp(sink - mx)
    o = jnp.einsum('bqk,bkd->bqd', p.astype(bf16), vv, preferred_element_type=f32)
    o_ref[...] = o / den

    q = qm_ref[...]
    k = km_ref[...]
    v = vm_ref[...]
    gp = gp_ref[...]
    gr = gr_ref[...]
    c0 = c_ref[...]
    n0 = n_ref[...]
    m0 = m_ref[...]
    ig_c = gp[:, :, 0:1]
    lf_c = gp[:, :, 1:2]
    ig_r = gr[:, 0:1, :]
    lf_r = gr[:, 1:2, :]
    ti = lax.broadcasted_iota(jnp.int32, q.shape, 1)
    si = lax.broadcasted_iota(jnp.int32, q.shape, 2)
    tri = si <= ti
    b_c = jnp.sum(jnp.where(tri, lf_r, 0.0), axis=2, keepdims=True)
    b_r = jnp.sum(jnp.where(ti <= si, lf_c, 0.0), axis=1, keepdims=True)
    dlog = jnp.where(tri, b_c - b_r + ig_r, NEG)
    inter = b_c + m0
    m_t = jnp.maximum(inter, jnp.max(dlog, axis=2, keepdims=True))
    dmat = jnp.exp(dlog - m_t)
    smat = jnp.zeros_like(q)
    for sx in range(nreal):
        col = jnp.sum(q * k[:, sx:sx + 1, :], axis=2, keepdims=True)
        smat = jnp.where(si == sx, col, smat)
    smat = smat * dmat
    a = jnp.exp(inter - m_t)
    qc = jnp.einsum('gtd,gde->gte', q.astype(bf16), c0.astype(bf16), preferred_element_type=f32)
    intra = jnp.zeros_like(q)
    for sx in range(nreal):
        intra = intra + smat[:, :, sx:sx + 1] * v[:, sx:sx + 1, :]
    num = a * qc + intra
    qn = jnp.sum(q * n0, axis=2, keepdims=True)
    den_m = a * qn + jnp.sum(smat, axis=2, keepdims=True)
    h_ref[...] = num / jnp.maximum(jnp.abs(den_m), jnp.exp(-m_t))
    last = nreal - 1
    m_new = m_t[:, last:last + 1, :]
    b_last = b_c[:, last:last + 1, :]
    a_end = jnp.exp(b_last + m0 - m_new)
    w_c = jnp.exp(b_last - b_c + ig_c - m_new)
    kw = k * w_c
    upd = jnp.einsum('gsd,gse->gde', kw.astype(bf16), v.astype(bf16), preferred_element_type=f32)
    cn_ref[...] = a_end * c0 + upd
    nn_ref[...] = a_end * n0 + jnp.sum(kw, axis=1, keepdims=True)
    mn_ref[...] = m_new


def _sample_core(qbd, ck, cv, kn, vn, sink_rows, qm, km, vm, gp, gr, c0, n0, m0):
    nb = qbd.shape[0]
    bb = SAMPLE_BATCH_BLOCK
    gb = bb * MLSTM_HEADS
    ng = nb * MLSTM_HEADS
    nrow = qbd.shape[1]
    tpad = kn.shape[1]
    blk = lambda shape: pl.BlockSpec(shape, lambda i: (i,) + (0,) * (len(shape) - 1))
    return pl.pallas_call(
        _sample_core_kernel,
        grid=(nb // bb,),
        in_specs=[blk((bb, nrow, LANES)), blk((bb, WINDOW, KV_WIDTH)), blk((bb, WINDOW, KV_WIDTH)),
                  blk((bb, tpad, KV_WIDTH)), blk((bb, tpad, KV_WIDTH)),
                  pl.BlockSpec((nrow, LANES), lambda i: (0, 0)),
                  blk((gb, tpad, MLSTM_HEAD_DIM)), blk((gb, tpad, MLSTM_HEAD_DIM)),
                  blk((gb, tpad, MLSTM_HEAD_DIM)), blk((gb, tpad, LANES)), blk((gb, tpad, LANES)),
                  blk((gb, MLSTM_HEAD_DIM, MLSTM_HEAD_DIM)), blk((gb, 1, MLSTM_HEAD_DIM)),
                  blk((gb, 1, 1))],
        out_specs=[blk((bb, nrow, LANES)), blk((gb, tpad, MLSTM_HEAD_DIM)),
                   blk((gb, MLSTM_HEAD_DIM, MLSTM_HEAD_DIM)), blk((gb, 1, MLSTM_HEAD_DIM)),
                   blk((gb, 1, 1))],
        out_shape=[jax.ShapeDtypeStruct((nb, nrow, LANES), f32),
                   jax.ShapeDtypeStruct((ng, tpad, MLSTM_HEAD_DIM), f32),
                   jax.ShapeDtypeStruct((ng, MLSTM_HEAD_DIM, MLSTM_HEAD_DIM), f32),
                   jax.ShapeDtypeStruct((ng, 1, MLSTM_HEAD_DIM), f32),
                   jax.ShapeDtypeStruct((ng, 1, 1), f32)],
        compiler_params=pltpu.CompilerParams(
            dimension_semantics=("arbitrary",), vmem_limit_bytes=VMEM_LIMIT),
        name="sample_core",
    )(qbd, ck, cv, kn, vn, sink_rows, qm, km, vm, gp, gr, c0, n0, m0)


def _sample_out_kernel(x_ref, ao_ref, hm_ref, om_ref, mnorm_ref, wout_ref, x1_ref):
    parts = [ao_ref[...].astype(bf16)]
    for hd in range(MLSTM_HEADS):
        cs = slice(hd * MLSTM_HEAD_DIM, (hd + 1) * MLSTM_HEAD_DIM)
        hn = _rms(hm_ref[:, cs], mnorm_ref[hd:hd + 1, :])
        parts.append((hn * jax.nn.sigmoid(om_ref[:, cs])).astype(bf16))
    mix = jnp.concatenate(parts, axis=1)
    x1_ref[...] = x_ref[...] + jnp.dot(mix, wout_ref[...], preferred_element_type=f32)


def _sample_out(x, ao, hm, om, mnorm, wout_b):
    n = x.shape[0]
    full = lambda shape: pl.BlockSpec(shape, lambda i: (0,) * len(shape))
    return pl.pallas_call(
        _sample_out_kernel,
        grid=(1,),
        in_specs=[full((n, D_MODEL)), full((n, ATTN_WIDTH)), full((n, MLSTM_WIDTH)),
                  full((n, MLSTM_WIDTH)), full((MLSTM_HEADS, MLSTM_HEAD_DIM)),
                  full((MIX_WIDTH, D_MODEL))],
        out_specs=full((n, D_MODEL)),
        out_shape=jax.ShapeDtypeStruct((n, D_MODEL), f32),
        compiler_params=pltpu.CompilerParams(vmem_limit_bytes=VMEM_LIMIT),
        name="sample_out",
    )(x, ao, hm, om, mnorm, wout_b)


def _rope_angles(pos):
    half = ATTN_HEAD_DIM // 2
    inv = ROPE_THETA ** (-jnp.arange(half, dtype=f32) / half)
    ang = pos[:, None] * inv[None, :]
    return jnp.cos(ang), jnp.sin(ang)


def _rope_tables(pos):
    c, s = _rope_angles(pos)
    cos = jnp.tile(c, (1, LANES // QUARTER))
    sin = jnp.tile(jnp.concatenate([-s, s], axis=1), (1, LANES // ATTN_HEAD_DIM))
    return cos, sin


def _rope_tables_quarters(pos):
    c, s = _rope_angles(pos)
    return jnp.tile(c, (1, LANES // QUARTER)), jnp.concatenate([-s, -s, s, s], axis=1)


def _quarters(a):
    lo, hi = a[..., :QUARTER], a[..., QUARTER:]
    return jnp.concatenate([lo, lo, hi, hi], axis=-1)


def _prompt_attn_weights(w):
    d = w.shape[0]
    wq = w[:, COL_QA:COL_KA].reshape(d, ATTN_WIDTH // LANES, 2, 2, QUARTER)
    wq = wq.transpose(0, 1, 3, 2, 4).reshape(d, ATTN_WIDTH)
    wk = _quarters(w[:, COL_KA:COL_VA].reshape(d, KV_HEADS, ATTN_HEAD_DIM)).reshape(d, KV_HEADS * LANES)
    wv = w[:, COL_VA:COL_QM].reshape(d, KV_HEADS, 1, ATTN_HEAD_DIM)
    wv = jnp.broadcast_to(wv, (d, KV_HEADS, 2, ATTN_HEAD_DIM)).reshape(d, KV_HEADS * LANES)
    return jnp.concatenate([wq, wk, wv], axis=1)


def _to_head_major(a, nb, nt, nh, tpad, fill=0.0):
    d = a.shape[1] // nh
    a = a.reshape(nb, nt, nh, d).transpose(0, 2, 1, 3).reshape(nb * nh, nt, d)
    return jnp.pad(a, ((0, 0), (0, tpad - nt), (0, 0)), constant_values=fill)


def kernel(x_prompt, x_sample, cache_k, cache_v, state_C, state_n, state_m, attn_norm, w_in, q_norm,
           k_norm, attn_sinks, b_ig, b_fg, mlstm_norm, w_out, ffn_norm, w_gate, w_up, w_down):
    assert w_in.shape[0] == 1 and x_prompt.shape[0] == 1
    tp = x_prompt.shape[1]
    nb, nt = x_sample.shape[0], x_sample.shape[1]
    tpad = SUBLANES
    nh = MLSTM_HEADS

    w = w_in[0]
    gate_w = jnp.zeros((D_MODEL, LANES), f32)
    gate_w = gate_w.at[:, 0:nh].set(w[:, COL_G:COL_G + nh])
    gate_w = gate_w.at[:, FG_LANE:FG_LANE + nh].set(w[:, COL_G + nh:COL_G + 2 * nh])
    win_p = jnp.concatenate([w[:, :COL_G], gate_w], axis=1).astype(bf16)
    gbias = jnp.zeros((1, LANES), f32)
    gbias = gbias.at[0, 0:nh].set(b_ig[0]).at[0, FG_LANE:FG_LANE + nh].set(b_fg[0])
    wout_b = w_out[0].astype(bf16)
    wg_b = w_gate[0].astype(bf16)
    wu_b = w_up[0].astype(bf16)
    wd_b = w_down[0].astype(bf16)
    anorm = attn_norm[0].reshape(1, D_MODEL)
    fnorm = ffn_norm[0].reshape(1, D_MODEL)
    qg = jnp.tile(q_norm[0], LANES // ATTN_HEAD_DIM).reshape(1, LANES)
    kg = jnp.tile(k_norm[0], LANES // ATTN_HEAD_DIM).reshape(1, LANES)
    mnorm = mlstm_norm[0].reshape(nh, MLSTM_HEAD_DIM)
    sinks = attn_sinks[0]

    wa = _prompt_attn_weights(w).astype(bf16)
    wm = win_p[:, COL_QM:]
    ws = win_p[:, COL_KA:COL_QM]
    idx = jnp.arange(2 * LANES)
    same = (idx[:, None] // LANES == idx[None, :] // LANES) & (
        (idx[:, None] // QUARTER) % 2 == (idx[None, :] // QUARTER) % 2)
    bd = same.astype(bf16)
    sink_rows_p = jnp.repeat(sinks.reshape(KV_HEADS, ATTN_GROUP), QBLOCK, axis=1)
    bias = jnp.where(jnp.arange(2 * QBLOCK)[None, None, :] == 0, sink_rows_p[:, :, None], NEG)
    qgq = _quarters(q_norm[0]).reshape(1, LANES)
    kgq = _quarters(k_norm[0]).reshape(1, LANES)
    pos_p = jnp.arange(tp, dtype=f32)
    cos_p, sin_p = _rope_tables_quarters(pos_p)
    cos_w, sin_w = _rope_tables(pos_p[tp - WINDOW:])
    x1_p, k_p, v_p, cext_p, m_p = _prompt_mixer(
        x_prompt[0], cos_p, sin_p, wa, wm, ws, wout_b, bd, bias, anorm, qgq, kgq, kg, cos_w, sin_w,
        gbias, mnorm)
    y_p = _ffn(x1_p, fnorm, wg_b, wu_b, wd_b)

    xs = x_sample.reshape(nb * nt, D_MODEL)
    cos_s, sin_s = _rope_tables(jnp.arange(nt, dtype=f32) + PAST_LEN)
    cos_s = jnp.tile(cos_s, (nb, 1))
    sin_s = jnp.tile(sin_s, (nb, 1))
    q_s, k_s, v_s, zm_s, gate_s = _sample_proj(xs, cos_s, sin_s, win_p, anorm, qg, kg, gbias)

    qh = q_s.reshape(nb, nt, ATTN_HEADS, ATTN_HEAD_DIM).transpose(0, 2, 1, 3)
    half_rows = ATTN_GROUP * nt
    q0 = qh[:, :ATTN_GROUP].reshape(nb, half_rows, ATTN_HEAD_DIM)
    q1 = qh[:, ATTN_GROUP:].reshape(nb, half_rows, ATTN_HEAD_DIM)
    zq = jnp.zeros_like(q0)
    qbd = jnp.concatenate([jnp.concatenate([q0, zq], axis=2), jnp.concatenate([zq, q1], axis=2)], axis=1)
    kn = jnp.pad(k_s.reshape(nb, nt, KV_WIDTH), ((0, 0), (0, tpad - nt), (0, 0)))
    vn = jnp.pad(v_s.reshape(nb, nt, KV_WIDTH), ((0, 0), (0, tpad - nt), (0, 0)))
    sink_rows = jnp.broadcast_to(jnp.repeat(sinks, nt)[:, None], (ATTN_HEADS * nt, LANES))
    qm = _to_head_major(zm_s[:, 0:MLSTM_WIDTH], nb, nt, nh, tpad)
    km = _to_head_major(zm_s[:, MLSTM_WIDTH:2 * MLSTM_WIDTH], nb, nt, nh, tpad)
    vm = _to_head_major(zm_s[:, 2 * MLSTM_WIDTH:3 * MLSTM_WIDTH], nb, nt, nh, tpad)
    ig = gate_s[:, 0:nh].reshape(nb, nt, nh).transpose(0, 2, 1).reshape(nb * nh, nt)
    lf = gate_s[:, FG_LANE:FG_LANE + nh].reshape(nb, nt, nh).transpose(0, 2, 1).reshape(nb * nh, nt)
    ig_cp = jnp.pad(ig, ((0, 0), (0, tpad - nt)), constant_values=NEG)
    lf_cp = jnp.pad(lf, ((0, 0), (0, tpad - nt)))
    gp = jnp.pad(jnp.stack([ig_cp, lf_cp], axis=2), ((0, 0), (0, 0), (0, LANES - 2)))
    ig_rp = jnp.pad(ig, ((0, 0), (0, LANES - nt)), constant_values=NEG)
    lf_rp = jnp.pad(lf, ((0, 0), (0, LANES - nt)))
    gr = jnp.pad(jnp.stack([ig_rp, lf_rp], axis=1), ((0, 0), (0, tpad - 2), (0, 0)))
    ck = cache_k[0].reshape(nb, WINDOW, KV_WIDTH)
    cv = cache_v[0].reshape(nb, WINDOW, KV_WIDTH)
    c0 = state_C[0].reshape(nb * nh, MLSTM_HEAD_DIM, MLSTM_HEAD_DIM)
    n0 = state_n[0].reshape(nb * nh, 1, MLSTM_HEAD_DIM)
    m0 = state_m[0].reshape(nb * nh, 1, 1)

    o_bd, h_s, c_new, n_new, m_new = _sample_core(qbd, ck, cv, kn, vn, sink_rows, qm, km, vm, gp, gr,
                                                  c0, n0, m0)

    o0 = o_bd[:, :half_rows, :ATTN_HEAD_DIM].reshape(nb, ATTN_GROUP, nt, ATTN_HEAD_DIM)
    o1 = o_bd[:, half_rows:, ATTN_HEAD_DIM:].reshape(nb, ATTN_GROUP, nt, ATTN_HEAD_DIM)
    ao = jnp.concatenate([o0, o1], axis=1).transpose(0, 2, 1, 3).reshape(nb * nt, ATTN_WIDTH)
    hm = h_s[:, :nt].reshape(nb, nh, nt, MLSTM_HEAD_DIM).transpose(0, 2, 1, 3).reshape(nb * nt, MLSTM_WIDTH)
    om = zm_s[:, 3 * MLSTM_WIDTH:]
    x1_s = _sample_out(xs, ao, hm, om, mnorm, wout_b)
    y_s = _ffn(x1_s, fnorm, wg_b, wu_b, wd_b)

    new_k_s = jnp.concatenate([ck[:, nt:], k_s.reshape(nb, nt, KV_WIDTH)], axis=1)
    new_v_s = jnp.concatenate([cv[:, nt:], v_s.reshape(nb, nt, KV_WIDTH)], axis=1)

    kv_shape = (1, 1, WINDOW, KV_HEADS, ATTN_HEAD_DIM)
    return (
        y_p[None],
        y_s.reshape(nb, nt, D_MODEL),
        k_p.reshape(kv_shape),
        v_p.reshape(kv_shape),
        cext_p[None, None, :, :, :MLSTM_HEAD_DIM],
        cext_p[None, None, :, :, MLSTM_HEAD_DIM],
        m_p[None, None, :nh, 0],
        new_k_s.reshape(1, nb, WINDOW, KV_HEADS, ATTN_HEAD_DIM),
        new_v_s.reshape(1, nb, WINDOW, KV_HEADS, ATTN_HEAD_DIM),
        c_new.reshape(1, nb, nh, MLSTM_HEAD_DIM, MLSTM_HEAD_DIM),
        n_new.reshape(1, nb, nh, MLSTM_HEAD_DIM),
        m_new.reshape(1, nb, nh),
    )
```

```python
import jax
import jax.numpy as jnp
import numpy as np
from jax import lax
from jax.experimental import pallas as pl
from jax.experimental.pallas import tpu as pltpu

D_MODEL = 1024
PAST_LEN = 16384
ATTN_HEADS = 8
KV_HEADS = 2
ATTN_HEAD_DIM = 64
ATTN_GROUP = ATTN_HEADS // KV_HEADS
ATTN_WIDTH = ATTN_HEADS * ATTN_HEAD_DIM
KV_WIDTH = KV_HEADS * ATTN_HEAD_DIM
WINDOW = 128
ROPE_THETA = 10000.0
MLSTM_HEADS = 4
MLSTM_HEAD_DIM = 128
MLSTM_WIDTH = MLSTM_HEADS * MLSTM_HEAD_DIM
MIX_WIDTH = ATTN_WIDTH + MLSTM_WIDTH
D_FF = 2816
NORM_EPS = 1e-6

LANES = 128
SUBLANES = 8
VMEM_LIMIT = 56 * 1024 * 1024

COL_QA = 0
COL_KA = COL_QA + ATTN_WIDTH
COL_VA = COL_KA + KV_WIDTH
COL_QM = COL_VA + KV_WIDTH
COL_KM = COL_QM + MLSTM_WIDTH
COL_VM = COL_KM + MLSTM_WIDTH
COL_OM = COL_VM + MLSTM_WIDTH
COL_G = COL_OM + MLSTM_WIDTH
IN_PAD = COL_G + LANES
FG_LANE = SUBLANES

PROMPT_BLOCK = 256
QBLOCK = WINDOW
MCHUNK = 128
FFN_BLOCK = 512
FFN_CHUNK = 256
SAMPLE_BATCH_BLOCK = 16
NEG = -1e30

f32 = jnp.float32
bf16 = jnp.bfloat16


def _rms(x, gain):
    return x * lax.rsqrt(jnp.mean(x * x, axis=-1, keepdims=True) + NORM_EPS) * gain


def _segsum64(s, lane):
    for k in (1, 2, 4, 8, 16, 32):
        s = s + jnp.where((lane & k) != 0, pltpu.roll(s, k, 1), pltpu.roll(s, LANES - k, 1))
    return s


def _headnorm_rope(xs, gain, cos, sin_signed, lane):
    ss = _segsum64(xs * xs, lane)
    y = xs * lax.rsqrt(ss * (1.0 / ATTN_HEAD_DIM) + NORM_EPS) * gain
    partner = jnp.where((lane & 32) != 0, pltpu.roll(y, 32, 1), pltpu.roll(y, LANES - 32, 1))
    return y * cos + partner * sin_signed


def _in_proj(x, anorm, win_ref):
    h = _rms(x, anorm).astype(bf16)
    return jnp.dot(h, win_ref[...], preferred_element_type=f32)


PA_Q = 0
PA_K = PA_Q + ATTN_WIDTH
PA_V = PA_K + KV_HEADS * LANES
PA_WIDTH = PA_V + KV_HEADS * LANES
PM_Q = 0
PM_K = PM_Q + MLSTM_WIDTH
PM_V = PM_K + MLSTM_WIDTH
PM_O = PM_V + MLSTM_WIDTH
PM_G = PM_O + MLSTM_WIDTH
PM_WIDTH = PM_G + LANES
QUARTER = ATTN_HEAD_DIM // 2


def _group_sumsq(xs, bd_ref):
    x2 = xs * xs
    hi = x2.astype(bf16)
    lo = (x2 - hi.astype(f32)).astype(bf16)
    return (jnp.dot(hi, bd_ref[...], preferred_element_type=f32)
            + jnp.dot(lo, bd_ref[...], preferred_element_type=f32))


def _norm_rope_quarters(xs, ss, gain, cos, sin_signed):
    y = xs * lax.rsqrt(ss * (1.0 / ATTN_HEAD_DIM) + NORM_EPS) * gain
    return y * cos + pltpu.roll(y, LANES // 2, 1) * sin_signed


def _prompt_mixer_kernel(x_ref, cos_ref, sin_ref, wa_ref, wm_ref, ws_ref, wout_ref, bd_ref, bias_ref,
                         anorm_ref, qg_ref, kg_ref, kgs_ref, coss_ref, sins_ref, gbias_ref, mnorm_ref,
                         x1_ref, kout_ref, vout_ref, cext_ref, mout_ref,
                         za_ref, zm_ref, mix_ref, kprev_ref, vprev_ref, cst_ref, mst_ref):
    step = pl.program_id(0)
    nstep = pl.num_programs(0)
    tb = x_ref.shape[0]

    @pl.when(step == 0)
    def _init():
        kprev_ref[...] = jnp.zeros_like(kprev_ref)
        vprev_ref[...] = jnp.zeros_like(vprev_ref)
        cst_ref[...] = jnp.zeros_like(cst_ref)
        mst_ref[...] = jnp.zeros_like(mst_ref)

    x = x_ref[...]
    h = _rms(x, anorm_ref[...]).astype(bf16)
    za_ref[...] = jnp.dot(h, wa_ref[...], preferred_element_type=f32)
    zm_ref[...] = jnp.dot(h, wm_ref[...], preferred_element_type=f32)

    @pl.when(step == nstep - 1)
    def _window_out():
        zs = jnp.dot(h[tb - WINDOW:, :], ws_ref[...], preferred_element_type=f32)
        lane_s = lax.broadcasted_iota(jnp.int32, (WINDOW, LANES), 1)
        kout_ref[...] = _headnorm_rope(zs[:, :KV_WIDTH], kgs_ref[...], coss_ref[...], sins_ref[...], lane_s)
        vout_ref[...] = zs[:, KV_WIDTH:]

    lane_t = lax.broadcasted_iota(jnp.int32, (tb, LANES), 1)
    gcol = zm_ref[:, PM_G:PM_G + LANES] + gbias_ref[...]
    acol = jnp.where(lane_t < FG_LANE, gcol, jax.nn.log_sigmoid(gcol))
    arow = acol.T
    lane8 = lax.broadcasted_iota(jnp.int32, (SUBLANES, LANES), 1)
    lane_in = lane8 & (MCHUNK - 1)
    m_prev = mst_ref[:, 0:1]
    stacks = []
    u_rows = []
    w_rows = []
    for sb in range(tb // LANES):
        ls = slice(sb * LANES, (sb + 1) * LANES)
        ig8 = arow[0:SUBLANES, ls]
        lf8 = arow[FG_LANE:FG_LANE + SUBLANES, ls]
        b8 = lf8
        k = 1
        while k < MCHUNK:
            b8 = b8 + jnp.where(lane_in >= k, pltpu.roll(b8, k, 1), 0.0)
            k *= 2
        u8 = ig8 - b8
        cm8 = u8
        k = 1
        while k < MCHUNK:
            cm8 = jnp.maximum(cm8, jnp.where(lane_in >= k, pltpu.roll(cm8, k, 1), NEG))
            k *= 2
        g8 = jnp.zeros_like(u8)
        mp8 = jnp.zeros_like(u8)
        gl8 = jnp.zeros_like(u8)
        for c in range(LANES // MCHUNK):
            in_chunk = (lane8 // MCHUNK) == c
            gc = jnp.maximum(cm8, m_prev)
            last = c * MCHUNK + MCHUNK - 1
            g_last = jnp.max(jnp.where(lane8 == last, gc, NEG), axis=1, keepdims=True)
            b_last = jnp.max(jnp.where(lane8 == last, b8, NEG), axis=1, keepdims=True)
            g8 = jnp.where(in_chunk, gc, g8)
            mp8 = jnp.where(in_chunk, m_prev, mp8)
            gl8 = jnp.where(in_chunk, g_last, gl8)
            m_prev = b_last + g_last
        a8 = jnp.exp(mp8 - g8)
        emt8 = jnp.exp(-(b8 + g8))
        aend8 = jnp.exp(mp8 - gl8)
        stacks.append(jnp.concatenate(
            [g8, a8, emt8, aend8, jnp.zeros((LANES - 4 * SUBLANES, LANES), f32)], axis=0))
        u_rows.append(u8)
        w_rows.append(jnp.exp(u8 - gl8))
    mst_ref[...] = jnp.broadcast_to(m_prev, mst_ref.shape)
    mout_ref[...] = jnp.broadcast_to(m_prev, mout_ref.shape)
    colform = jnp.concatenate(stacks, axis=1).T

    lane = lax.broadcasted_iota(jnp.int32, (QBLOCK, LANES), 1)
    head_a = ((lane // QUARTER) & 1) == 0
    low_half = lane < ATTN_HEAD_DIM
    qi = lax.broadcasted_iota(jnp.int32, (ATTN_GROUP * QBLOCK, 2 * QBLOCK), 0) & (QBLOCK - 1)
    kj = lax.broadcasted_iota(jnp.int32, (ATTN_GROUP * QBLOCK, 2 * QBLOCK), 1)
    band = (kj > qi) & (kj <= qi + QBLOCK)
    row0 = lax.broadcasted_iota(jnp.int32, (QBLOCK, LANES), 0) == 0
    ones_slab = jnp.ones((2 * QBLOCK, LANES), bf16)
    for qb in range(tb // QBLOCK):
        r0 = qb * QBLOCK
        rows = slice(r0, r0 + QBLOCK)
        cos = cos_ref[rows, :]
        sin = sin_ref[rows, :]
        ss = [_group_sumsq(za_ref[rows, d * 2 * LANES:(d + 1) * 2 * LANES], bd_ref)
              for d in range(PA_V // (2 * LANES))]
        slabs = []
        for j in range(PA_V // LANES):
            xs = za_ref[rows, j * LANES:(j + 1) * LANES]
            gain = qg_ref[...] if j < ATTN_WIDTH // LANES else kg_ref[...]
            slabs.append(_norm_rope_quarters(xs, ss[j // 2][:, (j % 2) * LANES:(j % 2 + 1) * LANES],
                                             gain, cos, sin))
        kmin = jnp.where(step * tb + r0 == 0, QBLOCK, 0)
        valid = band & (kj >= kmin)
        for c in range(KV_HEADS):
            kcur = slabs[ATTN_WIDTH // LANES + c].astype(bf16)
            vcur = za_ref[rows, PA_V + c * LANES:PA_V + (c + 1) * LANES].astype(bf16)
            kcat = jnp.concatenate([kprev_ref[c], kcur], axis=0)
            vext = jnp.concatenate(
                [jnp.concatenate([vprev_ref[c], vcur], axis=0), ones_slab], axis=1)
            q_heads = []
            for j in (2 * c, 2 * c + 1):
                qs = slabs[j] * (ATTN_HEAD_DIM ** -0.5)
                q_heads.append(jnp.where(head_a, qs, 0.0).astype(bf16))
                q_heads.append(jnp.where(head_a, 0.0, qs).astype(bf16))
            qst = jnp.concatenate(q_heads, axis=0)
            s = lax.dot_general(qst, kcat, (((1,), (1,)), ((), ())), preferred_element_type=f32)
            s = jnp.where(valid, s, bias_ref[c])
            p = jnp.exp(s - jnp.max(s, axis=-1, keepdims=True)).astype(bf16)
            of = jnp.dot(p, vext, preferred_element_type=f32)
            o = of[:, :LANES] / of[:, LANES:]
            for jj in range(2):
                pair = jnp.where(low_half, o[(2 * jj) * QBLOCK:(2 * jj + 1) * QBLOCK],
                                 o[(2 * jj + 1) * QBLOCK:(2 * jj + 2) * QBLOCK])
                col = (2 * c + jj) * LANES
                mix_ref[rows, col:col + LANES] = pair.astype(bf16)
            kprev_ref[c] = kcur
            vprev_ref[c] = jnp.where(row0, jnp.zeros_like(vcur), vcur)

    ti = lax.broadcasted_iota(jnp.int32, (MCHUNK, MCHUNK), 0)
    si = lax.broadcasted_iota(jnp.int32, (MCHUNK, MCHUNK), 1)
    causal = si <= ti
    ones_l = jnp.ones((MCHUNK, LANES), bf16)
    for hd in range(MLSTM_HEADS):
        hcols = lambda base: slice(base + hd * MLSTM_HEAD_DIM, base + (hd + 1) * MLSTM_HEAD_DIM)
        cext = cst_ref[hd]
        for c in range(tb // MCHUNK):
            rows = slice(c * MCHUNK, (c + 1) * MCHUNK)
            qb_ = zm_ref[rows, hcols(PM_Q)].astype(bf16)
            kf = zm_ref[rows, hcols(PM_K)] * (MLSTM_HEAD_DIM ** -0.5)
            vb = zm_ref[rows, hcols(PM_V)].astype(bf16)
            og = zm_ref[rows, hcols(PM_O)]
            g_c = colform[rows, hd:hd + 1]
            a_c = colform[rows, SUBLANES + hd:SUBLANES + hd + 1]
            emt_c = colform[rows, 2 * SUBLANES + hd:2 * SUBLANES + hd + 1]
            aend = colform[c * MCHUNK:c * MCHUNK + 1, 3 * SUBLANES + hd:3 * SUBLANES + hd + 1]
            sb, off = divmod(c * MCHUNK, LANES)
            u_r = u_rows[sb][hd:hd + 1, off:off + MCHUNK]
            w_r = w_rows[sb][hd:hd + 1, off:off + MCHUNK]
            dmat = jnp.exp(jnp.where(causal, u_r - g_c, NEG))
            smat = lax.dot_general(qb_, kf.astype(bf16), (((1,), (1,)), ((), ())),
                                   preferred_element_type=f32) * dmat
            vext = jnp.concatenate([vb, ones_l], axis=1)
            nd = (a_c * jnp.dot(qb_, cext.astype(bf16), preferred_element_type=f32)
                  + jnp.dot(smat.astype(bf16), vext, preferred_element_type=f32))
            hraw = nd[:, :MLSTM_HEAD_DIM] / jnp.maximum(jnp.abs(nd[:, MLSTM_HEAD_DIM:]), emt_c)
            hn = _rms(hraw, mnorm_ref[hd:hd + 1, :])
            mix_ref[rows, ATTN_WIDTH + hd * MLSTM_HEAD_DIM:ATTN_WIDTH + (hd + 1) * MLSTM_HEAD_DIM] = (
                (hn * jax.nn.sigmoid(og)).astype(bf16))
            kw_t = (kf.T * w_r).astype(bf16)
            cext = aend * cext + jnp.dot(kw_t, vext, preferred_element_type=f32)
        cst_ref[hd] = cext
        cext_ref[hd] = cext

    x1_ref[...] = x + jnp.dot(mix_ref[...], wout_ref[...], preferred_element_type=f32)


def _const_spec(shape):
    nd = len(shape)
    return pl.BlockSpec(shape, lambda i, *_: (0,) * nd)


def _prompt_mixer(x, cos, sin, wa, wm, ws, wout_b, bd, bias, anorm, qg, kg, kgs, coss, sins, gbias, mnorm):
    t = x.shape[0]
    tb = PROMPT_BLOCK
    state_shape = (MLSTM_HEADS, MLSTM_HEAD_DIM, 2 * MLSTM_HEAD_DIM)
    return pl.pallas_call(
        _prompt_mixer_kernel,
        grid=(t // tb,),
        in_specs=[
            pl.BlockSpec((tb, D_MODEL), lambda i: (i, 0)),
            pl.BlockSpec((tb, LANES), lambda i: (i, 0)),
            pl.BlockSpec((tb, LANES), lambda i: (i, 0)),
            _const_spec((D_MODEL, PA_WIDTH)),
            _const_spec((D_MODEL, PM_WIDTH)),
            _const_spec((D_MODEL, 2 * KV_WIDTH)),
            _const_spec((MIX_WIDTH, D_MODEL)),
            _const_spec((2 * LANES, 2 * LANES)),
            _const_spec((KV_HEADS, ATTN_GROUP * QBLOCK, 2 * QBLOCK)),
            _const_spec((1, D_MODEL)),
            _const_spec((1, LANES)),
            _const_spec((1, LANES)),
            _const_spec((1, LANES)),
            _const_spec((WINDOW, LANES)),
            _const_spec((WINDOW, LANES)),
            _const_spec((1, LANES)),
            _const_spec((MLSTM_HEADS, MLSTM_HEAD_DIM)),
        ],
        out_specs=[
            pl.BlockSpec((tb, D_MODEL), lambda i: (i, 0)),
            _const_spec((WINDOW, KV_WIDTH)),
            _const_spec((WINDOW, KV_WIDTH)),
            _const_spec(state_shape),
            _const_spec((SUBLANES, LANES)),
        ],
        out_shape=[
            jax.ShapeDtypeStruct((t, D_MODEL), f32),
            jax.ShapeDtypeStruct((WINDOW, KV_WIDTH), f32),
            jax.ShapeDtypeStruct((WINDOW, KV_WIDTH), f32),
            jax.ShapeDtypeStruct(state_shape, f32),
            jax.ShapeDtypeStruct((SUBLANES, LANES), f32),
        ],
        scratch_shapes=[
            pltpu.VMEM((tb, PA_WIDTH), f32),
            pltpu.VMEM((tb, PM_WIDTH), f32),
            pltpu.VMEM((tb, MIX_WIDTH), bf16),
            pltpu.VMEM((KV_HEADS, WINDOW, LANES), bf16),
            pltpu.VMEM((KV_HEADS, WINDOW, LANES), bf16),
            pltpu.VMEM(state_shape, f32),
            pltpu.VMEM((SUBLANES, LANES), f32),
        ],
        compiler_params=pltpu.CompilerParams(
            dimension_semantics=("arbitrary",), vmem_limit_bytes=VMEM_LIMIT),
        name="prompt_mixer",
    )(x, cos, sin, wa, wm, ws, wout_b, bd, bias, anorm, qg, kg, kgs, coss, sins, gbias, mnorm)


def _ffn_kernel(x_ref, g_ref, wg_ref, wu_ref, wd_ref, o_ref):
    x = x_ref[...]
    hf = _rms(x, g_ref[...]).astype(bf16)
    acc = x
    for c in range(D_FF // FFN_CHUNK):
        cs = slice(c * FFN_CHUNK, (c + 1) * FFN_CHUNK)
        gate = jnp.dot(hf, wg_ref[:, cs], preferred_element_type=f32)
        up = jnp.dot(hf, wu_ref[:, cs], preferred_element_type=f32)
        act = (gate * jax.nn.sigmoid(gate) * up).astype(bf16)
        acc = acc + jnp.dot(act, wd_ref[cs, :], preferred_element_type=f32)
    o_ref[...] = acc


def _ffn(x, fnorm, wg_b, wu_b, wd_b):
    n = x.shape[0]
    tm = FFN_BLOCK
    return pl.pallas_call(
        _ffn_kernel,
        grid=(n // tm,),
        in_specs=[
            pl.BlockSpec((tm, D_MODEL), lambda i: (i, 0)),
            _const_spec((1, D_MODEL)),
            _const_spec((D_MODEL, D_FF)),
            _const_spec((D_MODEL, D_FF)),
            _const_spec((D_FF, D_MODEL)),
        ],
        out_specs=pl.BlockSpec((tm, D_MODEL), lambda i: (i, 0)),
        out_shape=jax.ShapeDtypeStruct((n, D_MODEL), f32),
        compiler_params=pltpu.CompilerParams(
            dimension_semantics=("arbitrary",), vmem_limit_bytes=VMEM_LIMIT),
        name="ffn",
    )(x, fnorm, wg_b, wu_b, wd_b)


def _sample_proj_kernel(x_ref, cos_ref, sin_ref, win_ref, anorm_ref, qg_ref, kg_ref, gbias_ref,
                        q_ref, k_ref, v_ref, zm_ref, gate_ref):
    n = x_ref.shape[0]
    z = _in_proj(x_ref[...], anorm_ref[...], win_ref)
    lane = lax.broadcasted_iota(jnp.int32, (n, LANES), 1)
    cos = cos_ref[...]
    sin = sin_ref[...]
    for j in range(ATTN_WIDTH // LANES):
        qs = _headnorm_rope(z[:, COL_QA + j * LANES:COL_QA + (j + 1) * LANES], qg_ref[...], cos, sin, lane)
        q_ref[:, j * LANES:(j + 1) * LANES] = qs * (ATTN_HEAD_DIM ** -0.5)
    k_ref[...] = _headnorm_rope(z[:, COL_KA:COL_KA + KV_WIDTH], kg_ref[...], cos, sin, lane)
    v_ref[...] = z[:, COL_VA:COL_VA + KV_WIDTH]
    zm_ref[:, 0:MLSTM_WIDTH] = z[:, COL_QM:COL_KM]
    zm_ref[:, MLSTM_WIDTH:2 * MLSTM_WIDTH] = z[:, COL_KM:COL_VM] * (MLSTM_HEAD_DIM ** -0.5)
    zm_ref[:, 2 * MLSTM_WIDTH:4 * MLSTM_WIDTH] = z[:, COL_VM:COL_G]
    gcol = z[:, COL_G:COL_G + LANES] + gbias_ref[...]
    gate_ref[...] = jnp.where(lane < FG_LANE, gcol, jax.nn.log_sigmoid(gcol))


def _sample_proj(x, cos, sin, win_p, anorm, qg, kg, gbias):
    n = x.shape[0]
    full = lambda shape: pl.BlockSpec(shape, lambda i: (0,) * len(shape))
    return pl.pallas_call(
        _sample_proj_kernel,
        grid=(1,),
        in_specs=[full((n, D_MODEL)), full((n, LANES)), full((n, LANES)), full((D_MODEL, IN_PAD)),
                  full((1, D_MODEL)), full((1, LANES)), full((1, LANES)), full((1, LANES))],
        out_specs=[full((n, ATTN_WIDTH)), full((n, KV_WIDTH)), full((n, KV_WIDTH)),
                   full((n, 4 * MLSTM_WIDTH)), full((n, LANES))],
        out_shape=[jax.ShapeDtypeStruct((n, ATTN_WIDTH), f32),
                   jax.ShapeDtypeStruct((n, KV_WIDTH), f32),
                   jax.ShapeDtypeStruct((n, KV_WIDTH), f32),
                   jax.ShapeDtypeStruct((n, 4 * MLSTM_WIDTH), f32),
                   jax.ShapeDtypeStruct((n, LANES), f32)],
        compiler_params=pltpu.CompilerParams(vmem_limit_bytes=VMEM_LIMIT),
        name="sample_proj",
    )(x, cos, sin, win_p, anorm, qg, kg, gbias)


def _sample_core_kernel(qbd_ref, ckt_ref, cvt_ref, kn_ref, vn_ref, knt_ref, vnt_ref, sink_ref,
                        qm_ref, km_ref, vm_ref, gp_ref, gr_ref, c_ref, n_ref, m_ref,
                        o_ref, h_ref, cn_ref, nn_ref, mn_ref, nkt_ref, nvt_ref):
    bb = qbd_ref.shape[0]
    nrow = qbd_ref.shape[1]
    tpad = kn_ref.shape[1]
    nreal = nrow // ATTN_HEADS
    qbd = qbd_ref[...].astype(bf16)
    ckt = ckt_ref[...]
    cvt = cvt_ref[...]
    zpad_k = jnp.zeros((bb, LANES - tpad, LANES), bf16)
    kn = jnp.concatenate([kn_ref[...].astype(bf16), zpad_k], axis=1)
    vn = jnp.concatenate([vn_ref[...].astype(bf16), zpad_k], axis=1)
    s = jnp.concatenate(
        [jnp.einsum('bqd,bdw->bqw', qbd, ckt.astype(bf16), preferred_element_type=f32),
         jnp.einsum('bqd,bkd->bqk', qbd, kn, preferred_element_type=f32)], axis=2)
    lane3 = lax.broadcasted_iota(jnp.int32, ckt.shape, 2)
    keep = lane3 < WINDOW - nreal
    nkt_ref[...] = jnp.where(keep, pltpu.roll(ckt, WINDOW - nreal, 2), knt_ref[...])
    nvt_ref[...] = jnp.where(keep, pltpu.roll(cvt, WINDOW - nreal, 2), vnt_ref[...])
    tq = lax.broadcasted_iota(jnp.int32, s.shape, 1) & (nreal - 1)
    kj = lax.broadcasted_iota(jnp.int32, s.shape, 2)
    valid = ((kj < WINDOW) & (kj > tq)) | ((kj >= WINDOW) & ((kj - WINDOW) <= tq))
    s = jnp.where(valid, s, NEG)
    sink = sink_ref[:, 0:1][None]
    mx = jnp.maximum(jnp.max(s, axis=-1, keepdims=True), sink)
    p = jnp.exp(s - mx)
    den = jnp.sum(p, axis=-1, keepdims=True) + jnp.exp(sink - mx)
    pb = p.astype(bf16)
    o = (jnp.einsum('bqw,bdw->bqd', pb[:, :, :WINDOW], cvt.astype(bf16), preferred_element_type=f32)
         + jnp.einsum('bqk,bkd->bqd', pb[:, :, WINDOW:], vn, preferred_element_type=f32))
    o_ref[...] = o / den

    q = qm_ref[...]
    k = km_ref[...]
    v = vm_ref[...]
    gp = gp_ref[...]
    gr = gr_ref[...]
    c0 = c_ref[...]
    n0 = n_ref[...]
    m0 = m_ref[...]
    ig_c = gp[:, :, 0:1]
    lf_c = gp[:, :, 1:2]
    ig_r = gr[:, 0:1, :]
    lf_r = gr[:, 1:2, :]
    ti = lax.broadcasted_iota(jnp.int32, q.shape, 1)
    si = lax.broadcasted_iota(jnp.int32, q.shape, 2)
    tri = si <= ti
    b_c = jnp.sum(jnp.where(tri, lf_r, 0.0), axis=2, keepdims=True)
    b_r = jnp.sum(jnp.where(ti <= si, lf_c, 0.0), axis=1, keepdims=True)
    dlog = jnp.where(tri, b_c - b_r + ig_r, NEG)
    inter = b_c + m0
    m_t = jnp.maximum(inter, jnp.max(dlog, axis=2, keepdims=True))
    dmat = jnp.exp(dlog - m_t)
    smat = jnp.zeros_like(q)
    for sx in range(nreal):
        col = jnp.sum(q * k[:, sx:sx + 1, :], axis=2, keepdims=True)
        smat = jnp.where(si == sx, col, smat)
    smat = smat * dmat
    a = jnp.exp(inter - m_t)
    qc = jnp.einsum('gtd,gde->gte', q.astype(bf16), c0.astype(bf16), preferred_element_type=f32)
    intra = jnp.zeros_like(q)
    for sx in range(nreal):
        intra = intra + smat[:, :, sx:sx + 1] * v[:, sx:sx + 1, :]
    num = a * qc + intra
    qn = jnp.sum(q * n0, axis=2, keepdims=True)
    den_m = a * qn + jnp.sum(smat, axis=2, keepdims=True)
    h_ref[...] = num / jnp.maximum(jnp.abs(den_m), jnp.exp(-m_t))
    last = nreal - 1
    m_new = m_t[:, last:last + 1, :]
    b_last = b_c[:, last:last + 1, :]
    a_end = jnp.exp(b_last + m0 - m_new)
    w_c = jnp.exp(b_last - b_c + ig_c - m_new)
    kw = k * w_c
    upd = jnp.einsum('gsd,gse->gde', kw.astype(bf16), v.astype(bf16), preferred_element_type=f32)
    cn_ref[...] = a_end * c0 + upd
    nn_ref[...] = a_end * n0 + jnp.sum(kw, axis=1, keepdims=True)
    mn_ref[...] = m_new


def _sample_core(qbd, ckt, cvt, kn, vn, knt, vnt, sink_rows, qm, km, vm, gp, gr, c0, n0, m0):
    nb = qbd.shape[0]
    bb = SAMPLE_BATCH_BLOCK
    gb = bb * MLSTM_HEADS
    ng = nb * MLSTM_HEADS
    nrow = qbd.shape[1]
    tpad = kn.shape[1]
    blk = lambda shape: pl.BlockSpec(shape, lambda i: (i,) + (0,) * (len(shape) - 1))
    return pl.pallas_call(
        _sample_core_kernel,
        grid=(nb // bb,),
        in_specs=[blk((bb, nrow, LANES)), blk((bb, KV_WIDTH, WINDOW)), blk((bb, KV_WIDTH, WINDOW)),
                  blk((bb, tpad, KV_WIDTH)), blk((bb, tpad, KV_WIDTH)),
                  blk((bb, KV_WIDTH, WINDOW)), blk((bb, KV_WIDTH, WINDOW)),
                  pl.BlockSpec((nrow, LANES), lambda i: (0, 0)),
                  blk((gb, tpad, MLSTM_HEAD_DIM)), blk((gb, tpad, MLSTM_HEAD_DIM)),
                  blk((gb, tpad, MLSTM_HEAD_DIM)), blk((gb, tpad, LANES)), blk((gb, tpad, LANES)),
                  blk((gb, MLSTM_HEAD_DIM, MLSTM_HEAD_DIM)), blk((gb, 1, MLSTM_HEAD_DIM)),
                  blk((gb, 1, 1))],
        out_specs=[blk((bb, nrow, LANES)), blk((gb, tpad, MLSTM_HEAD_DIM)),
                   blk((gb, MLSTM_HEAD_DIM, MLSTM_HEAD_DIM)), blk((gb, 1, MLSTM_HEAD_DIM)),
                   blk((gb, 1, 1)), blk((bb, KV_WIDTH, WINDOW)), blk((bb, KV_WIDTH, WINDOW))],
        out_shape=[jax.ShapeDtypeStruct((nb, nrow, LANES), f32),
                   jax.ShapeDtypeStruct((ng, tpad, MLSTM_HEAD_DIM), f32),
                   jax.ShapeDtypeStruct((ng, MLSTM_HEAD_DIM, MLSTM_HEAD_DIM), f32),
                   jax.ShapeDtypeStruct((ng, 1, MLSTM_HEAD_DIM), f32),
                   jax.ShapeDtypeStruct((ng, 1, 1), f32),
                   jax.ShapeDtypeStruct((nb, KV_WIDTH, WINDOW), f32),
                   jax.ShapeDtypeStruct((nb, KV_WIDTH, WINDOW), f32)],
        compiler_params=pltpu.CompilerParams(
            dimension_semantics=("arbitrary",), vmem_limit_bytes=VMEM_LIMIT),
        name="sample_core",
    )(qbd, ckt, cvt, kn, vn, knt, vnt, sink_rows, qm, km, vm, gp, gr, c0, n0, m0)


def _sample_out_kernel(x_ref, ao_ref, hm_ref, om_ref, mnorm_ref, wout_ref, x1_ref):
    parts = [ao_ref[...].astype(bf16)]
    for hd in range(MLSTM_HEADS):
        cs = slice(hd * MLSTM_HEAD_DIM, (hd + 1) * MLSTM_HEAD_DIM)
        hn = _rms(hm_ref[:, cs], mnorm_ref[hd:hd + 1, :])
        parts.append((hn * jax.nn.sigmoid(om_ref[:, cs])).astype(bf16))
    mix = jnp.concatenate(parts, axis=1)
    x1_ref[...] = x_ref[...] + jnp.dot(mix, wout_ref[...], preferred_element_type=f32)


def _sample_out(x, ao, hm, om, mnorm, wout_b):
    n = x.shape[0]
    full = lambda shape: pl.BlockSpec(shape, lambda i: (0,) * len(shape))
    return pl.pallas_call(
        _sample_out_kernel,
        grid=(1,),
        in_specs=[full((n, D_MODEL)), full((n, ATTN_WIDTH)), full((n, MLSTM_WIDTH)),
                  full((n, MLSTM_WIDTH)), full((MLSTM_HEADS, MLSTM_HEAD_DIM)),
                  full((MIX_WIDTH, D_MODEL))],
        out_specs=full((n, D_MODEL)),
        out_shape=jax.ShapeDtypeStruct((n, D_MODEL), f32),
        compiler_params=pltpu.CompilerParams(vmem_limit_bytes=VMEM_LIMIT),
        name="sample_out",
    )(x, ao, hm, om, mnorm, wout_b)


def _rope_angles(pos):
    half = ATTN_HEAD_DIM // 2
    inv = np.float32(ROPE_THETA) ** (-np.arange(half, dtype=np.float32) / np.float32(half))
    ang = (pos.astype(np.float32)[:, None] * inv[None, :]).astype(np.float32)
    return np.cos(ang).astype(np.float32), np.sin(ang).astype(np.float32)


def _rope_tables(pos):
    c, s = _rope_angles(pos)
    cos = np.tile(c, (1, LANES // QUARTER))
    sin = np.tile(np.concatenate([-s, s], axis=1), (1, LANES // ATTN_HEAD_DIM))
    return cos, sin


def _rope_tables_quarters(pos):
    c, s = _rope_angles(pos)
    return np.tile(c, (1, LANES // QUARTER)), np.concatenate([-s, -s, s, s], axis=1)


def _quarters(a):
    lo, hi = a[..., :QUARTER], a[..., QUARTER:]
    return jnp.concatenate([lo, lo, hi, hi], axis=-1)


def _prompt_attn_weights(w):
    d = w.shape[0]
    wq = w[:, COL_QA:COL_KA].reshape(d, ATTN_WIDTH // LANES, 2, 2, QUARTER)
    wq = wq.transpose(0, 1, 3, 2, 4).reshape(d, ATTN_WIDTH)
    wk = _quarters(w[:, COL_KA:COL_VA].reshape(d, KV_HEADS, ATTN_HEAD_DIM)).reshape(d, KV_HEADS * LANES)
    wv = w[:, COL_VA:COL_QM].reshape(d, KV_HEADS, 1, ATTN_HEAD_DIM)
    wv = jnp.broadcast_to(wv, (d, KV_HEADS, 2, ATTN_HEAD_DIM)).reshape(d, KV_HEADS * LANES)
    return jnp.concatenate([wq, wk, wv], axis=1)


def _to_head_major(a, nb, nt, nh, tpad, fill=0.0):
    d = a.shape[1] // nh
    a = a.reshape(nb, nt, nh, d).transpose(0, 2, 1, 3).reshape(nb * nh, nt, d)
    return jnp.pad(a, ((0, 0), (0, tpad - nt), (0, 0)), constant_values=fill)


def kernel(x_prompt, x_sample, cache_k, cache_v, state_C, state_n, state_m, attn_norm, w_in, q_norm,
           k_norm, attn_sinks, b_ig, b_fg, mlstm_norm, w_out, ffn_norm, w_gate, w_up, w_down):
    assert w_in.shape[0] == 1 and x_prompt.shape[0] == 1
    tp = x_prompt.shape[1]
    nb, nt = x_sample.shape[0], x_sample.shape[1]
    tpad = SUBLANES
    nh = MLSTM_HEADS

    w = w_in[0]
    wz = jnp.zeros((D_MODEL, FG_LANE - nh), f32)
    win_p = jnp.concatenate(
        [w[:, :COL_G + nh], wz, w[:, COL_G + nh:], jnp.zeros((D_MODEL, LANES - FG_LANE - nh), f32)],
        axis=1).astype(bf16)
    gbias = jnp.concatenate(
        [b_ig[0], jnp.zeros((FG_LANE - nh,), f32), b_fg[0], jnp.zeros((LANES - FG_LANE - nh,), f32)]
    ).reshape(1, LANES)
    wout_b = w_out[0].astype(bf16)
    wg_b = w_gate[0].astype(bf16)
    wu_b = w_up[0].astype(bf16)
    wd_b = w_down[0].astype(bf16)
    anorm = attn_norm[0].reshape(1, D_MODEL)
    fnorm = ffn_norm[0].reshape(1, D_MODEL)
    qg = jnp.tile(q_norm[0], LANES // ATTN_HEAD_DIM).reshape(1, LANES)
    kg = jnp.tile(k_norm[0], LANES // ATTN_HEAD_DIM).reshape(1, LANES)
    mnorm = mlstm_norm[0].reshape(nh, MLSTM_HEAD_DIM)
    sinks = attn_sinks[0]

    wa = _prompt_attn_weights(w).astype(bf16)
    wm = win_p[:, COL_QM:]
    ws = win_p[:, COL_KA:COL_QM]
    idx = np.arange(2 * LANES)
    same = (idx[:, None] // LANES == idx[None, :] // LANES) & (
        (idx[:, None] // QUARTER) % 2 == (idx[None, :] // QUARTER) % 2)
    bd = jnp.asarray(same, dtype=bf16)
    sink_rows_p = jnp.repeat(sinks.reshape(KV_HEADS, ATTN_GROUP), QBLOCK, axis=1)
    bias = jnp.where(jnp.arange(2 * QBLOCK)[None, None, :] == 0, sink_rows_p[:, :, None], NEG)
    qgq = _quarters(q_norm[0]).reshape(1, LANES)
    kgq = _quarters(k_norm[0]).reshape(1, LANES)
    pos_p = np.arange(tp, dtype=np.float32)
    cos_p, sin_p = _rope_tables_quarters(pos_p)
    cos_w, sin_w = _rope_tables(pos_p[tp - WINDOW:])
    x1_p, k_p, v_p, cext_p, m_p = _prompt_mixer(
        x_prompt[0], cos_p, sin_p, wa, wm, ws, wout_b, bd, bias, anorm, qgq, kgq, kg, cos_w, sin_w,
        gbias, mnorm)
    y_p = _ffn(x1_p, fnorm, wg_b, wu_b, wd_b)

    xs = x_sample.reshape(nb * nt, D_MODEL)
    cos_s, sin_s = _rope_tables(np.arange(nt, dtype=np.float32) + np.float32(PAST_LEN))
    cos_s = np.tile(cos_s, (nb, 1))
    sin_s = np.tile(sin_s, (nb, 1))
    q_s, k_s, v_s, zm_s, gate_s = _sample_proj(xs, cos_s, sin_s, win_p, anorm, qg, kg, gbias)

    qh = q_s.reshape(nb, nt, ATTN_HEADS, ATTN_HEAD_DIM).transpose(0, 2, 1, 3)
    half_rows = ATTN_GROUP * nt
    q0 = qh[:, :ATTN_GROUP].reshape(nb, half_rows, ATTN_HEAD_DIM)
    q1 = qh[:, ATTN_GROUP:].reshape(nb, half_rows, ATTN_HEAD_DIM)
    zq = jnp.zeros_like(q0)
    qbd = jnp.concatenate([jnp.concatenate([q0, zq], axis=2), jnp.concatenate([zq, q1], axis=2)], axis=1)
    kn = jnp.pad(k_s.reshape(nb, nt, KV_WIDTH), ((0, 0), (0, tpad - nt), (0, 0)))
    vn = jnp.pad(v_s.reshape(nb, nt, KV_WIDTH), ((0, 0), (0, tpad - nt), (0, 0)))
    sink_rows = jnp.broadcast_to(jnp.repeat(sinks, nt)[:, None], (ATTN_HEADS * nt, LANES))
    qm = _to_head_major(zm_s[:, 0:MLSTM_WIDTH], nb, nt, nh, tpad)
    km = _to_head_major(zm_s[:, MLSTM_WIDTH:2 * MLSTM_WIDTH], nb, nt, nh, tpad)
    vm = _to_head_major(zm_s[:, 2 * MLSTM_WIDTH:3 * MLSTM_WIDTH], nb, nt, nh, tpad)
    ig = gate_s[:, 0:nh].reshape(nb, nt, nh).transpose(0, 2, 1).reshape(nb * nh, nt)
    lf = gate_s[:, FG_LANE:FG_LANE + nh].reshape(nb, nt, nh).transpose(0, 2, 1).reshape(nb * nh, nt)
    ig_cp = jnp.pad(ig, ((0, 0), (0, tpad - nt)), constant_values=NEG)
    lf_cp = jnp.pad(lf, ((0, 0), (0, tpad - nt)))
    gp = jnp.pad(jnp.stack([ig_cp, lf_cp], axis=2), ((0, 0), (0, 0), (0, LANES - 2)))
    ig_rp = jnp.pad(ig, ((0, 0), (0, LANES - nt)), constant_values=NEG)
    lf_rp = jnp.pad(lf, ((0, 0), (0, LANES - nt)))
    gr = jnp.pad(jnp.stack([ig_rp, lf_rp], axis=1), ((0, 0), (0, tpad - 2), (0, 0)))
    ckt = cache_k[0].reshape(nb, WINDOW, KV_WIDTH).transpose(0, 2, 1)
    cvt = cache_v[0].reshape(nb, WINDOW, KV_WIDTH).transpose(0, 2, 1)
    slot = ((0, 0), (0, 0), (WINDOW - nt, 0))
    knt = jnp.pad(k_s.reshape(nb, nt, KV_WIDTH).transpose(0, 2, 1), slot)
    vnt = jnp.pad(v_s.reshape(nb, nt, KV_WIDTH).transpose(0, 2, 1), slot)
    c0 = state_C[0].reshape(nb * nh, MLSTM_HEAD_DIM, MLSTM_HEAD_DIM)
    n0 = state_n[0].reshape(nb * nh, 1, MLSTM_HEAD_DIM)
    m0 = state_m[0].reshape(nb * nh, 1, 1)

    o_bd, h_s, c_new, n_new, m_new, nkt, nvt = _sample_core(
        qbd, ckt, cvt, kn, vn, knt, vnt, sink_rows, qm, km, vm, gp, gr, c0, n0, m0)

    o0 = o_bd[:, :half_rows, :ATTN_HEAD_DIM].reshape(nb, ATTN_GROUP, nt, ATTN_HEAD_DIM)
    o1 = o_bd[:, half_rows:, ATTN_HEAD_DIM:].reshape(nb, ATTN_GROUP, nt, ATTN_HEAD_DIM)
    ao = jnp.concatenate([o0, o1], axis=1).transpose(0, 2, 1, 3).reshape(nb * nt, ATTN_WIDTH)
    hm = h_s[:, :nt].reshape(nb, nh, nt, MLSTM_HEAD_DIM).transpose(0, 2, 1, 3).reshape(nb * nt, MLSTM_WIDTH)
    om = zm_s[:, 3 * MLSTM_WIDTH:]
    x1_s = _sample_out(xs, ao, hm, om, mnorm, wout_b)
    y_s = _ffn(x1_s, fnorm, wg_b, wu_b, wd_b)

    new_k_s = nkt.transpose(0, 2, 1)
    new_v_s = nvt.transpose(0, 2, 1)

    kv_shape = (1, 1, WINDOW, KV_HEADS, ATTN_HEAD_DIM)
    return (
        y_p[None],
        y_s.reshape(nb, nt, D_MODEL),
        k_p.reshape(kv_shape),
        v_p.reshape(kv_shape),
        cext_p[None, None, :, :, :MLSTM_HEAD_DIM],
        cext_p[None, None, :, :, MLSTM_HEAD_DIM],
        m_p[None, None, :nh, 0],
        new_k_s.reshape(1, nb, WINDOW, KV_HEADS, ATTN_HEAD_DIM),
        new_v_s.reshape(1, nb, WINDOW, KV_HEADS, ATTN_HEAD_DIM),
        c_new.reshape(1, nb, nh, MLSTM_HEAD_DIM, MLSTM_HEAD_DIM),
        n_new.reshape(1, nb, nh, MLSTM_HEAD_DIM),
        m_new.reshape(1, nb, nh),
    )
```

```python
import jax
import jax.numpy as jnp
import numpy as np
from jax import lax
from jax.experimental import pallas as pl
from jax.experimental.pallas import tpu as pltpu

D_MODEL = 1024
PAST_LEN = 16384
ATTN_HEADS = 8
KV_HEADS = 2
ATTN_HEAD_DIM = 64
ATTN_GROUP = ATTN_HEADS // KV_HEADS
ATTN_WIDTH = ATTN_HEADS * ATTN_HEAD_DIM
KV_WIDTH = KV_HEADS * ATTN_HEAD_DIM
WINDOW = 128
ROPE_THETA = 10000.0
MLSTM_HEADS = 4
MLSTM_HEAD_DIM = 128
MLSTM_WIDTH = MLSTM_HEADS * MLSTM_HEAD_DIM
MIX_WIDTH = ATTN_WIDTH + MLSTM_WIDTH
D_FF = 2816
NORM_EPS = 1e-6

LANES = 128
SUBLANES = 8
VMEM_LIMIT = 56 * 1024 * 1024

COL_QA = 0
COL_KA = COL_QA + ATTN_WIDTH
COL_VA = COL_KA + KV_WIDTH
COL_QM = COL_VA + KV_WIDTH
COL_KM = COL_QM + MLSTM_WIDTH
COL_VM = COL_KM + MLSTM_WIDTH
COL_OM = COL_VM + MLSTM_WIDTH
COL_G = COL_OM + MLSTM_WIDTH
IN_PAD = COL_G + LANES
FG_LANE = SUBLANES

PROMPT_BLOCK = 256
QBLOCK = WINDOW
MCHUNK = 128
PROJ_CHUNK = 256
FFN_BLOCK = 512
FFN_CHUNK = 256
SAMPLE_BATCH_BLOCK = 16
NEG = -1e30

f32 = jnp.float32
bf16 = jnp.bfloat16


def _rms(x, gain):
    return x * lax.rsqrt(jnp.mean(x * x, axis=-1, keepdims=True) + NORM_EPS) * gain


def _segsum64(s, lane):
    for k in (1, 2, 4, 8, 16, 32):
        s = s + jnp.where((lane & k) != 0, pltpu.roll(s, k, 1), pltpu.roll(s, LANES - k, 1))
    return s


def _headnorm_rope(xs, gain, cos, sin_signed, lane):
    ss = _segsum64(xs * xs, lane)
    y = xs * lax.rsqrt(ss * (1.0 / ATTN_HEAD_DIM) + NORM_EPS) * gain
    partner = jnp.where((lane & 32) != 0, pltpu.roll(y, 32, 1), pltpu.roll(y, LANES - 32, 1))
    return y * cos + partner * sin_signed


def _in_proj(x, anorm, win_ref):
    h = _rms(x, anorm).astype(bf16)
    return jnp.dot(h, win_ref[...], preferred_element_type=f32)


PA_Q = 0
PA_K = PA_Q + ATTN_WIDTH
PA_V = PA_K + KV_HEADS * LANES
PA_WIDTH = PA_V + KV_HEADS * LANES
PM_Q = 0
PM_K = PM_Q + MLSTM_WIDTH
PM_V = PM_K + MLSTM_WIDTH
PM_O = PM_V + MLSTM_WIDTH
PM_G = PM_O + MLSTM_WIDTH
PM_WIDTH = PM_G + LANES
QUARTER = ATTN_HEAD_DIM // 2


def _group_sumsq(xs, bd_ref):
    x2 = xs * xs
    hi = x2.astype(bf16)
    lo = (x2 - hi.astype(f32)).astype(bf16)
    return (jnp.dot(hi, bd_ref[...], preferred_element_type=f32)
            + jnp.dot(lo, bd_ref[...], preferred_element_type=f32))


def _norm_rope_quarters(xs, ss, gain, cos, sin_signed):
    y = xs * lax.rsqrt(ss * (1.0 / ATTN_HEAD_DIM) + NORM_EPS) * gain
    return y * cos + pltpu.roll(y, LANES // 2, 1) * sin_signed


def _col_chunks(width):
    return [(c, min(c + PROJ_CHUNK, width)) for c in range(0, width, PROJ_CHUNK)]


def _mixer_inproj_jobs(x_ref, anorm_ref, wa_ref, wm_ref, za_ref, zm_ref):
    h = _rms(x_ref[...], anorm_ref[...]).astype(bf16)

    def job(w_ref, z_ref, c0, c1):
        def run():
            z_ref[:, c0:c1] = jnp.dot(h, w_ref[:, c0:c1], preferred_element_type=f32)
        return run

    return ([job(wa_ref, za_ref, c0, c1) for c0, c1 in _col_chunks(PA_WIDTH)]
            + [job(wm_ref, zm_ref, c0, c1) for c0, c1 in _col_chunks(PM_WIDTH)])


def _mixer_outproj_jobs(xs_ref, mix_ref, wout_ref, x1_ref):
    def job(c0, c1):
        def run():
            x1_ref[:, c0:c1] = xs_ref[:, c0:c1] + jnp.dot(mix_ref[...], wout_ref[:, c0:c1],
                                                          preferred_element_type=f32)
        return run

    return [job(c0, c1) for c0, c1 in _col_chunks(D_MODEL)]


def _mixer_window_out(xs_ref, anorm_ref, ws_ref, kgs_ref, coss_ref, sins_ref, kout_ref, vout_ref):
    tb = xs_ref.shape[0]
    h = _rms(xs_ref[tb - WINDOW:, :], anorm_ref[...]).astype(bf16)
    zs = jnp.dot(h, ws_ref[...], preferred_element_type=f32)
    lane_s = lax.broadcasted_iota(jnp.int32, (WINDOW, LANES), 1)
    kout_ref[...] = _headnorm_rope(zs[:, :KV_WIDTH], kgs_ref[...], coss_ref[...], sins_ref[...], lane_s)
    vout_ref[...] = zs[:, KV_WIDTH:]


def _mixer_core(first_block, last_block, fillers, za_ref, zm_ref, mix_ref, cos_ref, sin_ref, bd_ref,
                bias_ref, qg_ref, kg_ref, gbias_ref, mnorm_ref, kprev_ref, vprev_ref, cst_ref, mst_ref,
                cext_ref, mout_ref):
    tb = za_ref.shape[0]
    fillers = list(fillers)
    n_fill = len(fillers)
    slots = 1 + tb // QBLOCK + (tb // QBLOCK) * KV_HEADS + (tb // MCHUNK) * MLSTM_HEADS
    progress = [0]

    def fill():
        progress[0] += 1
        while n_fill - len(fillers) < min(n_fill, -(-n_fill * progress[0] // slots)):
            fillers.pop(0)()

    lane_t = lax.broadcasted_iota(jnp.int32, (tb, LANES), 1)
    gcol = zm_ref[:, PM_G:PM_G + LANES] + gbias_ref[...]
    acol = jnp.where(lane_t < FG_LANE, gcol, jax.nn.log_sigmoid(gcol))
    arow = acol.T
    lane8 = lax.broadcasted_iota(jnp.int32, (SUBLANES, LANES), 1)
    lane_in = lane8 & (MCHUNK - 1)
    m_prev = mst_ref[:, 0:1]
    stacks = []
    u_rows = []
    w_rows = []
    for sb in range(tb // LANES):
        ls = slice(sb * LANES, (sb + 1) * LANES)
        ig8 = arow[0:SUBLANES, ls]
        lf8 = arow[FG_LANE:FG_LANE + SUBLANES, ls]
        b8 = lf8
        k = 1
        while k < MCHUNK:
            b8 = b8 + jnp.where(lane_in >= k, pltpu.roll(b8, k, 1), 0.0)
            k *= 2
        u8 = ig8 - b8
        cm8 = u8
        k = 1
        while k < MCHUNK:
            cm8 = jnp.maximum(cm8, jnp.where(lane_in >= k, pltpu.roll(cm8, k, 1), NEG))
            k *= 2
        g8 = jnp.zeros_like(u8)
        mp8 = jnp.zeros_like(u8)
        gl8 = jnp.zeros_like(u8)
        for c in range(LANES // MCHUNK):
            in_chunk = (lane8 // MCHUNK) == c
            gc = jnp.maximum(cm8, m_prev)
            last = c * MCHUNK + MCHUNK - 1
            g_last = jnp.max(jnp.where(lane8 == last, gc, NEG), axis=1, keepdims=True)
            b_last = jnp.max(jnp.where(lane8 == last, b8, NEG), axis=1, keepdims=True)
            g8 = jnp.where(in_chunk, gc, g8)
            mp8 = jnp.where(in_chunk, m_prev, mp8)
            gl8 = jnp.where(in_chunk, g_last, gl8)
            m_prev = b_last + g_last
        a8 = jnp.exp(mp8 - g8)
        emt8 = jnp.exp(-(b8 + g8))
        aend8 = jnp.exp(mp8 - gl8)
        stacks.append(jnp.concatenate(
            [g8, a8, emt8, aend8, jnp.zeros((LANES - 4 * SUBLANES, LANES), f32)], axis=0))
        u_rows.append(u8)
        w_rows.append(jnp.exp(u8 - gl8))
    mst_ref[...] = jnp.broadcast_to(m_prev, mst_ref.shape)
    if last_block:
        mout_ref[...] = jnp.broadcast_to(m_prev, mout_ref.shape)
    colform = jnp.concatenate(stacks, axis=1).T
    fill()

    lane = lax.broadcasted_iota(jnp.int32, (QBLOCK, LANES), 1)
    head_a = ((lane // QUARTER) & 1) == 0
    low_half = lane < ATTN_HEAD_DIM
    qi = lax.broadcasted_iota(jnp.int32, (ATTN_GROUP * QBLOCK, 2 * QBLOCK), 0) & (QBLOCK - 1)
    kj = lax.broadcasted_iota(jnp.int32, (ATTN_GROUP * QBLOCK, 2 * QBLOCK), 1)
    band = (kj > qi) & (kj <= qi + QBLOCK)
    row0 = lax.broadcasted_iota(jnp.int32, (QBLOCK, LANES), 0) == 0
    ones_slab = jnp.ones((2 * QBLOCK, LANES), bf16)

    def attn_prep(qb):
        rows = slice(qb * QBLOCK, (qb + 1) * QBLOCK)
        cos = cos_ref[rows, :]
        sin = sin_ref[rows, :]
        ss = [_group_sumsq(za_ref[rows, d * 2 * LANES:(d + 1) * 2 * LANES], bd_ref)
              for d in range(PA_V // (2 * LANES))]
        slabs = []
        for j in range(PA_V // LANES):
            xs = za_ref[rows, j * LANES:(j + 1) * LANES]
            gain = qg_ref[...] if j < ATTN_WIDTH // LANES else kg_ref[...]
            slabs.append(_norm_rope_quarters(xs, ss[j // 2][:, (j % 2) * LANES:(j % 2 + 1) * LANES],
                                             gain, cos, sin))
        return slabs

    def attn_unit(qb, c, slabs, kprev, vprev):
        rows = slice(qb * QBLOCK, (qb + 1) * QBLOCK)
        kcur = slabs[ATTN_WIDTH // LANES + c].astype(bf16)
        vcur = za_ref[rows, PA_V + c * LANES:PA_V + (c + 1) * LANES].astype(bf16)
        kcat = jnp.concatenate([kprev, kcur], axis=0)
        vext = jnp.concatenate([jnp.concatenate([vprev, vcur], axis=0), ones_slab], axis=1)
        q_heads = []
        for j in (2 * c, 2 * c + 1):
            qs = slabs[j] * (ATTN_HEAD_DIM ** -0.5)
            q_heads.append(jnp.where(head_a, qs, 0.0).astype(bf16))
            q_heads.append(jnp.where(head_a, 0.0, qs).astype(bf16))
        qst = jnp.concatenate(q_heads, axis=0)
        s = lax.dot_general(qst, kcat, (((1,), (1,)), ((), ())), preferred_element_type=f32)
        valid = band & (kj >= QBLOCK) if (first_block and qb == 0) else band
        s = jnp.where(valid, s, bias_ref[c])
        p = jnp.exp(s - jnp.max(s, axis=-1, keepdims=True)).astype(bf16)
        of = jnp.dot(p, vext, preferred_element_type=f32)
        o = of[:, :LANES] / of[:, LANES:]
        for jj in range(2):
            pair = jnp.where(low_half, o[(2 * jj) * QBLOCK:(2 * jj + 1) * QBLOCK],
                             o[(2 * jj + 1) * QBLOCK:(2 * jj + 2) * QBLOCK])
            col = (2 * c + jj) * LANES
            mix_ref[rows, col:col + LANES] = pair.astype(bf16)
        return kcur, jnp.where(row0, jnp.zeros_like(vcur), vcur)

    ti = lax.broadcasted_iota(jnp.int32, (MCHUNK, MCHUNK), 0)
    si = lax.broadcasted_iota(jnp.int32, (MCHUNK, MCHUNK), 1)
    causal = si <= ti
    ones_l = jnp.ones((MCHUNK, LANES), bf16)

    def mlstm_unit(hd, c, cext):
        hcols = lambda base: slice(base + hd * MLSTM_HEAD_DIM, base + (hd + 1) * MLSTM_HEAD_DIM)
        rows = slice(c * MCHUNK, (c + 1) * MCHUNK)
        qb_ = zm_ref[rows, hcols(PM_Q)].astype(bf16)
        kf = zm_ref[rows, hcols(PM_K)] * (MLSTM_HEAD_DIM ** -0.5)
        vb = zm_ref[rows, hcols(PM_V)].astype(bf16)
        og = zm_ref[rows, hcols(PM_O)]
        g_c = colform[rows, hd:hd + 1]
        a_c = colform[rows, SUBLANES + hd:SUBLANES + hd + 1]
        emt_c = colform[rows, 2 * SUBLANES + hd:2 * SUBLANES + hd + 1]
        aend = colform[c * MCHUNK:c * MCHUNK + 1, 3 * SUBLANES + hd:3 * SUBLANES + hd + 1]
        sb, off = divmod(c * MCHUNK, LANES)
        u_r = u_rows[sb][hd:hd + 1, off:off + MCHUNK]
        w_r = w_rows[sb][hd:hd + 1, off:off + MCHUNK]
        dmat = jnp.exp(jnp.where(causal, u_r - g_c, NEG))
        smat = lax.dot_general(qb_, kf.astype(bf16), (((1,), (1,)), ((), ())),
                               preferred_element_type=f32) * dmat
        vext = jnp.concatenate([vb, ones_l], axis=1)
        nd = (a_c * jnp.dot(qb_, cext.astype(bf16), preferred_element_type=f32)
              + jnp.dot(smat.astype(bf16), vext, preferred_element_type=f32))
        hraw = nd[:, :MLSTM_HEAD_DIM] / jnp.maximum(jnp.abs(nd[:, MLSTM_HEAD_DIM:]), emt_c)
        hn = _rms(hraw, mnorm_ref[hd:hd + 1, :])
        mix_ref[rows, ATTN_WIDTH + hd * MLSTM_HEAD_DIM:ATTN_WIDTH + (hd + 1) * MLSTM_HEAD_DIM] = (
            (hn * jax.nn.sigmoid(og)).astype(bf16))
        kw_t = (kf.T * w_r).astype(bf16)
        return aend * cext + jnp.dot(kw_t, vext, preferred_element_type=f32)

    nqb = tb // QBLOCK
    nch = tb // MCHUNK
    preps = []
    for qb in range(nqb):
        preps.append(attn_prep(qb))
        fill()
    kv_state = [(kprev_ref[c], vprev_ref[c]) for c in range(KV_HEADS)]
    cexts = [cst_ref[hd] for hd in range(MLSTM_HEADS)]
    attn_jobs = [(qb, c) for qb in range(nqb) for c in range(KV_HEADS)]
    mlstm_jobs = [(hd, c) for c in range(nch) for hd in range(MLSTM_HEADS)]
    while attn_jobs or mlstm_jobs:
        if attn_jobs:
            qb, c = attn_jobs.pop(0)
            kv_state[c] = attn_unit(qb, c, preps[qb], *kv_state[c])
            fill()
        for _ in range(2 if len(mlstm_jobs) > 2 * len(attn_jobs) else 1):
            if mlstm_jobs:
                hd, c = mlstm_jobs.pop(0)
                cexts[hd] = mlstm_unit(hd, c, cexts[hd])
                fill()
    assert not fillers
    for c in range(KV_HEADS):
        kprev_ref[c], vprev_ref[c] = kv_state[c]
    for hd in range(MLSTM_HEADS):
        cst_ref[hd] = cexts[hd]
        if last_block:
            cext_ref[hd] = cexts[hd]


def _prompt_mixer_kernel(x_ref, cos_ref, sin_ref, wa_ref, wm_ref, ws_ref, wout_ref, bd_ref, bias_ref,
                         anorm_ref, qg_ref, kg_ref, kgs_ref, coss_ref, sins_ref, gbias_ref, mnorm_ref,
                         x1_ref, kout_ref, vout_ref, cext_ref, mout_ref,
                         za0, za1, zm0, zm1, mix0, mix1, xs0, xs1, kprev_ref, vprev_ref, cst_ref, mst_ref):
    step = pl.program_id(0)
    nblk = pl.num_programs(0) - 2
    za, zm, mix, xs = (za0, za1), (zm0, zm1), (mix0, mix1), (xs0, xs1)

    def run(parity, do_in, do_core, do_out, first_block=False, last_block=False):
        jobs = []
        if do_out:
            jobs += _mixer_outproj_jobs(xs[parity], mix[parity], wout_ref, x1_ref)
        if do_in:
            jobs += _mixer_inproj_jobs(x_ref, anorm_ref, wa_ref, wm_ref, za[parity], zm[parity])
        if do_core:
            _mixer_core(first_block, last_block, jobs, za[1 - parity], zm[1 - parity], mix[1 - parity],
                        cos_ref, sin_ref, bd_ref, bias_ref, qg_ref, kg_ref, gbias_ref, mnorm_ref,
                        kprev_ref, vprev_ref, cst_ref, mst_ref, cext_ref, mout_ref)
        else:
            for job in jobs:
                job()
        if last_block:
            _mixer_window_out(xs[1 - parity], anorm_ref, ws_ref, kgs_ref, coss_ref, sins_ref,
                              kout_ref, vout_ref)
        if do_in:
            xs[parity][...] = x_ref[...]

    @pl.when(step == 0)
    def _first():
        kprev_ref[...] = jnp.zeros_like(kprev_ref)
        vprev_ref[...] = jnp.zeros_like(vprev_ref)
        cst_ref[...] = jnp.zeros_like(cst_ref)
        mst_ref[...] = jnp.zeros_like(mst_ref)
        run(0, True, False, False)

    @pl.when(step == 1)
    def _second():
        run(1, True, True, False, first_block=True)

    steady = (step >= 2) & (step < nblk)

    @pl.when(steady & (step % 2 == 0))
    def _even():
        run(0, True, True, True)

    @pl.when(steady & (step % 2 == 1))
    def _odd():
        run(1, True, True, True)

    @pl.when(step == nblk)
    def _drain_core():
        run(0, False, True, True, last_block=True)

    @pl.when(step == nblk + 1)
    def _drain_out():
        run(1, False, False, True)


def _const_spec(shape):
    nd = len(shape)
    return pl.BlockSpec(shape, lambda i, *_: (0,) * nd)


def _prompt_mixer(x, cos, sin, wa, wm, ws, wout_b, bd, bias, anorm, qg, kg, kgs, coss, sins, gbias, mnorm):
    t = x.shape[0]
    tb = PROMPT_BLOCK
    nblk = t // tb
    assert nblk % 2 == 0 and nblk >= 4
    state_shape = (MLSTM_HEADS, MLSTM_HEAD_DIM, 2 * MLSTM_HEAD_DIM)
    last = nblk - 1
    return pl.pallas_call(
        _prompt_mixer_kernel,
        grid=(nblk + 2,),
        in_specs=[
            pl.BlockSpec((tb, D_MODEL), lambda i: (jnp.minimum(i, last), 0)),
            pl.BlockSpec((tb, LANES), lambda i: (jnp.clip(i - 1, 0, last), 0)),
            pl.BlockSpec((tb, LANES), lambda i: (jnp.clip(i - 1, 0, last), 0)),
            _const_spec((D_MODEL, PA_WIDTH)),
            _const_spec((D_MODEL, PM_WIDTH)),
            _const_spec((D_MODEL, 2 * KV_WIDTH)),
            _const_spec((MIX_WIDTH, D_MODEL)),
            _const_spec((2 * LANES, 2 * LANES)),
            _const_spec((KV_HEADS, ATTN_GROUP * QBLOCK, 2 * QBLOCK)),
            _const_spec((1, D_MODEL)),
            _const_spec((1, LANES)),
            _const_spec((1, LANES)),
            _const_spec((1, LANES)),
            _const_spec((WINDOW, LANES)),
            _const_spec((WINDOW, LANES)),
            _const_spec((1, LANES)),
            _const_spec((MLSTM_HEADS, MLSTM_HEAD_DIM)),
        ],
        out_specs=[
            pl.BlockSpec((tb, D_MODEL), lambda i: (jnp.clip(i - 2, 0, last), 0)),
            _const_spec((WINDOW, KV_WIDTH)),
            _const_spec((WINDOW, KV_WIDTH)),
            _const_spec(state_shape),
            _const_spec((SUBLANES, LANES)),
        ],
        out_shape=[
            jax.ShapeDtypeStruct((t, D_MODEL), f32),
            jax.ShapeDtypeStruct((WINDOW, KV_WIDTH), f32),
            jax.ShapeDtypeStruct((WINDOW, KV_WIDTH), f32),
            jax.ShapeDtypeStruct(state_shape, f32),
            jax.ShapeDtypeStruct((SUBLANES, LANES), f32),
        ],
        scratch_shapes=(
            [pltpu.VMEM((tb, PA_WIDTH), f32)] * 2 + [pltpu.VMEM((tb, PM_WIDTH), f32)] * 2
            + [pltpu.VMEM((tb, MIX_WIDTH), bf16)] * 2 + [pltpu.VMEM((tb, D_MODEL), f32)] * 2
            + [pltpu.VMEM((KV_HEADS, WINDOW, LANES), bf16)] * 2
            + [pltpu.VMEM(state_shape, f32), pltpu.VMEM((SUBLANES, LANES), f32)]),
        compiler_params=pltpu.CompilerParams(
            dimension_semantics=("arbitrary",), vmem_limit_bytes=VMEM_LIMIT),
        name="prompt_mixer",
    )(x, cos, sin, wa, wm, ws, wout_b, bd, bias, anorm, qg, kg, kgs, coss, sins, gbias, mnorm)


def _ffn_kernel(x_ref, g_ref, wg_ref, wu_ref, wd_ref, o_ref):
    x = x_ref[...]
    hf = _rms(x, g_ref[...]).astype(bf16)
    acc = x
    for c in range(D_FF // FFN_CHUNK):
        cs = slice(c * FFN_CHUNK, (c + 1) * FFN_CHUNK)
        gate = jnp.dot(hf, wg_ref[:, cs], preferred_element_type=f32)
        up = jnp.dot(hf, wu_ref[:, cs], preferred_element_type=f32)
        act = (gate * jax.nn.sigmoid(gate) * up).astype(bf16)
        acc = acc + jnp.dot(act, wd_ref[cs, :], preferred_element_type=f32)
    o_ref[...] = acc


def _ffn(x, fnorm, wg_b, wu_b, wd_b):
    n = x.shape[0]
    tm = FFN_BLOCK
    return pl.pallas_call(
        _ffn_kernel,
        grid=(n // tm,),
        in_specs=[
            pl.BlockSpec((tm, D_MODEL), lambda i: (i, 0)),
            _const_spec((1, D_MODEL)),
            _const_spec((D_MODEL, D_FF)),
            _const_spec((D_MODEL, D_FF)),
            _const_spec((D_FF, D_MODEL)),
        ],
        out_specs=pl.BlockSpec((tm, D_MODEL), lambda i: (i, 0)),
        out_shape=jax.ShapeDtypeStruct((n, D_MODEL), f32),
        compiler_params=pltpu.CompilerParams(
            dimension_semantics=("arbitrary",), vmem_limit_bytes=VMEM_LIMIT),
        name="ffn",
    )(x, fnorm, wg_b, wu_b, wd_b)


def _sample_proj_kernel(x_ref, cos_ref, sin_ref, win_ref, anorm_ref, qg_ref, kg_ref, gbias_ref,
                        q_ref, k_ref, v_ref, zm_ref, gate_ref):
    n = x_ref.shape[0]
    z = _in_proj(x_ref[...], anorm_ref[...], win_ref)
    lane = lax.broadcasted_iota(jnp.int32, (n, LANES), 1)
    cos = cos_ref[...]
    sin = sin_ref[...]
    for j in range(ATTN_WIDTH // LANES):
        qs = _headnorm_rope(z[:, COL_QA + j * LANES:COL_QA + (j + 1) * LANES], qg_ref[...], cos, sin, lane)
        q_ref[:, j * LANES:(j + 1) * LANES] = qs * (ATTN_HEAD_DIM ** -0.5)
    k_ref[...] = _headnorm_rope(z[:, COL_KA:COL_KA + KV_WIDTH], kg_ref[...], cos, sin, lane)
    v_ref[...] = z[:, COL_VA:COL_VA + KV_WIDTH]
    zm_ref[:, 0:MLSTM_WIDTH] = z[:, COL_QM:COL_KM]
    zm_ref[:, MLSTM_WIDTH:2 * MLSTM_WIDTH] = z[:, COL_KM:COL_VM] * (MLSTM_HEAD_DIM ** -0.5)
    zm_ref[:, 2 * MLSTM_WIDTH:4 * MLSTM_WIDTH] = z[:, COL_VM:COL_G]
    gcol = z[:, COL_G:COL_G + LANES] + gbias_ref[...]
    gate_ref[...] = jnp.where(lane < FG_LANE, gcol, jax.nn.log_sigmoid(gcol))


def _sample_proj(x, cos, sin, win_p, anorm, qg, kg, gbias):
    n = x.shape[0]
    full = lambda shape: pl.BlockSpec(shape, lambda i: (0,) * len(shape))
    return pl.pallas_call(
        _sample_proj_kernel,
        grid=(1,),
        in_specs=[full((n, D_MODEL)), full((n, LANES)), full((n, LANES)), full((D_MODEL, IN_PAD)),
                  full((1, D_MODEL)), full((1, LANES)), full((1, LANES)), full((1, LANES))],
        out_specs=[full((n, ATTN_WIDTH)), full((n, KV_WIDTH)), full((n, KV_WIDTH)),
                   full((n, 4 * MLSTM_WIDTH)), full((n, LANES))],
        out_shape=[jax.ShapeDtypeStruct((n, ATTN_WIDTH), f32),
                   jax.ShapeDtypeStruct((n, KV_WIDTH), f32),
                   jax.ShapeDtypeStruct((n, KV_WIDTH), f32),
                   jax.ShapeDtypeStruct((n, 4 * MLSTM_WIDTH), f32),
                   jax.ShapeDtypeStruct((n, LANES), f32)],
        compiler_params=pltpu.CompilerParams(vmem_limit_bytes=VMEM_LIMIT),
        name="sample_proj",
    )(x, cos, sin, win_p, anorm, qg, kg, gbias)


def _sample_core_kernel(qbd_ref, ckt_ref, cvt_ref, kn_ref, vn_ref, knt_ref, vnt_ref, sink_ref,
                        qm_ref, km_ref, vm_ref, gp_ref, gr_ref, c_ref, n_ref, m_ref,
                        o_ref, h_ref, cn_ref, nn_ref, mn_ref, nkt_ref, nvt_ref):
    bb = qbd_ref.shape[0]
    nrow = qbd_ref.shape[1]
    tpad = kn_ref.shape[1]
    nreal = nrow // ATTN_HEADS
    qbd = qbd_ref[...].astype(bf16)
    ckt = ckt_ref[...]
    cvt = cvt_ref[...]
    zpad_k = jnp.zeros((bb, LANES - tpad, LANES), bf16)
    kn = jnp.concatenate([kn_ref[...].astype(bf16), zpad_k], axis=1)
    vn = jnp.concatenate([vn_ref[...].astype(bf16), zpad_k], axis=1)
    s = jnp.concatenate(
        [jnp.einsum('bqd,bdw->bqw', qbd, ckt.astype(bf16), preferred_element_type=f32),
         jnp.einsum('bqd,bkd->bqk', qbd, kn, preferred_element_type=f32)], axis=2)
    lane3 = lax.broadcasted_iota(jnp.int32, ckt.shape, 2)
    keep = lane3 < WINDOW - nreal
    nkt_ref[...] = jnp.where(keep, pltpu.roll(ckt, WINDOW - nreal, 2), knt_ref[...])
    nvt_ref[...] = jnp.where(keep, pltpu.roll(cvt, WINDOW - nreal, 2), vnt_ref[...])
    tq = lax.broadcasted_iota(jnp.int32, s.shape, 1) & (nreal - 1)
    kj = lax.broadcasted_iota(jnp.int32, s.shape, 2)
    valid = ((kj < WINDOW) & (kj > tq)) | ((kj >= WINDOW) & ((kj - WINDOW) <= tq))
    s = jnp.where(valid, s, NEG)
    sink = sink_ref[:, 0:1][None]
    mx = jnp.maximum(jnp.max(s, axis=-1, keepdims=True), sink)
    p = jnp.exp(s - mx)
    den = jnp.sum(p, axis=-1, keepdims=True) + jnp.exp(sink - mx)
    pb = p.astype(bf16)
    o = (jnp.einsum('bqw,bdw->bqd', pb[:, :, :WINDOW], cvt.astype(bf16), preferred_element_type=f32)
         + jnp.einsum('bqk,bkd->bqd', pb[:, :, WINDOW:], vn, preferred_element_type=f32))
    o_ref[...] = o / den

    q = qm_ref[...]
    k = km_ref[...]
    v = vm_ref[...]
    gp = gp_ref[...]
    gr = gr_ref[...]
    c0 = c_ref[...]
    n0 = n_ref[...]
    m0 = m_ref[...]
    ig_c = gp[:, :, 0:1]
    lf_c = gp[:, :, 1:2]
    ig_r = gr[:, 0:1, :]
    lf_r = gr[:, 1:2, :]
    ti = lax.broadcasted_iota(jnp.int32, q.shape, 1)
    si = lax.broadcasted_iota(jnp.int32, q.shape, 2)
    tri = si <= ti
    b_c = jnp.sum(jnp.where(tri, lf_r, 0.0), axis=2, keepdims=True)
    b_r = jnp.sum(jnp.where(ti <= si, lf_c, 0.0), axis=1, keepdims=True)
    dlog = jnp.where(tri, b_c - b_r + ig_r, NEG)
    inter = b_c + m0
    m_t = jnp.maximum(inter, jnp.max(dlog, axis=2, keepdims=True))
    dmat = jnp.exp(dlog - m_t)
    smat = jnp.zeros_like(q)
    for sx in range(nreal):
        col = jnp.sum(q * k[:, sx:sx + 1, :], axis=2, keepdims=True)
        smat = jnp.where(si == sx, col, smat)
    smat = smat * dmat
    a = jnp.exp(inter - m_t)
    qc = jnp.einsum('gtd,gde->gte', q.astype(bf16), c0.astype(bf16), preferred_element_type=f32)
    intra = jnp.zeros_like(q)
    for sx in range(nreal):
        intra = intra + smat[:, :, sx:sx + 1] * v[:, sx:sx + 1, :]
    num = a * qc + intra
    qn = jnp.sum(q * n0, axis=2, keepdims=True)
    den_m = a * qn + jnp.sum(smat, axis=2, keepdims=True)
    h_ref[...] = num / jnp.maximum(jnp.abs(den_m), jnp.exp(-m_t))
    last = nreal - 1
    m_new = m_t[:, last:last + 1, :]
    b_last = b_c[:, last:last + 1, :]
    a_end = jnp.exp(b_last + m0 - m_new)
    w_c = jnp.exp(b_last - b_c + ig_c - m_new)
    kw = k * w_c
    upd = jnp.einsum('gsd,gse->gde', kw.astype(bf16), v.astype(bf16), preferred_element_type=f32)
    cn_ref[...] = a_end * c0 + upd
    nn_ref[...] = a_end * n0 + jnp.sum(kw, axis=1, keepdims=True)
    mn_ref[...] = m_new


def _sample_core(qbd, ckt, cvt, kn, vn, knt, vnt, sink_rows, qm, km, vm, gp, gr, c0, n0, m0):
    nb = qbd.shape[0]
    bb = SAMPLE_BATCH_BLOCK
    gb = bb * MLSTM_HEADS
    ng = nb * MLSTM_HEADS
    nrow = qbd.shape[1]
    tpad = kn.shape[1]
    blk = lambda shape: pl.BlockSpec(shape, lambda i: (i,) + (0,) * (len(shape) - 1))
    return pl.pallas_call(
        _sample_core_kernel,
        grid=(nb // bb,),
        in_specs=[blk((bb, nrow, LANES)), blk((bb, KV_WIDTH, WINDOW)), blk((bb, KV_WIDTH, WINDOW)),
                  blk((bb, tpad, KV_WIDTH)), blk((bb, tpad, KV_WIDTH)),
                  blk((bb, KV_WIDTH, WINDOW)), blk((bb, KV_WIDTH, WINDOW)),
                  pl.BlockSpec((nrow, LANES), lambda i: (0, 0)),
                  blk((gb, tpad, MLSTM_HEAD_DIM)), blk((gb, tpad, MLSTM_HEAD_DIM)),
                  blk((gb, tpad, MLSTM_HEAD_DIM)), blk((gb, tpad, LANES)), blk((gb, tpad, LANES)),
                  blk((gb, MLSTM_HEAD_DIM, MLSTM_HEAD_DIM)), blk((gb, 1, MLSTM_HEAD_DIM)),
                  blk((gb, 1, 1))],
        out_specs=[blk((bb, nrow, LANES)), blk((gb, tpad, MLSTM_HEAD_DIM)),
                   blk((gb, MLSTM_HEAD_DIM, MLSTM_HEAD_DIM)), blk((gb, 1, MLSTM_HEAD_DIM)),
                   blk((gb, 1, 1)), blk((bb, KV_WIDTH, WINDOW)), blk((bb, KV_WIDTH, WINDOW))],
        out_shape=[jax.ShapeDtypeStruct((nb, nrow, LANES), f32),
                   jax.ShapeDtypeStruct((ng, tpad, MLSTM_HEAD_DIM), f32),
                   jax.ShapeDtypeStruct((ng, MLSTM_HEAD_DIM, MLSTM_HEAD_DIM), f32),
                   jax.ShapeDtypeStruct((ng, 1, MLSTM_HEAD_DIM), f32),
                   jax.ShapeDtypeStruct((ng, 1, 1), f32),
                   jax.ShapeDtypeStruct((nb, KV_WIDTH, WINDOW), f32),
                   jax.ShapeDtypeStruct((nb, KV_WIDTH, WINDOW), f32)],
        compiler_params=pltpu.CompilerParams(
            dimension_semantics=("arbitrary",), vmem_limit_bytes=VMEM_LIMIT),
        name="sample_core",
    )(qbd, ckt, cvt, kn, vn, knt, vnt, sink_rows, qm, km, vm, gp, gr, c0, n0, m0)


def _sample_out_kernel(x_ref, ao_ref, hm_ref, om_ref, mnorm_ref, wout_ref, x1_ref):
    parts = [ao_ref[...].astype(bf16)]
    for hd in range(MLSTM_HEADS):
        cs = slice(hd * MLSTM_HEAD_DIM, (hd + 1) * MLSTM_HEAD_DIM)
        hn = _rms(hm_ref[:, cs], mnorm_ref[hd:hd + 1, :])
        parts.append((hn * jax.nn.sigmoid(om_ref[:, cs])).astype(bf16))
    mix = jnp.concatenate(parts, axis=1)
    x1_ref[...] = x_ref[...] + jnp.dot(mix, wout_ref[...], preferred_element_type=f32)


def _sample_out(x, ao, hm, om, mnorm, wout_b):
    n = x.shape[0]
    full = lambda shape: pl.BlockSpec(shape, lambda i: (0,) * len(shape))
    return pl.pallas_call(
        _sample_out_kernel,
        grid=(1,),
        in_specs=[full((n, D_MODEL)), full((n, ATTN_WIDTH)), full((n, MLSTM_WIDTH)),
                  full((n, MLSTM_WIDTH)), full((MLSTM_HEADS, MLSTM_HEAD_DIM)),
                  full((MIX_WIDTH, D_MODEL))],
        out_specs=full((n, D_MODEL)),
        out_shape=jax.ShapeDtypeStruct((n, D_MODEL), f32),
        compiler_params=pltpu.CompilerParams(vmem_limit_bytes=VMEM_LIMIT),
        name="sample_out",
    )(x, ao, hm, om, mnorm, wout_b)


def _rope_angles(pos):
    half = ATTN_HEAD_DIM // 2
    inv = ROPE_THETA ** (-np.arange(half, dtype=np.float64) / half)
    ang = pos.astype(np.float64)[:, None] * inv[None, :]
    return np.cos(ang).astype(np.float32), np.sin(ang).astype(np.float32)


def _rope_tables(pos):
    c, s = _rope_angles(pos)
    cos = np.tile(c, (1, LANES // QUARTER))
    sin = np.tile(np.concatenate([-s, s], axis=1), (1, LANES // ATTN_HEAD_DIM))
    return cos, sin


def _rope_tables_quarters(pos):
    c, s = _rope_angles(pos)
    return np.tile(c, (1, LANES // QUARTER)), np.concatenate([-s, -s, s, s], axis=1)


def _quarters(a):
    lo, hi = a[..., :QUARTER], a[..., QUARTER:]
    return jnp.concatenate([lo, lo, hi, hi], axis=-1)


def _prompt_attn_weights(w):
    d = w.shape[0]
    wq = w[:, COL_QA:COL_KA].reshape(d, ATTN_WIDTH // LANES, 2, 2, QUARTER)
    wq = wq.transpose(0, 1, 3, 2, 4).reshape(d, ATTN_WIDTH)
    wk = _quarters(w[:, COL_KA:COL_VA].reshape(d, KV_HEADS, ATTN_HEAD_DIM)).reshape(d, KV_HEADS * LANES)
    wv = w[:, COL_VA:COL_QM].reshape(d, KV_HEADS, 1, ATTN_HEAD_DIM)
    wv = jnp.broadcast_to(wv, (d, KV_HEADS, 2, ATTN_HEAD_DIM)).reshape(d, KV_HEADS * LANES)
    return jnp.concatenate([wq, wk, wv], axis=1)


def _to_head_major(a, nb, nt, nh, tpad, fill=0.0):
    d = a.shape[1] // nh
    a = a.reshape(nb, nt, nh, d).transpose(0, 2, 1, 3).reshape(nb * nh, nt, d)
    return jnp.pad(a, ((0, 0), (0, tpad - nt), (0, 0)), constant_values=fill)


def kernel(x_prompt, x_sample, cache_k, cache_v, state_C, state_n, state_m, attn_norm, w_in, q_norm,
           k_norm, attn_sinks, b_ig, b_fg, mlstm_norm, w_out, ffn_norm, w_gate, w_up, w_down):
    assert w_in.shape[0] == 1 and x_prompt.shape[0] == 1
    tp = x_prompt.shape[1]
    nb, nt = x_sample.shape[0], x_sample.shape[1]
    tpad = SUBLANES
    nh = MLSTM_HEADS

    w = w_in[0]
    wz = jnp.zeros((D_MODEL, FG_LANE - nh), f32)
    win_p = jnp.concatenate(
        [w[:, :COL_G + nh], wz, w[:, COL_G + nh:], jnp.zeros((D_MODEL, LANES - FG_LANE - nh), f32)],
        axis=1).astype(bf16)
    gbias = jnp.concatenate(
        [b_ig[0], jnp.zeros((FG_LANE - nh,), f32), b_fg[0], jnp.zeros((LANES - FG_LANE - nh,), f32)]
    ).reshape(1, LANES)
    wout_b = w_out[0].astype(bf16)
    wg_b = w_gate[0].astype(bf16)
    wu_b = w_up[0].astype(bf16)
    wd_b = w_down[0].astype(bf16)
    anorm = attn_norm[0].reshape(1, D_MODEL)
    fnorm = ffn_norm[0].reshape(1, D_MODEL)
    qg = jnp.tile(q_norm[0], LANES // ATTN_HEAD_DIM).reshape(1, LANES)
    kg = jnp.tile(k_norm[0], LANES // ATTN_HEAD_DIM).reshape(1, LANES)
    mnorm = mlstm_norm[0].reshape(nh, MLSTM_HEAD_DIM)
    sinks = attn_sinks[0]

    wa = _prompt_attn_weights(w).astype(bf16)
    wm = win_p[:, COL_QM:]
    ws = win_p[:, COL_KA:COL_QM]
    idx = np.arange(2 * LANES)
    same = (idx[:, None] // LANES == idx[None, :] // LANES) & (
        (idx[:, None] // QUARTER) % 2 == (idx[None, :] // QUARTER) % 2)
    bd = jnp.asarray(same, dtype=bf16)
    sink_rows_p = jnp.repeat(sinks.reshape(KV_HEADS, ATTN_GROUP), QBLOCK, axis=1)
    bias = jnp.where(jnp.arange(2 * QBLOCK)[None, None, :] == 0, sink_rows_p[:, :, None], NEG)
    qgq = _quarters(q_norm[0]).reshape(1, LANES)
    kgq = _quarters(k_norm[0]).reshape(1, LANES)
    pos_p = np.arange(tp, dtype=np.float32)
    cos_p, sin_p = _rope_tables_quarters(pos_p)
    cos_w, sin_w = _rope_tables(pos_p[tp - WINDOW:])
    x1_p, k_p, v_p, cext_p, m_p = _prompt_mixer(
        x_prompt[0], cos_p, sin_p, wa, wm, ws, wout_b, bd, bias, anorm, qgq, kgq, kg, cos_w, sin_w,
        gbias, mnorm)
    y_p = _ffn(x1_p, fnorm, wg_b, wu_b, wd_b)

    xs = x_sample.reshape(nb * nt, D_MODEL)
    cos_s, sin_s = _rope_tables(np.arange(nt, dtype=np.float32) + np.float32(PAST_LEN))
    cos_s = np.tile(cos_s, (nb, 1))
    sin_s = np.tile(sin_s, (nb, 1))
    q_s, k_s, v_s, zm_s, gate_s = _sample_proj(xs, cos_s, sin_s, win_p, anorm, qg, kg, gbias)

    qh = q_s.reshape(nb, nt, ATTN_HEADS, ATTN_HEAD_DIM).transpose(0, 2, 1, 3)
    half_rows = ATTN_GROUP * nt
    q0 = qh[:, :ATTN_GROUP].reshape(nb, half_rows, ATTN_HEAD_DIM)
    q1 = qh[:, ATTN_GROUP:].reshape(nb, half_rows, ATTN_HEAD_DIM)
    zq = jnp.zeros_like(q0)
    qbd = jnp.concatenate([jnp.concatenate([q0, zq], axis=2), jnp.concatenate([zq, q1], axis=2)], axis=1)
    kn = jnp.pad(k_s.reshape(nb, nt, KV_WIDTH), ((0, 0), (0, tpad - nt), (0, 0)))
    vn = jnp.pad(v_s.reshape(nb, nt, KV_WIDTH), ((0, 0), (0, tpad - nt), (0, 0)))
    sink_rows = jnp.broadcast_to(jnp.repeat(sinks, nt)[:, None], (ATTN_HEADS * nt, LANES))
    qm = _to_head_major(zm_s[:, 0:MLSTM_WIDTH], nb, nt, nh, tpad)
    km = _to_head_major(zm_s[:, MLSTM_WIDTH:2 * MLSTM_WIDTH], nb, nt, nh, tpad)
    vm = _to_head_major(zm_s[:, 2 * MLSTM_WIDTH:3 * MLSTM_WIDTH], nb, nt, nh, tpad)
    ig = gate_s[:, 0:nh].reshape(nb, nt, nh).transpose(0, 2, 1).reshape(nb * nh, nt)
    lf = gate_s[:, FG_LANE:FG_LANE + nh].reshape(nb, nt, nh).transpose(0, 2, 1).reshape(nb * nh, nt)
    ig_cp = jnp.pad(ig, ((0, 0), (0, tpad - nt)), constant_values=NEG)
    lf_cp = jnp.pad(lf, ((0, 0), (0, tpad - nt)))
    gp = jnp.pad(jnp.stack([ig_cp, lf_cp], axis=2), ((0, 0), (0, 0), (0, LANES - 2)))
    ig_rp = jnp.pad(ig, ((0, 0), (0, LANES - nt)), constant_values=NEG)
    lf_rp = jnp.pad(lf, ((0, 0), (0, LANES - nt)))
    gr = jnp.pad(jnp.stack([ig_rp, lf_rp], axis=1), ((0, 0), (0, tpad - 2), (0, 0)))
    ckt = cache_k[0].reshape(nb, WINDOW, KV_WIDTH).transpose(0, 2, 1)
    cvt = cache_v[0].reshape(nb, WINDOW, KV_WIDTH).transpose(0, 2, 1)
    slot = ((0, 0), (0, 0), (WINDOW - nt, 0))
    knt = jnp.pad(k_s.reshape(nb, nt, KV_WIDTH).transpose(0, 2, 1), slot)
    vnt = jnp.pad(v_s.reshape(nb, nt, KV_WIDTH).transpose(0, 2, 1), slot)
    c0 = state_C[0].reshape(nb * nh, MLSTM_HEAD_DIM, MLSTM_HEAD_DIM)
    n0 = state_n[0].reshape(nb * nh, 1, MLSTM_HEAD_DIM)
    m0 = state_m[0].reshape(nb * nh, 1, 1)

    o_bd, h_s, c_new, n_new, m_new, nkt, nvt = _sample_core(
        qbd, ckt, cvt, kn, vn, knt, vnt, sink_rows, qm, km, vm, gp, gr, c0, n0, m0)

    o0 = o_bd[:, :half_rows, :ATTN_HEAD_DIM].reshape(nb, ATTN_GROUP, nt, ATTN_HEAD_DIM)
    o1 = o_bd[:, half_rows:, ATTN_HEAD_DIM:].reshape(nb, ATTN_GROUP, nt, ATTN_HEAD_DIM)
    ao = jnp.concatenate([o0, o1], axis=1).transpose(0, 2, 1, 3).reshape(nb * nt, ATTN_WIDTH)
    hm = h_s[:, :nt].reshape(nb, nh, nt, MLSTM_HEAD_DIM).transpose(0, 2, 1, 3).reshape(nb * nt, MLSTM_WIDTH)
    om = zm_s[:, 3 * MLSTM_WIDTH:]
    x1_s = _sample_out(xs, ao, hm, om, mnorm, wout_b)
    y_s = _ffn(x1_s, fnorm, wg_b, wu_b, wd_b)

    new_k_s = nkt.transpose(0, 2, 1)
    new_v_s = nvt.transpose(0, 2, 1)

    kv_shape = (1, 1, WINDOW, KV_HEADS, ATTN_HEAD_DIM)
    return (
        y_p[None],
        y_s.reshape(nb, nt, D_MODEL),
        k_p.reshape(kv_shape),
        v_p.reshape(kv_shape),
        cext_p[None, None, :, :, :MLSTM_HEAD_DIM],
        cext_p[None, None, :, :, MLSTM_HEAD_DIM],
        m_p[None, None, :nh, 0],
        new_k_s.reshape(1, nb, WINDOW, KV_HEADS, ATTN_HEAD_DIM),
        new_v_s.reshape(1, nb, WINDOW, KV_HEADS, ATTN_HEAD_DIM),
        c_new.reshape(1, nb, nh, MLSTM_HEAD_DIM, MLSTM_HEAD_DIM),
        n_new.reshape(1, nb, nh, MLSTM_HEAD_DIM),
        m_new.reshape(1, nb, nh),
    )
```

```python
import jax
import jax.numpy as jnp
import numpy as np
from jax import lax
from jax.experimental import pallas as pl
from jax.experimental.pallas import tpu as pltpu

D_MODEL = 1024
PAST_LEN = 16384
ATTN_HEADS = 8
KV_HEADS = 2
ATTN_HEAD_DIM = 64
ATTN_GROUP = ATTN_HEADS // KV_HEADS
ATTN_WIDTH = ATTN_HEADS * ATTN_HEAD_DIM
KV_WIDTH = KV_HEADS * ATTN_HEAD_DIM
WINDOW = 128
ROPE_THETA = 10000.0
MLSTM_HEADS = 4
MLSTM_HEAD_DIM = 128
MLSTM_WIDTH = MLSTM_HEADS * MLSTM_HEAD_DIM
MIX_WIDTH = ATTN_WIDTH + MLSTM_WIDTH
D_FF = 2816
NORM_EPS = 1e-6

LANES = 128
SUBLANES = 8
VMEM_LIMIT = 56 * 1024 * 1024

COL_QA = 0
COL_KA = COL_QA + ATTN_WIDTH
COL_VA = COL_KA + KV_WIDTH
COL_QM = COL_VA + KV_WIDTH
COL_KM = COL_QM + MLSTM_WIDTH
COL_VM = COL_KM + MLSTM_WIDTH
COL_OM = COL_VM + MLSTM_WIDTH
COL_G = COL_OM + MLSTM_WIDTH
IN_PAD = COL_G + LANES
FG_LANE = SUBLANES

PROMPT_BLOCK = 256
QBLOCK = WINDOW
MCHUNK = 128
PROJ_CHUNK = 256
FFN_BLOCK = 512
FFN_CHUNK = 256
SAMPLE_BATCH_BLOCK = 16
SAMPLE_TOKENS = 4
NEG = -1e30

f32 = jnp.float32
bf16 = jnp.bfloat16


def _rms(x, gain):
    return x * lax.rsqrt(jnp.mean(x * x, axis=-1, keepdims=True) + NORM_EPS) * gain


def _segsum64(s, lane):
    for k in (1, 2, 4, 8, 16, 32):
        s = s + jnp.where((lane & k) != 0, pltpu.roll(s, k, 1), pltpu.roll(s, LANES - k, 1))
    return s


def _headnorm_rope(xs, gain, cos, sin_signed, lane):
    ss = _segsum64(xs * xs, lane)
    y = xs * lax.rsqrt(ss * (1.0 / ATTN_HEAD_DIM) + NORM_EPS) * gain
    partner = jnp.where((lane & 32) != 0, pltpu.roll(y, 32, 1), pltpu.roll(y, LANES - 32, 1))
    return y * cos + partner * sin_signed


def _in_proj(x, anorm, win_ref):
    h = _rms(x, anorm).astype(bf16)
    return jnp.dot(h, win_ref[...], preferred_element_type=f32)


PA_Q = 0
PA_K = PA_Q + ATTN_WIDTH
PA_V = PA_K + KV_HEADS * LANES
PA_WIDTH = PA_V + KV_HEADS * LANES
PM_Q = 0
PM_K = PM_Q + MLSTM_WIDTH
PM_V = PM_K + MLSTM_WIDTH
PM_O = PM_V + MLSTM_WIDTH
PM_G = PM_O + MLSTM_WIDTH
PM_WIDTH = PM_G + LANES
QUARTER = ATTN_HEAD_DIM // 2


def _group_sumsq(xs, bd_ref):
    x2 = xs * xs
    hi = x2.astype(bf16)
    lo = (x2 - hi.astype(f32)).astype(bf16)
    return (jnp.dot(hi, bd_ref[...], preferred_element_type=f32)
            + jnp.dot(lo, bd_ref[...], preferred_element_type=f32))


def _norm_rope_quarters(xs, ss, gain, cos, sin_signed):
    y = xs * lax.rsqrt(ss * (1.0 / ATTN_HEAD_DIM) + NORM_EPS) * gain
    return y * cos + pltpu.roll(y, LANES // 2, 1) * sin_signed


def _col_chunks(width):
    return [(c, min(c + PROJ_CHUNK, width)) for c in range(0, width, PROJ_CHUNK)]


def _mixer_inproj_jobs(x_ref, anorm_ref, wa_ref, wm_ref, za_ref, zm_ref):
    h = _rms(x_ref[...], anorm_ref[...]).astype(bf16)

    def job(w_ref, z_ref, c0, c1):
        def run():
            z_ref[:, c0:c1] = jnp.dot(h, w_ref[:, c0:c1], preferred_element_type=f32)
        return run

    return ([job(wa_ref, za_ref, c0, c1) for c0, c1 in _col_chunks(PA_WIDTH)]
            + [job(wm_ref, zm_ref, c0, c1) for c0, c1 in _col_chunks(PM_WIDTH)])


def _mixer_outproj_jobs(xs_ref, mix_ref, wout_ref, x1_ref):
    def job(c0, c1):
        def run():
            x1_ref[:, c0:c1] = xs_ref[:, c0:c1] + jnp.dot(mix_ref[...], wout_ref[:, c0:c1],
                                                          preferred_element_type=f32)
        return run

    return [job(c0, c1) for c0, c1 in _col_chunks(D_MODEL)]


def _mixer_window_out(xs_ref, anorm_ref, ws_ref, kgs_ref, coss_ref, sins_ref, kout_ref, vout_ref):
    tb = xs_ref.shape[0]
    h = _rms(xs_ref[tb - WINDOW:, :], anorm_ref[...]).astype(bf16)
    zs = jnp.dot(h, ws_ref[...], preferred_element_type=f32)
    lane_s = lax.broadcasted_iota(jnp.int32, (WINDOW, LANES), 1)
    kout_ref[...] = _headnorm_rope(zs[:, :KV_WIDTH], kgs_ref[...], coss_ref[...], sins_ref[...], lane_s)
    vout_ref[...] = zs[:, KV_WIDTH:]


def _mixer_core(first_block, last_block, fillers, za_ref, zm_ref, mix_ref, cos_ref, sin_ref, bd_ref,
                bias_ref, qg_ref, kg_ref, gbias_ref, mnorm_ref, kprev_ref, vprev_ref, cst_ref, mst_ref,
                cext_ref, mout_ref):
    tb = za_ref.shape[0]
    fillers = list(fillers)
    n_fill = len(fillers)
    slots = 1 + tb // QBLOCK + (tb // QBLOCK) * KV_HEADS + (tb // MCHUNK) * MLSTM_HEADS
    progress = [0]

    def fill():
        progress[0] += 1
        while n_fill - len(fillers) < min(n_fill, -(-n_fill * progress[0] // slots)):
            fillers.pop(0)()

    lane_t = lax.broadcasted_iota(jnp.int32, (tb, LANES), 1)
    gcol = zm_ref[:, PM_G:PM_G + LANES] + gbias_ref[...]
    acol = jnp.where(lane_t < FG_LANE, gcol, jax.nn.log_sigmoid(gcol))
    arow = acol.T
    lane8 = lax.broadcasted_iota(jnp.int32, (SUBLANES, LANES), 1)
    lane_in = lane8 & (MCHUNK - 1)
    m_prev = mst_ref[:, 0:1]
    stacks = []
    u_rows = []
    w_rows = []
    for sb in range(tb // LANES):
        ls = slice(sb * LANES, (sb + 1) * LANES)
        ig8 = arow[0:SUBLANES, ls]
        lf8 = arow[FG_LANE:FG_LANE + SUBLANES, ls]
        b8 = lf8
        k = 1
        while k < MCHUNK:
            b8 = b8 + jnp.where(lane_in >= k, pltpu.roll(b8, k, 1), 0.0)
            k *= 2
        u8 = ig8 - b8
        cm8 = u8
        k = 1
        while k < MCHUNK:
            cm8 = jnp.maximum(cm8, jnp.where(lane_in >= k, pltpu.roll(cm8, k, 1), NEG))
            k *= 2
        g8 = jnp.zeros_like(u8)
        mp8 = jnp.zeros_like(u8)
        gl8 = jnp.zeros_like(u8)
        for c in range(LANES // MCHUNK):
            in_chunk = (lane8 // MCHUNK) == c
            gc = jnp.maximum(cm8, m_prev)
            last = c * MCHUNK + MCHUNK - 1
            g_last = jnp.max(jnp.where(lane8 == last, gc, NEG), axis=1, keepdims=True)
            b_last = jnp.max(jnp.where(lane8 == last, b8, NEG), axis=1, keepdims=True)
            g8 = jnp.where(in_chunk, gc, g8)
            mp8 = jnp.where(in_chunk, m_prev, mp8)
            gl8 = jnp.where(in_chunk, g_last, gl8)
            m_prev = b_last + g_last
        a8 = jnp.exp(mp8 - g8)
        emt8 = jnp.exp(-(b8 + g8))
        aend8 = jnp.exp(mp8 - gl8)
        stacks.append(jnp.concatenate(
            [g8, a8, emt8, aend8, jnp.zeros((LANES - 4 * SUBLANES, LANES), f32)], axis=0))
        u_rows.append(u8)
        w_rows.append(jnp.exp(u8 - gl8))
    mst_ref[...] = jnp.broadcast_to(m_prev, mst_ref.shape)
    if last_block:
        mout_ref[...] = jnp.broadcast_to(m_prev, mout_ref.shape)
    colform = jnp.concatenate(stacks, axis=1).T
    fill()

    lane = lax.broadcasted_iota(jnp.int32, (QBLOCK, LANES), 1)
    head_a = ((lane // QUARTER) & 1) == 0
    low_half = lane < ATTN_HEAD_DIM
    qi = lax.broadcasted_iota(jnp.int32, (ATTN_GROUP * QBLOCK, 2 * QBLOCK), 0) & (QBLOCK - 1)
    kj = lax.broadcasted_iota(jnp.int32, (ATTN_GROUP * QBLOCK, 2 * QBLOCK), 1)
    band = (kj > qi) & (kj <= qi + QBLOCK)
    row0 = lax.broadcasted_iota(jnp.int32, (QBLOCK, LANES), 0) == 0
    ones_slab = jnp.ones((2 * QBLOCK, LANES), bf16)

    def attn_prep(qb):
        rows = slice(qb * QBLOCK, (qb + 1) * QBLOCK)
        cos = cos_ref[rows, :]
        sin = sin_ref[rows, :]
        ss = [_group_sumsq(za_ref[rows, d * 2 * LANES:(d + 1) * 2 * LANES], bd_ref)
              for d in range(PA_V // (2 * LANES))]
        slabs = []
        for j in range(PA_V // LANES):
            xs = za_ref[rows, j * LANES:(j + 1) * LANES]
            gain = qg_ref[...] if j < ATTN_WIDTH // LANES else kg_ref[...]
            slabs.append(_norm_rope_quarters(xs, ss[j // 2][:, (j % 2) * LANES:(j % 2 + 1) * LANES],
                                             gain, cos, sin))
        return slabs

    def attn_unit(qb, c, slabs, kprev, vprev):
        rows = slice(qb * QBLOCK, (qb + 1) * QBLOCK)
        kcur = slabs[ATTN_WIDTH // LANES + c].astype(bf16)
        vcur = za_ref[rows, PA_V + c * LANES:PA_V + (c + 1) * LANES].astype(bf16)
        kcat = jnp.concatenate([kprev, kcur], axis=0)
        vext = jnp.concatenate([jnp.concatenate([vprev, vcur], axis=0), ones_slab], axis=1)
        q_heads = []
        for j in (2 * c, 2 * c + 1):
            qs = slabs[j] * (ATTN_HEAD_DIM ** -0.5)
            q_heads.append(jnp.where(head_a, qs, 0.0).astype(bf16))
            q_heads.append(jnp.where(head_a, 0.0, qs).astype(bf16))
        qst = jnp.concatenate(q_heads, axis=0)
        s = lax.dot_general(qst, kcat, (((1,), (1,)), ((), ())), preferred_element_type=f32)
        valid = band & (kj >= QBLOCK) if (first_block and qb == 0) else band
        s = jnp.where(valid, s, bias_ref[c])
        p = jnp.exp(s - jnp.max(s, axis=-1, keepdims=True)).astype(bf16)
        of = jnp.dot(p, vext, preferred_element_type=f32)
        o = of[:, :LANES] / of[:, LANES:]
        for jj in range(2):
            pair = jnp.where(low_half, o[(2 * jj) * QBLOCK:(2 * jj + 1) * QBLOCK],
                             o[(2 * jj + 1) * QBLOCK:(2 * jj + 2) * QBLOCK])
            col = (2 * c + jj) * LANES
            mix_ref[rows, col:col + LANES] = pair.astype(bf16)
        return kcur, jnp.where(row0, jnp.zeros_like(vcur), vcur)

    ti = lax.broadcasted_iota(jnp.int32, (MCHUNK, MCHUNK), 0)
    si = lax.broadcasted_iota(jnp.int32, (MCHUNK, MCHUNK), 1)
    causal = si <= ti
    ones_l = jnp.ones((MCHUNK, LANES), bf16)

    def mlstm_unit(hd, c, cext):
        hcols = lambda base: slice(base + hd * MLSTM_HEAD_DIM, base + (hd + 1) * MLSTM_HEAD_DIM)
        rows = slice(c * MCHUNK, (c + 1) * MCHUNK)
        qb_ = zm_ref[rows, hcols(PM_Q)].astype(bf16)
        kf = zm_ref[rows, hcols(PM_K)] * (MLSTM_HEAD_DIM ** -0.5)
        vb = zm_ref[rows, hcols(PM_V)].astype(bf16)
        og = zm_ref[rows, hcols(PM_O)]
        g_c = colform[rows, hd:hd + 1]
        a_c = colform[rows, SUBLANES + hd:SUBLANES + hd + 1]
        emt_c = colform[rows, 2 * SUBLANES + hd:2 * SUBLANES + hd + 1]
        aend = colform[c * MCHUNK:c * MCHUNK + 1, 3 * SUBLANES + hd:3 * SUBLANES + hd + 1]
        sb, off = divmod(c * MCHUNK, LANES)
        u_r = u_rows[sb][hd:hd + 1, off:off + MCHUNK]
        w_r = w_rows[sb][hd:hd + 1, off:off + MCHUNK]
        dmat = jnp.exp(jnp.where(causal, u_r - g_c, NEG))
        smat = lax.dot_general(qb_, kf.astype(bf16), (((1,), (1,)), ((), ())),
                               preferred_element_type=f32) * dmat
        vext = jnp.concatenate([vb, ones_l], axis=1)
        nd = (a_c * jnp.dot(qb_, cext.astype(bf16), preferred_element_type=f32)
              + jnp.dot(smat.astype(bf16), vext, preferred_element_type=f32))
        hraw = nd[:, :MLSTM_HEAD_DIM] / jnp.maximum(jnp.abs(nd[:, MLSTM_HEAD_DIM:]), emt_c)
        hn = _rms(hraw, mnorm_ref[hd:hd + 1, :])
        mix_ref[rows, ATTN_WIDTH + hd * MLSTM_HEAD_DIM:ATTN_WIDTH + (hd + 1) * MLSTM_HEAD_DIM] = (
            (hn * jax.nn.sigmoid(og)).astype(bf16))
        kw_t = (kf.T * w_r).astype(bf16)
        return aend * cext + jnp.dot(kw_t, vext, preferred_element_type=f32)

    nqb = tb // QBLOCK
    nch = tb // MCHUNK
    preps = []
    for qb in range(nqb):
        preps.append(attn_prep(qb))
        fill()
    kv_state = [(kprev_ref[c], vprev_ref[c]) for c in range(KV_HEADS)]
    cexts = [cst_ref[hd] for hd in range(MLSTM_HEADS)]
    attn_jobs = [(qb, c) for qb in range(nqb) for c in range(KV_HEADS)]
    mlstm_jobs = [(hd, c) for c in range(nch) for hd in range(MLSTM_HEADS)]
    while attn_jobs or mlstm_jobs:
        if attn_jobs:
            qb, c = attn_jobs.pop(0)
            kv_state[c] = attn_unit(qb, c, preps[qb], *kv_state[c])
            fill()
        for _ in range(2 if len(mlstm_jobs) > 2 * len(attn_jobs) else 1):
            if mlstm_jobs:
                hd, c = mlstm_jobs.pop(0)
                cexts[hd] = mlstm_unit(hd, c, cexts[hd])
                fill()
    assert not fillers
    for c in range(KV_HEADS):
        kprev_ref[c], vprev_ref[c] = kv_state[c]
    for hd in range(MLSTM_HEADS):
        cst_ref[hd] = cexts[hd]
        if last_block:
            cext_ref[hd] = cexts[hd]


def _prompt_mixer_kernel(x_ref, cos_ref, sin_ref, wa_ref, wm_ref, ws_ref, wout_ref, bd_ref, bias_ref,
                         anorm_ref, qg_ref, kg_ref, kgs_ref, coss_ref, sins_ref, gbias_ref, mnorm_ref,
                         x1_ref, kout_ref, vout_ref, cext_ref, mout_ref,
                         za0, za1, zm0, zm1, mix0, mix1, xs0, xs1, kprev_ref, vprev_ref, cst_ref, mst_ref):
    step = pl.program_id(0)
    nblk = pl.num_programs(0) - 2
    za, zm, mix, xs = (za0, za1), (zm0, zm1), (mix0, mix1), (xs0, xs1)

    def run(parity, do_in, do_core, do_out, first_block=False, last_block=False):
        jobs = []
        if do_out:
            jobs += _mixer_outproj_jobs(xs[parity], mix[parity], wout_ref, x1_ref)
        if do_in:
            jobs += _mixer_inproj_jobs(x_ref, anorm_ref, wa_ref, wm_ref, za[parity], zm[parity])
        if do_core:
            _mixer_core(first_block, last_block, jobs, za[1 - parity], zm[1 - parity], mix[1 - parity],
                        cos_ref, sin_ref, bd_ref, bias_ref, qg_ref, kg_ref, gbias_ref, mnorm_ref,
                        kprev_ref, vprev_ref, cst_ref, mst_ref, cext_ref, mout_ref)
        else:
            for job in jobs:
                job()
        if last_block:
            _mixer_window_out(xs[1 - parity], anorm_ref, ws_ref, kgs_ref, coss_ref, sins_ref,
                              kout_ref, vout_ref)
        if do_in:
            xs[parity][...] = x_ref[...]

    @pl.when(step == 0)
    def _first():
        kprev_ref[...] = jnp.zeros_like(kprev_ref)
        vprev_ref[...] = jnp.zeros_like(vprev_ref)
        cst_ref[...] = jnp.zeros_like(cst_ref)
        mst_ref[...] = jnp.zeros_like(mst_ref)
        run(0, True, False, False)

    @pl.when(step == 1)
    def _second():
        run(1, True, True, False, first_block=True)

    steady = (step >= 2) & (step < nblk)

    @pl.when(steady & (step % 2 == 0))
    def _even():
        run(0, True, True, True)

    @pl.when(steady & (step % 2 == 1))
    def _odd():
        run(1, True, True, True)

    @pl.when(step == nblk)
    def _drain_core():
        run(0, False, True, True, last_block=True)

    @pl.when(step == nblk + 1)
    def _drain_out():
        run(1, False, False, True)


def _const_spec(shape):
    nd = len(shape)
    return pl.BlockSpec(shape, lambda i, *_: (0,) * nd)


def _prompt_mixer(x, cos, sin, wa, wm, ws, wout_b, bd, bias, anorm, qg, kg, kgs, coss, sins, gbias, mnorm):
    t = x.shape[0]
    tb = PROMPT_BLOCK
    nblk = t // tb
    assert nblk % 2 == 0 and nblk >= 4
    state_shape = (MLSTM_HEADS, MLSTM_HEAD_DIM, 2 * MLSTM_HEAD_DIM)
    last = nblk - 1
    return pl.pallas_call(
        _prompt_mixer_kernel,
        grid=(nblk + 2,),
        in_specs=[
            pl.BlockSpec((tb, D_MODEL), lambda i: (jnp.minimum(i, last), 0)),
            pl.BlockSpec((tb, LANES), lambda i: (jnp.clip(i - 1, 0, last), 0)),
            pl.BlockSpec((tb, LANES), lambda i: (jnp.clip(i - 1, 0, last), 0)),
            _const_spec((D_MODEL, PA_WIDTH)),
            _const_spec((D_MODEL, PM_WIDTH)),
            _const_spec((D_MODEL, 2 * KV_WIDTH)),
            _const_spec((MIX_WIDTH, D_MODEL)),
            _const_spec((2 * LANES, 2 * LANES)),
            _const_spec((KV_HEADS, ATTN_GROUP * QBLOCK, 2 * QBLOCK)),
            _const_spec((1, D_MODEL)),
            _const_spec((1, LANES)),
            _const_spec((1, LANES)),
            _const_spec((1, LANES)),
            _const_spec((WINDOW, LANES)),
            _const_spec((WINDOW, LANES)),
            _const_spec((1, LANES)),
            _const_spec((MLSTM_HEADS, MLSTM_HEAD_DIM)),
        ],
        out_specs=[
            pl.BlockSpec((tb, D_MODEL), lambda i: (jnp.clip(i - 2, 0, last), 0)),
            _const_spec((WINDOW, KV_WIDTH)),
            _const_spec((WINDOW, KV_WIDTH)),
            _const_spec(state_shape),
            _const_spec((SUBLANES, LANES)),
        ],
        out_shape=[
            jax.ShapeDtypeStruct((t, D_MODEL), f32),
            jax.ShapeDtypeStruct((WINDOW, KV_WIDTH), f32),
            jax.ShapeDtypeStruct((WINDOW, KV_WIDTH), f32),
            jax.ShapeDtypeStruct(state_shape, f32),
            jax.ShapeDtypeStruct((SUBLANES, LANES), f32),
        ],
        scratch_shapes=(
            [pltpu.VMEM((tb, PA_WIDTH), f32)] * 2 + [pltpu.VMEM((tb, PM_WIDTH), f32)] * 2
            + [pltpu.VMEM((tb, MIX_WIDTH), bf16)] * 2 + [pltpu.VMEM((tb, D_MODEL), f32)] * 2
            + [pltpu.VMEM((KV_HEADS, WINDOW, LANES), bf16)] * 2
            + [pltpu.VMEM(state_shape, f32), pltpu.VMEM((SUBLANES, LANES), f32)]),
        compiler_params=pltpu.CompilerParams(
            dimension_semantics=("arbitrary",), vmem_limit_bytes=VMEM_LIMIT),
        name="prompt_mixer",
    )(x, cos, sin, wa, wm, ws, wout_b, bd, bias, anorm, qg, kg, kgs, coss, sins, gbias, mnorm)


def _ffn_kernel(x_ref, g_ref, wg_ref, wu_ref, wd_ref, o_ref):
    x = x_ref[...]
    hf = _rms(x, g_ref[...]).astype(bf16)
    acc = x
    for c in range(D_FF // FFN_CHUNK):
        cs = slice(c * FFN_CHUNK, (c + 1) * FFN_CHUNK)
        gate = jnp.dot(hf, wg_ref[:, cs], preferred_element_type=f32)
        up = jnp.dot(hf, wu_ref[:, cs], preferred_element_type=f32)
        act = (gate * jax.nn.sigmoid(gate) * up).astype(bf16)
        acc = acc + jnp.dot(act, wd_ref[cs, :], preferred_element_type=f32)
    o_ref[...] = acc


def _ffn(x, fnorm, wg_b, wu_b, wd_b):
    n = x.shape[0]
    tm = FFN_BLOCK
    return pl.pallas_call(
        _ffn_kernel,
        grid=(n // tm,),
        in_specs=[
            pl.BlockSpec((tm, D_MODEL), lambda i: (i, 0)),
            _const_spec((1, D_MODEL)),
            _const_spec((D_MODEL, D_FF)),
            _const_spec((D_MODEL, D_FF)),
            _const_spec((D_FF, D_MODEL)),
        ],
        out_specs=pl.BlockSpec((tm, D_MODEL), lambda i: (i, 0)),
        out_shape=jax.ShapeDtypeStruct((n, D_MODEL), f32),
        compiler_params=pltpu.CompilerParams(
            dimension_semantics=("arbitrary",), vmem_limit_bytes=VMEM_LIMIT),
        name="ffn",
    )(x, fnorm, wg_b, wu_b, wd_b)


def _sample_proj_kernel(x_ref, cos_ref, sin_ref, win_ref, anorm_ref, qg_ref, kg_ref, gbias_ref,
                        q_ref, k_ref, v_ref, zm_ref, gate_ref):
    n = x_ref.shape[0]
    z = _in_proj(x_ref[...], anorm_ref[...], win_ref)
    lane = lax.broadcasted_iota(jnp.int32, (n, LANES), 1)
    cos = cos_ref[...]
    sin = sin_ref[...]
    for j in range(ATTN_WIDTH // LANES):
        qs = _headnorm_rope(z[:, COL_QA + j * LANES:COL_QA + (j + 1) * LANES], qg_ref[...], cos, sin, lane)
        q_ref[:, j * LANES:(j + 1) * LANES] = qs * (ATTN_HEAD_DIM ** -0.5)
    k_ref[...] = _headnorm_rope(z[:, COL_KA:COL_KA + KV_WIDTH], kg_ref[...], cos, sin, lane)
    v_ref[...] = z[:, COL_VA:COL_VA + KV_WIDTH]
    zm_ref[:, 0:MLSTM_WIDTH] = z[:, COL_QM:COL_KM]
    zm_ref[:, MLSTM_WIDTH:2 * MLSTM_WIDTH] = z[:, COL_KM:COL_VM] * (MLSTM_HEAD_DIM ** -0.5)
    zm_ref[:, 2 * MLSTM_WIDTH:4 * MLSTM_WIDTH] = z[:, COL_VM:COL_G]
    gcol = z[:, COL_G:COL_G + LANES] + gbias_ref[...]
    gate_ref[...] = jnp.where(lane < FG_LANE, gcol, jax.nn.log_sigmoid(gcol))


def _sample_proj(x, cos, sin, win_p, anorm, qg, kg, gbias):
    n = x.shape[0]
    full = lambda shape: pl.BlockSpec(shape, lambda i: (0,) * len(shape))
    return pl.pallas_call(
        _sample_proj_kernel,
        grid=(1,),
        in_specs=[full((n, D_MODEL)), full((n, LANES)), full((n, LANES)), full((D_MODEL, IN_PAD)),
                  full((1, D_MODEL)), full((1, LANES)), full((1, LANES)), full((1, LANES))],
        out_specs=[full((n, ATTN_WIDTH)), full((n, KV_WIDTH)), full((n, KV_WIDTH)),
                   full((n, 4 * MLSTM_WIDTH)), full((n, LANES))],
        out_shape=[jax.ShapeDtypeStruct((n, ATTN_WIDTH), f32),
                   jax.ShapeDtypeStruct((n, KV_WIDTH), f32),
                   jax.ShapeDtypeStruct((n, KV_WIDTH), f32),
                   jax.ShapeDtypeStruct((n, 4 * MLSTM_WIDTH), f32),
                   jax.ShapeDtypeStruct((n, LANES), f32)],
        compiler_params=pltpu.CompilerParams(vmem_limit_bytes=VMEM_LIMIT),
        name="sample_proj",
    )(x, cos, sin, win_p, anorm, qg, kg, gbias)


def _sample_core_kernel(qbd_ref, ckt_ref, cvt_ref, kn_ref, vn_ref, knt_ref, vnt_ref, sink_ref,
                        qm_ref, km_ref, vm_ref, gp_ref, gr_ref, c_ref, n_ref, m_ref,
                        o_ref, h_ref, cn_ref, nn_ref, mn_ref, nkt_ref, nvt_ref):
    bb = qbd_ref.shape[0]
    nrow = qbd_ref.shape[1]
    tpad = kn_ref.shape[1]
    nreal = nrow // ATTN_HEADS
    qbd = qbd_ref[...].astype(bf16)
    ckt = ckt_ref[...]
    cvt = cvt_ref[...]
    zpad_k = jnp.zeros((bb, LANES - tpad, LANES), bf16)
    kn = jnp.concatenate([kn_ref[...].astype(bf16), zpad_k], axis=1)
    vn = jnp.concatenate([vn_ref[...].astype(bf16), zpad_k], axis=1)
    s = jnp.concatenate(
        [jnp.einsum('bqd,bdw->bqw', qbd, ckt.astype(bf16), preferred_element_type=f32),
         jnp.einsum('bqd,bkd->bqk', qbd, kn, preferred_element_type=f32)], axis=2)
    lane3 = lax.broadcasted_iota(jnp.int32, ckt.shape, 2)
    keep = lane3 < WINDOW - nreal
    nkt_ref[...] = jnp.where(keep, pltpu.roll(ckt, WINDOW - nreal, 2), knt_ref[...])
    nvt_ref[...] = jnp.where(keep, pltpu.roll(cvt, WINDOW - nreal, 2), vnt_ref[...])
    tq = lax.broadcasted_iota(jnp.int32, s.shape, 1) & (nreal - 1)
    kj = lax.broadcasted_iota(jnp.int32, s.shape, 2)
    valid = ((kj < WINDOW) & (kj > tq)) | ((kj >= WINDOW) & ((kj - WINDOW) <= tq))
    s = jnp.where(valid, s, NEG)
    sink = sink_ref[:, 0:1][None]
    mx = jnp.maximum(jnp.max(s, axis=-1, keepdims=True), sink)
    p = jnp.exp(s - mx)
    den = jnp.sum(p, axis=-1, keepdims=True) + jnp.exp(sink - mx)
    pb = p.astype(bf16)
    o = (jnp.einsum('bqw,bdw->bqd', pb[:, :, :WINDOW], cvt.astype(bf16), preferred_element_type=f32)
         + jnp.einsum('bqk,bkd->bqd', pb[:, :, WINDOW:], vn, preferred_element_type=f32))
    o_ref[...] = o / den

    q = qm_ref[...]
    k = km_ref[...]
    v = vm_ref[...]
    gp = gp_ref[...]
    gr = gr_ref[...]
    c0 = c_ref[...]
    n0 = n_ref[...]
    m0 = m_ref[...]
    ig_c = gp[:, :, 0:1]
    lf_c = gp[:, :, 1:2]
    ig_r = gr[:, 0:1, :]
    lf_r = gr[:, 1:2, :]
    ti = lax.broadcasted_iota(jnp.int32, q.shape, 1)
    si = lax.broadcasted_iota(jnp.int32, q.shape, 2)
    tri = si <= ti
    b_c = jnp.sum(jnp.where(tri, lf_r, 0.0), axis=2, keepdims=True)
    b_r = jnp.sum(jnp.where(ti <= si, lf_c, 0.0), axis=1, keepdims=True)
    dlog = jnp.where(tri, b_c - b_r + ig_r, NEG)
    inter = b_c + m0
    m_t = jnp.maximum(inter, jnp.max(dlog, axis=2, keepdims=True))
    dmat = jnp.exp(dlog - m_t)
    smat = jnp.zeros_like(q)
    for sx in range(nreal):
        col = jnp.sum(q * k[:, sx:sx + 1, :], axis=2, keepdims=True)
        smat = jnp.where(si == sx, col, smat)
    smat = smat * dmat
    a = jnp.exp(inter - m_t)
    qc = jnp.einsum('gtd,gde->gte', q.astype(bf16), c0.astype(bf16), preferred_element_type=f32)
    intra = jnp.zeros_like(q)
    for sx in range(nreal):
        intra = intra + smat[:, :, sx:sx + 1] * v[:, sx:sx + 1, :]
    num = a * qc + intra
    qn = jnp.sum(q * n0, axis=2, keepdims=True)
    den_m = a * qn + jnp.sum(smat, axis=2, keepdims=True)
    h_ref[...] = num / jnp.maximum(jnp.abs(den_m), jnp.exp(-m_t))
    last = nreal - 1
    m_new = m_t[:, last:last + 1, :]
    b_last = b_c[:, last:last + 1, :]
    a_end = jnp.exp(b_last + m0 - m_new)
    w_c = jnp.exp(b_last - b_c + ig_c - m_new)
    kw = k * w_c
    upd = jnp.einsum('gsd,gse->gde', kw.astype(bf16), v.astype(bf16), preferred_element_type=f32)
    cn_ref[...] = a_end * c0 + upd
    nn_ref[...] = a_end * n0 + jnp.sum(kw, axis=1, keepdims=True)
    mn_ref[...] = m_new


def _sample_core(qbd, ckt, cvt, kn, vn, knt, vnt, sink_rows, qm, km, vm, gp, gr, c0, n0, m0):
    nb = qbd.shape[0]
    bb = SAMPLE_BATCH_BLOCK
    gb = bb * MLSTM_HEADS
    ng = nb * MLSTM_HEADS
    nrow = qbd.shape[1]
    tpad = kn.shape[1]
    blk = lambda shape: pl.BlockSpec(shape, lambda i: (i,) + (0,) * (len(shape) - 1))
    return pl.pallas_call(
        _sample_core_kernel,
        grid=(nb // bb,),
        in_specs=[blk((bb, nrow, LANES)), blk((bb, KV_WIDTH, WINDOW)), blk((bb, KV_WIDTH, WINDOW)),
                  blk((bb, tpad, KV_WIDTH)), blk((bb, tpad, KV_WIDTH)),
                  blk((bb, KV_WIDTH, WINDOW)), blk((bb, KV_WIDTH, WINDOW)),
                  pl.BlockSpec((nrow, LANES), lambda i: (0, 0)),
                  blk((gb, tpad, MLSTM_HEAD_DIM)), blk((gb, tpad, MLSTM_HEAD_DIM)),
                  blk((gb, tpad, MLSTM_HEAD_DIM)), blk((gb, tpad, LANES)), blk((gb, tpad, LANES)),
                  blk((gb, MLSTM_HEAD_DIM, MLSTM_HEAD_DIM)), blk((gb, 1, MLSTM_HEAD_DIM)),
                  blk((gb, 1, 1))],
        out_specs=[blk((bb, nrow, LANES)), blk((gb, tpad, MLSTM_HEAD_DIM)),
                   blk((gb, MLSTM_HEAD_DIM, MLSTM_HEAD_DIM)), blk((gb, 1, MLSTM_HEAD_DIM)),
                   blk((gb, 1, 1)), blk((bb, KV_WIDTH, WINDOW)), blk((bb, KV_WIDTH, WINDOW))],
        out_shape=[jax.ShapeDtypeStruct((nb, nrow, LANES), f32),
                   jax.ShapeDtypeStruct((ng, tpad, MLSTM_HEAD_DIM), f32),
                   jax.ShapeDtypeStruct((ng, MLSTM_HEAD_DIM, MLSTM_HEAD_DIM), f32),
                   jax.ShapeDtypeStruct((ng, 1, MLSTM_HEAD_DIM), f32),
                   jax.ShapeDtypeStruct((ng, 1, 1), f32),
                   jax.ShapeDtypeStruct((nb, KV_WIDTH, WINDOW), f32),
                   jax.ShapeDtypeStruct((nb, KV_WIDTH, WINDOW), f32)],
        compiler_params=pltpu.CompilerParams(
            dimension_semantics=("arbitrary",), vmem_limit_bytes=VMEM_LIMIT),
        name="sample_core",
    )(qbd, ckt, cvt, kn, vn, knt, vnt, sink_rows, qm, km, vm, gp, gr, c0, n0, m0)


def _sample_out_kernel(x_ref, ao_ref, hm_ref, om_ref, mnorm_ref, wout_ref, x1_ref):
    parts = [ao_ref[...].astype(bf16)]
    for hd in range(MLSTM_HEADS):
        cs = slice(hd * MLSTM_HEAD_DIM, (hd + 1) * MLSTM_HEAD_DIM)
        hn = _rms(hm_ref[:, cs], mnorm_ref[hd:hd + 1, :])
        parts.append((hn * jax.nn.sigmoid(om_ref[:, cs])).astype(bf16))
    mix = jnp.concatenate(parts, axis=1)
    x1_ref[...] = x_ref[...] + jnp.dot(mix, wout_ref[...], preferred_element_type=f32)


def _sample_out(x, ao, hm, om, mnorm, wout_b):
    n = x.shape[0]
    full = lambda shape: pl.BlockSpec(shape, lambda i: (0,) * len(shape))
    return pl.pallas_call(
        _sample_out_kernel,
        grid=(1,),
        in_specs=[full((n, D_MODEL)), full((n, ATTN_WIDTH)), full((n, MLSTM_WIDTH)),
                  full((n, MLSTM_WIDTH)), full((MLSTM_HEADS, MLSTM_HEAD_DIM)),
                  full((MIX_WIDTH, D_MODEL))],
        out_specs=full((n, D_MODEL)),
        out_shape=jax.ShapeDtypeStruct((n, D_MODEL), f32),
        compiler_params=pltpu.CompilerParams(vmem_limit_bytes=VMEM_LIMIT),
        name="sample_out",
    )(x, ao, hm, om, mnorm, wout_b)


def _sample_mixer_kernel(x_ref, ckt_ref, cvt_ref, c_ref, n_ref, m_ref, cos_ref, sin_ref, win_ref, wout_ref,
                         sink_ref, anorm_ref, qg_ref, kg_ref, gbias_ref, mnorm_ref,
                         x1_ref, nkt_ref, nvt_ref, cn_ref, nn_ref, mn_ref):
    bb, tpad, _ = x_ref.shape
    nrows = bb * tpad
    nreal = SAMPLE_TOKENS
    z = _in_proj(x_ref[...].reshape(nrows, D_MODEL), anorm_ref[...], win_ref)
    lane = lax.broadcasted_iota(jnp.int32, (nrows, LANES), 1)
    low = lane < ATTN_HEAD_DIM
    cos = cos_ref[...]
    sin = sin_ref[...]

    def per_seq(a):
        return a.reshape(bb, tpad, a.shape[-1])

    q_rows = []
    for j in range(ATTN_GROUP):
        qs = _headnorm_rope(z[:, COL_QA + j * LANES:COL_QA + (j + 1) * LANES], qg_ref[...], cos, sin, lane)
        qs = qs * (ATTN_HEAD_DIM ** -0.5)
        q_rows.append(per_seq(jnp.where(low, qs, 0.0)).astype(bf16))
        q_rows.append(per_seq(jnp.where(low, 0.0, qs)).astype(bf16))
    qbd = jnp.concatenate(q_rows, axis=1)
    knew = _headnorm_rope(z[:, COL_KA:COL_KA + KV_WIDTH], kg_ref[...], cos, sin, lane)
    vnew = z[:, COL_VA:COL_VA + KV_WIDTH]
    zpad = jnp.zeros((bb, LANES - tpad, LANES), bf16)
    knp = jnp.concatenate([per_seq(knew).astype(bf16), zpad], axis=1)
    vnp = jnp.concatenate([per_seq(vnew).astype(bf16), zpad], axis=1)
    ckt = ckt_ref[...]
    cvt = cvt_ref[...]
    s = jnp.concatenate(
        [jnp.einsum('bqd,bdw->bqw', qbd, ckt.astype(bf16), preferred_element_type=f32),
         jnp.einsum('bqd,bkd->bqk', qbd, knp, preferred_element_type=f32)], axis=2)
    tq = lax.broadcasted_iota(jnp.int32, s.shape, 1) & (tpad - 1)
    kj = lax.broadcasted_iota(jnp.int32, s.shape, 2)
    valid = ((kj < WINDOW) & (kj > tq)) | ((kj >= WINDOW) & (kj - WINDOW <= tq) & (kj - WINDOW < nreal))
    s = jnp.where(valid, s, NEG)
    sink = sink_ref[:, 0:1][None]
    mx = jnp.maximum(jnp.max(s, axis=-1, keepdims=True), sink)
    p = jnp.exp(s - mx)
    den = jnp.sum(p, axis=-1, keepdims=True) + jnp.exp(sink - mx)
    pb = p.astype(bf16)
    o = (jnp.einsum('bqw,bdw->bqd', pb[:, :, :WINDOW], cvt.astype(bf16), preferred_element_type=f32)
         + jnp.einsum('bqk,bkd->bqd', pb[:, :, WINDOW:], vnp, preferred_element_type=f32)) / den
    low3 = lax.broadcasted_iota(jnp.int32, (bb, tpad, LANES), 2) < ATTN_HEAD_DIM
    mix_parts = []
    for j in range(ATTN_GROUP):
        r0 = 2 * j * tpad
        pair = jnp.where(low3, o[:, r0:r0 + tpad, :], o[:, r0 + tpad:r0 + 2 * tpad, :])
        mix_parts.append(pair.reshape(nrows, LANES).astype(bf16))

    keep = lax.broadcasted_iota(jnp.int32, (KV_WIDTH, WINDOW), 1) < WINDOW - nreal
    knt = knew.T
    vnt = vnew.T
    for b in range(bb):
        shift = (WINDOW - nreal - b * tpad) % LANES
        nkt_ref[b] = jnp.where(keep, pltpu.roll(ckt_ref[b], WINDOW - nreal, 1), pltpu.roll(knt, shift, 1))
        nvt_ref[b] = jnp.where(keep, pltpu.roll(cvt_ref[b], WINDOW - nreal, 1), pltpu.roll(vnt, shift, 1))

    gz = per_seq(z[:, COL_G:COL_G + LANES] + gbias_ref[...])
    lgz = jax.nn.log_sigmoid(gz)
    trow = lax.broadcasted_iota(jnp.int32, (bb, tpad, 1), 1)
    real = trow < nreal
    mn_ref[...] = jnp.zeros_like(mn_ref)
    for hd in range(MLSTM_HEADS):
        hcols = lambda base: slice(base + hd * MLSTM_HEAD_DIM, base + (hd + 1) * MLSTM_HEAD_DIM)
        q = per_seq(z[:, hcols(COL_QM)])
        k = per_seq(z[:, hcols(COL_KM)]) * (MLSTM_HEAD_DIM ** -0.5)
        v = per_seq(z[:, hcols(COL_VM)])
        og = per_seq(z[:, hcols(COL_OM)])
        c0 = c_ref[:, hd]
        n0 = n_ref[:, hd:hd + 1, :]
        m0 = m_ref[:, hd:hd + 1, :]
        ig_c = jnp.where(real, gz[:, :, hd:hd + 1], NEG)
        lf_c = jnp.where(real, lgz[:, :, FG_LANE + hd:FG_LANE + hd + 1], 0.0)
        b_c = jnp.zeros_like(lf_c)
        for sx in range(nreal):
            b_c = b_c + jnp.where(trow >= sx, lf_c[:, sx:sx + 1, :], 0.0)
        dlog = [jnp.where(trow >= sx, b_c - b_c[:, sx:sx + 1, :] + ig_c[:, sx:sx + 1, :], NEG)
                for sx in range(nreal)]
        inter = b_c + m0
        m_t = inter
        for sx in range(nreal):
            m_t = jnp.maximum(m_t, dlog[sx])
        a = jnp.exp(inter - m_t)
        qc = jnp.einsum('btd,bde->bte', q.astype(bf16), c0.astype(bf16), preferred_element_type=f32)
        num = a * qc
        den_m = a * jnp.sum(q * n0, axis=2, keepdims=True)
        for sx in range(nreal):
            sd = jnp.sum(q * k[:, sx:sx + 1, :], axis=2, keepdims=True) * jnp.exp(dlog[sx] - m_t)
            num = num + sd * v[:, sx:sx + 1, :]
            den_m = den_m + sd
        hraw = num / jnp.maximum(jnp.abs(den_m), jnp.exp(-m_t))
        hn = _rms(hraw, mnorm_ref[hd:hd + 1, :][None])
        mix_parts.append((hn * jax.nn.sigmoid(og)).reshape(nrows, MLSTM_HEAD_DIM).astype(bf16))
        last = nreal - 1
        m_new = m_t[:, last:last + 1, :]
        b_last = b_c[:, last:last + 1, :]
        a_end = jnp.exp(b_last + m0 - m_new)
        kw = k * jnp.exp(b_last - b_c + ig_c - m_new)
        cn_ref[:, hd] = a_end * c0 + jnp.einsum('bsd,bse->bde', kw.astype(bf16), v.astype(bf16),
                                               preferred_element_type=f32)
        nn_ref[:, hd:hd + 1, :] = a_end * n0 + jnp.sum(kw, axis=1, keepdims=True)
        mn_ref[:, hd:hd + 1, :] = jnp.broadcast_to(m_new, (bb, 1, LANES))

    mix = jnp.concatenate(mix_parts, axis=1)
    x1 = x_ref[...].reshape(nrows, D_MODEL) + jnp.dot(mix, wout_ref[...], preferred_element_type=f32)
    x1_ref[...] = x1.reshape(bb, tpad, D_MODEL)


def _sample_mixer(x_pad, ckt, cvt, c0, n0, m0, cos, sin, win_s, wout_s, sink_tile, anorm, qg, kg, gbias,
                  mnorm):
    nb, tpad, _ = x_pad.shape
    bb = SAMPLE_BATCH_BLOCK
    nh = MLSTM_HEADS
    blk = lambda shape: pl.BlockSpec(shape, lambda i: (i,) + (0,) * (len(shape) - 1))
    cblk = (bb, nh, MLSTM_HEAD_DIM, MLSTM_HEAD_DIM)
    return pl.pallas_call(
        _sample_mixer_kernel,
        grid=(nb // bb,),
        in_specs=[blk((bb, tpad, D_MODEL)), blk((bb, KV_WIDTH, WINDOW)), blk((bb, KV_WIDTH, WINDOW)),
                  blk(cblk), blk((bb, nh, MLSTM_HEAD_DIM)), blk((bb, nh, 1)),
                  _const_spec((bb * tpad, LANES)), _const_spec((bb * tpad, LANES)),
                  _const_spec((D_MODEL, IN_PAD)), _const_spec((MIX_WIDTH, D_MODEL)),
                  _const_spec((ATTN_HEADS * tpad, LANES)), _const_spec((1, D_MODEL)),
                  _const_spec((1, LANES)), _const_spec((1, LANES)), _const_spec((1, LANES)),
                  _const_spec((nh, MLSTM_HEAD_DIM))],
        out_specs=[blk((bb, tpad, D_MODEL)), blk((bb, KV_WIDTH, WINDOW)), blk((bb, KV_WIDTH, WINDOW)),
                   blk(cblk), blk((bb, nh, MLSTM_HEAD_DIM)), blk((bb, tpad, LANES))],
        out_shape=[jax.ShapeDtypeStruct((nb, tpad, D_MODEL), f32),
                   jax.ShapeDtypeStruct((nb, KV_WIDTH, WINDOW), f32),
                   jax.ShapeDtypeStruct((nb, KV_WIDTH, WINDOW), f32),
                   jax.ShapeDtypeStruct((nb,) + cblk[1:], f32),
                   jax.ShapeDtypeStruct((nb, nh, MLSTM_HEAD_DIM), f32),
                   jax.ShapeDtypeStruct((nb, tpad, LANES), f32)],
        compiler_params=pltpu.CompilerParams(
            dimension_semantics=("arbitrary",), vmem_limit_bytes=VMEM_LIMIT),
        name="sample_mixer",
    )(x_pad, ckt, cvt, c0, n0, m0, cos, sin, win_s, wout_s, sink_tile, anorm, qg, kg, gbias, mnorm)


def _rope_angles(pos):
    half = ATTN_HEAD_DIM // 2
    inv = ROPE_THETA ** (-np.arange(half, dtype=np.float64) / half)
    ang = pos.astype(np.float64)[:, None] * inv[None, :]
    return np.cos(ang).astype(np.float32), np.sin(ang).astype(np.float32)


def _rope_tables(pos):
    c, s = _rope_angles(pos)
    cos = np.tile(c, (1, LANES // QUARTER))
    sin = np.tile(np.concatenate([-s, s], axis=1), (1, LANES // ATTN_HEAD_DIM))
    return cos, sin


def _rope_tables_quarters(pos):
    c, s = _rope_angles(pos)
    return np.tile(c, (1, LANES // QUARTER)), np.concatenate([-s, -s, s, s], axis=1)


def _quarters(a):
    lo, hi = a[..., :QUARTER], a[..., QUARTER:]
    return jnp.concatenate([lo, lo, hi, hi], axis=-1)


def _prompt_attn_weights(w):
    d = w.shape[0]
    wq = w[:, COL_QA:COL_KA].reshape(d, ATTN_WIDTH // LANES, 2, 2, QUARTER)
    wq = wq.transpose(0, 1, 3, 2, 4).reshape(d, ATTN_WIDTH)
    wk = _quarters(w[:, COL_KA:COL_VA].reshape(d, KV_HEADS, ATTN_HEAD_DIM)).reshape(d, KV_HEADS * LANES)
    wv = w[:, COL_VA:COL_QM].reshape(d, KV_HEADS, 1, ATTN_HEAD_DIM)
    wv = jnp.broadcast_to(wv, (d, KV_HEADS, 2, ATTN_HEAD_DIM)).reshape(d, KV_HEADS * LANES)
    return jnp.concatenate([wq, wk, wv], axis=1)


def _to_head_major(a, nb, nt, nh, tpad, fill=0.0):
    d = a.shape[1] // nh
    a = a.reshape(nb, nt, nh, d).transpose(0, 2, 1, 3).reshape(nb * nh, nt, d)
    return jnp.pad(a, ((0, 0), (0, tpad - nt), (0, 0)), constant_values=fill)


def kernel(x_prompt, x_sample, cache_k, cache_v, state_C, state_n, state_m, attn_norm, w_in, q_norm,
           k_norm, attn_sinks, b_ig, b_fg, mlstm_norm, w_out, ffn_norm, w_gate, w_up, w_down):
    assert w_in.shape[0] == 1 and x_prompt.shape[0] == 1
    tp = x_prompt.shape[1]
    nb, nt = x_sample.shape[0], x_sample.shape[1]
    tpad = SUBLANES
    nh = MLSTM_HEADS

    w = w_in[0]
    wz = jnp.zeros((D_MODEL, FG_LANE - nh), f32)
    win_p = jnp.concatenate(
        [w[:, :COL_G + nh], wz, w[:, COL_G + nh:], jnp.zeros((D_MODEL, LANES - FG_LANE - nh), f32)],
        axis=1).astype(bf16)
    gbias = jnp.concatenate(
        [b_ig[0], jnp.zeros((FG_LANE - nh,), f32), b_fg[0], jnp.zeros((LANES - FG_LANE - nh,), f32)]
    ).reshape(1, LANES)
    wout_b = w_out[0].astype(bf16)
    wg_b = w_gate[0].astype(bf16)
    wu_b = w_up[0].astype(bf16)
    wd_b = w_down[0].astype(bf16)
    anorm = attn_norm[0].reshape(1, D_MODEL)
    fnorm = ffn_norm[0].reshape(1, D_MODEL)
    qg = jnp.tile(q_norm[0], LANES // ATTN_HEAD_DIM).reshape(1, LANES)
    kg = jnp.tile(k_norm[0], LANES // ATTN_HEAD_DIM).reshape(1, LANES)
    mnorm = mlstm_norm[0].reshape(nh, MLSTM_HEAD_DIM)
    sinks = attn_sinks[0]

    wa = _prompt_attn_weights(w).astype(bf16)
    wm = win_p[:, COL_QM:]
    ws = win_p[:, COL_KA:COL_QM]
    idx = np.arange(2 * LANES)
    same = (idx[:, None] // LANES == idx[None, :] // LANES) & (
        (idx[:, None] // QUARTER) % 2 == (idx[None, :] // QUARTER) % 2)
    bd = jnp.asarray(same, dtype=bf16)
    sink_rows_p = jnp.repeat(sinks.reshape(KV_HEADS, ATTN_GROUP), QBLOCK, axis=1)
    bias = jnp.where(jnp.arange(2 * QBLOCK)[None, None, :] == 0, sink_rows_p[:, :, None], NEG)
    qgq = _quarters(q_norm[0]).reshape(1, LANES)
    kgq = _quarters(k_norm[0]).reshape(1, LANES)
    pos_p = np.arange(tp, dtype=np.float32)
    cos_p, sin_p = _rope_tables_quarters(pos_p)
    cos_w, sin_w = _rope_tables(pos_p[tp - WINDOW:])
    x1_p, k_p, v_p, cext_p, m_p = _prompt_mixer(
        x_prompt[0], cos_p, sin_p, wa, wm, ws, wout_b, bd, bias, anorm, qgq, kgq, kg, cos_w, sin_w,
        gbias, mnorm)
    y_p = _ffn(x1_p, fnorm, wg_b, wu_b, wd_b)

    assert nt == SAMPLE_TOKENS
    wq_s = w[:, COL_QA:COL_KA].reshape(D_MODEL, KV_HEADS, ATTN_GROUP, ATTN_HEAD_DIM)
    wq_s = wq_s.transpose(0, 2, 1, 3).reshape(D_MODEL, ATTN_WIDTH).astype(bf16)
    win_s = jnp.concatenate([wq_s, win_p[:, COL_KA:]], axis=1)
    wo_a = w_out[0][:ATTN_WIDTH].reshape(KV_HEADS, ATTN_GROUP, ATTN_HEAD_DIM, D_MODEL)
    wo_a = wo_a.transpose(1, 0, 2, 3).reshape(ATTN_WIDTH, D_MODEL).astype(bf16)
    wout_s = jnp.concatenate([wo_a, wout_b[ATTN_WIDTH:]], axis=0)
    sink_tile = jnp.broadcast_to(
        jnp.repeat(sinks.reshape(KV_HEADS, ATTN_GROUP).T.reshape(-1), tpad)[:, None],
        (ATTN_HEADS * tpad, LANES))
    cos_s, sin_s = _rope_tables(np.arange(tpad, dtype=np.float32) + np.float32(PAST_LEN))
    cos_s = np.tile(cos_s, (SAMPLE_BATCH_BLOCK, 1))
    sin_s = np.tile(sin_s, (SAMPLE_BATCH_BLOCK, 1))
    x_pad = jnp.pad(x_sample, ((0, 0), (0, tpad - nt), (0, 0)))
    ckt = cache_k[0].reshape(nb, WINDOW, KV_WIDTH).transpose(0, 2, 1)
    cvt = cache_v[0].reshape(nb, WINDOW, KV_WIDTH).transpose(0, 2, 1)
    x1_pad, nkt, nvt, c_new, n_new, m_pad = _sample_mixer(
        x_pad, ckt, cvt, state_C[0], state_n[0], state_m[0][:, :, None], cos_s, sin_s, win_s, wout_s,
        sink_tile, anorm, qg, kg, gbias, mnorm)
    y_s = _ffn(x1_pad[:, :nt].reshape(nb * nt, D_MODEL), fnorm, wg_b, wu_b, wd_b)
    m_new = m_pad[:, :nh, 0]

    new_k_s = nkt.transpose(0, 2, 1)
    new_v_s = nvt.transpose(0, 2, 1)

    kv_shape = (1, 1, WINDOW, KV_HEADS, ATTN_HEAD_DIM)
    return (
        y_p[None],
        y_s.reshape(nb, nt, D_MODEL),
        k_p.reshape(kv_shape),
        v_p.reshape(kv_shape),
        cext_p[None, None, :, :, :MLSTM_HEAD_DIM],
        cext_p[None, None, :, :, MLSTM_HEAD_DIM],
        m_p[None, None, :nh, 0],
        new_k_s.reshape(1, nb, WINDOW, KV_HEADS, ATTN_HEAD_DIM),
        new_v_s.reshape(1, nb, WINDOW, KV_HEADS, ATTN_HEAD_DIM),
        c_new.reshape(1, nb, nh, MLSTM_HEAD_DIM, MLSTM_HEAD_DIM),
        n_new.reshape(1, nb, nh, MLSTM_HEAD_DIM),
        m_new.reshape(1, nb, nh),
    )
```

```python
import jax
import jax.numpy as jnp
import numpy as np
from jax import lax
from jax.experimental import pallas as pl
from jax.experimental.pallas import tpu as pltpu

D_MODEL = 1024
PAST_LEN = 16384
ATTN_HEADS = 8
KV_HEADS = 2
ATTN_HEAD_DIM = 64
ATTN_GROUP = ATTN_HEADS // KV_HEADS
ATTN_WIDTH = ATTN_HEADS * ATTN_HEAD_DIM
KV_WIDTH = KV_HEADS * ATTN_HEAD_DIM
WINDOW = 128
ROPE_THETA = 10000.0
MLSTM_HEADS = 4
MLSTM_HEAD_DIM = 128
MLSTM_WIDTH = MLSTM_HEADS * MLSTM_HEAD_DIM
MIX_WIDTH = ATTN_WIDTH + MLSTM_WIDTH
D_FF = 2816
NORM_EPS = 1e-6

LANES = 128
SUBLANES = 8
VMEM_LIMIT = 56 * 1024 * 1024

COL_QA = 0
COL_KA = COL_QA + ATTN_WIDTH
COL_VA = COL_KA + KV_WIDTH
COL_QM = COL_VA + KV_WIDTH
COL_KM = COL_QM + MLSTM_WIDTH
COL_VM = COL_KM + MLSTM_WIDTH
COL_OM = COL_VM + MLSTM_WIDTH
COL_G = COL_OM + MLSTM_WIDTH
IN_PAD = COL_G + LANES
FG_LANE = SUBLANES

PROMPT_BLOCK = 256
QBLOCK = WINDOW
MCHUNK = 128
PROJ_CHUNK = 256
FFN_BLOCK = 512
FFN_CHUNK = 256
SAMPLE_BATCH_BLOCK = 16
SAMPLE_TOKENS = 4
NEG = -1e30

f32 = jnp.float32
bf16 = jnp.bfloat16


def _rms(x, gain):
    return x * lax.rsqrt(jnp.mean(x * x, axis=-1, keepdims=True) + NORM_EPS) * gain


def _segsum64(s, lane):
    for k in (1, 2, 4, 8, 16, 32):
        s = s + jnp.where((lane & k) != 0, pltpu.roll(s, k, 1), pltpu.roll(s, LANES - k, 1))
    return s


def _headnorm_rope(xs, gain, cos, sin_signed, lane):
    ss = _segsum64(xs * xs, lane)
    y = xs * lax.rsqrt(ss * (1.0 / ATTN_HEAD_DIM) + NORM_EPS) * gain
    partner = jnp.where((lane & 32) != 0, pltpu.roll(y, 32, 1), pltpu.roll(y, LANES - 32, 1))
    return y * cos + partner * sin_signed


def _in_proj(x, anorm, win_ref):
    h = _rms(x, anorm).astype(bf16)
    return jnp.dot(h, win_ref[...], preferred_element_type=f32)


PA_Q = 0
PA_K = PA_Q + ATTN_WIDTH
PA_V = PA_K + KV_HEADS * LANES
PA_WIDTH = PA_V + KV_HEADS * LANES
PM_Q = 0
PM_K = PM_Q + MLSTM_WIDTH
PM_V = PM_K + MLSTM_WIDTH
PM_O = PM_V + MLSTM_WIDTH
PM_G = PM_O + MLSTM_WIDTH
PM_WIDTH = PM_G + LANES
QUARTER = ATTN_HEAD_DIM // 2


def _group_sumsq(xs, bd_ref):
    x2 = xs * xs
    hi = x2.astype(bf16)
    lo = (x2 - hi.astype(f32)).astype(bf16)
    return (jnp.dot(hi, bd_ref[...], preferred_element_type=f32)
            + jnp.dot(lo, bd_ref[...], preferred_element_type=f32))


def _norm_rope_quarters(xs, ss, gain, cos, sin_signed):
    y = xs * lax.rsqrt(ss * (1.0 / ATTN_HEAD_DIM) + NORM_EPS) * gain
    return y * cos + pltpu.roll(y, LANES // 2, 1) * sin_signed


def _col_chunks(width):
    return [(c, min(c + PROJ_CHUNK, width)) for c in range(0, width, PROJ_CHUNK)]


def _mixer_inproj_jobs(x_ref, anorm_ref, wa_ref, wm_ref, za_ref, zm_ref):
    h = _rms(x_ref[...], anorm_ref[...]).astype(bf16)

    def job(w_ref, z_ref, c0, c1):
        def run():
            z_ref[:, c0:c1] = jnp.dot(h, w_ref[:, c0:c1], preferred_element_type=f32)
        return run

    return ([job(wa_ref, za_ref, c0, c1) for c0, c1 in _col_chunks(PA_WIDTH)]
            + [job(wm_ref, zm_ref, c0, c1) for c0, c1 in _col_chunks(PM_WIDTH)])


def _mixer_outproj_jobs(xs_ref, mix_ref, wout_ref, x1_ref):
    def job(c0, c1):
        def run():
            x1_ref[:, c0:c1] = xs_ref[:, c0:c1] + jnp.dot(mix_ref[...], wout_ref[:, c0:c1],
                                                          preferred_element_type=f32)
        return run

    return [job(c0, c1) for c0, c1 in _col_chunks(D_MODEL)]


def _mixer_window_out(xs_ref, anorm_ref, ws_ref, kgs_ref, coss_ref, sins_ref, kout_ref, vout_ref):
    tb = xs_ref.shape[0]
    h = _rms(xs_ref[tb - WINDOW:, :], anorm_ref[...]).astype(bf16)
    zs = jnp.dot(h, ws_ref[...], preferred_element_type=f32)
    lane_s = lax.broadcasted_iota(jnp.int32, (WINDOW, LANES), 1)
    kout_ref[...] = _headnorm_rope(zs[:, :KV_WIDTH], kgs_ref[...], coss_ref[...], sins_ref[...], lane_s)
    vout_ref[...] = zs[:, KV_WIDTH:]


def _mixer_core(first_block, last_block, fillers, za_ref, zm_ref, mix_ref, cos_ref, sin_ref, bd_ref,
                bias_ref, qg_ref, kg_ref, gbias_ref, mnorm_ref, kprev_ref, vprev_ref, cst_ref, mst_ref,
                cext_ref, mout_ref):
    tb = za_ref.shape[0]
    fillers = list(fillers)
    n_fill = len(fillers)
    slots = 1 + tb // QBLOCK + (tb // QBLOCK) * KV_HEADS + (tb // MCHUNK) * MLSTM_HEADS
    progress = [0]

    def fill():
        progress[0] += 1
        while n_fill - len(fillers) < min(n_fill, -(-n_fill * progress[0] // slots)):
            fillers.pop(0)()

    lane_t = lax.broadcasted_iota(jnp.int32, (tb, LANES), 1)
    gcol = zm_ref[:, PM_G:PM_G + LANES] + gbias_ref[...]
    acol = jnp.where(lane_t < FG_LANE, gcol, jax.nn.log_sigmoid(gcol))
    arow = acol.T
    lane8 = lax.broadcasted_iota(jnp.int32, (SUBLANES, LANES), 1)
    lane_in = lane8 & (MCHUNK - 1)
    m_prev = mst_ref[:, 0:1]
    stacks = []
    u_rows = []
    w_rows = []
    for sb in range(tb // LANES):
        ls = slice(sb * LANES, (sb + 1) * LANES)
        ig8 = arow[0:SUBLANES, ls]
        lf8 = arow[FG_LANE:FG_LANE + SUBLANES, ls]
        b8 = lf8
        k = 1
        while k < MCHUNK:
            b8 = b8 + jnp.where(lane_in >= k, pltpu.roll(b8, k, 1), 0.0)
            k *= 2
        u8 = ig8 - b8
        cm8 = u8
        k = 1
        while k < MCHUNK:
            cm8 = jnp.maximum(cm8, jnp.where(lane_in >= k, pltpu.roll(cm8, k, 1), NEG))
            k *= 2
        g8 = jnp.zeros_like(u8)
        mp8 = jnp.zeros_like(u8)
        gl8 = jnp.zeros_like(u8)
        for c in range(LANES // MCHUNK):
            in_chunk = (lane8 // MCHUNK) == c
            gc = jnp.maximum(cm8, m_prev)
            last = c * MCHUNK + MCHUNK - 1
            g_last = jnp.max(jnp.where(lane8 == last, gc, NEG), axis=1, keepdims=True)
            b_last = jnp.max(jnp.where(lane8 == last, b8, NEG), axis=1, keepdims=True)
            g8 = jnp.where(in_chunk, gc, g8)
            mp8 = jnp.where(in_chunk, m_prev, mp8)
            gl8 = jnp.where(in_chunk, g_last, gl8)
            m_prev = b_last + g_last
        a8 = jnp.exp(mp8 - g8)
        emt8 = jnp.exp(-(b8 + g8))
        aend8 = jnp.exp(mp8 - gl8)
        stacks.append(jnp.concatenate(
            [g8, a8, emt8, aend8, jnp.zeros((LANES - 4 * SUBLANES, LANES), f32)], axis=0))
        u_rows.append(u8)
        w_rows.append(jnp.exp(u8 - gl8))
    mst_ref[...] = jnp.broadcast_to(m_prev, mst_ref.shape)
    if last_block:
        mout_ref[...] = jnp.broadcast_to(m_prev, mout_ref.shape)
    colform = jnp.concatenate(stacks, axis=1).T
    fill()

    lane = lax.broadcasted_iota(jnp.int32, (QBLOCK, LANES), 1)
    head_a = ((lane // QUARTER) & 1) == 0
    low_half = lane < ATTN_HEAD_DIM
    qi = lax.broadcasted_iota(jnp.int32, (ATTN_GROUP * QBLOCK, 2 * QBLOCK), 0) & (QBLOCK - 1)
    kj = lax.broadcasted_iota(jnp.int32, (ATTN_GROUP * QBLOCK, 2 * QBLOCK), 1)
    band = (kj > qi) & (kj <= qi + QBLOCK)
    row0 = lax.broadcasted_iota(jnp.int32, (QBLOCK, LANES), 0) == 0
    ones_slab = jnp.ones((2 * QBLOCK, LANES), bf16)

    def attn_prep(qb):
        rows = slice(qb * QBLOCK, (qb + 1) * QBLOCK)
        cos = cos_ref[rows, :]
        sin = sin_ref[rows, :]
        ss = [_group_sumsq(za_ref[rows, d * 2 * LANES:(d + 1) * 2 * LANES], bd_ref)
              for d in range(PA_V // (2 * LANES))]
        slabs = []
        for j in range(PA_V // LANES):
            xs = za_ref[rows, j * LANES:(j + 1) * LANES]
            gain = qg_ref[...] if j < ATTN_WIDTH // LANES else kg_ref[...]
            slabs.append(_norm_rope_quarters(xs, ss[j // 2][:, (j % 2) * LANES:(j % 2 + 1) * LANES],
                                             gain, cos, sin))
        return slabs

    def attn_unit(qb, c, slabs, kprev, vprev):
        rows = slice(qb * QBLOCK, (qb + 1) * QBLOCK)
        kcur = slabs[ATTN_WIDTH // LANES + c].astype(bf16)
        vcur = za_ref[rows, PA_V + c * LANES:PA_V + (c + 1) * LANES].astype(bf16)
        kcat = jnp.concatenate([kprev, kcur], axis=0)
        vext = jnp.concatenate([jnp.concatenate([vprev, vcur], axis=0), ones_slab], axis=1)
        q_heads = []
        for j in (2 * c, 2 * c + 1):
            qs = slabs[j] * (ATTN_HEAD_DIM ** -0.5)
            q_heads.append(jnp.where(head_a, qs, 0.0).astype(bf16))
            q_heads.append(jnp.where(head_a, 0.0, qs).astype(bf16))
        qst = jnp.concatenate(q_heads, axis=0)
        s = lax.dot_general(qst, kcat, (((1,), (1,)), ((), ())), preferred_element_type=f32)
        valid = band & (kj >= QBLOCK) if (first_block and qb == 0) else band
        s = jnp.where(valid, s, bias_ref[c])
        p = jnp.exp(s - jnp.max(s, axis=-1, keepdims=True)).astype(bf16)
        of = jnp.dot(p, vext, preferred_element_type=f32)
        o = of[:, :LANES] / of[:, LANES:]
        for jj in range(2):
            pair = jnp.where(low_half, o[(2 * jj) * QBLOCK:(2 * jj + 1) * QBLOCK],
                             o[(2 * jj + 1) * QBLOCK:(2 * jj + 2) * QBLOCK])
            col = (2 * c + jj) * LANES
            mix_ref[rows, col:col + LANES] = pair.astype(bf16)
        return kcur, jnp.where(row0, jnp.zeros_like(vcur), vcur)

    ti = lax.broadcasted_iota(jnp.int32, (MCHUNK, MCHUNK), 0)
    si = lax.broadcasted_iota(jnp.int32, (MCHUNK, MCHUNK), 1)
    causal = si <= ti
    ones_l = jnp.ones((MCHUNK, LANES), bf16)

    def mlstm_unit(hd, c, cext):
        hcols = lambda base: slice(base + hd * MLSTM_HEAD_DIM, base + (hd + 1) * MLSTM_HEAD_DIM)
        rows = slice(c * MCHUNK, (c + 1) * MCHUNK)
        qb_ = zm_ref[rows, hcols(PM_Q)].astype(bf16)
        kf = zm_ref[rows, hcols(PM_K)] * (MLSTM_HEAD_DIM ** -0.5)
        vb = zm_ref[rows, hcols(PM_V)].astype(bf16)
        og = zm_ref[rows, hcols(PM_O)]
        g_c = colform[rows, hd:hd + 1]
        a_c = colform[rows, SUBLANES + hd:SUBLANES + hd + 1]
        emt_c = colform[rows, 2 * SUBLANES + hd:2 * SUBLANES + hd + 1]
        aend = colform[c * MCHUNK:c * MCHUNK + 1, 3 * SUBLANES + hd:3 * SUBLANES + hd + 1]
        sb, off = divmod(c * MCHUNK, LANES)
        u_r = u_rows[sb][hd:hd + 1, off:off + MCHUNK]
        w_r = w_rows[sb][hd:hd + 1, off:off + MCHUNK]
        dmat = jnp.exp(jnp.where(causal, u_r - g_c, NEG))
        smat = lax.dot_general(qb_, kf.astype(bf16), (((1,), (1,)), ((), ())),
                               preferred_element_type=f32) * dmat
        vext = jnp.concatenate([vb, ones_l], axis=1)
        nd = (a_c * jnp.dot(qb_, cext.astype(bf16), preferred_element_type=f32)
              + jnp.dot(smat.astype(bf16), vext, preferred_element_type=f32))
        hraw = nd[:, :MLSTM_HEAD_DIM] / jnp.maximum(jnp.abs(nd[:, MLSTM_HEAD_DIM:]), emt_c)
        hn = _rms(hraw, mnorm_ref[hd:hd + 1, :])
        mix_ref[rows, ATTN_WIDTH + hd * MLSTM_HEAD_DIM:ATTN_WIDTH + (hd + 1) * MLSTM_HEAD_DIM] = (
            (hn * jax.nn.sigmoid(og)).astype(bf16))
        kw_t = (kf.T * w_r).astype(bf16)
        return aend * cext + jnp.dot(kw_t, vext, preferred_element_type=f32)

    nqb = tb // QBLOCK
    nch = tb // MCHUNK
    preps = []
    for qb in range(nqb):
        preps.append(attn_prep(qb))
        fill()
    kv_state = [(kprev_ref[c], vprev_ref[c]) for c in range(KV_HEADS)]
    cexts = [cst_ref[hd] for hd in range(MLSTM_HEADS)]
    attn_jobs = [(qb, c) for qb in range(nqb) for c in range(KV_HEADS)]
    mlstm_jobs = [(hd, c) for c in range(nch) for hd in range(MLSTM_HEADS)]
    while attn_jobs or mlstm_jobs:
        if attn_jobs:
            qb, c = attn_jobs.pop(0)
            kv_state[c] = attn_unit(qb, c, preps[qb], *kv_state[c])
            fill()
        for _ in range(2 if len(mlstm_jobs) > 2 * len(attn_jobs) else 1):
            if mlstm_jobs:
                hd, c = mlstm_jobs.pop(0)
                cexts[hd] = mlstm_unit(hd, c, cexts[hd])
                fill()
    assert not fillers
    for c in range(KV_HEADS):
        kprev_ref[c], vprev_ref[c] = kv_state[c]
    for hd in range(MLSTM_HEADS):
        cst_ref[hd] = cexts[hd]
        if last_block:
            cext_ref[hd] = cexts[hd]


def _prompt_mixer_kernel(x_ref, cos_ref, sin_ref, wa_ref, wm_ref, ws_ref, wout_ref, bd_ref, bias_ref,
                         anorm_ref, qg_ref, kg_ref, kgs_ref, coss_ref, sins_ref, gbias_ref, mnorm_ref,
                         x1_ref, kout_ref, vout_ref, cext_ref, mout_ref,
                         za0, za1, zm0, zm1, mix0, mix1, xs0, xs1, kprev_ref, vprev_ref, cst_ref, mst_ref):
    step = pl.program_id(0)
    nblk = pl.num_programs(0) - 2
    za, zm, mix, xs = (za0, za1), (zm0, zm1), (mix0, mix1), (xs0, xs1)

    def run(parity, do_in, do_core, do_out, first_block=False, last_block=False):
        jobs = []
        if do_out:
            jobs += _mixer_outproj_jobs(xs[parity], mix[parity], wout_ref, x1_ref)
        if do_in:
            jobs += _mixer_inproj_jobs(x_ref, anorm_ref, wa_ref, wm_ref, za[parity], zm[parity])
        if do_core:
            _mixer_core(first_block, last_block, jobs, za[1 - parity], zm[1 - parity], mix[1 - parity],
                        cos_ref, sin_ref, bd_ref, bias_ref, qg_ref, kg_ref, gbias_ref, mnorm_ref,
                        kprev_ref, vprev_ref, cst_ref, mst_ref, cext_ref, mout_ref)
        else:
            for job in jobs:
                job()
        if last_block:
            _mixer_window_out(xs[1 - parity], anorm_ref, ws_ref, kgs_ref, coss_ref, sins_ref,
                              kout_ref, vout_ref)
        if do_in:
            xs[parity][...] = x_ref[...]

    @pl.when(step == 0)
    def _first():
        kprev_ref[...] = jnp.zeros_like(kprev_ref)
        vprev_ref[...] = jnp.zeros_like(vprev_ref)
        cst_ref[...] = jnp.zeros_like(cst_ref)
        mst_ref[...] = jnp.zeros_like(mst_ref)
        run(0, True, False, False)

    @pl.when(step == 1)
    def _second():
        run(1, True, True, False, first_block=True)

    steady = (step >= 2) & (step < nblk)

    @pl.when(steady & (step % 2 == 0))
    def _even():
        run(0, True, True, True)

    @pl.when(steady & (step % 2 == 1))
    def _odd():
        run(1, True, True, True)

    @pl.when(step == nblk)
    def _drain_core():
        run(0, False, True, True, last_block=True)

    @pl.when(step == nblk + 1)
    def _drain_out():
        run(1, False, False, True)


def _const_spec(shape):
    nd = len(shape)
    return pl.BlockSpec(shape, lambda i, *_: (0,) * nd)


def _prompt_mixer(x, cos, sin, wa, wm, ws, wout_b, bd, bias, anorm, qg, kg, kgs, coss, sins, gbias, mnorm):
    t = x.shape[0]
    tb = PROMPT_BLOCK
    nblk = t // tb
    assert nblk % 2 == 0 and nblk >= 4
    state_shape = (MLSTM_HEADS, MLSTM_HEAD_DIM, 2 * MLSTM_HEAD_DIM)
    last = nblk - 1
    return pl.pallas_call(
        _prompt_mixer_kernel,
        grid=(nblk + 2,),
        in_specs=[
            pl.BlockSpec((tb, D_MODEL), lambda i: (jnp.minimum(i, last), 0)),
            pl.BlockSpec((tb, LANES), lambda i: (jnp.clip(i - 1, 0, last), 0)),
            pl.BlockSpec((tb, LANES), lambda i: (jnp.clip(i - 1, 0, last), 0)),
            _const_spec((D_MODEL, PA_WIDTH)),
            _const_spec((D_MODEL, PM_WIDTH)),
            _const_spec((D_MODEL, 2 * KV_WIDTH)),
            _const_spec((MIX_WIDTH, D_MODEL)),
            _const_spec((2 * LANES, 2 * LANES)),
            _const_spec((KV_HEADS, ATTN_GROUP * QBLOCK, 2 * QBLOCK)),
            _const_spec((1, D_MODEL)),
            _const_spec((1, LANES)),
            _const_spec((1, LANES)),
            _const_spec((1, LANES)),
            _const_spec((WINDOW, LANES)),
            _const_spec((WINDOW, LANES)),
            _const_spec((1, LANES)),
            _const_spec((MLSTM_HEADS, MLSTM_HEAD_DIM)),
        ],
        out_specs=[
            pl.BlockSpec((tb, D_MODEL), lambda i: (jnp.clip(i - 2, 0, last), 0)),
            _const_spec((WINDOW, KV_WIDTH)),
            _const_spec((WINDOW, KV_WIDTH)),
            _const_spec(state_shape),
            _const_spec((SUBLANES, LANES)),
        ],
        out_shape=[
            jax.ShapeDtypeStruct((t, D_MODEL), f32),
            jax.ShapeDtypeStruct((WINDOW, KV_WIDTH), f32),
            jax.ShapeDtypeStruct((WINDOW, KV_WIDTH), f32),
            jax.ShapeDtypeStruct(state_shape, f32),
            jax.ShapeDtypeStruct((SUBLANES, LANES), f32),
        ],
        scratch_shapes=(
            [pltpu.VMEM((tb, PA_WIDTH), f32)] * 2 + [pltpu.VMEM((tb, PM_WIDTH), f32)] * 2
            + [pltpu.VMEM((tb, MIX_WIDTH), bf16)] * 2 + [pltpu.VMEM((tb, D_MODEL), f32)] * 2
            + [pltpu.VMEM((KV_HEADS, WINDOW, LANES), bf16)] * 2
            + [pltpu.VMEM(state_shape, f32), pltpu.VMEM((SUBLANES, LANES), f32)]),
        compiler_params=pltpu.CompilerParams(
            dimension_semantics=("arbitrary",), vmem_limit_bytes=VMEM_LIMIT),
        name="prompt_mixer",
    )(x, cos, sin, wa, wm, ws, wout_b, bd, bias, anorm, qg, kg, kgs, coss, sins, gbias, mnorm)


def _ffn_kernel(xp_ref, xs_ref, g_ref, wg_ref, wu_ref, wd_ref, op_ref, os_ref):
    step = pl.program_id(0)
    last = pl.num_programs(0) - 1

    @pl.when(step < last)
    def _prompt_rows():
        _ffn_rows(xp_ref, g_ref, wg_ref, wu_ref, wd_ref, op_ref)

    @pl.when(step == last)
    def _sample_rows():
        _ffn_rows(xs_ref, g_ref, wg_ref, wu_ref, wd_ref, os_ref)


def _ffn_rows(x_ref, g_ref, wg_ref, wu_ref, wd_ref, o_ref):
    x = x_ref[...]
    hf = _rms(x, g_ref[...]).astype(bf16)
    acc = x
    for c in range(D_FF // FFN_CHUNK):
        cs = slice(c * FFN_CHUNK, (c + 1) * FFN_CHUNK)
        gate = jnp.dot(hf, wg_ref[:, cs], preferred_element_type=f32)
        up = jnp.dot(hf, wu_ref[:, cs], preferred_element_type=f32)
        act = (gate * jax.nn.sigmoid(gate) * up).astype(bf16)
        acc = acc + jnp.dot(act, wd_ref[cs, :], preferred_element_type=f32)
    o_ref[...] = acc


def _ffn(x_p, x_s, fnorm, wg_b, wu_b, wd_b):
    n = x_p.shape[0]
    ns = x_s.shape[0]
    tm = FFN_BLOCK
    last = n // tm - 1
    return pl.pallas_call(
        _ffn_kernel,
        grid=(n // tm + 1,),
        in_specs=[
            pl.BlockSpec((tm, D_MODEL), lambda i: (jnp.minimum(i, last), 0)),
            _const_spec((ns, D_MODEL)),
            _const_spec((1, D_MODEL)),
            _const_spec((D_MODEL, D_FF)),
            _const_spec((D_MODEL, D_FF)),
            _const_spec((D_FF, D_MODEL)),
        ],
        out_specs=[pl.BlockSpec((tm, D_MODEL), lambda i: (jnp.minimum(i, last), 0)),
                   _const_spec((ns, D_MODEL))],
        out_shape=[jax.ShapeDtypeStruct((n, D_MODEL), f32), jax.ShapeDtypeStruct((ns, D_MODEL), f32)],
        compiler_params=pltpu.CompilerParams(
            dimension_semantics=("arbitrary",), vmem_limit_bytes=VMEM_LIMIT),
        name="ffn",
    )(x_p, x_s, fnorm, wg_b, wu_b, wd_b)


def _sample_proj_kernel(x_ref, cos_ref, sin_ref, win_ref, anorm_ref, qg_ref, kg_ref, gbias_ref,
                        q_ref, k_ref, v_ref, zm_ref, gate_ref):
    n = x_ref.shape[0]
    z = _in_proj(x_ref[...], anorm_ref[...], win_ref)
    lane = lax.broadcasted_iota(jnp.int32, (n, LANES), 1)
    cos = cos_ref[...]
    sin = sin_ref[...]
    for j in range(ATTN_WIDTH // LANES):
        qs = _headnorm_rope(z[:, COL_QA + j * LANES:COL_QA + (j + 1) * LANES], qg_ref[...], cos, sin, lane)
        q_ref[:, j * LANES:(j + 1) * LANES] = qs * (ATTN_HEAD_DIM ** -0.5)
    k_ref[...] = _headnorm_rope(z[:, COL_KA:COL_KA + KV_WIDTH], kg_ref[...], cos, sin, lane)
    v_ref[...] = z[:, COL_VA:COL_VA + KV_WIDTH]
    zm_ref[:, 0:MLSTM_WIDTH] = z[:, COL_QM:COL_KM]
    zm_ref[:, MLSTM_WIDTH:2 * MLSTM_WIDTH] = z[:, COL_KM:COL_VM] * (MLSTM_HEAD_DIM ** -0.5)
    zm_ref[:, 2 * MLSTM_WIDTH:4 * MLSTM_WIDTH] = z[:, COL_VM:COL_G]
    gcol = z[:, COL_G:COL_G + LANES] + gbias_ref[...]
    gate_ref[...] = jnp.where(lane < FG_LANE, gcol, jax.nn.log_sigmoid(gcol))


def _sample_proj(x, cos, sin, win_p, anorm, qg, kg, gbias):
    n = x.shape[0]
    full = lambda shape: pl.BlockSpec(shape, lambda i: (0,) * len(shape))
    return pl.pallas_call(
        _sample_proj_kernel,
        grid=(1,),
        in_specs=[full((n, D_MODEL)), full((n, LANES)), full((n, LANES)), full((D_MODEL, IN_PAD)),
                  full((1, D_MODEL)), full((1, LANES)), full((1, LANES)), full((1, LANES))],
        out_specs=[full((n, ATTN_WIDTH)), full((n, KV_WIDTH)), full((n, KV_WIDTH)),
                   full((n, 4 * MLSTM_WIDTH)), full((n, LANES))],
        out_shape=[jax.ShapeDtypeStruct((n, ATTN_WIDTH), f32),
                   jax.ShapeDtypeStruct((n, KV_WIDTH), f32),
                   jax.ShapeDtypeStruct((n, KV_WIDTH), f32),
                   jax.ShapeDtypeStruct((n, 4 * MLSTM_WIDTH), f32),
                   jax.ShapeDtypeStruct((n, LANES), f32)],
        compiler_params=pltpu.CompilerParams(vmem_limit_bytes=VMEM_LIMIT),
        name="sample_proj",
    )(x, cos, sin, win_p, anorm, qg, kg, gbias)


def _sample_core_kernel(qbd_ref, ckt_ref, cvt_ref, kn_ref, vn_ref, knt_ref, vnt_ref, sink_ref,
                        qm_ref, km_ref, vm_ref, gp_ref, gr_ref, c_ref, n_ref, m_ref,
                        o_ref, h_ref, cn_ref, nn_ref, mn_ref, nkt_ref, nvt_ref):
    bb = qbd_ref.shape[0]
    nrow = qbd_ref.shape[1]
    tpad = kn_ref.shape[1]
    nreal = nrow // ATTN_HEADS
    qbd = qbd_ref[...].astype(bf16)
    ckt = ckt_ref[...]
    cvt = cvt_ref[...]
    zpad_k = jnp.zeros((bb, LANES - tpad, LANES), bf16)
    kn = jnp.concatenate([kn_ref[...].astype(bf16), zpad_k], axis=1)
    vn = jnp.concatenate([vn_ref[...].astype(bf16), zpad_k], axis=1)
    s = jnp.concatenate(
        [jnp.einsum('bqd,bdw->bqw', qbd, ckt.astype(bf16), preferred_element_type=f32),
         jnp.einsum('bqd,bkd->bqk', qbd, kn, preferred_element_type=f32)], axis=2)
    lane3 = lax.broadcasted_iota(jnp.int32, ckt.shape, 2)
    keep = lane3 < WINDOW - nreal
    nkt_ref[...] = jnp.where(keep, pltpu.roll(ckt, WINDOW - nreal, 2), knt_ref[...])
    nvt_ref[...] = jnp.where(keep, pltpu.roll(cvt, WINDOW - nreal, 2), vnt_ref[...])
    tq = lax.broadcasted_iota(jnp.int32, s.shape, 1) & (nreal - 1)
    kj = lax.broadcasted_iota(jnp.int32, s.shape, 2)
    valid = ((kj < WINDOW) & (kj > tq)) | ((kj >= WINDOW) & ((kj - WINDOW) <= tq))
    s = jnp.where(valid, s, NEG)
    sink = sink_ref[:, 0:1][None]
    mx = jnp.maximum(jnp.max(s, axis=-1, keepdims=True), sink)
    p = jnp.exp(s - mx)
    den = jnp.sum(p, axis=-1, keepdims=True) + jnp.exp(sink - mx)
    pb = p.astype(bf16)
    o = (jnp.einsum('bqw,bdw->bqd', pb[:, :, :WINDOW], cvt.astype(bf16), preferred_element_type=f32)
         + jnp.einsum('bqk,bkd->bqd', pb[:, :, WINDOW:], vn, preferred_element_type=f32))
    o_ref[...] = o / den

    q = qm_ref[...]
    k = km_ref[...]
    v = vm_ref[...]
    gp = gp_ref[...]
    gr = gr_ref[...]
    c0 = c_ref[...]
    n0 = n_ref[...]
    m0 = m_ref[...]
    ig_c = gp[:, :, 0:1]
    lf_c = gp[:, :, 1:2]
    ig_r = gr[:, 0:1, :]
    lf_r = gr[:, 1:2, :]
    ti = lax.broadcasted_iota(jnp.int32, q.shape, 1)
    si = lax.broadcasted_iota(jnp.int32, q.shape, 2)
    tri = si <= ti
    b_c = jnp.sum(jnp.where(tri, lf_r, 0.0), axis=2, keepdims=True)
    b_r = jnp.sum(jnp.where(ti <= si, lf_c, 0.0), axis=1, keepdims=True)
    dlog = jnp.where(tri, b_c - b_r + ig_r, NEG)
    inter = b_c + m0
    m_t = jnp.maximum(inter, jnp.max(dlog, axis=2, keepdims=True))
    dmat = jnp.exp(dlog - m_t)
    smat = jnp.zeros_like(q)
    for sx in range(nreal):
        col = jnp.sum(q * k[:, sx:sx + 1, :], axis=2, keepdims=True)
        smat = jnp.where(si == sx, col, smat)
    smat = smat * dmat
    a = jnp.exp(inter - m_t)
    qc = jnp.einsum('gtd,gde->gte', q.astype(bf16), c0.astype(bf16), preferred_element_type=f32)
    intra = jnp.zeros_like(q)
    for sx in range(nreal):
        intra = intra + smat[:, :, sx:sx + 1] * v[:, sx:sx + 1, :]
    num = a * qc + intra
    qn = jnp.sum(q * n0, axis=2, keepdims=True)
    den_m = a * qn + jnp.sum(smat, axis=2, keepdims=True)
    h_ref[...] = num / jnp.maximum(jnp.abs(den_m), jnp.exp(-m_t))
    last = nreal - 1
    m_new = m_t[:, last:last + 1, :]
    b_last = b_c[:, last:last + 1, :]
    a_end = jnp.exp(b_last + m0 - m_new)
    w_c = jnp.exp(b_last - b_c + ig_c - m_new)
    kw = k * w_c
    upd = jnp.einsum('gsd,gse->gde', kw.astype(bf16), v.astype(bf16), preferred_element_type=f32)
    cn_ref[...] = a_end * c0 + upd
    nn_ref[...] = a_end * n0 + jnp.sum(kw, axis=1, keepdims=True)
    mn_ref[...] = m_new


def _sample_core(qbd, ckt, cvt, kn, vn, knt, vnt, sink_rows, qm, km, vm, gp, gr, c0, n0, m0):
    nb = qbd.shape[0]
    bb = SAMPLE_BATCH_BLOCK
    gb = bb * MLSTM_HEADS
    ng = nb * MLSTM_HEADS
    nrow = qbd.shape[1]
    tpad = kn.shape[1]
    blk = lambda shape: pl.BlockSpec(shape, lambda i: (i,) + (0,) * (len(shape) - 1))
    return pl.pallas_call(
        _sample_core_kernel,
        grid=(nb // bb,),
        in_specs=[blk((bb, nrow, LANES)), blk((bb, KV_WIDTH, WINDOW)), blk((bb, KV_WIDTH, WINDOW)),
                  blk((bb, tpad, KV_WIDTH)), blk((bb, tpad, KV_WIDTH)),
                  blk((bb, KV_WIDTH, WINDOW)), blk((bb, KV_WIDTH, WINDOW)),
                  pl.BlockSpec((nrow, LANES), lambda i: (0, 0)),
                  blk((gb, tpad, MLSTM_HEAD_DIM)), blk((gb, tpad, MLSTM_HEAD_DIM)),
                  blk((gb, tpad, MLSTM_HEAD_DIM)), blk((gb, tpad, LANES)), blk((gb, tpad, LANES)),
                  blk((gb, MLSTM_HEAD_DIM, MLSTM_HEAD_DIM)), blk((gb, 1, MLSTM_HEAD_DIM)),
                  blk((gb, 1, 1))],
        out_specs=[blk((bb, nrow, LANES)), blk((gb, tpad, MLSTM_HEAD_DIM)),
                   blk((gb, MLSTM_HEAD_DIM, MLSTM_HEAD_DIM)), blk((gb, 1, MLSTM_HEAD_DIM)),
                   blk((gb, 1, 1)), blk((bb, KV_WIDTH, WINDOW)), blk((bb, KV_WIDTH, WINDOW))],
        out_shape=[jax.ShapeDtypeStruct((nb, nrow, LANES), f32),
                   jax.ShapeDtypeStruct((ng, tpad, MLSTM_HEAD_DIM), f32),
                   jax.ShapeDtypeStruct((ng, MLSTM_HEAD_DIM, MLSTM_HEAD_DIM), f32),
                   jax.ShapeDtypeStruct((ng, 1, MLSTM_HEAD_DIM), f32),
                   jax.ShapeDtypeStruct((ng, 1, 1), f32),
                   jax.ShapeDtypeStruct((nb, KV_WIDTH, WINDOW), f32),
                   jax.ShapeDtypeStruct((nb, KV_WIDTH, WINDOW), f32)],
        compiler_params=pltpu.CompilerParams(
            dimension_semantics=("arbitrary",), vmem_limit_bytes=VMEM_LIMIT),
        name="sample_core",
    )(qbd, ckt, cvt, kn, vn, knt, vnt, sink_rows, qm, km, vm, gp, gr, c0, n0, m0)


def _sample_out_kernel(x_ref, ao_ref, hm_ref, om_ref, mnorm_ref, wout_ref, x1_ref):
    parts = [ao_ref[...].astype(bf16)]
    for hd in range(MLSTM_HEADS):
        cs = slice(hd * MLSTM_HEAD_DIM, (hd + 1) * MLSTM_HEAD_DIM)
        hn = _rms(hm_ref[:, cs], mnorm_ref[hd:hd + 1, :])
        parts.append((hn * jax.nn.sigmoid(om_ref[:, cs])).astype(bf16))
    mix = jnp.concatenate(parts, axis=1)
    x1_ref[...] = x_ref[...] + jnp.dot(mix, wout_ref[...], preferred_element_type=f32)


def _sample_out(x, ao, hm, om, mnorm, wout_b):
    n = x.shape[0]
    full = lambda shape: pl.BlockSpec(shape, lambda i: (0,) * len(shape))
    return pl.pallas_call(
        _sample_out_kernel,
        grid=(1,),
        in_specs=[full((n, D_MODEL)), full((n, ATTN_WIDTH)), full((n, MLSTM_WIDTH)),
                  full((n, MLSTM_WIDTH)), full((MLSTM_HEADS, MLSTM_HEAD_DIM)),
                  full((MIX_WIDTH, D_MODEL))],
        out_specs=full((n, D_MODEL)),
        out_shape=jax.ShapeDtypeStruct((n, D_MODEL), f32),
        compiler_params=pltpu.CompilerParams(vmem_limit_bytes=VMEM_LIMIT),
        name="sample_out",
    )(x, ao, hm, om, mnorm, wout_b)


def _sample_mixer_kernel(x_ref, ckt_ref, cvt_ref, c_ref, n_ref, m_ref, cos_ref, sin_ref, win_ref, wout_ref,
                         sink_ref, bd_ref, anorm_ref, qg_ref, kg_ref, gbias_ref, mnorm_ref,
                         x1_ref, nkt_ref, nvt_ref, cn_ref, nn_ref, mn_ref):
    bb, tpad, _ = x_ref.shape
    nrows = bb * tpad
    nreal = SAMPLE_TOKENS
    z = _in_proj(x_ref[...].reshape(nrows, D_MODEL), anorm_ref[...], win_ref)
    lane = lax.broadcasted_iota(jnp.int32, (nrows, LANES), 1)
    low = lane < ATTN_HEAD_DIM
    cos = cos_ref[...]
    sin = sin_ref[...]

    def per_seq(a):
        return a.reshape(bb, tpad, a.shape[-1])

    def norm_rope(xs, gain):
        y = xs * lax.rsqrt(_group_sumsq(xs, bd_ref) * (1.0 / ATTN_HEAD_DIM) + NORM_EPS) * gain
        partner = jnp.where((lane & QUARTER) != 0, pltpu.roll(y, QUARTER, 1),
                            pltpu.roll(y, LANES - QUARTER, 1))
        return y * cos + partner * sin

    q_rows = []
    for j in range(ATTN_GROUP):
        qs = norm_rope(z[:, COL_QA + j * LANES:COL_QA + (j + 1) * LANES], qg_ref[...])
        qs = qs * (ATTN_HEAD_DIM ** -0.5)
        q_rows.append(per_seq(jnp.where(low, qs, 0.0)).astype(bf16))
        q_rows.append(per_seq(jnp.where(low, 0.0, qs)).astype(bf16))
    qbd = jnp.concatenate(q_rows, axis=1)
    knew = norm_rope(z[:, COL_KA:COL_KA + KV_WIDTH], kg_ref[...])
    vnew = z[:, COL_VA:COL_VA + KV_WIDTH]
    zpad = jnp.zeros((bb, LANES - tpad, LANES), bf16)
    knp = jnp.concatenate([per_seq(knew).astype(bf16), zpad], axis=1)
    vnp = jnp.concatenate([per_seq(vnew).astype(bf16), zpad], axis=1)
    ckt = ckt_ref[...]
    cvt = cvt_ref[...]
    s = jnp.concatenate(
        [jnp.einsum('bqd,bdw->bqw', qbd, ckt.astype(bf16), preferred_element_type=f32),
         jnp.einsum('bqd,bkd->bqk', qbd, knp, preferred_element_type=f32)], axis=2)
    tq = lax.broadcasted_iota(jnp.int32, s.shape, 1) & (tpad - 1)
    kj = lax.broadcasted_iota(jnp.int32, s.shape, 2)
    valid = ((kj < WINDOW) & (kj > tq)) | ((kj >= WINDOW) & (kj - WINDOW <= tq) & (kj - WINDOW < nreal))
    s = jnp.where(valid, s, NEG)
    sink = sink_ref[:, 0:1][None]
    mx = jnp.maximum(jnp.max(s, axis=-1, keepdims=True), sink)
    p = jnp.exp(s - mx)
    den = jnp.sum(p, axis=-1, keepdims=True) + jnp.exp(sink - mx)
    pb = p.astype(bf16)
    o = (jnp.einsum('bqw,bdw->bqd', pb[:, :, :WINDOW], cvt.astype(bf16), preferred_element_type=f32)
         + jnp.einsum('bqk,bkd->bqd', pb[:, :, WINDOW:], vnp, preferred_element_type=f32)) / den
    low3 = lax.broadcasted_iota(jnp.int32, (bb, tpad, LANES), 2) < ATTN_HEAD_DIM
    mix_parts = []
    for j in range(ATTN_GROUP):
        r0 = 2 * j * tpad
        pair = jnp.where(low3, o[:, r0:r0 + tpad, :], o[:, r0 + tpad:r0 + 2 * tpad, :])
        mix_parts.append(pair.reshape(nrows, LANES).astype(bf16))

    keep = lax.broadcasted_iota(jnp.int32, (KV_WIDTH, WINDOW), 1) < WINDOW - nreal
    knt = knew.T
    vnt = vnew.T
    for b in range(bb):
        shift = (WINDOW - nreal - b * tpad) % LANES
        nkt_ref[b] = jnp.where(keep, pltpu.roll(ckt_ref[b], WINDOW - nreal, 1), pltpu.roll(knt, shift, 1))
        nvt_ref[b] = jnp.where(keep, pltpu.roll(cvt_ref[b], WINDOW - nreal, 1), pltpu.roll(vnt, shift, 1))

    gz = per_seq(z[:, COL_G:COL_G + LANES] + gbias_ref[...])
    lgz = jax.nn.log_sigmoid(gz)
    trow = lax.broadcasted_iota(jnp.int32, (bb, tpad, 1), 1)
    real = trow < nreal
    mn_ref[...] = jnp.zeros_like(mn_ref)
    for hd in range(MLSTM_HEADS):
        hcols = lambda base: slice(base + hd * MLSTM_HEAD_DIM, base + (hd + 1) * MLSTM_HEAD_DIM)
        q = per_seq(z[:, hcols(COL_QM)])
        k = per_seq(z[:, hcols(COL_KM)]) * (MLSTM_HEAD_DIM ** -0.5)
        v = per_seq(z[:, hcols(COL_VM)])
        og = per_seq(z[:, hcols(COL_OM)])
        c0 = c_ref[:, hd]
        n0 = n_ref[:, hd:hd + 1, :]
        m0 = m_ref[:, hd:hd + 1, :]
        ig_c = jnp.where(real, gz[:, :, hd:hd + 1], NEG)
        lf_c = jnp.where(real, lgz[:, :, FG_LANE + hd:FG_LANE + hd + 1], 0.0)
        b_c = jnp.zeros_like(lf_c)
        for sx in range(nreal):
            b_c = b_c + jnp.where(trow >= sx, lf_c[:, sx:sx + 1, :], 0.0)
        dlog = [jnp.where(trow >= sx, b_c - b_c[:, sx:sx + 1, :] + ig_c[:, sx:sx + 1, :], NEG)
                for sx in range(nreal)]
        inter = b_c + m0
        m_t = inter
        for sx in range(nreal):
            m_t = jnp.maximum(m_t, dlog[sx])
        a = jnp.exp(inter - m_t)
        qc = jnp.einsum('btd,bde->bte', q.astype(bf16), c0.astype(bf16), preferred_element_type=f32)
        num = a * qc
        den_m = a * jnp.sum(q * n0, axis=2, keepdims=True)
        for sx in range(nreal):
            sd = jnp.sum(q * k[:, sx:sx + 1, :], axis=2, keepdims=True) * jnp.exp(dlog[sx] - m_t)
            num = num + sd * v[:, sx:sx + 1, :]
            den_m = den_m + sd
        hraw = num / jnp.maximum(jnp.abs(den_m), jnp.exp(-m_t))
        hn = _rms(hraw, mnorm_ref[hd:hd + 1, :][None])
        mix_parts.append((hn * jax.nn.sigmoid(og)).reshape(nrows, MLSTM_HEAD_DIM).astype(bf16))
        last = nreal - 1
        m_new = m_t[:, last:last + 1, :]
        b_last = b_c[:, last:last + 1, :]
        a_end = jnp.exp(b_last + m0 - m_new)
        kw = k * jnp.exp(b_last - b_c + ig_c - m_new)
        cn_ref[:, hd] = a_end * c0 + jnp.einsum('bsd,bse->bde', kw.astype(bf16), v.astype(bf16),
                                               preferred_element_type=f32)
        nn_ref[:, hd:hd + 1, :] = a_end * n0 + jnp.sum(kw, axis=1, keepdims=True)
        mn_ref[:, hd:hd + 1, :] = jnp.broadcast_to(m_new, (bb, 1, LANES))

    mix = jnp.concatenate(mix_parts, axis=1)
    x1 = x_ref[...].reshape(nrows, D_MODEL) + jnp.dot(mix, wout_ref[...], preferred_element_type=f32)
    x1_ref[...] = x1.reshape(bb, tpad, D_MODEL)


def _sample_mixer(x_pad, ckt, cvt, c0, n0, m0, cos, sin, win_s, wout_s, sink_tile, bd, anorm, qg, kg,
                  gbias, mnorm):
    nb, tpad, _ = x_pad.shape
    bb = SAMPLE_BATCH_BLOCK
    nh = MLSTM_HEADS
    blk = lambda shape: pl.BlockSpec(shape, lambda i: (i,) + (0,) * (len(shape) - 1))
    cblk = (bb, nh, MLSTM_HEAD_DIM, MLSTM_HEAD_DIM)
    return pl.pallas_call(
        _sample_mixer_kernel,
        grid=(nb // bb,),
        in_specs=[blk((bb, tpad, D_MODEL)), blk((bb, KV_WIDTH, WINDOW)), blk((bb, KV_WIDTH, WINDOW)),
                  blk(cblk), blk((bb, nh, MLSTM_HEAD_DIM)), blk((bb, nh, 1)),
                  _const_spec((bb * tpad, LANES)), _const_spec((bb * tpad, LANES)),
                  _const_spec((D_MODEL, IN_PAD)), _const_spec((MIX_WIDTH, D_MODEL)),
                  _const_spec((ATTN_HEADS * tpad, LANES)), _const_spec((LANES, LANES)),
                  _const_spec((1, D_MODEL)),
                  _const_spec((1, LANES)), _const_spec((1, LANES)), _const_spec((1, LANES)),
                  _const_spec((nh, MLSTM_HEAD_DIM))],
        out_specs=[blk((bb, tpad, D_MODEL)), blk((bb, KV_WIDTH, WINDOW)), blk((bb, KV_WIDTH, WINDOW)),
                   blk(cblk), blk((bb, nh, MLSTM_HEAD_DIM)), blk((bb, tpad, LANES))],
        out_shape=[jax.ShapeDtypeStruct((nb, tpad, D_MODEL), f32),
                   jax.ShapeDtypeStruct((nb, KV_WIDTH, WINDOW), f32),
                   jax.ShapeDtypeStruct((nb, KV_WIDTH, WINDOW), f32),
                   jax.ShapeDtypeStruct((nb,) + cblk[1:], f32),
                   jax.ShapeDtypeStruct((nb, nh, MLSTM_HEAD_DIM), f32),
                   jax.ShapeDtypeStruct((nb, tpad, LANES), f32)],
        compiler_params=pltpu.CompilerParams(
            dimension_semantics=("arbitrary",), vmem_limit_bytes=VMEM_LIMIT),
        name="sample_mixer",
    )(x_pad, ckt, cvt, c0, n0, m0, cos, sin, win_s, wout_s, sink_tile, bd, anorm, qg, kg, gbias, mnorm)


def _rope_angles(pos):
    half = ATTN_HEAD_DIM // 2
    inv = ROPE_THETA ** (-np.arange(half, dtype=np.float64) / half)
    ang = pos.astype(np.float64)[:, None] * inv[None, :]
    return np.cos(ang).astype(np.float32), np.sin(ang).astype(np.float32)


def _rope_tables(pos):
    c, s = _rope_angles(pos)
    cos = np.tile(c, (1, LANES // QUARTER))
    sin = np.tile(np.concatenate([-s, s], axis=1), (1, LANES // ATTN_HEAD_DIM))
    return cos, sin


def _rope_tables_quarters(pos):
    c, s = _rope_angles(pos)
    return np.tile(c, (1, LANES // QUARTER)), np.concatenate([-s, -s, s, s], axis=1)


def _quarters(a):
    lo, hi = a[..., :QUARTER], a[..., QUARTER:]
    return jnp.concatenate([lo, lo, hi, hi], axis=-1)


def _prompt_attn_weights(w):
    d = w.shape[0]
    wq = w[:, COL_QA:COL_KA].reshape(d, ATTN_WIDTH // LANES, 2, 2, QUARTER)
    wq = wq.transpose(0, 1, 3, 2, 4).reshape(d, ATTN_WIDTH)
    wk = _quarters(w[:, COL_KA:COL_VA].reshape(d, KV_HEADS, ATTN_HEAD_DIM)).reshape(d, KV_HEADS * LANES)
    wv = w[:, COL_VA:COL_QM].reshape(d, KV_HEADS, 1, ATTN_HEAD_DIM)
    wv = jnp.broadcast_to(wv, (d, KV_HEADS, 2, ATTN_HEAD_DIM)).reshape(d, KV_HEADS * LANES)
    return jnp.concatenate([wq, wk, wv], axis=1)


def _to_head_major(a, nb, nt, nh, tpad, fill=0.0):
    d = a.shape[1] // nh
    a = a.reshape(nb, nt, nh, d).transpose(0, 2, 1, 3).reshape(nb * nh, nt, d)
    return jnp.pad(a, ((0, 0), (0, tpad - nt), (0, 0)), constant_values=fill)


def kernel(x_prompt, x_sample, cache_k, cache_v, state_C, state_n, state_m, attn_norm, w_in, q_norm,
           k_norm, attn_sinks, b_ig, b_fg, mlstm_norm, w_out, ffn_norm, w_gate, w_up, w_down):
    assert w_in.shape[0] == 1 and x_prompt.shape[0] == 1
    tp = x_prompt.shape[1]
    nb, nt = x_sample.shape[0], x_sample.shape[1]
    tpad = SUBLANES
    nh = MLSTM_HEADS

    w = w_in[0].astype(bf16)
    wz = jnp.zeros((D_MODEL, FG_LANE - nh), bf16)
    win_p = jnp.concatenate(
        [w[:, :COL_G + nh], wz, w[:, COL_G + nh:], jnp.zeros((D_MODEL, LANES - FG_LANE - nh), bf16)],
        axis=1)
    gbias = jnp.concatenate(
        [b_ig[0], jnp.zeros((FG_LANE - nh,), f32), b_fg[0], jnp.zeros((LANES - FG_LANE - nh,), f32)]
    ).reshape(1, LANES)
    wout_b = w_out[0].astype(bf16)
    wg_b = w_gate[0].astype(bf16)
    wu_b = w_up[0].astype(bf16)
    wd_b = w_down[0].astype(bf16)
    anorm = attn_norm[0].reshape(1, D_MODEL)
    fnorm = ffn_norm[0].reshape(1, D_MODEL)
    qg = jnp.tile(q_norm[0], LANES // ATTN_HEAD_DIM).reshape(1, LANES)
    kg = jnp.tile(k_norm[0], LANES // ATTN_HEAD_DIM).reshape(1, LANES)
    mnorm = mlstm_norm[0].reshape(nh, MLSTM_HEAD_DIM)
    sinks = attn_sinks[0]

    wa = _prompt_attn_weights(w).astype(bf16)
    wm = win_p[:, COL_QM:]
    ws = win_p[:, COL_KA:COL_QM]
    idx = np.arange(2 * LANES)
    same = (idx[:, None] // LANES == idx[None, :] // LANES) & (
        (idx[:, None] // QUARTER) % 2 == (idx[None, :] // QUARTER) % 2)
    bd = jnp.asarray(same, dtype=bf16)
    sink_rows_p = jnp.repeat(sinks.reshape(KV_HEADS, ATTN_GROUP), QBLOCK, axis=1)
    bias = jnp.where(jnp.arange(2 * QBLOCK)[None, None, :] == 0, sink_rows_p[:, :, None], NEG)
    qgq = _quarters(q_norm[0]).reshape(1, LANES)
    kgq = _quarters(k_norm[0]).reshape(1, LANES)
    pos_p = np.arange(tp, dtype=np.float32)
    cos_p, sin_p = _rope_tables_quarters(pos_p)
    cos_w, sin_w = _rope_tables(pos_p[tp - WINDOW:])
    x1_p, k_p, v_p, cext_p, m_p = _prompt_mixer(
        x_prompt[0], cos_p, sin_p, wa, wm, ws, wout_b, bd, bias, anorm, qgq, kgq, kg, cos_w, sin_w,
        gbias, mnorm)

    assert nt == SAMPLE_TOKENS
    wq_s = w[:, COL_QA:COL_KA].reshape(D_MODEL, KV_HEADS, ATTN_GROUP, ATTN_HEAD_DIM)
    wq_s = wq_s.transpose(0, 2, 1, 3).reshape(D_MODEL, ATTN_WIDTH).astype(bf16)
    win_s = jnp.concatenate([wq_s, win_p[:, COL_KA:]], axis=1)
    wo_a = w_out[0][:ATTN_WIDTH].reshape(KV_HEADS, ATTN_GROUP, ATTN_HEAD_DIM, D_MODEL)
    wo_a = wo_a.transpose(1, 0, 2, 3).reshape(ATTN_WIDTH, D_MODEL).astype(bf16)
    wout_s = jnp.concatenate([wo_a, wout_b[ATTN_WIDTH:]], axis=0)
    sink_tile = jnp.broadcast_to(
        jnp.repeat(sinks.reshape(KV_HEADS, ATTN_GROUP).T.reshape(-1), tpad)[:, None],
        (ATTN_HEADS * tpad, LANES))
    lanes = np.arange(LANES)
    bd_s = jnp.asarray(lanes[:, None] // ATTN_HEAD_DIM == lanes[None, :] // ATTN_HEAD_DIM, dtype=bf16)
    cos_s, sin_s = _rope_tables(np.arange(tpad, dtype=np.float32) + np.float32(PAST_LEN))
    cos_s = np.tile(cos_s, (SAMPLE_BATCH_BLOCK, 1))
    sin_s = np.tile(sin_s, (SAMPLE_BATCH_BLOCK, 1))
    x_pad = jnp.pad(x_sample, ((0, 0), (0, tpad - nt), (0, 0)))
    ckt = cache_k[0].reshape(nb, WINDOW, KV_WIDTH).transpose(0, 2, 1)
    cvt = cache_v[0].reshape(nb, WINDOW, KV_WIDTH).transpose(0, 2, 1)
    x1_pad, nkt, nvt, c_new, n_new, m_pad = _sample_mixer(
        x_pad, ckt, cvt, state_C[0], state_n[0], state_m[0][:, :, None], cos_s, sin_s, win_s, wout_s,
        sink_tile, bd_s, anorm, qg, kg, gbias, mnorm)
    y_p, y_s = _ffn(x1_p, x1_pad[:, :nt].reshape(nb * nt, D_MODEL), fnorm, wg_b, wu_b, wd_b)
    m_new = m_pad[:, :nh, 0]

    new_k_s = nkt.transpose(0, 2, 1)
    new_v_s = nvt.transpose(0, 2, 1)

    kv_shape = (1, 1, WINDOW, KV_HEADS, ATTN_HEAD_DIM)
    return (
        y_p[None],
        y_s.reshape(nb, nt, D_MODEL),
        k_p.reshape(kv_shape),
        v_p.reshape(kv_shape),
        cext_p[None, None, :, :, :MLSTM_HEAD_DIM],
        cext_p[None, None, :, :, MLSTM_HEAD_DIM],
        m_p[None, None, :nh, 0],
        new_k_s.reshape(1, nb, WINDOW, KV_HEADS, ATTN_HEAD_DIM),
        new_v_s.reshape(1, nb, WINDOW, KV_HEADS, ATTN_HEAD_DIM),
        c_new.reshape(1, nb, nh, MLSTM_HEAD_DIM, MLSTM_HEAD_DIM),
        n_new.reshape(1, nb, nh, MLSTM_HEAD_DIM),
        m_new.reshape(1, nb, nh),
    )
```

```python
import jax
import jax.numpy as jnp
import numpy as np
from jax import lax
from jax.experimental import pallas as pl
from jax.experimental.pallas import tpu as pltpu

D_MODEL = 1024
PAST_LEN = 16384
ATTN_HEADS = 8
KV_HEADS = 2
ATTN_HEAD_DIM = 64
ATTN_GROUP = ATTN_HEADS // KV_HEADS
ATTN_WIDTH = ATTN_HEADS * ATTN_HEAD_DIM
KV_WIDTH = KV_HEADS * ATTN_HEAD_DIM
WINDOW = 128
ROPE_THETA = 10000.0
MLSTM_HEADS = 4
MLSTM_HEAD_DIM = 128
MLSTM_WIDTH = MLSTM_HEADS * MLSTM_HEAD_DIM
MIX_WIDTH = ATTN_WIDTH + MLSTM_WIDTH
D_FF = 2816
NORM_EPS = 1e-6

LANES = 128
SUBLANES = 8
VMEM_LIMIT = 56 * 1024 * 1024

COL_QA = 0
COL_KA = COL_QA + ATTN_WIDTH
COL_VA = COL_KA + KV_WIDTH
COL_QM = COL_VA + KV_WIDTH
COL_KM = COL_QM + MLSTM_WIDTH
COL_VM = COL_KM + MLSTM_WIDTH
COL_OM = COL_VM + MLSTM_WIDTH
COL_G = COL_OM + MLSTM_WIDTH
IN_WIDTH = COL_G + 2 * MLSTM_HEADS
FG_LANE = SUBLANES

PROMPT_BLOCK = 256
QBLOCK = WINDOW
MCHUNK = 128
PROJ_CHUNK = 256
FFN_BLOCK = 512
FFN_CHUNK = 256
SAMPLE_BATCH_BLOCK = 16
SAMPLE_TOKENS = 4
NEG = -1e30

f32 = jnp.float32
bf16 = jnp.bfloat16


def _rms(x, gain):
    return x * lax.rsqrt(jnp.mean(x * x, axis=-1, keepdims=True) + NORM_EPS) * gain


def _segsum64(s, lane):
    for k in (1, 2, 4, 8, 16, 32):
        s = s + jnp.where((lane & k) != 0, pltpu.roll(s, k, 1), pltpu.roll(s, LANES - k, 1))
    return s


def _headnorm_rope(xs, gain, cos, sin_signed, lane):
    ss = _segsum64(xs * xs, lane)
    y = xs * lax.rsqrt(ss * (1.0 / ATTN_HEAD_DIM) + NORM_EPS) * gain
    partner = jnp.where((lane & 32) != 0, pltpu.roll(y, 32, 1), pltpu.roll(y, LANES - 32, 1))
    return y * cos + partner * sin_signed


def _group_sumsq(xs, bd_ref):
    x2 = xs * xs
    hi = x2.astype(bf16)
    lo = (x2 - hi.astype(f32)).astype(bf16)
    return (jnp.dot(hi, bd_ref[...], preferred_element_type=f32)
            + jnp.dot(lo, bd_ref[...], preferred_element_type=f32))


PA_Q = 0
PA_K = PA_Q + ATTN_WIDTH
PA_V = PA_K + KV_HEADS * LANES
PA_WIDTH = PA_V + KV_HEADS * LANES
QUARTER = ATTN_HEAD_DIM // 2
QA_Q = 0
QA_K = QA_Q + 2 * ATTN_WIDTH
QA_V = QA_K + KV_HEADS * LANES
QA_VZ = QA_V + KV_HEADS * LANES
QA_WIDTH = QA_VZ + KV_HEADS * LANES
ZM_WIDTH = 4 * MLSTM_WIDTH


def _norm_rope_quarters(xs, ss, gain, cos, sin_signed):
    y = xs * lax.rsqrt(ss * (1.0 / ATTN_HEAD_DIM) + NORM_EPS) * gain
    return y * cos + pltpu.roll(y, LANES // 2, 1) * sin_signed


def _col_chunks(width):
    return [(c, min(c + PROJ_CHUNK, width)) for c in range(0, width, PROJ_CHUNK)]


def _mixer_proj_jobs(x_ref, anorm_ref, wa_ref, w_ref, wgate_ref, za_ref, zm_ref, gz_ref):
    h = _rms(x_ref[...], anorm_ref[...]).astype(bf16)

    def proj_job(w_src, wc0, z_ref, c0, c1):
        def run():
            z_ref[:, c0:c1] = jnp.dot(h, w_src[:, wc0 + c0:wc0 + c1], preferred_element_type=f32)
        return run

    return ([proj_job(wa_ref, 0, za_ref, c0, c1) for c0, c1 in _col_chunks(PA_WIDTH)]
            + [proj_job(w_ref, COL_QM, zm_ref, c0, c1) for c0, c1 in _col_chunks(ZM_WIDTH)]
            + [proj_job(wgate_ref, 0, gz_ref, 0, LANES)])


def _mixer_prep_jobs(za_ref, gz_ref, cos_ref, sin_ref, bd_ref, qg_ref, kg_ref, gbias_ref, qa_ref,
                     colf_ref, urow_ref, wrow_ref, mst_ref, mout_ref):
    tb = za_ref.shape[0]

    def gates_job():
        lane_t = lax.broadcasted_iota(jnp.int32, (tb, LANES), 1)
        gcol = gz_ref[...] + gbias_ref[...]
        acol = jnp.where(lane_t < FG_LANE, gcol, jax.nn.log_sigmoid(gcol))
        arow = acol.T
        lane8 = lax.broadcasted_iota(jnp.int32, (SUBLANES, LANES), 1)
        lane_in = lane8 & (MCHUNK - 1)
        m_prev = mst_ref[:, 0:1]
        stacks = []
        for sb in range(tb // LANES):
            ls = slice(sb * LANES, (sb + 1) * LANES)
            ig8 = arow[0:SUBLANES, ls]
            lf8 = arow[FG_LANE:FG_LANE + SUBLANES, ls]
            b8 = lf8
            k = 1
            while k < MCHUNK:
                b8 = b8 + jnp.where(lane_in >= k, pltpu.roll(b8, k, 1), 0.0)
                k *= 2
            u8 = ig8 - b8
            cm8 = u8
            k = 1
            while k < MCHUNK:
                cm8 = jnp.maximum(cm8, jnp.where(lane_in >= k, pltpu.roll(cm8, k, 1), NEG))
                k *= 2
            g8 = jnp.zeros_like(u8)
            mp8 = jnp.zeros_like(u8)
            gl8 = jnp.zeros_like(u8)
            for c in range(LANES // MCHUNK):
                in_chunk = (lane8 // MCHUNK) == c
                gc = jnp.maximum(cm8, m_prev)
                last = c * MCHUNK + MCHUNK - 1
                g_last = jnp.max(jnp.where(lane8 == last, gc, NEG), axis=1, keepdims=True)
                b_last = jnp.max(jnp.where(lane8 == last, b8, NEG), axis=1, keepdims=True)
                g8 = jnp.where(in_chunk, gc, g8)
                mp8 = jnp.where(in_chunk, m_prev, mp8)
                gl8 = jnp.where(in_chunk, g_last, gl8)
                m_prev = b_last + g_last
            a8 = jnp.exp(mp8 - g8)
            emt8 = jnp.exp(-(b8 + g8))
            aend8 = jnp.exp(mp8 - gl8)
            stacks.append(jnp.concatenate(
                [g8, a8, emt8, aend8, jnp.zeros((LANES - 4 * SUBLANES, LANES), f32)], axis=0))
            urow_ref[:, ls] = u8
            wrow_ref[:, ls] = jnp.exp(u8 - gl8)
        mst_ref[...] = jnp.broadcast_to(m_prev, mst_ref.shape)
        mout_ref[...] = jnp.broadcast_to(m_prev, mout_ref.shape)
        colf_ref[...] = jnp.concatenate(stacks, axis=1).T

    def prep_job(qb):
        def run():
            rows = slice(qb * QBLOCK, (qb + 1) * QBLOCK)
            lane = lax.broadcasted_iota(jnp.int32, (QBLOCK, LANES), 1)
            head_a = ((lane // QUARTER) & 1) == 0
            row0 = lax.broadcasted_iota(jnp.int32, (QBLOCK, LANES), 0) == 0
            cos = cos_ref[rows, :]
            sin = sin_ref[rows, :]
            ss = [_group_sumsq(za_ref[rows, d * 2 * LANES:(d + 1) * 2 * LANES], bd_ref)
                  for d in range(PA_V // (2 * LANES))]
            for j in range(PA_V // LANES):
                is_q = j < ATTN_WIDTH // LANES
                y = _norm_rope_quarters(za_ref[rows, j * LANES:(j + 1) * LANES],
                                        ss[j // 2][:, (j % 2) * LANES:(j % 2 + 1) * LANES],
                                        qg_ref[...] if is_q else kg_ref[...], cos, sin)
                if is_q:
                    y = y * (ATTN_HEAD_DIM ** -0.5)
                    qa_ref[rows, QA_Q + 2 * j * LANES:QA_Q + (2 * j + 1) * LANES] = (
                        jnp.where(head_a, y, 0.0).astype(bf16))
                    qa_ref[rows, QA_Q + (2 * j + 1) * LANES:QA_Q + (2 * j + 2) * LANES] = (
                        jnp.where(head_a, 0.0, y).astype(bf16))
                else:
                    c = j - ATTN_WIDTH // LANES
                    qa_ref[rows, QA_K + c * LANES:QA_K + (c + 1) * LANES] = y.astype(bf16)
            for c in range(KV_HEADS):
                v = za_ref[rows, PA_V + c * LANES:PA_V + (c + 1) * LANES]
                qa_ref[rows, QA_V + c * LANES:QA_V + (c + 1) * LANES] = v.astype(bf16)
                qa_ref[rows, QA_VZ + c * LANES:QA_VZ + (c + 1) * LANES] = (
                    jnp.where(row0, 0.0, v).astype(bf16))
        return run

    return [gates_job] + [prep_job(qb) for qb in range(tb // QBLOCK)]


def _mixer_outproj_jobs(xs_ref, mix_ref, wout_ref, x1_ref):
    def job(c0, c1):
        def run():
            x1_ref[:, c0:c1] = xs_ref[:, c0:c1] + jnp.dot(mix_ref[...], wout_ref[:, c0:c1],
                                                          preferred_element_type=f32)
        return run

    return [job(c0, c1) for c0, c1 in _col_chunks(D_MODEL)]


def _mixer_window_out(xs_ref, anorm_ref, w_ref, kgs_ref, coss_ref, sins_ref, kout_ref, vout_ref):
    tb = xs_ref.shape[0]
    h = _rms(xs_ref[tb - WINDOW:, :], anorm_ref[...]).astype(bf16)
    zs = jnp.dot(h, w_ref[:, COL_KA:COL_QM], preferred_element_type=f32)
    lane_s = lax.broadcasted_iota(jnp.int32, (WINDOW, LANES), 1)
    kout_ref[...] = _headnorm_rope(zs[:, :KV_WIDTH], kgs_ref[...], coss_ref[...], sins_ref[...], lane_s)
    vout_ref[...] = zs[:, KV_WIDTH:]


def _mixer_core(first_block, last_block, fillers, prep_jobs, qa_ref, zm_ref, colf_ref, urow_ref, wrow_ref,
                mix_ref, bias_ref, mnorm_ref, kprev_ref, vprev_ref, cst_ref, cext_ref):
    tb = qa_ref.shape[0]
    nqb = tb // QBLOCK
    nch = tb // MCHUNK
    fillers = list(fillers)
    n_fill = len(fillers)
    slots = len(prep_jobs) + nqb * KV_HEADS + nch * MLSTM_HEADS
    progress = [0]

    def fill():
        progress[0] += 1
        while n_fill - len(fillers) < min(n_fill, -(-n_fill * progress[0] // slots)):
            fillers.pop(0)()

    for job in prep_jobs:
        job()
        fill()

    low_half = lax.broadcasted_iota(jnp.int32, (QBLOCK, LANES), 1) < ATTN_HEAD_DIM
    qi = lax.broadcasted_iota(jnp.int32, (ATTN_GROUP * QBLOCK, 2 * QBLOCK), 0) & (QBLOCK - 1)
    kj = lax.broadcasted_iota(jnp.int32, (ATTN_GROUP * QBLOCK, 2 * QBLOCK), 1)
    band = (kj > qi) & (kj <= qi + QBLOCK)
    ones_slab = jnp.ones((2 * QBLOCK, LANES), bf16)

    def attn_unit(qb, c):
        rows = slice(qb * QBLOCK, (qb + 1) * QBLOCK)
        kcols = slice(QA_K + c * LANES, QA_K + (c + 1) * LANES)
        if qb == 0:
            kprev, vprev = kprev_ref[c], vprev_ref[c]
        else:
            prev = slice((qb - 1) * QBLOCK, qb * QBLOCK)
            kprev = qa_ref[prev, kcols]
            vprev = qa_ref[prev, QA_VZ + c * LANES:QA_VZ + (c + 1) * LANES]
        kcat = jnp.concatenate([kprev, qa_ref[rows, kcols]], axis=0)
        vcat = jnp.concatenate([vprev, qa_ref[rows, QA_V + c * LANES:QA_V + (c + 1) * LANES]], axis=0)
        vext = jnp.concatenate([vcat, ones_slab], axis=1)
        q0 = QA_Q + c * ATTN_GROUP * LANES
        qst = jnp.concatenate([qa_ref[rows, q0 + g * LANES:q0 + (g + 1) * LANES]
                               for g in range(ATTN_GROUP)], axis=0)
        s = lax.dot_general(qst, kcat, (((1,), (1,)), ((), ())), preferred_element_type=f32)
        valid = band & (kj >= QBLOCK) if (first_block and qb == 0) else band
        s = jnp.where(valid, s, bias_ref[c])
        p = jnp.exp(s - jnp.max(s, axis=-1, keepdims=True)).astype(bf16)
        of = jnp.dot(p, vext, preferred_element_type=f32)
        o = of[:, :LANES] / of[:, LANES:]
        for jj in range(2):
            pair = jnp.where(low_half, o[(2 * jj) * QBLOCK:(2 * jj + 1) * QBLOCK],
                             o[(2 * jj + 1) * QBLOCK:(2 * jj + 2) * QBLOCK])
            col = (2 * c + jj) * LANES
            mix_ref[rows, col:col + LANES] = pair.astype(bf16)

    ti = lax.broadcasted_iota(jnp.int32, (MCHUNK, MCHUNK), 0)
    si = lax.broadcasted_iota(jnp.int32, (MCHUNK, MCHUNK), 1)
    causal = si <= ti
    ones_l = jnp.ones((MCHUNK, LANES), bf16)

    def mlstm_unit(hd, c, cext):
        hcols = lambda k: slice((k * MLSTM_HEADS + hd) * MLSTM_HEAD_DIM,
                                (k * MLSTM_HEADS + hd + 1) * MLSTM_HEAD_DIM)
        rows = slice(c * MCHUNK, (c + 1) * MCHUNK)
        qb_ = zm_ref[rows, hcols(0)].astype(bf16)
        kf = zm_ref[rows, hcols(1)] * (MLSTM_HEAD_DIM ** -0.5)
        vb = zm_ref[rows, hcols(2)].astype(bf16)
        og = zm_ref[rows, hcols(3)]
        g_c = colf_ref[rows, hd:hd + 1]
        a_c = colf_ref[rows, SUBLANES + hd:SUBLANES + hd + 1]
        emt_c = colf_ref[rows, 2 * SUBLANES + hd:2 * SUBLANES + hd + 1]
        aend = colf_ref[c * MCHUNK:c * MCHUNK + 1, 3 * SUBLANES + hd:3 * SUBLANES + hd + 1]
        u_r = urow_ref[hd:hd + 1, rows]
        w_r = wrow_ref[hd:hd + 1, rows]
        dmat = jnp.exp(jnp.where(causal, u_r - g_c, NEG))
        smat = lax.dot_general(qb_, kf.astype(bf16), (((1,), (1,)), ((), ())),
                               preferred_element_type=f32) * dmat
        vext = jnp.concatenate([vb, ones_l], axis=1)
        nd = (a_c * jnp.dot(qb_, cext.astype(bf16), preferred_element_type=f32)
              + jnp.dot(smat.astype(bf16), vext, preferred_element_type=f32))
        hraw = nd[:, :MLSTM_HEAD_DIM] / jnp.maximum(jnp.abs(nd[:, MLSTM_HEAD_DIM:]), emt_c)
        hn = _rms(hraw, mnorm_ref[hd:hd + 1, :])
        mix_ref[rows, ATTN_WIDTH + hd * MLSTM_HEAD_DIM:ATTN_WIDTH + (hd + 1) * MLSTM_HEAD_DIM] = (
            (hn * jax.nn.sigmoid(og)).astype(bf16))
        kw_t = (kf.T * w_r).astype(bf16)
        return aend * cext + jnp.dot(kw_t, vext, preferred_element_type=f32)

    cexts =[cst_ref[hd] for hd in range(MLSTM_HEADS)]
    attn_jobs = [(qb, c) for qb in range(nqb) for c in range(KV_HEADS)]
    mlstm_jobs = [(hd, c) for c in range(nch) for hd in range(MLSTM_HEADS)]
    while attn_jobs or mlstm_jobs:
        if attn_jobs:
            attn_unit(*attn_jobs.pop(0))
            fill()
        for _ in range(2 if len(mlstm_jobs) > 2 * len(attn_jobs) else 1):
            if mlstm_jobs:
                hd, c = mlstm_jobs.pop(0)
                cexts[hd] = mlstm_unit(hd, c, cexts[hd])
                fill()
    assert not fillers
    last_rows = slice(tb - QBLOCK, tb)
    for c in range(KV_HEADS):
        kprev_ref[c] = qa_ref[last_rows, QA_K + c * LANES:QA_K + (c + 1) * LANES]
        vprev_ref[c] = qa_ref[last_rows, QA_VZ + c * LANES:QA_VZ + (c + 1) * LANES]
    for hd in range(MLSTM_HEADS):
        cst_ref[hd] = cexts[hd]
        if last_block:
            cext_ref[hd] = cexts[hd]


def _prompt_mixer_kernel(x_ref, cos_ref, sin_ref, wa_ref, w_ref, wgate_ref, wout_ref, bd_ref, bias_ref,
                         anorm_ref, qg_ref, kg_ref, kgs_ref, coss_ref, sins_ref, gbias_ref, mnorm_ref,
                         x1_ref, kout_ref, vout_ref, cext_ref, mout_ref,
                         za0, za1, zm0, zm1, gz0, gz1, mix0, mix1, xs0, xs1,
                         qa_ref, colf_ref, urow_ref, wrow_ref, kprev_ref, vprev_ref, cst_ref, mst_ref):
    step = pl.program_id(0)
    nblk = pl.num_programs(0) - 2
    za, zm, gz, mix, xs = (za0, za1), (zm0, zm1), (gz0, gz1), (mix0, mix1), (xs0, xs1)

    def run(parity, do_in, do_core, do_out, first_block=False, last_block=False):
        other = 1 - parity
        jobs = []
        if do_out:
            jobs += _mixer_outproj_jobs(xs[parity], mix[parity], wout_ref, x1_ref)
        if do_in:
            jobs += _mixer_proj_jobs(x_ref, anorm_ref, wa_ref, w_ref, wgate_ref, za[parity], zm[parity],
                                     gz[parity])
        if do_core:
            prep = _mixer_prep_jobs(za[other], gz[other], cos_ref, sin_ref, bd_ref, qg_ref, kg_ref,
                                    gbias_ref, qa_ref, colf_ref, urow_ref, wrow_ref, mst_ref, mout_ref)
            _mixer_core(first_block, last_block, jobs, prep, qa_ref, zm[other], colf_ref, urow_ref,
                        wrow_ref, mix[other], bias_ref, mnorm_ref, kprev_ref, vprev_ref, cst_ref, cext_ref)
        else:
            for job in jobs:
                job()
        if last_block:
            _mixer_window_out(xs[other], anorm_ref, w_ref, kgs_ref, coss_ref, sins_ref, kout_ref, vout_ref)
        if do_in:
            xs[parity][...] = x_ref[...]

    @pl.when(step == 0)
    def _first():
        kprev_ref[...] = jnp.zeros_like(kprev_ref)
        vprev_ref[...] = jnp.zeros_like(vprev_ref)
        cst_ref[...] = jnp.zeros_like(cst_ref)
        mst_ref[...] = jnp.zeros_like(mst_ref)
        run(0, True, False, False)

    @pl.when(step == 1)
    def _second():
        run(1, True, True, False, first_block=True)

    steady = (step >= 2) & (step < nblk)

    @pl.when(steady & (step % 2 == 0))
    def _even():
        run(0, True, True, True)

    @pl.when(steady & (step % 2 == 1))
    def _odd():
        run(1, True, True, True)

    @pl.when(step == nblk)
    def _drain_core():
        run(0, False, True, True, last_block=True)

    @pl.when(step == nblk + 1)
    def _drain_out():
        run(1, False, False, True)


def _const_spec(shape, single=False):
    nd = len(shape)
    if single:
        return pl.BlockSpec(shape, lambda i, *_: (0,) * nd, pipeline_mode=pl.Buffered(1))
    return pl.BlockSpec(shape, lambda i, *_: (0,) * nd)


def _prompt_mixer(x, cos, sin, wa, w, wgate, wout_b, bd, bias, anorm, qg, kg, kgs, coss, sins, gbias, mnorm):
    t = x.shape[0]
    tb = PROMPT_BLOCK
    nblk = t // tb
    assert nblk % 2 == 0 and nblk >= 4
    state_shape = (MLSTM_HEADS, MLSTM_HEAD_DIM, 2 * MLSTM_HEAD_DIM)
    last = nblk - 1
    cur = lambda i: (jnp.minimum(i, last), 0)
    return pl.pallas_call(
        _prompt_mixer_kernel,
        grid=(nblk + 2,),
        in_specs=[
            pl.BlockSpec((tb, D_MODEL), cur),
            pl.BlockSpec((tb, LANES), lambda i: (jnp.clip(i - 1, 0, last), 0)),
            pl.BlockSpec((tb, LANES), lambda i: (jnp.clip(i - 1, 0, last), 0)),
            _const_spec((D_MODEL, PA_WIDTH), single=True),
            _const_spec((D_MODEL, IN_WIDTH), single=True),
            _const_spec((D_MODEL, LANES), single=True),
            _const_spec((MIX_WIDTH, D_MODEL), single=True),
            _const_spec((2 * LANES, 2 * LANES), single=True),
            _const_spec((KV_HEADS, ATTN_GROUP * QBLOCK, 2 * QBLOCK), single=True),
            _const_spec((1, D_MODEL)),
            _const_spec((1, LANES)),
            _const_spec((1, LANES)),
            _const_spec((1, LANES)),
            _const_spec((WINDOW, LANES)),
            _const_spec((WINDOW, LANES)),
            _const_spec((1, LANES)),
            _const_spec((MLSTM_HEADS, MLSTM_HEAD_DIM)),
        ],
        out_specs=[
            pl.BlockSpec((tb, D_MODEL), lambda i: (jnp.clip(i - 2, 0, last), 0)),
            _const_spec((WINDOW, KV_WIDTH)),
            _const_spec((WINDOW, KV_WIDTH)),
            _const_spec(state_shape),
            _const_spec((SUBLANES, LANES)),
        ],
        out_shape=[
            jax.ShapeDtypeStruct((t, D_MODEL), f32),
            jax.ShapeDtypeStruct((WINDOW, KV_WIDTH), f32),
            jax.ShapeDtypeStruct((WINDOW, KV_WIDTH), f32),
            jax.ShapeDtypeStruct(state_shape, f32),
            jax.ShapeDtypeStruct((SUBLANES, LANES), f32),
        ],
        scratch_shapes=(
            [pltpu.VMEM((tb, PA_WIDTH), f32)] * 2 + [pltpu.VMEM((tb, ZM_WIDTH), f32)] * 2
            + [pltpu.VMEM((tb, LANES), f32)] * 2
            + [pltpu.VMEM((tb, MIX_WIDTH), bf16)] * 2 + [pltpu.VMEM((tb, D_MODEL), f32)] * 2
            + [pltpu.VMEM((tb, QA_WIDTH), bf16), pltpu.VMEM((tb, LANES), f32)]
            + [pltpu.VMEM((SUBLANES, tb), f32)] * 2
            + [pltpu.VMEM((KV_HEADS, WINDOW, LANES), bf16)] * 2
            + [pltpu.VMEM(state_shape, f32), pltpu.VMEM((SUBLANES, LANES), f32)]),
        compiler_params=pltpu.CompilerParams(
            dimension_semantics=("arbitrary",), vmem_limit_bytes=VMEM_LIMIT),
        name="prompt_mixer",
    )(x, cos, sin, wa, w, wgate, wout_b, bd, bias, anorm, qg, kg, kgs, coss, sins, gbias, mnorm)


def _ffn_kernel(xp_ref, xs_ref, g_ref, wg_ref, wu_ref, wd_ref, op_ref, os_ref):
    step = pl.program_id(0)
    last = pl.num_programs(0) - 1

    @pl.when(step < last)
    def _prompt_rows():
        _ffn_rows(xp_ref, g_ref, wg_ref, wu_ref, wd_ref, op_ref)

    @pl.when(step == last)
    def _sample_rows():
        _ffn_rows(xs_ref, g_ref, wg_ref, wu_ref, wd_ref, os_ref)


def _ffn_rows(x_ref, g_ref, wg_ref, wu_ref, wd_ref, o_ref):
    x = x_ref[...]
    hf = _rms(x, g_ref[...]).astype(bf16)
    acc = x
    for c in range(D_FF // FFN_CHUNK):
        cs = slice(c * FFN_CHUNK, (c + 1) * FFN_CHUNK)
        gate = jnp.dot(hf, wg_ref[:, cs], preferred_element_type=f32)
        up = jnp.dot(hf, wu_ref[:, cs], preferred_element_type=f32)
        act = (gate * jax.nn.sigmoid(gate) * up).astype(bf16)
        acc = acc + jnp.dot(act, wd_ref[cs, :], preferred_element_type=f32)
    o_ref[...] = acc


def _ffn(x_p, x_s, fnorm, wg_b, wu_b, wd_b):
    n = x_p.shape[0]
    ns = x_s.shape[0]
    tm = FFN_BLOCK
    last = n // tm - 1
    return pl.pallas_call(
        _ffn_kernel,
        grid=(n // tm + 1,),
        in_specs=[
            pl.BlockSpec((tm, D_MODEL), lambda i: (jnp.minimum(i, last), 0)),
            _const_spec((ns, D_MODEL), single=True),
            _const_spec((1, D_MODEL)),
            _const_spec((D_MODEL, D_FF), single=True),
            _const_spec((D_MODEL, D_FF), single=True),
            _const_spec((D_FF, D_MODEL), single=True),
        ],
        out_specs=[pl.BlockSpec((tm, D_MODEL), lambda i: (jnp.minimum(i, last), 0)),
                   _const_spec((ns, D_MODEL))],
        out_shape=[jax.ShapeDtypeStruct((n, D_MODEL), f32), jax.ShapeDtypeStruct((ns, D_MODEL), f32)],
        compiler_params=pltpu.CompilerParams(
            dimension_semantics=("arbitrary",), vmem_limit_bytes=VMEM_LIMIT),
        name="ffn",
    )(x_p, x_s, fnorm, wg_b, wu_b, wd_b)


def _sample_mixer_kernel(x_ref, ckt_ref, cvt_ref, c_ref, n_ref, m_ref, cos_ref, sin_ref, wqs_ref, w_ref,
                         wgate_ref, wout_ref, sink_ref, bd_ref, anorm_ref, qg_ref, kg_ref, gbias_ref,
                         mnorm_ref, x1_ref, nkt_ref, nvt_ref, cn_ref, nn_ref, mn_ref):
    bb, tpad, _ = x_ref.shape
    nrows = bb * tpad
    nreal = SAMPLE_TOKENS
    h = _rms(x_ref[...].reshape(nrows, D_MODEL), anorm_ref[...]).astype(bf16)
    z = jnp.concatenate(
        [jnp.dot(h, wqs_ref[...], preferred_element_type=f32),
         jnp.dot(h, w_ref[:, COL_KA:COL_G], preferred_element_type=f32),
         jnp.dot(h, wgate_ref[...], preferred_element_type=f32)], axis=1)
    lane = lax.broadcasted_iota(jnp.int32, (nrows, LANES), 1)
    low = lane < ATTN_HEAD_DIM
    cos = cos_ref[...]
    sin = sin_ref[...]

    def per_seq(a):
        return a.reshape(bb, tpad, a.shape[-1])

    def norm_rope(xs, gain):
        y = xs * lax.rsqrt(_group_sumsq(xs, bd_ref) * (1.0 / ATTN_HEAD_DIM) + NORM_EPS) * gain
        partner = jnp.where((lane & QUARTER) != 0, pltpu.roll(y, QUARTER, 1),
                            pltpu.roll(y, LANES - QUARTER, 1))
        return y * cos + partner * sin

    q_rows = []
    for j in range(ATTN_GROUP):
        qs = norm_rope(z[:, COL_QA + j * LANES:COL_QA + (j + 1) * LANES], qg_ref[...])
        qs = qs * (ATTN_HEAD_DIM ** -0.5)
        q_rows.append(per_seq(jnp.where(low, qs, 0.0)).astype(bf16))
        q_rows.append(per_seq(jnp.where(low, 0.0, qs)).astype(bf16))
    qbd = jnp.concatenate(q_rows, axis=1)
    knew = norm_rope(z[:, COL_KA:COL_KA + KV_WIDTH], kg_ref[...])
    vnew = z[:, COL_VA:COL_VA + KV_WIDTH]
    zpad = jnp.zeros((bb, LANES - tpad, LANES), bf16)
    knp = jnp.concatenate([per_seq(knew).astype(bf16), zpad], axis=1)
    vnp = jnp.concatenate([per_seq(vnew).astype(bf16), zpad], axis=1)
    ckt = ckt_ref[...]
    cvt = cvt_ref[...]
    s = jnp.concatenate(
        [jnp.einsum('bqd,bdw->bqw', qbd, ckt.astype(bf16), preferred_element_type=f32),
         jnp.einsum('bqd,bkd->bqk', qbd, knp, preferred_element_type=f32)], axis=2)
    tq = lax.broadcasted_iota(jnp.int32, s.shape, 1) & (tpad - 1)
    kj = lax.broadcasted_iota(jnp.int32, s.shape, 2)
    valid = ((kj < WINDOW) & (kj > tq)) | ((kj >= WINDOW) & (kj - WINDOW <= tq) & (kj - WINDOW < nreal))
    s = jnp.where(valid, s, NEG)
    sink = sink_ref[:, 0:1][None]
    mx = jnp.maximum(jnp.max(s, axis=-1, keepdims=True), sink)
    p = jnp.exp(s - mx)
    den = jnp.sum(p, axis=-1, keepdims=True) + jnp.exp(sink - mx)
    pb = p.astype(bf16)
    o = (jnp.einsum('bqw,bdw->bqd', pb[:, :, :WINDOW], cvt.astype(bf16), preferred_element_type=f32)
         + jnp.einsum('bqk,bkd->bqd', pb[:, :, WINDOW:], vnp, preferred_element_type=f32)) / den
    low3 = lax.broadcasted_iota(jnp.int32, (bb, tpad, LANES), 2) < ATTN_HEAD_DIM
    mix_parts = []
    for j in range(ATTN_GROUP):
        r0 = 2 * j * tpad
        pair = jnp.where(low3, o[:, r0:r0 + tpad, :], o[:, r0 + tpad:r0 + 2 * tpad, :])
        mix_parts.append(pair.reshape(nrows, LANES).astype(bf16))

    keep = lax.broadcasted_iota(jnp.int32, (KV_WIDTH, WINDOW), 1) < WINDOW - nreal
    knt = knew.T
    vnt = vnew.T
    for b in range(bb):
        shift = (WINDOW - nreal - b * tpad) % LANES
        nkt_ref[b] = jnp.where(keep, pltpu.roll(ckt_ref[b], WINDOW - nreal, 1), pltpu.roll(knt, shift, 1))
        nvt_ref[b] = jnp.where(keep, pltpu.roll(cvt_ref[b], WINDOW - nreal, 1), pltpu.roll(vnt, shift, 1))

    gz = per_seq(z[:, COL_G:COL_G + LANES] + gbias_ref[...])
    lgz = jax.nn.log_sigmoid(gz)
    trow = lax.broadcasted_iota(jnp.int32, (bb, tpad, 1), 1)
    real = trow < nreal
    mn_ref[...] = jnp.zeros_like(mn_ref)
    for hd in range(MLSTM_HEADS):
        hcols = lambda base: slice(base + hd * MLSTM_HEAD_DIM, base + (hd + 1) * MLSTM_HEAD_DIM)
        q = per_seq(z[:, hcols(COL_QM)])
        k = per_seq(z[:, hcols(COL_KM)]) * (MLSTM_HEAD_DIM ** -0.5)
        v = per_seq(z[:, hcols(COL_VM)])
        og = per_seq(z[:, hcols(COL_OM)])
        c0 = c_ref[:, hd]
        n0 = n_ref[:, hd:hd + 1, :]
        m0 = m_ref[:, hd:hd + 1, :]
        ig_c = jnp.where(real, gz[:, :, hd:hd + 1], NEG)
        lf_c = jnp.where(real, lgz[:, :, FG_LANE + hd:FG_LANE + hd + 1], 0.0)
        b_c = jnp.zeros_like(lf_c)
        for sx in range(nreal):
            b_c = b_c + jnp.where(trow >= sx, lf_c[:, sx:sx + 1, :], 0.0)
        dlog = [jnp.where(trow >= sx, b_c - b_c[:, sx:sx + 1, :] + ig_c[:, sx:sx + 1, :], NEG)
                for sx in range(nreal)]
        inter = b_c + m0
        m_t = inter
        for sx in range(nreal):
            m_t = jnp.maximum(m_t, dlog[sx])
        a = jnp.exp(inter - m_t)
        qc = jnp.einsum('btd,bde->bte', q.astype(bf16), c0.astype(bf16), preferred_element_type=f32)
        num = a * qc
        den_m = a * jnp.sum(q * n0, axis=2, keepdims=True)
        for sx in range(nreal):
            sd = jnp.sum(q * k[:, sx:sx + 1, :], axis=2, keepdims=True) * jnp.exp(dlog[sx] - m_t)
            num = num + sd * v[:, sx:sx + 1, :]
            den_m = den_m + sd
        hraw = num / jnp.maximum(jnp.abs(den_m), jnp.exp(-m_t))
        hn = _rms(hraw, mnorm_ref[hd:hd + 1, :][None])
        mix_parts.append((hn * jax.nn.sigmoid(og)).reshape(nrows, MLSTM_HEAD_DIM).astype(bf16))
        last = nreal - 1
        m_new = m_t[:, last:last + 1, :]
        b_last = b_c[:, last:last + 1, :]
        a_end = jnp.exp(b_last + m0 - m_new)
        kw = k * jnp.exp(b_last - b_c + ig_c - m_new)
        cn_ref[:, hd] = a_end * c0 + jnp.einsum('bsd,bse->bde', kw.astype(bf16), v.astype(bf16),
                                               preferred_element_type=f32)
        nn_ref[:, hd:hd + 1, :] = a_end * n0 + jnp.sum(kw, axis=1, keepdims=True)
        mn_ref[:, hd:hd + 1, :] = jnp.broadcast_to(m_new, (bb, 1, LANES))

    mix = jnp.concatenate(mix_parts, axis=1)
    x1 = x_ref[...].reshape(nrows, D_MODEL) + jnp.dot(mix, wout_ref[...], preferred_element_type=f32)
    x1_ref[...] = x1.reshape(bb, tpad, D_MODEL)


def _sample_mixer(x_pad, ckt, cvt, c0, n0, m0, cos, sin, wq_s, w, wgate, wout_s, sink_tile, bd, anorm, qg,
                  kg, gbias, mnorm):
    nb, tpad, _ = x_pad.shape
    bb = SAMPLE_BATCH_BLOCK
    nh = MLSTM_HEADS
    blk = lambda shape: pl.BlockSpec(shape, lambda i: (i,) + (0,) * (len(shape) - 1))
    cblk = (bb, nh, MLSTM_HEAD_DIM, MLSTM_HEAD_DIM)
    return pl.pallas_call(
        _sample_mixer_kernel,
        grid=(nb // bb,),
        in_specs=[blk((bb, tpad, D_MODEL)), blk((bb, KV_WIDTH, WINDOW)), blk((bb, KV_WIDTH, WINDOW)),
                  blk(cblk), blk((bb, nh, MLSTM_HEAD_DIM)), blk((bb, nh, 1)),
                  _const_spec((bb * tpad, LANES)), _const_spec((bb * tpad, LANES)),
                  _const_spec((D_MODEL, ATTN_WIDTH), single=True),
                  _const_spec((D_MODEL, IN_WIDTH), single=True),
                  _const_spec((D_MODEL, LANES), single=True),
                  _const_spec((MIX_WIDTH, D_MODEL), single=True),
                  _const_spec((ATTN_HEADS * tpad, LANES)), _const_spec((LANES, LANES)),
                  _const_spec((1, D_MODEL)),
                  _const_spec((1, LANES)), _const_spec((1, LANES)), _const_spec((1, LANES)),
                  _const_spec((nh, MLSTM_HEAD_DIM))],
        out_specs=[blk((bb, tpad, D_MODEL)), blk((bb, KV_WIDTH, WINDOW)), blk((bb, KV_WIDTH, WINDOW)),
                   blk(cblk), blk((bb, nh, MLSTM_HEAD_DIM)), blk((bb, tpad, LANES))],
        out_shape=[jax.ShapeDtypeStruct((nb, tpad, D_MODEL), f32),
                   jax.ShapeDtypeStruct((nb, KV_WIDTH, WINDOW), f32),
                   jax.ShapeDtypeStruct((nb, KV_WIDTH, WINDOW), f32),
                   jax.ShapeDtypeStruct((nb,) + cblk[1:], f32),
                   jax.ShapeDtypeStruct((nb, nh, MLSTM_HEAD_DIM), f32),
                   jax.ShapeDtypeStruct((nb, tpad, LANES), f32)],
        compiler_params=pltpu.CompilerParams(
            dimension_semantics=("arbitrary",), vmem_limit_bytes=VMEM_LIMIT),
        name="sample_mixer",
    )(x_pad, ckt, cvt, c0, n0, m0, cos, sin, wq_s, w, wgate, wout_s, sink_tile, bd, anorm, qg, kg, gbias,
      mnorm)


def _rope_angles(pos):
    half = ATTN_HEAD_DIM // 2
    inv = ROPE_THETA ** (-np.arange(half, dtype=np.float64) / half)
    ang = pos.astype(np.float64)[:, None] * inv[None, :]
    return np.cos(ang).astype(np.float32), np.sin(ang).astype(np.float32)


def _rope_tables(pos):
    c, s = _rope_angles(pos)
    cos = np.tile(c, (1, LANES // QUARTER))
    sin = np.tile(np.concatenate([-s, s], axis=1), (1, LANES // ATTN_HEAD_DIM))
    return cos, sin


def _rope_tables_quarters(pos):
    c, s = _rope_angles(pos)
    return np.tile(c, (1, LANES // QUARTER)), np.concatenate([-s, -s, s, s], axis=1)


def _quarters(a):
    lo, hi = a[..., :QUARTER], a[..., QUARTER:]
    return jnp.concatenate([lo, lo, hi, hi], axis=-1)


def _prompt_attn_weights(w):
    d = w.shape[0]
    wq = w[:, COL_QA:COL_KA].reshape(d, ATTN_WIDTH // LANES, 2, 2, QUARTER)
    wq = wq.transpose(0, 1, 3, 2, 4).reshape(d, ATTN_WIDTH)
    wk = _quarters(w[:, COL_KA:COL_VA].reshape(d, KV_HEADS, ATTN_HEAD_DIM)).reshape(d, KV_HEADS * LANES)
    wv = w[:, COL_VA:COL_QM].reshape(d, KV_HEADS, 1, ATTN_HEAD_DIM)
    wv = jnp.broadcast_to(wv, (d, KV_HEADS, 2, ATTN_HEAD_DIM)).reshape(d, KV_HEADS * LANES)
    return jnp.concatenate([wq, wk, wv], axis=1)


def kernel(x_prompt, x_sample, cache_k, cache_v, state_C, state_n, state_m, attn_norm, w_in, q_norm,
           k_norm, attn_sinks, b_ig, b_fg, mlstm_norm, w_out, ffn_norm, w_gate, w_up, w_down):
    assert w_in.shape[0] == 1 and x_prompt.shape[0] == 1
    tp = x_prompt.shape[1]
    nb, nt = x_sample.shape[0], x_sample.shape[1]
    assert nt == SAMPLE_TOKENS
    tpad = SUBLANES
    nh = MLSTM_HEADS

    w = w_in[0].astype(bf16)
    pad_a = jnp.zeros((D_MODEL, FG_LANE - nh), bf16)
    pad_b = jnp.zeros((D_MODEL, LANES - FG_LANE - nh), bf16)
    wgate = jnp.concatenate([w[:, COL_G:COL_G + nh], pad_a, w[:, COL_G + nh:], pad_b], axis=1)
    gbias = jnp.concatenate(
        [b_ig[0], jnp.zeros((FG_LANE - nh,), f32), b_fg[0], jnp.zeros((LANES - FG_LANE - nh,), f32)]
    ).reshape(1, LANES)
    wout_b = w_out[0].astype(bf16)
    wg_b = w_gate[0].astype(bf16)
    wu_b = w_up[0].astype(bf16)
    wd_b = w_down[0].astype(bf16)
    anorm = attn_norm[0].reshape(1, D_MODEL)
    fnorm = ffn_norm[0].reshape(1, D_MODEL)
    qg = jnp.tile(q_norm[0], LANES // ATTN_HEAD_DIM).reshape(1, LANES)
    kg = jnp.tile(k_norm[0], LANES // ATTN_HEAD_DIM).reshape(1, LANES)
    mnorm = mlstm_norm[0].reshape(nh, MLSTM_HEAD_DIM)
    sinks = attn_sinks[0]

    wa = _prompt_attn_weights(w)
    idx = np.arange(2 * LANES)
    same = (idx[:, None] // LANES == idx[None, :] // LANES) & (
        (idx[:, None] // QUARTER) % 2 == (idx[None, :] // QUARTER) % 2)
    bd = jnp.asarray(same, dtype=bf16)
    sink_rows_p = jnp.repeat(sinks.reshape(KV_HEADS, ATTN_GROUP), QBLOCK, axis=1)
    bias = jnp.where(jnp.arange(2 * QBLOCK)[None, None, :] == 0, sink_rows_p[:, :, None], NEG)
    qgq = _quarters(q_norm[0]).reshape(1, LANES)
    kgq = _quarters(k_norm[0]).reshape(1, LANES)
    pos_p = np.arange(tp, dtype=np.float32)
    cos_p, sin_p = _rope_tables_quarters(pos_p)
    cos_w, sin_w = _rope_tables(pos_p[tp - WINDOW:])
    x1_p, k_p, v_p, cext_p, m_p = _prompt_mixer(
        x_prompt[0], cos_p, sin_p, wa, w, wgate, wout_b, bd, bias, anorm, qgq, kgq, kg, cos_w, sin_w,
        gbias, mnorm)

    wq_s = w[:, COL_QA:COL_KA].reshape(D_MODEL, KV_HEADS, ATTN_GROUP, ATTN_HEAD_DIM)
    wq_s = wq_s.transpose(0, 2, 1, 3).reshape(D_MODEL, ATTN_WIDTH)
    wo_a = wout_b[:ATTN_WIDTH].reshape(KV_HEADS, ATTN_GROUP, ATTN_HEAD_DIM, D_MODEL)
    wo_a = wo_a.transpose(1, 0, 2, 3).reshape(ATTN_WIDTH, D_MODEL)
    wout_s = jnp.concatenate([wo_a, wout_b[ATTN_WIDTH:]], axis=0)
    sink_tile = jnp.broadcast_to(
        jnp.repeat(sinks.reshape(KV_HEADS, ATTN_GROUP).T.reshape(-1), tpad)[:, None],
        (ATTN_HEADS * tpad, LANES))
    lanes = np.arange(LANES)
    bd_s = jnp.asarray(lanes[:, None] // ATTN_HEAD_DIM == lanes[None, :] // ATTN_HEAD_DIM, dtype=bf16)
    cos_s, sin_s = _rope_tables(np.arange(tpad, dtype=np.float32) + np.float32(PAST_LEN))
    cos_s = np.tile(cos_s, (SAMPLE_BATCH_BLOCK, 1))
    sin_s = np.tile(sin_s, (SAMPLE_BATCH_BLOCK, 1))
    x_pad = jnp.pad(x_sample, ((0, 0), (0, tpad - nt), (0, 0)))
    ckt = cache_k[0].reshape(nb, WINDOW, KV_WIDTH).transpose(0, 2, 1)
    cvt = cache_v[0].reshape(nb, WINDOW, KV_WIDTH).transpose(0, 2, 1)
    x1_pad, nkt, nvt, c_new, n_new, m_pad = _sample_mixer(
        x_pad, ckt, cvt, state_C[0], state_n[0], state_m[0][:, :, None], cos_s, sin_s, wq_s, w, wgate,
        wout_s, sink_tile, bd_s, anorm, qg, kg, gbias, mnorm)
    y_p, y_s = _ffn(x1_p, x1_pad[:, :nt].reshape(nb * nt, D_MODEL), fnorm, wg_b, wu_b, wd_b)
    m_new = m_pad[:, :nh, 0]

    new_k_s = nkt.transpose(0, 2, 1)
    new_v_s = nvt.transpose(0, 2, 1)

    kv_shape = (1, 1, WINDOW, KV_HEADS, ATTN_HEAD_DIM)
    return (
        y_p[None],
        y_s.reshape(nb, nt, D_MODEL),
        k_p.reshape(kv_shape),
        v_p.reshape(kv_shape),
        cext_p[None, None, :, :, :MLSTM_HEAD_DIM],
        cext_p[None, None, :, :, MLSTM_HEAD_DIM],
        m_p[None, None, :nh, 0],
        new_k_s.reshape(1, nb, WINDOW, KV_HEADS, ATTN_HEAD_DIM),
        new_v_s.reshape(1, nb, WINDOW, KV_HEADS, ATTN_HEAD_DIM),
        c_new.reshape(1, nb, nh, MLSTM_HEAD_DIM, MLSTM_HEAD_DIM),
        n_new.reshape(1, nb, nh, MLSTM_HEAD_DIM),
        m_new.reshape(1, nb, nh),
    )
```

```python
import jax
import jax.numpy as jnp
import numpy as np
from jax import lax
from jax.experimental import pallas as pl
from jax.experimental.pallas import tpu as pltpu

D_MODEL = 1024
PAST_LEN = 16384
ATTN_HEADS = 8
KV_HEADS = 2
ATTN_HEAD_DIM = 64
ATTN_GROUP = ATTN_HEADS // KV_HEADS
ATTN_WIDTH = ATTN_HEADS * ATTN_HEAD_DIM
KV_WIDTH = KV_HEADS * ATTN_HEAD_DIM
WINDOW = 128
ROPE_THETA = 10000.0
MLSTM_HEADS = 4
MLSTM_HEAD_DIM = 128
MLSTM_WIDTH = MLSTM_HEADS * MLSTM_HEAD_DIM
MIX_WIDTH = ATTN_WIDTH + MLSTM_WIDTH
D_FF = 2816
NORM_EPS = 1e-6

LANES = 128
SUBLANES = 8
VMEM_LIMIT = 56 * 1024 * 1024

COL_QA = 0
COL_KA = COL_QA + ATTN_WIDTH
COL_VA = COL_KA + KV_WIDTH
COL_QM = COL_VA + KV_WIDTH
COL_KM = COL_QM + MLSTM_WIDTH
COL_VM = COL_KM + MLSTM_WIDTH
COL_OM = COL_VM + MLSTM_WIDTH
COL_G = COL_OM + MLSTM_WIDTH
IN_WIDTH = COL_G + 2 * MLSTM_HEADS
FG_LANE = SUBLANES

PROMPT_BLOCK = 256
QBLOCK = WINDOW
MCHUNK = 128
PROJ_CHUNK = 256
FFN_BLOCK = 512
FFN_CHUNK = 256
SAMPLE_BATCH_BLOCK = 16
SAMPLE_TOKENS = 4
NEG = -1e30

f32 = jnp.float32
bf16 = jnp.bfloat16


def _rms(x, gain):
    return x * lax.rsqrt(jnp.mean(x * x, axis=-1, keepdims=True) + NORM_EPS) * gain


def _segsum64(s, lane):
    for k in (1, 2, 4, 8, 16, 32):
        s = s + jnp.where((lane & k) != 0, pltpu.roll(s, k, 1), pltpu.roll(s, LANES - k, 1))
    return s


def _headnorm_rope(xs, gain, cos, sin_signed, lane):
    ss = _segsum64(xs * xs, lane)
    y = xs * lax.rsqrt(ss * (1.0 / ATTN_HEAD_DIM) + NORM_EPS) * gain
    partner = jnp.where((lane & 32) != 0, pltpu.roll(y, 32, 1), pltpu.roll(y, LANES - 32, 1))
    return y * cos + partner * sin_signed


def _group_sumsq(xs, bd_ref):
    x2 = xs * xs
    hi = x2.astype(bf16)
    lo = (x2 - hi.astype(f32)).astype(bf16)
    return (jnp.dot(hi, bd_ref[...], preferred_element_type=f32)
            + jnp.dot(lo, bd_ref[...], preferred_element_type=f32))


PA_Q = 0
PA_K = PA_Q + ATTN_WIDTH
PA_V = PA_K + KV_HEADS * LANES
PA_WIDTH = PA_V + KV_HEADS * LANES
QUARTER = ATTN_HEAD_DIM // 2
QA_Q = 0
QA_K = QA_Q + 2 * ATTN_WIDTH
QA_V = QA_K + KV_HEADS * LANES
QA_VZ = QA_V + KV_HEADS * LANES
QA_WIDTH = QA_VZ + KV_HEADS * LANES
ZM_WIDTH = 4 * MLSTM_WIDTH


def _norm_rope_quarters(xs, ss, gain, cos, sin_signed):
    y = xs * lax.rsqrt(ss * (1.0 / ATTN_HEAD_DIM) + NORM_EPS) * gain
    return y * cos + pltpu.roll(y, LANES // 2, 1) * sin_signed


def _col_chunks(width):
    return [(c, min(c + PROJ_CHUNK, width)) for c in range(0, width, PROJ_CHUNK)]


def _mixer_proj_jobs(x_ref, anorm_ref, wa_ref, w_ref, wgate_ref, za_ref, zm_ref, gz_ref):
    h = _rms(x_ref[...], anorm_ref[...]).astype(bf16)

    def proj_job(w_src, wc0, z_ref, c0, c1):
        def run():
            z_ref[:, c0:c1] = jnp.dot(h, w_src[:, wc0 + c0:wc0 + c1], preferred_element_type=f32)
        return run

    return ([proj_job(wa_ref, 0, za_ref, c0, c1) for c0, c1 in _col_chunks(PA_WIDTH)]
            + [proj_job(w_ref, COL_QM, zm_ref, c0, c1) for c0, c1 in _col_chunks(ZM_WIDTH)]
            + [proj_job(wgate_ref, 0, gz_ref, 0, LANES)])


def _mixer_prep_jobs(za_ref, gz_ref, cos_ref, sin_ref, bd_ref, qg_ref, kg_ref, gbias_ref, qa_ref,
                     colf_ref, urow_ref, wrow_ref, mst_ref, mout_ref):
    tb = za_ref.shape[0]

    def gates_job():
        lane_t = lax.broadcasted_iota(jnp.int32, (tb, LANES), 1)
        gcol = gz_ref[...] + gbias_ref[...]
        acol = jnp.where(lane_t < FG_LANE, gcol, jax.nn.log_sigmoid(gcol))
        arow = acol.T
        lane8 = lax.broadcasted_iota(jnp.int32, (SUBLANES, LANES), 1)
        lane_in = lane8 & (MCHUNK - 1)
        m_prev = mst_ref[:, 0:1]
        stacks = []
        for sb in range(tb // LANES):
            ls = slice(sb * LANES, (sb + 1) * LANES)
            ig8 = arow[0:SUBLANES, ls]
            lf8 = arow[FG_LANE:FG_LANE + SUBLANES, ls]
            b8 = lf8
            k = 1
            while k < MCHUNK:
                b8 = b8 + jnp.where(lane_in >= k, pltpu.roll(b8, k, 1), 0.0)
                k *= 2
            u8 = ig8 - b8
            cm8 = u8
            k = 1
            while k < MCHUNK:
                cm8 = jnp.maximum(cm8, jnp.where(lane_in >= k, pltpu.roll(cm8, k, 1), NEG))
                k *= 2
            g8 = jnp.zeros_like(u8)
            mp8 = jnp.zeros_like(u8)
            gl8 = jnp.zeros_like(u8)
            for c in range(LANES // MCHUNK):
                in_chunk = (lane8 // MCHUNK) == c
                gc = jnp.maximum(cm8, m_prev)
                last = c * MCHUNK + MCHUNK - 1
                g_last = jnp.max(jnp.where(lane8 == last, gc, NEG), axis=1, keepdims=True)
                b_last = jnp.max(jnp.where(lane8 == last, b8, NEG), axis=1, keepdims=True)
                g8 = jnp.where(in_chunk, gc, g8)
                mp8 = jnp.where(in_chunk, m_prev, mp8)
                gl8 = jnp.where(in_chunk, g_last, gl8)
                m_prev = b_last + g_last
            a8 = jnp.exp(mp8 - g8)
            emt8 = jnp.exp(-(b8 + g8))
            aend8 = jnp.exp(mp8 - gl8)
            stacks.append(jnp.concatenate(
                [g8, a8, emt8, aend8, jnp.zeros((LANES - 4 * SUBLANES, LANES), f32)], axis=0))
            urow_ref[:, ls] = u8
            wrow_ref[:, ls] = jnp.exp(u8 - gl8)
        mst_ref[...] = jnp.broadcast_to(m_prev, mst_ref.shape)
        mout_ref[...] = jnp.broadcast_to(m_prev, mout_ref.shape)
        colf_ref[...] = jnp.concatenate(stacks, axis=1).T

    def prep_job(qb):
        def run():
            rows = slice(qb * QBLOCK, (qb + 1) * QBLOCK)
            lane = lax.broadcasted_iota(jnp.int32, (QBLOCK, LANES), 1)
            head_a = ((lane // QUARTER) & 1) == 0
            row0 = lax.broadcasted_iota(jnp.int32, (QBLOCK, LANES), 0) == 0
            cos = cos_ref[rows, :]
            sin = sin_ref[rows, :]
            ss = [_group_sumsq(za_ref[rows, d * 2 * LANES:(d + 1) * 2 * LANES], bd_ref)
                  for d in range(PA_V // (2 * LANES))]
            for j in range(PA_V // LANES):
                is_q = j < ATTN_WIDTH // LANES
                y = _norm_rope_quarters(za_ref[rows, j * LANES:(j + 1) * LANES],
                                        ss[j // 2][:, (j % 2) * LANES:(j % 2 + 1) * LANES],
                                        qg_ref[...] if is_q else kg_ref[...], cos, sin)
                if is_q:
                    y = y * (ATTN_HEAD_DIM ** -0.5)
                    qa_ref[rows, QA_Q + 2 * j * LANES:QA_Q + (2 * j + 1) * LANES] = (
                        jnp.where(head_a, y, 0.0).astype(bf16))
                    qa_ref[rows, QA_Q + (2 * j + 1) * LANES:QA_Q + (2 * j + 2) * LANES] = (
                        jnp.where(head_a, 0.0, y).astype(bf16))
                else:
                    c = j - ATTN_WIDTH // LANES
                    qa_ref[rows, QA_K + c * LANES:QA_K + (c + 1) * LANES] = y.astype(bf16)
            for c in range(KV_HEADS):
                v = za_ref[rows, PA_V + c * LANES:PA_V + (c + 1) * LANES]
                qa_ref[rows, QA_V + c * LANES:QA_V + (c + 1) * LANES] = v.astype(bf16)
                qa_ref[rows, QA_VZ + c * LANES:QA_VZ + (c + 1) * LANES] = (
                    jnp.where(row0, 0.0, v).astype(bf16))
        return run

    return [gates_job] + [prep_job(qb) for qb in range(tb // QBLOCK)]


def _mixer_outproj_jobs(xs_ref, mix_ref, wout_ref, x1_ref):
    def job(c0, c1):
        def run():
            x1_ref[:, c0:c1] = xs_ref[:, c0:c1] + jnp.dot(mix_ref[...], wout_ref[:, c0:c1],
                                                          preferred_element_type=f32)
        return run

    return [job(c0, c1) for c0, c1 in _col_chunks(D_MODEL)]


def _mixer_ffn_jobs(x1_ref, fnorm_ref, wg_ref, wu_ref, wd_ref, act_ref, y_ref):
    hf = _rms(x1_ref[...], fnorm_ref[...]).astype(bf16)

    def act_job(c0, c1):
        def run():
            gate = jnp.dot(hf, wg_ref[:, c0:c1], preferred_element_type=f32)
            up = jnp.dot(hf, wu_ref[:, c0:c1], preferred_element_type=f32)
            act_ref[:, c0:c1] = (gate * jax.nn.sigmoid(gate) * up).astype(bf16)
        return run

    def down_job(c0, c1):
        def run():
            y_ref[:, c0:c1] = x1_ref[:, c0:c1] + jnp.dot(act_ref[...], wd_ref[:, c0:c1],
                                                         preferred_element_type=f32)
        return run

    return ([act_job(c0, c1) for c0, c1 in _col_chunks(D_FF)],
            [down_job(c0, c1) for c0, c1 in _col_chunks(D_MODEL)])


def _mixer_window_out(xs_ref, anorm_ref, w_ref, kgs_ref, coss_ref, sins_ref, kout_ref, vout_ref):
    tb = xs_ref.shape[0]
    h = _rms(xs_ref[tb - WINDOW:, :], anorm_ref[...]).astype(bf16)
    zs = jnp.dot(h, w_ref[:, COL_KA:COL_QM], preferred_element_type=f32)
    lane_s = lax.broadcasted_iota(jnp.int32, (WINDOW, LANES), 1)
    kout_ref[...] = _headnorm_rope(zs[:, :KV_WIDTH], kgs_ref[...], coss_ref[...], sins_ref[...], lane_s)
    vout_ref[...] = zs[:, KV_WIDTH:]


def _mixer_core(first_block, last_block, fillers, prep_jobs, qa_ref, zm_ref, colf_ref, urow_ref, wrow_ref,
                mix_ref, bias_ref, mnorm_ref, kprev_ref, vprev_ref, cst_ref, cext_ref):
    tb = qa_ref.shape[0]
    nqb = tb // QBLOCK
    nch = tb // MCHUNK
    fillers = list(fillers)
    n_fill = len(fillers)
    slots = len(prep_jobs) + nqb * KV_HEADS + nch * MLSTM_HEADS
    progress = [0]

    def fill():
        progress[0] += 1
        while n_fill - len(fillers) < min(n_fill, -(-n_fill * progress[0] // slots)):
            fillers.pop(0)()

    for job in prep_jobs:
        job()
        fill()

    low_half = lax.broadcasted_iota(jnp.int32, (QBLOCK, LANES), 1) < ATTN_HEAD_DIM
    qi = lax.broadcasted_iota(jnp.int32, (ATTN_GROUP * QBLOCK, 2 * QBLOCK), 0) & (QBLOCK - 1)
    kj = lax.broadcasted_iota(jnp.int32, (ATTN_GROUP * QBLOCK, 2 * QBLOCK), 1)
    band = (kj > qi) & (kj <= qi + QBLOCK)
    ones_slab = jnp.ones((2 * QBLOCK, LANES), bf16)

    def attn_unit(qb, c):
        rows = slice(qb * QBLOCK, (qb + 1) * QBLOCK)
        kcols = slice(QA_K + c * LANES, QA_K + (c + 1) * LANES)
        if qb == 0:
            kprev, vprev = kprev_ref[c], vprev_ref[c]
        else:
            prev = slice((qb - 1) * QBLOCK, qb * QBLOCK)
            kprev = qa_ref[prev, kcols]
            vprev = qa_ref[prev, QA_VZ + c * LANES:QA_VZ + (c + 1) * LANES]
        kcat = jnp.concatenate([kprev, qa_ref[rows, kcols]], axis=0)
        vcat = jnp.concatenate([vprev, qa_ref[rows, QA_V + c * LANES:QA_V + (c + 1) * LANES]], axis=0)
        vext = jnp.concatenate([vcat, ones_slab], axis=1)
        q0 = QA_Q + c * ATTN_GROUP * LANES
        qst = jnp.concatenate([qa_ref[rows, q0 + g * LANES:q0 + (g + 1) * LANES]
                               for g in range(ATTN_GROUP)], axis=0)
        s = lax.dot_general(qst, kcat, (((1,), (1,)), ((), ())), preferred_element_type=f32)
        valid = band & (kj >= QBLOCK) if (first_block and qb == 0) else band
        s = jnp.where(valid, s, bias_ref[c])
        p = jnp.exp(s - jnp.max(s, axis=-1, keepdims=True)).astype(bf16)
        of = jnp.dot(p, vext, preferred_element_type=f32)
        o = of[:, :LANES] / of[:, LANES:]
        for jj in range(2):
            pair = jnp.where(low_half, o[(2 * jj) * QBLOCK:(2 * jj + 1) * QBLOCK],
                             o[(2 * jj + 1) * QBLOCK:(2 * jj + 2) * QBLOCK])
            col = (2 * c + jj) * LANES
            mix_ref[rows, col:col + LANES] = pair.astype(bf16)

    ti = lax.broadcasted_iota(jnp.int32, (MCHUNK, MCHUNK), 0)
    si = lax.broadcasted_iota(jnp.int32, (MCHUNK, MCHUNK), 1)
    causal = si <= ti
    ones_l = jnp.ones((MCHUNK, LANES), bf16)

    def mlstm_unit(hd, c, cext):
        hcols = lambda k: slice((k * MLSTM_HEADS + hd) * MLSTM_HEAD_DIM,
                                (k * MLSTM_HEADS + hd + 1) * MLSTM_HEAD_DIM)
        rows = slice(c * MCHUNK, (c + 1) * MCHUNK)
        qb_ = zm_ref[rows, hcols(0)].astype(bf16)
        kf = zm_ref[rows, hcols(1)] * (MLSTM_HEAD_DIM ** -0.5)
        vb = zm_ref[rows, hcols(2)].astype(bf16)
        og = zm_ref[rows, hcols(3)]
        g_c = colf_ref[rows, hd:hd + 1]
        a_c = colf_ref[rows, SUBLANES + hd:SUBLANES + hd + 1]
        emt_c = colf_ref[rows, 2 * SUBLANES + hd:2 * SUBLANES + hd + 1]
        aend = colf_ref[c * MCHUNK:c * MCHUNK + 1, 3 * SUBLANES + hd:3 * SUBLANES + hd + 1]
        u_r = urow_ref[hd:hd + 1, rows]
        w_r = wrow_ref[hd:hd + 1, rows]
        dmat = jnp.exp(jnp.where(causal, u_r - g_c, NEG))
        smat = lax.dot_general(qb_, kf.astype(bf16), (((1,), (1,)), ((), ())),
                               preferred_element_type=f32) * dmat
        vext = jnp.concatenate([vb, ones_l], axis=1)
        nd = (a_c * jnp.dot(qb_, cext.astype(bf16), preferred_element_type=f32)
              + jnp.dot(smat.astype(bf16), vext, preferred_element_type=f32))
        hraw = nd[:, :MLSTM_HEAD_DIM] / jnp.maximum(jnp.abs(nd[:, MLSTM_HEAD_DIM:]), emt_c)
        hn = _rms(hraw, mnorm_ref[hd:hd + 1, :])
        mix_ref[rows, ATTN_WIDTH + hd * MLSTM_HEAD_DIM:ATTN_WIDTH + (hd + 1) * MLSTM_HEAD_DIM] = (
            (hn * jax.nn.sigmoid(og)).astype(bf16))
        kw_t = (kf.T * w_r).astype(bf16)
        return aend * cext + jnp.dot(kw_t, vext, preferred_element_type=f32)

    cexts =[cst_ref[hd] for hd in range(MLSTM_HEADS)]
    attn_jobs = [(qb, c) for qb in range(nqb) for c in range(KV_HEADS)]
    mlstm_jobs = [(hd, c) for c in range(nch) for hd in range(MLSTM_HEADS)]
    while attn_jobs or mlstm_jobs:
        if attn_jobs:
            attn_unit(*attn_jobs.pop(0))
            fill()
        for _ in range(2 if len(mlstm_jobs) > 2 * len(attn_jobs) else 1):
            if mlstm_jobs:
                hd, c = mlstm_jobs.pop(0)
                cexts[hd] = mlstm_unit(hd, c, cexts[hd])
                fill()
    assert not fillers
    last_rows = slice(tb - QBLOCK, tb)
    for c in range(KV_HEADS):
        kprev_ref[c] = qa_ref[last_rows, QA_K + c * LANES:QA_K + (c + 1) * LANES]
        vprev_ref[c] = qa_ref[last_rows, QA_VZ + c * LANES:QA_VZ + (c + 1) * LANES]
    for hd in range(MLSTM_HEADS):
        cst_ref[hd] = cexts[hd]
        if last_block:
            cext_ref[hd] = cexts[hd]


def _prompt_layer_kernel(x_ref, cos_ref, sin_ref, wa_ref, w_ref, wgate_ref, wout_ref, wg_ref, wu_ref, wd_ref,
                         bd_ref, bias_ref, anorm_ref, fnorm_ref, qg_ref, kg_ref, kgs_ref, coss_ref,
                         sins_ref, gbias_ref, mnorm_ref,
                         y_ref, kout_ref, vout_ref, cext_ref, mout_ref,
                         za0, za1, zm0, zm1, gz0, gz1, mix0, mix1, xs0, xs1, x1b0, x1b1, act_ref,
                         qa_ref, colf_ref, urow_ref, wrow_ref, kprev_ref, vprev_ref, cst_ref, mst_ref):
    step = pl.program_id(0)
    nblk = pl.num_programs(0) - 3
    za, zm, gz, mix, xs = (za0, za1), (zm0, zm1), (gz0, gz1), (mix0, mix1), (xs0, xs1)
    x1b = (x1b0, x1b1)

    def run(parity, do_in, do_core, do_out, do_ffn, first_block=False, last_block=False):
        other = 1 - parity
        jobs = []
        if do_out:
            jobs += _mixer_outproj_jobs(xs[parity], mix[parity], wout_ref, x1b[parity])
        proj = (_mixer_proj_jobs(x_ref, anorm_ref, wa_ref, w_ref, wgate_ref, za[parity], zm[parity],
                                 gz[parity]) if do_in else [])
        act, down = (_mixer_ffn_jobs(x1b[other], fnorm_ref, wg_ref, wu_ref, wd_ref, act_ref, y_ref)
                     if do_ffn else ([], []))
        for i in range(max(len(proj), len(act))):
            jobs += proj[i:i + 1] + act[i:i + 1]
        jobs += down
        if do_core:
            prep = _mixer_prep_jobs(za[other], gz[other], cos_ref, sin_ref, bd_ref, qg_ref, kg_ref,
                                    gbias_ref, qa_ref, colf_ref, urow_ref, wrow_ref, mst_ref, mout_ref)
            _mixer_core(first_block, last_block, jobs, prep, qa_ref, zm[other], colf_ref, urow_ref,
                        wrow_ref, mix[other], bias_ref, mnorm_ref, kprev_ref, vprev_ref, cst_ref, cext_ref)
        else:
            for job in jobs:
                job()
        if last_block:
            _mixer_window_out(xs[other], anorm_ref, w_ref, kgs_ref, coss_ref, sins_ref, kout_ref, vout_ref)
        if do_in:
            xs[parity][...] = x_ref[...]

    @pl.when(step == 0)
    def _first():
        kprev_ref[...] = jnp.zeros_like(kprev_ref)
        vprev_ref[...] = jnp.zeros_like(vprev_ref)
        cst_ref[...] = jnp.zeros_like(cst_ref)
        mst_ref[...] = jnp.zeros_like(mst_ref)
        run(0, True, False, False, False)

    @pl.when(step == 1)
    def _second():
        run(1, True, True, False, False, first_block=True)

    @pl.when(step == 2)
    def _third():
        run(0, True, True, True, False)

    steady = (step >= 3) & (step < nblk)

    @pl.when(steady & (step % 2 == 0))
    def _even():
        run(0, True, True, True, True)

    @pl.when(steady & (step % 2 == 1))
    def _odd():
        run(1, True, True, True, True)

    @pl.when(step == nblk)
    def _drain_core():
        run(0, False, True, True, True, last_block=True)

    @pl.when(step == nblk + 1)
    def _drain_out():
        run(1, False, False, True, True)

    @pl.when(step == nblk + 2)
    def _drain_ffn():
        run(0, False, False, False, True)


def _const_spec(shape, single=False):
    nd = len(shape)
    if single:
        return pl.BlockSpec(shape, lambda i, *_: (0,) * nd, pipeline_mode=pl.Buffered(1))
    return pl.BlockSpec(shape, lambda i, *_: (0,) * nd)


def _prompt_layer(x, cos, sin, wa, w, wgate, wout_b, wg_b, wu_b, wd_b, bd, bias, anorm, fnorm, qg, kg, kgs,
                  coss, sins, gbias, mnorm):
    t = x.shape[0]
    tb = PROMPT_BLOCK
    nblk = t // tb
    assert nblk % 2 == 0 and nblk >= 6
    state_shape = (MLSTM_HEADS, MLSTM_HEAD_DIM, 2 * MLSTM_HEAD_DIM)
    last = nblk - 1
    cur = lambda i: (jnp.minimum(i, last), 0)
    return pl.pallas_call(
        _prompt_layer_kernel,
        grid=(nblk + 3,),
        in_specs=[
            pl.BlockSpec((tb, D_MODEL), cur),
            pl.BlockSpec((tb, LANES), lambda i: (jnp.clip(i - 1, 0, last), 0)),
            pl.BlockSpec((tb, LANES), lambda i: (jnp.clip(i - 1, 0, last), 0)),
            _const_spec((D_MODEL, PA_WIDTH), single=True),
            _const_spec((D_MODEL, IN_WIDTH), single=True),
            _const_spec((D_MODEL, LANES), single=True),
            _const_spec((MIX_WIDTH, D_MODEL), single=True),
            _const_spec((D_MODEL, D_FF), single=True),
            _const_spec((D_MODEL, D_FF), single=True),
            _const_spec((D_FF, D_MODEL), single=True),
            _const_spec((2 * LANES, 2 * LANES), single=True),
            _const_spec((KV_HEADS, ATTN_GROUP * QBLOCK, 2 * QBLOCK), single=True),
            _const_spec((1, D_MODEL)),
            _const_spec((1, D_MODEL)),
            _const_spec((1, LANES)),
            _const_spec((1, LANES)),
            _const_spec((1, LANES)),
            _const_spec((WINDOW, LANES)),
            _const_spec((WINDOW, LANES)),
            _const_spec((1, LANES)),
            _const_spec((MLSTM_HEADS, MLSTM_HEAD_DIM)),
        ],
        out_specs=[
            pl.BlockSpec((tb, D_MODEL), lambda i: (jnp.clip(i - 3, 0, last), 0)),
            _const_spec((WINDOW, KV_WIDTH)),
            _const_spec((WINDOW, KV_WIDTH)),
            _const_spec(state_shape),
            _const_spec((SUBLANES, LANES)),
        ],
        out_shape=[
            jax.ShapeDtypeStruct((t, D_MODEL), f32),
            jax.ShapeDtypeStruct((WINDOW, KV_WIDTH), f32),
            jax.ShapeDtypeStruct((WINDOW, KV_WIDTH), f32),
            jax.ShapeDtypeStruct(state_shape, f32),
            jax.ShapeDtypeStruct((SUBLANES, LANES), f32),
        ],
        scratch_shapes=(
            [pltpu.VMEM((tb, PA_WIDTH), f32)] * 2 + [pltpu.VMEM((tb, ZM_WIDTH), f32)] * 2
            + [pltpu.VMEM((tb, LANES), f32)] * 2
            + [pltpu.VMEM((tb, MIX_WIDTH), bf16)] * 2 + [pltpu.VMEM((tb, D_MODEL), f32)] * 4
            + [pltpu.VMEM((tb, D_FF), bf16)]
            + [pltpu.VMEM((tb, QA_WIDTH), bf16), pltpu.VMEM((tb, LANES), f32)]
            + [pltpu.VMEM((SUBLANES, tb), f32)] * 2
            + [pltpu.VMEM((KV_HEADS, WINDOW, LANES), bf16)] * 2
            + [pltpu.VMEM(state_shape, f32), pltpu.VMEM((SUBLANES, LANES), f32)]),
        compiler_params=pltpu.CompilerParams(
            dimension_semantics=("arbitrary",), vmem_limit_bytes=VMEM_LIMIT),
        name="prompt_layer",
    )(x, cos, sin, wa, w, wgate, wout_b, wg_b, wu_b, wd_b, bd, bias, anorm, fnorm, qg, kg, kgs, coss, sins,
      gbias, mnorm)


def _ffn_kernel(x_ref, g_ref, wg_ref, wu_ref, wd_ref, o_ref):
    x = x_ref[...]
    hf = _rms(x, g_ref[...]).astype(bf16)
    acc = x
    for c in range(D_FF // FFN_CHUNK):
        cs = slice(c * FFN_CHUNK, (c + 1) * FFN_CHUNK)
        gate = jnp.dot(hf, wg_ref[:, cs], preferred_element_type=f32)
        up = jnp.dot(hf, wu_ref[:, cs], preferred_element_type=f32)
        act = (gate * jax.nn.sigmoid(gate) * up).astype(bf16)
        acc = acc + jnp.dot(act, wd_ref[cs, :], preferred_element_type=f32)
    o_ref[...] = acc


def _ffn(x, fnorm, wg_b, wu_b, wd_b):
    n = x.shape[0]
    tm = min(FFN_BLOCK, n)
    return pl.pallas_call(
        _ffn_kernel,
        grid=(n // tm,),
        in_specs=[
            pl.BlockSpec((tm, D_MODEL), lambda i: (i, 0)),
            _const_spec((1, D_MODEL)),
            _const_spec((D_MODEL, D_FF), single=True),
            _const_spec((D_MODEL, D_FF), single=True),
            _const_spec((D_FF, D_MODEL), single=True),
        ],
        out_specs=pl.BlockSpec((tm, D_MODEL), lambda i: (i, 0)),
        out_shape=jax.ShapeDtypeStruct((n, D_MODEL), f32),
        compiler_params=pltpu.CompilerParams(
            dimension_semantics=("arbitrary",), vmem_limit_bytes=VMEM_LIMIT),
        name="ffn",
    )(x, fnorm, wg_b, wu_b, wd_b)


def _sample_mixer_kernel(x_ref, ckt_ref, cvt_ref, c_ref, n_ref, m_ref, cos_ref, sin_ref, wqs_ref, w_ref,
                         wgate_ref, wout_ref, sink_ref, bd_ref, anorm_ref, qg_ref, kg_ref, gbias_ref,
                         mnorm_ref, x1_ref, nkt_ref, nvt_ref, cn_ref, nn_ref, mn_ref):
    bb, tpad, _ = x_ref.shape
    nrows = bb * tpad
    nreal = SAMPLE_TOKENS
    h = _rms(x_ref[...].reshape(nrows, D_MODEL), anorm_ref[...]).astype(bf16)
    z = jnp.concatenate(
        [jnp.dot(h, wqs_ref[...], preferred_element_type=f32),
         jnp.dot(h, w_ref[:, COL_KA:COL_G], preferred_element_type=f32),
         jnp.dot(h, wgate_ref[...], preferred_element_type=f32)], axis=1)
    lane = lax.broadcasted_iota(jnp.int32, (nrows, LANES), 1)
    low = lane < ATTN_HEAD_DIM
    cos = cos_ref[...]
    sin = sin_ref[...]

    def per_seq(a):
        return a.reshape(bb, tpad, a.shape[-1])

    def norm_rope(xs, gain):
        y = xs * lax.rsqrt(_group_sumsq(xs, bd_ref) * (1.0 / ATTN_HEAD_DIM) + NORM_EPS) * gain
        partner = jnp.where((lane & QUARTER) != 0, pltpu.roll(y, QUARTER, 1),
                            pltpu.roll(y, LANES - QUARTER, 1))
        return y * cos + partner * sin

    q_rows = []
    for j in range(ATTN_GROUP):
        qs = norm_rope(z[:, COL_QA + j * LANES:COL_QA + (j + 1) * LANES], qg_ref[...])
        qs = qs * (ATTN_HEAD_DIM ** -0.5)
        q_rows.append(per_seq(jnp.where(low, qs, 0.0)).astype(bf16))
        q_rows.append(per_seq(jnp.where(low, 0.0, qs)).astype(bf16))
    qbd = jnp.concatenate(q_rows, axis=1)
    knew = norm_rope(z[:, COL_KA:COL_KA + KV_WIDTH], kg_ref[...])
    vnew = z[:, COL_VA:COL_VA + KV_WIDTH]
    zpad = jnp.zeros((bb, LANES - tpad, LANES), bf16)
    knp = jnp.concatenate([per_seq(knew).astype(bf16), zpad], axis=1)
    vnp = jnp.concatenate([per_seq(vnew).astype(bf16), zpad], axis=1)
    ckt = ckt_ref[...]
    cvt = cvt_ref[...]
    s = jnp.concatenate(
        [jnp.einsum('bqd,bdw->bqw', qbd, ckt.astype(bf16), preferred_element_type=f32),
         jnp.einsum('bqd,bkd->bqk', qbd, knp, preferred_element_type=f32)], axis=2)
    tq = lax.broadcasted_iota(jnp.int32, s.shape, 1) & (tpad - 1)
    kj = lax.broadcasted_iota(jnp.int32, s.shape, 2)
    valid = ((kj < WINDOW) & (kj > tq)) | ((kj >= WINDOW) & (kj - WINDOW <= tq) & (kj - WINDOW < nreal))
    s = jnp.where(valid, s, NEG)
    sink = sink_ref[:, 0:1][None]
    mx = jnp.maximum(jnp.max(s, axis=-1, keepdims=True), sink)
    p = jnp.exp(s - mx)
    den = jnp.sum(p, axis=-1, keepdims=True) + jnp.exp(sink - mx)
    pb = p.astype(bf16)
    o = (jnp.einsum('bqw,bdw->bqd', pb[:, :, :WINDOW], cvt.astype(bf16), preferred_element_type=f32)
         + jnp.einsum('bqk,bkd->bqd', pb[:, :, WINDOW:], vnp, preferred_element_type=f32)) / den
    low3 = lax.broadcasted_iota(jnp.int32, (bb, tpad, LANES), 2) < ATTN_HEAD_DIM
    mix_parts = []
    for j in range(ATTN_GROUP):
        r0 = 2 * j * tpad
        pair = jnp.where(low3, o[:, r0:r0 + tpad, :], o[:, r0 + tpad:r0 + 2 * tpad, :])
        mix_parts.append(pair.reshape(nrows, LANES).astype(bf16))

    keep = lax.broadcasted_iota(jnp.int32, (KV_WIDTH, WINDOW), 1) < WINDOW - nreal
    knt = knew.T
    vnt = vnew.T
    for b in range(bb):
        shift = (WINDOW - nreal - b * tpad) % LANES
        nkt_ref[b] = jnp.where(keep, pltpu.roll(ckt_ref[b], WINDOW - nreal, 1), pltpu.roll(knt, shift, 1))
        nvt_ref[b] = jnp.where(keep, pltpu.roll(cvt_ref[b], WINDOW - nreal, 1), pltpu.roll(vnt, shift, 1))

    gz = per_seq(z[:, COL_G:COL_G + LANES] + gbias_ref[...])
    lgz = jax.nn.log_sigmoid(gz)
    trow = lax.broadcasted_iota(jnp.int32, (bb, tpad, 1), 1)
    real = trow < nreal
    mn_ref[...] = jnp.zeros_like(mn_ref)
    for hd in range(MLSTM_HEADS):
        hcols = lambda base: slice(base + hd * MLSTM_HEAD_DIM, base + (hd + 1) * MLSTM_HEAD_DIM)
        q = per_seq(z[:, hcols(COL_QM)])
        k = per_seq(z[:, hcols(COL_KM)]) * (MLSTM_HEAD_DIM ** -0.5)
        v = per_seq(z[:, hcols(COL_VM)])
        og = per_seq(z[:, hcols(COL_OM)])
        c0 = c_ref[:, hd]
        n0 = n_ref[:, hd:hd + 1, :]
        m0 = m_ref[:, hd:hd + 1, :]
        ig_c = jnp.where(real, gz[:, :, hd:hd + 1], NEG)
        lf_c = jnp.where(real, lgz[:, :, FG_LANE + hd:FG_LANE + hd + 1], 0.0)
        b_c = jnp.zeros_like(lf_c)
        for sx in range(nreal):
            b_c = b_c + jnp.where(trow >= sx, lf_c[:, sx:sx + 1, :], 0.0)
        dlog = [jnp.where(trow >= sx, b_c - b_c[:, sx:sx + 1, :] + ig_c[:, sx:sx + 1, :], NEG)
                for sx in range(nreal)]
        inter = b_c + m0
        m_t = inter
        for sx in range(nreal):
            m_t = jnp.maximum(m_t, dlog[sx])
        a = jnp.exp(inter - m_t)
        qc = jnp.einsum('btd,bde->bte', q.astype(bf16), c0.astype(bf16), preferred_element_type=f32)
        num = a * qc
        den_m = a * jnp.sum(q * n0, axis=2, keepdims=True)
        for sx in range(nreal):
            sd = jnp.sum(q * k[:, sx:sx + 1, :], axis=2, keepdims=True) * jnp.exp(dlog[sx] - m_t)
            num = num + sd * v[:, sx:sx + 1, :]
            den_m = den_m + sd
        hraw = num / jnp.maximum(jnp.abs(den_m), jnp.exp(-m_t))
        hn = _rms(hraw, mnorm_ref[hd:hd + 1, :][None])
        mix_parts.append((hn * jax.nn.sigmoid(og)).reshape(nrows, MLSTM_HEAD_DIM).astype(bf16))
        last = nreal - 1
        m_new = m_t[:, last:last + 1, :]
        b_last = b_c[:, last:last + 1, :]
        a_end = jnp.exp(b_last + m0 - m_new)
        kw = k * jnp.exp(b_last - b_c + ig_c - m_new)
        cn_ref[:, hd] = a_end * c0 + jnp.einsum('bsd,bse->bde', kw.astype(bf16), v.astype(bf16),
                                               preferred_element_type=f32)
        nn_ref[:, hd:hd + 1, :] = a_end * n0 + jnp.sum(kw, axis=1, keepdims=True)
        mn_ref[:, hd:hd + 1, :] = jnp.broadcast_to(m_new, (bb, 1, LANES))

    mix = jnp.concatenate(mix_parts, axis=1)
    x1 = x_ref[...].reshape(nrows, D_MODEL) + jnp.dot(mix, wout_ref[...], preferred_element_type=f32)
    x1_ref[...] = x1.reshape(bb, tpad, D_MODEL)


def _sample_mixer(x_pad, ckt, cvt, c0, n0, m0, cos, sin, wq_s, w, wgate, wout_s, sink_tile, bd, anorm, qg,
                  kg, gbias, mnorm):
    nb, tpad, _ = x_pad.shape
    bb = SAMPLE_BATCH_BLOCK
    nh = MLSTM_HEADS
    blk = lambda shape: pl.BlockSpec(shape, lambda i: (i,) + (0,) * (len(shape) - 1))
    cblk = (bb, nh, MLSTM_HEAD_DIM, MLSTM_HEAD_DIM)
    return pl.pallas_call(
        _sample_mixer_kernel,
        grid=(nb // bb,),
        in_specs=[blk((bb, tpad, D_MODEL)), blk((bb, KV_WIDTH, WINDOW)), blk((bb, KV_WIDTH, WINDOW)),
                  blk(cblk), blk((bb, nh, MLSTM_HEAD_DIM)), blk((bb, nh, 1)),
                  _const_spec((bb * tpad, LANES)), _const_spec((bb * tpad, LANES)),
                  _const_spec((D_MODEL, ATTN_WIDTH), single=True),
                  _const_spec((D_MODEL, IN_WIDTH), single=True),
                  _const_spec((D_MODEL, LANES), single=True),
                  _const_spec((MIX_WIDTH, D_MODEL), single=True),
                  _const_spec((ATTN_HEADS * tpad, LANES)), _const_spec((LANES, LANES)),
                  _const_spec((1, D_MODEL)),
                  _const_spec((1, LANES)), _const_spec((1, LANES)), _const_spec((1, LANES)),
                  _const_spec((nh, MLSTM_HEAD_DIM))],
        out_specs=[blk((bb, tpad, D_MODEL)), blk((bb, KV_WIDTH, WINDOW)), blk((bb, KV_WIDTH, WINDOW)),
                   blk(cblk), blk((bb, nh, MLSTM_HEAD_DIM)), blk((bb, tpad, LANES))],
        out_shape=[jax.ShapeDtypeStruct((nb, tpad, D_MODEL), f32),
                   jax.ShapeDtypeStruct((nb, KV_WIDTH, WINDOW), f32),
                   jax.ShapeDtypeStruct((nb, KV_WIDTH, WINDOW), f32),
                   jax.ShapeDtypeStruct((nb,) + cblk[1:], f32),
                   jax.ShapeDtypeStruct((nb, nh, MLSTM_HEAD_DIM), f32),
                   jax.ShapeDtypeStruct((nb, tpad, LANES), f32)],
        compiler_params=pltpu.CompilerParams(
            dimension_semantics=("arbitrary",), vmem_limit_bytes=VMEM_LIMIT),
        name="sample_mixer",
    )(x_pad, ckt, cvt, c0, n0, m0, cos, sin, wq_s, w, wgate, wout_s, sink_tile, bd, anorm, qg, kg, gbias,
      mnorm)


def _rope_angles(pos):
    half = ATTN_HEAD_DIM // 2
    inv = ROPE_THETA ** (-np.arange(half, dtype=np.float64) / half)
    ang = pos.astype(np.float64)[:, None] * inv[None, :]
    return np.cos(ang).astype(np.float32), np.sin(ang).astype(np.float32)


def _rope_tables(pos):
    c, s = _rope_angles(pos)
    cos = np.tile(c, (1, LANES // QUARTER))
    sin = np.tile(np.concatenate([-s, s], axis=1), (1, LANES // ATTN_HEAD_DIM))
    return cos, sin


def _rope_tables_quarters(pos):
    c, s = _rope_angles(pos)
    return np.tile(c, (1, LANES // QUARTER)), np.concatenate([-s, -s, s, s], axis=1)


def _quarters(a):
    lo, hi = a[..., :QUARTER], a[..., QUARTER:]
    return jnp.concatenate([lo, lo, hi, hi], axis=-1)


def _prompt_attn_weights(w):
    d = w.shape[0]
    wq = w[:, COL_QA:COL_KA].reshape(d, ATTN_WIDTH // LANES, 2, 2, QUARTER)
    wq = wq.transpose(0, 1, 3, 2, 4).reshape(d, ATTN_WIDTH)
    wk = _quarters(w[:, COL_KA:COL_VA].reshape(d, KV_HEADS, ATTN_HEAD_DIM)).reshape(d, KV_HEADS * LANES)
    wv = w[:, COL_VA:COL_QM].reshape(d, KV_HEADS, 1, ATTN_HEAD_DIM)
    wv = jnp.broadcast_to(wv, (d, KV_HEADS, 2, ATTN_HEAD_DIM)).reshape(d, KV_HEADS * LANES)
    return jnp.concatenate([wq, wk, wv], axis=1)


def kernel(x_prompt, x_sample, cache_k, cache_v, state_C, state_n, state_m, attn_norm, w_in, q_norm,
           k_norm, attn_sinks, b_ig, b_fg, mlstm_norm, w_out, ffn_norm, w_gate, w_up, w_down):
    assert w_in.shape[0] == 1 and x_prompt.shape[0] == 1
    tp = x_prompt.shape[1]
    nb, nt = x_sample.shape[0], x_sample.shape[1]
    assert nt == SAMPLE_TOKENS
    tpad = SUBLANES
    nh = MLSTM_HEADS

    w = w_in[0].astype(bf16)
    pad_a = jnp.zeros((D_MODEL, FG_LANE - nh), bf16)
    pad_b = jnp.zeros((D_MODEL, LANES - FG_LANE - nh), bf16)
    wgate = jnp.concatenate([w[:, COL_G:COL_G + nh], pad_a, w[:, COL_G + nh:], pad_b], axis=1)
    gbias = jnp.concatenate(
        [b_ig[0], jnp.zeros((FG_LANE - nh,), f32), b_fg[0], jnp.zeros((LANES - FG_LANE - nh,), f32)]
    ).reshape(1, LANES)
    wout_b = w_out[0].astype(bf16)
    wg_b = w_gate[0].astype(bf16)
    wu_b = w_up[0].astype(bf16)
    wd_b = w_down[0].astype(bf16)
    anorm = attn_norm[0].reshape(1, D_MODEL)
    fnorm = ffn_norm[0].reshape(1, D_MODEL)
    qg = jnp.tile(q_norm[0], LANES // ATTN_HEAD_DIM).reshape(1, LANES)
    kg = jnp.tile(k_norm[0], LANES // ATTN_HEAD_DIM).reshape(1, LANES)
    mnorm = mlstm_norm[0].reshape(nh, MLSTM_HEAD_DIM)
    sinks = attn_sinks[0]

    wa = _prompt_attn_weights(w)
    idx = np.arange(2 * LANES)
    same = (idx[:, None] // LANES == idx[None, :] // LANES) & (
        (idx[:, None] // QUARTER) % 2 == (idx[None, :] // QUARTER) % 2)
    bd = jnp.asarray(same, dtype=bf16)
    sink_rows_p = jnp.repeat(sinks.reshape(KV_HEADS, ATTN_GROUP), QBLOCK, axis=1)
    bias = jnp.where(jnp.arange(2 * QBLOCK)[None, None, :] == 0, sink_rows_p[:, :, None], NEG)
    qgq = _quarters(q_norm[0]).reshape(1, LANES)
    kgq = _quarters(k_norm[0]).reshape(1, LANES)
    pos_p = np.arange(tp, dtype=np.float32)
    cos_p, sin_p = _rope_tables_quarters(pos_p)
    cos_w, sin_w = _rope_tables(pos_p[tp - WINDOW:])
    y_p, k_p, v_p, cext_p, m_p = _prompt_layer(
        x_prompt[0], cos_p, sin_p, wa, w, wgate, wout_b, wg_b, wu_b, wd_b, bd, bias, anorm, fnorm, qgq, kgq,
        kg, cos_w, sin_w, gbias, mnorm)

    wq_s = w[:, COL_QA:COL_KA].reshape(D_MODEL, KV_HEADS, ATTN_GROUP, ATTN_HEAD_DIM)
    wq_s = wq_s.transpose(0, 2, 1, 3).reshape(D_MODEL, ATTN_WIDTH)
    wo_a = wout_b[:ATTN_WIDTH].reshape(KV_HEADS, ATTN_GROUP, ATTN_HEAD_DIM, D_MODEL)
    wo_a = wo_a.transpose(1, 0, 2, 3).reshape(ATTN_WIDTH, D_MODEL)
    wout_s = jnp.concatenate([wo_a, wout_b[ATTN_WIDTH:]], axis=0)
    sink_tile = jnp.broadcast_to(
        jnp.repeat(sinks.reshape(KV_HEADS, ATTN_GROUP).T.reshape(-1), tpad)[:, None],
        (ATTN_HEADS * tpad, LANES))
    lanes = np.arange(LANES)
    bd_s = jnp.asarray(lanes[:, None] // ATTN_HEAD_DIM == lanes[None, :] // ATTN_HEAD_DIM, dtype=bf16)
    cos_s, sin_s = _rope_tables(np.arange(tpad, dtype=np.float32) + np.float32(PAST_LEN))
    cos_s = np.tile(cos_s, (SAMPLE_BATCH_BLOCK, 1))
    sin_s = np.tile(sin_s, (SAMPLE_BATCH_BLOCK, 1))
    x_pad = jnp.pad(x_sample, ((0, 0), (0, tpad - nt), (0, 0)))
    ckt = cache_k[0].reshape(nb, WINDOW, KV_WIDTH).transpose(0, 2, 1)
    cvt = cache_v[0].reshape(nb, WINDOW, KV_WIDTH).transpose(0, 2, 1)
    x1_pad, nkt, nvt, c_new, n_new, m_pad = _sample_mixer(
        x_pad, ckt, cvt, state_C[0], state_n[0], state_m[0][:, :, None], cos_s, sin_s, wq_s, w, wgate,
        wout_s, sink_tile, bd_s, anorm, qg, kg, gbias, mnorm)
    y_s = _ffn(x1_pad[:, :nt].reshape(nb * nt, D_MODEL), fnorm, wg_b, wu_b, wd_b)
    m_new = m_pad[:, :nh, 0]

    new_k_s = nkt.transpose(0, 2, 1)
    new_v_s = nvt.transpose(0, 2, 1)

    kv_shape = (1, 1, WINDOW, KV_HEADS, ATTN_HEAD_DIM)
    return (
        y_p[None],
        y_s.reshape(nb, nt, D_MODEL),
        k_p.reshape(kv_shape),
        v_p.reshape(kv_shape),
        cext_p[None, None, :, :, :MLSTM_HEAD_DIM],
        cext_p[None, None, :, :, MLSTM_HEAD_DIM],
        m_p[None, None, :nh, 0],
        new_k_s.reshape(1, nb, WINDOW, KV_HEADS, ATTN_HEAD_DIM),
        new_v_s.reshape(1, nb, WINDOW, KV_HEADS, ATTN_HEAD_DIM),
        c_new.reshape(1, nb, nh, MLSTM_HEAD_DIM, MLSTM_HEAD_DIM),
        n_new.reshape(1, nb, nh, MLSTM_HEAD_DIM),
        m_new.reshape(1, nb, nh),
    )
```

```python
import jax
import jax.numpy as jnp
import numpy as np
from jax import lax
from jax.experimental import pallas as pl
from jax.experimental.pallas import tpu as pltpu

D_MODEL = 1024
PAST_LEN = 16384
ATTN_HEADS = 8
KV_HEADS = 2
ATTN_HEAD_DIM = 64
ATTN_GROUP = ATTN_HEADS // KV_HEADS
ATTN_WIDTH = ATTN_HEADS * ATTN_HEAD_DIM
KV_WIDTH = KV_HEADS * ATTN_HEAD_DIM
WINDOW = 128
ROPE_THETA = 10000.0
MLSTM_HEADS = 4
MLSTM_HEAD_DIM = 128
MLSTM_WIDTH = MLSTM_HEADS * MLSTM_HEAD_DIM
MIX_WIDTH = ATTN_WIDTH + MLSTM_WIDTH
D_FF = 2816
NORM_EPS = 1e-6

LANES = 128
SUBLANES = 8
VMEM_LIMIT = 56 * 1024 * 1024

COL_QA = 0
COL_KA = COL_QA + ATTN_WIDTH
COL_VA = COL_KA + KV_WIDTH
COL_QM = COL_VA + KV_WIDTH
COL_KM = COL_QM + MLSTM_WIDTH
COL_VM = COL_KM + MLSTM_WIDTH
COL_OM = COL_VM + MLSTM_WIDTH
COL_G = COL_OM + MLSTM_WIDTH
IN_WIDTH = COL_G + 2 * MLSTM_HEADS
FG_LANE = SUBLANES

PROMPT_BLOCK = 256
QBLOCK = WINDOW
MCHUNK = 128
PROJ_CHUNK = 256
FFN_BLOCK = 512
FFN_CHUNK = 256
SAMPLE_BATCH_BLOCK = 16
SAMPLE_TOKENS = 4
NEG = -1e30

f32 = jnp.float32
bf16 = jnp.bfloat16


def _rms(x, gain):
    return x * lax.rsqrt(jnp.mean(x * x, axis=-1, keepdims=True) + NORM_EPS) * gain


def _segsum64(s, lane):
    for k in (1, 2, 4, 8, 16, 32):
        s = s + jnp.where((lane & k) != 0, pltpu.roll(s, k, 1), pltpu.roll(s, LANES - k, 1))
    return s


def _headnorm_rope(xs, gain, cos, sin_signed, lane):
    ss = _segsum64(xs * xs, lane)
    y = xs * lax.rsqrt(ss * (1.0 / ATTN_HEAD_DIM) + NORM_EPS) * gain
    partner = jnp.where((lane & 32) != 0, pltpu.roll(y, 32, 1), pltpu.roll(y, LANES - 32, 1))
    return y * cos + partner * sin_signed


def _group_sumsq(xs, bd_ref):
    x2 = xs * xs
    hi = x2.astype(bf16)
    lo = (x2 - hi.astype(f32)).astype(bf16)
    return (jnp.dot(hi, bd_ref[...], preferred_element_type=f32)
            + jnp.dot(lo, bd_ref[...], preferred_element_type=f32))


PA_Q = 0
PA_K = PA_Q + ATTN_WIDTH
PA_V = PA_K + KV_HEADS * LANES
PA_WIDTH = PA_V + KV_HEADS * LANES
QUARTER = ATTN_HEAD_DIM // 2
QA_Q = 0
QA_K = QA_Q + 2 * ATTN_WIDTH
QA_V = QA_K + KV_HEADS * LANES
QA_VZ = QA_V + KV_HEADS * LANES
QA_WIDTH = QA_VZ + KV_HEADS * LANES
ZM_WIDTH = 4 * MLSTM_WIDTH


def _norm_rope_quarters(xs, ss, gain, cos, sin_signed):
    y = xs * lax.rsqrt(ss * (1.0 / ATTN_HEAD_DIM) + NORM_EPS) * gain
    return y * cos + pltpu.roll(y, LANES // 2, 1) * sin_signed


def _col_chunks(width):
    return [(c, min(c + PROJ_CHUNK, width)) for c in range(0, width, PROJ_CHUNK)]


def _mixer_proj_jobs(x_ref, anorm_ref, wa_ref, w_ref, wgate_ref, za_ref, zm_ref, gz_ref):
    h = _rms(x_ref[...], anorm_ref[...]).astype(bf16)

    def proj_job(w_src, wc0, z_ref, c0, c1):
        def run():
            z_ref[:, c0:c1] = jnp.dot(h, w_src[:, wc0 + c0:wc0 + c1], preferred_element_type=f32)
        return run

    return ([proj_job(wa_ref, 0, za_ref, c0, c1) for c0, c1 in _col_chunks(PA_WIDTH)]
            + [proj_job(w_ref, COL_QM, zm_ref, c0, c1) for c0, c1 in _col_chunks(ZM_WIDTH)]
            + [proj_job(wgate_ref, 0, gz_ref, 0, LANES)])


def _mixer_prep_jobs(za_ref, gz_ref, cos_ref, sin_ref, bd_ref, qg_ref, kg_ref, gbias_ref, qa_ref,
                     colf_ref, urow_ref, wrow_ref, mst_ref, mout_ref):
    tb = za_ref.shape[0]

    def gates_job():
        lane_t = lax.broadcasted_iota(jnp.int32, (tb, LANES), 1)
        gcol = gz_ref[...] + gbias_ref[...]
        acol = jnp.where(lane_t < FG_LANE, gcol, jax.nn.log_sigmoid(gcol))
        arow = acol.T
        lane8 = lax.broadcasted_iota(jnp.int32, (SUBLANES, LANES), 1)
        lane_in = lane8 & (MCHUNK - 1)
        m_prev = mst_ref[:, 0:1]
        stacks = []
        for sb in range(tb // LANES):
            ls = slice(sb * LANES, (sb + 1) * LANES)
            ig8 = arow[0:SUBLANES, ls]
            lf8 = arow[FG_LANE:FG_LANE + SUBLANES, ls]
            b8 = lf8
            k = 1
            while k < MCHUNK:
                b8 = b8 + jnp.where(lane_in >= k, pltpu.roll(b8, k, 1), 0.0)
                k *= 2
            u8 = ig8 - b8
            cm8 = u8
            k = 1
            while k < MCHUNK:
                cm8 = jnp.maximum(cm8, jnp.where(lane_in >= k, pltpu.roll(cm8, k, 1), NEG))
                k *= 2
            g8 = jnp.zeros_like(u8)
            mp8 = jnp.zeros_like(u8)
            gl8 = jnp.zeros_like(u8)
            for c in range(LANES // MCHUNK):
                in_chunk = (lane8 // MCHUNK) == c
                gc = jnp.maximum(cm8, m_prev)
                last = c * MCHUNK + MCHUNK - 1
                g_last = jnp.max(jnp.where(lane8 == last, gc, NEG), axis=1, keepdims=True)
                b_last = jnp.max(jnp.where(lane8 == last, b8, NEG), axis=1, keepdims=True)
                g8 = jnp.where(in_chunk, gc, g8)
                mp8 = jnp.where(in_chunk, m_prev, mp8)
                gl8 = jnp.where(in_chunk, g_last, gl8)
                m_prev = b_last + g_last
            a8 = jnp.exp(mp8 - g8)
            emt8 = jnp.exp(-(b8 + g8))
            aend8 = jnp.exp(mp8 - gl8)
            stacks.append(jnp.concatenate(
                [g8, a8, emt8, aend8, jnp.zeros((LANES - 4 * SUBLANES, LANES), f32)], axis=0))
            urow_ref[:, ls] = u8
            wrow_ref[:, ls] = jnp.exp(u8 - gl8)
        mst_ref[...] = jnp.broadcast_to(m_prev, mst_ref.shape)
        mout_ref[...] = jnp.broadcast_to(m_prev, mout_ref.shape)
        colf_ref[...] = jnp.concatenate(stacks, axis=1).T

    def prep_job(qb):
        def run():
            rows = slice(qb * QBLOCK, (qb + 1) * QBLOCK)
            lane = lax.broadcasted_iota(jnp.int32, (QBLOCK, LANES), 1)
            head_a = ((lane // QUARTER) & 1) == 0
            row0 = lax.broadcasted_iota(jnp.int32, (QBLOCK, LANES), 0) == 0
            cos = cos_ref[rows, :]
            sin = sin_ref[rows, :]
            ss = [_group_sumsq(za_ref[rows, d * 2 * LANES:(d + 1) * 2 * LANES], bd_ref)
                  for d in range(PA_V // (2 * LANES))]
            for j in range(PA_V // LANES):
                is_q = j < ATTN_WIDTH // LANES
                y = _norm_rope_quarters(za_ref[rows, j * LANES:(j + 1) * LANES],
                                        ss[j // 2][:, (j % 2) * LANES:(j % 2 + 1) * LANES],
                                        qg_ref[...] if is_q else kg_ref[...], cos, sin)
                if is_q:
                    y = y * (ATTN_HEAD_DIM ** -0.5)
                    qa_ref[rows, QA_Q + 2 * j * LANES:QA_Q + (2 * j + 1) * LANES] = (
                        jnp.where(head_a, y, 0.0).astype(bf16))
                    qa_ref[rows, QA_Q + (2 * j + 1) * LANES:QA_Q + (2 * j + 2) * LANES] = (
                        jnp.where(head_a, 0.0, y).astype(bf16))
                else:
                    c = j - ATTN_WIDTH // LANES
                    qa_ref[rows, QA_K + c * LANES:QA_K + (c + 1) * LANES] = y.astype(bf16)
            for c in range(KV_HEADS):
                v = za_ref[rows, PA_V + c * LANES:PA_V + (c + 1) * LANES]
                qa_ref[rows, QA_V + c * LANES:QA_V + (c + 1) * LANES] = v.astype(bf16)
                qa_ref[rows, QA_VZ + c * LANES:QA_VZ + (c + 1) * LANES] = (
                    jnp.where(row0, 0.0, v).astype(bf16))
        return run

    return [gates_job] + [prep_job(qb) for qb in range(tb // QBLOCK)]


def _mixer_outproj_jobs(xs_ref, mix_ref, wout_ref, x1_ref):
    def job(c0, c1):
        def run():
            x1_ref[:, c0:c1] = xs_ref[:, c0:c1] + jnp.dot(mix_ref[...], wout_ref[:, c0:c1],
                                                          preferred_element_type=f32)
        return run

    return [job(c0, c1) for c0, c1 in _col_chunks(D_MODEL)]


def _mixer_window_out(xs_ref, anorm_ref, w_ref, kgs_ref, coss_ref, sins_ref, kout_ref, vout_ref):
    tb = xs_ref.shape[0]
    h = _rms(xs_ref[tb - WINDOW:, :], anorm_ref[...]).astype(bf16)
    zs = jnp.dot(h, w_ref[:, COL_KA:COL_QM], preferred_element_type=f32)
    lane_s = lax.broadcasted_iota(jnp.int32, (WINDOW, LANES), 1)
    kout_ref[...] = _headnorm_rope(zs[:, :KV_WIDTH], kgs_ref[...], coss_ref[...], sins_ref[...], lane_s)
    vout_ref[...] = zs[:, KV_WIDTH:]


def _mixer_core(first_block, last_block, head_fillers, fillers, prep_jobs, qa_ref, zm_ref, colf_ref,
                urow_ref, wrow_ref, mix_ref, bias_ref, mnorm_ref, kprev_ref, vprev_ref, cst_ref, cext_ref):
    tb = qa_ref.shape[0]
    nqb = tb // QBLOCK
    nch = tb // MCHUNK

    def spread(jobs, slots):
        jobs = list(jobs)
        total = len(jobs)
        progress = [0]

        def fill():
            progress[0] += 1
            while total - len(jobs) < min(total, -(-total * progress[0] // slots)):
                jobs.pop(0)()
        return fill

    fill_head = spread(head_fillers, len(prep_jobs))
    for job in prep_jobs:
        job()
        fill_head()
    fill = spread(fillers, nqb * KV_HEADS + nch * MLSTM_HEADS)

    low_half = lax.broadcasted_iota(jnp.int32, (QBLOCK, LANES), 1) < ATTN_HEAD_DIM
    qi = lax.broadcasted_iota(jnp.int32, (ATTN_GROUP * QBLOCK, 2 * QBLOCK), 0) & (QBLOCK - 1)
    kj = lax.broadcasted_iota(jnp.int32, (ATTN_GROUP * QBLOCK, 2 * QBLOCK), 1)
    band = (kj > qi) & (kj <= qi + QBLOCK)
    ones_slab = jnp.ones((2 * QBLOCK, LANES), bf16)

    def attn_unit(qb, c):
        rows = slice(qb * QBLOCK, (qb + 1) * QBLOCK)
        kcols = slice(QA_K + c * LANES, QA_K + (c + 1) * LANES)
        if qb == 0:
            kprev, vprev = kprev_ref[c], vprev_ref[c]
        else:
            prev = slice((qb - 1) * QBLOCK, qb * QBLOCK)
            kprev = qa_ref[prev, kcols]
            vprev = qa_ref[prev, QA_VZ + c * LANES:QA_VZ + (c + 1) * LANES]
        kcat = jnp.concatenate([kprev, qa_ref[rows, kcols]], axis=0)
        vcat = jnp.concatenate([vprev, qa_ref[rows, QA_V + c * LANES:QA_V + (c + 1) * LANES]], axis=0)
        vext = jnp.concatenate([vcat, ones_slab], axis=1)
        q0 = QA_Q + c * ATTN_GROUP * LANES
        qst = jnp.concatenate([qa_ref[rows, q0 + g * LANES:q0 + (g + 1) * LANES]
                               for g in range(ATTN_GROUP)], axis=0)
        s = lax.dot_general(qst, kcat, (((1,), (1,)), ((), ())), preferred_element_type=f32)
        valid = band & (kj >= QBLOCK) if (first_block and qb == 0) else band
        s = jnp.where(valid, s, bias_ref[c])
        p = jnp.exp(s - jnp.max(s, axis=-1, keepdims=True)).astype(bf16)
        of = jnp.dot(p, vext, preferred_element_type=f32)
        o = of[:, :LANES] / of[:, LANES:]
        for jj in range(2):
            pair = jnp.where(low_half, o[(2 * jj) * QBLOCK:(2 * jj + 1) * QBLOCK],
                             o[(2 * jj + 1) * QBLOCK:(2 * jj + 2) * QBLOCK])
            col = (2 * c + jj) * LANES
            mix_ref[rows, col:col + LANES] = pair.astype(bf16)

    ti = lax.broadcasted_iota(jnp.int32, (MCHUNK, MCHUNK), 0)
    si = lax.broadcasted_iota(jnp.int32, (MCHUNK, MCHUNK), 1)
    causal = si <= ti
    ones_l = jnp.ones((MCHUNK, LANES), bf16)

    def mlstm_unit(hd, c, cext):
        hcols = lambda k: slice((k * MLSTM_HEADS + hd) * MLSTM_HEAD_DIM,
                                (k * MLSTM_HEADS + hd + 1) * MLSTM_HEAD_DIM)
        rows = slice(c * MCHUNK, (c + 1) * MCHUNK)
        qb_ = zm_ref[rows, hcols(0)].astype(bf16)
        kf = zm_ref[rows, hcols(1)] * (MLSTM_HEAD_DIM ** -0.5)
        vb = zm_ref[rows, hcols(2)].astype(bf16)
        og = zm_ref[rows, hcols(3)]
        g_c = colf_ref[rows, hd:hd + 1]
        a_c = colf_ref[rows, SUBLANES + hd:SUBLANES + hd + 1]
        emt_c = colf_ref[rows, 2 * SUBLANES + hd:2 * SUBLANES + hd + 1]
        aend = colf_ref[c * MCHUNK:c * MCHUNK + 1, 3 * SUBLANES + hd:3 * SUBLANES + hd + 1]
        u_r = urow_ref[hd:hd + 1, rows]
        w_r = wrow_ref[hd:hd + 1, rows]
        dmat = jnp.exp(jnp.where(causal, u_r - g_c, NEG))
        smat = lax.dot_general(qb_, kf.astype(bf16), (((1,), (1,)), ((), ())),
                               preferred_element_type=f32) * dmat
        vext = jnp.concatenate([vb, ones_l], axis=1)
        nd = (a_c * jnp.dot(qb_, cext.astype(bf16), preferred_element_type=f32)
              + jnp.dot(smat.astype(bf16), vext, preferred_element_type=f32))
        hraw = nd[:, :MLSTM_HEAD_DIM] / jnp.maximum(jnp.abs(nd[:, MLSTM_HEAD_DIM:]), emt_c)
        hn = _rms(hraw, mnorm_ref[hd:hd + 1, :])
        mix_ref[rows, ATTN_WIDTH + hd * MLSTM_HEAD_DIM:ATTN_WIDTH + (hd + 1) * MLSTM_HEAD_DIM] = (
            (hn * jax.nn.sigmoid(og)).astype(bf16))
        kw_t = (kf.T * w_r).astype(bf16)
        return aend * cext + jnp.dot(kw_t, vext, preferred_element_type=f32)

    cexts =[cst_ref[hd] for hd in range(MLSTM_HEADS)]
    attn_jobs = [(qb, c) for qb in range(nqb) for c in range(KV_HEADS)]
    mlstm_jobs = [(hd, c) for c in range(nch) for hd in range(MLSTM_HEADS)]
    while attn_jobs or mlstm_jobs:
        if attn_jobs:
            attn_unit(*attn_jobs.pop(0))
            fill()
        for _ in range(2 if len(mlstm_jobs) > 2 * len(attn_jobs) else 1):
            if mlstm_jobs:
                hd, c = mlstm_jobs.pop(0)
                cexts[hd] = mlstm_unit(hd, c, cexts[hd])
                fill()
    last_rows = slice(tb - QBLOCK, tb)
    for c in range(KV_HEADS):
        kprev_ref[c] = qa_ref[last_rows, QA_K + c * LANES:QA_K + (c + 1) * LANES]
        vprev_ref[c] = qa_ref[last_rows, QA_VZ + c * LANES:QA_VZ + (c + 1) * LANES]
    for hd in range(MLSTM_HEADS):
        cst_ref[hd] = cexts[hd]
        if last_block:
            cext_ref[hd] = cexts[hd]


def _prompt_mixer_kernel(x_ref, xres_ref, cos_ref, sin_ref, wa_ref, w_ref, wgate_ref, wout_ref, bd_ref,
                         bias_ref, anorm_ref, qg_ref, kg_ref, kgs_ref, coss_ref, sins_ref, gbias_ref,
                         mnorm_ref, x1_ref, kout_ref, vout_ref, cext_ref, mout_ref,
                         za_ref, zm_ref, gz_ref, zmw_ref, mix_ref,
                         qa_ref, colf_ref, urow_ref, wrow_ref, kprev_ref, vprev_ref, cst_ref, mst_ref):
    step = pl.program_id(0)
    nblk = pl.num_programs(0) - 2

    def run(do_in, do_core, do_out, first_block=False, last_block=False):
        out_jobs = _mixer_outproj_jobs(xres_ref, mix_ref, wout_ref, x1_ref) if do_out else []
        proj_jobs = (_mixer_proj_jobs(x_ref, anorm_ref, wa_ref, w_ref, wgate_ref, za_ref, zm_ref, gz_ref)
                     if do_in else [])
        if do_core:
            prep = _mixer_prep_jobs(za_ref, gz_ref, cos_ref, sin_ref, bd_ref, qg_ref, kg_ref, gbias_ref,
                                    qa_ref, colf_ref, urow_ref, wrow_ref, mst_ref, mout_ref)

            def keep_zm():
                zmw_ref[...] = zm_ref[...]

            _mixer_core(first_block, last_block, out_jobs, proj_jobs, prep + [keep_zm], qa_ref, zmw_ref,
                        colf_ref, urow_ref, wrow_ref, mix_ref, bias_ref, mnorm_ref, kprev_ref, vprev_ref,
                        cst_ref, cext_ref)
        else:
            for job in out_jobs + proj_jobs:
                job()
        if last_block:
            _mixer_window_out(x_ref, anorm_ref, w_ref, kgs_ref, coss_ref, sins_ref, kout_ref, vout_ref)

    @pl.when(step == 0)
    def _first():
        kprev_ref[...] = jnp.zeros_like(kprev_ref)
        vprev_ref[...] = jnp.zeros_like(vprev_ref)
        cst_ref[...] = jnp.zeros_like(cst_ref)
        mst_ref[...] = jnp.zeros_like(mst_ref)
        run(True, False, False)

    @pl.when(step == 1)
    def _second():
        run(True, True, False, first_block=True)

    @pl.when((step >= 2) & (step < nblk))
    def _steady():
        run(True, True, True)

    @pl.when(step == nblk)
    def _drain_core():
        run(False, True, True, last_block=True)

    @pl.when(step == nblk + 1)
    def _drain_out():
        run(False, False, True)


def _const_spec(shape, single=False):
    nd = len(shape)
    if single:
        return pl.BlockSpec(shape, lambda i, *_: (0,) * nd, pipeline_mode=pl.Buffered(1))
    return pl.BlockSpec(shape, lambda i, *_: (0,) * nd)


def _prompt_mixer(x, cos, sin, wa, w, wgate, wout_b, bd, bias, anorm, qg, kg, kgs, coss, sins, gbias, mnorm):
    t = x.shape[0]
    tb = PROMPT_BLOCK
    nblk = t // tb
    assert nblk >= 3
    state_shape = (MLSTM_HEADS, MLSTM_HEAD_DIM, 2 * MLSTM_HEAD_DIM)
    last = nblk - 1
    lag = lambda d: (lambda i: (jnp.clip(i - d, 0, last), 0))
    return pl.pallas_call(
        _prompt_mixer_kernel,
        grid=(nblk + 2,),
        in_specs=[
            pl.BlockSpec((tb, D_MODEL), lag(0)),
            pl.BlockSpec((tb, D_MODEL), lag(2)),
            pl.BlockSpec((tb, LANES), lag(1)),
            pl.BlockSpec((tb, LANES), lag(1)),
            _const_spec((D_MODEL, PA_WIDTH), single=True),
            _const_spec((D_MODEL, IN_WIDTH), single=True),
            _const_spec((D_MODEL, LANES), single=True),
            _const_spec((MIX_WIDTH, D_MODEL), single=True),
            _const_spec((2 * LANES, 2 * LANES), single=True),
            _const_spec((KV_HEADS, ATTN_GROUP * QBLOCK, 2 * QBLOCK), single=True),
            _const_spec((1, D_MODEL)),
            _const_spec((1, LANES)),
            _const_spec((1, LANES)),
            _const_spec((1, LANES)),
            _const_spec((WINDOW, LANES)),
            _const_spec((WINDOW, LANES)),
            _const_spec((1, LANES)),
            _const_spec((MLSTM_HEADS, MLSTM_HEAD_DIM)),
        ],
        out_specs=[
            pl.BlockSpec((tb, D_MODEL), lag(2)),
            _const_spec((WINDOW, KV_WIDTH)),
            _const_spec((WINDOW, KV_WIDTH)),
            _const_spec(state_shape),
            _const_spec((SUBLANES, LANES)),
        ],
        out_shape=[
            jax.ShapeDtypeStruct((t, D_MODEL), f32),
            jax.ShapeDtypeStruct((WINDOW, KV_WIDTH), f32),
            jax.ShapeDtypeStruct((WINDOW, KV_WIDTH), f32),
            jax.ShapeDtypeStruct(state_shape, f32),
            jax.ShapeDtypeStruct((SUBLANES, LANES), f32),
        ],
        scratch_shapes=[
            pltpu.VMEM((tb, PA_WIDTH), f32), pltpu.VMEM((tb, ZM_WIDTH), f32), pltpu.VMEM((tb, LANES), f32),
            pltpu.VMEM((tb, ZM_WIDTH), f32), pltpu.VMEM((tb, MIX_WIDTH), bf16),
            pltpu.VMEM((tb, QA_WIDTH), bf16), pltpu.VMEM((tb, LANES), f32),
            pltpu.VMEM((SUBLANES, tb), f32), pltpu.VMEM((SUBLANES, tb), f32),
            pltpu.VMEM((KV_HEADS, WINDOW, LANES), bf16), pltpu.VMEM((KV_HEADS, WINDOW, LANES), bf16),
            pltpu.VMEM(state_shape, f32), pltpu.VMEM((SUBLANES, LANES), f32)],
        compiler_params=pltpu.CompilerParams(
            dimension_semantics=("arbitrary",), vmem_limit_bytes=VMEM_LIMIT),
        name="prompt_mixer",
    )(x, x, cos, sin, wa, w, wgate, wout_b, bd, bias, anorm, qg, kg, kgs, coss, sins, gbias, mnorm)


def _ffn_kernel(xp_ref, xs_ref, g_ref, wg_ref, wu_ref, wd_ref, op_ref, os_ref):
    step = pl.program_id(0)
    last = pl.num_programs(0) - 1

    @pl.when(step < last)
    def _prompt_rows():
        _ffn_rows(xp_ref, g_ref, wg_ref, wu_ref, wd_ref, op_ref)

    @pl.when(step == last)
    def _sample_rows():
        _ffn_rows(xs_ref, g_ref, wg_ref, wu_ref, wd_ref, os_ref)


def _ffn_rows(x_ref, g_ref, wg_ref, wu_ref, wd_ref, o_ref):
    x = x_ref[...]
    hf = _rms(x, g_ref[...]).astype(bf16)
    acc = x
    for c in range(D_FF // FFN_CHUNK):
        cs = slice(c * FFN_CHUNK, (c + 1) * FFN_CHUNK)
        gate = jnp.dot(hf, wg_ref[:, cs], preferred_element_type=f32)
        up = jnp.dot(hf, wu_ref[:, cs], preferred_element_type=f32)
        act = (gate * jax.nn.sigmoid(gate) * up).astype(bf16)
        acc = acc + jnp.dot(act, wd_ref[cs, :], preferred_element_type=f32)
    o_ref[...] = acc


def _ffn(x_p, x_s, fnorm, wg_b, wu_b, wd_b):
    n = x_p.shape[0]
    ns = x_s.shape[0]
    tm = FFN_BLOCK
    last = n // tm - 1
    return pl.pallas_call(
        _ffn_kernel,
        grid=(n // tm + 1,),
        in_specs=[
            pl.BlockSpec((tm, D_MODEL), lambda i: (jnp.minimum(i, last), 0)),
            _const_spec((ns, D_MODEL), single=True),
            _const_spec((1, D_MODEL)),
            _const_spec((D_MODEL, D_FF), single=True),
            _const_spec((D_MODEL, D_FF), single=True),
            _const_spec((D_FF, D_MODEL), single=True),
        ],
        out_specs=[pl.BlockSpec((tm, D_MODEL), lambda i: (jnp.minimum(i, last), 0)),
                   _const_spec((ns, D_MODEL))],
        out_shape=[jax.ShapeDtypeStruct((n, D_MODEL), f32), jax.ShapeDtypeStruct((ns, D_MODEL), f32)],
        compiler_params=pltpu.CompilerParams(
            dimension_semantics=("arbitrary",), vmem_limit_bytes=VMEM_LIMIT),
        name="ffn",
    )(x_p, x_s, fnorm, wg_b, wu_b, wd_b)


def _sample_mixer_kernel(x_ref, ckt_ref, cvt_ref, c_ref, n_ref, m_ref, cos_ref, sin_ref, wqs_ref, w_ref,
                         wgate_ref, wout_ref, sink_ref, bd_ref, anorm_ref, qg_ref, kg_ref, gbias_ref,
                         mnorm_ref, x1_ref, nkt_ref, nvt_ref, cn_ref, nn_ref, mn_ref):
    bb, tpad, _ = x_ref.shape
    nrows = bb * tpad
    nreal = SAMPLE_TOKENS
    h = _rms(x_ref[...].reshape(nrows, D_MODEL), anorm_ref[...]).astype(bf16)
    z = jnp.concatenate(
        [jnp.dot(h, wqs_ref[...], preferred_element_type=f32),
         jnp.dot(h, w_ref[:, COL_KA:COL_G], preferred_element_type=f32),
         jnp.dot(h, wgate_ref[...], preferred_element_type=f32)], axis=1)
    lane = lax.broadcasted_iota(jnp.int32, (nrows, LANES), 1)
    low = lane < ATTN_HEAD_DIM
    cos = cos_ref[...]
    sin = sin_ref[...]

    def per_seq(a):
        return a.reshape(bb, tpad, a.shape[-1])

    def norm_rope(xs, gain):
        y = xs * lax.rsqrt(_group_sumsq(xs, bd_ref) * (1.0 / ATTN_HEAD_DIM) + NORM_EPS) * gain
        partner = jnp.where((lane & QUARTER) != 0, pltpu.roll(y, QUARTER, 1),
                            pltpu.roll(y, LANES - QUARTER, 1))
        return y * cos + partner * sin

    q_rows = []
    for j in range(ATTN_GROUP):
        qs = norm_rope(z[:, COL_QA + j * LANES:COL_QA + (j + 1) * LANES], qg_ref[...])
        qs = qs * (ATTN_HEAD_DIM ** -0.5)
        q_rows.append(per_seq(jnp.where(low, qs, 0.0)).astype(bf16))
        q_rows.append(per_seq(jnp.where(low, 0.0, qs)).astype(bf16))
    qbd = jnp.concatenate(q_rows, axis=1)
    knew = norm_rope(z[:, COL_KA:COL_KA + KV_WIDTH], kg_ref[...])
    vnew = z[:, COL_VA:COL_VA + KV_WIDTH]
    zpad = jnp.zeros((bb, LANES - tpad, LANES), bf16)
    knp = jnp.concatenate([per_seq(knew).astype(bf16), zpad], axis=1)
    vnp = jnp.concatenate([per_seq(vnew).astype(bf16), zpad], axis=1)
    ckt = ckt_ref[...]
    cvt = cvt_ref[...]
    s = jnp.concatenate(
        [jnp.einsum('bqd,bdw->bqw', qbd, ckt.astype(bf16), preferred_element_type=f32),
         jnp.einsum('bqd,bkd->bqk', qbd, knp, preferred_element_type=f32)], axis=2)
    tq = lax.broadcasted_iota(jnp.int32, s.shape, 1) & (tpad - 1)
    kj = lax.broadcasted_iota(jnp.int32, s.shape, 2)
    valid = ((kj < WINDOW) & (kj > tq)) | ((kj >= WINDOW) & (kj - WINDOW <= tq) & (kj - WINDOW < nreal))
    s = jnp.where(valid, s, NEG)
    sink = sink_ref[:, 0:1][None]
    mx = jnp.maximum(jnp.max(s, axis=-1, keepdims=True), sink)
    p = jnp.exp(s - mx)
    den = jnp.sum(p, axis=-1, keepdims=True) + jnp.exp(sink - mx)
    pb = p.astype(bf16)
    o = (jnp.einsum('bqw,bdw->bqd', pb[:, :, :WINDOW], cvt.astype(bf16), preferred_element_type=f32)
         + jnp.einsum('bqk,bkd->bqd', pb[:, :, WINDOW:], vnp, preferred_element_type=f32)) / den
    low3 = lax.broadcasted_iota(jnp.int32, (bb, tpad, LANES), 2) < ATTN_HEAD_DIM
    mix_parts = []
    for j in range(ATTN_GROUP):
        r0 = 2 * j * tpad
        pair = jnp.where(low3, o[:, r0:r0 + tpad, :], o[:, r0 + tpad:r0 + 2 * tpad, :])
        mix_parts.append(pair.reshape(nrows, LANES).astype(bf16))

    keep = lax.broadcasted_iota(jnp.int32, (KV_WIDTH, WINDOW), 1) < WINDOW - nreal
    knt = knew.T
    vnt = vnew.T
    for b in range(bb):
        shift = (WINDOW - nreal - b * tpad) % LANES
        nkt_ref[b] = jnp.where(keep, pltpu.roll(ckt_ref[b], WINDOW - nreal, 1), pltpu.roll(knt, shift, 1))
        nvt_ref[b] = jnp.where(keep, pltpu.roll(cvt_ref[b], WINDOW - nreal, 1), pltpu.roll(vnt, shift, 1))

    gz = per_seq(z[:, COL_G:COL_G + LANES] + gbias_ref[...])
    lgz = jax.nn.log_sigmoid(gz)
    trow = lax.broadcasted_iota(jnp.int32, (bb, tpad, 1), 1)
    real = trow < nreal
    mn_ref[...] = jnp.zeros_like(mn_ref)
    for hd in range(MLSTM_HEADS):
        hcols = lambda base: slice(base + hd * MLSTM_HEAD_DIM, base + (hd + 1) * MLSTM_HEAD_DIM)
        q = per_seq(z[:, hcols(COL_QM)])
        k = per_seq(z[:, hcols(COL_KM)]) * (MLSTM_HEAD_DIM ** -0.5)
        v = per_seq(z[:, hcols(COL_VM)])
        og = per_seq(z[:, hcols(COL_OM)])
        c0 = c_ref[:, hd]
        n0 = n_ref[:, hd:hd + 1, :]
        m0 = m_ref[:, hd:hd + 1, :]
        ig_c = jnp.where(real, gz[:, :, hd:hd + 1], NEG)
        lf_c = jnp.where(real, lgz[:, :, FG_LANE + hd:FG_LANE + hd + 1], 0.0)
        b_c = jnp.zeros_like(lf_c)
        for sx in range(nreal):
            b_c = b_c + jnp.where(trow >= sx, lf_c[:, sx:sx + 1, :], 0.0)
        dlog = [jnp.where(trow >= sx, b_c - b_c[:, sx:sx + 1, :] + ig_c[:, sx:sx + 1, :], NEG)
                for sx in range(nreal)]
        inter = b_c + m0
        m_t = inter
        for sx in range(nreal):
            m_t = jnp.maximum(m_t, dlog[sx])
        a = jnp.exp(inter - m_t)
        qc = jnp.einsum('btd,bde->bte', q.astype(bf16), c0.astype(bf16), preferred_element_type=f32)
        num = a * qc
        den_m = a * jnp.sum(q * n0, axis=2, keepdims=True)
        for sx in range(nreal):
            sd = jnp.sum(q * k[:, sx:sx + 1, :], axis=2, keepdims=True) * jnp.exp(dlog[sx] - m_t)
            num = num + sd * v[:, sx:sx + 1, :]
            den_m = den_m + sd
        hraw = num / jnp.maximum(jnp.abs(den_m), jnp.exp(-m_t))
        hn = _rms(hraw, mnorm_ref[hd:hd + 1, :][None])
        mix_parts.append((hn * jax.nn.sigmoid(og)).reshape(nrows, MLSTM_HEAD_DIM).astype(bf16))
        last = nreal - 1
        m_new = m_t[:, last:last + 1, :]
        b_last = b_c[:, last:last + 1, :]
        a_end = jnp.exp(b_last + m0 - m_new)
        kw = k * jnp.exp(b_last - b_c + ig_c - m_new)
        cn_ref[:, hd] = a_end * c0 + jnp.einsum('bsd,bse->bde', kw.astype(bf16), v.astype(bf16),
                                               preferred_element_type=f32)
        nn_ref[:, hd:hd + 1, :] = a_end * n0 + jnp.sum(kw, axis=1, keepdims=True)
        mn_ref[:, hd:hd + 1, :] = jnp.broadcast_to(m_new, (bb, 1, LANES))

    mix = jnp.concatenate(mix_parts, axis=1)
    x1 = x_ref[...].reshape(nrows, D_MODEL) + jnp.dot(mix, wout_ref[...], preferred_element_type=f32)
    x1_ref[...] = x1.reshape(bb, tpad, D_MODEL)


def _sample_mixer(x_pad, ckt, cvt, c0, n0, m0, cos, sin, wq_s, w, wgate, wout_s, sink_tile, bd, anorm, qg,
                  kg, gbias, mnorm):
    nb, tpad, _ = x_pad.shape
    bb = SAMPLE_BATCH_BLOCK
    nh = MLSTM_HEADS
    blk = lambda shape: pl.BlockSpec(shape, lambda i: (i,) + (0,) * (len(shape) - 1))
    cblk = (bb, nh, MLSTM_HEAD_DIM, MLSTM_HEAD_DIM)
    return pl.pallas_call(
        _sample_mixer_kernel,
        grid=(nb // bb,),
        in_specs=[blk((bb, tpad, D_MODEL)), blk((bb, KV_WIDTH, WINDOW)), blk((bb, KV_WIDTH, WINDOW)),
                  blk(cblk), blk((bb, nh, MLSTM_HEAD_DIM)), blk((bb, nh, 1)),
                  _const_spec((bb * tpad, LANES)), _const_spec((bb * tpad, LANES)),
                  _const_spec((D_MODEL, ATTN_WIDTH), single=True),
                  _const_spec((D_MODEL, IN_WIDTH), single=True),
                  _const_spec((D_MODEL, LANES), single=True),
                  _const_spec((MIX_WIDTH, D_MODEL), single=True),
                  _const_spec((ATTN_HEADS * tpad, LANES)), _const_spec((LANES, LANES)),
                  _const_spec((1, D_MODEL)),
                  _const_spec((1, LANES)), _const_spec((1, LANES)), _const_spec((1, LANES)),
                  _const_spec((nh, MLSTM_HEAD_DIM))],
        out_specs=[blk((bb, tpad, D_MODEL)), blk((bb, KV_WIDTH, WINDOW)), blk((bb, KV_WIDTH, WINDOW)),
                   blk(cblk), blk((bb, nh, MLSTM_HEAD_DIM)), blk((bb, tpad, LANES))],
        out_shape=[jax.ShapeDtypeStruct((nb, tpad, D_MODEL), f32),
                   jax.ShapeDtypeStruct((nb, KV_WIDTH, WINDOW), f32),
                   jax.ShapeDtypeStruct((nb, KV_WIDTH, WINDOW), f32),
                   jax.ShapeDtypeStruct((nb,) + cblk[1:], f32),
                   jax.ShapeDtypeStruct((nb, nh, MLSTM_HEAD_DIM), f32),
                   jax.ShapeDtypeStruct((nb, tpad, LANES), f32)],
        compiler_params=pltpu.CompilerParams(
            dimension_semantics=("arbitrary",), vmem_limit_bytes=VMEM_LIMIT),
        name="sample_mixer",
    )(x_pad, ckt, cvt, c0, n0, m0, cos, sin, wq_s, w, wgate, wout_s, sink_tile, bd, anorm, qg, kg, gbias,
      mnorm)


def _rope_angles(pos):
    half = ATTN_HEAD_DIM // 2
    inv = ROPE_THETA ** (-np.arange(half, dtype=np.float64) / half)
    ang = pos.astype(np.float64)[:, None] * inv[None, :]
    return np.cos(ang).astype(np.float32), np.sin(ang).astype(np.float32)


def _rope_tables(pos):
    c, s = _rope_angles(pos)
    cos = np.tile(c, (1, LANES // QUARTER))
    sin = np.tile(np.concatenate([-s, s], axis=1), (1, LANES // ATTN_HEAD_DIM))
    return cos, sin


def _rope_tables_quarters(pos):
    c, s = _rope_angles(pos)
    return np.tile(c, (1, LANES // QUARTER)), np.concatenate([-s, -s, s, s], axis=1)


def _quarters(a):
    lo, hi = a[..., :QUARTER], a[..., QUARTER:]
    return jnp.concatenate([lo, lo, hi, hi], axis=-1)


def _prompt_attn_weights(w):
    d = w.shape[0]
    wq = w[:, COL_QA:COL_KA].reshape(d, ATTN_WIDTH // LANES, 2, 2, QUARTER)
    wq = wq.transpose(0, 1, 3, 2, 4).reshape(d, ATTN_WIDTH)
    wk = _quarters(w[:, COL_KA:COL_VA].reshape(d, KV_HEADS, ATTN_HEAD_DIM)).reshape(d, KV_HEADS * LANES)
    wv = w[:, COL_VA:COL_QM].reshape(d, KV_HEADS, 1, ATTN_HEAD_DIM)
    wv = jnp.broadcast_to(wv, (d, KV_HEADS, 2, ATTN_HEAD_DIM)).reshape(d, KV_HEADS * LANES)
    return jnp.concatenate([wq, wk, wv], axis=1)


def kernel(x_prompt, x_sample, cache_k, cache_v, state_C, state_n, state_m, attn_norm, w_in, q_norm,
           k_norm, attn_sinks, b_ig, b_fg, mlstm_norm, w_out, ffn_norm, w_gate, w_up, w_down):
    assert w_in.shape[0] == 1 and x_prompt.shape[0] == 1
    tp = x_prompt.shape[1]
    nb, nt = x_sample.shape[0], x_sample.shape[1]
    assert nt == SAMPLE_TOKENS
    tpad = SUBLANES
    nh = MLSTM_HEADS

    w = w_in[0].astype(bf16)
    pad_a = jnp.zeros((D_MODEL, FG_LANE - nh), bf16)
    pad_b = jnp.zeros((D_MODEL, LANES - FG_LANE - nh), bf16)
    wgate = jnp.concatenate([w[:, COL_G:COL_G + nh], pad_a, w[:, COL_G + nh:], pad_b], axis=1)
    gbias = jnp.concatenate(
        [b_ig[0], jnp.zeros((FG_LANE - nh,), f32), b_fg[0], jnp.zeros((LANES - FG_LANE - nh,), f32)]
    ).reshape(1, LANES)
    wout_b = w_out[0].astype(bf16)
    wg_b = w_gate[0].astype(bf16)
    wu_b = w_up[0].astype(bf16)
    wd_b = w_down[0].astype(bf16)
    anorm = attn_norm[0].reshape(1, D_MODEL)
    fnorm = ffn_norm[0].reshape(1, D_MODEL)
    qg = jnp.tile(q_norm[0], LANES // ATTN_HEAD_DIM).reshape(1, LANES)
    kg = jnp.tile(k_norm[0], LANES // ATTN_HEAD_DIM).reshape(1, LANES)
    mnorm = mlstm_norm[0].reshape(nh, MLSTM_HEAD_DIM)
    sinks = attn_sinks[0]

    wa = _prompt_attn_weights(w)
    idx = np.arange(2 * LANES)
    same = (idx[:, None] // LANES == idx[None, :] // LANES) & (
        (idx[:, None] // QUARTER) % 2 == (idx[None, :] // QUARTER) % 2)
    bd = jnp.asarray(same, dtype=bf16)
    sink_rows_p = jnp.repeat(sinks.reshape(KV_HEADS, ATTN_GROUP), QBLOCK, axis=1)
    bias = jnp.where(jnp.arange(2 * QBLOCK)[None, None, :] == 0, sink_rows_p[:, :, None], NEG)
    qgq = _quarters(q_norm[0]).reshape(1, LANES)
    kgq = _quarters(k_norm[0]).reshape(1, LANES)
    pos_p = np.arange(tp, dtype=np.float32)
    cos_p, sin_p = _rope_tables_quarters(pos_p)
    cos_w, sin_w = _rope_tables(pos_p[tp - WINDOW:])
    x1_p, k_p, v_p, cext_p, m_p = _prompt_mixer(
        x_prompt[0], cos_p, sin_p, wa, w, wgate, wout_b, bd, bias, anorm, qgq, kgq, kg, cos_w, sin_w,
        gbias, mnorm)

    wq_s = w[:, COL_QA:COL_KA].reshape(D_MODEL, KV_HEADS, ATTN_GROUP, ATTN_HEAD_DIM)
    wq_s = wq_s.transpose(0, 2, 1, 3).reshape(D_MODEL, ATTN_WIDTH)
    wo_a = wout_b[:ATTN_WIDTH].reshape(KV_HEADS, ATTN_GROUP, ATTN_HEAD_DIM, D_MODEL)
    wo_a = wo_a.transpose(1, 0, 2, 3).reshape(ATTN_WIDTH, D_MODEL)
    wout_s = jnp.concatenate([wo_a, wout_b[ATTN_WIDTH:]], axis=0)
    sink_tile = jnp.broadcast_to(
        jnp.repeat(sinks.reshape(KV_HEADS, ATTN_GROUP).T.reshape(-1), tpad)[:, None],
        (ATTN_HEADS * tpad, LANES))
    lanes = np.arange(LANES)
    bd_s = jnp.asarray(lanes[:, None] // ATTN_HEAD_DIM == lanes[None, :] // ATTN_HEAD_DIM, dtype=bf16)
    cos_s, sin_s = _rope_tables(np.arange(tpad, dtype=np.float32) + np.float32(PAST_LEN))
    cos_s = np.tile(cos_s, (SAMPLE_BATCH_BLOCK, 1))
    sin_s = np.tile(sin_s, (SAMPLE_BATCH_BLOCK, 1))
    x_pad = jnp.pad(x_sample, ((0, 0), (0, tpad - nt), (0, 0)))
    ckt = cache_k[0].reshape(nb, WINDOW, KV_WIDTH).transpose(0, 2, 1)
    cvt = cache_v[0].reshape(nb, WINDOW, KV_WIDTH).transpose(0, 2, 1)
    x1_pad, nkt, nvt, c_new, n_new, m_pad = _sample_mixer(
        x_pad, ckt, cvt, state_C[0], state_n[0], state_m[0][:, :, None], cos_s, sin_s, wq_s, w, wgate,
        wout_s, sink_tile, bd_s, anorm, qg, kg, gbias, mnorm)
    y_p, y_s = _ffn(x1_p, x1_pad[:, :nt].reshape(nb * nt, D_MODEL), fnorm, wg_b, wu_b, wd_b)
    m_new = m_pad[:, :nh, 0]

    new_k_s = nkt.transpose(0, 2, 1)
    new_v_s = nvt.transpose(0, 2, 1)

    kv_shape = (1, 1, WINDOW, KV_HEADS, ATTN_HEAD_DIM)
    return (
        y_p[None],
        y_s.reshape(nb, nt, D_MODEL),
        k_p.reshape(kv_shape),
        v_p.reshape(kv_shape),
        cext_p[None, None, :, :, :MLSTM_HEAD_DIM],
        cext_p[None, None, :, :, MLSTM_HEAD_DIM],
        m_p[None, None, :nh, 0],
        new_k_s.reshape(1, nb, WINDOW, KV_HEADS, ATTN_HEAD_DIM),
        new_v_s.reshape(1, nb, WINDOW, KV_HEADS, ATTN_HEAD_DIM),
        c_new.reshape(1, nb, nh, MLSTM_HEAD_DIM, MLSTM_HEAD_DIM),
        n_new.reshape(1, nb, nh, MLSTM_HEAD_DIM),
        m_new.reshape(1, nb, nh),
    )
```

```python
import jax
import jax.numpy as jnp
import numpy as np
from jax import lax
from jax.experimental import pallas as pl
from jax.experimental.pallas import tpu as pltpu

D_MODEL = 1024
PAST_LEN = 16384
ATTN_HEADS = 8
KV_HEADS = 2
ATTN_HEAD_DIM = 64
ATTN_GROUP = ATTN_HEADS // KV_HEADS
ATTN_WIDTH = ATTN_HEADS * ATTN_HEAD_DIM
KV_WIDTH = KV_HEADS * ATTN_HEAD_DIM
WINDOW = 128
ROPE_THETA = 10000.0
MLSTM_HEADS = 4
MLSTM_HEAD_DIM = 128
MLSTM_WIDTH = MLSTM_HEADS * MLSTM_HEAD_DIM
MIX_WIDTH = ATTN_WIDTH + MLSTM_WIDTH
D_FF = 2816
NORM_EPS = 1e-6

LANES = 128
SUBLANES = 8
VMEM_LIMIT = 56 * 1024 * 1024

COL_QA = 0
COL_KA = COL_QA + ATTN_WIDTH
COL_VA = COL_KA + KV_WIDTH
COL_QM = COL_VA + KV_WIDTH
COL_KM = COL_QM + MLSTM_WIDTH
COL_VM = COL_KM + MLSTM_WIDTH
COL_OM = COL_VM + MLSTM_WIDTH
COL_G = COL_OM + MLSTM_WIDTH
IN_WIDTH = COL_G + 2 * MLSTM_HEADS
FG_LANE = SUBLANES

PROMPT_BLOCK = 512
QBLOCK = WINDOW
MCHUNK = 128
PROJ_CHUNK = 256
FFN_BLOCK = 512
FFN_CHUNK = 256
SAMPLE_BATCH_BLOCK = 16
SAMPLE_TOKENS = 4
NEG = -1e30

f32 = jnp.float32
bf16 = jnp.bfloat16


def _rms(x, gain):
    return x * lax.rsqrt(jnp.mean(x * x, axis=-1, keepdims=True) + NORM_EPS) * gain


def _segsum64(s, lane):
    for k in (1, 2, 4, 8, 16, 32):
        s = s + jnp.where((lane & k) != 0, pltpu.roll(s, k, 1), pltpu.roll(s, LANES - k, 1))
    return s


def _headnorm_rope(xs, gain, cos, sin_signed, lane):
    ss = _segsum64(xs * xs, lane)
    y = xs * lax.rsqrt(ss * (1.0 / ATTN_HEAD_DIM) + NORM_EPS) * gain
    partner = jnp.where((lane & 32) != 0, pltpu.roll(y, 32, 1), pltpu.roll(y, LANES - 32, 1))
    return y * cos + partner * sin_signed


def _group_sumsq(xs, bd_ref):
    x2 = xs * xs
    hi = x2.astype(bf16)
    lo = (x2 - hi.astype(f32)).astype(bf16)
    return (jnp.dot(hi, bd_ref[...], preferred_element_type=f32)
            + jnp.dot(lo, bd_ref[...], preferred_element_type=f32))


PA_Q = 0
PA_K = PA_Q + ATTN_WIDTH
PA_V = PA_K + KV_HEADS * LANES
PA_WIDTH = PA_V + KV_HEADS * LANES
QUARTER = ATTN_HEAD_DIM // 2
QA_Q = 0
QA_K = QA_Q + 2 * ATTN_WIDTH
QA_V = QA_K + KV_HEADS * LANES
QA_VZ = QA_V + KV_HEADS * LANES
QA_WIDTH = QA_VZ + KV_HEADS * LANES
ZM_WIDTH = 4 * MLSTM_WIDTH


def _norm_rope_quarters(xs, ss, gain, cos, sin_signed):
    y = xs * lax.rsqrt(ss * (1.0 / ATTN_HEAD_DIM) + NORM_EPS) * gain
    return y * cos + pltpu.roll(y, LANES // 2, 1) * sin_signed


def _col_chunks(width):
    return [(c, min(c + PROJ_CHUNK, width)) for c in range(0, width, PROJ_CHUNK)]


def _mixer_proj_jobs(x_ref, anorm_ref, wa_ref, w_ref, wgate_ref, za_ref, zm_ref, gz_ref):
    h = _rms(x_ref[...], anorm_ref[...]).astype(bf16)

    def proj_job(w_src, wc0, z_ref, c0, c1):
        def run():
            z_ref[:, c0:c1] = jnp.dot(h, w_src[:, wc0 + c0:wc0 + c1], preferred_element_type=f32)
        return run

    return ([proj_job(wa_ref, 0, za_ref, c0, c1) for c0, c1 in _col_chunks(PA_WIDTH)]
            + [proj_job(w_ref, COL_QM, zm_ref, c0, c1) for c0, c1 in _col_chunks(ZM_WIDTH)]
            + [proj_job(wgate_ref, 0, gz_ref, 0, LANES)])


def _mixer_prep_jobs(za_ref, gz_ref, cos_ref, sin_ref, bd_ref, qg_ref, kg_ref, gbias_ref, qa_ref,
                     colf_ref, urow_ref, wrow_ref, mst_ref, mout_ref):
    tb = za_ref.shape[0]

    def gates_job():
        lane_t = lax.broadcasted_iota(jnp.int32, (tb, LANES), 1)
        gcol = gz_ref[...] + gbias_ref[...]
        acol = jnp.where(lane_t < FG_LANE, gcol, jax.nn.log_sigmoid(gcol))
        arow = acol.T
        lane8 = lax.broadcasted_iota(jnp.int32, (SUBLANES, LANES), 1)
        lane_in = lane8 & (MCHUNK - 1)
        m_prev = mst_ref[:, 0:1]
        stacks = []
        for sb in range(tb // LANES):
            ls = slice(sb * LANES, (sb + 1) * LANES)
            ig8 = arow[0:SUBLANES, ls]
            lf8 = arow[FG_LANE:FG_LANE + SUBLANES, ls]
            b8 = lf8
            k = 1
            while k < MCHUNK:
                b8 = b8 + jnp.where(lane_in >= k, pltpu.roll(b8, k, 1), 0.0)
                k *= 2
            u8 = ig8 - b8
            cm8 = u8
            k = 1
            while k < MCHUNK:
                cm8 = jnp.maximum(cm8, jnp.where(lane_in >= k, pltpu.roll(cm8, k, 1), NEG))
                k *= 2
            g8 = jnp.zeros_like(u8)
            mp8 = jnp.zeros_like(u8)
            gl8 = jnp.zeros_like(u8)
            for c in range(LANES // MCHUNK):
                in_chunk = (lane8 // MCHUNK) == c
                gc = jnp.maximum(cm8, m_prev)
                last = c * MCHUNK + MCHUNK - 1
                g_last = jnp.max(jnp.where(lane8 == last, gc, NEG), axis=1, keepdims=True)
                b_last = jnp.max(jnp.where(lane8 == last, b8, NEG), axis=1, keepdims=True)
                g8 = jnp.where(in_chunk, gc, g8)
                mp8 = jnp.where(in_chunk, m_prev, mp8)
                gl8 = jnp.where(in_chunk, g_last, gl8)
                m_prev = b_last + g_last
            a8 = jnp.exp(mp8 - g8)
            emt8 = jnp.exp(-(b8 + g8))
            aend8 = jnp.exp(mp8 - gl8)
            stacks.append(jnp.concatenate(
                [g8, a8, emt8, aend8, jnp.zeros((LANES - 4 * SUBLANES, LANES), f32)], axis=0))
            urow_ref[:, ls] = u8
            wrow_ref[:, ls] = jnp.exp(u8 - gl8)
        mst_ref[...] = jnp.broadcast_to(m_prev, mst_ref.shape)
        mout_ref[...] = jnp.broadcast_to(m_prev, mout_ref.shape)
        colf_ref[...] = jnp.concatenate(stacks, axis=1).T

    def prep_job(qb):
        def run():
            rows = slice(qb * QBLOCK, (qb + 1) * QBLOCK)
            lane = lax.broadcasted_iota(jnp.int32, (QBLOCK, LANES), 1)
            head_a = ((lane // QUARTER) & 1) == 0
            row0 = lax.broadcasted_iota(jnp.int32, (QBLOCK, LANES), 0) == 0
            cos = cos_ref[rows, :]
            sin = sin_ref[rows, :]
            ss = [_group_sumsq(za_ref[rows, d * 2 * LANES:(d + 1) * 2 * LANES], bd_ref)
                  for d in range(PA_V // (2 * LANES))]
            for j in range(PA_V // LANES):
                is_q = j < ATTN_WIDTH // LANES
                y = _norm_rope_quarters(za_ref[rows, j * LANES:(j + 1) * LANES],
                                        ss[j // 2][:, (j % 2) * LANES:(j % 2 + 1) * LANES],
                                        qg_ref[...] if is_q else kg_ref[...], cos, sin)
                if is_q:
                    y = y * (ATTN_HEAD_DIM ** -0.5)
                    qa_ref[rows, QA_Q + 2 * j * LANES:QA_Q + (2 * j + 1) * LANES] = (
                        jnp.where(head_a, y, 0.0).astype(bf16))
                    qa_ref[rows, QA_Q + (2 * j + 1) * LANES:QA_Q + (2 * j + 2) * LANES] = (
                        jnp.where(head_a, 0.0, y).astype(bf16))
                else:
                    c = j - ATTN_WIDTH // LANES
                    qa_ref[rows, QA_K + c * LANES:QA_K + (c + 1) * LANES] = y.astype(bf16)
            for c in range(KV_HEADS):
                v = za_ref[rows, PA_V + c * LANES:PA_V + (c + 1) * LANES]
                qa_ref[rows, QA_V + c * LANES:QA_V + (c + 1) * LANES] = v.astype(bf16)
                qa_ref[rows, QA_VZ + c * LANES:QA_VZ + (c + 1) * LANES] = (
                    jnp.where(row0, 0.0, v).astype(bf16))
        return run

    return [gates_job] + [prep_job(qb) for qb in range(tb // QBLOCK)]


def _mixer_outproj_jobs(xs_ref, mix_ref, wout_ref, x1_ref):
    def job(c0, c1):
        def run():
            x1_ref[:, c0:c1] = xs_ref[:, c0:c1] + jnp.dot(mix_ref[...], wout_ref[:, c0:c1],
                                                          preferred_element_type=f32)
        return run

    return [job(c0, c1) for c0, c1 in _col_chunks(D_MODEL)]


def _mixer_window_out(xs_ref, anorm_ref, w_ref, kgs_ref, coss_ref, sins_ref, kout_ref, vout_ref):
    tb = xs_ref.shape[0]
    h = _rms(xs_ref[tb - WINDOW:, :], anorm_ref[...]).astype(bf16)
    zs = jnp.dot(h, w_ref[:, COL_KA:COL_QM], preferred_element_type=f32)
    lane_s = lax.broadcasted_iota(jnp.int32, (WINDOW, LANES), 1)
    kout_ref[...] = _headnorm_rope(zs[:, :KV_WIDTH], kgs_ref[...], coss_ref[...], sins_ref[...], lane_s)
    vout_ref[...] = zs[:, KV_WIDTH:]


def _mixer_core(first_block, last_block, head_fillers, fillers, prep_jobs, qa_ref, zm_ref, colf_ref,
                urow_ref, wrow_ref, mix_ref, bias_ref, mnorm_ref, kprev_ref, vprev_ref, cst_ref, cext_ref):
    tb = qa_ref.shape[0]
    nqb = tb // QBLOCK
    nch = tb // MCHUNK

    def spread(jobs, slots):
        jobs = list(jobs)
        total = len(jobs)
        progress = [0]

        def fill():
            progress[0] += 1
            while total - len(jobs) < min(total, -(-total * progress[0] // slots)):
                jobs.pop(0)()
        return fill

    fill_head = spread(head_fillers, len(prep_jobs))
    for job in prep_jobs:
        job()
        fill_head()
    fill = spread(fillers, nqb * KV_HEADS + nch * MLSTM_HEADS)

    low_half = lax.broadcasted_iota(jnp.int32, (QBLOCK, LANES), 1) < ATTN_HEAD_DIM
    qi = lax.broadcasted_iota(jnp.int32, (ATTN_GROUP * QBLOCK, 2 * QBLOCK), 0) & (QBLOCK - 1)
    kj = lax.broadcasted_iota(jnp.int32, (ATTN_GROUP * QBLOCK, 2 * QBLOCK), 1)
    band = (kj > qi) & (kj <= qi + QBLOCK)
    ones_slab = jnp.ones((2 * QBLOCK, LANES), bf16)

    def attn_unit(qb, c):
        rows = slice(qb * QBLOCK, (qb + 1) * QBLOCK)
        kcols = slice(QA_K + c * LANES, QA_K + (c + 1) * LANES)
        if qb == 0:
            kprev, vprev = kprev_ref[c], vprev_ref[c]
        else:
            prev = slice((qb - 1) * QBLOCK, qb * QBLOCK)
            kprev = qa_ref[prev, kcols]
            vprev = qa_ref[prev, QA_VZ + c * LANES:QA_VZ + (c + 1) * LANES]
        kcat = jnp.concatenate([kprev, qa_ref[rows, kcols]], axis=0)
        vcat = jnp.concatenate([vprev, qa_ref[rows, QA_V + c * LANES:QA_V + (c + 1) * LANES]], axis=0)
        vext = jnp.concatenate([vcat, ones_slab], axis=1)
        q0 = QA_Q + c * ATTN_GROUP * LANES
        qst = jnp.concatenate([qa_ref[rows, q0 + g * LANES:q0 + (g + 1) * LANES]
                               for g in range(ATTN_GROUP)], axis=0)
        s = lax.dot_general(qst, kcat, (((1,), (1,)), ((), ())), preferred_element_type=f32)
        valid = band & (kj >= QBLOCK) if (first_block and qb == 0) else band
        s = jnp.where(valid, s, bias_ref[c])
        p = jnp.exp(s - jnp.max(s, axis=-1, keepdims=True)).astype(bf16)
        of = jnp.dot(p, vext, preferred_element_type=f32)
        o = of[:, :LANES] / of[:, LANES:]
        for jj in range(2):
            pair = jnp.where(low_half, o[(2 * jj) * QBLOCK:(2 * jj + 1) * QBLOCK],
                             o[(2 * jj + 1) * QBLOCK:(2 * jj + 2) * QBLOCK])
            col = (2 * c + jj) * LANES
            mix_ref[rows, col:col + LANES] = pair.astype(bf16)

    ti = lax.broadcasted_iota(jnp.int32, (MCHUNK, MCHUNK), 0)
    si = lax.broadcasted_iota(jnp.int32, (MCHUNK, MCHUNK), 1)
    causal = si <= ti
    ones_l = jnp.ones((MCHUNK, LANES), bf16)

    def mlstm_unit(hd, c, cext):
        hcols = lambda k: slice((k * MLSTM_HEADS + hd) * MLSTM_HEAD_DIM,
                                (k * MLSTM_HEADS + hd + 1) * MLSTM_HEAD_DIM)
        rows = slice(c * MCHUNK, (c + 1) * MCHUNK)
        qb_ = zm_ref[rows, hcols(0)].astype(bf16)
        kf = zm_ref[rows, hcols(1)] * (MLSTM_HEAD_DIM ** -0.5)
        vb = zm_ref[rows, hcols(2)].astype(bf16)
        og = zm_ref[rows, hcols(3)]
        g_c = colf_ref[rows, hd:hd + 1]
        a_c = colf_ref[rows, SUBLANES + hd:SUBLANES + hd + 1]
        emt_c = colf_ref[rows, 2 * SUBLANES + hd:2 * SUBLANES + hd + 1]
        aend = colf_ref[c * MCHUNK:c * MCHUNK + 1, 3 * SUBLANES + hd:3 * SUBLANES + hd + 1]
        u_r = urow_ref[hd:hd + 1, rows]
        w_r = wrow_ref[hd:hd + 1, rows]
        dmat = jnp.exp(jnp.where(causal, u_r - g_c, NEG))
        smat = lax.dot_general(qb_, kf.astype(bf16), (((1,), (1,)), ((), ())),
                               preferred_element_type=f32) * dmat
        vext = jnp.concatenate([vb, ones_l], axis=1)
        nd = (a_c * jnp.dot(qb_, cext.astype(bf16), preferred_element_type=f32)
              + jnp.dot(smat.astype(bf16), vext, preferred_element_type=f32))
        hraw = nd[:, :MLSTM_HEAD_DIM] / jnp.maximum(jnp.abs(nd[:, MLSTM_HEAD_DIM:]), emt_c)
        hn = _rms(hraw, mnorm_ref[hd:hd + 1, :])
        mix_ref[rows, ATTN_WIDTH + hd * MLSTM_HEAD_DIM:ATTN_WIDTH + (hd + 1) * MLSTM_HEAD_DIM] = (
            (hn * jax.nn.sigmoid(og)).astype(bf16))
        kw_t = (kf.T * w_r).astype(bf16)
        return aend * cext + jnp.dot(kw_t, vext, preferred_element_type=f32)

    cexts =[cst_ref[hd] for hd in range(MLSTM_HEADS)]
    attn_jobs = [(qb, c) for qb in range(nqb) for c in range(KV_HEADS)]
    mlstm_jobs = [(hd, c) for c in range(nch) for hd in range(MLSTM_HEADS)]
    while attn_jobs or mlstm_jobs:
        if attn_jobs:
            attn_unit(*attn_jobs.pop(0))
            fill()
        for _ in range(2 if len(mlstm_jobs) > 2 * len(attn_jobs) else 1):
            if mlstm_jobs:
                hd, c = mlstm_jobs.pop(0)
                cexts[hd] = mlstm_unit(hd, c, cexts[hd])
                fill()
    last_rows = slice(tb - QBLOCK, tb)
    for c in range(KV_HEADS):
        kprev_ref[c] = qa_ref[last_rows, QA_K + c * LANES:QA_K + (c + 1) * LANES]
        vprev_ref[c] = qa_ref[last_rows, QA_VZ + c * LANES:QA_VZ + (c + 1) * LANES]
    for hd in range(MLSTM_HEADS):
        cst_ref[hd] = cexts[hd]
        if last_block:
            cext_ref[hd] = cexts[hd]


def _prompt_mixer_kernel(x_ref, xres_ref, cos_ref, sin_ref, wa_ref, w_ref, wgate_ref, wout_ref, bd_ref,
                         bias_ref, anorm_ref, qg_ref, kg_ref, kgs_ref, coss_ref, sins_ref, gbias_ref,
                         mnorm_ref, x1_ref, kout_ref, vout_ref, cext_ref, mout_ref,
                         za_ref, zm_ref, gz_ref, zmw_ref, mix_ref,
                         qa_ref, colf_ref, urow_ref, wrow_ref, kprev_ref, vprev_ref, cst_ref, mst_ref):
    step = pl.program_id(0)
    nblk = pl.num_programs(0) - 2

    def run(do_in, do_core, do_out, first_block=False, last_block=False):
        out_jobs = _mixer_outproj_jobs(xres_ref, mix_ref, wout_ref, x1_ref) if do_out else []
        proj_jobs = (_mixer_proj_jobs(x_ref, anorm_ref, wa_ref, w_ref, wgate_ref, za_ref, zm_ref, gz_ref)
                     if do_in else [])
        if do_core:
            prep = _mixer_prep_jobs(za_ref, gz_ref, cos_ref, sin_ref, bd_ref, qg_ref, kg_ref, gbias_ref,
                                    qa_ref, colf_ref, urow_ref, wrow_ref, mst_ref, mout_ref)

            def keep_zm():
                zmw_ref[...] = zm_ref[...]

            _mixer_core(first_block, last_block, out_jobs, proj_jobs, prep + [keep_zm], qa_ref, zmw_ref,
                        colf_ref, urow_ref, wrow_ref, mix_ref, bias_ref, mnorm_ref, kprev_ref, vprev_ref,
                        cst_ref, cext_ref)
        else:
            for job in out_jobs + proj_jobs:
                job()
        if last_block:
            _mixer_window_out(x_ref, anorm_ref, w_ref, kgs_ref, coss_ref, sins_ref, kout_ref, vout_ref)

    @pl.when(step == 0)
    def _first():
        kprev_ref[...] = jnp.zeros_like(kprev_ref)
        vprev_ref[...] = jnp.zeros_like(vprev_ref)
        cst_ref[...] = jnp.zeros_like(cst_ref)
        mst_ref[...] = jnp.zeros_like(mst_ref)
        run(True, False, False)

    @pl.when(step == 1)
    def _second():
        run(True, True, False, first_block=True)

    @pl.when((step >= 2) & (step < nblk))
    def _steady():
        run(True, True, True)

    @pl.when(step == nblk)
    def _drain_core():
        run(False, True, True, last_block=True)

    @pl.when(step == nblk + 1)
    def _drain_out():
        run(False, False, True)


def _const_spec(shape, single=False):
    nd = len(shape)
    if single:
        return pl.BlockSpec(shape, lambda i, *_: (0,) * nd, pipeline_mode=pl.Buffered(1))
    return pl.BlockSpec(shape, lambda i, *_: (0,) * nd)


def _prompt_mixer(x, cos, sin, wa, w, wgate, wout_b, bd, bias, anorm, qg, kg, kgs, coss, sins, gbias, mnorm):
    t = x.shape[0]
    tb = PROMPT_BLOCK
    nblk = t // tb
    assert nblk >= 3
    state_shape = (MLSTM_HEADS, MLSTM_HEAD_DIM, 2 * MLSTM_HEAD_DIM)
    last = nblk - 1
    lag = lambda d: (lambda i: (jnp.clip(i - d, 0, last), 0))
    return pl.pallas_call(
        _prompt_mixer_kernel,
        grid=(nblk + 2,),
        in_specs=[
            pl.BlockSpec((tb, D_MODEL), lag(0)),
            pl.BlockSpec((tb, D_MODEL), lag(2)),
            pl.BlockSpec((tb, LANES), lag(1)),
            pl.BlockSpec((tb, LANES), lag(1)),
            _const_spec((D_MODEL, PA_WIDTH), single=True),
            _const_spec((D_MODEL, IN_WIDTH), single=True),
            _const_spec((D_MODEL, LANES), single=True),
            _const_spec((MIX_WIDTH, D_MODEL), single=True),
            _const_spec((2 * LANES, 2 * LANES), single=True),
            _const_spec((KV_HEADS, ATTN_GROUP * QBLOCK, 2 * QBLOCK), single=True),
            _const_spec((1, D_MODEL)),
            _const_spec((1, LANES)),
            _const_spec((1, LANES)),
            _const_spec((1, LANES)),
            _const_spec((WINDOW, LANES)),
            _const_spec((WINDOW, LANES)),
            _const_spec((1, LANES)),
            _const_spec((MLSTM_HEADS, MLSTM_HEAD_DIM)),
        ],
        out_specs=[
            pl.BlockSpec((tb, D_MODEL), lag(2)),
            _const_spec((WINDOW, KV_WIDTH)),
            _const_spec((WINDOW, KV_WIDTH)),
            _const_spec(state_shape),
            _const_spec((SUBLANES, LANES)),
        ],
        out_shape=[
            jax.ShapeDtypeStruct((t, D_MODEL), f32),
            jax.ShapeDtypeStruct((WINDOW, KV_WIDTH), f32),
            jax.ShapeDtypeStruct((WINDOW, KV_WIDTH), f32),
            jax.ShapeDtypeStruct(state_shape, f32),
            jax.ShapeDtypeStruct((SUBLANES, LANES), f32),
        ],
        scratch_shapes=[
            pltpu.VMEM((tb, PA_WIDTH), f32), pltpu.VMEM((tb, ZM_WIDTH), f32), pltpu.VMEM((tb, LANES), f32),
            pltpu.VMEM((tb, ZM_WIDTH), f32), pltpu.VMEM((tb, MIX_WIDTH), bf16),
            pltpu.VMEM((tb, QA_WIDTH), bf16), pltpu.VMEM((tb, LANES), f32),
            pltpu.VMEM((SUBLANES, tb), f32), pltpu.VMEM((SUBLANES, tb), f32),
            pltpu.VMEM((KV_HEADS, WINDOW, LANES), bf16), pltpu.VMEM((KV_HEADS, WINDOW, LANES), bf16),
            pltpu.VMEM(state_shape, f32), pltpu.VMEM((SUBLANES, LANES), f32)],
        compiler_params=pltpu.CompilerParams(
            dimension_semantics=("arbitrary",), vmem_limit_bytes=VMEM_LIMIT),
        name="prompt_mixer",
    )(x, x, cos, sin, wa, w, wgate, wout_b, bd, bias, anorm, qg, kg, kgs, coss, sins, gbias, mnorm)


def _ffn_kernel(xp_ref, xs_ref, g_ref, wg_ref, wu_ref, wd_ref, op_ref, os_ref):
    step = pl.program_id(0)
    last = pl.num_programs(0) - 1

    @pl.when(step < last)
    def _prompt_rows():
        _ffn_rows(xp_ref, g_ref, wg_ref, wu_ref, wd_ref, op_ref)

    @pl.when(step == last)
    def _sample_rows():
        _ffn_rows(xs_ref, g_ref, wg_ref, wu_ref, wd_ref, os_ref)


def _ffn_rows(x_ref, g_ref, wg_ref, wu_ref, wd_ref, o_ref):
    x = x_ref[...]
    hf = _rms(x, g_ref[...]).astype(bf16)
    acc = x
    for c in range(D_FF // FFN_CHUNK):
        cs = slice(c * FFN_CHUNK, (c + 1) * FFN_CHUNK)
        gate = jnp.dot(hf, wg_ref[:, cs], preferred_element_type=f32)
        up = jnp.dot(hf, wu_ref[:, cs], preferred_element_type=f32)
        act = (gate * jax.nn.sigmoid(gate) * up).astype(bf16)
        acc = acc + jnp.dot(act, wd_ref[cs, :], preferred_element_type=f32)
    o_ref[...] = acc


def _ffn(x_p, x_s, fnorm, wg_b, wu_b, wd_b):
    n = x_p.shape[0]
    ns = x_s.shape[0]
    tm = FFN_BLOCK
    last = n // tm - 1
    return pl.pallas_call(
        _ffn_kernel,
        grid=(n // tm + 1,),
        in_specs=[
            pl.BlockSpec((tm, D_MODEL), lambda i: (jnp.minimum(i, last), 0)),
            _const_spec((ns, D_MODEL), single=True),
            _const_spec((1, D_MODEL)),
            _const_spec((D_MODEL, D_FF), single=True),
            _const_spec((D_MODEL, D_FF), single=True),
            _const_spec((D_FF, D_MODEL), single=True),
        ],
        out_specs=[pl.BlockSpec((tm, D_MODEL), lambda i: (jnp.minimum(i, last), 0)),
                   _const_spec((ns, D_MODEL))],
        out_shape=[jax.ShapeDtypeStruct((n, D_MODEL), f32), jax.ShapeDtypeStruct((ns, D_MODEL), f32)],
        compiler_params=pltpu.CompilerParams(
            dimension_semantics=("arbitrary",), vmem_limit_bytes=VMEM_LIMIT),
        name="ffn",
    )(x_p, x_s, fnorm, wg_b, wu_b, wd_b)


def _sample_mixer_kernel(x_ref, ckt_ref, cvt_ref, c_ref, n_ref, m_ref, cos_ref, sin_ref, wqs_ref, w_ref,
                         wgate_ref, wout_ref, sink_ref, bd_ref, anorm_ref, qg_ref, kg_ref, gbias_ref,
                         mnorm_ref, x1_ref, nkt_ref, nvt_ref, cn_ref, nn_ref, mn_ref):
    bb, tpad, _ = x_ref.shape
    nrows = bb * tpad
    nreal = SAMPLE_TOKENS
    h = _rms(x_ref[...].reshape(nrows, D_MODEL), anorm_ref[...]).astype(bf16)
    z = jnp.concatenate(
        [jnp.dot(h, wqs_ref[...], preferred_element_type=f32),
         jnp.dot(h, w_ref[:, COL_KA:COL_G], preferred_element_type=f32),
         jnp.dot(h, wgate_ref[...], preferred_element_type=f32)], axis=1)
    lane = lax.broadcasted_iota(jnp.int32, (nrows, LANES), 1)
    low = lane < ATTN_HEAD_DIM
    cos = cos_ref[...]
    sin = sin_ref[...]

    def per_seq(a):
        return a.reshape(bb, tpad, a.shape[-1])

    def norm_rope(xs, gain):
        y = xs * lax.rsqrt(_group_sumsq(xs, bd_ref) * (1.0 / ATTN_HEAD_DIM) + NORM_EPS) * gain
        partner = jnp.where((lane & QUARTER) != 0, pltpu.roll(y, QUARTER, 1),
                            pltpu.roll(y, LANES - QUARTER, 1))
        return y * cos + partner * sin

    q_rows = []
    for j in range(ATTN_GROUP):
        qs = norm_rope(z[:, COL_QA + j * LANES:COL_QA + (j + 1) * LANES], qg_ref[...])
        qs = qs * (ATTN_HEAD_DIM ** -0.5)
        q_rows.append(per_seq(jnp.where(low, qs, 0.0)).astype(bf16))
        q_rows.append(per_seq(jnp.where(low, 0.0, qs)).astype(bf16))
    qbd = jnp.concatenate(q_rows, axis=1)
    knew = norm_rope(z[:, COL_KA:COL_KA + KV_WIDTH], kg_ref[...])
    vnew = z[:, COL_VA:COL_VA + KV_WIDTH]
    zpad = jnp.zeros((bb, LANES - tpad, LANES), bf16)
    knp = jnp.concatenate([per_seq(knew).astype(bf16), zpad], axis=1)
    vnp = jnp.concatenate([per_seq(vnew).astype(bf16), zpad], axis=1)
    ckt = ckt_ref[...]
    cvt = cvt_ref[...]
    s = jnp.concatenate(
        [jnp.einsum('bqd,bdw->bqw', qbd, ckt.astype(bf16), preferred_element_type=f32),
         jnp.einsum('bqd,bkd->bqk', qbd, knp, preferred_element_type=f32)], axis=2)
    tq = lax.broadcasted_iota(jnp.int32, s.shape, 1) & (tpad - 1)
    kj = lax.broadcasted_iota(jnp.int32, s.shape, 2)
    valid = ((kj < WINDOW) & (kj > tq)) | ((kj >= WINDOW) & (kj - WINDOW <= tq) & (kj - WINDOW < nreal))
    s = jnp.where(valid, s, NEG)
    sink = sink_ref[:, 0:1][None]
    mx = jnp.maximum(jnp.max(s, axis=-1, keepdims=True), sink)
    p = jnp.exp(s - mx)
    den = jnp.sum(p, axis=-1, keepdims=True) + jnp.exp(sink - mx)
    pb = p.astype(bf16)
    o = (jnp.einsum('bqw,bdw->bqd', pb[:, :, :WINDOW], cvt.astype(bf16), preferred_element_type=f32)
         + jnp.einsum('bqk,bkd->bqd', pb[:, :, WINDOW:], vnp, preferred_element_type=f32)) / den
    low3 = lax.broadcasted_iota(jnp.int32, (bb, tpad, LANES), 2) < ATTN_HEAD_DIM
    mix_parts = []
    for j in range(ATTN_GROUP):
        r0 = 2 * j * tpad
        pair = jnp.where(low3, o[:, r0:r0 + tpad, :], o[:, r0 + tpad:r0 + 2 * tpad, :])
        mix_parts.append(pair.reshape(nrows, LANES).astype(bf16))

    keep = lax.broadcasted_iota(jnp.int32, (KV_WIDTH, WINDOW), 1) < WINDOW - nreal
    knt = knew.T
    vnt = vnew.T
    for b in range(bb):
        shift = (WINDOW - nreal - b * tpad) % LANES
        nkt_ref[b] = jnp.where(keep, pltpu.roll(ckt_ref[b], WINDOW - nreal, 1), pltpu.roll(knt, shift, 1))
        nvt_ref[b] = jnp.where(keep, pltpu.roll(cvt_ref[b], WINDOW - nreal, 1), pltpu.roll(vnt, shift, 1))

    gz = per_seq(z[:, COL_G:COL_G + LANES] + gbias_ref[...])
    lgz = jax.nn.log_sigmoid(gz)
    trow = lax.broadcasted_iota(jnp.int32, (bb, tpad, 1), 1)
    real = trow < nreal
    mn_ref[...] = jnp.zeros_like(mn_ref)
    for hd in range(MLSTM_HEADS):
        hcols = lambda base: slice(base + hd * MLSTM_HEAD_DIM, base + (hd + 1) * MLSTM_HEAD_DIM)
        q = per_seq(z[:, hcols(COL_QM)])
        k = per_seq(z[:, hcols(COL_KM)]) * (MLSTM_HEAD_DIM ** -0.5)
        v = per_seq(z[:, hcols(COL_VM)])
        og = per_seq(z[:, hcols(COL_OM)])
        c0 = c_ref[:, hd]
        n0 = n_ref[:, hd:hd + 1, :]
        m0 = m_ref[:, hd:hd + 1, :]
        ig_c = jnp.where(real, gz[:, :, hd:hd + 1], NEG)
        lf_c = jnp.where(real, lgz[:, :, FG_LANE + hd:FG_LANE + hd + 1], 0.0)
        b_c = jnp.zeros_like(lf_c)
        for sx in range(nreal):
            b_c = b_c + jnp.where(trow >= sx, lf_c[:, sx:sx + 1, :], 0.0)
        dlog = [jnp.where(trow >= sx, b_c - b_c[:, sx:sx + 1, :] + ig_c[:, sx:sx + 1, :], NEG)
                for sx in range(nreal)]
        inter = b_c + m0
        m_t = inter
        for sx in range(nreal):
            m_t = jnp.maximum(m_t, dlog[sx])
        a = jnp.exp(inter - m_t)
        qc = jnp.einsum('btd,bde->bte', q.astype(bf16), c0.astype(bf16), preferred_element_type=f32)
        num = a * qc
        den_m = a * jnp.sum(q * n0, axis=2, keepdims=True)
        for sx in range(nreal):
            sd = jnp.sum(q * k[:, sx:sx + 1, :], axis=2, keepdims=True) * jnp.exp(dlog[sx] - m_t)
            num = num + sd * v[:, sx:sx + 1, :]
            den_m = den_m + sd
        hraw = num / jnp.maximum(jnp.abs(den_m), jnp.exp(-m_t))
        hn = _rms(hraw, mnorm_ref[hd:hd + 1, :][None])
        mix_parts.append((hn * jax.nn.sigmoid(og)).reshape(nrows, MLSTM_HEAD_DIM).astype(bf16))
        last = nreal - 1
        m_new = m_t[:, last:last + 1, :]
        b_last = b_c[:, last:last + 1, :]
        a_end = jnp.exp(b_last + m0 - m_new)
        kw = k * jnp.exp(b_last - b_c + ig_c - m_new)
        cn_ref[:, hd] = a_end * c0 + jnp.einsum('bsd,bse->bde', kw.astype(bf16), v.astype(bf16),
                                               preferred_element_type=f32)
        nn_ref[:, hd:hd + 1, :] = a_end * n0 + jnp.sum(kw, axis=1, keepdims=True)
        mn_ref[:, hd:hd + 1, :] = jnp.broadcast_to(m_new, (bb, 1, LANES))

    mix = jnp.concatenate(mix_parts, axis=1)
    x1 = x_ref[...].reshape(nrows, D_MODEL) + jnp.dot(mix, wout_ref[...], preferred_element_type=f32)
    x1_ref[...] = x1.reshape(bb, tpad, D_MODEL)


def _sample_mixer(x_pad, ckt, cvt, c0, n0, m0, cos, sin, wq_s, w, wgate, wout_s, sink_tile, bd, anorm, qg,
                  kg, gbias, mnorm):
    nb, tpad, _ = x_pad.shape
    bb = SAMPLE_BATCH_BLOCK
    nh = MLSTM_HEADS
    blk = lambda shape: pl.BlockSpec(shape, lambda i: (i,) + (0,) * (len(shape) - 1))
    cblk = (bb, nh, MLSTM_HEAD_DIM, MLSTM_HEAD_DIM)
    return pl.pallas_call(
        _sample_mixer_kernel,
        grid=(nb // bb,),
        in_specs=[blk((bb, tpad, D_MODEL)), blk((bb, KV_WIDTH, WINDOW)), blk((bb, KV_WIDTH, WINDOW)),
                  blk(cblk), blk((bb, nh, MLSTM_HEAD_DIM)), blk((bb, nh, 1)),
                  _const_spec((bb * tpad, LANES)), _const_spec((bb * tpad, LANES)),
                  _const_spec((D_MODEL, ATTN_WIDTH), single=True),
                  _const_spec((D_MODEL, IN_WIDTH), single=True),
                  _const_spec((D_MODEL, LANES), single=True),
                  _const_spec((MIX_WIDTH, D_MODEL), single=True),
                  _const_spec((ATTN_HEADS * tpad, LANES)), _const_spec((LANES, LANES)),
                  _const_spec((1, D_MODEL)),
                  _const_spec((1, LANES)), _const_spec((1, LANES)), _const_spec((1, LANES)),
                  _const_spec((nh, MLSTM_HEAD_DIM))],
        out_specs=[blk((bb, tpad, D_MODEL)), blk((bb, KV_WIDTH, WINDOW)), blk((bb, KV_WIDTH, WINDOW)),
                   blk(cblk), blk((bb, nh, MLSTM_HEAD_DIM)), blk((bb, tpad, LANES))],
        out_shape=[jax.ShapeDtypeStruct((nb, tpad, D_MODEL), f32),
                   jax.ShapeDtypeStruct((nb, KV_WIDTH, WINDOW), f32),
                   jax.ShapeDtypeStruct((nb, KV_WIDTH, WINDOW), f32),
                   jax.ShapeDtypeStruct((nb,) + cblk[1:], f32),
                   jax.ShapeDtypeStruct((nb, nh, MLSTM_HEAD_DIM), f32),
                   jax.ShapeDtypeStruct((nb, tpad, LANES), f32)],
        compiler_params=pltpu.CompilerParams(
            dimension_semantics=("arbitrary",), vmem_limit_bytes=VMEM_LIMIT),
        name="sample_mixer",
    )(x_pad, ckt, cvt, c0, n0, m0, cos, sin, wq_s, w, wgate, wout_s, sink_tile, bd, anorm, qg, kg, gbias,
      mnorm)


def _rope_angles(pos):
    half = ATTN_HEAD_DIM // 2
    inv = ROPE_THETA ** (-np.arange(half, dtype=np.float64) / half)
    ang = pos.astype(np.float64)[:, None] * inv[None, :]
    return np.cos(ang).astype(np.float32), np.sin(ang).astype(np.float32)


def _rope_tables(pos):
    c, s = _rope_angles(pos)
    cos = np.tile(c, (1, LANES // QUARTER))
    sin = np.tile(np.concatenate([-s, s], axis=1), (1, LANES // ATTN_HEAD_DIM))
    return cos, sin


def _rope_tables_quarters(pos):
    c, s = _rope_angles(pos)
    return np.tile(c, (1, LANES // QUARTER)), np.concatenate([-s, -s, s, s], axis=1)


def _quarters(a):
    lo, hi = a[..., :QUARTER], a[..., QUARTER:]
    return jnp.concatenate([lo, lo, hi, hi], axis=-1)


def _prompt_attn_weights(w):
    d = w.shape[0]
    wq = w[:, COL_QA:COL_KA].reshape(d, ATTN_WIDTH // LANES, 2, 2, QUARTER)
    wq = wq.transpose(0, 1, 3, 2, 4).reshape(d, ATTN_WIDTH)
    wk = _quarters(w[:, COL_KA:COL_VA].reshape(d, KV_HEADS, ATTN_HEAD_DIM)).reshape(d, KV_HEADS * LANES)
    wv = w[:, COL_VA:COL_QM].reshape(d, KV_HEADS, 1, ATTN_HEAD_DIM)
    wv = jnp.broadcast_to(wv, (d, KV_HEADS, 2, ATTN_HEAD_DIM)).reshape(d, KV_HEADS * LANES)
    return jnp.concatenate([wq, wk, wv], axis=1)


def kernel(x_prompt, x_sample, cache_k, cache_v, state_C, state_n, state_m, attn_norm, w_in, q_norm,
           k_norm, attn_sinks, b_ig, b_fg, mlstm_norm, w_out, ffn_norm, w_gate, w_up, w_down):
    assert w_in.shape[0] == 1 and x_prompt.shape[0] == 1
    tp = x_prompt.shape[1]
    nb, nt = x_sample.shape[0], x_sample.shape[1]
    assert nt == SAMPLE_TOKENS
    tpad = SUBLANES
    nh = MLSTM_HEADS

    w = w_in[0].astype(bf16)
    pad_a = jnp.zeros((D_MODEL, FG_LANE - nh), bf16)
    pad_b = jnp.zeros((D_MODEL, LANES - FG_LANE - nh), bf16)
    wgate = jnp.concatenate([w[:, COL_G:COL_G + nh], pad_a, w[:, COL_G + nh:], pad_b], axis=1)
    gbias = jnp.concatenate(
        [b_ig[0], jnp.zeros((FG_LANE - nh,), f32), b_fg[0], jnp.zeros((LANES - FG_LANE - nh,), f32)]
    ).reshape(1, LANES)
    wout_b = w_out[0].astype(bf16)
    wg_b = w_gate[0].astype(bf16)
    wu_b = w_up[0].astype(bf16)
    wd_b = w_down[0].astype(bf16)
    anorm = attn_norm[0].reshape(1, D_MODEL)
    fnorm = ffn_norm[0].reshape(1, D_MODEL)
    qg = jnp.tile(q_norm[0], LANES // ATTN_HEAD_DIM).reshape(1, LANES)
    kg = jnp.tile(k_norm[0], LANES // ATTN_HEAD_DIM).reshape(1, LANES)
    mnorm = mlstm_norm[0].reshape(nh, MLSTM_HEAD_DIM)
    sinks = attn_sinks[0]

    wa = _prompt_attn_weights(w)
    idx = np.arange(2 * LANES)
    same = (idx[:, None] // LANES == idx[None, :] // LANES) & (
        (idx[:, None] // QUARTER) % 2 == (idx[None, :] // QUARTER) % 2)
    bd = jnp.asarray(same, dtype=bf16)
    sink_rows_p = jnp.repeat(sinks.reshape(KV_HEADS, ATTN_GROUP), QBLOCK, axis=1)
    bias = jnp.where(jnp.arange(2 * QBLOCK)[None, None, :] == 0, sink_rows_p[:, :, None], NEG)
    qgq = _quarters(q_norm[0]).reshape(1, LANES)
    kgq = _quarters(k_norm[0]).reshape(1, LANES)
    pos_p = np.arange(tp, dtype=np.float32)
    cos_p, sin_p = _rope_tables_quarters(pos_p)
    cos_w, sin_w = _rope_tables(pos_p[tp - WINDOW:])
    x1_p, k_p, v_p, cext_p, m_p = _prompt_mixer(
        x_prompt[0], cos_p, sin_p, wa, w, wgate, wout_b, bd, bias, anorm, qgq, kgq, kg, cos_w, sin_w,
        gbias, mnorm)

    wq_s = w[:, COL_QA:COL_KA].reshape(D_MODEL, KV_HEADS, ATTN_GROUP, ATTN_HEAD_DIM)
    wq_s = wq_s.transpose(0, 2, 1, 3).reshape(D_MODEL, ATTN_WIDTH)
    wo_a = wout_b[:ATTN_WIDTH].reshape(KV_HEADS, ATTN_GROUP, ATTN_HEAD_DIM, D_MODEL)
    wo_a = wo_a.transpose(1, 0, 2, 3).reshape(ATTN_WIDTH, D_MODEL)
    wout_s = jnp.concatenate([wo_a, wout_b[ATTN_WIDTH:]], axis=0)
    sink_tile = jnp.broadcast_to(
        jnp.repeat(sinks.reshape(KV_HEADS, ATTN_GROUP).T.reshape(-1), tpad)[:, None],
        (ATTN_HEADS * tpad, LANES))
    lanes = np.arange(LANES)
    bd_s = jnp.asarray(lanes[:, None] // ATTN_HEAD_DIM == lanes[None, :] // ATTN_HEAD_DIM, dtype=bf16)
    cos_s, sin_s = _rope_tables(np.arange(tpad, dtype=np.float32) + np.float32(PAST_LEN))
    cos_s = np.tile(cos_s, (SAMPLE_BATCH_BLOCK, 1))
    sin_s = np.tile(sin_s, (SAMPLE_BATCH_BLOCK, 1))
    x_pad = jnp.pad(x_sample, ((0, 0), (0, tpad - nt), (0, 0)))
    ckt = cache_k[0].reshape(nb, WINDOW, KV_WIDTH).transpose(0, 2, 1)
    cvt = cache_v[0].reshape(nb, WINDOW, KV_WIDTH).transpose(0, 2, 1)
    x1_pad, nkt, nvt, c_new, n_new, m_pad = _sample_mixer(
        x_pad, ckt, cvt, state_C[0], state_n[0], state_m[0][:, :, None], cos_s, sin_s, wq_s, w, wgate,
        wout_s, sink_tile, bd_s, anorm, qg, kg, gbias, mnorm)
    y_p, y_s = _ffn(x1_p, x1_pad[:, :nt].reshape(nb * nt, D_MODEL), fnorm, wg_b, wu_b, wd_b)
    m_new = m_pad[:, :nh, 0]

    new_k_s = nkt.transpose(0, 2, 1)
    new_v_s = nvt.transpose(0, 2, 1)

    kv_shape = (1, 1, WINDOW, KV_HEADS, ATTN_HEAD_DIM)
    return (
        y_p[None],
        y_s.reshape(nb, nt, D_MODEL),
        k_p.reshape(kv_shape),
        v_p.reshape(kv_shape),
        cext_p[None, None, :, :, :MLSTM_HEAD_DIM],
        cext_p[None, None, :, :, MLSTM_HEAD_DIM],
        m_p[None, None, :nh, 0],
        new_k_s.reshape(1, nb, WINDOW, KV_HEADS, ATTN_HEAD_DIM),
        new_v_s.reshape(1, nb, WINDOW, KV_HEADS, ATTN_HEAD_DIM),
        c_new.reshape(1, nb, nh, MLSTM_HEAD_DIM, MLSTM_HEAD_DIM),
        n_new.reshape(1, nb, nh, MLSTM_HEAD_DIM),
        m_new.reshape(1, nb, nh),
    )
```

```python
import jax
import jax.numpy as jnp
import numpy as np
from jax import lax
from jax.experimental import pallas as pl
from jax.experimental.pallas import tpu as pltpu

D_MODEL = 1024
PAST_LEN = 16384
ATTN_HEADS = 8
KV_HEADS = 2
ATTN_HEAD_DIM = 64
ATTN_GROUP = ATTN_HEADS // KV_HEADS
ATTN_WIDTH = ATTN_HEADS * ATTN_HEAD_DIM
KV_WIDTH = KV_HEADS * ATTN_HEAD_DIM
WINDOW = 128
ROPE_THETA = 10000.0
MLSTM_HEADS = 4
MLSTM_HEAD_DIM = 128
MLSTM_WIDTH = MLSTM_HEADS * MLSTM_HEAD_DIM
MIX_WIDTH = ATTN_WIDTH + MLSTM_WIDTH
D_FF = 2816
NORM_EPS = 1e-6

LANES = 128
SUBLANES = 8
VMEM_LIMIT = 56 * 1024 * 1024

COL_QA = 0
COL_KA = COL_QA + ATTN_WIDTH
COL_VA = COL_KA + KV_WIDTH
COL_QM = COL_VA + KV_WIDTH
COL_KM = COL_QM + MLSTM_WIDTH
COL_VM = COL_KM + MLSTM_WIDTH
COL_OM = COL_VM + MLSTM_WIDTH
COL_G = COL_OM + MLSTM_WIDTH
IN_WIDTH = COL_G + 2 * MLSTM_HEADS
FG_LANE = SUBLANES

PROMPT_BLOCK = 256
QBLOCK = WINDOW
MCHUNK = 128
PROJ_CHUNK = 256
FFN_BLOCK = 512
FFN_CHUNK = 256
WEIGHT_STAGE_ROWS = 128
SAMPLE_BATCH_BLOCK = 16
SAMPLE_TOKENS = 4
NEG = -1e30

f32 = jnp.float32
bf16 = jnp.bfloat16


def _rms(x, gain):
    return x * lax.rsqrt(jnp.mean(x * x, axis=-1, keepdims=True) + NORM_EPS) * gain


def _segsum64(s, lane):
    for k in (1, 2, 4, 8, 16, 32):
        s = s + jnp.where((lane & k) != 0, pltpu.roll(s, k, 1), pltpu.roll(s, LANES - k, 1))
    return s


def _headnorm_rope(xs, gain, cos, sin_signed, lane):
    ss = _segsum64(xs * xs, lane)
    y = xs * lax.rsqrt(ss * (1.0 / ATTN_HEAD_DIM) + NORM_EPS) * gain
    partner = jnp.where((lane & 32) != 0, pltpu.roll(y, 32, 1), pltpu.roll(y, LANES - 32, 1))
    return y * cos + partner * sin_signed


def _group_sumsq(xs, bd_ref):
    x2 = xs * xs
    hi = x2.astype(bf16)
    lo = (x2 - hi.astype(f32)).astype(bf16)
    return (jnp.dot(hi, bd_ref[...], preferred_element_type=f32)
            + jnp.dot(lo, bd_ref[...], preferred_element_type=f32))


PA_Q = 0
PA_K = PA_Q + ATTN_WIDTH
PA_V = PA_K + KV_HEADS * LANES
PA_WIDTH = PA_V + KV_HEADS * LANES
QUARTER = ATTN_HEAD_DIM // 2
QA_Q = 0
QA_K = QA_Q + 2 * ATTN_WIDTH
QA_V = QA_K + KV_HEADS * LANES
QA_VZ = QA_V + KV_HEADS * LANES
QA_WIDTH = QA_VZ + KV_HEADS * LANES
ZM_WIDTH = 4 * MLSTM_WIDTH


def _norm_rope_quarters(xs, ss, gain, cos, sin_signed):
    y = xs * lax.rsqrt(ss * (1.0 / ATTN_HEAD_DIM) + NORM_EPS) * gain
    return y * cos + pltpu.roll(y, LANES // 2, 1) * sin_signed


def _col_chunks(width):
    return [(c, min(c + PROJ_CHUNK, width)) for c in range(0, width, PROJ_CHUNK)]


def _mixer_proj_jobs(x_ref, anorm_ref, wa_ref, w_ref, wgate_ref, za_ref, zm_ref, gz_ref):
    h = _rms(x_ref[...], anorm_ref[...]).astype(bf16)

    def proj_job(w_src, wc0, z_ref, c0, c1):
        def run():
            z_ref[:, c0:c1] = jnp.dot(h, w_src[:, wc0 + c0:wc0 + c1], preferred_element_type=f32)
        return run

    return ([proj_job(wa_ref, 0, za_ref, c0, c1) for c0, c1 in _col_chunks(PA_WIDTH)]
            + [proj_job(w_ref, COL_QM, zm_ref, c0, c1) for c0, c1 in _col_chunks(ZM_WIDTH)]
            + [proj_job(wgate_ref, 0, gz_ref, 0, LANES)])


def _mixer_prep_jobs(za_ref, gz_ref, cos_ref, sin_ref, bd_ref, qg_ref, kg_ref, gbias_ref, qa_ref,
                     colf_ref, urow_ref, wrow_ref, mst_ref, mout_ref):
    tb = za_ref.shape[0]

    def gates_job():
        lane_t = lax.broadcasted_iota(jnp.int32, (tb, LANES), 1)
        gcol = gz_ref[...] + gbias_ref[...]
        acol = jnp.where(lane_t < FG_LANE, gcol, jax.nn.log_sigmoid(gcol))
        arow = acol.T
        lane8 = lax.broadcasted_iota(jnp.int32, (SUBLANES, LANES), 1)
        lane_in = lane8 & (MCHUNK - 1)
        m_prev = mst_ref[:, 0:1]
        stacks = []
        for sb in range(tb // LANES):
            ls = slice(sb * LANES, (sb + 1) * LANES)
            ig8 = arow[0:SUBLANES, ls]
            lf8 = arow[FG_LANE:FG_LANE + SUBLANES, ls]
            b8 = lf8
            k = 1
            while k < MCHUNK:
                b8 = b8 + jnp.where(lane_in >= k, pltpu.roll(b8, k, 1), 0.0)
                k *= 2
            u8 = ig8 - b8
            cm8 = u8
            k = 1
            while k < MCHUNK:
                cm8 = jnp.maximum(cm8, jnp.where(lane_in >= k, pltpu.roll(cm8, k, 1), NEG))
                k *= 2
            g8 = jnp.zeros_like(u8)
            mp8 = jnp.zeros_like(u8)
            gl8 = jnp.zeros_like(u8)
            for c in range(LANES // MCHUNK):
                in_chunk = (lane8 // MCHUNK) == c
                gc = jnp.maximum(cm8, m_prev)
                last = c * MCHUNK + MCHUNK - 1
                g_last = jnp.max(jnp.where(lane8 == last, gc, NEG), axis=1, keepdims=True)
                b_last = jnp.max(jnp.where(lane8 == last, b8, NEG), axis=1, keepdims=True)
                g8 = jnp.where(in_chunk, gc, g8)
                mp8 = jnp.where(in_chunk, m_prev, mp8)
                gl8 = jnp.where(in_chunk, g_last, gl8)
                m_prev = b_last + g_last
            a8 = jnp.exp(mp8 - g8)
            emt8 = jnp.exp(-(b8 + g8))
            aend8 = jnp.exp(mp8 - gl8)
            stacks.append(jnp.concatenate(
                [g8, a8, emt8, aend8, jnp.zeros((LANES - 4 * SUBLANES, LANES), f32)], axis=0))
            urow_ref[:, ls] = u8
            wrow_ref[:, ls] = jnp.exp(u8 - gl8)
        mst_ref[...] = jnp.broadcast_to(m_prev, mst_ref.shape)
        mout_ref[...] = jnp.broadcast_to(m_prev, mout_ref.shape)
        colf_ref[...] = jnp.concatenate(stacks, axis=1).T

    def prep_job(qb):
        def run():
            rows = slice(qb * QBLOCK, (qb + 1) * QBLOCK)
            lane = lax.broadcasted_iota(jnp.int32, (QBLOCK, LANES), 1)
            head_a = ((lane // QUARTER) & 1) == 0
            row0 = lax.broadcasted_iota(jnp.int32, (QBLOCK, LANES), 0) == 0
            cos = cos_ref[rows, :]
            sin = sin_ref[rows, :]
            ss = [_group_sumsq(za_ref[rows, d * 2 * LANES:(d + 1) * 2 * LANES], bd_ref)
                  for d in range(PA_V // (2 * LANES))]
            for j in range(PA_V // LANES):
                is_q = j < ATTN_WIDTH // LANES
                y = _norm_rope_quarters(za_ref[rows, j * LANES:(j + 1) * LANES],
                                        ss[j // 2][:, (j % 2) * LANES:(j % 2 + 1) * LANES],
                                        qg_ref[...] if is_q else kg_ref[...], cos, sin)
                if is_q:
                    y = y * (ATTN_HEAD_DIM ** -0.5)
                    qa_ref[rows, QA_Q + 2 * j * LANES:QA_Q + (2 * j + 1) * LANES] = (
                        jnp.where(head_a, y, 0.0).astype(bf16))
                    qa_ref[rows, QA_Q + (2 * j + 1) * LANES:QA_Q + (2 * j + 2) * LANES] = (
                        jnp.where(head_a, 0.0, y).astype(bf16))
                else:
                    c = j - ATTN_WIDTH // LANES
                    qa_ref[rows, QA_K + c * LANES:QA_K + (c + 1) * LANES] = y.astype(bf16)
            for c in range(KV_HEADS):
                v = za_ref[rows, PA_V + c * LANES:PA_V + (c + 1) * LANES]
                qa_ref[rows, QA_V + c * LANES:QA_V + (c + 1) * LANES] = v.astype(bf16)
                qa_ref[rows, QA_VZ + c * LANES:QA_VZ + (c + 1) * LANES] = (
                    jnp.where(row0, 0.0, v).astype(bf16))
        return run

    return [gates_job] + [prep_job(qb) for qb in range(tb // QBLOCK)]


def _mixer_outproj_jobs(xs_ref, mix_ref, wout_ref, x1_ref):
    def job(c0, c1):
        def run():
            x1_ref[:, c0:c1] = xs_ref[:, c0:c1] + jnp.dot(mix_ref[...], wout_ref[:, c0:c1],
                                                          preferred_element_type=f32)
        return run

    return [job(c0, c1) for c0, c1 in _col_chunks(D_MODEL)]


def _mixer_window_out(xs_ref, anorm_ref, w_ref, kgs_ref, coss_ref, sins_ref, kout_ref, vout_ref):
    tb = xs_ref.shape[0]
    h = _rms(xs_ref[tb - WINDOW:, :], anorm_ref[...]).astype(bf16)
    zs = jnp.dot(h, w_ref[:, COL_KA:COL_QM], preferred_element_type=f32)
    lane_s = lax.broadcasted_iota(jnp.int32, (WINDOW, LANES), 1)
    kout_ref[...] = _headnorm_rope(zs[:, :KV_WIDTH], kgs_ref[...], coss_ref[...], sins_ref[...], lane_s)
    vout_ref[...] = zs[:, KV_WIDTH:]


def _mixer_core(first_block, last_block, fillers, prep_jobs, qa_ref, zm_ref, colf_ref, urow_ref, wrow_ref,
                mix_ref, bias_ref, mnorm_ref, kprev_ref, vprev_ref, cst_ref, cext_ref):
    tb = qa_ref.shape[0]
    nqb = tb // QBLOCK
    nch = tb // MCHUNK
    fillers = list(fillers)
    n_fill = len(fillers)
    slots = len(prep_jobs) + nqb * KV_HEADS + nch * MLSTM_HEADS
    progress = [0]

    def fill():
        progress[0] += 1
        while n_fill - len(fillers) < min(n_fill, -(-n_fill * progress[0] // slots)):
            fillers.pop(0)()

    for job in prep_jobs:
        job()
        fill()

    low_half = lax.broadcasted_iota(jnp.int32, (QBLOCK, LANES), 1) < ATTN_HEAD_DIM
    qi = lax.broadcasted_iota(jnp.int32, (ATTN_GROUP * QBLOCK, 2 * QBLOCK), 0) & (QBLOCK - 1)
    kj = lax.broadcasted_iota(jnp.int32, (ATTN_GROUP * QBLOCK, 2 * QBLOCK), 1)
    band = (kj > qi) & (kj <= qi + QBLOCK)
    ones_slab = jnp.ones((2 * QBLOCK, LANES), bf16)

    def attn_unit(qb, c):
        rows = slice(qb * QBLOCK, (qb + 1) * QBLOCK)
        kcols = slice(QA_K + c * LANES, QA_K + (c + 1) * LANES)
        if qb == 0:
            kprev, vprev = kprev_ref[c], vprev_ref[c]
        else:
            prev = slice((qb - 1) * QBLOCK, qb * QBLOCK)
            kprev = qa_ref[prev, kcols]
            vprev = qa_ref[prev, QA_VZ + c * LANES:QA_VZ + (c + 1) * LANES]
        kcat = jnp.concatenate([kprev, qa_ref[rows, kcols]], axis=0)
        vcat = jnp.concatenate([vprev, qa_ref[rows, QA_V + c * LANES:QA_V + (c + 1) * LANES]], axis=0)
        vext = jnp.concatenate([vcat, ones_slab], axis=1)
        q0 = QA_Q + c * ATTN_GROUP * LANES
        qst = jnp.concatenate([qa_ref[rows, q0 + g * LANES:q0 + (g + 1) * LANES]
                               for g in range(ATTN_GROUP)], axis=0)
        s = lax.dot_general(qst, kcat, (((1,), (1,)), ((), ())), preferred_element_type=f32)
        valid = band & (kj >= QBLOCK) if (first_block and qb == 0) else band
        s = jnp.where(valid, s, bias_ref[c])
        p = jnp.exp(s - jnp.max(s, axis=-1, keepdims=True)).astype(bf16)
        of = jnp.dot(p, vext, preferred_element_type=f32)
        o = of[:, :LANES] / of[:, LANES:]
        for jj in range(2):
            pair = jnp.where(low_half, o[(2 * jj) * QBLOCK:(2 * jj + 1) * QBLOCK],
                             o[(2 * jj + 1) * QBLOCK:(2 * jj + 2) * QBLOCK])
            col = (2 * c + jj) * LANES
            mix_ref[rows, col:col + LANES] = pair.astype(bf16)

    ti = lax.broadcasted_iota(jnp.int32, (MCHUNK, MCHUNK), 0)
    si = lax.broadcasted_iota(jnp.int32, (MCHUNK, MCHUNK), 1)
    causal = si <= ti
    ones_l = jnp.ones((MCHUNK, LANES), bf16)

    def mlstm_unit(hd, c, cext):
        hcols = lambda k: slice((k * MLSTM_HEADS + hd) * MLSTM_HEAD_DIM,
                                (k * MLSTM_HEADS + hd + 1) * MLSTM_HEAD_DIM)
        rows = slice(c * MCHUNK, (c + 1) * MCHUNK)
        qb_ = zm_ref[rows, hcols(0)].astype(bf16)
        kf = zm_ref[rows, hcols(1)] * (MLSTM_HEAD_DIM ** -0.5)
        vb = zm_ref[rows, hcols(2)].astype(bf16)
        og = zm_ref[rows, hcols(3)]
        g_c = colf_ref[rows, hd:hd + 1]
        a_c = colf_ref[rows, SUBLANES + hd:SUBLANES + hd + 1]
        emt_c = colf_ref[rows, 2 * SUBLANES + hd:2 * SUBLANES + hd + 1]
        aend = colf_ref[c * MCHUNK:c * MCHUNK + 1, 3 * SUBLANES + hd:3 * SUBLANES + hd + 1]
        u_r = urow_ref[hd:hd + 1, rows]
        w_r = wrow_ref[hd:hd + 1, rows]
        dmat = jnp.exp(jnp.where(causal, u_r - g_c, NEG))
        smat = lax.dot_general(qb_, kf.astype(bf16), (((1,), (1,)), ((), ())),
                               preferred_element_type=f32) * dmat
        vext = jnp.concatenate([vb, ones_l], axis=1)
        nd = (a_c * jnp.dot(qb_, cext.astype(bf16), preferred_element_type=f32)
              + jnp.dot(smat.astype(bf16), vext, preferred_element_type=f32))
        hraw = nd[:, :MLSTM_HEAD_DIM] / jnp.maximum(jnp.abs(nd[:, MLSTM_HEAD_DIM:]), emt_c)
        hn = _rms(hraw, mnorm_ref[hd:hd + 1, :])
        mix_ref[rows, ATTN_WIDTH + hd * MLSTM_HEAD_DIM:ATTN_WIDTH + (hd + 1) * MLSTM_HEAD_DIM] = (
            (hn * jax.nn.sigmoid(og)).astype(bf16))
        kw_t = (kf.T * w_r).astype(bf16)
        return aend * cext + jnp.dot(kw_t, vext, preferred_element_type=f32)

    cexts =[cst_ref[hd] for hd in range(MLSTM_HEADS)]
    attn_jobs = [(qb, c) for qb in range(nqb) for c in range(KV_HEADS)]
    mlstm_jobs = [(hd, c) for c in range(nch) for hd in range(MLSTM_HEADS)]
    while attn_jobs or mlstm_jobs:
        if attn_jobs:
            attn_unit(*attn_jobs.pop(0))
            fill()
        for _ in range(2 if len(mlstm_jobs) > 2 * len(attn_jobs) else 1):
            if mlstm_jobs:
                hd, c = mlstm_jobs.pop(0)
                cexts[hd] = mlstm_unit(hd, c, cexts[hd])
                fill()
    last_rows = slice(tb - QBLOCK, tb)
    for c in range(KV_HEADS):
        kprev_ref[c] = qa_ref[last_rows, QA_K + c * LANES:QA_K + (c + 1) * LANES]
        vprev_ref[c] = qa_ref[last_rows, QA_VZ + c * LANES:QA_VZ + (c + 1) * LANES]
    for hd in range(MLSTM_HEADS):
        cst_ref[hd] = cexts[hd]
        if last_block:
            cext_ref[hd] = cexts[hd]


def _prompt_mixer_kernel(x_ref, cos_ref, sin_ref, wa_ref, w_ref, wgate_ref, wout_ref, bd_ref, bias_ref,
                         anorm_ref, qg_ref, kg_ref, kgs_ref, coss_ref, sins_ref, gbias_ref, mnorm_ref,
                         x1_ref, kout_ref, vout_ref, cext_ref, mout_ref,
                         za0, za1, zm0, zm1, gz0, gz1, mix0, mix1, xs0, xs1,
                         qa_ref, colf_ref, urow_ref, wrow_ref, kprev_ref, vprev_ref, cst_ref, mst_ref):
    step = pl.program_id(0)
    nblk = pl.num_programs(0) - 2
    za, zm, gz, mix, xs = (za0, za1), (zm0, zm1), (gz0, gz1), (mix0, mix1), (xs0, xs1)

    def run(parity, do_in, do_core, do_out, first_block=False, last_block=False):
        other = 1 - parity
        jobs = []
        if do_out:
            jobs += _mixer_outproj_jobs(xs[parity], mix[parity], wout_ref, x1_ref)
        if do_in:
            jobs += _mixer_proj_jobs(x_ref, anorm_ref, wa_ref, w_ref, wgate_ref, za[parity], zm[parity],
                                     gz[parity])
        if do_core:
            prep = _mixer_prep_jobs(za[other], gz[other], cos_ref, sin_ref, bd_ref, qg_ref, kg_ref,
                                    gbias_ref, qa_ref, colf_ref, urow_ref, wrow_ref, mst_ref, mout_ref)
            _mixer_core(first_block, last_block, jobs, prep, qa_ref, zm[other], colf_ref, urow_ref,
                        wrow_ref, mix[other], bias_ref, mnorm_ref, kprev_ref, vprev_ref, cst_ref, cext_ref)
        else:
            for job in jobs:
                job()
        if last_block:
            _mixer_window_out(xs[other], anorm_ref, w_ref, kgs_ref, coss_ref, sins_ref, kout_ref, vout_ref)
        if do_in:
            xs[parity][...] = x_ref[...]

    @pl.when(step == 0)
    def _first():
        kprev_ref[...] = jnp.zeros_like(kprev_ref)
        vprev_ref[...] = jnp.zeros_like(vprev_ref)
        cst_ref[...] = jnp.zeros_like(cst_ref)
        mst_ref[...] = jnp.zeros_like(mst_ref)
        run(0, True, False, False)

    @pl.when(step == 1)
    def _second():
        run(1, True, True, False, first_block=True)

    steady = (step >= 2) & (step < nblk)

    @pl.when(steady & (step % 2 == 0))
    def _even():
        run(0, True, True, True)

    @pl.when(steady & (step % 2 == 1))
    def _odd():
        run(1, True, True, True)

    @pl.when(step == nblk)
    def _drain_core():
        run(0, False, True, True, last_block=True)

    @pl.when(step == nblk + 1)
    def _drain_out():
        run(1, False, False, True)


def _const_spec(shape, single=False):
    nd = len(shape)
    if single:
        return pl.BlockSpec(shape, lambda i, *_: (0,) * nd, pipeline_mode=pl.Buffered(1))
    return pl.BlockSpec(shape, lambda i, *_: (0,) * nd)


def _prompt_mixer(x, cos, sin, wa, w, wgate, wout_b, bd, bias, anorm, qg, kg, kgs, coss, sins, gbias, mnorm):
    t = x.shape[0]
    tb = PROMPT_BLOCK
    nblk = t // tb
    assert nblk % 2 == 0 and nblk >= 4
    state_shape = (MLSTM_HEADS, MLSTM_HEAD_DIM, 2 * MLSTM_HEAD_DIM)
    last = nblk - 1
    lag = lambda d: (lambda i: (jnp.clip(i - d, 0, last), 0))
    return pl.pallas_call(
        _prompt_mixer_kernel,
        grid=(nblk + 2,),
        in_specs=[
            pl.BlockSpec((tb, D_MODEL), lag(0)),
            pl.BlockSpec((tb, LANES), lag(1)),
            pl.BlockSpec((tb, LANES), lag(1)),
            _const_spec((D_MODEL, PA_WIDTH), single=True),
            _const_spec((D_MODEL, IN_WIDTH), single=True),
            _const_spec((D_MODEL, LANES), single=True),
            _const_spec((MIX_WIDTH, D_MODEL), single=True),
            _const_spec((2 * LANES, 2 * LANES), single=True),
            _const_spec((KV_HEADS, ATTN_GROUP * QBLOCK, 2 * QBLOCK), single=True),
            _const_spec((1, D_MODEL)),
            _const_spec((1, LANES)),
            _const_spec((1, LANES)),
            _const_spec((1, LANES)),
            _const_spec((WINDOW, LANES)),
            _const_spec((WINDOW, LANES)),
            _const_spec((1, LANES)),
            _const_spec((MLSTM_HEADS, MLSTM_HEAD_DIM)),
        ],
        out_specs=[
            pl.BlockSpec((tb, D_MODEL), lag(2)),
            _const_spec((WINDOW, KV_WIDTH)),
            _const_spec((WINDOW, KV_WIDTH)),
            _const_spec(state_shape),
            _const_spec((SUBLANES, LANES)),
        ],
        out_shape=[
            jax.ShapeDtypeStruct((t, D_MODEL), f32),
            jax.ShapeDtypeStruct((WINDOW, KV_WIDTH), f32),
            jax.ShapeDtypeStruct((WINDOW, KV_WIDTH), f32),
            jax.ShapeDtypeStruct(state_shape, f32),
            jax.ShapeDtypeStruct((SUBLANES, LANES), f32),
        ],
        scratch_shapes=(
            [pltpu.VMEM((tb, PA_WIDTH), f32)] * 2 + [pltpu.VMEM((tb, ZM_WIDTH), f32)] * 2
            + [pltpu.VMEM((tb, LANES), f32)] * 2
            + [pltpu.VMEM((tb, MIX_WIDTH), bf16)] * 2 + [pltpu.VMEM((tb, D_MODEL), f32)] * 2
            + [pltpu.VMEM((tb, QA_WIDTH), bf16), pltpu.VMEM((tb, LANES), f32)]
            + [pltpu.VMEM((SUBLANES, tb), f32)] * 2
            + [pltpu.VMEM((KV_HEADS, WINDOW, LANES), bf16)] * 2
            + [pltpu.VMEM(state_shape, f32), pltpu.VMEM((SUBLANES, LANES), f32)]),
        compiler_params=pltpu.CompilerParams(
            dimension_semantics=("arbitrary",), vmem_limit_bytes=VMEM_LIMIT),
        name="prompt_mixer",
    )(x, cos, sin, wa, w, wgate, wout_b, bd, bias, anorm, qg, kg, kgs, coss, sins, gbias, mnorm)


def _stage_weights_bf16(pairs, stage_ref, sem_ref):
    rows = stage_ref.shape[1]
    chunks = [(src, dst, r0) for src, dst in pairs for r0 in range(0, src.shape[0], rows)]

    def copy(i):
        src, _, r0 = chunks[i]
        slot = i % 2
        return pltpu.make_async_copy(src.at[pl.ds(r0, rows), :],
                                     stage_ref.at[slot, :, pl.ds(0, src.shape[1])], sem_ref.at[slot])

    copy(0).start()
    for i, (src, dst, r0) in enumerate(chunks):
        if i + 1 < len(chunks):
            copy(i + 1).start()
        copy(i).wait()
        dst[r0:r0 + rows, :] = stage_ref[i % 2, :, 0:src.shape[1]].astype(bf16)


def _ffn_kernel(xp_ref, xs_ref, g_ref, wg_hbm, wu_hbm, wd_hbm, op_ref, os_ref,
                wg_ref, wu_ref, wd_ref, stage_ref, sem_ref):
    step = pl.program_id(0)
    last = pl.num_programs(0) - 1

    @pl.when(step == 0)
    def _weights():
        _stage_weights_bf16([(wg_hbm, wg_ref), (wu_hbm, wu_ref), (wd_hbm, wd_ref)], stage_ref, sem_ref)

    @pl.when(step < last)
    def _prompt_rows():
        _ffn_rows(xp_ref, g_ref, wg_ref, wu_ref, wd_ref, op_ref)

    @pl.when(step == last)
    def _sample_rows():
        _ffn_rows(xs_ref, g_ref, wg_ref, wu_ref, wd_ref, os_ref)


def _ffn_rows(x_ref, g_ref, wg_ref, wu_ref, wd_ref, o_ref):
    x = x_ref[...]
    hf = _rms(x, g_ref[...]).astype(bf16)
    acc = x
    for c in range(D_FF // FFN_CHUNK):
        cs = slice(c * FFN_CHUNK, (c + 1) * FFN_CHUNK)
        gate = jnp.dot(hf, wg_ref[:, cs], preferred_element_type=f32)
        up = jnp.dot(hf, wu_ref[:, cs], preferred_element_type=f32)
        act = (gate * jax.nn.sigmoid(gate) * up).astype(bf16)
        acc = acc + jnp.dot(act, wd_ref[cs, :], preferred_element_type=f32)
    o_ref[...] = acc


def _ffn(x_p, x_s, fnorm, w_gate, w_up, w_down):
    n = x_p.shape[0]
    ns = x_s.shape[0]
    tm = FFN_BLOCK
    last = n // tm - 1
    hbm = pl.BlockSpec(memory_space=pl.ANY)
    return pl.pallas_call(
        _ffn_kernel,
        grid=(n // tm + 1,),
        in_specs=[
            pl.BlockSpec((tm, D_MODEL), lambda i: (jnp.minimum(i, last), 0)),
            _const_spec((ns, D_MODEL), single=True),
            _const_spec((1, D_MODEL)),
            hbm, hbm, hbm,
        ],
        out_specs=[pl.BlockSpec((tm, D_MODEL), lambda i: (jnp.minimum(i, last), 0)),
                   _const_spec((ns, D_MODEL))],
        out_shape=[jax.ShapeDtypeStruct((n, D_MODEL), f32), jax.ShapeDtypeStruct((ns, D_MODEL), f32)],
        scratch_shapes=[
            pltpu.VMEM((D_MODEL, D_FF), bf16), pltpu.VMEM((D_MODEL, D_FF), bf16),
            pltpu.VMEM((D_FF, D_MODEL), bf16),
            pltpu.VMEM((2, WEIGHT_STAGE_ROWS, D_FF), f32), pltpu.SemaphoreType.DMA((2,))],
        compiler_params=pltpu.CompilerParams(
            dimension_semantics=("arbitrary",), vmem_limit_bytes=VMEM_LIMIT),
        name="ffn",
    )(x_p, x_s, fnorm, w_gate, w_up, w_down)


def _sample_mixer_kernel(x_ref, ckt_ref, cvt_ref, c_ref, n_ref, m_ref, cos_ref, sin_ref, wqs_ref, w_ref,
                         wgate_ref, wout_ref, sink_ref, bd_ref, anorm_ref, qg_ref, kg_ref, gbias_ref,
                         mnorm_ref, x1_ref, nkt_ref, nvt_ref, cn_ref, nn_ref, mn_ref):
    bb, tpad, _ = x_ref.shape
    nrows = bb * tpad
    nreal = SAMPLE_TOKENS
    h = _rms(x_ref[...].reshape(nrows, D_MODEL), anorm_ref[...]).astype(bf16)
    z = jnp.concatenate(
        [jnp.dot(h, wqs_ref[...], preferred_element_type=f32),
         jnp.dot(h, w_ref[:, COL_KA:COL_G], preferred_element_type=f32),
         jnp.dot(h, wgate_ref[...], preferred_element_type=f32)], axis=1)
    lane = lax.broadcasted_iota(jnp.int32, (nrows, LANES), 1)
    low = lane < ATTN_HEAD_DIM
    cos = cos_ref[...]
    sin = sin_ref[...]

    def per_seq(a):
        return a.reshape(bb, tpad, a.shape[-1])

    def norm_rope(xs, gain):
        y = xs * lax.rsqrt(_group_sumsq(xs, bd_ref) * (1.0 / ATTN_HEAD_DIM) + NORM_EPS) * gain
        partner = jnp.where((lane & QUARTER) != 0, pltpu.roll(y, QUARTER, 1),
                            pltpu.roll(y, LANES - QUARTER, 1))
        return y * cos + partner * sin

    q_rows = []
    for j in range(ATTN_GROUP):
        qs = norm_rope(z[:, COL_QA + j * LANES:COL_QA + (j + 1) * LANES], qg_ref[...])
        qs = qs * (ATTN_HEAD_DIM ** -0.5)
        q_rows.append(per_seq(jnp.where(low, qs, 0.0)).astype(bf16))
        q_rows.append(per_seq(jnp.where(low, 0.0, qs)).astype(bf16))
    qbd = jnp.concatenate(q_rows, axis=1)
    knew = norm_rope(z[:, COL_KA:COL_KA + KV_WIDTH], kg_ref[...])
    vnew = z[:, COL_VA:COL_VA + KV_WIDTH]
    zpad = jnp.zeros((bb, LANES - tpad, LANES), bf16)
    knp = jnp.concatenate([per_seq(knew).astype(bf16), zpad], axis=1)
    vnp = jnp.concatenate([per_seq(vnew).astype(bf16), zpad], axis=1)
    ckt = ckt_ref[...]
    cvt = cvt_ref[...]
    s = jnp.concatenate(
        [jnp.einsum('bqd,bdw->bqw', qbd, ckt.astype(bf16), preferred_element_type=f32),
         jnp.einsum('bqd,bkd->bqk', qbd, knp, preferred_element_type=f32)], axis=2)
    tq = lax.broadcasted_iota(jnp.int32, s.shape, 1) & (tpad - 1)
    kj = lax.broadcasted_iota(jnp.int32, s.shape, 2)
    valid = ((kj < WINDOW) & (kj > tq)) | ((kj >= WINDOW) & (kj - WINDOW <= tq) & (kj - WINDOW < nreal))
    s = jnp.where(valid, s, NEG)
    sink = sink_ref[:, 0:1][None]
    mx = jnp.maximum(jnp.max(s, axis=-1, keepdims=True), sink)
    p = jnp.exp(s - mx)
    den = jnp.sum(p, axis=-1, keepdims=True) + jnp.exp(sink - mx)
    pb = p.astype(bf16)
    o = (jnp.einsum('bqw,bdw->bqd', pb[:, :, :WINDOW], cvt.astype(bf16), preferred_element_type=f32)
         + jnp.einsum('bqk,bkd->bqd', pb[:, :, WINDOW:], vnp, preferred_element_type=f32)) / den
    low3 = lax.broadcasted_iota(jnp.int32, (bb, tpad, LANES), 2) < ATTN_HEAD_DIM
    mix_parts = []
    for j in range(ATTN_GROUP):
        r0 = 2 * j * tpad
        pair = jnp.where(low3, o[:, r0:r0 + tpad, :], o[:, r0 + tpad:r0 + 2 * tpad, :])
        mix_parts.append(pair.reshape(nrows, LANES).astype(bf16))

    keep = lax.broadcasted_iota(jnp.int32, (KV_WIDTH, WINDOW), 1) < WINDOW - nreal
    knt = knew.T
    vnt = vnew.T
    for b in range(bb):
        shift = (WINDOW - nreal - b * tpad) % LANES
        nkt_ref[b] = jnp.where(keep, pltpu.roll(ckt_ref[b], WINDOW - nreal, 1), pltpu.roll(knt, shift, 1))
        nvt_ref[b] = jnp.where(keep, pltpu.roll(cvt_ref[b], WINDOW - nreal, 1), pltpu.roll(vnt, shift, 1))

    gz = per_seq(z[:, COL_G:COL_G + LANES] + gbias_ref[...])
    lgz = jax.nn.log_sigmoid(gz)
    trow = lax.broadcasted_iota(jnp.int32, (bb, tpad, 1), 1)
    real = trow < nreal
    mn_ref[...] = jnp.zeros_like(mn_ref)
    for hd in range(MLSTM_HEADS):
        hcols = lambda base: slice(base + hd * MLSTM_HEAD_DIM, base + (hd + 1) * MLSTM_HEAD_DIM)
        q = per_seq(z[:, hcols(COL_QM)])
        k = per_seq(z[:, hcols(COL_KM)]) * (MLSTM_HEAD_DIM ** -0.5)
        v = per_seq(z[:, hcols(COL_VM)])
        og = per_seq(z[:, hcols(COL_OM)])
        c0 = c_ref[:, hd]
        n0 = n_ref[:, hd:hd + 1, :]
        m0 = m_ref[:, hd:hd + 1, :]
        ig_c = jnp.where(real, gz[:, :, hd:hd + 1], NEG)
        lf_c = jnp.where(real, lgz[:, :, FG_LANE + hd:FG_LANE + hd + 1], 0.0)
        b_c = jnp.zeros_like(lf_c)
        for sx in range(nreal):
            b_c = b_c + jnp.where(trow >= sx, lf_c[:, sx:sx + 1, :], 0.0)
        dlog = [jnp.where(trow >= sx, b_c - b_c[:, sx:sx + 1, :] + ig_c[:, sx:sx + 1, :], NEG)
                for sx in range(nreal)]
        inter = b_c + m0
        m_t = inter
        for sx in range(nreal):
            m_t = jnp.maximum(m_t, dlog[sx])
        a = jnp.exp(inter - m_t)
        qc = jnp.einsum('btd,bde->bte', q.astype(bf16), c0.astype(bf16), preferred_element_type=f32)
        num = a * qc
        den_m = a * jnp.sum(q * n0, axis=2, keepdims=True)
        for sx in range(nreal):
            sd = jnp.sum(q * k[:, sx:sx + 1, :], axis=2, keepdims=True) * jnp.exp(dlog[sx] - m_t)
            num = num + sd * v[:, sx:sx + 1, :]
            den_m = den_m + sd
        hraw = num / jnp.maximum(jnp.abs(den_m), jnp.exp(-m_t))
        hn = _rms(hraw, mnorm_ref[hd:hd + 1, :][None])
        mix_parts.append((hn * jax.nn.sigmoid(og)).reshape(nrows, MLSTM_HEAD_DIM).astype(bf16))
        last = nreal - 1
        m_new = m_t[:, last:last + 1, :]
        b_last = b_c[:, last:last + 1, :]
        a_end = jnp.exp(b_last + m0 - m_new)
        kw = k * jnp.exp(b_last - b_c + ig_c - m_new)
        cn_ref[:, hd] = a_end * c0 + jnp.einsum('bsd,bse->bde', kw.astype(bf16), v.astype(bf16),
                                               preferred_element_type=f32)
        nn_ref[:, hd:hd + 1, :] = a_end * n0 + jnp.sum(kw, axis=1, keepdims=True)
        mn_ref[:, hd:hd + 1, :] = jnp.broadcast_to(m_new, (bb, 1, LANES))

    mix = jnp.concatenate(mix_parts, axis=1)
    x1 = x_ref[...].reshape(nrows, D_MODEL) + jnp.dot(mix, wout_ref[...], preferred_element_type=f32)
    x1_ref[...] = x1.reshape(bb, tpad, D_MODEL)


def _sample_mixer(x_pad, ckt, cvt, c0, n0, m0, cos, sin, wq_s, w, wgate, wout_s, sink_tile, bd, anorm, qg,
                  kg, gbias, mnorm):
    nb, tpad, _ = x_pad.shape
    bb = SAMPLE_BATCH_BLOCK
    nh = MLSTM_HEADS
    blk = lambda shape: pl.BlockSpec(shape, lambda i: (i,) + (0,) * (len(shape) - 1))
    cblk = (bb, nh, MLSTM_HEAD_DIM, MLSTM_HEAD_DIM)
    return pl.pallas_call(
        _sample_mixer_kernel,
        grid=(nb // bb,),
        in_specs=[blk((bb, tpad, D_MODEL)), blk((bb, KV_WIDTH, WINDOW)), blk((bb, KV_WIDTH, WINDOW)),
                  blk(cblk), blk((bb, nh, MLSTM_HEAD_DIM)), blk((bb, nh, 1)),
                  _const_spec((bb * tpad, LANES)), _const_spec((bb * tpad, LANES)),
                  _const_spec((D_MODEL, ATTN_WIDTH), single=True),
                  _const_spec((D_MODEL, IN_WIDTH), single=True),
                  _const_spec((D_MODEL, LANES), single=True),
                  _const_spec((MIX_WIDTH, D_MODEL), single=True),
                  _const_spec((ATTN_HEADS * tpad, LANES)), _const_spec((LANES, LANES)),
                  _const_spec((1, D_MODEL)),
                  _const_spec((1, LANES)), _const_spec((1, LANES)), _const_spec((1, LANES)),
                  _const_spec((nh, MLSTM_HEAD_DIM))],
        out_specs=[blk((bb, tpad, D_MODEL)), blk((bb, KV_WIDTH, WINDOW)), blk((bb, KV_WIDTH, WINDOW)),
                   blk(cblk), blk((bb, nh, MLSTM_HEAD_DIM)), blk((bb, tpad, LANES))],
        out_shape=[jax.ShapeDtypeStruct((nb, tpad, D_MODEL), f32),
                   jax.ShapeDtypeStruct((nb, KV_WIDTH, WINDOW), f32),
                   jax.ShapeDtypeStruct((nb, KV_WIDTH, WINDOW), f32),
                   jax.ShapeDtypeStruct((nb,) + cblk[1:], f32),
                   jax.ShapeDtypeStruct((nb, nh, MLSTM_HEAD_DIM), f32),
                   jax.ShapeDtypeStruct((nb, tpad, LANES), f32)],
        compiler_params=pltpu.CompilerParams(
            dimension_semantics=("arbitrary",), vmem_limit_bytes=VMEM_LIMIT),
        name="sample_mixer",
    )(x_pad, ckt, cvt, c0, n0, m0, cos, sin, wq_s, w, wgate, wout_s, sink_tile, bd, anorm, qg, kg, gbias,
      mnorm)


def _rope_angles(pos):
    half = ATTN_HEAD_DIM // 2
    inv = ROPE_THETA ** (-np.arange(half, dtype=np.float64) / half)
    ang = pos.astype(np.float64)[:, None] * inv[None, :]
    return np.cos(ang).astype(np.float32), np.sin(ang).astype(np.float32)


def _rope_tables(pos):
    c, s = _rope_angles(pos)
    cos = np.tile(c, (1, LANES // QUARTER))
    sin = np.tile(np.concatenate([-s, s], axis=1), (1, LANES // ATTN_HEAD_DIM))
    return cos, sin


def _rope_tables_quarters(pos):
    c, s = _rope_angles(pos)
    return np.tile(c, (1, LANES // QUARTER)), np.concatenate([-s, -s, s, s], axis=1)


def _quarters(a):
    lo, hi = a[..., :QUARTER], a[..., QUARTER:]
    return jnp.concatenate([lo, lo, hi, hi], axis=-1)


def _prompt_attn_weights(w):
    d = w.shape[0]
    wq = w[:, COL_QA:COL_KA].reshape(d, ATTN_WIDTH // LANES, 2, 2, QUARTER)
    wq = wq.transpose(0, 1, 3, 2, 4).reshape(d, ATTN_WIDTH)
    wk = _quarters(w[:, COL_KA:COL_VA].reshape(d, KV_HEADS, ATTN_HEAD_DIM)).reshape(d, KV_HEADS * LANES)
    wv = w[:, COL_VA:COL_QM].reshape(d, KV_HEADS, 1, ATTN_HEAD_DIM)
    wv = jnp.broadcast_to(wv, (d, KV_HEADS, 2, ATTN_HEAD_DIM)).reshape(d, KV_HEADS * LANES)
    return jnp.concatenate([wq, wk, wv], axis=1)


def kernel(x_prompt, x_sample, cache_k, cache_v, state_C, state_n, state_m, attn_norm, w_in, q_norm,
           k_norm, attn_sinks, b_ig, b_fg, mlstm_norm, w_out, ffn_norm, w_gate, w_up, w_down):
    assert w_in.shape[0] == 1 and x_prompt.shape[0] == 1
    tp = x_prompt.shape[1]
    nb, nt = x_sample.shape[0], x_sample.shape[1]
    assert nt == SAMPLE_TOKENS
    tpad = SUBLANES
    nh = MLSTM_HEADS

    w = w_in[0].astype(bf16)
    pad_a = jnp.zeros((D_MODEL, FG_LANE - nh), bf16)
    pad_b = jnp.zeros((D_MODEL, LANES - FG_LANE - nh), bf16)
    wgate = jnp.concatenate([w[:, COL_G:COL_G + nh], pad_a, w[:, COL_G + nh:], pad_b], axis=1)
    gbias = jnp.concatenate(
        [b_ig[0], jnp.zeros((FG_LANE - nh,), f32), b_fg[0], jnp.zeros((LANES - FG_LANE - nh,), f32)]
    ).reshape(1, LANES)
    wout_b = w_out[0].astype(bf16)
    anorm = attn_norm[0].reshape(1, D_MODEL)
    fnorm = ffn_norm[0].reshape(1, D_MODEL)
    qg = jnp.tile(q_norm[0], LANES // ATTN_HEAD_DIM).reshape(1, LANES)
    kg = jnp.tile(k_norm[0], LANES // ATTN_HEAD_DIM).reshape(1, LANES)
    mnorm = mlstm_norm[0].reshape(nh, MLSTM_HEAD_DIM)
    sinks = attn_sinks[0]

    wa = _prompt_attn_weights(w)
    idx = np.arange(2 * LANES)
    same = (idx[:, None] // LANES == idx[None, :] // LANES) & (
        (idx[:, None] // QUARTER) % 2 == (idx[None, :] // QUARTER) % 2)
    bd = jnp.asarray(same, dtype=bf16)
    sink_rows_p = jnp.repeat(sinks.reshape(KV_HEADS, ATTN_GROUP), QBLOCK, axis=1)
    bias = jnp.where(jnp.arange(2 * QBLOCK)[None, None, :] == 0, sink_rows_p[:, :, None], NEG)
    qgq = _quarters(q_norm[0]).reshape(1, LANES)
    kgq = _quarters(k_norm[0]).reshape(1, LANES)
    pos_p = np.arange(tp, dtype=np.float32)
    cos_p, sin_p = _rope_tables_quarters(pos_p)
    cos_w, sin_w = _rope_tables(pos_p[tp - WINDOW:])
    x1_p, k_p, v_p, cext_p, m_p = _prompt_mixer(
        x_prompt[0], cos_p, sin_p, wa, w, wgate, wout_b, bd, bias, anorm, qgq, kgq, kg, cos_w, sin_w,
        gbias, mnorm)

    wq_s = w[:, COL_QA:COL_KA].reshape(D_MODEL, KV_HEADS, ATTN_GROUP, ATTN_HEAD_DIM)
    wq_s = wq_s.transpose(0, 2, 1, 3).reshape(D_MODEL, ATTN_WIDTH)
    wo_a = wout_b[:ATTN_WIDTH].reshape(KV_HEADS, ATTN_GROUP, ATTN_HEAD_DIM, D_MODEL)
    wo_a = wo_a.transpose(1, 0, 2, 3).reshape(ATTN_WIDTH, D_MODEL)
    wout_s = jnp.concatenate([wo_a, wout_b[ATTN_WIDTH:]], axis=0)
    sink_tile = jnp.broadcast_to(
        jnp.repeat(sinks.reshape(KV_HEADS, ATTN_GROUP).T.reshape(-1), tpad)[:, None],
        (ATTN_HEADS * tpad, LANES))
    lanes = np.arange(LANES)
    bd_s = jnp.asarray(lanes[:, None] // ATTN_HEAD_DIM == lanes[None, :] // ATTN_HEAD_DIM, dtype=bf16)
    cos_s, sin_s = _rope_tables(np.arange(tpad, dtype=np.float32) + np.float32(PAST_LEN))
    cos_s = np.tile(cos_s, (SAMPLE_BATCH_BLOCK, 1))
    sin_s = np.tile(sin_s, (SAMPLE_BATCH_BLOCK, 1))
    x_pad = jnp.pad(x_sample, ((0, 0), (0, tpad - nt), (0, 0)))
    ckt = cache_k[0].reshape(nb, WINDOW, KV_WIDTH).transpose(0, 2, 1)
    cvt = cache_v[0].reshape(nb, WINDOW, KV_WIDTH).transpose(0, 2, 1)
    x1_pad, nkt, nvt, c_new, n_new, m_pad = _sample_mixer(
        x_pad, ckt, cvt, state_C[0], state_n[0], state_m[0][:, :, None], cos_s, sin_s, wq_s, w, wgate,
        wout_s, sink_tile, bd_s, anorm, qg, kg, gbias, mnorm)
    y_p, y_s = _ffn(x1_p, x1_pad[:, :nt].reshape(nb * nt, D_MODEL), fnorm, w_gate[0], w_up[0], w_down[0])
    m_new = m_pad[:, :nh, 0]

    new_k_s = nkt.transpose(0, 2, 1)
    new_v_s = nvt.transpose(0, 2, 1)

    kv_shape = (1, 1, WINDOW, KV_HEADS, ATTN_HEAD_DIM)
    return (
        y_p[None],
        y_s.reshape(nb, nt, D_MODEL),
        k_p.reshape(kv_shape),
        v_p.reshape(kv_shape),
        cext_p[None, None, :, :, :MLSTM_HEAD_DIM],
        cext_p[None, None, :, :, MLSTM_HEAD_DIM],
        m_p[None, None, :nh, 0],
        new_k_s.reshape(1, nb, WINDOW, KV_HEADS, ATTN_HEAD_DIM),
        new_v_s.reshape(1, nb, WINDOW, KV_HEADS, ATTN_HEAD_DIM),
        c_new.reshape(1, nb, nh, MLSTM_HEAD_DIM, MLSTM_HEAD_DIM),
        n_new.reshape(1, nb, nh, MLSTM_HEAD_DIM),
        m_new.reshape(1, nb, nh),
    )
```

```python
import jax
import jax.numpy as jnp
import numpy as np
from jax import lax
from jax.experimental import pallas as pl
from jax.experimental.pallas import tpu as pltpu

D_MODEL = 1024
PAST_LEN = 16384
ATTN_HEADS = 8
KV_HEADS = 2
ATTN_HEAD_DIM = 64
ATTN_GROUP = ATTN_HEADS // KV_HEADS
ATTN_WIDTH = ATTN_HEADS * ATTN_HEAD_DIM
KV_WIDTH = KV_HEADS * ATTN_HEAD_DIM
WINDOW = 128
ROPE_THETA = 10000.0
MLSTM_HEADS = 4
MLSTM_HEAD_DIM = 128
MLSTM_WIDTH = MLSTM_HEADS * MLSTM_HEAD_DIM
MIX_WIDTH = ATTN_WIDTH + MLSTM_WIDTH
D_FF = 2816
NORM_EPS = 1e-6

LANES = 128
SUBLANES = 8
VMEM_LIMIT = 56 * 1024 * 1024

COL_QA = 0
COL_KA = COL_QA + ATTN_WIDTH
COL_VA = COL_KA + KV_WIDTH
COL_QM = COL_VA + KV_WIDTH
COL_KM = COL_QM + MLSTM_WIDTH
COL_VM = COL_KM + MLSTM_WIDTH
COL_OM = COL_VM + MLSTM_WIDTH
COL_G = COL_OM + MLSTM_WIDTH
IN_WIDTH = COL_G + 2 * MLSTM_HEADS
FG_LANE = SUBLANES

PROMPT_BLOCK = 256
QBLOCK = WINDOW
MCHUNK = 128
PROJ_CHUNK = 256
FFN_BLOCK = 512
FFN_CHUNK = 256
SAMPLE_BATCH_BLOCK = 16
SAMPLE_TOKENS = 4
NEG = -1e30

f32 = jnp.float32
bf16 = jnp.bfloat16


def _rms(x, gain):
    return x * lax.rsqrt(jnp.mean(x * x, axis=-1, keepdims=True) + NORM_EPS) * gain


def _segsum64(s, lane):
    for k in (1, 2, 4, 8, 16, 32):
        s = s + jnp.where((lane & k) != 0, pltpu.roll(s, k, 1), pltpu.roll(s, LANES - k, 1))
    return s


def _headnorm_rope(xs, gain, cos, sin_signed, lane):
    ss = _segsum64(xs * xs, lane)
    y = xs * lax.rsqrt(ss * (1.0 / ATTN_HEAD_DIM) + NORM_EPS) * gain
    partner = jnp.where((lane & 32) != 0, pltpu.roll(y, 32, 1), pltpu.roll(y, LANES - 32, 1))
    return y * cos + partner * sin_signed


def _group_sumsq(xs, bd_ref):
    x2 = xs * xs
    hi = x2.astype(bf16)
    lo = (x2 - hi.astype(f32)).astype(bf16)
    return (jnp.dot(hi, bd_ref[...], preferred_element_type=f32)
            + jnp.dot(lo, bd_ref[...], preferred_element_type=f32))


PA_Q = 0
PA_K = PA_Q + ATTN_WIDTH
PA_V = PA_K + KV_HEADS * LANES
PA_WIDTH = PA_V + KV_HEADS * LANES
QUARTER = ATTN_HEAD_DIM // 2
QA_Q = 0
QA_K = QA_Q + 2 * ATTN_WIDTH
QA_V = QA_K + KV_HEADS * LANES
QA_VZ = QA_V + KV_HEADS * LANES
QA_WIDTH = QA_VZ + KV_HEADS * LANES
ZM_WIDTH = 4 * MLSTM_WIDTH
ATTN_PHASES = 4
MLSTM_PHASES = 5


def _norm_rope_quarters(xs, ss, gain, cos, sin_signed):
    y = xs * lax.rsqrt(ss * (1.0 / ATTN_HEAD_DIM) + NORM_EPS) * gain
    return y * cos + pltpu.roll(y, LANES // 2, 1) * sin_signed


def _col_chunks(width):
    return [(c, min(c + PROJ_CHUNK, width)) for c in range(0, width, PROJ_CHUNK)]


def _mixer_proj_jobs(x_ref, anorm_ref, wa_ref, w_ref, wgate_ref, za_ref, zm_ref, gz_ref):
    h = _rms(x_ref[...], anorm_ref[...]).astype(bf16)

    def proj_job(w_src, wc0, z_ref, c0, c1):
        def run():
            z_ref[:, c0:c1] = jnp.dot(h, w_src[:, wc0 + c0:wc0 + c1], preferred_element_type=f32)
        return run

    return ([proj_job(wa_ref, 0, za_ref, c0, c1) for c0, c1 in _col_chunks(PA_WIDTH)]
            + [proj_job(w_ref, COL_QM, zm_ref, c0, c1) for c0, c1 in _col_chunks(ZM_WIDTH)]
            + [proj_job(wgate_ref, 0, gz_ref, 0, LANES)])


def _mixer_prep_jobs(za_ref, gz_ref, cos_ref, sin_ref, bd_ref, qg_ref, kg_ref, gbias_ref, qa_ref,
                     colf_ref, urow_ref, wrow_ref, mst_ref, mout_ref):
    tb = za_ref.shape[0]

    def gates_job():
        lane_t = lax.broadcasted_iota(jnp.int32, (tb, LANES), 1)
        gcol = gz_ref[...] + gbias_ref[...]
        acol = jnp.where(lane_t < FG_LANE, gcol, jax.nn.log_sigmoid(gcol))
        arow = acol.T
        lane8 = lax.broadcasted_iota(jnp.int32, (SUBLANES, LANES), 1)
        lane_in = lane8 & (MCHUNK - 1)
        m_prev = mst_ref[:, 0:1]
        stacks = []
        for sb in range(tb // LANES):
            ls = slice(sb * LANES, (sb + 1) * LANES)
            ig8 = arow[0:SUBLANES, ls]
            lf8 = arow[FG_LANE:FG_LANE + SUBLANES, ls]
            b8 = lf8
            k = 1
            while k < MCHUNK:
                b8 = b8 + jnp.where(lane_in >= k, pltpu.roll(b8, k, 1), 0.0)
                k *= 2
            u8 = ig8 - b8
            cm8 = u8
            k = 1
            while k < MCHUNK:
                cm8 = jnp.maximum(cm8, jnp.where(lane_in >= k, pltpu.roll(cm8, k, 1), NEG))
                k *= 2
            g8 = jnp.zeros_like(u8)
            mp8 = jnp.zeros_like(u8)
            gl8 = jnp.zeros_like(u8)
            for c in range(LANES // MCHUNK):
                in_chunk = (lane8 // MCHUNK) == c
                gc = jnp.maximum(cm8, m_prev)
                last = c * MCHUNK + MCHUNK - 1
                g_last = jnp.max(jnp.where(lane8 == last, gc, NEG), axis=1, keepdims=True)
                b_last = jnp.max(jnp.where(lane8 == last, b8, NEG), axis=1, keepdims=True)
                g8 = jnp.where(in_chunk, gc, g8)
                mp8 = jnp.where(in_chunk, m_prev, mp8)
                gl8 = jnp.where(in_chunk, g_last, gl8)
                m_prev = b_last + g_last
            a8 = jnp.exp(mp8 - g8)
            emt8 = jnp.exp(-(b8 + g8))
            aend8 = jnp.exp(mp8 - gl8)
            stacks.append(jnp.concatenate(
                [g8, a8, emt8, aend8, jnp.zeros((LANES - 4 * SUBLANES, LANES), f32)], axis=0))
            urow_ref[:, ls] = u8
            wrow_ref[:, ls] = jnp.exp(u8 - gl8)
        mst_ref[...] = jnp.broadcast_to(m_prev, mst_ref.shape)
        mout_ref[...] = jnp.broadcast_to(m_prev, mout_ref.shape)
        colf_ref[...] = jnp.concatenate(stacks, axis=1).T

    def prep_job(qb):
        def run():
            rows = slice(qb * QBLOCK, (qb + 1) * QBLOCK)
            lane = lax.broadcasted_iota(jnp.int32, (QBLOCK, LANES), 1)
            head_a = ((lane // QUARTER) & 1) == 0
            row0 = lax.broadcasted_iota(jnp.int32, (QBLOCK, LANES), 0) == 0
            cos = cos_ref[rows, :]
            sin = sin_ref[rows, :]
            ss = [_group_sumsq(za_ref[rows, d * 2 * LANES:(d + 1) * 2 * LANES], bd_ref)
                  for d in range(PA_V // (2 * LANES))]
            for j in range(PA_V // LANES):
                is_q = j < ATTN_WIDTH // LANES
                y = _norm_rope_quarters(za_ref[rows, j * LANES:(j + 1) * LANES],
                                        ss[j // 2][:, (j % 2) * LANES:(j % 2 + 1) * LANES],
                                        qg_ref[...] if is_q else kg_ref[...], cos, sin)
                if is_q:
                    y = y * (ATTN_HEAD_DIM ** -0.5)
                    qa_ref[rows, QA_Q + 2 * j * LANES:QA_Q + (2 * j + 1) * LANES] = (
                        jnp.where(head_a, y, 0.0).astype(bf16))
                    qa_ref[rows, QA_Q + (2 * j + 1) * LANES:QA_Q + (2 * j + 2) * LANES] = (
                        jnp.where(head_a, 0.0, y).astype(bf16))
                else:
                    c = j - ATTN_WIDTH // LANES
                    qa_ref[rows, QA_K + c * LANES:QA_K + (c + 1) * LANES] = y.astype(bf16)
            for c in range(KV_HEADS):
                v = za_ref[rows, PA_V + c * LANES:PA_V + (c + 1) * LANES]
                qa_ref[rows, QA_V + c * LANES:QA_V + (c + 1) * LANES] = v.astype(bf16)
                qa_ref[rows, QA_VZ + c * LANES:QA_VZ + (c + 1) * LANES] = (
                    jnp.where(row0, 0.0, v).astype(bf16))
        return run

    return [gates_job] + [prep_job(qb) for qb in range(tb // QBLOCK)]


def _mixer_outproj_jobs(xs_ref, mix_ref, wout_ref, x1_ref):
    def job(c0, c1):
        def run():
            x1_ref[:, c0:c1] = xs_ref[:, c0:c1] + jnp.dot(mix_ref[...], wout_ref[:, c0:c1],
                                                          preferred_element_type=f32)
        return run

    return [job(c0, c1) for c0, c1 in _col_chunks(D_MODEL)]


def _mixer_window_out(xs_ref, anorm_ref, w_ref, kgs_ref, coss_ref, sins_ref, kout_ref, vout_ref):
    tb = xs_ref.shape[0]
    h = _rms(xs_ref[tb - WINDOW:, :], anorm_ref[...]).astype(bf16)
    zs = jnp.dot(h, w_ref[:, COL_KA:COL_QM], preferred_element_type=f32)
    lane_s = lax.broadcasted_iota(jnp.int32, (WINDOW, LANES), 1)
    kout_ref[...] = _headnorm_rope(zs[:, :KV_WIDTH], kgs_ref[...], coss_ref[...], sins_ref[...], lane_s)
    vout_ref[...] = zs[:, KV_WIDTH:]


def _mixer_core(first_block, last_block, fillers, prep_jobs, qa_ref, zm_ref, colf_ref, urow_ref, wrow_ref,
                mix_ref, bias_ref, mnorm_ref, kprev_ref, vprev_ref, cst_ref, cext_ref):
    tb = qa_ref.shape[0]
    nqb = tb // QBLOCK
    nch = tb // MCHUNK
    fillers = list(fillers)
    n_fill = len(fillers)
    slots = len(prep_jobs) + nqb * ATTN_PHASES + nch * MLSTM_PHASES
    progress = [0]

    def fill():
        progress[0] += 1
        while n_fill - len(fillers) < min(n_fill, -(-n_fill * progress[0] // slots)):
            fillers.pop(0)()

    for job in prep_jobs:
        job()
        fill()

    low_half = lax.broadcasted_iota(jnp.int32, (QBLOCK, LANES), 1) < ATTN_HEAD_DIM
    qi = lax.broadcasted_iota(jnp.int32, (ATTN_GROUP * QBLOCK, 2 * QBLOCK), 0) & (QBLOCK - 1)
    kj = lax.broadcasted_iota(jnp.int32, (ATTN_GROUP * QBLOCK, 2 * QBLOCK), 1)
    band = (kj > qi) & (kj <= qi + QBLOCK)
    ones_slab = jnp.ones((2 * QBLOCK, LANES), bf16)

    def attn_phases(qb):
        rows = slice(qb * QBLOCK, (qb + 1) * QBLOCK)
        heads = range(KV_HEADS)
        st = {}

        def scores():
            for c in heads:
                kcols = slice(QA_K + c * LANES, QA_K + (c + 1) * LANES)
                if qb == 0:
                    kprev, vprev = kprev_ref[c], vprev_ref[c]
                else:
                    prev = slice((qb - 1) * QBLOCK, qb * QBLOCK)
                    kprev = qa_ref[prev, kcols]
                    vprev = qa_ref[prev, QA_VZ + c * LANES:QA_VZ + (c + 1) * LANES]
                kcat = jnp.concatenate([kprev, qa_ref[rows, kcols]], axis=0)
                vcat = jnp.concatenate(
                    [vprev, qa_ref[rows, QA_V + c * LANES:QA_V + (c + 1) * LANES]], axis=0)
                st['vext', c] = jnp.concatenate([vcat, ones_slab], axis=1)
                q0 = QA_Q + c * ATTN_GROUP * LANES
                qst = jnp.concatenate([qa_ref[rows, q0 + g * LANES:q0 + (g + 1) * LANES]
                                       for g in range(ATTN_GROUP)], axis=0)
                st['s', c] = lax.dot_general(qst, kcat, (((1,), (1,)), ((), ())),
                                             preferred_element_type=f32)

        def softmax():
            valid = band & (kj >= QBLOCK) if (first_block and qb == 0) else band
            for c in heads:
                s = jnp.where(valid, st.pop(('s', c)), bias_ref[c])
                st['p', c] = jnp.exp(s - jnp.max(s, axis=-1, keepdims=True)).astype(bf16)

        def values():
            for c in heads:
                st['of', c] = jnp.dot(st.pop(('p', c)), st.pop(('vext', c)),
                                      preferred_element_type=f32)

        def normalise():
            for c in heads:
                of = st.pop(('of', c))
                o = of[:, :LANES] / of[:, LANES:]
                for jj in range(2):
                    pair = jnp.where(low_half, o[(2 * jj) * QBLOCK:(2 * jj + 1) * QBLOCK],
                                     o[(2 * jj + 1) * QBLOCK:(2 * jj + 2) * QBLOCK])
                    col = (2 * c + jj) * LANES
                    mix_ref[rows, col:col + LANES] = pair.astype(bf16)

        return [scores, softmax, values, normalise]

    ti = lax.broadcasted_iota(jnp.int32, (MCHUNK, MCHUNK), 0)
    si = lax.broadcasted_iota(jnp.int32, (MCHUNK, MCHUNK), 1)
    causal = si <= ti
    ones_l = jnp.ones((MCHUNK, LANES), bf16)

    cexts = [cst_ref[hd] for hd in range(MLSTM_HEADS)]

    def mlstm_phases(c):
        rows = slice(c * MCHUNK, (c + 1) * MCHUNK)
        heads = range(MLSTM_HEADS)
        hcols = lambda k, hd: slice((k * MLSTM_HEADS + hd) * MLSTM_HEAD_DIM,
                                    (k * MLSTM_HEADS + hd + 1) * MLSTM_HEAD_DIM)
        st = {}

        def scores():
            for hd in heads:
                st['q', hd] = zm_ref[rows, hcols(0, hd)].astype(bf16)
                st['k', hd] = zm_ref[rows, hcols(1, hd)] * (MLSTM_HEAD_DIM ** -0.5)
                st['s', hd] = lax.dot_general(st['q', hd], st['k', hd].astype(bf16),
                                              (((1,), (1,)), ((), ())), preferred_element_type=f32)

        def decay():
            for hd in heads:
                g_c = colf_ref[rows, hd:hd + 1]
                u_r = urow_ref[hd:hd + 1, rows]
                dmat = jnp.exp(jnp.where(causal, u_r - g_c, NEG))
                st['s', hd] = (st['s', hd] * dmat).astype(bf16)

        def readout():
            for hd in heads:
                a_c = colf_ref[rows, SUBLANES + hd:SUBLANES + hd + 1]
                vext = jnp.concatenate([zm_ref[rows, hcols(2, hd)].astype(bf16), ones_l], axis=1)
                st['v', hd] = vext
                st['nd', hd] = (
                    a_c * jnp.dot(st.pop(('q', hd)), cexts[hd].astype(bf16), preferred_element_type=f32)
                    + jnp.dot(st.pop(('s', hd)), vext, preferred_element_type=f32))

        def emit():
            for hd in heads:
                nd = st.pop(('nd', hd))
                emt_c = colf_ref[rows, 2 * SUBLANES + hd:2 * SUBLANES + hd + 1]
                hraw = nd[:, :MLSTM_HEAD_DIM] / jnp.maximum(jnp.abs(nd[:, MLSTM_HEAD_DIM:]), emt_c)
                hn = _rms(hraw, mnorm_ref[hd:hd + 1, :])
                og = zm_ref[rows, hcols(3, hd)]
                mix_ref[rows, ATTN_WIDTH + hd * MLSTM_HEAD_DIM:ATTN_WIDTH + (hd + 1) * MLSTM_HEAD_DIM] = (
                    (hn * jax.nn.sigmoid(og)).astype(bf16))

        def update():
            for hd in heads:
                aend = colf_ref[c * MCHUNK:c * MCHUNK + 1, 3 * SUBLANES + hd:3 * SUBLANES + hd + 1]
                w_r = wrow_ref[hd:hd + 1, rows]
                kw_t = (st.pop(('k', hd)).T * w_r).astype(bf16)
                cexts[hd] = aend * cexts[hd] + jnp.dot(kw_t, st.pop(('v', hd)),
                                                       preferred_element_type=f32)

        return [scores, decay, readout, emit, update]

    streams = ([p for qb in range(nqb) for p in attn_phases(qb)],
               [p for c in range(nch) for p in mlstm_phases(c)])
    assert [len(s) for s in streams] == [nqb * ATTN_PHASES, nch * MLSTM_PHASES]
    for i in range(max(len(s) for s in streams)):
        for s in streams:
            if i < len(s):
                s[i]()
                fill()
    last_rows = slice(tb - QBLOCK, tb)
    for c in range(KV_HEADS):
        kprev_ref[c] = qa_ref[last_rows, QA_K + c * LANES:QA_K + (c + 1) * LANES]
        vprev_ref[c] = qa_ref[last_rows, QA_VZ + c * LANES:QA_VZ + (c + 1) * LANES]
    for hd in range(MLSTM_HEADS):
        cst_ref[hd] = cexts[hd]
        if last_block:
            cext_ref[hd] = cexts[hd]


def _prompt_mixer_kernel(x_ref, cos_ref, sin_ref, wa_ref, w_ref, wgate_ref, wout_ref, bd_ref, bias_ref,
                         anorm_ref, qg_ref, kg_ref, kgs_ref, coss_ref, sins_ref, gbias_ref, mnorm_ref,
                         x1_ref, kout_ref, vout_ref, cext_ref, mout_ref,
                         za0, za1, zm0, zm1, gz0, gz1, mix0, mix1, xs0, xs1,
                         qa_ref, colf_ref, urow_ref, wrow_ref, kprev_ref, vprev_ref, cst_ref, mst_ref):
    step = pl.program_id(0)
    nblk = pl.num_programs(0) - 2
    za, zm, gz, mix, xs = (za0, za1), (zm0, zm1), (gz0, gz1), (mix0, mix1), (xs0, xs1)

    def run(parity, do_in, do_core, do_out, first_block=False, last_block=False):
        other = 1 - parity
        jobs = []
        if do_out:
            jobs += _mixer_outproj_jobs(xs[parity], mix[parity], wout_ref, x1_ref)
        if do_in:
            jobs += _mixer_proj_jobs(x_ref, anorm_ref, wa_ref, w_ref, wgate_ref, za[parity], zm[parity],
                                     gz[parity])
        if do_core:
            prep = _mixer_prep_jobs(za[other], gz[other], cos_ref, sin_ref, bd_ref, qg_ref, kg_ref,
                                    gbias_ref, qa_ref, colf_ref, urow_ref, wrow_ref, mst_ref, mout_ref)
            _mixer_core(first_block, last_block, jobs, prep, qa_ref, zm[other], colf_ref, urow_ref,
                        wrow_ref, mix[other], bias_ref, mnorm_ref, kprev_ref, vprev_ref, cst_ref, cext_ref)
        else:
            for job in jobs:
                job()
        if last_block:
            _mixer_window_out(xs[other], anorm_ref, w_ref, kgs_ref, coss_ref, sins_ref, kout_ref, vout_ref)
        if do_in:
            xs[parity][...] = x_ref[...]

    @pl.when(step == 0)
    def _first():
        kprev_ref[...] = jnp.zeros_like(kprev_ref)
        vprev_ref[...] = jnp.zeros_like(vprev_ref)
        cst_ref[...] = jnp.zeros_like(cst_ref)
        mst_ref[...] = jnp.zeros_like(mst_ref)
        run(0, True, False, False)

    @pl.when(step == 1)
    def _second():
        run(1, True, True, False, first_block=True)

    steady = (step >= 2) & (step < nblk)

    @pl.when(steady & (step % 2 == 0))
    def _even():
        run(0, True, True, True)

    @pl.when(steady & (step % 2 == 1))
    def _odd():
        run(1, True, True, True)

    @pl.when(step == nblk)
    def _drain_core():
        run(0, False, True, True, last_block=True)

    @pl.when(step == nblk + 1)
    def _drain_out():
        run(1, False, False, True)


def _const_spec(shape, single=False):
    nd = len(shape)
    if single:
        return pl.BlockSpec(shape, lambda i, *_: (0,) * nd, pipeline_mode=pl.Buffered(1))
    return pl.BlockSpec(shape, lambda i, *_: (0,) * nd)


def _prompt_mixer(x, cos, sin, wa, w, wgate, wout_b, bd, bias, anorm, qg, kg, kgs, coss, sins, gbias, mnorm):
    t = x.shape[0]
    tb = PROMPT_BLOCK
    nblk = t // tb
    assert nblk % 2 == 0 and nblk >= 4
    state_shape = (MLSTM_HEADS, MLSTM_HEAD_DIM, 2 * MLSTM_HEAD_DIM)
    last = nblk - 1
    lag = lambda d: (lambda i: (jnp.clip(i - d, 0, last), 0))
    return pl.pallas_call(
        _prompt_mixer_kernel,
        grid=(nblk + 2,),
        in_specs=[
            pl.BlockSpec((tb, D_MODEL), lag(0)),
            pl.BlockSpec((tb, LANES), lag(1)),
            pl.BlockSpec((tb, LANES), lag(1)),
            _const_spec((D_MODEL, PA_WIDTH), single=True),
            _const_spec((D_MODEL, IN_WIDTH), single=True),
            _const_spec((D_MODEL, LANES), single=True),
            _const_spec((MIX_WIDTH, D_MODEL), single=True),
            _const_spec((2 * LANES, 2 * LANES), single=True),
            _const_spec((KV_HEADS, ATTN_GROUP * QBLOCK, 2 * QBLOCK), single=True),
            _const_spec((1, D_MODEL)),
            _const_spec((1, LANES)),
            _const_spec((1, LANES)),
            _const_spec((1, LANES)),
            _const_spec((WINDOW, LANES)),
            _const_spec((WINDOW, LANES)),
            _const_spec((1, LANES)),
            _const_spec((MLSTM_HEADS, MLSTM_HEAD_DIM)),
        ],
        out_specs=[
            pl.BlockSpec((tb, D_MODEL), lag(2)),
            _const_spec((WINDOW, KV_WIDTH)),
            _const_spec((WINDOW, KV_WIDTH)),
            _const_spec(state_shape),
            _const_spec((SUBLANES, LANES)),
        ],
        out_shape=[
            jax.ShapeDtypeStruct((t, D_MODEL), f32),
            jax.ShapeDtypeStruct((WINDOW, KV_WIDTH), f32),
            jax.ShapeDtypeStruct((WINDOW, KV_WIDTH), f32),
            jax.ShapeDtypeStruct(state_shape, f32),
            jax.ShapeDtypeStruct((SUBLANES, LANES), f32),
        ],
        scratch_shapes=(
            [pltpu.VMEM((tb, PA_WIDTH), f32)] * 2 + [pltpu.VMEM((tb, ZM_WIDTH), f32)] * 2
            + [pltpu.VMEM((tb, LANES), f32)] * 2
            + [pltpu.VMEM((tb, MIX_WIDTH), bf16)] * 2 + [pltpu.VMEM((tb, D_MODEL), f32)] * 2
            + [pltpu.VMEM((tb, QA_WIDTH), bf16), pltpu.VMEM((tb, LANES), f32)]
            + [pltpu.VMEM((SUBLANES, tb), f32)] * 2
            + [pltpu.VMEM((KV_HEADS, WINDOW, LANES), bf16)] * 2
            + [pltpu.VMEM(state_shape, f32), pltpu.VMEM((SUBLANES, LANES), f32)]),
        compiler_params=pltpu.CompilerParams(
            dimension_semantics=("arbitrary",), vmem_limit_bytes=VMEM_LIMIT),
        name="prompt_mixer",
    )(x, cos, sin, wa, w, wgate, wout_b, bd, bias, anorm, qg, kg, kgs, coss, sins, gbias, mnorm)


def _ffn_kernel(xp_ref, xs_ref, g_ref, wg_ref, wu_ref, wd_ref, op_ref, os_ref):
    step = pl.program_id(0)
    last = pl.num_programs(0) - 1

    @pl.when(step < last)
    def _prompt_rows():
        _ffn_rows(xp_ref, g_ref, wg_ref, wu_ref, wd_ref, op_ref)

    @pl.when(step == last)
    def _sample_rows():
        _ffn_rows(xs_ref, g_ref, wg_ref, wu_ref, wd_ref, os_ref)


def _ffn_rows(x_ref, g_ref, wg_ref, wu_ref, wd_ref, o_ref):
    x = x_ref[...]
    hf = _rms(x, g_ref[...]).astype(bf16)
    acc = x
    for c in range(D_FF // FFN_CHUNK):
        cs = slice(c * FFN_CHUNK, (c + 1) * FFN_CHUNK)
        gate = jnp.dot(hf, wg_ref[:, cs], preferred_element_type=f32)
        up = jnp.dot(hf, wu_ref[:, cs], preferred_element_type=f32)
        act = (gate * jax.nn.sigmoid(gate) * up).astype(bf16)
        acc = acc + jnp.dot(act, wd_ref[cs, :], preferred_element_type=f32)
    o_ref[...] = acc


def _ffn(x_p, x_s, fnorm, wg_b, wu_b, wd_b):
    n = x_p.shape[0]
    ns = x_s.shape[0]
    tm = FFN_BLOCK
    last = n // tm - 1
    return pl.pallas_call(
        _ffn_kernel,
        grid=(n // tm + 1,),
        in_specs=[
            pl.BlockSpec((tm, D_MODEL), lambda i: (jnp.minimum(i, last), 0)),
            _const_spec((ns, D_MODEL), single=True),
            _const_spec((1, D_MODEL)),
            _const_spec((D_MODEL, D_FF), single=True),
            _const_spec((D_MODEL, D_FF), single=True),
            _const_spec((D_FF, D_MODEL), single=True),
        ],
        out_specs=[pl.BlockSpec((tm, D_MODEL), lambda i: (jnp.minimum(i, last), 0)),
                   _const_spec((ns, D_MODEL))],
        out_shape=[jax.ShapeDtypeStruct((n, D_MODEL), f32), jax.ShapeDtypeStruct((ns, D_MODEL), f32)],
        compiler_params=pltpu.CompilerParams(
            dimension_semantics=("arbitrary",), vmem_limit_bytes=VMEM_LIMIT),
        name="ffn",
    )(x_p, x_s, fnorm, wg_b, wu_b, wd_b)


def _sample_mixer_kernel(x_ref, ckt_ref, cvt_ref, c_ref, n_ref, m_ref, cos_ref, sin_ref, wqs_ref, w_ref,
                         wgate_ref, wout_ref, sink_ref, bd_ref, anorm_ref, qg_ref, kg_ref, gbias_ref,
                         mnorm_ref, x1_ref, nkt_ref, nvt_ref, cn_ref, nn_ref, mn_ref):
    bb, tpad, _ = x_ref.shape
    nrows = bb * tpad
    nreal = SAMPLE_TOKENS
    h = _rms(x_ref[...].reshape(nrows, D_MODEL), anorm_ref[...]).astype(bf16)
    z = jnp.concatenate(
        [jnp.dot(h, wqs_ref[...], preferred_element_type=f32),
         jnp.dot(h, w_ref[:, COL_KA:COL_G], preferred_element_type=f32),
         jnp.dot(h, wgate_ref[...], preferred_element_type=f32)], axis=1)
    lane = lax.broadcasted_iota(jnp.int32, (nrows, LANES), 1)
    low = lane < ATTN_HEAD_DIM
    cos = cos_ref[...]
    sin = sin_ref[...]

    def per_seq(a):
        return a.reshape(bb, tpad, a.shape[-1])

    def norm_rope(xs, gain):
        y = xs * lax.rsqrt(_group_sumsq(xs, bd_ref) * (1.0 / ATTN_HEAD_DIM) + NORM_EPS) * gain
        partner = jnp.where((lane & QUARTER) != 0, pltpu.roll(y, QUARTER, 1),
                            pltpu.roll(y, LANES - QUARTER, 1))
        return y * cos + partner * sin

    q_rows = []
    for j in range(ATTN_GROUP):
        qs = norm_rope(z[:, COL_QA + j * LANES:COL_QA + (j + 1) * LANES], qg_ref[...])
        qs = qs * (ATTN_HEAD_DIM ** -0.5)
        q_rows.append(per_seq(jnp.where(low, qs, 0.0)).astype(bf16))
        q_rows.append(per_seq(jnp.where(low, 0.0, qs)).astype(bf16))
    qbd = jnp.concatenate(q_rows, axis=1)
    knew = norm_rope(z[:, COL_KA:COL_KA + KV_WIDTH], kg_ref[...])
    vnew = z[:, COL_VA:COL_VA + KV_WIDTH]
    zpad = jnp.zeros((bb, LANES - tpad, LANES), bf16)
    knp = jnp.concatenate([per_seq(knew).astype(bf16), zpad], axis=1)
    vnp = jnp.concatenate([per_seq(vnew).astype(bf16), zpad], axis=1)
    ckt = ckt_ref[...]
    cvt = cvt_ref[...]
    s = jnp.concatenate(
        [jnp.einsum('bqd,bdw->bqw', qbd, ckt.astype(bf16), preferred_element_type=f32),
         jnp.einsum('bqd,bkd->bqk', qbd, knp, preferred_element_type=f32)], axis=2)
    tq = lax.broadcasted_iota(jnp.int32, s.shape, 1) & (tpad - 1)
    kj = lax.broadcasted_iota(jnp.int32, s.shape, 2)
    valid = ((kj < WINDOW) & (kj > tq)) | ((kj >= WINDOW) & (kj - WINDOW <= tq) & (kj - WINDOW < nreal))
    s = jnp.where(valid, s, NEG)
    sink = sink_ref[:, 0:1][None]
    mx = jnp.maximum(jnp.max(s, axis=-1, keepdims=True), sink)
    p = jnp.exp(s - mx)
    den = jnp.sum(p, axis=-1, keepdims=True) + jnp.exp(sink - mx)
    pb = p.astype(bf16)
    o = (jnp.einsum('bqw,bdw->bqd', pb[:, :, :WINDOW], cvt.astype(bf16), preferred_element_type=f32)
         + jnp.einsum('bqk,bkd->bqd', pb[:, :, WINDOW:], vnp, preferred_element_type=f32)) / den
    low3 = lax.broadcasted_iota(jnp.int32, (bb, tpad, LANES), 2) < ATTN_HEAD_DIM
    mix_parts = []
    for j in range(ATTN_GROUP):
        r0 = 2 * j * tpad
        pair = jnp.where(low3, o[:, r0:r0 + tpad, :], o[:, r0 + tpad:r0 + 2 * tpad, :])
        mix_parts.append(pair.reshape(nrows, LANES).astype(bf16))

    keep = lax.broadcasted_iota(jnp.int32, (KV_WIDTH, WINDOW), 1) < WINDOW - nreal
    knt = knew.T
    vnt = vnew.T
    for b in range(bb):
        shift = (WINDOW - nreal - b * tpad) % LANES
        nkt_ref[b] = jnp.where(keep, pltpu.roll(ckt_ref[b], WINDOW - nreal, 1), pltpu.roll(knt, shift, 1))
        nvt_ref[b] = jnp.where(keep, pltpu.roll(cvt_ref[b], WINDOW - nreal, 1), pltpu.roll(vnt, shift, 1))

    gz = per_seq(z[:, COL_G:COL_G + LANES] + gbias_ref[...])
    lgz = jax.nn.log_sigmoid(gz)
    trow = lax.broadcasted_iota(jnp.int32, (bb, tpad, 1), 1)
    real = trow < nreal
    mn_ref[...] = jnp.zeros_like(mn_ref)
    for hd in range(MLSTM_HEADS):
        hcols = lambda base: slice(base + hd * MLSTM_HEAD_DIM, base + (hd + 1) * MLSTM_HEAD_DIM)
        q = per_seq(z[:, hcols(COL_QM)])
        k = per_seq(z[:, hcols(COL_KM)]) * (MLSTM_HEAD_DIM ** -0.5)
        v = per_seq(z[:, hcols(COL_VM)])
        og = per_seq(z[:, hcols(COL_OM)])
        c0 = c_ref[:, hd]
        n0 = n_ref[:, hd:hd + 1, :]
        m0 = m_ref[:, hd:hd + 1, :]
        ig_c = jnp.where(real, gz[:, :, hd:hd + 1], NEG)
        lf_c = jnp.where(real, lgz[:, :, FG_LANE + hd:FG_LANE + hd + 1], 0.0)
        b_c = jnp.zeros_like(lf_c)
        for sx in range(nreal):
            b_c = b_c + jnp.where(trow >= sx, lf_c[:, sx:sx + 1, :], 0.0)
        dlog = [jnp.where(trow >= sx, b_c - b_c[:, sx:sx + 1, :] + ig_c[:, sx:sx + 1, :], NEG)
                for sx in range(nreal)]
        inter = b_c + m0
        m_t = inter
        for sx in range(nreal):
            m_t = jnp.maximum(m_t, dlog[sx])
        a = jnp.exp(inter - m_t)
        qc = jnp.einsum('btd,bde->bte', q.astype(bf16), c0.astype(bf16), preferred_element_type=f32)
        num = a * qc
        den_m = a * jnp.sum(q * n0, axis=2, keepdims=True)
        for sx in range(nreal):
            sd = jnp.sum(q * k[:, sx:sx + 1, :], axis=2, keepdims=True) * jnp.exp(dlog[sx] - m_t)
            num = num + sd * v[:, sx:sx + 1, :]
            den_m = den_m + sd
        hraw = num / jnp.maximum(jnp.abs(den_m), jnp.exp(-m_t))
        hn = _rms(hraw, mnorm_ref[hd:hd + 1, :][None])
        mix_parts.append((hn * jax.nn.sigmoid(og)).reshape(nrows, MLSTM_HEAD_DIM).astype(bf16))
        last = nreal - 1
        m_new = m_t[:, last:last + 1, :]
        b_last = b_c[:, last:last + 1, :]
        a_end = jnp.exp(b_last + m0 - m_new)
        kw = k * jnp.exp(b_last - b_c + ig_c - m_new)
        cn_ref[:, hd] = a_end * c0 + jnp.einsum('bsd,bse->bde', kw.astype(bf16), v.astype(bf16),
                                               preferred_element_type=f32)
        nn_ref[:, hd:hd + 1, :] = a_end * n0 + jnp.sum(kw, axis=1, keepdims=True)
        mn_ref[:, hd:hd + 1, :] = jnp.broadcast_to(m_new, (bb, 1, LANES))

    mix = jnp.concatenate(mix_parts, axis=1)
    x1 = x_ref[...].reshape(nrows, D_MODEL) + jnp.dot(mix, wout_ref[...], preferred_element_type=f32)
    x1_ref[...] = x1.reshape(bb, tpad, D_MODEL)


def _sample_mixer(x_pad, ckt, cvt, c0, n0, m0, cos, sin, wq_s, w, wgate, wout_s, sink_tile, bd, anorm, qg,
                  kg, gbias, mnorm):
    nb, tpad, _ = x_pad.shape
    bb = SAMPLE_BATCH_BLOCK
    nh = MLSTM_HEADS
    blk = lambda shape: pl.BlockSpec(shape, lambda i: (i,) + (0,) * (len(shape) - 1))
    cblk = (bb, nh, MLSTM_HEAD_DIM, MLSTM_HEAD_DIM)
    return pl.pallas_call(
        _sample_mixer_kernel,
        grid=(nb // bb,),
        in_specs=[blk((bb, tpad, D_MODEL)), blk((bb, KV_WIDTH, WINDOW)), blk((bb, KV_WIDTH, WINDOW)),
                  blk(cblk), blk((bb, nh, MLSTM_HEAD_DIM)), blk((bb, nh, 1)),
                  _const_spec((bb * tpad, LANES)), _const_spec((bb * tpad, LANES)),
                  _const_spec((D_MODEL, ATTN_WIDTH), single=True),
                  _const_spec((D_MODEL, IN_WIDTH), single=True),
                  _const_spec((D_MODEL, LANES), single=True),
                  _const_spec((MIX_WIDTH, D_MODEL), single=True),
                  _const_spec((ATTN_HEADS * tpad, LANES)), _const_spec((LANES, LANES)),
                  _const_spec((1, D_MODEL)),
                  _const_spec((1, LANES)), _const_spec((1, LANES)), _const_spec((1, LANES)),
                  _const_spec((nh, MLSTM_HEAD_DIM))],
        out_specs=[blk((bb, tpad, D_MODEL)), blk((bb, KV_WIDTH, WINDOW)), blk((bb, KV_WIDTH, WINDOW)),
                   blk(cblk), blk((bb, nh, MLSTM_HEAD_DIM)), blk((bb, tpad, LANES))],
        out_shape=[jax.ShapeDtypeStruct((nb, tpad, D_MODEL), f32),
                   jax.ShapeDtypeStruct((nb, KV_WIDTH, WINDOW), f32),
                   jax.ShapeDtypeStruct((nb, KV_WIDTH, WINDOW), f32),
                   jax.ShapeDtypeStruct((nb,) + cblk[1:], f32),
                   jax.ShapeDtypeStruct((nb, nh, MLSTM_HEAD_DIM), f32),
                   jax.ShapeDtypeStruct((nb, tpad, LANES), f32)],
        compiler_params=pltpu.CompilerParams(
            dimension_semantics=("arbitrary",), vmem_limit_bytes=VMEM_LIMIT),
        name="sample_mixer",
    )(x_pad, ckt, cvt, c0, n0, m0, cos, sin, wq_s, w, wgate, wout_s, sink_tile, bd, anorm, qg, kg, gbias,
      mnorm)


def _rope_angles(pos):
    half = ATTN_HEAD_DIM // 2
    inv = ROPE_THETA ** (-np.arange(half, dtype=np.float64) / half)
    ang = pos.astype(np.float64)[:, None] * inv[None, :]
    return np.cos(ang).astype(np.float32), np.sin(ang).astype(np.float32)


def _rope_tables(pos):
    c, s = _rope_angles(pos)
    cos = np.tile(c, (1, LANES // QUARTER))
    sin = np.tile(np.concatenate([-s, s], axis=1), (1, LANES // ATTN_HEAD_DIM))
    return cos, sin


def _rope_tables_quarters(pos):
    c, s = _rope_angles(pos)
    return np.tile(c, (1, LANES // QUARTER)), np.concatenate([-s, -s, s, s], axis=1)


def _quarters(a):
    lo, hi = a[..., :QUARTER], a[..., QUARTER:]
    return jnp.concatenate([lo, lo, hi, hi], axis=-1)


def _prompt_attn_weights(w):
    d = w.shape[0]
    wq = w[:, COL_QA:COL_KA].reshape(d, ATTN_WIDTH // LANES, 2, 2, QUARTER)
    wq = wq.transpose(0, 1, 3, 2, 4).reshape(d, ATTN_WIDTH)
    wk = _quarters(w[:, COL_KA:COL_VA].reshape(d, KV_HEADS, ATTN_HEAD_DIM)).reshape(d, KV_HEADS * LANES)
    wv = w[:, COL_VA:COL_QM].reshape(d, KV_HEADS, 1, ATTN_HEAD_DIM)
    wv = jnp.broadcast_to(wv, (d, KV_HEADS, 2, ATTN_HEAD_DIM)).reshape(d, KV_HEADS * LANES)
    return jnp.concatenate([wq, wk, wv], axis=1)


def kernel(x_prompt, x_sample, cache_k, cache_v, state_C, state_n, state_m, attn_norm, w_in, q_norm,
           k_norm, attn_sinks, b_ig, b_fg, mlstm_norm, w_out, ffn_norm, w_gate, w_up, w_down):
    assert w_in.shape[0] == 1 and x_prompt.shape[0] == 1
    tp = x_prompt.shape[1]
    nb, nt = x_sample.shape[0], x_sample.shape[1]
    assert nt == SAMPLE_TOKENS
    tpad = SUBLANES
    nh = MLSTM_HEADS

    w = w_in[0].astype(bf16)
    pad_a = jnp.zeros((D_MODEL, FG_LANE - nh), bf16)
    pad_b = jnp.zeros((D_MODEL, LANES - FG_LANE - nh), bf16)
    wgate = jnp.concatenate([w[:, COL_G:COL_G + nh], pad_a, w[:, COL_G + nh:], pad_b], axis=1)
    gbias = jnp.concatenate(
        [b_ig[0], jnp.zeros((FG_LANE - nh,), f32), b_fg[0], jnp.zeros((LANES - FG_LANE - nh,), f32)]
    ).reshape(1, LANES)
    wout_b = w_out[0].astype(bf16)
    wg_b = w_gate[0].astype(bf16)
    wu_b = w_up[0].astype(bf16)
    wd_b = w_down[0].astype(bf16)
    anorm = attn_norm[0].reshape(1, D_MODEL)
    fnorm = ffn_norm[0].reshape(1, D_MODEL)
    qg = jnp.tile(q_norm[0], LANES // ATTN_HEAD_DIM).reshape(1, LANES)
    kg = jnp.tile(k_norm[0], LANES // ATTN_HEAD_DIM).reshape(1, LANES)
    mnorm = mlstm_norm[0].reshape(nh, MLSTM_HEAD_DIM)
    sinks = attn_sinks[0]

    wa = _prompt_attn_weights(w)
    idx = np.arange(2 * LANES)
    same = (idx[:, None] // LANES == idx[None, :] // LANES) & (
        (idx[:, None] // QUARTER) % 2 == (idx[None, :] // QUARTER) % 2)
    bd = jnp.asarray(same, dtype=bf16)
    sink_rows_p = jnp.repeat(sinks.reshape(KV_HEADS, ATTN_GROUP), QBLOCK, axis=1)
    bias = jnp.where(jnp.arange(2 * QBLOCK)[None, None, :] == 0, sink_rows_p[:, :, None], NEG)
    qgq = _quarters(q_norm[0]).reshape(1, LANES)
    kgq = _quarters(k_norm[0]).reshape(1, LANES)
    pos_p = np.arange(tp, dtype=np.float32)
    cos_p, sin_p = _rope_tables_quarters(pos_p)
    cos_w, sin_w = _rope_tables(pos_p[tp - WINDOW:])
    x1_p, k_p, v_p, cext_p, m_p = _prompt_mixer(
        x_prompt[0], cos_p, sin_p, wa, w, wgate, wout_b, bd, bias, anorm, qgq, kgq, kg, cos_w, sin_w,
        gbias, mnorm)

    wq_s = w[:, COL_QA:COL_KA].reshape(D_MODEL, KV_HEADS, ATTN_GROUP, ATTN_HEAD_DIM)
    wq_s = wq_s.transpose(0, 2, 1, 3).reshape(D_MODEL, ATTN_WIDTH)
    wo_a = wout_b[:ATTN_WIDTH].reshape(KV_HEADS, ATTN_GROUP, ATTN_HEAD_DIM, D_MODEL)
    wo_a = wo_a.transpose(1, 0, 2, 3).reshape(ATTN_WIDTH, D_MODEL)
    wout_s = jnp.concatenate([wo_a, wout_b[ATTN_WIDTH:]], axis=0)
    sink_tile = jnp.broadcast_to(
        jnp.repeat(sinks.reshape(KV_HEADS, ATTN_GROUP).T.reshape(-1), tpad)[:, None],
        (ATTN_HEADS * tpad, LANES))
    lanes = np.arange(LANES)
    bd_s = jnp.asarray(lanes[:, None] // ATTN_HEAD_DIM == lanes[None, :] // ATTN_HEAD_DIM, dtype=bf16)
    cos_s, sin_s = _rope_tables(np.arange(tpad, dtype=np.float32) + np.float32(PAST_LEN))
    cos_s = np.tile(cos_s, (SAMPLE_BATCH_BLOCK, 1))
    sin_s = np.tile(sin_s, (SAMPLE_BATCH_BLOCK, 1))
    x_pad = jnp.pad(x_sample, ((0, 0), (0, tpad - nt), (0, 0)))
    ckt = cache_k[0].reshape(nb, WINDOW, KV_WIDTH).transpose(0, 2, 1)
    cvt = cache_v[0].reshape(nb, WINDOW, KV_WIDTH).transpose(0, 2, 1)
    x1_pad, nkt, nvt, c_new, n_new, m_pad = _sample_mixer(
        x_pad, ckt, cvt, state_C[0], state_n[0], state_m[0][:, :, None], cos_s, sin_s, wq_s, w, wgate,
        wout_s, sink_tile, bd_s, anorm, qg, kg, gbias, mnorm)
    y_p, y_s = _ffn(x1_p, x1_pad[:, :nt].reshape(nb * nt, D_MODEL), fnorm, wg_b, wu_b, wd_b)
    m_new = m_pad[:, :nh, 0]

    new_k_s = nkt.transpose(0, 2, 1)
    new_v_s = nvt.transpose(0, 2, 1)

    kv_shape = (1, 1, WINDOW, KV_HEADS, ATTN_HEAD_DIM)
    return (
        y_p[None],
        y_s.reshape(nb, nt, D_MODEL),
        k_p.reshape(kv_shape),
        v_p.reshape(kv_shape),
        cext_p[None, None, :, :, :MLSTM_HEAD_DIM],
        cext_p[None, None, :, :, MLSTM_HEAD_DIM],
        m_p[None, None, :nh, 0],
        new_k_s.reshape(1, nb, WINDOW, KV_HEADS, ATTN_HEAD_DIM),
        new_v_s.reshape(1, nb, WINDOW, KV_HEADS, ATTN_HEAD_DIM),
        c_new.reshape(1, nb, nh, MLSTM_HEAD_DIM, MLSTM_HEAD_DIM),
        n_new.reshape(1, nb, nh, MLSTM_HEAD_DIM),
        m_new.reshape(1, nb, nh),
    )
```

```python
import jax
import jax.numpy as jnp
import numpy as np
from jax import lax
from jax.experimental import pallas as pl
from jax.experimental.pallas import tpu as pltpu

D_MODEL = 1024
PAST_LEN = 16384
ATTN_HEADS = 8
KV_HEADS = 2
ATTN_HEAD_DIM = 64
ATTN_GROUP = ATTN_HEADS // KV_HEADS
ATTN_WIDTH = ATTN_HEADS * ATTN_HEAD_DIM
KV_WIDTH = KV_HEADS * ATTN_HEAD_DIM
WINDOW = 128
ROPE_THETA = 10000.0
MLSTM_HEADS = 4
MLSTM_HEAD_DIM = 128
MLSTM_WIDTH = MLSTM_HEADS * MLSTM_HEAD_DIM
MIX_WIDTH = ATTN_WIDTH + MLSTM_WIDTH
D_FF = 2816
NORM_EPS = 1e-6

LANES = 128
SUBLANES = 8
VMEM_LIMIT = 56 * 1024 * 1024

COL_QA = 0
COL_KA = COL_QA + ATTN_WIDTH
COL_VA = COL_KA + KV_WIDTH
COL_QM = COL_VA + KV_WIDTH
COL_KM = COL_QM + MLSTM_WIDTH
COL_VM = COL_KM + MLSTM_WIDTH
COL_OM = COL_VM + MLSTM_WIDTH
COL_G = COL_OM + MLSTM_WIDTH
IN_WIDTH = COL_G + 2 * MLSTM_HEADS
FG_LANE = SUBLANES

PROMPT_BLOCK = 256
QBLOCK = WINDOW
MCHUNK = 128
PROJ_CHUNK = 256
FFN_BLOCK = 512
FFN_CHUNK = 256
SAMPLE_BATCH_BLOCK = 16
SAMPLE_TOKENS = 4
NEG = -1e30

f32 = jnp.float32
bf16 = jnp.bfloat16


def _rms(x, gain):
    return x * lax.rsqrt(jnp.mean(x * x, axis=-1, keepdims=True) + NORM_EPS) * gain


def _segsum64(s, lane):
    for k in (1, 2, 4, 8, 16, 32):
        s = s + jnp.where((lane & k) != 0, pltpu.roll(s, k, 1), pltpu.roll(s, LANES - k, 1))
    return s


def _headnorm_rope(xs, gain, cos, sin_signed, lane):
    ss = _segsum64(xs * xs, lane)
    y = xs * lax.rsqrt(ss * (1.0 / ATTN_HEAD_DIM) + NORM_EPS) * gain
    partner = jnp.where((lane & 32) != 0, pltpu.roll(y, 32, 1), pltpu.roll(y, LANES - 32, 1))
    return y * cos + partner * sin_signed


def _group_sumsq(xs, bd_ref):
    x2 = xs * xs
    hi = x2.astype(bf16)
    lo = (x2 - hi.astype(f32)).astype(bf16)
    return (jnp.dot(hi, bd_ref[...], preferred_element_type=f32)
            + jnp.dot(lo, bd_ref[...], preferred_element_type=f32))


PA_Q = 0
PA_K = PA_Q + ATTN_WIDTH
PA_V = PA_K + KV_HEADS * LANES
PA_WIDTH = PA_V + KV_HEADS * LANES
QUARTER = ATTN_HEAD_DIM // 2
QA_Q = 0
QA_K = QA_Q + 2 * ATTN_WIDTH
QA_V = QA_K + KV_HEADS * LANES
QA_VZ = QA_V + KV_HEADS * LANES
QA_WIDTH = QA_VZ + KV_HEADS * LANES
ZM_WIDTH = 4 * MLSTM_WIDTH
ATTN_PHASES = 4
MLSTM_PHASES = 5


def _norm_rope_quarters(xs, ss, gain, cos, sin_signed):
    y = xs * lax.rsqrt(ss * (1.0 / ATTN_HEAD_DIM) + NORM_EPS) * gain
    return y * cos + pltpu.roll(y, LANES // 2, 1) * sin_signed


def _col_chunks(width):
    return [(c, min(c + PROJ_CHUNK, width)) for c in range(0, width, PROJ_CHUNK)]


def _mixer_proj_jobs(x_ref, anorm_ref, wa_ref, w_ref, wgate_ref, za_ref, zm_ref, gz_ref):
    h = _rms(x_ref[...], anorm_ref[...]).astype(bf16)

    def proj_job(w_src, wc0, z_ref, c0, c1):
        def run():
            z_ref[:, c0:c1] = jnp.dot(h, w_src[:, wc0 + c0:wc0 + c1], preferred_element_type=f32)
        return run

    return ([proj_job(wa_ref, 0, za_ref, c0, c1) for c0, c1 in _col_chunks(PA_WIDTH)]
            + [proj_job(w_ref, COL_QM, zm_ref, c0, c1) for c0, c1 in _col_chunks(ZM_WIDTH)]
            + [proj_job(wgate_ref, 0, gz_ref, 0, LANES)])


def _mixer_prep_jobs(za_ref, gz_ref, cos_ref, sin_ref, bd_ref, qg_ref, kg_ref, gbias_ref, qa_ref,
                     colf_ref, urow_ref, wrow_ref, mst_ref, mout_ref):
    tb = za_ref.shape[0]

    def gates_job():
        lane_t = lax.broadcasted_iota(jnp.int32, (tb, LANES), 1)
        gcol = gz_ref[...] + gbias_ref[...]
        acol = jnp.where(lane_t < FG_LANE, gcol, jax.nn.log_sigmoid(gcol))
        arow = acol.T
        lane8 = lax.broadcasted_iota(jnp.int32, (SUBLANES, LANES), 1)
        lane_in = lane8 & (MCHUNK - 1)
        m_prev = mst_ref[:, 0:1]
        stacks = []
        for sb in range(tb // LANES):
            ls = slice(sb * LANES, (sb + 1) * LANES)
            ig8 = arow[0:SUBLANES, ls]
            lf8 = arow[FG_LANE:FG_LANE + SUBLANES, ls]
            b8 = lf8
            k = 1
            while k < MCHUNK:
                b8 = b8 + jnp.where(lane_in >= k, pltpu.roll(b8, k, 1), 0.0)
                k *= 2
            u8 = ig8 - b8
            cm8 = u8
            k = 1
            while k < MCHUNK:
                cm8 = jnp.maximum(cm8, jnp.where(lane_in >= k, pltpu.roll(cm8, k, 1), NEG))
                k *= 2
            g8 = jnp.zeros_like(u8)
            mp8 = jnp.zeros_like(u8)
            gl8 = jnp.zeros_like(u8)
            for c in range(LANES // MCHUNK):
                in_chunk = (lane8 // MCHUNK) == c
                gc = jnp.maximum(cm8, m_prev)
                last = c * MCHUNK + MCHUNK - 1
                g_last = jnp.max(jnp.where(lane8 == last, gc, NEG), axis=1, keepdims=True)
                b_last = jnp.max(jnp.where(lane8 == last, b8, NEG), axis=1, keepdims=True)
                g8 = jnp.where(in_chunk, gc, g8)
                mp8 = jnp.where(in_chunk, m_prev, mp8)
                gl8 = jnp.where(in_chunk, g_last, gl8)
                m_prev = b_last + g_last
            a8 = jnp.exp(mp8 - g8)
            emt8 = jnp.exp(-(b8 + g8))
            aend8 = jnp.exp(mp8 - gl8)
            stacks.append(jnp.concatenate(
                [g8, a8, emt8, aend8, jnp.zeros((LANES - 4 * SUBLANES, LANES), f32)], axis=0))
            urow_ref[:, ls] = u8
            wrow_ref[:, ls] = jnp.exp(u8 - gl8)
        mst_ref[...] = jnp.broadcast_to(m_prev, mst_ref.shape)
        mout_ref[...] = jnp.broadcast_to(m_prev, mout_ref.shape)
        colf_ref[...] = jnp.concatenate(stacks, axis=1).T

    def prep_job(qb):
        def run():
            rows = slice(qb * QBLOCK, (qb + 1) * QBLOCK)
            lane = lax.broadcasted_iota(jnp.int32, (QBLOCK, LANES), 1)
            head_a = ((lane // QUARTER) & 1) == 0
            row0 = lax.broadcasted_iota(jnp.int32, (QBLOCK, LANES), 0) == 0
            cos = cos_ref[rows, :]
            sin = sin_ref[rows, :]
            ss = [_group_sumsq(za_ref[rows, d * 2 * LANES:(d + 1) * 2 * LANES], bd_ref)
                  for d in range(PA_V // (2 * LANES))]
            for j in range(PA_V // LANES):
                is_q = j < ATTN_WIDTH // LANES
                y = _norm_rope_quarters(za_ref[rows, j * LANES:(j + 1) * LANES],
                                        ss[j // 2][:, (j % 2) * LANES:(j % 2 + 1) * LANES],
                                        qg_ref[...] if is_q else kg_ref[...], cos, sin)
                if is_q:
                    y = y * (ATTN_HEAD_DIM ** -0.5)
                    qa_ref[rows, QA_Q + 2 * j * LANES:QA_Q + (2 * j + 1) * LANES] = (
                        jnp.where(head_a, y, 0.0).astype(bf16))
                    qa_ref[rows, QA_Q + (2 * j + 1) * LANES:QA_Q + (2 * j + 2) * LANES] = (
                        jnp.where(head_a, 0.0, y).astype(bf16))
                else:
                    c = j - ATTN_WIDTH // LANES
                    qa_ref[rows, QA_K + c * LANES:QA_K + (c + 1) * LANES] = y.astype(bf16)
            for c in range(KV_HEADS):
                v = za_ref[rows, PA_V + c * LANES:PA_V + (c + 1) * LANES]
                qa_ref[rows, QA_V + c * LANES:QA_V + (c + 1) * LANES] = v.astype(bf16)
                qa_ref[rows, QA_VZ + c * LANES:QA_VZ + (c + 1) * LANES] = (
                    jnp.where(row0, 0.0, v).astype(bf16))
        return run

    return [gates_job] + [prep_job(qb) for qb in range(tb // QBLOCK)]


def _mixer_outproj_jobs(xs_ref, mix_ref, wout_ref, x1_ref):
    def job(c0, c1):
        def run():
            x1_ref[:, c0:c1] = xs_ref[:, c0:c1] + jnp.dot(mix_ref[...], wout_ref[:, c0:c1],
                                                          preferred_element_type=f32)
        return run

    return [job(c0, c1) for c0, c1 in _col_chunks(D_MODEL)]


def _mixer_window_out(xs_ref, anorm_ref, w_ref, kgs_ref, coss_ref, sins_ref, kout_ref, vout_ref):
    tb = xs_ref.shape[0]
    h = _rms(xs_ref[tb - WINDOW:, :], anorm_ref[...]).astype(bf16)
    zs = jnp.dot(h, w_ref[:, COL_KA:COL_QM], preferred_element_type=f32)
    lane_s = lax.broadcasted_iota(jnp.int32, (WINDOW, LANES), 1)
    kout_ref[...] = _headnorm_rope(zs[:, :KV_WIDTH], kgs_ref[...], coss_ref[...], sins_ref[...], lane_s)
    vout_ref[...] = zs[:, KV_WIDTH:]


def _mixer_core(first_block, last_block, fillers, prep_jobs, qa_ref, zm_ref, colf_ref, urow_ref, wrow_ref,
                mix_ref, bias_ref, mnorm_ref, kprev_ref, vprev_ref, cst_ref, cext_ref):
    tb = qa_ref.shape[0]
    nqb = tb // QBLOCK
    nch = tb // MCHUNK
    fillers = list(fillers)
    n_fill = len(fillers)
    slots = len(prep_jobs) + ATTN_PHASES + nch * MLSTM_PHASES
    progress = [0]

    def fill():
        progress[0] += 1
        while n_fill - len(fillers) < min(n_fill, -(-n_fill * progress[0] // slots)):
            fillers.pop(0)()

    low_half = lax.broadcasted_iota(jnp.int32, (QBLOCK, LANES), 1) < ATTN_HEAD_DIM
    qi = lax.broadcasted_iota(jnp.int32, (ATTN_GROUP * QBLOCK, 2 * QBLOCK), 0) & (QBLOCK - 1)
    kj = lax.broadcasted_iota(jnp.int32, (ATTN_GROUP * QBLOCK, 2 * QBLOCK), 1)
    band = (kj > qi) & (kj <= qi + QBLOCK)
    ones_slab = jnp.ones((2 * QBLOCK, LANES), bf16)

    def attn_phases():
        chains = [(qb, c) for qb in range(nqb) for c in range(KV_HEADS)]
        rows_of = lambda qb: slice(qb * QBLOCK, (qb + 1) * QBLOCK)
        st = {}

        def scores():
            for qb, c in chains:
                rows = rows_of(qb)
                kcols = slice(QA_K + c * LANES, QA_K + (c + 1) * LANES)
                if qb == 0:
                    kprev, vprev = kprev_ref[c], vprev_ref[c]
                else:
                    kprev = qa_ref[rows_of(qb - 1), kcols]
                    vprev = qa_ref[rows_of(qb - 1), QA_VZ + c * LANES:QA_VZ + (c + 1) * LANES]
                kcat = jnp.concatenate([kprev, qa_ref[rows, kcols]], axis=0)
                vcat = jnp.concatenate(
                    [vprev, qa_ref[rows, QA_V + c * LANES:QA_V + (c + 1) * LANES]], axis=0)
                st['vext', qb, c] = jnp.concatenate([vcat, ones_slab], axis=1)
                q0 = QA_Q + c * ATTN_GROUP * LANES
                qst = jnp.concatenate([qa_ref[rows, q0 + g * LANES:q0 + (g + 1) * LANES]
                                       for g in range(ATTN_GROUP)], axis=0)
                st['s', qb, c] = lax.dot_general(qst, kcat, (((1,), (1,)), ((), ())),
                                                 preferred_element_type=f32)

        def softmax():
            for qb, c in chains:
                valid = band & (kj >= QBLOCK) if (first_block and qb == 0) else band
                s = jnp.where(valid, st.pop(('s', qb, c)), bias_ref[c])
                st['p', qb, c] = jnp.exp(s - jnp.max(s, axis=-1, keepdims=True)).astype(bf16)

        def values():
            for qb, c in chains:
                st['of', qb, c] = jnp.dot(st.pop(('p', qb, c)), st.pop(('vext', qb, c)),
                                          preferred_element_type=f32)

        def normalise():
            for qb, c in chains:
                of = st.pop(('of', qb, c))
                o = of[:, :LANES] / of[:, LANES:]
                for jj in range(2):
                    pair = jnp.where(low_half, o[(2 * jj) * QBLOCK:(2 * jj + 1) * QBLOCK],
                                     o[(2 * jj + 1) * QBLOCK:(2 * jj + 2) * QBLOCK])
                    col = (2 * c + jj) * LANES
                    mix_ref[rows_of(qb), col:col + LANES] = pair.astype(bf16)

        return [scores, softmax, values, normalise]

    ti = lax.broadcasted_iota(jnp.int32, (MCHUNK, MCHUNK), 0)
    si = lax.broadcasted_iota(jnp.int32, (MCHUNK, MCHUNK), 1)
    causal = si <= ti
    ones_l = jnp.ones((MCHUNK, LANES), bf16)

    cexts = [cst_ref[hd] for hd in range(MLSTM_HEADS)]

    def mlstm_phases(c):
        rows = slice(c * MCHUNK, (c + 1) * MCHUNK)
        heads = range(MLSTM_HEADS)
        hcols = lambda k, hd: slice((k * MLSTM_HEADS + hd) * MLSTM_HEAD_DIM,
                                    (k * MLSTM_HEADS + hd + 1) * MLSTM_HEAD_DIM)
        st = {}

        def scores():
            for hd in heads:
                st['q', hd] = zm_ref[rows, hcols(0, hd)].astype(bf16)
                st['k', hd] = zm_ref[rows, hcols(1, hd)] * (MLSTM_HEAD_DIM ** -0.5)
                st['s', hd] = lax.dot_general(st['q', hd], st['k', hd].astype(bf16),
                                              (((1,), (1,)), ((), ())), preferred_element_type=f32)

        def decay():
            for hd in heads:
                g_c = colf_ref[rows, hd:hd + 1]
                u_r = urow_ref[hd:hd + 1, rows]
                dmat = jnp.exp(jnp.where(causal, u_r - g_c, NEG))
                st['s', hd] = (st['s', hd] * dmat).astype(bf16)

        def readout():
            for hd in heads:
                a_c = colf_ref[rows, SUBLANES + hd:SUBLANES + hd + 1]
                vext = jnp.concatenate([zm_ref[rows, hcols(2, hd)].astype(bf16), ones_l], axis=1)
                st['v', hd] = vext
                st['nd', hd] = (
                    a_c * jnp.dot(st.pop(('q', hd)), cexts[hd].astype(bf16), preferred_element_type=f32)
                    + jnp.dot(st.pop(('s', hd)), vext, preferred_element_type=f32))

        def emit():
            for hd in heads:
                nd = st.pop(('nd', hd))
                emt_c = colf_ref[rows, 2 * SUBLANES + hd:2 * SUBLANES + hd + 1]
                hraw = nd[:, :MLSTM_HEAD_DIM] / jnp.maximum(jnp.abs(nd[:, MLSTM_HEAD_DIM:]), emt_c)
                hn = _rms(hraw, mnorm_ref[hd:hd + 1, :])
                og = zm_ref[rows, hcols(3, hd)]
                mix_ref[rows, ATTN_WIDTH + hd * MLSTM_HEAD_DIM:ATTN_WIDTH + (hd + 1) * MLSTM_HEAD_DIM] = (
                    (hn * jax.nn.sigmoid(og)).astype(bf16))

        def update():
            for hd in heads:
                aend = colf_ref[c * MCHUNK:c * MCHUNK + 1, 3 * SUBLANES + hd:3 * SUBLANES + hd + 1]
                w_r = wrow_ref[hd:hd + 1, rows]
                kw_t = (st.pop(('k', hd)).T * w_r).astype(bf16)
                cexts[hd] = aend * cexts[hd] + jnp.dot(kw_t, st.pop(('v', hd)),
                                                       preferred_element_type=f32)

        return [scores, decay, readout, emit, update]

    gates_job, rope_jobs = prep_jobs[0], prep_jobs[1:]
    attn = attn_phases()
    chunks = [mlstm_phases(c) for c in range(nch)]
    mlstm = []
    for c, ph in enumerate(chunks):
        ahead = chunks[c + 1][:2] if c + 1 < nch else []
        mlstm += (ph if c == 0 else ph[2:])[:-2] + ahead + ph[-2:]
    assert len(attn) == ATTN_PHASES and len(mlstm) == nch * MLSTM_PHASES
    order = rope_jobs + attn[:1] + [gates_job]
    rest = attn[1:]
    for i in range(max(len(rest), len(mlstm))):
        order += rest[i:i + 1] + mlstm[i:i + 1]
    assert len(order) == slots
    for job in order:
        job()
        fill()
    last_rows = slice(tb - QBLOCK, tb)
    for c in range(KV_HEADS):
        kprev_ref[c] = qa_ref[last_rows, QA_K + c * LANES:QA_K + (c + 1) * LANES]
        vprev_ref[c] = qa_ref[last_rows, QA_VZ + c * LANES:QA_VZ + (c + 1) * LANES]
    for hd in range(MLSTM_HEADS):
        cst_ref[hd] = cexts[hd]
        if last_block:
            cext_ref[hd] = cexts[hd]


def _prompt_mixer_kernel(x_ref, cos_ref, sin_ref, wa_ref, w_ref, wgate_ref, wout_ref, bd_ref, bias_ref,
                         anorm_ref, qg_ref, kg_ref, kgs_ref, coss_ref, sins_ref, gbias_ref, mnorm_ref,
                         x1_ref, kout_ref, vout_ref, cext_ref, mout_ref,
                         za0, za1, zm0, zm1, gz0, gz1, mix0, mix1, xs0, xs1,
                         qa_ref, colf_ref, urow_ref, wrow_ref, kprev_ref, vprev_ref, cst_ref, mst_ref):
    step = pl.program_id(0)
    nblk = pl.num_programs(0) - 2
    za, zm, gz, mix, xs = (za0, za1), (zm0, zm1), (gz0, gz1), (mix0, mix1), (xs0, xs1)

    def run(parity, do_in, do_core, do_out, first_block=False, last_block=False):
        other = 1 - parity
        jobs = []
        if do_out:
            jobs += _mixer_outproj_jobs(xs[parity], mix[parity], wout_ref, x1_ref)
        if do_in:
            jobs += _mixer_proj_jobs(x_ref, anorm_ref, wa_ref, w_ref, wgate_ref, za[parity], zm[parity],
                                     gz[parity])
        if do_core:
            prep = _mixer_prep_jobs(za[other], gz[other], cos_ref, sin_ref, bd_ref, qg_ref, kg_ref,
                                    gbias_ref, qa_ref, colf_ref, urow_ref, wrow_ref, mst_ref, mout_ref)
            _mixer_core(first_block, last_block, jobs, prep, qa_ref, zm[other], colf_ref, urow_ref,
                        wrow_ref, mix[other], bias_ref, mnorm_ref, kprev_ref, vprev_ref, cst_ref, cext_ref)
        else:
            for job in jobs:
                job()
        if last_block:
            _mixer_window_out(xs[other], anorm_ref, w_ref, kgs_ref, coss_ref, sins_ref, kout_ref, vout_ref)
        if do_in:
            xs[parity][...] = x_ref[...]

    @pl.when(step == 0)
    def _first():
        kprev_ref[...] = jnp.zeros_like(kprev_ref)
        vprev_ref[...] = jnp.zeros_like(vprev_ref)
        cst_ref[...] = jnp.zeros_like(cst_ref)
        mst_ref[...] = jnp.zeros_like(mst_ref)
        run(0, True, False, False)

    @pl.when(step == 1)
    def _second():
        run(1, True, True, False, first_block=True)

    steady = (step >= 2) & (step < nblk)

    @pl.when(steady & (step % 2 == 0))
    def _even():
        run(0, True, True, True)

    @pl.when(steady & (step % 2 == 1))
    def _odd():
        run(1, True, True, True)

    @pl.when(step == nblk)
    def _drain_core():
        run(0, False, True, True, last_block=True)

    @pl.when(step == nblk + 1)
    def _drain_out():
        run(1, False, False, True)


def _const_spec(shape, single=False):
    nd = len(shape)
    if single:
        return pl.BlockSpec(shape, lambda i, *_: (0,) * nd, pipeline_mode=pl.Buffered(1))
    return pl.BlockSpec(shape, lambda i, *_: (0,) * nd)


def _prompt_mixer(x, cos, sin, wa, w, wgate, wout_b, bd, bias, anorm, qg, kg, kgs, coss, sins, gbias, mnorm):
    t = x.shape[0]
    tb = PROMPT_BLOCK
    nblk = t // tb
    assert nblk % 2 == 0 and nblk >= 4
    state_shape = (MLSTM_HEADS, MLSTM_HEAD_DIM, 2 * MLSTM_HEAD_DIM)
    last = nblk - 1
    lag = lambda d: (lambda i: (jnp.clip(i - d, 0, last), 0))
    return pl.pallas_call(
        _prompt_mixer_kernel,
        grid=(nblk + 2,),
        in_specs=[
            pl.BlockSpec((tb, D_MODEL), lag(0)),
            pl.BlockSpec((tb, LANES), lag(1)),
            pl.BlockSpec((tb, LANES), lag(1)),
            _const_spec((D_MODEL, PA_WIDTH), single=True),
            _const_spec((D_MODEL, IN_WIDTH), single=True),
            _const_spec((D_MODEL, LANES), single=True),
            _const_spec((MIX_WIDTH, D_MODEL), single=True),
            _const_spec((2 * LANES, 2 * LANES), single=True),
            _const_spec((KV_HEADS, ATTN_GROUP * QBLOCK, 2 * QBLOCK), single=True),
            _const_spec((1, D_MODEL)),
            _const_spec((1, LANES)),
            _const_spec((1, LANES)),
            _const_spec((1, LANES)),
            _const_spec((WINDOW, LANES)),
            _const_spec((WINDOW, LANES)),
            _const_spec((1, LANES)),
            _const_spec((MLSTM_HEADS, MLSTM_HEAD_DIM)),
        ],
        out_specs=[
            pl.BlockSpec((tb, D_MODEL), lag(2)),
            _const_spec((WINDOW, KV_WIDTH)),
            _const_spec((WINDOW, KV_WIDTH)),
            _const_spec(state_shape),
            _const_spec((SUBLANES, LANES)),
        ],
        out_shape=[
            jax.ShapeDtypeStruct((t, D_MODEL), f32),
            jax.ShapeDtypeStruct((WINDOW, KV_WIDTH), f32),
            jax.ShapeDtypeStruct((WINDOW, KV_WIDTH), f32),
            jax.ShapeDtypeStruct(state_shape, f32),
            jax.ShapeDtypeStruct((SUBLANES, LANES), f32),
        ],
        scratch_shapes=(
            [pltpu.VMEM((tb, PA_WIDTH), f32)] * 2 + [pltpu.VMEM((tb, ZM_WIDTH), f32)] * 2
            + [pltpu.VMEM((tb, LANES), f32)] * 2
            + [pltpu.VMEM((tb, MIX_WIDTH), bf16)] * 2 + [pltpu.VMEM((tb, D_MODEL), f32)] * 2
            + [pltpu.VMEM((tb, QA_WIDTH), bf16), pltpu.VMEM((tb, LANES), f32)]
            + [pltpu.VMEM((SUBLANES, tb), f32)] * 2
            + [pltpu.VMEM((KV_HEADS, WINDOW, LANES), bf16)] * 2
            + [pltpu.VMEM(state_shape, f32), pltpu.VMEM((SUBLANES, LANES), f32)]),
        compiler_params=pltpu.CompilerParams(
            dimension_semantics=("arbitrary",), vmem_limit_bytes=VMEM_LIMIT),
        name="prompt_mixer",
    )(x, cos, sin, wa, w, wgate, wout_b, bd, bias, anorm, qg, kg, kgs, coss, sins, gbias, mnorm)


def _ffn_kernel(xp_ref, xs_ref, g_ref, wg_ref, wu_ref, wd_ref, op_ref, os_ref):
    step = pl.program_id(0)
    last = pl.num_programs(0) - 1

    @pl.when(step < last)
    def _prompt_rows():
        _ffn_rows(xp_ref, g_ref, wg_ref, wu_ref, wd_ref, op_ref)

    @pl.when(step == last)
    def _sample_rows():
        _ffn_rows(xs_ref, g_ref, wg_ref, wu_ref, wd_ref, os_ref)


def _ffn_rows(x_ref, g_ref, wg_ref, wu_ref, wd_ref, o_ref):
    x = x_ref[...]
    hf = _rms(x, g_ref[...]).astype(bf16)
    acc = x
    for c in range(D_FF // FFN_CHUNK):
        cs = slice(c * FFN_CHUNK, (c + 1) * FFN_CHUNK)
        gate = jnp.dot(hf, wg_ref[:, cs], preferred_element_type=f32)
        up = jnp.dot(hf, wu_ref[:, cs], preferred_element_type=f32)
        act = (gate * jax.nn.sigmoid(gate) * up).astype(bf16)
        acc = acc + jnp.dot(act, wd_ref[cs, :], preferred_element_type=f32)
    o_ref[...] = acc


def _ffn(x_p, x_s, fnorm, wg_b, wu_b, wd_b):
    n = x_p.shape[0]
    ns = x_s.shape[0]
    tm = FFN_BLOCK
    last = n // tm - 1
    return pl.pallas_call(
        _ffn_kernel,
        grid=(n // tm + 1,),
        in_specs=[
            pl.BlockSpec((tm, D_MODEL), lambda i: (jnp.minimum(i, last), 0)),
            _const_spec((ns, D_MODEL), single=True),
            _const_spec((1, D_MODEL)),
            _const_spec((D_MODEL, D_FF), single=True),
            _const_spec((D_MODEL, D_FF), single=True),
            _const_spec((D_FF, D_MODEL), single=True),
        ],
        out_specs=[pl.BlockSpec((tm, D_MODEL), lambda i: (jnp.minimum(i, last), 0)),
                   _const_spec((ns, D_MODEL))],
        out_shape=[jax.ShapeDtypeStruct((n, D_MODEL), f32), jax.ShapeDtypeStruct((ns, D_MODEL), f32)],
        compiler_params=pltpu.CompilerParams(
            dimension_semantics=("arbitrary",), vmem_limit_bytes=VMEM_LIMIT),
        name="ffn",
    )(x_p, x_s, fnorm, wg_b, wu_b, wd_b)


def _sample_mixer_kernel(x_ref, ckt_ref, cvt_ref, c_ref, n_ref, m_ref, cos_ref, sin_ref, wqs_ref, w_ref,
                         wgate_ref, wout_ref, sink_ref, bd_ref, anorm_ref, qg_ref, kg_ref, gbias_ref,
                         mnorm_ref, x1_ref, nkt_ref, nvt_ref, cn_ref, nn_ref, mn_ref):
    bb, tpad, _ = x_ref.shape
    nrows = bb * tpad
    nreal = SAMPLE_TOKENS
    h = _rms(x_ref[...].reshape(nrows, D_MODEL), anorm_ref[...]).astype(bf16)
    z = jnp.concatenate(
        [jnp.dot(h, wqs_ref[...], preferred_element_type=f32),
         jnp.dot(h, w_ref[:, COL_KA:COL_G], preferred_element_type=f32),
         jnp.dot(h, wgate_ref[...], preferred_element_type=f32)], axis=1)
    lane = lax.broadcasted_iota(jnp.int32, (nrows, LANES), 1)
    low = lane < ATTN_HEAD_DIM
    cos = cos_ref[...]
    sin = sin_ref[...]

    def per_seq(a):
        return a.reshape(bb, tpad, a.shape[-1])

    def norm_rope(xs, gain):
        y = xs * lax.rsqrt(_group_sumsq(xs, bd_ref) * (1.0 / ATTN_HEAD_DIM) + NORM_EPS) * gain
        partner = jnp.where((lane & QUARTER) != 0, pltpu.roll(y, QUARTER, 1),
                            pltpu.roll(y, LANES - QUARTER, 1))
        return y * cos + partner * sin

    q_rows = []
    for j in range(ATTN_GROUP):
        qs = norm_rope(z[:, COL_QA + j * LANES:COL_QA + (j + 1) * LANES], qg_ref[...])
        qs = qs * (ATTN_HEAD_DIM ** -0.5)
        q_rows.append(per_seq(jnp.where(low, qs, 0.0)).astype(bf16))
        q_rows.append(per_seq(jnp.where(low, 0.0, qs)).astype(bf16))
    qbd = jnp.concatenate(q_rows, axis=1)
    knew = norm_rope(z[:, COL_KA:COL_KA + KV_WIDTH], kg_ref[...])
    vnew = z[:, COL_VA:COL_VA + KV_WIDTH]
    zpad = jnp.zeros((bb, LANES - tpad, LANES), bf16)
    knp = jnp.concatenate([per_seq(knew).astype(bf16), zpad], axis=1)
    vnp = jnp.concatenate([per_seq(vnew).astype(bf16), zpad], axis=1)
    ckt = ckt_ref[...]
    cvt = cvt_ref[...]
    s = jnp.concatenate(
        [jnp.einsum('bqd,bdw->bqw', qbd, ckt.astype(bf16), preferred_element_type=f32),
         jnp.einsum('bqd,bkd->bqk', qbd, knp, preferred_element_type=f32)], axis=2)
    tq = lax.broadcasted_iota(jnp.int32, s.shape, 1) & (tpad - 1)
    kj = lax.broadcasted_iota(jnp.int32, s.shape, 2)
    valid = ((kj < WINDOW) & (kj > tq)) | ((kj >= WINDOW) & (kj - WINDOW <= tq) & (kj - WINDOW < nreal))
    s = jnp.where(valid, s, NEG)
    sink = sink_ref[:, 0:1][None]
    mx = jnp.maximum(jnp.max(s, axis=-1, keepdims=True), sink)
    p = jnp.exp(s - mx)
    den = jnp.sum(p, axis=-1, keepdims=True) + jnp.exp(sink - mx)
    pb = p.astype(bf16)
    o = (jnp.einsum('bqw,bdw->bqd', pb[:, :, :WINDOW], cvt.astype(bf16), preferred_element_type=f32)
         + jnp.einsum('bqk,bkd->bqd', pb[:, :, WINDOW:], vnp, preferred_element_type=f32)) / den
    low3 = lax.broadcasted_iota(jnp.int32, (bb, tpad, LANES), 2) < ATTN_HEAD_DIM
    mix_parts = []
    for j in range(ATTN_GROUP):
        r0 = 2 * j * tpad
        pair = jnp.where(low3, o[:, r0:r0 + tpad, :], o[:, r0 + tpad:r0 + 2 * tpad, :])
        mix_parts.append(pair.reshape(nrows, LANES).astype(bf16))

    keep = lax.broadcasted_iota(jnp.int32, (KV_WIDTH, WINDOW), 1) < WINDOW - nreal
    knt = knew.T
    vnt = vnew.T
    for b in range(bb):
        shift = (WINDOW - nreal - b * tpad) % LANES
        nkt_ref[b] = jnp.where(keep, pltpu.roll(ckt_ref[b], WINDOW - nreal, 1), pltpu.roll(knt, shift, 1))
        nvt_ref[b] = jnp.where(keep, pltpu.roll(cvt_ref[b], WINDOW - nreal, 1), pltpu.roll(vnt, shift, 1))

    gz = per_seq(z[:, COL_G:COL_G + LANES] + gbias_ref[...])
    lgz = jax.nn.log_sigmoid(gz)
    trow = lax.broadcasted_iota(jnp.int32, (bb, tpad, 1), 1)
    real = trow < nreal
    mn_ref[...] = jnp.zeros_like(mn_ref)
    for hd in range(MLSTM_HEADS):
        hcols = lambda base: slice(base + hd * MLSTM_HEAD_DIM, base + (hd + 1) * MLSTM_HEAD_DIM)
        q = per_seq(z[:, hcols(COL_QM)])
        k = per_seq(z[:, hcols(COL_KM)]) * (MLSTM_HEAD_DIM ** -0.5)
        v = per_seq(z[:, hcols(COL_VM)])
        og = per_seq(z[:, hcols(COL_OM)])
        c0 = c_ref[:, hd]
        n0 = n_ref[:, hd:hd + 1, :]
        m0 = m_ref[:, hd:hd + 1, :]
        ig_c = jnp.where(real, gz[:, :, hd:hd + 1], NEG)
        lf_c = jnp.where(real, lgz[:, :, FG_LANE + hd:FG_LANE + hd + 1], 0.0)
        b_c = jnp.zeros_like(lf_c)
        for sx in range(nreal):
            b_c = b_c + jnp.where(trow >= sx, lf_c[:, sx:sx + 1, :], 0.0)
        dlog = [jnp.where(trow >= sx, b_c - b_c[:, sx:sx + 1, :] + ig_c[:, sx:sx + 1, :], NEG)
                for sx in range(nreal)]
        inter = b_c + m0
        m_t = inter
        for sx in range(nreal):
            m_t = jnp.maximum(m_t, dlog[sx])
        a = jnp.exp(inter - m_t)
        qc = jnp.einsum('btd,bde->bte', q.astype(bf16), c0.astype(bf16), preferred_element_type=f32)
        num = a * qc
        den_m = a * jnp.sum(q * n0, axis=2, keepdims=True)
        for sx in range(nreal):
            sd = jnp.sum(q * k[:, sx:sx + 1, :], axis=2, keepdims=True) * jnp.exp(dlog[sx] - m_t)
            num = num + sd * v[:, sx:sx + 1, :]
            den_m = den_m + sd
        hraw = num / jnp.maximum(jnp.abs(den_m), jnp.exp(-m_t))
        hn = _rms(hraw, mnorm_ref[hd:hd + 1, :][None])
        mix_parts.append((hn * jax.nn.sigmoid(og)).reshape(nrows, MLSTM_HEAD_DIM).astype(bf16))
        last = nreal - 1
        m_new = m_t[:, last:last + 1, :]
        b_last = b_c[:, last:last + 1, :]
        a_end = jnp.exp(b_last + m0 - m_new)
        kw = k * jnp.exp(b_last - b_c + ig_c - m_new)
        cn_ref[:, hd] = a_end * c0 + jnp.einsum('bsd,bse->bde', kw.astype(bf16), v.astype(bf16),
                                               preferred_element_type=f32)
        nn_ref[:, hd:hd + 1, :] = a_end * n0 + jnp.sum(kw, axis=1, keepdims=True)
        mn_ref[:, hd:hd + 1, :] = jnp.broadcast_to(m_new, (bb, 1, LANES))

    mix = jnp.concatenate(mix_parts, axis=1)
    x1 = x_ref[...].reshape(nrows, D_MODEL) + jnp.dot(mix, wout_ref[...], preferred_element_type=f32)
    x1_ref[...] = x1.reshape(bb, tpad, D_MODEL)


def _sample_mixer(x_pad, ckt, cvt, c0, n0, m0, cos, sin, wq_s, w, wgate, wout_s, sink_tile, bd, anorm, qg,
                  kg, gbias, mnorm):
    nb, tpad, _ = x_pad.shape
    bb = SAMPLE_BATCH_BLOCK
    nh = MLSTM_HEADS
    blk = lambda shape: pl.BlockSpec(shape, lambda i: (i,) + (0,) * (len(shape) - 1))
    cblk = (bb, nh, MLSTM_HEAD_DIM, MLSTM_HEAD_DIM)
    return pl.pallas_call(
        _sample_mixer_kernel,
        grid=(nb // bb,),
        in_specs=[blk((bb, tpad, D_MODEL)), blk((bb, KV_WIDTH, WINDOW)), blk((bb, KV_WIDTH, WINDOW)),
                  blk(cblk), blk((bb, nh, MLSTM_HEAD_DIM)), blk((bb, nh, 1)),
                  _const_spec((bb * tpad, LANES)), _const_spec((bb * tpad, LANES)),
                  _const_spec((D_MODEL, ATTN_WIDTH), single=True),
                  _const_spec((D_MODEL, IN_WIDTH), single=True),
                  _const_spec((D_MODEL, LANES), single=True),
                  _const_spec((MIX_WIDTH, D_MODEL), single=True),
                  _const_spec((ATTN_HEADS * tpad, LANES)), _const_spec((LANES, LANES)),
                  _const_spec((1, D_MODEL)),
                  _const_spec((1, LANES)), _const_spec((1, LANES)), _const_spec((1, LANES)),
                  _const_spec((nh, MLSTM_HEAD_DIM))],
        out_specs=[blk((bb, tpad, D_MODEL)), blk((bb, KV_WIDTH, WINDOW)), blk((bb, KV_WIDTH, WINDOW)),
                   blk(cblk), blk((bb, nh, MLSTM_HEAD_DIM)), blk((bb, tpad, LANES))],
        out_shape=[jax.ShapeDtypeStruct((nb, tpad, D_MODEL), f32),
                   jax.ShapeDtypeStruct((nb, KV_WIDTH, WINDOW), f32),
                   jax.ShapeDtypeStruct((nb, KV_WIDTH, WINDOW), f32),
                   jax.ShapeDtypeStruct((nb,) + cblk[1:], f32),
                   jax.ShapeDtypeStruct((nb, nh, MLSTM_HEAD_DIM), f32),
                   jax.ShapeDtypeStruct((nb, tpad, LANES), f32)],
        compiler_params=pltpu.CompilerParams(
            dimension_semantics=("arbitrary",), vmem_limit_bytes=VMEM_LIMIT),
        name="sample_mixer",
    )(x_pad, ckt, cvt, c0, n0, m0, cos, sin, wq_s, w, wgate, wout_s, sink_tile, bd, anorm, qg, kg, gbias,
      mnorm)


def _rope_angles(pos):
    half = ATTN_HEAD_DIM // 2
    inv = ROPE_THETA ** (-np.arange(half, dtype=np.float64) / half)
    ang = pos.astype(np.float64)[:, None] * inv[None, :]
    return np.cos(ang).astype(np.float32), np.sin(ang).astype(np.float32)


def _rope_tables(pos):
    c, s = _rope_angles(pos)
    cos = np.tile(c, (1, LANES // QUARTER))
    sin = np.tile(np.concatenate([-s, s], axis=1), (1, LANES // ATTN_HEAD_DIM))
    return cos, sin


def _rope_tables_quarters(pos):
    c, s = _rope_angles(pos)
    return np.tile(c, (1, LANES // QUARTER)), np.concatenate([-s, -s, s, s], axis=1)


def _quarters(a):
    lo, hi = a[..., :QUARTER], a[..., QUARTER:]
    return jnp.concatenate([lo, lo, hi, hi], axis=-1)


def _prompt_attn_weights(w):
    d = w.shape[0]
    wq = w[:, COL_QA:COL_KA].reshape(d, ATTN_WIDTH // LANES, 2, 2, QUARTER)
    wq = wq.transpose(0, 1, 3, 2, 4).reshape(d, ATTN_WIDTH)
    wk = _quarters(w[:, COL_KA:COL_VA].reshape(d, KV_HEADS, ATTN_HEAD_DIM)).reshape(d, KV_HEADS * LANES)
    wv = w[:, COL_VA:COL_QM].reshape(d, KV_HEADS, 1, ATTN_HEAD_DIM)
    wv = jnp.broadcast_to(wv, (d, KV_HEADS, 2, ATTN_HEAD_DIM)).reshape(d, KV_HEADS * LANES)
    return jnp.concatenate([wq, wk, wv], axis=1)


def kernel(x_prompt, x_sample, cache_k, cache_v, state_C, state_n, state_m, attn_norm, w_in, q_norm,
           k_norm, attn_sinks, b_ig, b_fg, mlstm_norm, w_out, ffn_norm, w_gate, w_up, w_down):
    assert w_in.shape[0] == 1 and x_prompt.shape[0] == 1
    tp = x_prompt.shape[1]
    nb, nt = x_sample.shape[0], x_sample.shape[1]
    assert nt == SAMPLE_TOKENS
    tpad = SUBLANES
    nh = MLSTM_HEADS

    w = w_in[0].astype(bf16)
    pad_a = jnp.zeros((D_MODEL, FG_LANE - nh), bf16)
    pad_b = jnp.zeros((D_MODEL, LANES - FG_LANE - nh), bf16)
    wgate = jnp.concatenate([w[:, COL_G:COL_G + nh], pad_a, w[:, COL_G + nh:], pad_b], axis=1)
    gbias = jnp.concatenate(
        [b_ig[0], jnp.zeros((FG_LANE - nh,), f32), b_fg[0], jnp.zeros((LANES - FG_LANE - nh,), f32)]
    ).reshape(1, LANES)
    wout_b = w_out[0].astype(bf16)
    wg_b = w_gate[0].astype(bf16)
    wu_b = w_up[0].astype(bf16)
    wd_b = w_down[0].astype(bf16)
    anorm = attn_norm[0].reshape(1, D_MODEL)
    fnorm = ffn_norm[0].reshape(1, D_MODEL)
    qg = jnp.tile(q_norm[0], LANES // ATTN_HEAD_DIM).reshape(1, LANES)
    kg = jnp.tile(k_norm[0], LANES // ATTN_HEAD_DIM).reshape(1, LANES)
    mnorm = mlstm_norm[0].reshape(nh, MLSTM_HEAD_DIM)
    sinks = attn_sinks[0]

    wa = _prompt_attn_weights(w)
    idx = np.arange(2 * LANES)
    same = (idx[:, None] // LANES == idx[None, :] // LANES) & (
        (idx[:, None] // QUARTER) % 2 == (idx[None, :] // QUARTER) % 2)
    bd = jnp.asarray(same, dtype=bf16)
    sink_rows_p = jnp.repeat(sinks.reshape(KV_HEADS, ATTN_GROUP), QBLOCK, axis=1)
    bias = jnp.where(jnp.arange(2 * QBLOCK)[None, None, :] == 0, sink_rows_p[:, :, None], NEG)
    qgq = _quarters(q_norm[0]).reshape(1, LANES)
    kgq = _quarters(k_norm[0]).reshape(1, LANES)
    pos_p = np.arange(tp, dtype=np.float32)
    cos_p, sin_p = _rope_tables_quarters(pos_p)
    cos_w, sin_w = _rope_tables(pos_p[tp - WINDOW:])
    x1_p, k_p, v_p, cext_p, m_p = _prompt_mixer(
        x_prompt[0], cos_p, sin_p, wa, w, wgate, wout_b, bd, bias, anorm, qgq, kgq, kg, cos_w, sin_w,
        gbias, mnorm)

    wq_s = w[:, COL_QA:COL_KA].reshape(D_MODEL, KV_HEADS, ATTN_GROUP, ATTN_HEAD_DIM)
    wq_s = wq_s.transpose(0, 2, 1, 3).reshape(D_MODEL, ATTN_WIDTH)
    wo_a = wout_b[:ATTN_WIDTH].reshape(KV_HEADS, ATTN_GROUP, ATTN_HEAD_DIM, D_MODEL)
    wo_a = wo_a.transpose(1, 0, 2, 3).reshape(ATTN_WIDTH, D_MODEL)
    wout_s = jnp.concatenate([wo_a, wout_b[ATTN_WIDTH:]], axis=0)
    sink_tile = jnp.broadcast_to(
        jnp.repeat(sinks.reshape(KV_HEADS, ATTN_GROUP).T.reshape(-1), tpad)[:, None],
        (ATTN_HEADS * tpad, LANES))
    lanes = np.arange(LANES)
    bd_s = jnp.asarray(lanes[:, None] // ATTN_HEAD_DIM == lanes[None, :] // ATTN_HEAD_DIM, dtype=bf16)
    cos_s, sin_s = _rope_tables(np.arange(tpad, dtype=np.float32) + np.float32(PAST_LEN))
    cos_s = np.tile(cos_s, (SAMPLE_BATCH_BLOCK, 1))
    sin_s = np.tile(sin_s, (SAMPLE_BATCH_BLOCK, 1))
    x_pad = jnp.pad(x_sample, ((0, 0), (0, tpad - nt), (0, 0)))
    ckt = cache_k[0].reshape(nb, WINDOW, KV_WIDTH).transpose(0, 2, 1)
    cvt = cache_v[0].reshape(nb, WINDOW, KV_WIDTH).transpose(0, 2, 1)
    x1_pad, nkt, nvt, c_new, n_new, m_pad = _sample_mixer(
        x_pad, ckt, cvt, state_C[0], state_n[0], state_m[0][:, :, None], cos_s, sin_s, wq_s, w, wgate,
        wout_s, sink_tile, bd_s, anorm, qg, kg, gbias, mnorm)
    y_p, y_s = _ffn(x1_p, x1_pad[:, :nt].reshape(nb * nt, D_MODEL), fnorm, wg_b, wu_b, wd_b)
    m_new = m_pad[:, :nh, 0]

    new_k_s = nkt.transpose(0, 2, 1)
    new_v_s = nvt.transpose(0, 2, 1)

    kv_shape = (1, 1, WINDOW, KV_HEADS, ATTN_HEAD_DIM)
    return (
        y_p[None],
        y_s.reshape(nb, nt, D_MODEL),
        k_p.reshape(kv_shape),
        v_p.reshape(kv_shape),
        cext_p[None, None, :, :, :MLSTM_HEAD_DIM],
        cext_p[None, None, :, :, MLSTM_HEAD_DIM],
        m_p[None, None, :nh, 0],
        new_k_s.reshape(1, nb, WINDOW, KV_HEADS, ATTN_HEAD_DIM),
        new_v_s.reshape(1, nb, WINDOW, KV_HEADS, ATTN_HEAD_DIM),
        c_new.reshape(1, nb, nh, MLSTM_HEAD_DIM, MLSTM_HEAD_DIM),
        n_new.reshape(1, nb, nh, MLSTM_HEAD_DIM),
        m_new.reshape(1, nb, nh),
    )
```

```python
import jax
import jax.numpy as jnp
import numpy as np
from jax import lax
from jax.experimental import pallas as pl
from jax.experimental.pallas import tpu as pltpu

D_MODEL = 1024
PAST_LEN = 16384
ATTN_HEADS = 8
KV_HEADS = 2
ATTN_HEAD_DIM = 64
ATTN_GROUP = ATTN_HEADS // KV_HEADS
ATTN_WIDTH = ATTN_HEADS * ATTN_HEAD_DIM
KV_WIDTH = KV_HEADS * ATTN_HEAD_DIM
WINDOW = 128
ROPE_THETA = 10000.0
MLSTM_HEADS = 4
MLSTM_HEAD_DIM = 128
MLSTM_WIDTH = MLSTM_HEADS * MLSTM_HEAD_DIM
MIX_WIDTH = ATTN_WIDTH + MLSTM_WIDTH
D_FF = 2816
NORM_EPS = 1e-6

LANES = 128
SUBLANES = 8
VMEM_LIMIT = 56 * 1024 * 1024

COL_QA = 0
COL_KA = COL_QA + ATTN_WIDTH
COL_VA = COL_KA + KV_WIDTH
COL_QM = COL_VA + KV_WIDTH
COL_KM = COL_QM + MLSTM_WIDTH
COL_VM = COL_KM + MLSTM_WIDTH
COL_OM = COL_VM + MLSTM_WIDTH
COL_G = COL_OM + MLSTM_WIDTH
IN_WIDTH = COL_G + 2 * MLSTM_HEADS
FG_LANE = SUBLANES

PROMPT_BLOCK = 256
QBLOCK = WINDOW
MCHUNK = 128
PROJ_CHUNK = 256
FFN_BLOCK = 512
FFN_CHUNK = 256
SAMPLE_BATCH_BLOCK = 16
SAMPLE_TOKENS = 4
NEG = -1e30

f32 = jnp.float32
bf16 = jnp.bfloat16


def _rms(x, gain):
    return x * lax.rsqrt(jnp.mean(x * x, axis=-1, keepdims=True) + NORM_EPS) * gain


def _segsum64(s, lane):
    for k in (1, 2, 4, 8, 16, 32):
        s = s + jnp.where((lane & k) != 0, pltpu.roll(s, k, 1), pltpu.roll(s, LANES - k, 1))
    return s


def _headnorm_rope(xs, gain, cos, sin_signed, lane):
    ss = _segsum64(xs * xs, lane)
    y = xs * lax.rsqrt(ss * (1.0 / ATTN_HEAD_DIM) + NORM_EPS) * gain
    partner = jnp.where((lane & 32) != 0, pltpu.roll(y, 32, 1), pltpu.roll(y, LANES - 32, 1))
    return y * cos + partner * sin_signed


def _group_sumsq(xs, bd_ref):
    x2 = xs * xs
    hi = x2.astype(bf16)
    lo = (x2 - hi.astype(f32)).astype(bf16)
    return (jnp.dot(hi, bd_ref[...], preferred_element_type=f32)
            + jnp.dot(lo, bd_ref[...], preferred_element_type=f32))


PA_Q = 0
PA_K = PA_Q + ATTN_WIDTH
PA_V = PA_K + KV_HEADS * LANES
PA_WIDTH = PA_V + KV_HEADS * LANES
QUARTER = ATTN_HEAD_DIM // 2
QA_Q = 0
QA_K = QA_Q + 2 * ATTN_WIDTH
QA_V = QA_K + KV_HEADS * LANES
QA_VZ = QA_V + KV_HEADS * LANES
QA_WIDTH = QA_VZ + KV_HEADS * LANES
ZM_WIDTH = 4 * MLSTM_WIDTH
ATTN_PHASES = 4
MLSTM_PHASES = 5


def _norm_rope_quarters(xs, ss, gain, cos, sin_signed):
    y = xs * lax.rsqrt(ss * (1.0 / ATTN_HEAD_DIM) + NORM_EPS) * gain
    return y * cos + pltpu.roll(y, LANES // 2, 1) * sin_signed


def _col_chunks(width):
    return [(c, min(c + PROJ_CHUNK, width)) for c in range(0, width, PROJ_CHUNK)]


def _mixer_proj_jobs(x_ref, anorm_ref, wa_ref, w_ref, wgate_ref, za_ref, zm_ref, gz_ref):
    h = _rms(x_ref[...], anorm_ref[...]).astype(bf16)

    def proj_job(w_src, wc0, z_ref, c0, c1):
        def run():
            z_ref[:, c0:c1] = jnp.dot(h, w_src[:, wc0 + c0:wc0 + c1], preferred_element_type=f32)
        return run

    return ([proj_job(wa_ref, 0, za_ref, c0, c1) for c0, c1 in _col_chunks(PA_WIDTH)]
            + [proj_job(w_ref, COL_QM, zm_ref, c0, c1) for c0, c1 in _col_chunks(ZM_WIDTH)]
            + [proj_job(wgate_ref, 0, gz_ref, 0, LANES)])


def _mixer_prep_jobs(za_ref, gz_ref, cos_ref, sin_ref, bd_ref, qg_ref, kg_ref, gbias_ref, qa_ref,
                     colf_ref, urow_ref, wrow_ref, mst_ref, mout_ref):
    tb = za_ref.shape[0]

    def gates_job():
        lane_t = lax.broadcasted_iota(jnp.int32, (tb, LANES), 1)
        gcol = gz_ref[...] + gbias_ref[...]
        acol = jnp.where(lane_t < FG_LANE, gcol, jax.nn.log_sigmoid(gcol))
        arow = acol.T
        lane8 = lax.broadcasted_iota(jnp.int32, (SUBLANES, LANES), 1)
        lane_in = lane8 & (MCHUNK - 1)
        m_prev = mst_ref[:, 0:1]
        stacks = []
        for sb in range(tb // LANES):
            ls = slice(sb * LANES, (sb + 1) * LANES)
            ig8 = arow[0:SUBLANES, ls]
            lf8 = arow[FG_LANE:FG_LANE + SUBLANES, ls]
            b8 = lf8
            k = 1
            while k < MCHUNK:
                b8 = b8 + jnp.where(lane_in >= k, pltpu.roll(b8, k, 1), 0.0)
                k *= 2
            u8 = ig8 - b8
            cm8 = u8
            k = 1
            while k < MCHUNK:
                cm8 = jnp.maximum(cm8, jnp.where(lane_in >= k, pltpu.roll(cm8, k, 1), NEG))
                k *= 2
            g8 = jnp.zeros_like(u8)
            mp8 = jnp.zeros_like(u8)
            gl8 = jnp.zeros_like(u8)
            for c in range(LANES // MCHUNK):
                in_chunk = (lane8 // MCHUNK) == c
                gc = jnp.maximum(cm8, m_prev)
                last = c * MCHUNK + MCHUNK - 1
                g_last = jnp.max(jnp.where(lane8 == last, gc, NEG), axis=1, keepdims=True)
                b_last = jnp.max(jnp.where(lane8 == last, b8, NEG), axis=1, keepdims=True)
                g8 = jnp.where(in_chunk, gc, g8)
                mp8 = jnp.where(in_chunk, m_prev, mp8)
                gl8 = jnp.where(in_chunk, g_last, gl8)
                m_prev = b_last + g_last
            a8 = jnp.exp(mp8 - g8)
            emt8 = jnp.exp(-(b8 + g8))
            aend8 = jnp.exp(mp8 - gl8)
            stacks.append(jnp.concatenate(
                [g8, a8, emt8, aend8, jnp.zeros((LANES - 4 * SUBLANES, LANES), f32)], axis=0))
            urow_ref[:, ls] = u8
            wrow_ref[:, ls] = jnp.exp(u8 - gl8)
        mst_ref[...] = jnp.broadcast_to(m_prev, mst_ref.shape)
        mout_ref[...] = jnp.broadcast_to(m_prev, mout_ref.shape)
        colf_ref[...] = jnp.concatenate(stacks, axis=1).T

    def rope_job():
        lane = lax.broadcasted_iota(jnp.int32, (tb, LANES), 1)
        head_a = ((lane // QUARTER) & 1) == 0
        row0 = (lax.broadcasted_iota(jnp.int32, (tb, LANES), 0) & (QBLOCK - 1)) == 0
        cos = cos_ref[...]
        sin = sin_ref[...]
        ss = [_group_sumsq(za_ref[:, d * 2 * LANES:(d + 1) * 2 * LANES], bd_ref)
              for d in range(PA_V // (2 * LANES))]
        for j in range(PA_V // LANES):
            is_q = j < ATTN_WIDTH // LANES
            y = _norm_rope_quarters(za_ref[:, j * LANES:(j + 1) * LANES],
                                    ss[j // 2][:, (j % 2) * LANES:(j % 2 + 1) * LANES],
                                    qg_ref[...] if is_q else kg_ref[...], cos, sin)
            if is_q:
                y = y * (ATTN_HEAD_DIM ** -0.5)
                qa_ref[:, QA_Q + 2 * j * LANES:QA_Q + (2 * j + 1) * LANES] = (
                    jnp.where(head_a, y, 0.0).astype(bf16))
                qa_ref[:, QA_Q + (2 * j + 1) * LANES:QA_Q + (2 * j + 2) * LANES] = (
                    jnp.where(head_a, 0.0, y).astype(bf16))
            else:
                c = j - ATTN_WIDTH // LANES
                qa_ref[:, QA_K + c * LANES:QA_K + (c + 1) * LANES] = y.astype(bf16)
        for c in range(KV_HEADS):
            v = za_ref[:, PA_V + c * LANES:PA_V + (c + 1) * LANES]
            qa_ref[:, QA_V + c * LANES:QA_V + (c + 1) * LANES] = v.astype(bf16)
            qa_ref[:, QA_VZ + c * LANES:QA_VZ + (c + 1) * LANES] = jnp.where(row0, 0.0, v).astype(bf16)

    return [gates_job, rope_job]


def _mixer_outproj_jobs(xs_ref, mix_ref, wout_ref, x1_ref):
    def job(c0, c1):
        def run():
            x1_ref[:, c0:c1] = xs_ref[:, c0:c1] + jnp.dot(mix_ref[...], wout_ref[:, c0:c1],
                                                          preferred_element_type=f32)
        return run

    return [job(c0, c1) for c0, c1 in _col_chunks(D_MODEL)]


def _mixer_window_out(xs_ref, anorm_ref, w_ref, kgs_ref, coss_ref, sins_ref, kout_ref, vout_ref):
    tb = xs_ref.shape[0]
    h = _rms(xs_ref[tb - WINDOW:, :], anorm_ref[...]).astype(bf16)
    zs = jnp.dot(h, w_ref[:, COL_KA:COL_QM], preferred_element_type=f32)
    lane_s = lax.broadcasted_iota(jnp.int32, (WINDOW, LANES), 1)
    kout_ref[...] = _headnorm_rope(zs[:, :KV_WIDTH], kgs_ref[...], coss_ref[...], sins_ref[...], lane_s)
    vout_ref[...] = zs[:, KV_WIDTH:]


def _mixer_core(first_block, last_block, fillers, prep_jobs, qa_ref, zm_ref, colf_ref, urow_ref, wrow_ref,
                mix_ref, bias_ref, mnorm_ref, kprev_ref, vprev_ref, cst_ref, cext_ref):
    tb = qa_ref.shape[0]
    nqb = tb // QBLOCK
    nch = tb // MCHUNK
    fillers = list(fillers)
    n_fill = len(fillers)
    slots = len(prep_jobs) + ATTN_PHASES + nch * MLSTM_PHASES
    progress = [0]

    def fill():
        progress[0] += 1
        while n_fill - len(fillers) < min(n_fill, -(-n_fill * progress[0] // slots)):
            fillers.pop(0)()

    low_half = lax.broadcasted_iota(jnp.int32, (QBLOCK, LANES), 1) < ATTN_HEAD_DIM
    qi = lax.broadcasted_iota(jnp.int32, (ATTN_GROUP * QBLOCK, 2 * QBLOCK), 0) & (QBLOCK - 1)
    kj = lax.broadcasted_iota(jnp.int32, (ATTN_GROUP * QBLOCK, 2 * QBLOCK), 1)
    band = (kj > qi) & (kj <= qi + QBLOCK)
    ones_slab = jnp.ones((2 * QBLOCK, LANES), bf16)

    def attn_phases(chains):
        rows_of = lambda qb: slice(qb * QBLOCK, (qb + 1) * QBLOCK)
        st = {}

        def scores():
            for qb, c in chains:
                rows = rows_of(qb)
                kcols = slice(QA_K + c * LANES, QA_K + (c + 1) * LANES)
                if qb == 0:
                    kprev, vprev = kprev_ref[c], vprev_ref[c]
                else:
                    kprev = qa_ref[rows_of(qb - 1), kcols]
                    vprev = qa_ref[rows_of(qb - 1), QA_VZ + c * LANES:QA_VZ + (c + 1) * LANES]
                kcat = jnp.concatenate([kprev, qa_ref[rows, kcols]], axis=0)
                vcat = jnp.concatenate(
                    [vprev, qa_ref[rows, QA_V + c * LANES:QA_V + (c + 1) * LANES]], axis=0)
                st['vext', qb, c] = jnp.concatenate([vcat, ones_slab], axis=1)
                q0 = QA_Q + c * ATTN_GROUP * LANES
                qst = jnp.concatenate([qa_ref[rows, q0 + g * LANES:q0 + (g + 1) * LANES]
                                       for g in range(ATTN_GROUP)], axis=0)
                st['s', qb, c] = lax.dot_general(qst, kcat, (((1,), (1,)), ((), ())),
                                                 preferred_element_type=f32)

        def softmax():
            for qb, c in chains:
                valid = band & (kj >= QBLOCK) if (first_block and qb == 0) else band
                s = jnp.where(valid, st.pop(('s', qb, c)), bias_ref[c])
                st['p', qb, c] = jnp.exp(s - jnp.max(s, axis=-1, keepdims=True)).astype(bf16)

        def values():
            for qb, c in chains:
                st['of', qb, c] = jnp.dot(st.pop(('p', qb, c)), st.pop(('vext', qb, c)),
                                          preferred_element_type=f32)

        def normalise():
            for qb, c in chains:
                of = st.pop(('of', qb, c))
                o = of[:, :LANES] / of[:, LANES:]
                for jj in range(2):
                    pair = jnp.where(low_half, o[(2 * jj) * QBLOCK:(2 * jj + 1) * QBLOCK],
                                     o[(2 * jj + 1) * QBLOCK:(2 * jj + 2) * QBLOCK])
                    col = (2 * c + jj) * LANES
                    mix_ref[rows_of(qb), col:col + LANES] = pair.astype(bf16)

        return [scores, softmax, values, normalise]

    ti = lax.broadcasted_iota(jnp.int32, (MCHUNK, MCHUNK), 0)
    si = lax.broadcasted_iota(jnp.int32, (MCHUNK, MCHUNK), 1)
    causal = si <= ti
    ones_l = jnp.ones((MCHUNK, LANES), bf16)

    cexts = [cst_ref[hd] for hd in range(MLSTM_HEADS)]

    def mlstm_phases(c):
        rows = slice(c * MCHUNK, (c + 1) * MCHUNK)
        heads = range(MLSTM_HEADS)
        hcols = lambda k, hd: slice((k * MLSTM_HEADS + hd) * MLSTM_HEAD_DIM,
                                    (k * MLSTM_HEADS + hd + 1) * MLSTM_HEAD_DIM)
        st = {}

        def scores():
            for hd in heads:
                st['q', hd] = zm_ref[rows, hcols(0, hd)].astype(bf16)
                st['k', hd] = zm_ref[rows, hcols(1, hd)] * (MLSTM_HEAD_DIM ** -0.5)
                st['s', hd] = lax.dot_general(st['q', hd], st['k', hd].astype(bf16),
                                              (((1,), (1,)), ((), ())), preferred_element_type=f32)

        def decay():
            for hd in heads:
                g_c = colf_ref[rows, hd:hd + 1]
                u_r = urow_ref[hd:hd + 1, rows]
                dmat = jnp.exp(jnp.where(causal, u_r - g_c, NEG))
                st['s', hd] = (st['s', hd] * dmat).astype(bf16)

        def readout():
            for hd in heads:
                a_c = colf_ref[rows, SUBLANES + hd:SUBLANES + hd + 1]
                vext = jnp.concatenate([zm_ref[rows, hcols(2, hd)].astype(bf16), ones_l], axis=1)
                st['v', hd] = vext
                st['nd', hd] = (
                    a_c * jnp.dot(st.pop(('q', hd)), cexts[hd].astype(bf16), preferred_element_type=f32)
                    + jnp.dot(st.pop(('s', hd)), vext, preferred_element_type=f32))

        def emit():
            for hd in heads:
                nd = st.pop(('nd', hd))
                emt_c = colf_ref[rows, 2 * SUBLANES + hd:2 * SUBLANES + hd + 1]
                hraw = nd[:, :MLSTM_HEAD_DIM] / jnp.maximum(jnp.abs(nd[:, MLSTM_HEAD_DIM:]), emt_c)
                hn = _rms(hraw, mnorm_ref[hd:hd + 1, :])
                og = zm_ref[rows, hcols(3, hd)]
                mix_ref[rows, ATTN_WIDTH + hd * MLSTM_HEAD_DIM:ATTN_WIDTH + (hd + 1) * MLSTM_HEAD_DIM] = (
                    (hn * jax.nn.sigmoid(og)).astype(bf16))

        def update():
            for hd in heads:
                aend = colf_ref[c * MCHUNK:c * MCHUNK + 1, 3 * SUBLANES + hd:3 * SUBLANES + hd + 1]
                w_r = wrow_ref[hd:hd + 1, rows]
                kw_t = (st.pop(('k', hd)).T * w_r).astype(bf16)
                cexts[hd] = aend * cexts[hd] + jnp.dot(kw_t, st.pop(('v', hd)),
                                                       preferred_element_type=f32)

        return [scores, decay, readout, emit, update]

    gates_job, rope_jobs = prep_jobs[0], prep_jobs[1:]
    attn = attn_phases([(qb, c) for qb in range(nqb) for c in range(KV_HEADS)])
    chunks = [mlstm_phases(c) for c in range(nch)]
    mlstm = []
    for c, ph in enumerate(chunks):
        ahead = chunks[c + 1][:2] if c + 1 < nch else []
        mlstm += (ph if c == 0 else ph[2:])[:-2] + ahead + ph[-2:]
    assert len(attn) == ATTN_PHASES and len(mlstm) == nch * MLSTM_PHASES
    order = rope_jobs + attn[:1] + [gates_job]
    rest = attn[1:]
    for i in range(max(len(rest), len(mlstm))):
        order += rest[i:i + 1] + mlstm[i:i + 1]
    assert len(order) == slots
    for job in order:
        job()
        fill()
    last_rows = slice(tb - QBLOCK, tb)
    for c in range(KV_HEADS):
        kprev_ref[c] = qa_ref[last_rows, QA_K + c * LANES:QA_K + (c + 1) * LANES]
        vprev_ref[c] = qa_ref[last_rows, QA_VZ + c * LANES:QA_VZ + (c + 1) * LANES]
    for hd in range(MLSTM_HEADS):
        cst_ref[hd] = cexts[hd]
        if last_block:
            cext_ref[hd] = cexts[hd]


def _prompt_mixer_kernel(x_ref, cos_ref, sin_ref, wa_ref, w_ref, wgate_ref, wout_ref, bd_ref, bias_ref,
                         anorm_ref, qg_ref, kg_ref, kgs_ref, coss_ref, sins_ref, gbias_ref, mnorm_ref,
                         x1_ref, kout_ref, vout_ref, cext_ref, mout_ref,
                         za0, za1, zm0, zm1, gz0, gz1, mix0, mix1, xs0, xs1,
                         qa_ref, colf_ref, urow_ref, wrow_ref, kprev_ref, vprev_ref, cst_ref, mst_ref):
    step = pl.program_id(0)
    nblk = pl.num_programs(0) - 2
    za, zm, gz, mix, xs = (za0, za1), (zm0, zm1), (gz0, gz1), (mix0, mix1), (xs0, xs1)

    def run(parity, do_in, do_core, do_out, first_block=False, last_block=False):
        other = 1 - parity
        jobs = []
        if do_out:
            jobs += _mixer_outproj_jobs(xs[parity], mix[parity], wout_ref, x1_ref)
        if do_in:
            jobs += _mixer_proj_jobs(x_ref, anorm_ref, wa_ref, w_ref, wgate_ref, za[parity], zm[parity],
                                     gz[parity])
        if do_core:
            prep = _mixer_prep_jobs(za[other], gz[other], cos_ref, sin_ref, bd_ref, qg_ref, kg_ref,
                                    gbias_ref, qa_ref, colf_ref, urow_ref, wrow_ref, mst_ref, mout_ref)
            _mixer_core(first_block, last_block, jobs, prep, qa_ref, zm[other], colf_ref, urow_ref,
                        wrow_ref, mix[other], bias_ref, mnorm_ref, kprev_ref, vprev_ref, cst_ref, cext_ref)
        else:
            for job in jobs:
                job()
        if last_block:
            _mixer_window_out(xs[other], anorm_ref, w_ref, kgs_ref, coss_ref, sins_ref, kout_ref, vout_ref)
        if do_in:
            xs[parity][...] = x_ref[...]

    @pl.when(step == 0)
    def _first():
        kprev_ref[...] = jnp.zeros_like(kprev_ref)
        vprev_ref[...] = jnp.zeros_like(vprev_ref)
        cst_ref[...] = jnp.zeros_like(cst_ref)
        mst_ref[...] = jnp.zeros_like(mst_ref)
        run(0, True, False, False)

    @pl.when(step == 1)
    def _second():
        run(1, True, True, False, first_block=True)

    steady = (step >= 2) & (step < nblk)

    @pl.when(steady & (step % 2 == 0))
    def _even():
        run(0, True, True, True)

    @pl.when(steady & (step % 2 == 1))
    def _odd():
        run(1, True, True, True)

    @pl.when(step == nblk)
    def _drain_core():
        run(0, False, True, True, last_block=True)

    @pl.when(step == nblk + 1)
    def _drain_out():
        run(1, False, False, True)


def _const_spec(shape, single=False):
    nd = len(shape)
    if single:
        return pl.BlockSpec(shape, lambda i, *_: (0,) * nd, pipeline_mode=pl.Buffered(1))
    return pl.BlockSpec(shape, lambda i, *_: (0,) * nd)


def _prompt_mixer(x, cos, sin, wa, w, wgate, wout_b, bd, bias, anorm, qg, kg, kgs, coss, sins, gbias, mnorm):
    t = x.shape[0]
    tb = PROMPT_BLOCK
    nblk = t // tb
    assert nblk % 2 == 0 and nblk >= 4
    state_shape = (MLSTM_HEADS, MLSTM_HEAD_DIM, 2 * MLSTM_HEAD_DIM)
    last = nblk - 1
    lag = lambda d: (lambda i: (jnp.clip(i - d, 0, last), 0))
    return pl.pallas_call(
        _prompt_mixer_kernel,
        grid=(nblk + 2,),
        in_specs=[
            pl.BlockSpec((tb, D_MODEL), lag(0)),
            pl.BlockSpec((tb, LANES), lag(1)),
            pl.BlockSpec((tb, LANES), lag(1)),
            _const_spec((D_MODEL, PA_WIDTH), single=True),
            _const_spec((D_MODEL, IN_WIDTH), single=True),
            _const_spec((D_MODEL, LANES), single=True),
            _const_spec((MIX_WIDTH, D_MODEL), single=True),
            _const_spec((2 * LANES, 2 * LANES), single=True),
            _const_spec((KV_HEADS, ATTN_GROUP * QBLOCK, 2 * QBLOCK), single=True),
            _const_spec((1, D_MODEL)),
            _const_spec((1, LANES)),
            _const_spec((1, LANES)),
            _const_spec((1, LANES)),
            _const_spec((WINDOW, LANES)),
            _const_spec((WINDOW, LANES)),
            _const_spec((1, LANES)),
            _const_spec((MLSTM_HEADS, MLSTM_HEAD_DIM)),
        ],
        out_specs=[
            pl.BlockSpec((tb, D_MODEL), lag(2)),
            _const_spec((WINDOW, KV_WIDTH)),
            _const_spec((WINDOW, KV_WIDTH)),
            _const_spec(state_shape),
            _const_spec((SUBLANES, LANES)),
        ],
        out_shape=[
            jax.ShapeDtypeStruct((t, D_MODEL), f32),
            jax.ShapeDtypeStruct((WINDOW, KV_WIDTH), f32),
            jax.ShapeDtypeStruct((WINDOW, KV_WIDTH), f32),
            jax.ShapeDtypeStruct(state_shape, f32),
            jax.ShapeDtypeStruct((SUBLANES, LANES), f32),
        ],
        scratch_shapes=(
            [pltpu.VMEM((tb, PA_WIDTH), f32)] * 2 + [pltpu.VMEM((tb, ZM_WIDTH), f32)] * 2
            + [pltpu.VMEM((tb, LANES), f32)] * 2
            + [pltpu.VMEM((tb, MIX_WIDTH), bf16)] * 2 + [pltpu.VMEM((tb, D_MODEL), f32)] * 2
            + [pltpu.VMEM((tb, QA_WIDTH), bf16), pltpu.VMEM((tb, LANES), f32)]
            + [pltpu.VMEM((SUBLANES, tb), f32)] * 2
            + [pltpu.VMEM((KV_HEADS, WINDOW, LANES), bf16)] * 2
            + [pltpu.VMEM(state_shape, f32), pltpu.VMEM((SUBLANES, LANES), f32)]),
        compiler_params=pltpu.CompilerParams(
            dimension_semantics=("arbitrary",), vmem_limit_bytes=VMEM_LIMIT),
        name="prompt_mixer",
    )(x, cos, sin, wa, w, wgate, wout_b, bd, bias, anorm, qg, kg, kgs, coss, sins, gbias, mnorm)


def _ffn_kernel(xp_ref, xs_ref, g_ref, wg_ref, wu_ref, wd_ref, op_ref, os_ref):
    step = pl.program_id(0)
    last = pl.num_programs(0) - 1

    @pl.when(step < last)
    def _prompt_rows():
        _ffn_rows(xp_ref, g_ref, wg_ref, wu_ref, wd_ref, op_ref)

    @pl.when(step == last)
    def _sample_rows():
        _ffn_rows(xs_ref, g_ref, wg_ref, wu_ref, wd_ref, os_ref)


def _ffn_rows(x_ref, g_ref, wg_ref, wu_ref, wd_ref, o_ref):
    x = x_ref[...]
    hf = _rms(x, g_ref[...]).astype(bf16)
    acc = x
    for c in range(D_FF // FFN_CHUNK):
        cs = slice(c * FFN_CHUNK, (c + 1) * FFN_CHUNK)
        gate = jnp.dot(hf, wg_ref[:, cs], preferred_element_type=f32)
        up = jnp.dot(hf, wu_ref[:, cs], preferred_element_type=f32)
        act = (gate * jax.nn.sigmoid(gate) * up).astype(bf16)
        acc = acc + jnp.dot(act, wd_ref[cs, :], preferred_element_type=f32)
    o_ref[...] = acc


def _ffn(x_p, x_s, fnorm, wg_b, wu_b, wd_b):
    n = x_p.shape[0]
    ns = x_s.shape[0]
    tm = FFN_BLOCK
    last = n // tm - 1
    return pl.pallas_call(
        _ffn_kernel,
        grid=(n // tm + 1,),
        in_specs=[
            pl.BlockSpec((tm, D_MODEL), lambda i: (jnp.minimum(i, last), 0)),
            _const_spec((ns, D_MODEL), single=True),
            _const_spec((1, D_MODEL)),
            _const_spec((D_MODEL, D_FF), single=True),
            _const_spec((D_MODEL, D_FF), single=True),
            _const_spec((D_FF, D_MODEL), single=True),
        ],
        out_specs=[pl.BlockSpec((tm, D_MODEL), lambda i: (jnp.minimum(i, last), 0)),
                   _const_spec((ns, D_MODEL))],
        out_shape=[jax.ShapeDtypeStruct((n, D_MODEL), f32), jax.ShapeDtypeStruct((ns, D_MODEL), f32)],
        compiler_params=pltpu.CompilerParams(
            dimension_semantics=("arbitrary",), vmem_limit_bytes=VMEM_LIMIT),
        name="ffn",
    )(x_p, x_s, fnorm, wg_b, wu_b, wd_b)


def _sample_mixer_kernel(x_ref, ckt_ref, cvt_ref, c_ref, n_ref, m_ref, cos_ref, sin_ref, wqs_ref, w_ref,
                         wgate_ref, wout_ref, sink_ref, bd_ref, anorm_ref, qg_ref, kg_ref, gbias_ref,
                         mnorm_ref, x1_ref, nkt_ref, nvt_ref, cn_ref, nn_ref, mn_ref):
    bb, tpad, _ = x_ref.shape
    nrows = bb * tpad
    nreal = SAMPLE_TOKENS
    h = _rms(x_ref[...].reshape(nrows, D_MODEL), anorm_ref[...]).astype(bf16)
    z = jnp.concatenate(
        [jnp.dot(h, wqs_ref[...], preferred_element_type=f32),
         jnp.dot(h, w_ref[:, COL_KA:COL_G], preferred_element_type=f32),
         jnp.dot(h, wgate_ref[...], preferred_element_type=f32)], axis=1)
    lane = lax.broadcasted_iota(jnp.int32, (nrows, LANES), 1)
    low = lane < ATTN_HEAD_DIM
    cos = cos_ref[...]
    sin = sin_ref[...]

    def per_seq(a):
        return a.reshape(bb, tpad, a.shape[-1])

    def norm_rope(xs, gain):
        y = xs * lax.rsqrt(_group_sumsq(xs, bd_ref) * (1.0 / ATTN_HEAD_DIM) + NORM_EPS) * gain
        partner = jnp.where((lane & QUARTER) != 0, pltpu.roll(y, QUARTER, 1),
                            pltpu.roll(y, LANES - QUARTER, 1))
        return y * cos + partner * sin

    q_rows = []
    for j in range(ATTN_GROUP):
        qs = norm_rope(z[:, COL_QA + j * LANES:COL_QA + (j + 1) * LANES], qg_ref[...])
        qs = qs * (ATTN_HEAD_DIM ** -0.5)
        q_rows.append(per_seq(jnp.where(low, qs, 0.0)).astype(bf16))
        q_rows.append(per_seq(jnp.where(low, 0.0, qs)).astype(bf16))
    qbd = jnp.concatenate(q_rows, axis=1)
    knew = norm_rope(z[:, COL_KA:COL_KA + KV_WIDTH], kg_ref[...])
    vnew = z[:, COL_VA:COL_VA + KV_WIDTH]
    zpad = jnp.zeros((bb, LANES - tpad, LANES), bf16)
    knp = jnp.concatenate([per_seq(knew).astype(bf16), zpad], axis=1)
    vnp = jnp.concatenate([per_seq(vnew).astype(bf16), zpad], axis=1)
    ckt = ckt_ref[...]
    cvt = cvt_ref[...]
    s = jnp.concatenate(
        [jnp.einsum('bqd,bdw->bqw', qbd, ckt.astype(bf16), preferred_element_type=f32),
         jnp.einsum('bqd,bkd->bqk', qbd, knp, preferred_element_type=f32)], axis=2)
    tq = lax.broadcasted_iota(jnp.int32, s.shape, 1) & (tpad - 1)
    kj = lax.broadcasted_iota(jnp.int32, s.shape, 2)
    valid = ((kj < WINDOW) & (kj > tq)) | ((kj >= WINDOW) & (kj - WINDOW <= tq) & (kj - WINDOW < nreal))
    s = jnp.where(valid, s, NEG)
    sink = sink_ref[:, 0:1][None]
    mx = jnp.maximum(jnp.max(s, axis=-1, keepdims=True), sink)
    p = jnp.exp(s - mx)
    den = jnp.sum(p, axis=-1, keepdims=True) + jnp.exp(sink - mx)
    pb = p.astype(bf16)
    o = (jnp.einsum('bqw,bdw->bqd', pb[:, :, :WINDOW], cvt.astype(bf16), preferred_element_type=f32)
         + jnp.einsum('bqk,bkd->bqd', pb[:, :, WINDOW:], vnp, preferred_element_type=f32)) / den
    low3 = lax.broadcasted_iota(jnp.int32, (bb, tpad, LANES), 2) < ATTN_HEAD_DIM
    mix_parts = []
    for j in range(ATTN_GROUP):
        r0 = 2 * j * tpad
        pair = jnp.where(low3, o[:, r0:r0 + tpad, :], o[:, r0 + tpad:r0 + 2 * tpad, :])
        mix_parts.append(pair.reshape(nrows, LANES).astype(bf16))

    keep = lax.broadcasted_iota(jnp.int32, (KV_WIDTH, WINDOW), 1) < WINDOW - nreal
    knt = knew.T
    vnt = vnew.T
    for b in range(bb):
        shift = (WINDOW - nreal - b * tpad) % LANES
        nkt_ref[b] = jnp.where(keep, pltpu.roll(ckt_ref[b], WINDOW - nreal, 1), pltpu.roll(knt, shift, 1))
        nvt_ref[b] = jnp.where(keep, pltpu.roll(cvt_ref[b], WINDOW - nreal, 1), pltpu.roll(vnt, shift, 1))

    gz = per_seq(z[:, COL_G:COL_G + LANES] + gbias_ref[...])
    lgz = jax.nn.log_sigmoid(gz)
    trow = lax.broadcasted_iota(jnp.int32, (bb, tpad, 1), 1)
    real = trow < nreal
    mn_ref[...] = jnp.zeros_like(mn_ref)
    heads = range(MLSTM_HEADS)
    hcols = lambda base, hd: slice(base + hd * MLSTM_HEAD_DIM, base + (hd + 1) * MLSTM_HEAD_DIM)
    last = nreal - 1
    st = {}
    for hd in heads:
        m0 = m_ref[:, hd:hd + 1, :]
        ig_c = jnp.where(real, gz[:, :, hd:hd + 1], NEG)
        lf_c = jnp.where(real, lgz[:, :, FG_LANE + hd:FG_LANE + hd + 1], 0.0)
        b_c = jnp.zeros_like(lf_c)
        for sx in range(nreal):
            b_c = b_c + jnp.where(trow >= sx, lf_c[:, sx:sx + 1, :], 0.0)
        dlog = [jnp.where(trow >= sx, b_c - b_c[:, sx:sx + 1, :] + ig_c[:, sx:sx + 1, :], NEG)
                for sx in range(nreal)]
        inter = b_c + m0
        m_t = inter
        for sx in range(nreal):
            m_t = jnp.maximum(m_t, dlog[sx])
        m_new = m_t[:, last:last + 1, :]
        b_last = b_c[:, last:last + 1, :]
        st[hd] = dict(m_t=m_t, a=jnp.exp(inter - m_t), dexp=[jnp.exp(d - m_t) for d in dlog],
                      m_new=m_new, a_end=jnp.exp(b_last + m0 - m_new),
                      w_c=jnp.exp(b_last - b_c + ig_c - m_new))
    for hd in heads:
        s = st[hd]
        s['q'] = per_seq(z[:, hcols(COL_QM, hd)])
        s['k'] = per_seq(z[:, hcols(COL_KM, hd)]) * (MLSTM_HEAD_DIM ** -0.5)
        s['v'] = per_seq(z[:, hcols(COL_VM, hd)])
        s['qc'] = jnp.einsum('btd,bde->bte', s['q'].astype(bf16), c_ref[:, hd].astype(bf16),
                             preferred_element_type=f32)
    for hd in heads:
        s = st[hd]
        q, k, v = s['q'], s['k'], s['v']
        num = s['a'] * s.pop('qc')
        den_m = s['a'] * jnp.sum(q * n_ref[:, hd:hd + 1, :], axis=2, keepdims=True)
        for sx in range(nreal):
            sd = jnp.sum(q * k[:, sx:sx + 1, :], axis=2, keepdims=True) * s['dexp'][sx]
            num = num + sd * v[:, sx:sx + 1, :]
            den_m = den_m + sd
        hraw = num / jnp.maximum(jnp.abs(den_m), jnp.exp(-s['m_t']))
        hn = _rms(hraw, mnorm_ref[hd:hd + 1, :][None])
        og = per_seq(z[:, hcols(COL_OM, hd)])
        mix_parts.append((hn * jax.nn.sigmoid(og)).reshape(nrows, MLSTM_HEAD_DIM).astype(bf16))
    for hd in heads:
        s = st[hd]
        kw = s['k'] * s['w_c']
        cn_ref[:, hd] = s['a_end'] * c_ref[:, hd] + jnp.einsum(
            'bsd,bse->bde', kw.astype(bf16), s['v'].astype(bf16), preferred_element_type=f32)
        nn_ref[:, hd:hd + 1, :] = s['a_end'] * n_ref[:, hd:hd + 1, :] + jnp.sum(kw, axis=1, keepdims=True)
        mn_ref[:, hd:hd + 1, :] = jnp.broadcast_to(s['m_new'], (bb, 1, LANES))

    mix = jnp.concatenate(mix_parts, axis=1)
    x1 = x_ref[...].reshape(nrows, D_MODEL) + jnp.dot(mix, wout_ref[...], preferred_element_type=f32)
    x1_ref[...] = x1.reshape(bb, tpad, D_MODEL)


def _sample_mixer(x_pad, ckt, cvt, c0, n0, m0, cos, sin, wq_s, w, wgate, wout_s, sink_tile, bd, anorm, qg,
                  kg, gbias, mnorm):
    nb, tpad, _ = x_pad.shape
    bb = SAMPLE_BATCH_BLOCK
    nh = MLSTM_HEADS
    blk = lambda shape: pl.BlockSpec(shape, lambda i: (i,) + (0,) * (len(shape) - 1))
    cblk = (bb, nh, MLSTM_HEAD_DIM, MLSTM_HEAD_DIM)
    return pl.pallas_call(
        _sample_mixer_kernel,
        grid=(nb // bb,),
        in_specs=[blk((bb, tpad, D_MODEL)), blk((bb, KV_WIDTH, WINDOW)), blk((bb, KV_WIDTH, WINDOW)),
                  blk(cblk), blk((bb, nh, MLSTM_HEAD_DIM)), blk((bb, nh, 1)),
                  _const_spec((bb * tpad, LANES)), _const_spec((bb * tpad, LANES)),
                  _const_spec((D_MODEL, ATTN_WIDTH), single=True),
                  _const_spec((D_MODEL, IN_WIDTH), single=True),
                  _const_spec((D_MODEL, LANES), single=True),
                  _const_spec((MIX_WIDTH, D_MODEL), single=True),
                  _const_spec((ATTN_HEADS * tpad, LANES)), _const_spec((LANES, LANES)),
                  _const_spec((1, D_MODEL)),
                  _const_spec((1, LANES)), _const_spec((1, LANES)), _const_spec((1, LANES)),
                  _const_spec((nh, MLSTM_HEAD_DIM))],
        out_specs=[blk((bb, tpad, D_MODEL)), blk((bb, KV_WIDTH, WINDOW)), blk((bb, KV_WIDTH, WINDOW)),
                   blk(cblk), blk((bb, nh, MLSTM_HEAD_DIM)), blk((bb, tpad, LANES))],
        out_shape=[jax.ShapeDtypeStruct((nb, tpad, D_MODEL), f32),
                   jax.ShapeDtypeStruct((nb, KV_WIDTH, WINDOW), f32),
                   jax.ShapeDtypeStruct((nb, KV_WIDTH, WINDOW), f32),
                   jax.ShapeDtypeStruct((nb,) + cblk[1:], f32),
                   jax.ShapeDtypeStruct((nb, nh, MLSTM_HEAD_DIM), f32),
                   jax.ShapeDtypeStruct((nb, tpad, LANES), f32)],
        compiler_params=pltpu.CompilerParams(
            dimension_semantics=("arbitrary",), vmem_limit_bytes=VMEM_LIMIT),
        name="sample_mixer",
    )(x_pad, ckt, cvt, c0, n0, m0, cos, sin, wq_s, w, wgate, wout_s, sink_tile, bd, anorm, qg, kg, gbias,
      mnorm)


def _rope_angles(pos):
    half = ATTN_HEAD_DIM // 2
    inv = ROPE_THETA ** (-np.arange(half, dtype=np.float64) / half)
    ang = pos.astype(np.float64)[:, None] * inv[None, :]
    return np.cos(ang).astype(np.float32), np.sin(ang).astype(np.float32)


def _rope_tables(pos):
    c, s = _rope_angles(pos)
    cos = np.tile(c, (1, LANES // QUARTER))
    sin = np.tile(np.concatenate([-s, s], axis=1), (1, LANES // ATTN_HEAD_DIM))
    return cos, sin


def _rope_tables_quarters(pos):
    c, s = _rope_angles(pos)
    return np.tile(c, (1, LANES // QUARTER)), np.concatenate([-s, -s, s, s], axis=1)


def _quarters(a):
    lo, hi = a[..., :QUARTER], a[..., QUARTER:]
    return jnp.concatenate([lo, lo, hi, hi], axis=-1)


def _prompt_attn_weights(w):
    d = w.shape[0]
    wq = w[:, COL_QA:COL_KA].reshape(d, ATTN_WIDTH // LANES, 2, 2, QUARTER)
    wq = wq.transpose(0, 1, 3, 2, 4).reshape(d, ATTN_WIDTH)
    wk = _quarters(w[:, COL_KA:COL_VA].reshape(d, KV_HEADS, ATTN_HEAD_DIM)).reshape(d, KV_HEADS * LANES)
    wv = w[:, COL_VA:COL_QM].reshape(d, KV_HEADS, 1, ATTN_HEAD_DIM)
    wv = jnp.broadcast_to(wv, (d, KV_HEADS, 2, ATTN_HEAD_DIM)).reshape(d, KV_HEADS * LANES)
    return jnp.concatenate([wq, wk, wv], axis=1)


def kernel(x_prompt, x_sample, cache_k, cache_v, state_C, state_n, state_m, attn_norm, w_in, q_norm,
           k_norm, attn_sinks, b_ig, b_fg, mlstm_norm, w_out, ffn_norm, w_gate, w_up, w_down):
    assert w_in.shape[0] == 1 and x_prompt.shape[0] == 1
    tp = x_prompt.shape[1]
    nb, nt = x_sample.shape[0], x_sample.shape[1]
    assert nt == SAMPLE_TOKENS
    tpad = SUBLANES
    nh = MLSTM_HEADS

    w = w_in[0].astype(bf16)
    pad_a = jnp.zeros((D_MODEL, FG_LANE - nh), bf16)
    pad_b = jnp.zeros((D_MODEL, LANES - FG_LANE - nh), bf16)
    wgate = jnp.concatenate([w[:, COL_G:COL_G + nh], pad_a, w[:, COL_G + nh:], pad_b], axis=1)
    gbias = jnp.concatenate(
        [b_ig[0], jnp.zeros((FG_LANE - nh,), f32), b_fg[0], jnp.zeros((LANES - FG_LANE - nh,), f32)]
    ).reshape(1, LANES)
    wout_b = w_out[0].astype(bf16)
    wg_b = w_gate[0].astype(bf16)
    wu_b = w_up[0].astype(bf16)
    wd_b = w_down[0].astype(bf16)
    anorm = attn_norm[0].reshape(1, D_MODEL)
    fnorm = ffn_norm[0].reshape(1, D_MODEL)
    qg = jnp.tile(q_norm[0], LANES // ATTN_HEAD_DIM).reshape(1, LANES)
    kg = jnp.tile(k_norm[0], LANES // ATTN_HEAD_DIM).reshape(1, LANES)
    mnorm = mlstm_norm[0].reshape(nh, MLSTM_HEAD_DIM)
    sinks = attn_sinks[0]

    wa = _prompt_attn_weights(w)
    idx = np.arange(2 * LANES)
    same = (idx[:, None] // LANES == idx[None, :] // LANES) & (
        (idx[:, None] // QUARTER) % 2 == (idx[None, :] // QUARTER) % 2)
    bd = jnp.asarray(same, dtype=bf16)
    sink_rows_p = jnp.repeat(sinks.reshape(KV_HEADS, ATTN_GROUP), QBLOCK, axis=1)
    bias = jnp.where(jnp.arange(2 * QBLOCK)[None, None, :] == 0, sink_rows_p[:, :, None], NEG)
    qgq = _quarters(q_norm[0]).reshape(1, LANES)
    kgq = _quarters(k_norm[0]).reshape(1, LANES)
    pos_p = np.arange(tp, dtype=np.float32)
    cos_p, sin_p = _rope_tables_quarters(pos_p)
    cos_w, sin_w = _rope_tables(pos_p[tp - WINDOW:])
    x1_p, k_p, v_p, cext_p, m_p = _prompt_mixer(
        x_prompt[0], cos_p, sin_p, wa, w, wgate, wout_b, bd, bias, anorm, qgq, kgq, kg, cos_w, sin_w,
        gbias, mnorm)

    wq_s = w[:, COL_QA:COL_KA].reshape(D_MODEL, KV_HEADS, ATTN_GROUP, ATTN_HEAD_DIM)
    wq_s = wq_s.transpose(0, 2, 1, 3).reshape(D_MODEL, ATTN_WIDTH)
    wo_a = wout_b[:ATTN_WIDTH].reshape(KV_HEADS, ATTN_GROUP, ATTN_HEAD_DIM, D_MODEL)
    wo_a = wo_a.transpose(1, 0, 2, 3).reshape(ATTN_WIDTH, D_MODEL)
    wout_s = jnp.concatenate([wo_a, wout_b[ATTN_WIDTH:]], axis=0)
    sink_tile = jnp.broadcast_to(
        jnp.repeat(sinks.reshape(KV_HEADS, ATTN_GROUP).T.reshape(-1), tpad)[:, None],
        (ATTN_HEADS * tpad, LANES))
    lanes = np.arange(LANES)
    bd_s = jnp.asarray(lanes[:, None] // ATTN_HEAD_DIM == lanes[None, :] // ATTN_HEAD_DIM, dtype=bf16)
    cos_s, sin_s = _rope_tables(np.arange(tpad, dtype=np.float32) + np.float32(PAST_LEN))
    cos_s = np.tile(cos_s, (SAMPLE_BATCH_BLOCK, 1))
    sin_s = np.tile(sin_s, (SAMPLE_BATCH_BLOCK, 1))
    x_pad = jnp.pad(x_sample, ((0, 0), (0, tpad - nt), (0, 0)))
    ckt = cache_k[0].reshape(nb, WINDOW, KV_WIDTH).transpose(0, 2, 1)
    cvt = cache_v[0].reshape(nb, WINDOW, KV_WIDTH).transpose(0, 2, 1)
    x1_pad, nkt, nvt, c_new, n_new, m_pad = _sample_mixer(
        x_pad, ckt, cvt, state_C[0], state_n[0], state_m[0][:, :, None], cos_s, sin_s, wq_s, w, wgate,
        wout_s, sink_tile, bd_s, anorm, qg, kg, gbias, mnorm)
    y_p, y_s = _ffn(x1_p, x1_pad[:, :nt].reshape(nb * nt, D_MODEL), fnorm, wg_b, wu_b, wd_b)
    m_new = m_pad[:, :nh, 0]

    new_k_s = nkt.transpose(0, 2, 1)
    new_v_s = nvt.transpose(0, 2, 1)

    kv_shape = (1, 1, WINDOW, KV_HEADS, ATTN_HEAD_DIM)
    return (
        y_p[None],
        y_s.reshape(nb, nt, D_MODEL),
        k_p.reshape(kv_shape),
        v_p.reshape(kv_shape),
        cext_p[None, None, :, :, :MLSTM_HEAD_DIM],
        cext_p[None, None, :, :, MLSTM_HEAD_DIM],
        m_p[None, None, :nh, 0],
        new_k_s.reshape(1, nb, WINDOW, KV_HEADS, ATTN_HEAD_DIM),
        new_v_s.reshape(1, nb, WINDOW, KV_HEADS, ATTN_HEAD_DIM),
        c_new.reshape(1, nb, nh, MLSTM_HEAD_DIM, MLSTM_HEAD_DIM),
        n_new.reshape(1, nb, nh, MLSTM_HEAD_DIM),
        m_new.reshape(1, nb, nh),
    )
```

```python
import jax
import jax.numpy as jnp
import numpy as np
from jax import lax
from jax.experimental import pallas as pl
from jax.experimental.pallas import tpu as pltpu

D_MODEL = 1024
PAST_LEN = 16384
ATTN_HEADS = 8
KV_HEADS = 2
ATTN_HEAD_DIM = 64
ATTN_GROUP = ATTN_HEADS // KV_HEADS
ATTN_WIDTH = ATTN_HEADS * ATTN_HEAD_DIM
KV_WIDTH = KV_HEADS * ATTN_HEAD_DIM
WINDOW = 128
ROPE_THETA = 10000.0
MLSTM_HEADS = 4
MLSTM_HEAD_DIM = 128
MLSTM_WIDTH = MLSTM_HEADS * MLSTM_HEAD_DIM
MIX_WIDTH = ATTN_WIDTH + MLSTM_WIDTH
D_FF = 2816
NORM_EPS = 1e-6

LANES = 128
SUBLANES = 8
VMEM_LIMIT = 56 * 1024 * 1024

COL_QA = 0
COL_KA = COL_QA + ATTN_WIDTH
COL_VA = COL_KA + KV_WIDTH
COL_QM = COL_VA + KV_WIDTH
COL_KM = COL_QM + MLSTM_WIDTH
COL_VM = COL_KM + MLSTM_WIDTH
COL_OM = COL_VM + MLSTM_WIDTH
COL_G = COL_OM + MLSTM_WIDTH
IN_WIDTH = COL_G + 2 * MLSTM_HEADS
FG_LANE = SUBLANES

PROMPT_BLOCK = 256
QBLOCK = WINDOW
MCHUNK = 128
PROJ_CHUNK = 256
FFN_BLOCK = 512
FFN_CHUNK = 256
SAMPLE_BATCH_BLOCK = 16
SAMPLE_TOKENS = 4
NEG = -1e30

f32 = jnp.float32
bf16 = jnp.bfloat16


def _rms(x, gain):
    return x * lax.rsqrt(jnp.mean(x * x, axis=-1, keepdims=True) + NORM_EPS) * gain


def _segsum64(s, lane):
    for k in (1, 2, 4, 8, 16, 32):
        s = s + jnp.where((lane & k) != 0, pltpu.roll(s, k, 1), pltpu.roll(s, LANES - k, 1))
    return s


def _headnorm_rope(xs, gain, cos, sin_signed, lane):
    ss = _segsum64(xs * xs, lane)
    y = xs * lax.rsqrt(ss * (1.0 / ATTN_HEAD_DIM) + NORM_EPS) * gain
    partner = jnp.where((lane & 32) != 0, pltpu.roll(y, 32, 1), pltpu.roll(y, LANES - 32, 1))
    return y * cos + partner * sin_signed


def _group_sumsq(xs, bd_ref):
    x2 = xs * xs
    hi = x2.astype(bf16)
    lo = (x2 - hi.astype(f32)).astype(bf16)
    return (jnp.dot(hi, bd_ref[...], preferred_element_type=f32)
            + jnp.dot(lo, bd_ref[...], preferred_element_type=f32))


PA_Q = 0
PA_K = PA_Q + ATTN_WIDTH
PA_V = PA_K + KV_HEADS * LANES
PA_WIDTH = PA_V + KV_HEADS * LANES
QUARTER = ATTN_HEAD_DIM // 2
QA_Q = 0
QA_K = QA_Q + 2 * ATTN_WIDTH
QA_V = QA_K + KV_HEADS * LANES
QA_VZ = QA_V + KV_HEADS * LANES
QA_WIDTH = QA_VZ + KV_HEADS * LANES
ZM_WIDTH = 4 * MLSTM_WIDTH
ATTN_PHASES = 4
MLSTM_PHASES = 5


def _norm_rope_quarters(xs, ss, gain, cos, sin_signed):
    y = xs * lax.rsqrt(ss * (1.0 / ATTN_HEAD_DIM) + NORM_EPS) * gain
    return y * cos + pltpu.roll(y, LANES // 2, 1) * sin_signed


def _col_chunks(width):
    return [(c, min(c + PROJ_CHUNK, width)) for c in range(0, width, PROJ_CHUNK)]


def _mixer_proj_jobs(x_ref, anorm_ref, wa_ref, w_ref, wgate_ref, za_ref, zm_ref, gz_ref):
    h = _rms(x_ref[...], anorm_ref[...]).astype(bf16)

    def proj_job(w_src, wc0, z_ref, c0, c1):
        def run():
            z_ref[:, c0:c1] = jnp.dot(h, w_src[:, wc0 + c0:wc0 + c1], preferred_element_type=f32)
        return run

    return ([proj_job(wa_ref, 0, za_ref, c0, c1) for c0, c1 in _col_chunks(PA_WIDTH)]
            + [proj_job(w_ref, COL_QM, zm_ref, c0, c1) for c0, c1 in _col_chunks(ZM_WIDTH)]
            + [proj_job(wgate_ref, 0, gz_ref, 0, LANES)])


def _mixer_prep_jobs(za_ref, gz_ref, cos_ref, sin_ref, bd_ref, qg_ref, kg_ref, gbias_ref, qa_ref,
                     colf_ref, urow_ref, wrow_ref, mst_ref, mout_ref):
    tb = za_ref.shape[0]

    def gates_job():
        lane_t = lax.broadcasted_iota(jnp.int32, (tb, LANES), 1)
        gcol = gz_ref[...] + gbias_ref[...]
        acol = jnp.where(lane_t < FG_LANE, gcol, jax.nn.log_sigmoid(gcol))
        arow = acol.T
        lane8 = lax.broadcasted_iota(jnp.int32, (SUBLANES, LANES), 1)
        nsb = tb // LANES
        slabs = [slice(sb * LANES, (sb + 1) * LANES) for sb in range(nsb)]
        ig_all = jnp.concatenate([arow[0:SUBLANES, ls] for ls in slabs], axis=0)
        b_all = jnp.concatenate([arow[FG_LANE:FG_LANE + SUBLANES, ls] for ls in slabs], axis=0)
        lane_in = lax.broadcasted_iota(jnp.int32, b_all.shape, 1) & (MCHUNK - 1)
        k = 1
        while k < MCHUNK:
            b_all = b_all + jnp.where(lane_in >= k, pltpu.roll(b_all, k, 1), 0.0)
            k *= 2
        u_all = ig_all - b_all
        cm_all = u_all
        k = 1
        while k < MCHUNK:
            cm_all = jnp.maximum(cm_all, jnp.where(lane_in >= k, pltpu.roll(cm_all, k, 1), NEG))
            k *= 2
        m_prev = mst_ref[:, 0:1]
        stacks = []
        for sb, ls in enumerate(slabs):
            sub = slice(sb * SUBLANES, (sb + 1) * SUBLANES)
            b8, u8, cm8 = b_all[sub], u_all[sub], cm_all[sub]
            g8 = jnp.zeros_like(u8)
            mp8 = jnp.zeros_like(u8)
            gl8 = jnp.zeros_like(u8)
            for c in range(LANES // MCHUNK):
                in_chunk = (lane8 // MCHUNK) == c
                gc = jnp.maximum(cm8, m_prev)
                last = c * MCHUNK + MCHUNK - 1
                g_last = jnp.max(jnp.where(lane8 == last, gc, NEG), axis=1, keepdims=True)
                b_last = jnp.max(jnp.where(lane8 == last, b8, NEG), axis=1, keepdims=True)
                g8 = jnp.where(in_chunk, gc, g8)
                mp8 = jnp.where(in_chunk, m_prev, mp8)
                gl8 = jnp.where(in_chunk, g_last, gl8)
                m_prev = b_last + g_last
            a8 = jnp.exp(mp8 - g8)
            emt8 = jnp.exp(-(b8 + g8))
            aend8 = jnp.exp(mp8 - gl8)
            stacks.append(jnp.concatenate(
                [g8, a8, emt8, aend8, jnp.zeros((LANES - 4 * SUBLANES, LANES), f32)], axis=0))
            urow_ref[:, ls] = u8
            wrow_ref[:, ls] = jnp.exp(u8 - gl8)
        mst_ref[...] = jnp.broadcast_to(m_prev, mst_ref.shape)
        mout_ref[...] = jnp.broadcast_to(m_prev, mout_ref.shape)
        colf_ref[...] = jnp.concatenate(stacks, axis=1).T

    def rope_job(qb):
        def run():
            rows = slice(qb * QBLOCK, (qb + 1) * QBLOCK)
            lane = lax.broadcasted_iota(jnp.int32, (QBLOCK, LANES), 1)
            head_a = ((lane // QUARTER) & 1) == 0
            row0 = lax.broadcasted_iota(jnp.int32, (QBLOCK, LANES), 0) == 0
            cos = cos_ref[rows, :]
            sin = sin_ref[rows, :]
            ss = [_group_sumsq(za_ref[rows, d * 2 * LANES:(d + 1) * 2 * LANES], bd_ref)
                  for d in range(PA_V // (2 * LANES))]
            for j in range(PA_V // LANES):
                is_q = j < ATTN_WIDTH // LANES
                y = _norm_rope_quarters(za_ref[rows, j * LANES:(j + 1) * LANES],
                                        ss[j // 2][:, (j % 2) * LANES:(j % 2 + 1) * LANES],
                                        qg_ref[...] if is_q else kg_ref[...], cos, sin)
                if is_q:
                    y = y * (ATTN_HEAD_DIM ** -0.5)
                    qa_ref[rows, QA_Q + 2 * j * LANES:QA_Q + (2 * j + 1) * LANES] = (
                        jnp.where(head_a, y, 0.0).astype(bf16))
                    qa_ref[rows, QA_Q + (2 * j + 1) * LANES:QA_Q + (2 * j + 2) * LANES] = (
                        jnp.where(head_a, 0.0, y).astype(bf16))
                else:
                    c = j - ATTN_WIDTH // LANES
                    qa_ref[rows, QA_K + c * LANES:QA_K + (c + 1) * LANES] = y.astype(bf16)
            for c in range(KV_HEADS):
                v = za_ref[rows, PA_V + c * LANES:PA_V + (c + 1) * LANES]
                qa_ref[rows, QA_V + c * LANES:QA_V + (c + 1) * LANES] = v.astype(bf16)
                qa_ref[rows, QA_VZ + c * LANES:QA_VZ + (c + 1) * LANES] = (
                    jnp.where(row0, 0.0, v).astype(bf16))
        return run

    return [gates_job] + [rope_job(qb) for qb in range(tb // QBLOCK)]


def _mixer_outproj_jobs(xs_ref, mix_ref, wout_ref, x1_ref):
    def job(c0, c1):
        def run():
            x1_ref[:, c0:c1] = xs_ref[:, c0:c1] + jnp.dot(mix_ref[...], wout_ref[:, c0:c1],
                                                          preferred_element_type=f32)
        return run

    return [job(c0, c1) for c0, c1 in _col_chunks(D_MODEL)]


def _mixer_window_out(xs_ref, anorm_ref, w_ref, kgs_ref, coss_ref, sins_ref, kout_ref, vout_ref):
    tb = xs_ref.shape[0]
    h = _rms(xs_ref[tb - WINDOW:, :], anorm_ref[...]).astype(bf16)
    zs = jnp.dot(h, w_ref[:, COL_KA:COL_QM], preferred_element_type=f32)
    lane_s = lax.broadcasted_iota(jnp.int32, (WINDOW, LANES), 1)
    kout_ref[...] = _headnorm_rope(zs[:, :KV_WIDTH], kgs_ref[...], coss_ref[...], sins_ref[...], lane_s)
    vout_ref[...] = zs[:, KV_WIDTH:]


def _mixer_core(first_block, last_block, fillers, prep_jobs, qa_ref, zm_ref, colf_ref, urow_ref, wrow_ref,
                mix_ref, bias_ref, mnorm_ref, kprev_ref, vprev_ref, cst_ref, cext_ref):
    tb = qa_ref.shape[0]
    nqb = tb // QBLOCK
    nch = tb // MCHUNK
    fillers = list(fillers)
    n_fill = len(fillers)
    slots = len(prep_jobs) + ATTN_PHASES + nch * MLSTM_PHASES
    progress = [0]

    def fill():
        progress[0] += 1
        while n_fill - len(fillers) < min(n_fill, -(-n_fill * progress[0] // slots)):
            fillers.pop(0)()

    low_half = lax.broadcasted_iota(jnp.int32, (QBLOCK, LANES), 1) < ATTN_HEAD_DIM
    qi = lax.broadcasted_iota(jnp.int32, (ATTN_GROUP * QBLOCK, 2 * QBLOCK), 0) & (QBLOCK - 1)
    kj = lax.broadcasted_iota(jnp.int32, (ATTN_GROUP * QBLOCK, 2 * QBLOCK), 1)
    band = (kj > qi) & (kj <= qi + QBLOCK)
    ones_slab = jnp.ones((2 * QBLOCK, LANES), bf16)

    def attn_phases(chains):
        rows_of = lambda qb: slice(qb * QBLOCK, (qb + 1) * QBLOCK)
        st = {}

        def scores():
            for qb, c in chains:
                rows = rows_of(qb)
                kcols = slice(QA_K + c * LANES, QA_K + (c + 1) * LANES)
                if qb == 0:
                    kprev, vprev = kprev_ref[c], vprev_ref[c]
                else:
                    kprev = qa_ref[rows_of(qb - 1), kcols]
                    vprev = qa_ref[rows_of(qb - 1), QA_VZ + c * LANES:QA_VZ + (c + 1) * LANES]
                kcat = jnp.concatenate([kprev, qa_ref[rows, kcols]], axis=0)
                vcat = jnp.concatenate(
                    [vprev, qa_ref[rows, QA_V + c * LANES:QA_V + (c + 1) * LANES]], axis=0)
                st['vext', qb, c] = jnp.concatenate([vcat, ones_slab], axis=1)
                q0 = QA_Q + c * ATTN_GROUP * LANES
                qst = jnp.concatenate([qa_ref[rows, q0 + g * LANES:q0 + (g + 1) * LANES]
                                       for g in range(ATTN_GROUP)], axis=0)
                st['s', qb, c] = lax.dot_general(qst, kcat, (((1,), (1,)), ((), ())),
                                                 preferred_element_type=f32)

        def softmax():
            for qb, c in chains:
                valid = band & (kj >= QBLOCK) if (first_block and qb == 0) else band
                s = jnp.where(valid, st.pop(('s', qb, c)), bias_ref[c])
                st['p', qb, c] = jnp.exp(s - jnp.max(s, axis=-1, keepdims=True)).astype(bf16)

        def values():
            for qb, c in chains:
                st['of', qb, c] = jnp.dot(st.pop(('p', qb, c)), st.pop(('vext', qb, c)),
                                          preferred_element_type=f32)

        def normalise():
            for qb, c in chains:
                of = st.pop(('of', qb, c))
                o = of[:, :LANES] / of[:, LANES:]
                for jj in range(2):
                    pair = jnp.where(low_half, o[(2 * jj) * QBLOCK:(2 * jj + 1) * QBLOCK],
                                     o[(2 * jj + 1) * QBLOCK:(2 * jj + 2) * QBLOCK])
                    col = (2 * c + jj) * LANES
                    mix_ref[rows_of(qb), col:col + LANES] = pair.astype(bf16)

        return [scores, softmax, values, normalise]

    ti = lax.broadcasted_iota(jnp.int32, (MCHUNK, MCHUNK), 0)
    si = lax.broadcasted_iota(jnp.int32, (MCHUNK, MCHUNK), 1)
    causal = si <= ti
    ones_l = jnp.ones((MCHUNK, LANES), bf16)

    cexts = [cst_ref[hd] for hd in range(MLSTM_HEADS)]

    def mlstm_phases(c):
        rows = slice(c * MCHUNK, (c + 1) * MCHUNK)
        heads = range(MLSTM_HEADS)
        hcols = lambda k, hd: slice((k * MLSTM_HEADS + hd) * MLSTM_HEAD_DIM,
                                    (k * MLSTM_HEADS + hd + 1) * MLSTM_HEAD_DIM)
        st = {}

        def scores():
            for hd in heads:
                st['q', hd] = zm_ref[rows, hcols(0, hd)].astype(bf16)
                st['k', hd] = zm_ref[rows, hcols(1, hd)] * (MLSTM_HEAD_DIM ** -0.5)
                st['s', hd] = lax.dot_general(st['q', hd], st['k', hd].astype(bf16),
                                              (((1,), (1,)), ((), ())), preferred_element_type=f32)

        def decay():
            for hd in heads:
                g_c = colf_ref[rows, hd:hd + 1]
                u_r = urow_ref[hd:hd + 1, rows]
                dmat = jnp.exp(jnp.where(causal, u_r - g_c, NEG))
                st['s', hd] = (st['s', hd] * dmat).astype(bf16)

        def readout():
            for hd in heads:
                a_c = colf_ref[rows, SUBLANES + hd:SUBLANES + hd + 1]
                vext = jnp.concatenate([zm_ref[rows, hcols(2, hd)].astype(bf16), ones_l], axis=1)
                st['v', hd] = vext
                st['nd', hd] = (
                    a_c * jnp.dot(st.pop(('q', hd)), cexts[hd].astype(bf16), preferred_element_type=f32)
                    + jnp.dot(st.pop(('s', hd)), vext, preferred_element_type=f32))

        def emit():
            for hd in heads:
                nd = st.pop(('nd', hd))
                emt_c = colf_ref[rows, 2 * SUBLANES + hd:2 * SUBLANES + hd + 1]
                hraw = nd[:, :MLSTM_HEAD_DIM] / jnp.maximum(jnp.abs(nd[:, MLSTM_HEAD_DIM:]), emt_c)
                hn = _rms(hraw, mnorm_ref[hd:hd + 1, :])
                og = zm_ref[rows, hcols(3, hd)]
                mix_ref[rows, ATTN_WIDTH + hd * MLSTM_HEAD_DIM:ATTN_WIDTH + (hd + 1) * MLSTM_HEAD_DIM] = (
                    (hn * jax.nn.sigmoid(og)).astype(bf16))

        def update():
            for hd in heads:
                aend = colf_ref[c * MCHUNK:c * MCHUNK + 1, 3 * SUBLANES + hd:3 * SUBLANES + hd + 1]
                w_r = wrow_ref[hd:hd + 1, rows]
                kw_t = (st.pop(('k', hd)).T * w_r).astype(bf16)
                cexts[hd] = aend * cexts[hd] + jnp.dot(kw_t, st.pop(('v', hd)),
                                                       preferred_element_type=f32)

        return [scores, decay, readout, emit, update]

    gates_job, rope_jobs = prep_jobs[0], prep_jobs[1:]
    attn = attn_phases([(qb, c) for qb in range(nqb) for c in range(KV_HEADS)])
    chunks = [mlstm_phases(c) for c in range(nch)]
    mlstm = []
    for c, ph in enumerate(chunks):
        ahead = chunks[c + 1][:2] if c + 1 < nch else []
        mlstm += (ph if c == 0 else ph[2:])[:-2] + ahead + ph[-2:]
    assert len(attn) == ATTN_PHASES and len(mlstm) == nch * MLSTM_PHASES
    order = rope_jobs + attn[:1] + [gates_job]
    rest = attn[1:]
    for i in range(max(len(rest), len(mlstm))):
        order += rest[i:i + 1] + mlstm[i:i + 1]
    assert len(order) == slots
    for job in order:
        job()
        fill()
    last_rows = slice(tb - QBLOCK, tb)
    for c in range(KV_HEADS):
        kprev_ref[c] = qa_ref[last_rows, QA_K + c * LANES:QA_K + (c + 1) * LANES]
        vprev_ref[c] = qa_ref[last_rows, QA_VZ + c * LANES:QA_VZ + (c + 1) * LANES]
    for hd in range(MLSTM_HEADS):
        cst_ref[hd] = cexts[hd]
        if last_block:
            cext_ref[hd] = cexts[hd]


def _prompt_mixer_kernel(x_ref, cos_ref, sin_ref, wa_ref, w_ref, wgate_ref, wout_ref, bd_ref, bias_ref,
                         anorm_ref, qg_ref, kg_ref, kgs_ref, coss_ref, sins_ref, gbias_ref, mnorm_ref,
                         x1_ref, kout_ref, vout_ref, cext_ref, mout_ref,
                         za0, za1, zm0, zm1, gz0, gz1, mix0, mix1, xs0, xs1,
                         qa_ref, colf_ref, urow_ref, wrow_ref, kprev_ref, vprev_ref, cst_ref, mst_ref):
    step = pl.program_id(0)
    nblk = pl.num_programs(0) - 2
    za, zm, gz, mix, xs = (za0, za1), (zm0, zm1), (gz0, gz1), (mix0, mix1), (xs0, xs1)

    def run(parity, do_in, do_core, do_out, first_block=False, last_block=False):
        other = 1 - parity
        jobs = []
        if do_out:
            jobs += _mixer_outproj_jobs(xs[parity], mix[parity], wout_ref, x1_ref)
        if do_in:
            jobs += _mixer_proj_jobs(x_ref, anorm_ref, wa_ref, w_ref, wgate_ref, za[parity], zm[parity],
                                     gz[parity])
        if do_core:
            prep = _mixer_prep_jobs(za[other], gz[other], cos_ref, sin_ref, bd_ref, qg_ref, kg_ref,
                                    gbias_ref, qa_ref, colf_ref, urow_ref, wrow_ref, mst_ref, mout_ref)
            _mixer_core(first_block, last_block, jobs, prep, qa_ref, zm[other], colf_ref, urow_ref,
                        wrow_ref, mix[other], bias_ref, mnorm_ref, kprev_ref, vprev_ref, cst_ref, cext_ref)
        else:
            for job in jobs:
                job()
        if last_block:
            _mixer_window_out(xs[other], anorm_ref, w_ref, kgs_ref, coss_ref, sins_ref, kout_ref, vout_ref)
        if do_in:
            xs[parity][...] = x_ref[...]

    @pl.when(step == 0)
    def _first():
        kprev_ref[...] = jnp.zeros_like(kprev_ref)
        vprev_ref[...] = jnp.zeros_like(vprev_ref)
        cst_ref[...] = jnp.zeros_like(cst_ref)
        mst_ref[...] = jnp.zeros_like(mst_ref)
        run(0, True, False, False)

    @pl.when(step == 1)
    def _second():
        run(1, True, True, False, first_block=True)

    steady = (step >= 2) & (step < nblk)

    @pl.when(steady & (step % 2 == 0))
    def _even():
        run(0, True, True, True)

    @pl.when(steady & (step % 2 == 1))
    def _odd():
        run(1, True, True, True)

    @pl.when(step == nblk)
    def _drain_core():
        run(0, False, True, True, last_block=True)

    @pl.when(step == nblk + 1)
    def _drain_out():
        run(1, False, False, True)


def _const_spec(shape, single=False):
    nd = len(shape)
    if single:
        return pl.BlockSpec(shape, lambda i, *_: (0,) * nd, pipeline_mode=pl.Buffered(1))
    return pl.BlockSpec(shape, lambda i, *_: (0,) * nd)


def _prompt_mixer(x, cos, sin, wa, w, wgate, wout_b, bd, bias, anorm, qg, kg, kgs, coss, sins, gbias, mnorm):
    t = x.shape[0]
    tb = PROMPT_BLOCK
    nblk = t // tb
    assert nblk % 2 == 0 and nblk >= 4
    state_shape = (MLSTM_HEADS, MLSTM_HEAD_DIM, 2 * MLSTM_HEAD_DIM)
    last = nblk - 1
    lag = lambda d: (lambda i: (jnp.clip(i - d, 0, last), 0))
    return pl.pallas_call(
        _prompt_mixer_kernel,
        grid=(nblk + 2,),
        in_specs=[
            pl.BlockSpec((tb, D_MODEL), lag(0)),
            pl.BlockSpec((tb, LANES), lag(1)),
            pl.BlockSpec((tb, LANES), lag(1)),
            _const_spec((D_MODEL, PA_WIDTH), single=True),
            _const_spec((D_MODEL, IN_WIDTH), single=True),
            _const_spec((D_MODEL, LANES), single=True),
            _const_spec((MIX_WIDTH, D_MODEL), single=True),
            _const_spec((2 * LANES, 2 * LANES), single=True),
            _const_spec((KV_HEADS, ATTN_GROUP * QBLOCK, 2 * QBLOCK), single=True),
            _const_spec((1, D_MODEL)),
            _const_spec((1, LANES)),
            _const_spec((1, LANES)),
            _const_spec((1, LANES)),
            _const_spec((WINDOW, LANES)),
            _const_spec((WINDOW, LANES)),
            _const_spec((1, LANES)),
            _const_spec((MLSTM_HEADS, MLSTM_HEAD_DIM)),
        ],
        out_specs=[
            pl.BlockSpec((tb, D_MODEL), lag(2)),
            _const_spec((WINDOW, KV_WIDTH)),
            _const_spec((WINDOW, KV_WIDTH)),
            _const_spec(state_shape),
            _const_spec((SUBLANES, LANES)),
        ],
        out_shape=[
            jax.ShapeDtypeStruct((t, D_MODEL), f32),
            jax.ShapeDtypeStruct((WINDOW, KV_WIDTH), f32),
            jax.ShapeDtypeStruct((WINDOW, KV_WIDTH), f32),
            jax.ShapeDtypeStruct(state_shape, f32),
            jax.ShapeDtypeStruct((SUBLANES, LANES), f32),
        ],
        scratch_shapes=(
            [pltpu.VMEM((tb, PA_WIDTH), f32)] * 2 + [pltpu.VMEM((tb, ZM_WIDTH), f32)] * 2
            + [pltpu.VMEM((tb, LANES), f32)] * 2
            + [pltpu.VMEM((tb, MIX_WIDTH), bf16)] * 2 + [pltpu.VMEM((tb, D_MODEL), f32)] * 2
            + [pltpu.VMEM((tb, QA_WIDTH), bf16), pltpu.VMEM((tb, LANES), f32)]
            + [pltpu.VMEM((SUBLANES, tb), f32)] * 2
            + [pltpu.VMEM((KV_HEADS, WINDOW, LANES), bf16)] * 2
            + [pltpu.VMEM(state_shape, f32), pltpu.VMEM((SUBLANES, LANES), f32)]),
        compiler_params=pltpu.CompilerParams(
            dimension_semantics=("arbitrary",), vmem_limit_bytes=VMEM_LIMIT),
        name="prompt_mixer",
    )(x, cos, sin, wa, w, wgate, wout_b, bd, bias, anorm, qg, kg, kgs, coss, sins, gbias, mnorm)


def _ffn_kernel(xp_ref, xs_ref, g_ref, wg_ref, wu_ref, wd_ref, op_ref, os_ref):
    step = pl.program_id(0)
    last = pl.num_programs(0) - 1

    @pl.when(step < last)
    def _prompt_rows():
        _ffn_rows(xp_ref, g_ref, wg_ref, wu_ref, wd_ref, op_ref)

    @pl.when(step == last)
    def _sample_rows():
        _ffn_rows(xs_ref, g_ref, wg_ref, wu_ref, wd_ref, os_ref)


def _ffn_rows(x_ref, g_ref, wg_ref, wu_ref, wd_ref, o_ref):
    x = x_ref[...]
    hf = _rms(x, g_ref[...]).astype(bf16)
    acc = x
    for c in range(D_FF // FFN_CHUNK):
        cs = slice(c * FFN_CHUNK, (c + 1) * FFN_CHUNK)
        gate = jnp.dot(hf, wg_ref[:, cs], preferred_element_type=f32)
        up = jnp.dot(hf, wu_ref[:, cs], preferred_element_type=f32)
        act = (gate * jax.nn.sigmoid(gate) * up).astype(bf16)
        acc = acc + jnp.dot(act, wd_ref[cs, :], preferred_element_type=f32)
    o_ref[...] = acc


def _ffn(x_p, x_s, fnorm, wg_b, wu_b, wd_b):
    n = x_p.shape[0]
    ns = x_s.shape[0]
    tm = FFN_BLOCK
    last = n // tm - 1
    return pl.pallas_call(
        _ffn_kernel,
        grid=(n // tm + 1,),
        in_specs=[
            pl.BlockSpec((tm, D_MODEL), lambda i: (jnp.minimum(i, last), 0)),
            _const_spec((ns, D_MODEL), single=True),
            _const_spec((1, D_MODEL)),
            _const_spec((D_MODEL, D_FF), single=True),
            _const_spec((D_MODEL, D_FF), single=True),
            _const_spec((D_FF, D_MODEL), single=True),
        ],
        out_specs=[pl.BlockSpec((tm, D_MODEL), lambda i: (jnp.minimum(i, last), 0)),
                   _const_spec((ns, D_MODEL))],
        out_shape=[jax.ShapeDtypeStruct((n, D_MODEL), f32), jax.ShapeDtypeStruct((ns, D_MODEL), f32)],
        compiler_params=pltpu.CompilerParams(
            dimension_semantics=("arbitrary",), vmem_limit_bytes=VMEM_LIMIT),
        name="ffn",
    )(x_p, x_s, fnorm, wg_b, wu_b, wd_b)


def _sample_mixer_kernel(x_ref, ckt_ref, cvt_ref, c_ref, n_ref, m_ref, cos_ref, sin_ref, wqs_ref, w_ref,
                         wgate_ref, wout_ref, sink_ref, bd_ref, anorm_ref, qg_ref, kg_ref, gbias_ref,
                         mnorm_ref, x1_ref, nkt_ref, nvt_ref, cn_ref, nn_ref, mn_ref):
    bb, tpad, _ = x_ref.shape
    nrows = bb * tpad
    nreal = SAMPLE_TOKENS
    h = _rms(x_ref[...].reshape(nrows, D_MODEL), anorm_ref[...]).astype(bf16)
    z = jnp.concatenate(
        [jnp.dot(h, wqs_ref[...], preferred_element_type=f32),
         jnp.dot(h, w_ref[:, COL_KA:COL_G], preferred_element_type=f32),
         jnp.dot(h, wgate_ref[...], preferred_element_type=f32)], axis=1)
    lane = lax.broadcasted_iota(jnp.int32, (nrows, LANES), 1)
    low = lane < ATTN_HEAD_DIM
    cos = cos_ref[...]
    sin = sin_ref[...]

    def per_seq(a):
        return a.reshape(bb, tpad, a.shape[-1])

    def norm_rope(xs, gain):
        y = xs * lax.rsqrt(_group_sumsq(xs, bd_ref) * (1.0 / ATTN_HEAD_DIM) + NORM_EPS) * gain
        partner = jnp.where((lane & QUARTER) != 0, pltpu.roll(y, QUARTER, 1),
                            pltpu.roll(y, LANES - QUARTER, 1))
        return y * cos + partner * sin

    q_rows = []
    for j in range(ATTN_GROUP):
        qs = norm_rope(z[:, COL_QA + j * LANES:COL_QA + (j + 1) * LANES], qg_ref[...])
        qs = qs * (ATTN_HEAD_DIM ** -0.5)
        q_rows.append(per_seq(jnp.where(low, qs, 0.0)).astype(bf16))
        q_rows.append(per_seq(jnp.where(low, 0.0, qs)).astype(bf16))
    qbd = jnp.concatenate(q_rows, axis=1)
    knew = norm_rope(z[:, COL_KA:COL_KA + KV_WIDTH], kg_ref[...])
    vnew = z[:, COL_VA:COL_VA + KV_WIDTH]
    zpad = jnp.zeros((bb, LANES - tpad, LANES), bf16)
    knp = jnp.concatenate([per_seq(knew).astype(bf16), zpad], axis=1)
    vnp = jnp.concatenate([per_seq(vnew).astype(bf16), zpad], axis=1)
    ckt = ckt_ref[...]
    cvt = cvt_ref[...]
    s = jnp.concatenate(
        [jnp.einsum('bqd,bdw->bqw', qbd, ckt.astype(bf16), preferred_element_type=f32),
         jnp.einsum('bqd,bkd->bqk', qbd, knp, preferred_element_type=f32)], axis=2)
    tq = lax.broadcasted_iota(jnp.int32, s.shape, 1) & (tpad - 1)
    kj = lax.broadcasted_iota(jnp.int32, s.shape, 2)
    valid = ((kj < WINDOW) & (kj > tq)) | ((kj >= WINDOW) & (kj - WINDOW <= tq) & (kj - WINDOW < nreal))
    s = jnp.where(valid, s, NEG)
    sink = sink_ref[:, 0:1][None]
    mx = jnp.maximum(jnp.max(s, axis=-1, keepdims=True), sink)
    p = jnp.exp(s - mx)
    den = jnp.sum(p, axis=-1, keepdims=True) + jnp.exp(sink - mx)
    pb = p.astype(bf16)
    o = (jnp.einsum('bqw,bdw->bqd', pb[:, :, :WINDOW], cvt.astype(bf16), preferred_element_type=f32)
         + jnp.einsum('bqk,bkd->bqd', pb[:, :, WINDOW:], vnp, preferred_element_type=f32)) / den
    low3 = lax.broadcasted_iota(jnp.int32, (bb, tpad, LANES), 2) < ATTN_HEAD_DIM
    mix_parts = []
    for j in range(ATTN_GROUP):
        r0 = 2 * j * tpad
        pair = jnp.where(low3, o[:, r0:r0 + tpad, :], o[:, r0 + tpad:r0 + 2 * tpad, :])
        mix_parts.append(pair.reshape(nrows, LANES).astype(bf16))

    keep = lax.broadcasted_iota(jnp.int32, (KV_WIDTH, WINDOW), 1) < WINDOW - nreal
    knt = knew.T
    vnt = vnew.T
    for b in range(bb):
        shift = (WINDOW - nreal - b * tpad) % LANES
        nkt_ref[b] = jnp.where(keep, pltpu.roll(ckt_ref[b], WINDOW - nreal, 1), pltpu.roll(knt, shift, 1))
        nvt_ref[b] = jnp.where(keep, pltpu.roll(cvt_ref[b], WINDOW - nreal, 1), pltpu.roll(vnt, shift, 1))

    gz = per_seq(z[:, COL_G:COL_G + LANES] + gbias_ref[...])
    lgz = jax.nn.log_sigmoid(gz)
    trow = lax.broadcasted_iota(jnp.int32, (bb, tpad, 1), 1)
    real = trow < nreal
    mn_ref[...] = jnp.zeros_like(mn_ref)
    heads = range(MLSTM_HEADS)
    hcols = lambda base, hd: slice(base + hd * MLSTM_HEAD_DIM, base + (hd + 1) * MLSTM_HEAD_DIM)
    last = nreal - 1
    st = {}
    for hd in heads:
        m0 = m_ref[:, hd:hd + 1, :]
        ig_c = jnp.where(real, gz[:, :, hd:hd + 1], NEG)
        lf_c = jnp.where(real, lgz[:, :, FG_LANE + hd:FG_LANE + hd + 1], 0.0)
        b_c = jnp.zeros_like(lf_c)
        for sx in range(nreal):
            b_c = b_c + jnp.where(trow >= sx, lf_c[:, sx:sx + 1, :], 0.0)
        dlog = [jnp.where(trow >= sx, b_c - b_c[:, sx:sx + 1, :] + ig_c[:, sx:sx + 1, :], NEG)
                for sx in range(nreal)]
        inter = b_c + m0
        m_t = inter
        for sx in range(nreal):
            m_t = jnp.maximum(m_t, dlog[sx])
        m_new = m_t[:, last:last + 1, :]
        b_last = b_c[:, last:last + 1, :]
        st[hd] = dict(m_t=m_t, a=jnp.exp(inter - m_t), dexp=[jnp.exp(d - m_t) for d in dlog],
                      m_new=m_new, a_end=jnp.exp(b_last + m0 - m_new),
                      w_c=jnp.exp(b_last - b_c + ig_c - m_new))
    for hd in heads:
        s = st[hd]
        s['q'] = per_seq(z[:, hcols(COL_QM, hd)])
        s['k'] = per_seq(z[:, hcols(COL_KM, hd)]) * (MLSTM_HEAD_DIM ** -0.5)
        s['v'] = per_seq(z[:, hcols(COL_VM, hd)])
        s['qc'] = jnp.einsum('btd,bde->bte', s['q'].astype(bf16), c_ref[:, hd].astype(bf16),
                             preferred_element_type=f32)
    for hd in heads:
        s = st[hd]
        q, k, v = s['q'], s['k'], s['v']
        num = s['a'] * s.pop('qc')
        den_m = s['a'] * jnp.sum(q * n_ref[:, hd:hd + 1, :], axis=2, keepdims=True)
        for sx in range(nreal):
            sd = jnp.sum(q * k[:, sx:sx + 1, :], axis=2, keepdims=True) * s['dexp'][sx]
            num = num + sd * v[:, sx:sx + 1, :]
            den_m = den_m + sd
        hraw = num / jnp.maximum(jnp.abs(den_m), jnp.exp(-s['m_t']))
        hn = _rms(hraw, mnorm_ref[hd:hd + 1, :][None])
        og = per_seq(z[:, hcols(COL_OM, hd)])
        mix_parts.append((hn * jax.nn.sigmoid(og)).reshape(nrows, MLSTM_HEAD_DIM).astype(bf16))
    for hd in heads:
        s = st[hd]
        kw = s['k'] * s['w_c']
        cn_ref[:, hd] = s['a_end'] * c_ref[:, hd] + jnp.einsum(
            'bsd,bse->bde', kw.astype(bf16), s['v'].astype(bf16), preferred_element_type=f32)
        nn_ref[:, hd:hd + 1, :] = s['a_end'] * n_ref[:, hd:hd + 1, :] + jnp.sum(kw, axis=1, keepdims=True)
        mn_ref[:, hd:hd + 1, :] = jnp.broadcast_to(s['m_new'], (bb, 1, LANES))

    mix = jnp.concatenate(mix_parts, axis=1)
    x1 = x_ref[...].reshape(nrows, D_MODEL) + jnp.dot(mix, wout_ref[...], preferred_element_type=f32)
    x1_ref[...] = x1.reshape(bb, tpad, D_MODEL)


def _sample_mixer(x_pad, ckt, cvt, c0, n0, m0, cos, sin, wq_s, w, wgate, wout_s, sink_tile, bd, anorm, qg,
                  kg, gbias, mnorm):
    nb, tpad, _ = x_pad.shape
    bb = SAMPLE_BATCH_BLOCK
    nh = MLSTM_HEADS
    blk = lambda shape: pl.BlockSpec(shape, lambda i: (i,) + (0,) * (len(shape) - 1))
    cblk = (bb, nh, MLSTM_HEAD_DIM, MLSTM_HEAD_DIM)
    return pl.pallas_call(
        _sample_mixer_kernel,
        grid=(nb // bb,),
        in_specs=[blk((bb, tpad, D_MODEL)), blk((bb, KV_WIDTH, WINDOW)), blk((bb, KV_WIDTH, WINDOW)),
                  blk(cblk), blk((bb, nh, MLSTM_HEAD_DIM)), blk((bb, nh, 1)),
                  _const_spec((bb * tpad, LANES)), _const_spec((bb * tpad, LANES)),
                  _const_spec((D_MODEL, ATTN_WIDTH), single=True),
                  _const_spec((D_MODEL, IN_WIDTH), single=True),
                  _const_spec((D_MODEL, LANES), single=True),
                  _const_spec((MIX_WIDTH, D_MODEL), single=True),
                  _const_spec((ATTN_HEADS * tpad, LANES)), _const_spec((LANES, LANES)),
                  _const_spec((1, D_MODEL)),
                  _const_spec((1, LANES)), _const_spec((1, LANES)), _const_spec((1, LANES)),
                  _const_spec((nh, MLSTM_HEAD_DIM))],
        out_specs=[blk((bb, tpad, D_MODEL)), blk((bb, KV_WIDTH, WINDOW)), blk((bb, KV_WIDTH, WINDOW)),
                   blk(cblk), blk((bb, nh, MLSTM_HEAD_DIM)), blk((bb, tpad, LANES))],
        out_shape=[jax.ShapeDtypeStruct((nb, tpad, D_MODEL), f32),
                   jax.ShapeDtypeStruct((nb, KV_WIDTH, WINDOW), f32),
                   jax.ShapeDtypeStruct((nb, KV_WIDTH, WINDOW), f32),
                   jax.ShapeDtypeStruct((nb,) + cblk[1:], f32),
                   jax.ShapeDtypeStruct((nb, nh, MLSTM_HEAD_DIM), f32),
                   jax.ShapeDtypeStruct((nb, tpad, LANES), f32)],
        compiler_params=pltpu.CompilerParams(
            dimension_semantics=("arbitrary",), vmem_limit_bytes=VMEM_LIMIT),
        name="sample_mixer",
    )(x_pad, ckt, cvt, c0, n0, m0, cos, sin, wq_s, w, wgate, wout_s, sink_tile, bd, anorm, qg, kg, gbias,
      mnorm)


def _rope_angles(pos):
    half = ATTN_HEAD_DIM // 2
    inv = ROPE_THETA ** (-np.arange(half, dtype=np.float64) / half)
    ang = pos.astype(np.float64)[:, None] * inv[None, :]
    return np.cos(ang).astype(np.float32), np.sin(ang).astype(np.float32)


def _rope_tables(pos):
    c, s = _rope_angles(pos)
    cos = np.tile(c, (1, LANES // QUARTER))
    sin = np.tile(np.concatenate([-s, s], axis=1), (1, LANES // ATTN_HEAD_DIM))
    return cos, sin


def _rope_tables_quarters(pos):
    c, s = _rope_angles(pos)
    return np.tile(c, (1, LANES // QUARTER)), np.concatenate([-s, -s, s, s], axis=1)


def _quarters(a):
    lo, hi = a[..., :QUARTER], a[..., QUARTER:]
    return jnp.concatenate([lo, lo, hi, hi], axis=-1)


def _prompt_attn_weights(w):
    d = w.shape[0]
    wq = w[:, COL_QA:COL_KA].reshape(d, ATTN_WIDTH // LANES, 2, 2, QUARTER)
    wq = wq.transpose(0, 1, 3, 2, 4).reshape(d, ATTN_WIDTH)
    wk = _quarters(w[:, COL_KA:COL_VA].reshape(d, KV_HEADS, ATTN_HEAD_DIM)).reshape(d, KV_HEADS * LANES)
    wv = w[:, COL_VA:COL_QM].reshape(d, KV_HEADS, 1, ATTN_HEAD_DIM)
    wv = jnp.broadcast_to(wv, (d, KV_HEADS, 2, ATTN_HEAD_DIM)).reshape(d, KV_HEADS * LANES)
    return jnp.concatenate([wq, wk, wv], axis=1)


def kernel(x_prompt, x_sample, cache_k, cache_v, state_C, state_n, state_m, attn_norm, w_in, q_norm,
           k_norm, attn_sinks, b_ig, b_fg, mlstm_norm, w_out, ffn_norm, w_gate, w_up, w_down):
    assert w_in.shape[0] == 1 and x_prompt.shape[0] == 1
    tp = x_prompt.shape[1]
    nb, nt = x_sample.shape[0], x_sample.shape[1]
    assert nt == SAMPLE_TOKENS
    tpad = SUBLANES
    nh = MLSTM_HEADS

    w = w_in[0].astype(bf16)
    pad_a = jnp.zeros((D_MODEL, FG_LANE - nh), bf16)
    pad_b = jnp.zeros((D_MODEL, LANES - FG_LANE - nh), bf16)
    wgate = jnp.concatenate([w[:, COL_G:COL_G + nh], pad_a, w[:, COL_G + nh:], pad_b], axis=1)
    gbias = jnp.concatenate(
        [b_ig[0], jnp.zeros((FG_LANE - nh,), f32), b_fg[0], jnp.zeros((LANES - FG_LANE - nh,), f32)]
    ).reshape(1, LANES)
    wout_b = w_out[0].astype(bf16)
    wg_b = w_gate[0].astype(bf16)
    wu_b = w_up[0].astype(bf16)
    wd_b = w_down[0].astype(bf16)
    anorm = attn_norm[0].reshape(1, D_MODEL)
    fnorm = ffn_norm[0].reshape(1, D_MODEL)
    qg = jnp.tile(q_norm[0], LANES // ATTN_HEAD_DIM).reshape(1, LANES)
    kg = jnp.tile(k_norm[0], LANES // ATTN_HEAD_DIM).reshape(1, LANES)
    mnorm = mlstm_norm[0].reshape(nh, MLSTM_HEAD_DIM)
    sinks = attn_sinks[0]

    wa = _prompt_attn_weights(w)
    idx = np.arange(2 * LANES)
    same = (idx[:, None] // LANES == idx[None, :] // LANES) & (
        (idx[:, None] // QUARTER) % 2 == (idx[None, :] // QUARTER) % 2)
    bd = jnp.asarray(same, dtype=bf16)
    sink_rows_p = jnp.repeat(sinks.reshape(KV_HEADS, ATTN_GROUP), QBLOCK, axis=1)
    bias = jnp.where(jnp.arange(2 * QBLOCK)[None, None, :] == 0, sink_rows_p[:, :, None], NEG)
    qgq = _quarters(q_norm[0]).reshape(1, LANES)
    kgq = _quarters(k_norm[0]).reshape(1, LANES)
    pos_p = np.arange(tp, dtype=np.float32)
    cos_p, sin_p = _rope_tables_quarters(pos_p)
    cos_w, sin_w = _rope_tables(pos_p[tp - WINDOW:])
    x1_p, k_p, v_p, cext_p, m_p = _prompt_mixer(
        x_prompt[0], cos_p, sin_p, wa, w, wgate, wout_b, bd, bias, anorm, qgq, kgq, kg, cos_w, sin_w,
        gbias, mnorm)

    wq_s = w[:, COL_QA:COL_KA].reshape(D_MODEL, KV_HEADS, ATTN_GROUP, ATTN_HEAD_DIM)
    wq_s = wq_s.transpose(0, 2, 1, 3).reshape(D_MODEL, ATTN_WIDTH)
    wo_a = wout_b[:ATTN_WIDTH].reshape(KV_HEADS, ATTN_GROUP, ATTN_HEAD_DIM, D_MODEL)
    wo_a = wo_a.transpose(1, 0, 2, 3).reshape(ATTN_WIDTH, D_MODEL)
    wout_s = jnp.concatenate([wo_a, wout_b[ATTN_WIDTH:]], axis=0)
    sink_tile = jnp.broadcast_to(
        jnp.repeat(sinks.reshape(KV_HEADS, ATTN_GROUP).T.reshape(-1), tpad)[:, None],
        (ATTN_HEADS * tpad, LANES))
    lanes = np.arange(LANES)
    bd_s = jnp.asarray(lanes[:, None] // ATTN_HEAD_DIM == lanes[None, :] // ATTN_HEAD_DIM, dtype=bf16)
    cos_s, sin_s = _rope_tables(np.arange(tpad, dtype=np.float32) + np.float32(PAST_LEN))
    cos_s = np.tile(cos_s, (SAMPLE_BATCH_BLOCK, 1))
    sin_s = np.tile(sin_s, (SAMPLE_BATCH_BLOCK, 1))
    x_pad = jnp.pad(x_sample, ((0, 0), (0, tpad - nt), (0, 0)))
    ckt = cache_k[0].reshape(nb, WINDOW, KV_WIDTH).transpose(0, 2, 1)
    cvt = cache_v[0].reshape(nb, WINDOW, KV_WIDTH).transpose(0, 2, 1)
    x1_pad, nkt, nvt, c_new, n_new, m_pad = _sample_mixer(
        x_pad, ckt, cvt, state_C[0], state_n[0], state_m[0][:, :, None], cos_s, sin_s, wq_s, w, wgate,
        wout_s, sink_tile, bd_s, anorm, qg, kg, gbias, mnorm)
    y_p, y_s = _ffn(x1_p, x1_pad[:, :nt].reshape(nb * nt, D_MODEL), fnorm, wg_b, wu_b, wd_b)
    m_new = m_pad[:, :nh, 0]

    new_k_s = nkt.transpose(0, 2, 1)
    new_v_s = nvt.transpose(0, 2, 1)

    kv_shape = (1, 1, WINDOW, KV_HEADS, ATTN_HEAD_DIM)
    return (
        y_p[None],
        y_s.reshape(nb, nt, D_MODEL),
        k_p.reshape(kv_shape),
        v_p.reshape(kv_shape),
        cext_p[None, None, :, :, :MLSTM_HEAD_DIM],
        cext_p[None, None, :, :, MLSTM_HEAD_DIM],
        m_p[None, None, :nh, 0],
        new_k_s.reshape(1, nb, WINDOW, KV_HEADS, ATTN_HEAD_DIM),
        new_v_s.reshape(1, nb, WINDOW, KV_HEADS, ATTN_HEAD_DIM),
        c_new.reshape(1, nb, nh, MLSTM_HEAD_DIM, MLSTM_HEAD_DIM),
        n_new.reshape(1, nb, nh, MLSTM_HEAD_DIM),
        m_new.reshape(1, nb, nh),
    )
```

```python
import jax
import jax.numpy as jnp
import numpy as np
from jax import lax
from jax.experimental import pallas as pl
from jax.experimental.pallas import tpu as pltpu

D_MODEL = 1024
PAST_LEN = 16384
ATTN_HEADS = 8
KV_HEADS = 2
ATTN_HEAD_DIM = 64
ATTN_GROUP = ATTN_HEADS // KV_HEADS
ATTN_WIDTH = ATTN_HEADS * ATTN_HEAD_DIM
KV_WIDTH = KV_HEADS * ATTN_HEAD_DIM
WINDOW = 128
ROPE_THETA = 10000.0
MLSTM_HEADS = 4
MLSTM_HEAD_DIM = 128
MLSTM_WIDTH = MLSTM_HEADS * MLSTM_HEAD_DIM
MIX_WIDTH = ATTN_WIDTH + MLSTM_WIDTH
D_FF = 2816
NORM_EPS = 1e-6

LANES = 128
SUBLANES = 8
VMEM_LIMIT = 56 * 1024 * 1024

COL_QA = 0
COL_KA = COL_QA + ATTN_WIDTH
COL_VA = COL_KA + KV_WIDTH
COL_QM = COL_VA + KV_WIDTH
COL_KM = COL_QM + MLSTM_WIDTH
COL_VM = COL_KM + MLSTM_WIDTH
COL_OM = COL_VM + MLSTM_WIDTH
COL_G = COL_OM + MLSTM_WIDTH
IN_WIDTH = COL_G + 2 * MLSTM_HEADS
FG_LANE = SUBLANES

PROMPT_BLOCK = 512
QBLOCK = WINDOW
MCHUNK = 128
PROJ_CHUNK = 256
FFN_BLOCK = 512
FFN_CHUNK = 256
SAMPLE_BATCH_BLOCK = 16
SAMPLE_TOKENS = 4
NEG = -1e30

f32 = jnp.float32
bf16 = jnp.bfloat16


def _rms(x, gain):
    return x * lax.rsqrt(jnp.mean(x * x, axis=-1, keepdims=True) + NORM_EPS) * gain


def _segsum64(s, lane):
    for k in (1, 2, 4, 8, 16, 32):
        s = s + jnp.where((lane & k) != 0, pltpu.roll(s, k, 1), pltpu.roll(s, LANES - k, 1))
    return s


def _headnorm_rope(xs, gain, cos, sin_signed, lane):
    ss = _segsum64(xs * xs, lane)
    y = xs * lax.rsqrt(ss * (1.0 / ATTN_HEAD_DIM) + NORM_EPS) * gain
    partner = jnp.where((lane & 32) != 0, pltpu.roll(y, 32, 1), pltpu.roll(y, LANES - 32, 1))
    return y * cos + partner * sin_signed


def _group_sumsq(xs, bd_ref):
    x2 = xs * xs
    hi = x2.astype(bf16)
    lo = (x2 - hi.astype(f32)).astype(bf16)
    return (jnp.dot(hi, bd_ref[...], preferred_element_type=f32)
            + jnp.dot(lo, bd_ref[...], preferred_element_type=f32))


PA_Q = 0
PA_K = PA_Q + ATTN_WIDTH
PA_V = PA_K + KV_HEADS * LANES
PA_WIDTH = PA_V + KV_HEADS * LANES
QUARTER = ATTN_HEAD_DIM // 2
QA_Q = 0
QA_K = QA_Q + 2 * ATTN_WIDTH
QA_V = QA_K + KV_HEADS * LANES
QA_VZ = QA_V + KV_HEADS * LANES
QA_WIDTH = QA_VZ + KV_HEADS * LANES
ZM_WIDTH = 4 * MLSTM_WIDTH
ATTN_PHASES = 4
MLSTM_PHASES = 5


def _norm_rope_quarters(xs, ss, gain, cos, sin_signed):
    y = xs * lax.rsqrt(ss * (1.0 / ATTN_HEAD_DIM) + NORM_EPS) * gain
    return y * cos + pltpu.roll(y, LANES // 2, 1) * sin_signed


def _col_chunks(width):
    return [(c, min(c + PROJ_CHUNK, width)) for c in range(0, width, PROJ_CHUNK)]


def _mixer_proj_jobs(x_ref, anorm_ref, wa_ref, w_ref, wgate_ref, za_ref, zm_ref, gz_ref):
    h = _rms(x_ref[...], anorm_ref[...]).astype(bf16)

    def proj_job(w_src, wc0, z_ref, c0, c1):
        def run():
            z_ref[:, c0:c1] = jnp.dot(h, w_src[:, wc0 + c0:wc0 + c1], preferred_element_type=f32)
        return run

    return ([proj_job(wa_ref, 0, za_ref, c0, c1) for c0, c1 in _col_chunks(PA_WIDTH)]
            + [proj_job(w_ref, COL_QM, zm_ref, c0, c1) for c0, c1 in _col_chunks(ZM_WIDTH)]
            + [proj_job(wgate_ref, 0, gz_ref, 0, LANES)])


def _mixer_prep_jobs(za_ref, gz_ref, cos_ref, sin_ref, bd_ref, qg_ref, kg_ref, gbias_ref, qa_ref,
                     colf_ref, urow_ref, wrow_ref, mst_ref, mout_ref):
    tb = za_ref.shape[0]

    def gates_job():
        lane_t = lax.broadcasted_iota(jnp.int32, (tb, LANES), 1)
        gcol = gz_ref[...] + gbias_ref[...]
        acol = jnp.where(lane_t < FG_LANE, gcol, jax.nn.log_sigmoid(gcol))
        arow = acol.T
        lane8 = lax.broadcasted_iota(jnp.int32, (SUBLANES, LANES), 1)
        nsb = tb // LANES
        slabs = [slice(sb * LANES, (sb + 1) * LANES) for sb in range(nsb)]
        ig_all = jnp.concatenate([arow[0:SUBLANES, ls] for ls in slabs], axis=0)
        b_all = jnp.concatenate([arow[FG_LANE:FG_LANE + SUBLANES, ls] for ls in slabs], axis=0)
        lane_in = lax.broadcasted_iota(jnp.int32, b_all.shape, 1) & (MCHUNK - 1)
        k = 1
        while k < MCHUNK:
            b_all = b_all + jnp.where(lane_in >= k, pltpu.roll(b_all, k, 1), 0.0)
            k *= 2
        u_all = ig_all - b_all
        cm_all = u_all
        k = 1
        while k < MCHUNK:
            cm_all = jnp.maximum(cm_all, jnp.where(lane_in >= k, pltpu.roll(cm_all, k, 1), NEG))
            k *= 2
        m_prev = mst_ref[:, 0:1]
        stacks = []
        for sb, ls in enumerate(slabs):
            sub = slice(sb * SUBLANES, (sb + 1) * SUBLANES)
            b8, u8, cm8 = b_all[sub], u_all[sub], cm_all[sub]
            g8 = jnp.zeros_like(u8)
            mp8 = jnp.zeros_like(u8)
            gl8 = jnp.zeros_like(u8)
            for c in range(LANES // MCHUNK):
                in_chunk = (lane8 // MCHUNK) == c
                gc = jnp.maximum(cm8, m_prev)
                last = c * MCHUNK + MCHUNK - 1
                g_last = jnp.max(jnp.where(lane8 == last, gc, NEG), axis=1, keepdims=True)
                b_last = jnp.max(jnp.where(lane8 == last, b8, NEG), axis=1, keepdims=True)
                g8 = jnp.where(in_chunk, gc, g8)
                mp8 = jnp.where(in_chunk, m_prev, mp8)
                gl8 = jnp.where(in_chunk, g_last, gl8)
                m_prev = b_last + g_last
            a8 = jnp.exp(mp8 - g8)
            emt8 = jnp.exp(-(b8 + g8))
            aend8 = jnp.exp(mp8 - gl8)
            stacks.append(jnp.concatenate(
                [g8, a8, emt8, aend8, jnp.zeros((LANES - 4 * SUBLANES, LANES), f32)], axis=0))
            urow_ref[:, ls] = u8
            wrow_ref[:, ls] = jnp.exp(u8 - gl8)
        mst_ref[...] = jnp.broadcast_to(m_prev, mst_ref.shape)
        mout_ref[...] = jnp.broadcast_to(m_prev, mout_ref.shape)
        colf_ref[...] = jnp.concatenate(stacks, axis=1).T

    def rope_job(qb):
        def run():
            rows = slice(qb * QBLOCK, (qb + 1) * QBLOCK)
            lane = lax.broadcasted_iota(jnp.int32, (QBLOCK, LANES), 1)
            head_a = ((lane // QUARTER) & 1) == 0
            row0 = lax.broadcasted_iota(jnp.int32, (QBLOCK, LANES), 0) == 0
            cos = cos_ref[rows, :]
            sin = sin_ref[rows, :]
            ss = [_group_sumsq(za_ref[rows, d * 2 * LANES:(d + 1) * 2 * LANES], bd_ref)
                  for d in range(PA_V // (2 * LANES))]
            for j in range(PA_V // LANES):
                is_q = j < ATTN_WIDTH // LANES
                y = _norm_rope_quarters(za_ref[rows, j * LANES:(j + 1) * LANES],
                                        ss[j // 2][:, (j % 2) * LANES:(j % 2 + 1) * LANES],
                                        qg_ref[...] if is_q else kg_ref[...], cos, sin)
                if is_q:
                    y = y * (ATTN_HEAD_DIM ** -0.5)
                    qa_ref[rows, QA_Q + 2 * j * LANES:QA_Q + (2 * j + 1) * LANES] = (
                        jnp.where(head_a, y, 0.0).astype(bf16))
                    qa_ref[rows, QA_Q + (2 * j + 1) * LANES:QA_Q + (2 * j + 2) * LANES] = (
                        jnp.where(head_a, 0.0, y).astype(bf16))
                else:
                    c = j - ATTN_WIDTH // LANES
                    qa_ref[rows, QA_K + c * LANES:QA_K + (c + 1) * LANES] = y.astype(bf16)
            for c in range(KV_HEADS):
                v = za_ref[rows, PA_V + c * LANES:PA_V + (c + 1) * LANES]
                qa_ref[rows, QA_V + c * LANES:QA_V + (c + 1) * LANES] = v.astype(bf16)
                qa_ref[rows, QA_VZ + c * LANES:QA_VZ + (c + 1) * LANES] = (
                    jnp.where(row0, 0.0, v).astype(bf16))
        return run

    return [gates_job] + [rope_job(qb) for qb in range(tb // QBLOCK)]


def _mixer_outproj_jobs(xs_ref, mix_ref, wout_ref, x1_ref):
    def job(c0, c1):
        def run():
            x1_ref[:, c0:c1] = xs_ref[:, c0:c1] + jnp.dot(mix_ref[...], wout_ref[:, c0:c1],
                                                          preferred_element_type=f32)
        return run

    return [job(c0, c1) for c0, c1 in _col_chunks(D_MODEL)]


def _mixer_window_out(xs_ref, anorm_ref, w_ref, kgs_ref, coss_ref, sins_ref, kout_ref, vout_ref):
    tb = xs_ref.shape[0]
    h = _rms(xs_ref[tb - WINDOW:, :], anorm_ref[...]).astype(bf16)
    zs = jnp.dot(h, w_ref[:, COL_KA:COL_QM], preferred_element_type=f32)
    lane_s = lax.broadcasted_iota(jnp.int32, (WINDOW, LANES), 1)
    kout_ref[...] = _headnorm_rope(zs[:, :KV_WIDTH], kgs_ref[...], coss_ref[...], sins_ref[...], lane_s)
    vout_ref[...] = zs[:, KV_WIDTH:]


def _mixer_core(first_block, last_block, fillers, prep_jobs, qa_ref, zm_ref, colf_ref, urow_ref, wrow_ref,
                mix_ref, bias_ref, mnorm_ref, kprev_ref, vprev_ref, cst_ref, cext_ref):
    tb = qa_ref.shape[0]
    nqb = tb // QBLOCK
    nch = tb // MCHUNK
    fillers = list(fillers)
    n_fill = len(fillers)
    slots = len(prep_jobs) + ATTN_PHASES + nch * MLSTM_PHASES
    progress = [0]

    def fill():
        progress[0] += 1
        while n_fill - len(fillers) < min(n_fill, -(-n_fill * progress[0] // slots)):
            fillers.pop(0)()

    low_half = lax.broadcasted_iota(jnp.int32, (QBLOCK, LANES), 1) < ATTN_HEAD_DIM
    qi = lax.broadcasted_iota(jnp.int32, (ATTN_GROUP * QBLOCK, 2 * QBLOCK), 0) & (QBLOCK - 1)
    kj = lax.broadcasted_iota(jnp.int32, (ATTN_GROUP * QBLOCK, 2 * QBLOCK), 1)
    band = (kj > qi) & (kj <= qi + QBLOCK)
    ones_slab = jnp.ones((2 * QBLOCK, LANES), bf16)

    def attn_phases(chains):
        rows_of = lambda qb: slice(qb * QBLOCK, (qb + 1) * QBLOCK)
        st = {}

        def scores():
            for qb, c in chains:
                rows = rows_of(qb)
                kcols = slice(QA_K + c * LANES, QA_K + (c + 1) * LANES)
                if qb == 0:
                    kprev, vprev = kprev_ref[c], vprev_ref[c]
                else:
                    kprev = qa_ref[rows_of(qb - 1), kcols]
                    vprev = qa_ref[rows_of(qb - 1), QA_VZ + c * LANES:QA_VZ + (c + 1) * LANES]
                kcat = jnp.concatenate([kprev, qa_ref[rows, kcols]], axis=0)
                vcat = jnp.concatenate(
                    [vprev, qa_ref[rows, QA_V + c * LANES:QA_V + (c + 1) * LANES]], axis=0)
                st['vext', qb, c] = jnp.concatenate([vcat, ones_slab], axis=1)
                q0 = QA_Q + c * ATTN_GROUP * LANES
                qst = jnp.concatenate([qa_ref[rows, q0 + g * LANES:q0 + (g + 1) * LANES]
                                       for g in range(ATTN_GROUP)], axis=0)
                st['s', qb, c] = lax.dot_general(qst, kcat, (((1,), (1,)), ((), ())),
                                                 preferred_element_type=f32)

        def softmax():
            for qb, c in chains:
                valid = band & (kj >= QBLOCK) if (first_block and qb == 0) else band
                s = jnp.where(valid, st.pop(('s', qb, c)), bias_ref[c])
                st['p', qb, c] = jnp.exp(s - jnp.max(s, axis=-1, keepdims=True)).astype(bf16)

        def values():
            for qb, c in chains:
                st['of', qb, c] = jnp.dot(st.pop(('p', qb, c)), st.pop(('vext', qb, c)),
                                          preferred_element_type=f32)

        def normalise():
            for qb, c in chains:
                of = st.pop(('of', qb, c))
                o = of[:, :LANES] / of[:, LANES:]
                for jj in range(2):
                    pair = jnp.where(low_half, o[(2 * jj) * QBLOCK:(2 * jj + 1) * QBLOCK],
                                     o[(2 * jj + 1) * QBLOCK:(2 * jj + 2) * QBLOCK])
                    col = (2 * c + jj) * LANES
                    mix_ref[rows_of(qb), col:col + LANES] = pair.astype(bf16)

        return [scores, softmax, values, normalise]

    ti = lax.broadcasted_iota(jnp.int32, (MCHUNK, MCHUNK), 0)
    si = lax.broadcasted_iota(jnp.int32, (MCHUNK, MCHUNK), 1)
    causal = si <= ti
    ones_l = jnp.ones((MCHUNK, LANES), bf16)

    cexts = [cst_ref[hd] for hd in range(MLSTM_HEADS)]

    def mlstm_phases(c):
        rows = slice(c * MCHUNK, (c + 1) * MCHUNK)
        heads = range(MLSTM_HEADS)
        hcols = lambda k, hd: slice((k * MLSTM_HEADS + hd) * MLSTM_HEAD_DIM,
                                    (k * MLSTM_HEADS + hd + 1) * MLSTM_HEAD_DIM)
        st = {}

        def scores():
            for hd in heads:
                st['q', hd] = zm_ref[rows, hcols(0, hd)].astype(bf16)
                st['k', hd] = zm_ref[rows, hcols(1, hd)] * (MLSTM_HEAD_DIM ** -0.5)
                st['s', hd] = lax.dot_general(st['q', hd], st['k', hd].astype(bf16),
                                              (((1,), (1,)), ((), ())), preferred_element_type=f32)

        def decay():
            for hd in heads:
                g_c = colf_ref[rows, hd:hd + 1]
                u_r = urow_ref[hd:hd + 1, rows]
                dmat = jnp.exp(jnp.where(causal, u_r - g_c, NEG))
                st['s', hd] = (st['s', hd] * dmat).astype(bf16)

        def readout():
            for hd in heads:
                a_c = colf_ref[rows, SUBLANES + hd:SUBLANES + hd + 1]
                vext = jnp.concatenate([zm_ref[rows, hcols(2, hd)].astype(bf16), ones_l], axis=1)
                st['v', hd] = vext
                st['nd', hd] = (
                    a_c * jnp.dot(st.pop(('q', hd)), cexts[hd].astype(bf16), preferred_element_type=f32)
                    + jnp.dot(st.pop(('s', hd)), vext, preferred_element_type=f32))

        def emit():
            for hd in heads:
                nd = st.pop(('nd', hd))
                emt_c = colf_ref[rows, 2 * SUBLANES + hd:2 * SUBLANES + hd + 1]
                hraw = nd[:, :MLSTM_HEAD_DIM] / jnp.maximum(jnp.abs(nd[:, MLSTM_HEAD_DIM:]), emt_c)
                hn = _rms(hraw, mnorm_ref[hd:hd + 1, :])
                og = zm_ref[rows, hcols(3, hd)]
                mix_ref[rows, ATTN_WIDTH + hd * MLSTM_HEAD_DIM:ATTN_WIDTH + (hd + 1) * MLSTM_HEAD_DIM] = (
                    (hn * jax.nn.sigmoid(og)).astype(bf16))

        def update():
            for hd in heads:
                aend = colf_ref[c * MCHUNK:c * MCHUNK + 1, 3 * SUBLANES + hd:3 * SUBLANES + hd + 1]
                w_r = wrow_ref[hd:hd + 1, rows]
                kw_t = (st.pop(('k', hd)).T * w_r).astype(bf16)
                cexts[hd] = aend * cexts[hd] + jnp.dot(kw_t, st.pop(('v', hd)),
                                                       preferred_element_type=f32)

        return [scores, decay, readout, emit, update]

    gates_job, rope_jobs = prep_jobs[0], prep_jobs[1:]
    attn = attn_phases([(qb, c) for qb in range(nqb) for c in range(KV_HEADS)])
    chunks = [mlstm_phases(c) for c in range(nch)]
    mlstm = []
    for c, ph in enumerate(chunks):
        ahead = chunks[c + 1][:2] if c + 1 < nch else []
        mlstm += (ph if c == 0 else ph[2:])[:-2] + ahead + ph[-2:]
    assert len(attn) == ATTN_PHASES and len(mlstm) == nch * MLSTM_PHASES
    order = rope_jobs + attn[:1] + [gates_job]
    rest = attn[1:]
    for i in range(max(len(rest), len(mlstm))):
        order += rest[i:i + 1] + mlstm[i:i + 1]
    assert len(order) == slots
    for job in order:
        job()
        fill()
    last_rows = slice(tb - QBLOCK, tb)
    for c in range(KV_HEADS):
        kprev_ref[c] = qa_ref[last_rows, QA_K + c * LANES:QA_K + (c + 1) * LANES]
        vprev_ref[c] = qa_ref[last_rows, QA_VZ + c * LANES:QA_VZ + (c + 1) * LANES]
    for hd in range(MLSTM_HEADS):
        cst_ref[hd] = cexts[hd]
        if last_block:
            cext_ref[hd] = cexts[hd]


def _prompt_mixer_kernel(x_ref, cos_ref, sin_ref, wa_ref, w_ref, wgate_ref, wout_ref, bd_ref, bias_ref,
                         anorm_ref, qg_ref, kg_ref, kgs_ref, coss_ref, sins_ref, gbias_ref, mnorm_ref,
                         x1_ref, kout_ref, vout_ref, cext_ref, mout_ref,
                         za0, za1, zm0, zm1, gz0, gz1, mix0, mix1, xs0, xs1,
                         qa_ref, colf_ref, urow_ref, wrow_ref, kprev_ref, vprev_ref, cst_ref, mst_ref):
    step = pl.program_id(0)
    nblk = pl.num_programs(0) - 2
    za, zm, gz, mix, xs = (za0, za1), (zm0, zm1), (gz0, gz1), (mix0, mix1), (xs0, xs1)

    def run(parity, do_in, do_core, do_out, first_block=False, last_block=False):
        other = 1 - parity
        jobs = []
        if do_out:
            jobs += _mixer_outproj_jobs(xs[parity], mix[parity], wout_ref, x1_ref)
        if do_in:
            jobs += _mixer_proj_jobs(x_ref, anorm_ref, wa_ref, w_ref, wgate_ref, za[parity], zm[parity],
                                     gz[parity])
        if do_core:
            prep = _mixer_prep_jobs(za[other], gz[other], cos_ref, sin_ref, bd_ref, qg_ref, kg_ref,
                                    gbias_ref, qa_ref, colf_ref, urow_ref, wrow_ref, mst_ref, mout_ref)
            _mixer_core(first_block, last_block, jobs, prep, qa_ref, zm[other], colf_ref, urow_ref,
                        wrow_ref, mix[other], bias_ref, mnorm_ref, kprev_ref, vprev_ref, cst_ref, cext_ref)
        else:
            for job in jobs:
                job()
        if last_block:
            _mixer_window_out(xs[other], anorm_ref, w_ref, kgs_ref, coss_ref, sins_ref, kout_ref, vout_ref)
        if do_in:
            xs[parity][...] = x_ref[...]

    @pl.when(step == 0)
    def _first():
        kprev_ref[...] = jnp.zeros_like(kprev_ref)
        vprev_ref[...] = jnp.zeros_like(vprev_ref)
        cst_ref[...] = jnp.zeros_like(cst_ref)
        mst_ref[...] = jnp.zeros_like(mst_ref)
        run(0, True, False, False)

    @pl.when(step == 1)
    def _second():
        run(1, True, True, False, first_block=True)

    steady = (step >= 2) & (step < nblk)

    @pl.when(steady & (step % 2 == 0))
    def _even():
        run(0, True, True, True)

    @pl.when(steady & (step % 2 == 1))
    def _odd():
        run(1, True, True, True)

    @pl.when(step == nblk)
    def _drain_core():
        run(0, False, True, True, last_block=True)

    @pl.when(step == nblk + 1)
    def _drain_out():
        run(1, False, False, True)


def _const_spec(shape, single=False):
    nd = len(shape)
    if single:
        return pl.BlockSpec(shape, lambda i, *_: (0,) * nd, pipeline_mode=pl.Buffered(1))
    return pl.BlockSpec(shape, lambda i, *_: (0,) * nd)


def _prompt_mixer(x, cos, sin, wa, w, wgate, wout_b, bd, bias, anorm, qg, kg, kgs, coss, sins, gbias, mnorm):
    t = x.shape[0]
    tb = PROMPT_BLOCK
    nblk = t // tb
    assert nblk % 2 == 0 and nblk >= 4
    state_shape = (MLSTM_HEADS, MLSTM_HEAD_DIM, 2 * MLSTM_HEAD_DIM)
    last = nblk - 1
    lag = lambda d: (lambda i: (jnp.clip(i - d, 0, last), 0))
    return pl.pallas_call(
        _prompt_mixer_kernel,
        grid=(nblk + 2,),
        in_specs=[
            pl.BlockSpec((tb, D_MODEL), lag(0)),
            pl.BlockSpec((tb, LANES), lag(1)),
            pl.BlockSpec((tb, LANES), lag(1)),
            _const_spec((D_MODEL, PA_WIDTH), single=True),
            _const_spec((D_MODEL, IN_WIDTH), single=True),
            _const_spec((D_MODEL, LANES), single=True),
            _const_spec((MIX_WIDTH, D_MODEL), single=True),
            _const_spec((2 * LANES, 2 * LANES), single=True),
            _const_spec((KV_HEADS, ATTN_GROUP * QBLOCK, 2 * QBLOCK), single=True),
            _const_spec((1, D_MODEL)),
            _const_spec((1, LANES)),
            _const_spec((1, LANES)),
            _const_spec((1, LANES)),
            _const_spec((WINDOW, LANES)),
            _const_spec((WINDOW, LANES)),
            _const_spec((1, LANES)),
            _const_spec((MLSTM_HEADS, MLSTM_HEAD_DIM)),
        ],
        out_specs=[
            pl.BlockSpec((tb, D_MODEL), lag(2)),
            _const_spec((WINDOW, KV_WIDTH)),
            _const_spec((WINDOW, KV_WIDTH)),
            _const_spec(state_shape),
            _const_spec((SUBLANES, LANES)),
        ],
        out_shape=[
            jax.ShapeDtypeStruct((t, D_MODEL), f32),
            jax.ShapeDtypeStruct((WINDOW, KV_WIDTH), f32),
            jax.ShapeDtypeStruct((WINDOW, KV_WIDTH), f32),
            jax.ShapeDtypeStruct(state_shape, f32),
            jax.ShapeDtypeStruct((SUBLANES, LANES), f32),
        ],
        scratch_shapes=(
            [pltpu.VMEM((tb, PA_WIDTH), f32)] * 2 + [pltpu.VMEM((tb, ZM_WIDTH), f32)] * 2
            + [pltpu.VMEM((tb, LANES), f32)] * 2
            + [pltpu.VMEM((tb, MIX_WIDTH), bf16)] * 2 + [pltpu.VMEM((tb, D_MODEL), f32)] * 2
            + [pltpu.VMEM((tb, QA_WIDTH), bf16), pltpu.VMEM((tb, LANES), f32)]
            + [pltpu.VMEM((SUBLANES, tb), f32)] * 2
            + [pltpu.VMEM((KV_HEADS, WINDOW, LANES), bf16)] * 2
            + [pltpu.VMEM(state_shape, f32), pltpu.VMEM((SUBLANES, LANES), f32)]),
        compiler_params=pltpu.CompilerParams(
            dimension_semantics=("arbitrary",), vmem_limit_bytes=VMEM_LIMIT),
        name="prompt_mixer",
    )(x, cos, sin, wa, w, wgate, wout_b, bd, bias, anorm, qg, kg, kgs, coss, sins, gbias, mnorm)


def _ffn_kernel(xp_ref, xs_ref, g_ref, wg_ref, wu_ref, wd_ref, op_ref, os_ref):
    step = pl.program_id(0)
    last = pl.num_programs(0) - 1

    @pl.when(step < last)
    def _prompt_rows():
        _ffn_rows(xp_ref, g_ref, wg_ref, wu_ref, wd_ref, op_ref)

    @pl.when(step == last)
    def _sample_rows():
        _ffn_rows(xs_ref, g_ref, wg_ref, wu_ref, wd_ref, os_ref)


def _ffn_rows(x_ref, g_ref, wg_ref, wu_ref, wd_ref, o_ref):
    x = x_ref[...]
    hf = _rms(x, g_ref[...]).astype(bf16)
    acc = x
    for c in range(D_FF // FFN_CHUNK):
        cs = slice(c * FFN_CHUNK, (c + 1) * FFN_CHUNK)
        gate = jnp.dot(hf, wg_ref[:, cs], preferred_element_type=f32)
        up = jnp.dot(hf, wu_ref[:, cs], preferred_element_type=f32)
        act = (gate * jax.nn.sigmoid(gate) * up).astype(bf16)
        acc = acc + jnp.dot(act, wd_ref[cs, :], preferred_element_type=f32)
    o_ref[...] = acc


def _ffn(x_p, x_s, fnorm, wg_b, wu_b, wd_b):
    n = x_p.shape[0]
    ns = x_s.shape[0]
    tm = FFN_BLOCK
    last = n // tm - 1
    return pl.pallas_call(
        _ffn_kernel,
        grid=(n // tm + 1,),
        in_specs=[
            pl.BlockSpec((tm, D_MODEL), lambda i: (jnp.minimum(i, last), 0)),
            _const_spec((ns, D_MODEL), single=True),
            _const_spec((1, D_MODEL)),
            _const_spec((D_MODEL, D_FF), single=True),
            _const_spec((D_MODEL, D_FF), single=True),
            _const_spec((D_FF, D_MODEL), single=True),
        ],
        out_specs=[pl.BlockSpec((tm, D_MODEL), lambda i: (jnp.minimum(i, last), 0)),
                   _const_spec((ns, D_MODEL))],
        out_shape=[jax.ShapeDtypeStruct((n, D_MODEL), f32), jax.ShapeDtypeStruct((ns, D_MODEL), f32)],
        compiler_params=pltpu.CompilerParams(
            dimension_semantics=("arbitrary",), vmem_limit_bytes=VMEM_LIMIT),
        name="ffn",
    )(x_p, x_s, fnorm, wg_b, wu_b, wd_b)


def _sample_mixer_kernel(x_ref, ckt_ref, cvt_ref, c_ref, n_ref, m_ref, cos_ref, sin_ref, wqs_ref, w_ref,
                         wgate_ref, wout_ref, sink_ref, bd_ref, anorm_ref, qg_ref, kg_ref, gbias_ref,
                         mnorm_ref, x1_ref, nkt_ref, nvt_ref, cn_ref, nn_ref, mn_ref):
    bb, tpad, _ = x_ref.shape
    nrows = bb * tpad
    nreal = SAMPLE_TOKENS
    h = _rms(x_ref[...].reshape(nrows, D_MODEL), anorm_ref[...]).astype(bf16)
    z = jnp.concatenate(
        [jnp.dot(h, wqs_ref[...], preferred_element_type=f32),
         jnp.dot(h, w_ref[:, COL_KA:COL_G], preferred_element_type=f32),
         jnp.dot(h, wgate_ref[...], preferred_element_type=f32)], axis=1)
    lane = lax.broadcasted_iota(jnp.int32, (nrows, LANES), 1)
    low = lane < ATTN_HEAD_DIM
    cos = cos_ref[...]
    sin = sin_ref[...]

    def per_seq(a):
        return a.reshape(bb, tpad, a.shape[-1])

    def norm_rope(xs, gain):
        y = xs * lax.rsqrt(_group_sumsq(xs, bd_ref) * (1.0 / ATTN_HEAD_DIM) + NORM_EPS) * gain
        partner = jnp.where((lane & QUARTER) != 0, pltpu.roll(y, QUARTER, 1),
                            pltpu.roll(y, LANES - QUARTER, 1))
        return y * cos + partner * sin

    q_rows = []
    for j in range(ATTN_GROUP):
        qs = norm_rope(z[:, COL_QA + j * LANES:COL_QA + (j + 1) * LANES], qg_ref[...])
        qs = qs * (ATTN_HEAD_DIM ** -0.5)
        q_rows.append(per_seq(jnp.where(low, qs, 0.0)).astype(bf16))
        q_rows.append(per_seq(jnp.where(low, 0.0, qs)).astype(bf16))
    qbd = jnp.concatenate(q_rows, axis=1)
    knew = norm_rope(z[:, COL_KA:COL_KA + KV_WIDTH], kg_ref[...])
    vnew = z[:, COL_VA:COL_VA + KV_WIDTH]
    zpad = jnp.zeros((bb, LANES - tpad, LANES), bf16)
    knp = jnp.concatenate([per_seq(knew).astype(bf16), zpad], axis=1)
    vnp = jnp.concatenate([per_seq(vnew).astype(bf16), zpad], axis=1)
    ckt = ckt_ref[...]
    cvt = cvt_ref[...]
    s = jnp.concatenate(
        [jnp.einsum('bqd,bdw->bqw', qbd, ckt.astype(bf16), preferred_element_type=f32),
         jnp.einsum('bqd,bkd->bqk', qbd, knp, preferred_element_type=f32)], axis=2)
    tq = lax.broadcasted_iota(jnp.int32, s.shape, 1) & (tpad - 1)
    kj = lax.broadcasted_iota(jnp.int32, s.shape, 2)
    valid = ((kj < WINDOW) & (kj > tq)) | ((kj >= WINDOW) & (kj - WINDOW <= tq) & (kj - WINDOW < nreal))
    s = jnp.where(valid, s, NEG)
    sink = sink_ref[:, 0:1][None]
    mx = jnp.maximum(jnp.max(s, axis=-1, keepdims=True), sink)
    p = jnp.exp(s - mx)
    den = jnp.sum(p, axis=-1, keepdims=True) + jnp.exp(sink - mx)
    pb = p.astype(bf16)
    o = (jnp.einsum('bqw,bdw->bqd', pb[:, :, :WINDOW], cvt.astype(bf16), preferred_element_type=f32)
         + jnp.einsum('bqk,bkd->bqd', pb[:, :, WINDOW:], vnp, preferred_element_type=f32)) / den
    low3 = lax.broadcasted_iota(jnp.int32, (bb, tpad, LANES), 2) < ATTN_HEAD_DIM
    mix_parts = []
    for j in range(ATTN_GROUP):
        r0 = 2 * j * tpad
        pair = jnp.where(low3, o[:, r0:r0 + tpad, :], o[:, r0 + tpad:r0 + 2 * tpad, :])
        mix_parts.append(pair.reshape(nrows, LANES).astype(bf16))

    keep = lax.broadcasted_iota(jnp.int32, (KV_WIDTH, WINDOW), 1) < WINDOW - nreal
    knt = knew.T
    vnt = vnew.T
    for b in range(bb):
        shift = (WINDOW - nreal - b * tpad) % LANES
        nkt_ref[b] = jnp.where(keep, pltpu.roll(ckt_ref[b], WINDOW - nreal, 1), pltpu.roll(knt, shift, 1))
        nvt_ref[b] = jnp.where(keep, pltpu.roll(cvt_ref[b], WINDOW - nreal, 1), pltpu.roll(vnt, shift, 1))

    gz = per_seq(z[:, COL_G:COL_G + LANES] + gbias_ref[...])
    lgz = jax.nn.log_sigmoid(gz)
    trow = lax.broadcasted_iota(jnp.int32, (bb, tpad, 1), 1)
    real = trow < nreal
    mn_ref[...] = jnp.zeros_like(mn_ref)
    heads = range(MLSTM_HEADS)
    hcols = lambda base, hd: slice(base + hd * MLSTM_HEAD_DIM, base + (hd + 1) * MLSTM_HEAD_DIM)
    last = nreal - 1
    st = {}
    for hd in heads:
        m0 = m_ref[:, hd:hd + 1, :]
        ig_c = jnp.where(real, gz[:, :, hd:hd + 1], NEG)
        lf_c = jnp.where(real, lgz[:, :, FG_LANE + hd:FG_LANE + hd + 1], 0.0)
        b_c = jnp.zeros_like(lf_c)
        for sx in range(nreal):
            b_c = b_c + jnp.where(trow >= sx, lf_c[:, sx:sx + 1, :], 0.0)
        dlog = [jnp.where(trow >= sx, b_c - b_c[:, sx:sx + 1, :] + ig_c[:, sx:sx + 1, :], NEG)
                for sx in range(nreal)]
        inter = b_c + m0
        m_t = inter
        for sx in range(nreal):
            m_t = jnp.maximum(m_t, dlog[sx])
        m_new = m_t[:, last:last + 1, :]
        b_last = b_c[:, last:last + 1, :]
        st[hd] = dict(m_t=m_t, a=jnp.exp(inter - m_t), dexp=[jnp.exp(d - m_t) for d in dlog],
                      m_new=m_new, a_end=jnp.exp(b_last + m0 - m_new),
                      w_c=jnp.exp(b_last - b_c + ig_c - m_new))
    for hd in heads:
        s = st[hd]
        s['q'] = per_seq(z[:, hcols(COL_QM, hd)])
        s['k'] = per_seq(z[:, hcols(COL_KM, hd)]) * (MLSTM_HEAD_DIM ** -0.5)
        s['v'] = per_seq(z[:, hcols(COL_VM, hd)])
        s['qc'] = jnp.einsum('btd,bde->bte', s['q'].astype(bf16), c_ref[:, hd].astype(bf16),
                             preferred_element_type=f32)
    for hd in heads:
        s = st[hd]
        q, k, v = s['q'], s['k'], s['v']
        num = s['a'] * s.pop('qc')
        den_m = s['a'] * jnp.sum(q * n_ref[:, hd:hd + 1, :], axis=2, keepdims=True)
        for sx in range(nreal):
            sd = jnp.sum(q * k[:, sx:sx + 1, :], axis=2, keepdims=True) * s['dexp'][sx]
            num = num + sd * v[:, sx:sx + 1, :]
            den_m = den_m + sd
        hraw = num / jnp.maximum(jnp.abs(den_m), jnp.exp(-s['m_t']))
        hn = _rms(hraw, mnorm_ref[hd:hd + 1, :][None])
        og = per_seq(z[:, hcols(COL_OM, hd)])
        mix_parts.append((hn * jax.nn.sigmoid(og)).reshape(nrows, MLSTM_HEAD_DIM).astype(bf16))
    for hd in heads:
        s = st[hd]
        kw = s['k'] * s['w_c']
        cn_ref[:, hd] = s['a_end'] * c_ref[:, hd] + jnp.einsum(
            'bsd,bse->bde', kw.astype(bf16), s['v'].astype(bf16), preferred_element_type=f32)
        nn_ref[:, hd:hd + 1, :] = s['a_end'] * n_ref[:, hd:hd + 1, :] + jnp.sum(kw, axis=1, keepdims=True)
        mn_ref[:, hd:hd + 1, :] = jnp.broadcast_to(s['m_new'], (bb, 1, LANES))

    mix = jnp.concatenate(mix_parts, axis=1)
    x1 = x_ref[...].reshape(nrows, D_MODEL) + jnp.dot(mix, wout_ref[...], preferred_element_type=f32)
    x1_ref[...] = x1.reshape(bb, tpad, D_MODEL)


def _sample_mixer(x_pad, ckt, cvt, c0, n0, m0, cos, sin, wq_s, w, wgate, wout_s, sink_tile, bd, anorm, qg,
                  kg, gbias, mnorm):
    nb, tpad, _ = x_pad.shape
    bb = SAMPLE_BATCH_BLOCK
    nh = MLSTM_HEADS
    blk = lambda shape: pl.BlockSpec(shape, lambda i: (i,) + (0,) * (len(shape) - 1))
    cblk = (bb, nh, MLSTM_HEAD_DIM, MLSTM_HEAD_DIM)
    return pl.pallas_call(
        _sample_mixer_kernel,
        grid=(nb // bb,),
        in_specs=[blk((bb, tpad, D_MODEL)), blk((bb, KV_WIDTH, WINDOW)), blk((bb, KV_WIDTH, WINDOW)),
                  blk(cblk), blk((bb, nh, MLSTM_HEAD_DIM)), blk((bb, nh, 1)),
                  _const_spec((bb * tpad, LANES)), _const_spec((bb * tpad, LANES)),
                  _const_spec((D_MODEL, ATTN_WIDTH), single=True),
                  _const_spec((D_MODEL, IN_WIDTH), single=True),
                  _const_spec((D_MODEL, LANES), single=True),
                  _const_spec((MIX_WIDTH, D_MODEL), single=True),
                  _const_spec((ATTN_HEADS * tpad, LANES)), _const_spec((LANES, LANES)),
                  _const_spec((1, D_MODEL)),
                  _const_spec((1, LANES)), _const_spec((1, LANES)), _const_spec((1, LANES)),
                  _const_spec((nh, MLSTM_HEAD_DIM))],
        out_specs=[blk((bb, tpad, D_MODEL)), blk((bb, KV_WIDTH, WINDOW)), blk((bb, KV_WIDTH, WINDOW)),
                   blk(cblk), blk((bb, nh, MLSTM_HEAD_DIM)), blk((bb, tpad, LANES))],
        out_shape=[jax.ShapeDtypeStruct((nb, tpad, D_MODEL), f32),
                   jax.ShapeDtypeStruct((nb, KV_WIDTH, WINDOW), f32),
                   jax.ShapeDtypeStruct((nb, KV_WIDTH, WINDOW), f32),
                   jax.ShapeDtypeStruct((nb,) + cblk[1:], f32),
                   jax.ShapeDtypeStruct((nb, nh, MLSTM_HEAD_DIM), f32),
                   jax.ShapeDtypeStruct((nb, tpad, LANES), f32)],
        compiler_params=pltpu.CompilerParams(
            dimension_semantics=("arbitrary",), vmem_limit_bytes=VMEM_LIMIT),
        name="sample_mixer",
    )(x_pad, ckt, cvt, c0, n0, m0, cos, sin, wq_s, w, wgate, wout_s, sink_tile, bd, anorm, qg, kg, gbias,
      mnorm)


def _rope_angles(pos):
    half = ATTN_HEAD_DIM // 2
    inv = ROPE_THETA ** (-np.arange(half, dtype=np.float64) / half)
    ang = pos.astype(np.float64)[:, None] * inv[None, :]
    return np.cos(ang).astype(np.float32), np.sin(ang).astype(np.float32)


def _rope_tables(pos):
    c, s = _rope_angles(pos)
    cos = np.tile(c, (1, LANES // QUARTER))
    sin = np.tile(np.concatenate([-s, s], axis=1), (1, LANES // ATTN_HEAD_DIM))
    return cos, sin


def _rope_tables_quarters(pos):
    c, s = _rope_angles(pos)
    return np.tile(c, (1, LANES // QUARTER)), np.concatenate([-s, -s, s, s], axis=1)


def _quarters(a):
    lo, hi = a[..., :QUARTER], a[..., QUARTER:]
    return jnp.concatenate([lo, lo, hi, hi], axis=-1)


def _prompt_attn_weights(w):
    d = w.shape[0]
    wq = w[:, COL_QA:COL_KA].reshape(d, ATTN_WIDTH // LANES, 2, 2, QUARTER)
    wq = wq.transpose(0, 1, 3, 2, 4).reshape(d, ATTN_WIDTH)
    wk = _quarters(w[:, COL_KA:COL_VA].reshape(d, KV_HEADS, ATTN_HEAD_DIM)).reshape(d, KV_HEADS * LANES)
    wv = w[:, COL_VA:COL_QM].reshape(d, KV_HEADS, 1, ATTN_HEAD_DIM)
    wv = jnp.broadcast_to(wv, (d, KV_HEADS, 2, ATTN_HEAD_DIM)).reshape(d, KV_HEADS * LANES)
    return jnp.concatenate([wq, wk, wv], axis=1)


def kernel(x_prompt, x_sample, cache_k, cache_v, state_C, state_n, state_m, attn_norm, w_in, q_norm,
           k_norm, attn_sinks, b_ig, b_fg, mlstm_norm, w_out, ffn_norm, w_gate, w_up, w_down):
    assert w_in.shape[0] == 1 and x_prompt.shape[0] == 1
    tp = x_prompt.shape[1]
    nb, nt = x_sample.shape[0], x_sample.shape[1]
    assert nt == SAMPLE_TOKENS
    tpad = SUBLANES
    nh = MLSTM_HEADS

    w = w_in[0].astype(bf16)
    pad_a = jnp.zeros((D_MODEL, FG_LANE - nh), bf16)
    pad_b = jnp.zeros((D_MODEL, LANES - FG_LANE - nh), bf16)
    wgate = jnp.concatenate([w[:, COL_G:COL_G + nh], pad_a, w[:, COL_G + nh:], pad_b], axis=1)
    gbias = jnp.concatenate(
        [b_ig[0], jnp.zeros((FG_LANE - nh,), f32), b_fg[0], jnp.zeros((LANES - FG_LANE - nh,), f32)]
    ).reshape(1, LANES)
    wout_b = w_out[0].astype(bf16)
    wg_b = w_gate[0].astype(bf16)
    wu_b = w_up[0].astype(bf16)
    wd_b = w_down[0].astype(bf16)
    anorm = attn_norm[0].reshape(1, D_MODEL)
    fnorm = ffn_norm[0].reshape(1, D_MODEL)
    qg = jnp.tile(q_norm[0], LANES // ATTN_HEAD_DIM).reshape(1, LANES)
    kg = jnp.tile(k_norm[0], LANES // ATTN_HEAD_DIM).reshape(1, LANES)
    mnorm = mlstm_norm[0].reshape(nh, MLSTM_HEAD_DIM)
    sinks = attn_sinks[0]

    wa = _prompt_attn_weights(w)
    idx = np.arange(2 * LANES)
    same = (idx[:, None] // LANES == idx[None, :] // LANES) & (
        (idx[:, None] // QUARTER) % 2 == (idx[None, :] // QUARTER) % 2)
    bd = jnp.asarray(same, dtype=bf16)
    sink_rows_p = jnp.repeat(sinks.reshape(KV_HEADS, ATTN_GROUP), QBLOCK, axis=1)
    bias = jnp.where(jnp.arange(2 * QBLOCK)[None, None, :] == 0, sink_rows_p[:, :, None], NEG)
    qgq = _quarters(q_norm[0]).reshape(1, LANES)
    kgq = _quarters(k_norm[0]).reshape(1, LANES)
    pos_p = np.arange(tp, dtype=np.float32)
    cos_p, sin_p = _rope_tables_quarters(pos_p)
    cos_w, sin_w = _rope_tables(pos_p[tp - WINDOW:])
    x1_p, k_p, v_p, cext_p, m_p = _prompt_mixer(
        x_prompt[0], cos_p, sin_p, wa, w, wgate, wout_b, bd, bias, anorm, qgq, kgq, kg, cos_w, sin_w,
        gbias, mnorm)

    wq_s = w[:, COL_QA:COL_KA].reshape(D_MODEL, KV_HEADS, ATTN_GROUP, ATTN_HEAD_DIM)
    wq_s = wq_s.transpose(0, 2, 1, 3).reshape(D_MODEL, ATTN_WIDTH)
    wo_a = wout_b[:ATTN_WIDTH].reshape(KV_HEADS, ATTN_GROUP, ATTN_HEAD_DIM, D_MODEL)
    wo_a = wo_a.transpose(1, 0, 2, 3).reshape(ATTN_WIDTH, D_MODEL)
    wout_s = jnp.concatenate([wo_a, wout_b[ATTN_WIDTH:]], axis=0)
    sink_tile = jnp.broadcast_to(
        jnp.repeat(sinks.reshape(KV_HEADS, ATTN_GROUP).T.reshape(-1), tpad)[:, None],
        (ATTN_HEADS * tpad, LANES))
    lanes = np.arange(LANES)
    bd_s = jnp.asarray(lanes[:, None] // ATTN_HEAD_DIM == lanes[None, :] // ATTN_HEAD_DIM, dtype=bf16)
    cos_s, sin_s = _rope_tables(np.arange(tpad, dtype=np.float32) + np.float32(PAST_LEN))
    cos_s = np.tile(cos_s, (SAMPLE_BATCH_BLOCK, 1))
    sin_s = np.tile(sin_s, (SAMPLE_BATCH_BLOCK, 1))
    x_pad = jnp.pad(x_sample, ((0, 0), (0, tpad - nt), (0, 0)))
    ckt = cache_k[0].reshape(nb, WINDOW, KV_WIDTH).transpose(0, 2, 1)
    cvt = cache_v[0].reshape(nb, WINDOW, KV_WIDTH).transpose(0, 2, 1)
    x1_pad, nkt, nvt, c_new, n_new, m_pad = _sample_mixer(
        x_pad, ckt, cvt, state_C[0], state_n[0], state_m[0][:, :, None], cos_s, sin_s, wq_s, w, wgate,
        wout_s, sink_tile, bd_s, anorm, qg, kg, gbias, mnorm)
    y_p, y_s = _ffn(x1_p, x1_pad[:, :nt].reshape(nb * nt, D_MODEL), fnorm, wg_b, wu_b, wd_b)
    m_new = m_pad[:, :nh, 0]

    new_k_s = nkt.transpose(0, 2, 1)
    new_v_s = nvt.transpose(0, 2, 1)

    kv_shape = (1, 1, WINDOW, KV_HEADS, ATTN_HEAD_DIM)
    return (
        y_p[None],
        y_s.reshape(nb, nt, D_MODEL),
        k_p.reshape(kv_shape),
        v_p.reshape(kv_shape),
        cext_p[None, None, :, :, :MLSTM_HEAD_DIM],
        cext_p[None, None, :, :, MLSTM_HEAD_DIM],
        m_p[None, None, :nh, 0],
        new_k_s.reshape(1, nb, WINDOW, KV_HEADS, ATTN_HEAD_DIM),
        new_v_s.reshape(1, nb, WINDOW, KV_HEADS, ATTN_HEAD_DIM),
        c_new.reshape(1, nb, nh, MLSTM_HEAD_DIM, MLSTM_HEAD_DIM),
        n_new.reshape(1, nb, nh, MLSTM_HEAD_DIM),
        m_new.reshape(1, nb, nh),
    )
```

```python
import jax
import jax.numpy as jnp
import numpy as np
from jax import lax
from jax.experimental import pallas as pl
from jax.experimental.pallas import tpu as pltpu

D_MODEL = 1024
PAST_LEN = 16384
ATTN_HEADS = 8
KV_HEADS = 2
ATTN_HEAD_DIM = 64
ATTN_GROUP = ATTN_HEADS // KV_HEADS
ATTN_WIDTH = ATTN_HEADS * ATTN_HEAD_DIM
KV_WIDTH = KV_HEADS * ATTN_HEAD_DIM
WINDOW = 128
ROPE_THETA = 10000.0
MLSTM_HEADS = 4
MLSTM_HEAD_DIM = 128
MLSTM_WIDTH = MLSTM_HEADS * MLSTM_HEAD_DIM
MIX_WIDTH = ATTN_WIDTH + MLSTM_WIDTH
D_FF = 2816
NORM_EPS = 1e-6

LANES = 128
SUBLANES = 8
VMEM_LIMIT = 56 * 1024 * 1024

COL_QA = 0
COL_KA = COL_QA + ATTN_WIDTH
COL_VA = COL_KA + KV_WIDTH
COL_QM = COL_VA + KV_WIDTH
COL_KM = COL_QM + MLSTM_WIDTH
COL_VM = COL_KM + MLSTM_WIDTH
COL_OM = COL_VM + MLSTM_WIDTH
COL_G = COL_OM + MLSTM_WIDTH
IN_WIDTH = COL_G + 2 * MLSTM_HEADS
FG_LANE = SUBLANES

PROMPT_BLOCK = 256
QBLOCK = WINDOW
MCHUNK = 128
PROJ_CHUNK = 256
FFN_BLOCK = 512
FFN_CHUNK = 256
SAMPLE_BATCH_BLOCK = 16
SAMPLE_TOKENS = 4
NEG = -1e30

f32 = jnp.float32
bf16 = jnp.bfloat16


def _rms(x, gain):
    return x * lax.rsqrt(jnp.mean(x * x, axis=-1, keepdims=True) + NORM_EPS) * gain


def _segsum64(s, lane):
    for k in (1, 2, 4, 8, 16, 32):
        s = s + jnp.where((lane & k) != 0, pltpu.roll(s, k, 1), pltpu.roll(s, LANES - k, 1))
    return s


def _headnorm_rope(xs, gain, cos, sin_signed, lane):
    ss = _segsum64(xs * xs, lane)
    y = xs * lax.rsqrt(ss * (1.0 / ATTN_HEAD_DIM) + NORM_EPS) * gain
    partner = jnp.where((lane & 32) != 0, pltpu.roll(y, 32, 1), pltpu.roll(y, LANES - 32, 1))
    return y * cos + partner * sin_signed


def _group_sumsq(xs, bd_ref):
    x2 = xs * xs
    hi = x2.astype(bf16)
    lo = (x2 - hi.astype(f32)).astype(bf16)
    return (jnp.dot(hi, bd_ref[...], preferred_element_type=f32)
            + jnp.dot(lo, bd_ref[...], preferred_element_type=f32))


PA_Q = 0
PA_K = PA_Q + ATTN_WIDTH
PA_V = PA_K + KV_HEADS * LANES
PA_WIDTH = PA_V + KV_HEADS * LANES
QUARTER = ATTN_HEAD_DIM // 2
QA_Q = 0
QA_K = QA_Q + 2 * ATTN_WIDTH
QA_V = QA_K + KV_HEADS * LANES
QA_VZ = QA_V + KV_HEADS * LANES
QA_WIDTH = QA_VZ + KV_HEADS * LANES
ZM_WIDTH = 4 * MLSTM_WIDTH
ATTN_PHASES = 4
MLSTM_PHASES = 5


def _norm_rope_quarters(xs, ss, gain, cos, sin_signed):
    y = xs * lax.rsqrt(ss * (1.0 / ATTN_HEAD_DIM) + NORM_EPS) * gain
    return y * cos + pltpu.roll(y, LANES // 2, 1) * sin_signed


def _col_chunks(width):
    return [(c, min(c + PROJ_CHUNK, width)) for c in range(0, width, PROJ_CHUNK)]


def _mixer_proj_jobs(x_ref, anorm_ref, wa_ref, w_ref, wgate_ref, za_ref, zm_ref, gz_ref):
    h = _rms(x_ref[...], anorm_ref[...]).astype(bf16)

    def proj_job(w_src, wc0, z_ref, c0, c1):
        def run():
            z_ref[:, c0:c1] = jnp.dot(h, w_src[:, wc0 + c0:wc0 + c1], preferred_element_type=f32)
        return run

    return ([proj_job(wa_ref, 0, za_ref, c0, c1) for c0, c1 in _col_chunks(PA_WIDTH)]
            + [proj_job(w_ref, COL_QM, zm_ref, c0, c1) for c0, c1 in _col_chunks(ZM_WIDTH)]
            + [proj_job(wgate_ref, 0, gz_ref, 0, LANES)])


def _mixer_prep_jobs(za_ref, gz_ref, cos_ref, sin_ref, bd_ref, qg_ref, kg_ref, gbias_ref, qa_ref,
                     colf_ref, urow_ref, wrow_ref, mst_ref, mout_ref):
    tb = za_ref.shape[0]

    def gates_job():
        lane_t = lax.broadcasted_iota(jnp.int32, (tb, LANES), 1)
        gcol = gz_ref[...] + gbias_ref[...]
        acol = jnp.where(lane_t < FG_LANE, gcol, jax.nn.log_sigmoid(gcol))
        arow = acol.T
        lane8 = lax.broadcasted_iota(jnp.int32, (SUBLANES, LANES), 1)
        nsb = tb // LANES
        slabs = [slice(sb * LANES, (sb + 1) * LANES) for sb in range(nsb)]
        ig_all = jnp.concatenate([arow[0:SUBLANES, ls] for ls in slabs], axis=0)
        b_all = jnp.concatenate([arow[FG_LANE:FG_LANE + SUBLANES, ls] for ls in slabs], axis=0)
        lane_in = lax.broadcasted_iota(jnp.int32, b_all.shape, 1) & (MCHUNK - 1)

        def scan(x, combine, identity):
            k = 1
            while k < MCHUNK:
                terms = [jnp.where(lane_in >= j * k, pltpu.roll(x, j * k, 1), identity)
                         for j in (1, 2, 3) if j * k < MCHUNK]
                for t in terms:
                    x = combine(x, t)
                k *= 4
            return x

        b_all = scan(b_all, jnp.add, 0.0)
        u_all = ig_all - b_all
        cm_all = scan(u_all, jnp.maximum, NEG)
        m_prev = mst_ref[:, 0:1]
        stacks = []
        for sb, ls in enumerate(slabs):
            sub = slice(sb * SUBLANES, (sb + 1) * SUBLANES)
            b8, u8, cm8 = b_all[sub], u_all[sub], cm_all[sub]
            g8 = jnp.zeros_like(u8)
            mp8 = jnp.zeros_like(u8)
            gl8 = jnp.zeros_like(u8)
            for c in range(LANES // MCHUNK):
                in_chunk = (lane8 // MCHUNK) == c
                gc = jnp.maximum(cm8, m_prev)
                last = c * MCHUNK + MCHUNK - 1
                g_last = jnp.max(jnp.where(lane8 == last, gc, NEG), axis=1, keepdims=True)
                b_last = jnp.max(jnp.where(lane8 == last, b8, NEG), axis=1, keepdims=True)
                g8 = jnp.where(in_chunk, gc, g8)
                mp8 = jnp.where(in_chunk, m_prev, mp8)
                gl8 = jnp.where(in_chunk, g_last, gl8)
                m_prev = b_last + g_last
            a8 = jnp.exp(mp8 - g8)
            emt8 = jnp.exp(-(b8 + g8))
            aend8 = jnp.exp(mp8 - gl8)
            stacks.append(jnp.concatenate(
                [g8, a8, emt8, aend8, jnp.zeros((LANES - 4 * SUBLANES, LANES), f32)], axis=0))
            urow_ref[:, ls] = u8
            wrow_ref[:, ls] = jnp.exp(u8 - gl8)
        mst_ref[...] = jnp.broadcast_to(m_prev, mst_ref.shape)
        mout_ref[...] = jnp.broadcast_to(m_prev, mout_ref.shape)
        colf_ref[...] = jnp.concatenate(stacks, axis=1).T

    def rope_job(qb):
        def run():
            rows = slice(qb * QBLOCK, (qb + 1) * QBLOCK)
            lane = lax.broadcasted_iota(jnp.int32, (QBLOCK, LANES), 1)
            head_a = ((lane // QUARTER) & 1) == 0
            row0 = lax.broadcasted_iota(jnp.int32, (QBLOCK, LANES), 0) == 0
            cos = cos_ref[rows, :]
            sin = sin_ref[rows, :]
            ss = [_group_sumsq(za_ref[rows, d * 2 * LANES:(d + 1) * 2 * LANES], bd_ref)
                  for d in range(PA_V // (2 * LANES))]
            for j in range(PA_V // LANES):
                is_q = j < ATTN_WIDTH // LANES
                y = _norm_rope_quarters(za_ref[rows, j * LANES:(j + 1) * LANES],
                                        ss[j // 2][:, (j % 2) * LANES:(j % 2 + 1) * LANES],
                                        qg_ref[...] if is_q else kg_ref[...], cos, sin)
                if is_q:
                    y = y * (ATTN_HEAD_DIM ** -0.5)
                    qa_ref[rows, QA_Q + 2 * j * LANES:QA_Q + (2 * j + 1) * LANES] = (
                        jnp.where(head_a, y, 0.0).astype(bf16))
                    qa_ref[rows, QA_Q + (2 * j + 1) * LANES:QA_Q + (2 * j + 2) * LANES] = (
                        jnp.where(head_a, 0.0, y).astype(bf16))
                else:
                    c = j - ATTN_WIDTH // LANES
                    qa_ref[rows, QA_K + c * LANES:QA_K + (c + 1) * LANES] = y.astype(bf16)
            for c in range(KV_HEADS):
                v = za_ref[rows, PA_V + c * LANES:PA_V + (c + 1) * LANES]
                qa_ref[rows, QA_V + c * LANES:QA_V + (c + 1) * LANES] = v.astype(bf16)
                qa_ref[rows, QA_VZ + c * LANES:QA_VZ + (c + 1) * LANES] = (
                    jnp.where(row0, 0.0, v).astype(bf16))
        return run

    return [gates_job] + [rope_job(qb) for qb in range(tb // QBLOCK)]


def _mixer_outproj_jobs(xs_ref, mix_ref, wout_ref, x1_ref):
    def job(c0, c1):
        def run():
            x1_ref[:, c0:c1] = xs_ref[:, c0:c1] + jnp.dot(mix_ref[...], wout_ref[:, c0:c1],
                                                          preferred_element_type=f32)
        return run

    return [job(c0, c1) for c0, c1 in _col_chunks(D_MODEL)]


def _mixer_window_out(xs_ref, anorm_ref, w_ref, kgs_ref, coss_ref, sins_ref, kout_ref, vout_ref):
    tb = xs_ref.shape[0]
    h = _rms(xs_ref[tb - WINDOW:, :], anorm_ref[...]).astype(bf16)
    zs = jnp.dot(h, w_ref[:, COL_KA:COL_QM], preferred_element_type=f32)
    lane_s = lax.broadcasted_iota(jnp.int32, (WINDOW, LANES), 1)
    kout_ref[...] = _headnorm_rope(zs[:, :KV_WIDTH], kgs_ref[...], coss_ref[...], sins_ref[...], lane_s)
    vout_ref[...] = zs[:, KV_WIDTH:]


def _mixer_core(first_block, last_block, fillers, prep_jobs, qa_ref, zm_ref, colf_ref, urow_ref, wrow_ref,
                mix_ref, bias_ref, mnorm_ref, kprev_ref, vprev_ref, cst_ref, cext_ref):
    tb = qa_ref.shape[0]
    nqb = tb // QBLOCK
    nch = tb // MCHUNK
    fillers = list(fillers)
    n_fill = len(fillers)
    slots = len(prep_jobs) + ATTN_PHASES + nch * MLSTM_PHASES
    progress = [0]

    def fill():
        progress[0] += 1
        while n_fill - len(fillers) < min(n_fill, -(-n_fill * progress[0] // slots)):
            fillers.pop(0)()

    low_half = lax.broadcasted_iota(jnp.int32, (QBLOCK, LANES), 1) < ATTN_HEAD_DIM
    qi = lax.broadcasted_iota(jnp.int32, (ATTN_GROUP * QBLOCK, 2 * QBLOCK), 0) & (QBLOCK - 1)
    kj = lax.broadcasted_iota(jnp.int32, (ATTN_GROUP * QBLOCK, 2 * QBLOCK), 1)
    band = (kj > qi) & (kj <= qi + QBLOCK)
    ones_slab = jnp.ones((2 * QBLOCK, LANES), bf16)

    def attn_phases(chains):
        rows_of = lambda qb: slice(qb * QBLOCK, (qb + 1) * QBLOCK)
        st = {}

        def scores():
            for qb, c in chains:
                rows = rows_of(qb)
                kcols = slice(QA_K + c * LANES, QA_K + (c + 1) * LANES)
                if qb == 0:
                    kprev, vprev = kprev_ref[c], vprev_ref[c]
                else:
                    kprev = qa_ref[rows_of(qb - 1), kcols]
                    vprev = qa_ref[rows_of(qb - 1), QA_VZ + c * LANES:QA_VZ + (c + 1) * LANES]
                kcat = jnp.concatenate([kprev, qa_ref[rows, kcols]], axis=0)
                vcat = jnp.concatenate(
                    [vprev, qa_ref[rows, QA_V + c * LANES:QA_V + (c + 1) * LANES]], axis=0)
                st['vext', qb, c] = jnp.concatenate([vcat, ones_slab], axis=1)
                q0 = QA_Q + c * ATTN_GROUP * LANES
                qst = jnp.concatenate([qa_ref[rows, q0 + g * LANES:q0 + (g + 1) * LANES]
                                       for g in range(ATTN_GROUP)], axis=0)
                st['s', qb, c] = lax.dot_general(qst, kcat, (((1,), (1,)), ((), ())),
                                                 preferred_element_type=f32)

        def softmax():
            for qb, c in chains:
                valid = band & (kj >= QBLOCK) if (first_block and qb == 0) else band
                s = jnp.where(valid, st.pop(('s', qb, c)), bias_ref[c])
                st['p', qb, c] = jnp.exp(s - jnp.max(s, axis=-1, keepdims=True)).astype(bf16)

        def values():
            for qb, c in chains:
                st['of', qb, c] = jnp.dot(st.pop(('p', qb, c)), st.pop(('vext', qb, c)),
                                          preferred_element_type=f32)

        def normalise():
            for qb, c in chains:
                of = st.pop(('of', qb, c))
                o = of[:, :LANES] / of[:, LANES:]
                for jj in range(2):
                    pair = jnp.where(low_half, o[(2 * jj) * QBLOCK:(2 * jj + 1) * QBLOCK],
                                     o[(2 * jj + 1) * QBLOCK:(2 * jj + 2) * QBLOCK])
                    col = (2 * c + jj) * LANES
                    mix_ref[rows_of(qb), col:col + LANES] = pair.astype(bf16)

        return [scores, softmax, values, normalise]

    ti = lax.broadcasted_iota(jnp.int32, (MCHUNK, MCHUNK), 0)
    si = lax.broadcasted_iota(jnp.int32, (MCHUNK, MCHUNK), 1)
    causal = si <= ti
    ones_l = jnp.ones((MCHUNK, LANES), bf16)

    cexts = [cst_ref[hd] for hd in range(MLSTM_HEADS)]

    def mlstm_phases(c):
        rows = slice(c * MCHUNK, (c + 1) * MCHUNK)
        heads = range(MLSTM_HEADS)
        hcols = lambda k, hd: slice((k * MLSTM_HEADS + hd) * MLSTM_HEAD_DIM,
                                    (k * MLSTM_HEADS + hd + 1) * MLSTM_HEAD_DIM)
        st = {}

        def scores():
            for hd in heads:
                st['q', hd] = zm_ref[rows, hcols(0, hd)].astype(bf16)
                st['k', hd] = zm_ref[rows, hcols(1, hd)] * (MLSTM_HEAD_DIM ** -0.5)
                st['s', hd] = lax.dot_general(st['q', hd], st['k', hd].astype(bf16),
                                              (((1,), (1,)), ((), ())), preferred_element_type=f32)

        def decay():
            for hd in heads:
                g_c = colf_ref[rows, hd:hd + 1]
                u_r = urow_ref[hd:hd + 1, rows]
                dmat = jnp.exp(jnp.where(causal, u_r - g_c, NEG))
                st['s', hd] = (st['s', hd] * dmat).astype(bf16)

        def readout():
            for hd in heads:
                a_c = colf_ref[rows, SUBLANES + hd:SUBLANES + hd + 1]
                vext = jnp.concatenate([zm_ref[rows, hcols(2, hd)].astype(bf16), ones_l], axis=1)
                st['v', hd] = vext
                st['nd', hd] = (
                    a_c * jnp.dot(st.pop(('q', hd)), cexts[hd].astype(bf16), preferred_element_type=f32)
                    + jnp.dot(st.pop(('s', hd)), vext, preferred_element_type=f32))

        def emit():
            for hd in heads:
                nd = st.pop(('nd', hd))
                emt_c = colf_ref[rows, 2 * SUBLANES + hd:2 * SUBLANES + hd + 1]
                hraw = nd[:, :MLSTM_HEAD_DIM] / jnp.maximum(jnp.abs(nd[:, MLSTM_HEAD_DIM:]), emt_c)
                hn = _rms(hraw, mnorm_ref[hd:hd + 1, :])
                og = zm_ref[rows, hcols(3, hd)]
                mix_ref[rows, ATTN_WIDTH + hd * MLSTM_HEAD_DIM:ATTN_WIDTH + (hd + 1) * MLSTM_HEAD_DIM] = (
                    (hn * jax.nn.sigmoid(og)).astype(bf16))

        def update():
            for hd in heads:
                aend = colf_ref[c * MCHUNK:c * MCHUNK + 1, 3 * SUBLANES + hd:3 * SUBLANES + hd + 1]
                w_r = wrow_ref[hd:hd + 1, rows]
                kw_t = (st.pop(('k', hd)).T * w_r).astype(bf16)
                cexts[hd] = aend * cexts[hd] + jnp.dot(kw_t, st.pop(('v', hd)),
                                                       preferred_element_type=f32)

        return [scores, decay, readout, emit, update]

    gates_job, rope_jobs = prep_jobs[0], prep_jobs[1:]
    attn = attn_phases([(qb, c) for qb in range(nqb) for c in range(KV_HEADS)])
    chunks = [mlstm_phases(c) for c in range(nch)]
    mlstm = []
    for c, ph in enumerate(chunks):
        ahead = chunks[c + 1][:2] if c + 1 < nch else []
        mlstm += (ph if c == 0 else ph[2:])[:-2] + ahead + ph[-2:]
    assert len(attn) == ATTN_PHASES and len(mlstm) == nch * MLSTM_PHASES
    order = [gates_job] + rope_jobs + attn[:1]
    rest = attn[1:]
    for i in range(max(len(rest), len(mlstm))):
        order += rest[i:i + 1] + mlstm[i:i + 1]
    assert len(order) == slots
    for job in order:
        job()
        fill()
    last_rows = slice(tb - QBLOCK, tb)
    for c in range(KV_HEADS):
        kprev_ref[c] = qa_ref[last_rows, QA_K + c * LANES:QA_K + (c + 1) * LANES]
        vprev_ref[c] = qa_ref[last_rows, QA_VZ + c * LANES:QA_VZ + (c + 1) * LANES]
    for hd in range(MLSTM_HEADS):
        cst_ref[hd] = cexts[hd]
        if last_block:
            cext_ref[hd] = cexts[hd]


def _prompt_mixer_kernel(x_ref, cos_ref, sin_ref, wa_ref, w_ref, wgate_ref, wout_ref, bd_ref, bias_ref,
                         anorm_ref, qg_ref, kg_ref, kgs_ref, coss_ref, sins_ref, gbias_ref, mnorm_ref,
                         x1_ref, kout_ref, vout_ref, cext_ref, mout_ref,
                         za0, za1, zm0, zm1, gz0, gz1, mix0, mix1, xs0, xs1,
                         qa_ref, colf_ref, urow_ref, wrow_ref, kprev_ref, vprev_ref, cst_ref, mst_ref):
    step = pl.program_id(0)
    nblk = pl.num_programs(0) - 2
    za, zm, gz, mix, xs = (za0, za1), (zm0, zm1), (gz0, gz1), (mix0, mix1), (xs0, xs1)

    def run(parity, do_in, do_core, do_out, first_block=False, last_block=False):
        other = 1 - parity
        jobs = []
        if do_out:
            jobs += _mixer_outproj_jobs(xs[parity], mix[parity], wout_ref, x1_ref)
        if do_in:
            jobs += _mixer_proj_jobs(x_ref, anorm_ref, wa_ref, w_ref, wgate_ref, za[parity], zm[parity],
                                     gz[parity])
        if do_core:
            prep = _mixer_prep_jobs(za[other], gz[other], cos_ref, sin_ref, bd_ref, qg_ref, kg_ref,
                                    gbias_ref, qa_ref, colf_ref, urow_ref, wrow_ref, mst_ref, mout_ref)
            _mixer_core(first_block, last_block, jobs, prep, qa_ref, zm[other], colf_ref, urow_ref,
                        wrow_ref, mix[other], bias_ref, mnorm_ref, kprev_ref, vprev_ref, cst_ref, cext_ref)
        else:
            for job in jobs:
                job()
        if last_block:
            _mixer_window_out(xs[other], anorm_ref, w_ref, kgs_ref, coss_ref, sins_ref, kout_ref, vout_ref)
        if do_in:
            xs[parity][...] = x_ref[...]

    @pl.when(step == 0)
    def _first():
        kprev_ref[...] = jnp.zeros_like(kprev_ref)
        vprev_ref[...] = jnp.zeros_like(vprev_ref)
        cst_ref[...] = jnp.zeros_like(cst_ref)
        mst_ref[...] = jnp.zeros_like(mst_ref)
        run(0, True, False, False)

    @pl.when(step == 1)
    def _second():
        run(1, True, True, False, first_block=True)

    steady = (step >= 2) & (step < nblk)

    @pl.when(steady & (step % 2 == 0))
    def _even():
        run(0, True, True, True)

    @pl.when(steady & (step % 2 == 1))
    def _odd():
        run(1, True, True, True)

    @pl.when(step == nblk)
    def _drain_core():
        run(0, False, True, True, last_block=True)

    @pl.when(step == nblk + 1)
    def _drain_out():
        run(1, False, False, True)


def _const_spec(shape, single=False):
    nd = len(shape)
    if single:
        return pl.BlockSpec(shape, lambda i, *_: (0,) * nd, pipeline_mode=pl.Buffered(1))
    return pl.BlockSpec(shape, lambda i, *_: (0,) * nd)


def _prompt_mixer(x, cos, sin, wa, w, wgate, wout_b, bd, bias, anorm, qg, kg, kgs, coss, sins, gbias, mnorm):
    t = x.shape[0]
    tb = PROMPT_BLOCK
    nblk = t // tb
    assert nblk % 2 == 0 and nblk >= 4
    state_shape = (MLSTM_HEADS, MLSTM_HEAD_DIM, 2 * MLSTM_HEAD_DIM)
    last = nblk - 1
    lag = lambda d: (lambda i: (jnp.clip(i - d, 0, last), 0))
    return pl.pallas_call(
        _prompt_mixer_kernel,
        grid=(nblk + 2,),
        in_specs=[
            pl.BlockSpec((tb, D_MODEL), lag(0)),
            pl.BlockSpec((tb, LANES), lag(1)),
            pl.BlockSpec((tb, LANES), lag(1)),
            _const_spec((D_MODEL, PA_WIDTH), single=True),
            _const_spec((D_MODEL, IN_WIDTH), single=True),
            _const_spec((D_MODEL, LANES), single=True),
            _const_spec((MIX_WIDTH, D_MODEL), single=True),
            _const_spec((2 * LANES, 2 * LANES), single=True),
            _const_spec((KV_HEADS, ATTN_GROUP * QBLOCK, 2 * QBLOCK), single=True),
            _const_spec((1, D_MODEL)),
            _const_spec((1, LANES)),
            _const_spec((1, LANES)),
            _const_spec((1, LANES)),
            _const_spec((WINDOW, LANES)),
            _const_spec((WINDOW, LANES)),
            _const_spec((1, LANES)),
            _const_spec((MLSTM_HEADS, MLSTM_HEAD_DIM)),
        ],
        out_specs=[
            pl.BlockSpec((tb, D_MODEL), lag(2)),
            _const_spec((WINDOW, KV_WIDTH)),
            _const_spec((WINDOW, KV_WIDTH)),
            _const_spec(state_shape),
            _const_spec((SUBLANES, LANES)),
        ],
        out_shape=[
            jax.ShapeDtypeStruct((t, D_MODEL), f32),
            jax.ShapeDtypeStruct((WINDOW, KV_WIDTH), f32),
            jax.ShapeDtypeStruct((WINDOW, KV_WIDTH), f32),
            jax.ShapeDtypeStruct(state_shape, f32),
            jax.ShapeDtypeStruct((SUBLANES, LANES), f32),
        ],
        scratch_shapes=(
            [pltpu.VMEM((tb, PA_WIDTH), f32)] * 2 + [pltpu.VMEM((tb, ZM_WIDTH), f32)] * 2
            + [pltpu.VMEM((tb, LANES), f32)] * 2
            + [pltpu.VMEM((tb, MIX_WIDTH), bf16)] * 2 + [pltpu.VMEM((tb, D_MODEL), f32)] * 2
            + [pltpu.VMEM((tb, QA_WIDTH), bf16), pltpu.VMEM((tb, LANES), f32)]
            + [pltpu.VMEM((SUBLANES, tb), f32)] * 2
            + [pltpu.VMEM((KV_HEADS, WINDOW, LANES), bf16)] * 2
            + [pltpu.VMEM(state_shape, f32), pltpu.VMEM((SUBLANES, LANES), f32)]),
        compiler_params=pltpu.CompilerParams(
            dimension_semantics=("arbitrary",), vmem_limit_bytes=VMEM_LIMIT),
        name="prompt_mixer",
    )(x, cos, sin, wa, w, wgate, wout_b, bd, bias, anorm, qg, kg, kgs, coss, sins, gbias, mnorm)


def _ffn_kernel(xp_ref, xs_ref, g_ref, wg_ref, wu_ref, wd_ref, op_ref, os_ref):
    step = pl.program_id(0)
    last = pl.num_programs(0) - 1

    @pl.when(step < last)
    def _prompt_rows():
        _ffn_rows(xp_ref, g_ref, wg_ref, wu_ref, wd_ref, op_ref)

    @pl.when(step == last)
    def _sample_rows():
        _ffn_rows(xs_ref, g_ref, wg_ref, wu_ref, wd_ref, os_ref)


def _ffn_rows(x_ref, g_ref, wg_ref, wu_ref, wd_ref, o_ref):
    x = x_ref[...]
    hf = _rms(x, g_ref[...]).astype(bf16)
    acc = x
    for c in range(D_FF // FFN_CHUNK):
        cs = slice(c * FFN_CHUNK, (c + 1) * FFN_CHUNK)
        gate = jnp.dot(hf, wg_ref[:, cs], preferred_element_type=f32)
        up = jnp.dot(hf, wu_ref[:, cs], preferred_element_type=f32)
        act = (gate * jax.nn.sigmoid(gate) * up).astype(bf16)
        acc = acc + jnp.dot(act, wd_ref[cs, :], preferred_element_type=f32)
    o_ref[...] = acc


def _ffn(x_p, x_s, fnorm, wg_b, wu_b, wd_b):
    n = x_p.shape[0]
    ns = x_s.shape[0]
    tm = FFN_BLOCK
    last = n // tm - 1
    return pl.pallas_call(
        _ffn_kernel,
        grid=(n // tm + 1,),
        in_specs=[
            pl.BlockSpec((tm, D_MODEL), lambda i: (jnp.minimum(i, last), 0)),
            _const_spec((ns, D_MODEL), single=True),
            _const_spec((1, D_MODEL)),
            _const_spec((D_MODEL, D_FF), single=True),
            _const_spec((D_MODEL, D_FF), single=True),
            _const_spec((D_FF, D_MODEL), single=True),
        ],
        out_specs=[pl.BlockSpec((tm, D_MODEL), lambda i: (jnp.minimum(i, last), 0)),
                   _const_spec((ns, D_MODEL))],
        out_shape=[jax.ShapeDtypeStruct((n, D_MODEL), f32), jax.ShapeDtypeStruct((ns, D_MODEL), f32)],
        compiler_params=pltpu.CompilerParams(
            dimension_semantics=("arbitrary",), vmem_limit_bytes=VMEM_LIMIT),
        name="ffn",
    )(x_p, x_s, fnorm, wg_b, wu_b, wd_b)


def _sample_mixer_kernel(x_ref, ckt_ref, cvt_ref, c_ref, n_ref, m_ref, cos_ref, sin_ref, wqs_ref, w_ref,
                         wgate_ref, wout_ref, sink_ref, bd_ref, anorm_ref, qg_ref, kg_ref, gbias_ref,
                         mnorm_ref, x1_ref, nkt_ref, nvt_ref, cn_ref, nn_ref, mn_ref):
    bb, tpad, _ = x_ref.shape
    nrows = bb * tpad
    nreal = SAMPLE_TOKENS
    h = _rms(x_ref[...].reshape(nrows, D_MODEL), anorm_ref[...]).astype(bf16)
    z = jnp.concatenate(
        [jnp.dot(h, wqs_ref[...], preferred_element_type=f32),
         jnp.dot(h, w_ref[:, COL_KA:COL_G], preferred_element_type=f32),
         jnp.dot(h, wgate_ref[...], preferred_element_type=f32)], axis=1)
    lane = lax.broadcasted_iota(jnp.int32, (nrows, LANES), 1)
    low = lane < ATTN_HEAD_DIM
    cos = cos_ref[...]
    sin = sin_ref[...]

    def per_seq(a):
        return a.reshape(bb, tpad, a.shape[-1])

    def norm_rope(xs, gain):
        y = xs * lax.rsqrt(_group_sumsq(xs, bd_ref) * (1.0 / ATTN_HEAD_DIM) + NORM_EPS) * gain
        partner = jnp.where((lane & QUARTER) != 0, pltpu.roll(y, QUARTER, 1),
                            pltpu.roll(y, LANES - QUARTER, 1))
        return y * cos + partner * sin

    q_rows = []
    for j in range(ATTN_GROUP):
        qs = norm_rope(z[:, COL_QA + j * LANES:COL_QA + (j + 1) * LANES], qg_ref[...])
        qs = qs * (ATTN_HEAD_DIM ** -0.5)
        q_rows.append(per_seq(jnp.where(low, qs, 0.0)).astype(bf16))
        q_rows.append(per_seq(jnp.where(low, 0.0, qs)).astype(bf16))
    qbd = jnp.concatenate(q_rows, axis=1)
    knew = norm_rope(z[:, COL_KA:COL_KA + KV_WIDTH], kg_ref[...])
    vnew = z[:, COL_VA:COL_VA + KV_WIDTH]
    zpad = jnp.zeros((bb, LANES - tpad, LANES), bf16)
    knp = jnp.concatenate([per_seq(knew).astype(bf16), zpad], axis=1)
    vnp = jnp.concatenate([per_seq(vnew).astype(bf16), zpad], axis=1)
    ckt = ckt_ref[...]
    cvt = cvt_ref[...]
    s = jnp.concatenate(
        [jnp.einsum('bqd,bdw->bqw', qbd, ckt.astype(bf16), preferred_element_type=f32),
         jnp.einsum('bqd,bkd->bqk', qbd, knp, preferred_element_type=f32)], axis=2)
    tq = lax.broadcasted_iota(jnp.int32, s.shape, 1) & (tpad - 1)
    kj = lax.broadcasted_iota(jnp.int32, s.shape, 2)
    valid = ((kj < WINDOW) & (kj > tq)) | ((kj >= WINDOW) & (kj - WINDOW <= tq) & (kj - WINDOW < nreal))
    s = jnp.where(valid, s, NEG)
    sink = sink_ref[:, 0:1][None]
    mx = jnp.maximum(jnp.max(s, axis=-1, keepdims=True), sink)
    p = jnp.exp(s - mx)
    den = jnp.sum(p, axis=-1, keepdims=True) + jnp.exp(sink - mx)
    pb = p.astype(bf16)
    o = (jnp.einsum('bqw,bdw->bqd', pb[:, :, :WINDOW], cvt.astype(bf16), preferred_element_type=f32)
         + jnp.einsum('bqk,bkd->bqd', pb[:, :, WINDOW:], vnp, preferred_element_type=f32)) / den
    low3 = lax.broadcasted_iota(jnp.int32, (bb, tpad, LANES), 2) < ATTN_HEAD_DIM
    mix_parts = []
    for j in range(ATTN_GROUP):
        r0 = 2 * j * tpad
        pair = jnp.where(low3, o[:, r0:r0 + tpad, :], o[:, r0 + tpad:r0 + 2 * tpad, :])
        mix_parts.append(pair.reshape(nrows, LANES).astype(bf16))

    keep = lax.broadcasted_iota(jnp.int32, (KV_WIDTH, WINDOW), 1) < WINDOW - nreal
    knt = knew.T
    vnt = vnew.T
    for b in range(bb):
        shift = (WINDOW - nreal - b * tpad) % LANES
        nkt_ref[b] = jnp.where(keep, pltpu.roll(ckt_ref[b], WINDOW - nreal, 1), pltpu.roll(knt, shift, 1))
        nvt_ref[b] = jnp.where(keep, pltpu.roll(cvt_ref[b], WINDOW - nreal, 1), pltpu.roll(vnt, shift, 1))

    gz = per_seq(z[:, COL_G:COL_G + LANES] + gbias_ref[...])
    lgz = jax.nn.log_sigmoid(gz)
    trow = lax.broadcasted_iota(jnp.int32, (bb, tpad, 1), 1)
    real = trow < nreal
    mn_ref[...] = jnp.zeros_like(mn_ref)
    heads = range(MLSTM_HEADS)
    hcols = lambda base, hd: slice(base + hd * MLSTM_HEAD_DIM, base + (hd + 1) * MLSTM_HEAD_DIM)
    last = nreal - 1
    st = {}
    for hd in heads:
        m0 = m_ref[:, hd:hd + 1, :]
        ig_c = jnp.where(real, gz[:, :, hd:hd + 1], NEG)
        lf_c = jnp.where(real, lgz[:, :, FG_LANE + hd:FG_LANE + hd + 1], 0.0)
        b_c = jnp.zeros_like(lf_c)
        for sx in range(nreal):
            b_c = b_c + jnp.where(trow >= sx, lf_c[:, sx:sx + 1, :], 0.0)
        dlog = [jnp.where(trow >= sx, b_c - b_c[:, sx:sx + 1, :] + ig_c[:, sx:sx + 1, :], NEG)
                for sx in range(nreal)]
        inter = b_c + m0
        m_t = inter
        for sx in range(nreal):
            m_t = jnp.maximum(m_t, dlog[sx])
        m_new = m_t[:, last:last + 1, :]
        b_last = b_c[:, last:last + 1, :]
        st[hd] = dict(m_t=m_t, a=jnp.exp(inter - m_t), dexp=[jnp.exp(d - m_t) for d in dlog],
                      m_new=m_new, a_end=jnp.exp(b_last + m0 - m_new),
                      w_c=jnp.exp(b_last - b_c + ig_c - m_new))
    for hd in heads:
        s = st[hd]
        s['q'] = per_seq(z[:, hcols(COL_QM, hd)])
        s['k'] = per_seq(z[:, hcols(COL_KM, hd)]) * (MLSTM_HEAD_DIM ** -0.5)
        s['v'] = per_seq(z[:, hcols(COL_VM, hd)])
        s['qc'] = jnp.einsum('btd,bde->bte', s['q'].astype(bf16), c_ref[:, hd].astype(bf16),
                             preferred_element_type=f32)
    for hd in heads:
        s = st[hd]
        q, k, v = s['q'], s['k'], s['v']
        num = s['a'] * s.pop('qc')
        den_m = s['a'] * jnp.sum(q * n_ref[:, hd:hd + 1, :], axis=2, keepdims=True)
        for sx in range(nreal):
            sd = jnp.sum(q * k[:, sx:sx + 1, :], axis=2, keepdims=True) * s['dexp'][sx]
            num = num + sd * v[:, sx:sx + 1, :]
            den_m = den_m + sd
        hraw = num / jnp.maximum(jnp.abs(den_m), jnp.exp(-s['m_t']))
        hn = _rms(hraw, mnorm_ref[hd:hd + 1, :][None])
        og = per_seq(z[:, hcols(COL_OM, hd)])
        mix_parts.append((hn * jax.nn.sigmoid(og)).reshape(nrows, MLSTM_HEAD_DIM).astype(bf16))
    for hd in heads:
        s = st[hd]
        kw = s['k'] * s['w_c']
        cn_ref[:, hd] = s['a_end'] * c_ref[:, hd] + jnp.einsum(
            'bsd,bse->bde', kw.astype(bf16), s['v'].astype(bf16), preferred_element_type=f32)
        nn_ref[:, hd:hd + 1, :] = s['a_end'] * n_ref[:, hd:hd + 1, :] + jnp.sum(kw, axis=1, keepdims=True)
        mn_ref[:, hd:hd + 1, :] = jnp.broadcast_to(s['m_new'], (bb, 1, LANES))

    mix = jnp.concatenate(mix_parts, axis=1)
    x1 = x_ref[...].reshape(nrows, D_MODEL) + jnp.dot(mix, wout_ref[...], preferred_element_type=f32)
    x1_ref[...] = x1.reshape(bb, tpad, D_MODEL)


def _sample_mixer(x_pad, ckt, cvt, c0, n0, m0, cos, sin, wq_s, w, wgate, wout_s, sink_tile, bd, anorm, qg,
                  kg, gbias, mnorm):
    nb, tpad, _ = x_pad.shape
    bb = SAMPLE_BATCH_BLOCK
    nh = MLSTM_HEADS
    blk = lambda shape: pl.BlockSpec(shape, lambda i: (i,) + (0,) * (len(shape) - 1))
    cblk = (bb, nh, MLSTM_HEAD_DIM, MLSTM_HEAD_DIM)
    return pl.pallas_call(
        _sample_mixer_kernel,
        grid=(nb // bb,),
        in_specs=[blk((bb, tpad, D_MODEL)), blk((bb, KV_WIDTH, WINDOW)), blk((bb, KV_WIDTH, WINDOW)),
                  blk(cblk), blk((bb, nh, MLSTM_HEAD_DIM)), blk((bb, nh, 1)),
                  _const_spec((bb * tpad, LANES)), _const_spec((bb * tpad, LANES)),
                  _const_spec((D_MODEL, ATTN_WIDTH), single=True),
                  _const_spec((D_MODEL, IN_WIDTH), single=True),
                  _const_spec((D_MODEL, LANES), single=True),
                  _const_spec((MIX_WIDTH, D_MODEL), single=True),
                  _const_spec((ATTN_HEADS * tpad, LANES)), _const_spec((LANES, LANES)),
                  _const_spec((1, D_MODEL)),
                  _const_spec((1, LANES)), _const_spec((1, LANES)), _const_spec((1, LANES)),
                  _const_spec((nh, MLSTM_HEAD_DIM))],
        out_specs=[blk((bb, tpad, D_MODEL)), blk((bb, KV_WIDTH, WINDOW)), blk((bb, KV_WIDTH, WINDOW)),
                   blk(cblk), blk((bb, nh, MLSTM_HEAD_DIM)), blk((bb, tpad, LANES))],
        out_shape=[jax.ShapeDtypeStruct((nb, tpad, D_MODEL), f32),
                   jax.ShapeDtypeStruct((nb, KV_WIDTH, WINDOW), f32),
                   jax.ShapeDtypeStruct((nb, KV_WIDTH, WINDOW), f32),
                   jax.ShapeDtypeStruct((nb,) + cblk[1:], f32),
                   jax.ShapeDtypeStruct((nb, nh, MLSTM_HEAD_DIM), f32),
                   jax.ShapeDtypeStruct((nb, tpad, LANES), f32)],
        compiler_params=pltpu.CompilerParams(
            dimension_semantics=("arbitrary",), vmem_limit_bytes=VMEM_LIMIT),
        name="sample_mixer",
    )(x_pad, ckt, cvt, c0, n0, m0, cos, sin, wq_s, w, wgate, wout_s, sink_tile, bd, anorm, qg, kg, gbias,
      mnorm)


def _rope_angles(pos):
    half = ATTN_HEAD_DIM // 2
    inv = ROPE_THETA ** (-np.arange(half, dtype=np.float64) / half)
    ang = pos.astype(np.float64)[:, None] * inv[None, :]
    return np.cos(ang).astype(np.float32), np.sin(ang).astype(np.float32)


def _rope_tables(pos):
    c, s = _rope_angles(pos)
    cos = np.tile(c, (1, LANES // QUARTER))
    sin = np.tile(np.concatenate([-s, s], axis=1), (1, LANES // ATTN_HEAD_DIM))
    return cos, sin


def _rope_tables_quarters(pos):
    c, s = _rope_angles(pos)
    return np.tile(c, (1, LANES // QUARTER)), np.concatenate([-s, -s, s, s], axis=1)


def _quarters(a):
    lo, hi = a[..., :QUARTER], a[..., QUARTER:]
    return jnp.concatenate([lo, lo, hi, hi], axis=-1)


def _prompt_attn_weights(w):
    d = w.shape[0]
    wq = w[:, COL_QA:COL_KA].reshape(d, ATTN_WIDTH // LANES, 2, 2, QUARTER)
    wq = wq.transpose(0, 1, 3, 2, 4).reshape(d, ATTN_WIDTH)
    wk = _quarters(w[:, COL_KA:COL_VA].reshape(d, KV_HEADS, ATTN_HEAD_DIM)).reshape(d, KV_HEADS * LANES)
    wv = w[:, COL_VA:COL_QM].reshape(d, KV_HEADS, 1, ATTN_HEAD_DIM)
    wv = jnp.broadcast_to(wv, (d, KV_HEADS, 2, ATTN_HEAD_DIM)).reshape(d, KV_HEADS * LANES)
    return jnp.concatenate([wq, wk, wv], axis=1)


def kernel(x_prompt, x_sample, cache_k, cache_v, state_C, state_n, state_m, attn_norm, w_in, q_norm,
           k_norm, attn_sinks, b_ig, b_fg, mlstm_norm, w_out, ffn_norm, w_gate, w_up, w_down):
    assert w_in.shape[0] == 1 and x_prompt.shape[0] == 1
    tp = x_prompt.shape[1]
    nb, nt = x_sample.shape[0], x_sample.shape[1]
    assert nt == SAMPLE_TOKENS
    tpad = SUBLANES
    nh = MLSTM_HEADS

    w = w_in[0].astype(bf16)
    pad_a = jnp.zeros((D_MODEL, FG_LANE - nh), bf16)
    pad_b = jnp.zeros((D_MODEL, LANES - FG_LANE - nh), bf16)
    wgate = jnp.concatenate([w[:, COL_G:COL_G + nh], pad_a, w[:, COL_G + nh:], pad_b], axis=1)
    gbias = jnp.concatenate(
        [b_ig[0], jnp.zeros((FG_LANE - nh,), f32), b_fg[0], jnp.zeros((LANES - FG_LANE - nh,), f32)]
    ).reshape(1, LANES)
    wout_b = w_out[0].astype(bf16)
    wg_b = w_gate[0].astype(bf16)
    wu_b = w_up[0].astype(bf16)
    wd_b = w_down[0].astype(bf16)
    anorm = attn_norm[0].reshape(1, D_MODEL)
    fnorm = ffn_norm[0].reshape(1, D_MODEL)
    qg = jnp.tile(q_norm[0], LANES // ATTN_HEAD_DIM).reshape(1, LANES)
    kg = jnp.tile(k_norm[0], LANES // ATTN_HEAD_DIM).reshape(1, LANES)
    mnorm = mlstm_norm[0].reshape(nh, MLSTM_HEAD_DIM)
    sinks = attn_sinks[0]

    wa = _prompt_attn_weights(w)
    idx = np.arange(2 * LANES)
    same = (idx[:, None] // LANES == idx[None, :] // LANES) & (
        (idx[:, None] // QUARTER) % 2 == (idx[None, :] // QUARTER) % 2)
    bd = jnp.asarray(same, dtype=bf16)
    sink_rows_p = jnp.repeat(sinks.reshape(KV_HEADS, ATTN_GROUP), QBLOCK, axis=1)
    bias = jnp.where(jnp.arange(2 * QBLOCK)[None, None, :] == 0, sink_rows_p[:, :, None], NEG)
    qgq = _quarters(q_norm[0]).reshape(1, LANES)
    kgq = _quarters(k_norm[0]).reshape(1, LANES)
    pos_p = np.arange(tp, dtype=np.float32)
    cos_p, sin_p = _rope_tables_quarters(pos_p)
    cos_w, sin_w = _rope_tables(pos_p[tp - WINDOW:])
    x1_p, k_p, v_p, cext_p, m_p = _prompt_mixer(
        x_prompt[0], cos_p, sin_p, wa, w, wgate, wout_b, bd, bias, anorm, qgq, kgq, kg, cos_w, sin_w,
        gbias, mnorm)

    wq_s = w[:, COL_QA:COL_KA].reshape(D_MODEL, KV_HEADS, ATTN_GROUP, ATTN_HEAD_DIM)
    wq_s = wq_s.transpose(0, 2, 1, 3).reshape(D_MODEL, ATTN_WIDTH)
    wo_a = wout_b[:ATTN_WIDTH].reshape(KV_HEADS, ATTN_GROUP, ATTN_HEAD_DIM, D_MODEL)
    wo_a = wo_a.transpose(1, 0, 2, 3).reshape(ATTN_WIDTH, D_MODEL)
    wout_s = jnp.concatenate([wo_a, wout_b[ATTN_WIDTH:]], axis=0)
    sink_tile = jnp.broadcast_to(
        jnp.repeat(sinks.reshape(KV_HEADS, ATTN_GROUP).T.reshape(-1), tpad)[:, None],
        (ATTN_HEADS * tpad, LANES))
    lanes = np.arange(LANES)
    bd_s = jnp.asarray(lanes[:, None] // ATTN_HEAD_DIM == lanes[None, :] // ATTN_HEAD_DIM, dtype=bf16)
    cos_s, sin_s = _rope_tables(np.arange(tpad, dtype=np.float32) + np.float32(PAST_LEN))
    cos_s = np.tile(cos_s, (SAMPLE_BATCH_BLOCK, 1))
    sin_s = np.tile(sin_s, (SAMPLE_BATCH_BLOCK, 1))
    x_pad = jnp.pad(x_sample, ((0, 0), (0, tpad - nt), (0, 0)))
    ckt = cache_k[0].reshape(nb, WINDOW, KV_WIDTH).transpose(0, 2, 1)
    cvt = cache_v[0].reshape(nb, WINDOW, KV_WIDTH).transpose(0, 2, 1)
    x1_pad, nkt, nvt, c_new, n_new, m_pad = _sample_mixer(
        x_pad, ckt, cvt, state_C[0], state_n[0], state_m[0][:, :, None], cos_s, sin_s, wq_s, w, wgate,
        wout_s, sink_tile, bd_s, anorm, qg, kg, gbias, mnorm)
    y_p, y_s = _ffn(x1_p, x1_pad[:, :nt].reshape(nb * nt, D_MODEL), fnorm, wg_b, wu_b, wd_b)
    m_new = m_pad[:, :nh, 0]

    new_k_s = nkt.transpose(0, 2, 1)
    new_v_s = nvt.transpose(0, 2, 1)

    kv_shape = (1, 1, WINDOW, KV_HEADS, ATTN_HEAD_DIM)
    return (
        y_p[None],
        y_s.reshape(nb, nt, D_MODEL),
        k_p.reshape(kv_shape),
        v_p.reshape(kv_shape),
        cext_p[None, None, :, :, :MLSTM_HEAD_DIM],
        cext_p[None, None, :, :, MLSTM_HEAD_DIM],
        m_p[None, None, :nh, 0],
        new_k_s.reshape(1, nb, WINDOW, KV_HEADS, ATTN_HEAD_DIM),
        new_v_s.reshape(1, nb, WINDOW, KV_HEADS, ATTN_HEAD_DIM),
        c_new.reshape(1, nb, nh, MLSTM_HEAD_DIM, MLSTM_HEAD_DIM),
        n_new.reshape(1, nb, nh, MLSTM_HEAD_DIM),
        m_new.reshape(1, nb, nh),
    )
```

```python
import jax
import jax.numpy as jnp
import numpy as np
from jax import lax
from jax.experimental import pallas as pl
from jax.experimental.pallas import tpu as pltpu

D_MODEL = 1024
PAST_LEN = 16384
ATTN_HEADS = 8
KV_HEADS = 2
ATTN_HEAD_DIM = 64
ATTN_GROUP = ATTN_HEADS // KV_HEADS
ATTN_WIDTH = ATTN_HEADS * ATTN_HEAD_DIM
KV_WIDTH = KV_HEADS * ATTN_HEAD_DIM
WINDOW = 128
ROPE_THETA = 10000.0
MLSTM_HEADS = 4
MLSTM_HEAD_DIM = 128
MLSTM_WIDTH = MLSTM_HEADS * MLSTM_HEAD_DIM
MIX_WIDTH = ATTN_WIDTH + MLSTM_WIDTH
D_FF = 2816
NORM_EPS = 1e-6

LANES = 128
SUBLANES = 8
VMEM_LIMIT = 56 * 1024 * 1024

COL_QA = 0
COL_KA = COL_QA + ATTN_WIDTH
COL_VA = COL_KA + KV_WIDTH
COL_QM = COL_VA + KV_WIDTH
COL_KM = COL_QM + MLSTM_WIDTH
COL_VM = COL_KM + MLSTM_WIDTH
COL_OM = COL_VM + MLSTM_WIDTH
COL_G = COL_OM + MLSTM_WIDTH
IN_WIDTH = COL_G + 2 * MLSTM_HEADS
FG_LANE = SUBLANES

PROMPT_BLOCK = 256
QBLOCK = WINDOW
MCHUNK = 128
PROJ_CHUNK = 256
FFN_BLOCK = 512
FFN_CHUNK = 256
SAMPLE_BATCH_BLOCK = 16
SAMPLE_TOKENS = 4
NEG = -1e30

f32 = jnp.float32
bf16 = jnp.bfloat16


def _rms(x, gain):
    return x * lax.rsqrt(jnp.mean(x * x, axis=-1, keepdims=True) + NORM_EPS) * gain


def _segsum64(s, lane):
    for k in (1, 2, 4, 8, 16, 32):
        s = s + jnp.where((lane & k) != 0, pltpu.roll(s, k, 1), pltpu.roll(s, LANES - k, 1))
    return s


def _headnorm_rope(xs, gain, cos, sin_signed, lane):
    ss = _segsum64(xs * xs, lane)
    y = xs * lax.rsqrt(ss * (1.0 / ATTN_HEAD_DIM) + NORM_EPS) * gain
    partner = jnp.where((lane & 32) != 0, pltpu.roll(y, 32, 1), pltpu.roll(y, LANES - 32, 1))
    return y * cos + partner * sin_signed


def _group_sumsq(xs, bd_ref):
    x2 = xs * xs
    hi = x2.astype(bf16)
    lo = (x2 - hi.astype(f32)).astype(bf16)
    return (jnp.dot(hi, bd_ref[...], preferred_element_type=f32)
            + jnp.dot(lo, bd_ref[...], preferred_element_type=f32))


PA_Q = 0
PA_K = PA_Q + ATTN_WIDTH
PA_V = PA_K + KV_HEADS * LANES
PA_WIDTH = PA_V + KV_HEADS * LANES
QUARTER = ATTN_HEAD_DIM // 2
QA_Q = 0
QA_K = QA_Q + 2 * ATTN_WIDTH
QA_V = QA_K + KV_HEADS * LANES
QA_VZ = QA_V + KV_HEADS * LANES
QA_WIDTH = QA_VZ + KV_HEADS * LANES
ZM_WIDTH = 4 * MLSTM_WIDTH
ATTN_PHASES = 4
MLSTM_PHASES = 5


def _norm_rope_quarters(xs, ss, gain, cos, sin_signed):
    y = xs * lax.rsqrt(ss * (1.0 / ATTN_HEAD_DIM) + NORM_EPS) * gain
    return y * cos + pltpu.roll(y, LANES // 2, 1) * sin_signed


def _col_chunks(width):
    return [(c, min(c + PROJ_CHUNK, width)) for c in range(0, width, PROJ_CHUNK)]


def _mixer_proj_jobs(x_ref, anorm_ref, wa_ref, w_ref, wgate_ref, za_ref, zm_ref, gz_ref):
    h = _rms(x_ref[...], anorm_ref[...]).astype(bf16)

    def proj_job(w_src, wc0, z_ref, c0, c1):
        def run():
            z_ref[:, c0:c1] = jnp.dot(h, w_src[:, wc0 + c0:wc0 + c1], preferred_element_type=f32)
        return run

    return ([proj_job(wa_ref, 0, za_ref, c0, c1) for c0, c1 in _col_chunks(PA_WIDTH)]
            + [proj_job(w_ref, COL_QM, zm_ref, c0, c1) for c0, c1 in _col_chunks(ZM_WIDTH)]
            + [proj_job(wgate_ref, 0, gz_ref, 0, LANES)])


def _mixer_prep_jobs(za_ref, gz_ref, cos_ref, sin_ref, bd_ref, qg_ref, kg_ref, gbias_ref, qa_ref,
                     colf_ref, urow_ref, wrow_ref, mst_ref, mout_ref):
    tb = za_ref.shape[0]

    def gates_job():
        lane_t = lax.broadcasted_iota(jnp.int32, (tb, LANES), 1)
        gcol = gz_ref[...] + gbias_ref[...]
        acol = jnp.where(lane_t < FG_LANE, gcol, jax.nn.log_sigmoid(gcol))
        arow = acol.T
        lane8 = lax.broadcasted_iota(jnp.int32, (SUBLANES, LANES), 1)
        nsb = tb // LANES
        slabs = [slice(sb * LANES, (sb + 1) * LANES) for sb in range(nsb)]
        ig_all = jnp.concatenate([arow[0:SUBLANES, ls] for ls in slabs], axis=0)
        b_all = jnp.concatenate([arow[FG_LANE:FG_LANE + SUBLANES, ls] for ls in slabs], axis=0)
        lane_in = lax.broadcasted_iota(jnp.int32, b_all.shape, 1) & (MCHUNK - 1)

        def scan(x, combine, identity):
            k = 1
            while k < MCHUNK:
                terms = [jnp.where(lane_in >= j * k, pltpu.roll(x, j * k, 1), identity)
                         for j in (1, 2, 3) if j * k < MCHUNK]
                for t in terms:
                    x = combine(x, t)
                k *= 4
            return x

        b_all = scan(b_all, jnp.add, 0.0)
        u_all = ig_all - b_all
        cm_all = scan(u_all, jnp.maximum, NEG)
        m_prev = mst_ref[:, 0:1]
        stacks = []
        ends = {}
        for sb in range(nsb):
            sub = slice(sb * SUBLANES, (sb + 1) * SUBLANES)
            for c in range(LANES // MCHUNK):
                at_end = lane8 == c * MCHUNK + MCHUNK - 1
                ends[sb, c] = (jnp.max(jnp.where(at_end, cm_all[sub], NEG), axis=1, keepdims=True),
                               jnp.max(jnp.where(at_end, b_all[sub], NEG), axis=1, keepdims=True))
        for sb, ls in enumerate(slabs):
            sub = slice(sb * SUBLANES, (sb + 1) * SUBLANES)
            b8, u8, cm8 = b_all[sub], u_all[sub], cm_all[sub]
            g8 = jnp.zeros_like(u8)
            mp8 = jnp.zeros_like(u8)
            gl8 = jnp.zeros_like(u8)
            for c in range(LANES // MCHUNK):
                in_chunk = (lane8 // MCHUNK) == c
                gc = jnp.maximum(cm8, m_prev)
                cm_last, b_last = ends[sb, c]
                g_last = jnp.maximum(cm_last, m_prev)
                g8 = jnp.where(in_chunk, gc, g8)
                mp8 = jnp.where(in_chunk, m_prev, mp8)
                gl8 = jnp.where(in_chunk, g_last, gl8)
                m_prev = b_last + g_last
            a8 = jnp.exp(mp8 - g8)
            emt8 = jnp.exp(-(b8 + g8))
            aend8 = jnp.exp(mp8 - gl8)
            stacks.append(jnp.concatenate(
                [g8, a8, emt8, aend8, jnp.zeros((LANES - 4 * SUBLANES, LANES), f32)], axis=0))
            urow_ref[:, ls] = u8
            wrow_ref[:, ls] = jnp.exp(u8 - gl8)
        mst_ref[...] = jnp.broadcast_to(m_prev, mst_ref.shape)
        mout_ref[...] = jnp.broadcast_to(m_prev, mout_ref.shape)
        colf_ref[...] = jnp.concatenate(stacks, axis=1).T

    def rope_job(qb):
        def run():
            rows = slice(qb * QBLOCK, (qb + 1) * QBLOCK)
            lane = lax.broadcasted_iota(jnp.int32, (QBLOCK, LANES), 1)
            head_a = ((lane // QUARTER) & 1) == 0
            row0 = lax.broadcasted_iota(jnp.int32, (QBLOCK, LANES), 0) == 0
            cos = cos_ref[rows, :]
            sin = sin_ref[rows, :]
            ss = [_group_sumsq(za_ref[rows, d * 2 * LANES:(d + 1) * 2 * LANES], bd_ref)
                  for d in range(PA_V // (2 * LANES))]
            for j in range(PA_V // LANES):
                is_q = j < ATTN_WIDTH // LANES
                y = _norm_rope_quarters(za_ref[rows, j * LANES:(j + 1) * LANES],
                                        ss[j // 2][:, (j % 2) * LANES:(j % 2 + 1) * LANES],
                                        qg_ref[...] if is_q else kg_ref[...], cos, sin)
                if is_q:
                    y = y * (ATTN_HEAD_DIM ** -0.5)
                    qa_ref[rows, QA_Q + 2 * j * LANES:QA_Q + (2 * j + 1) * LANES] = (
                        jnp.where(head_a, y, 0.0).astype(bf16))
                    qa_ref[rows, QA_Q + (2 * j + 1) * LANES:QA_Q + (2 * j + 2) * LANES] = (
                        jnp.where(head_a, 0.0, y).astype(bf16))
                else:
                    c = j - ATTN_WIDTH // LANES
                    qa_ref[rows, QA_K + c * LANES:QA_K + (c + 1) * LANES] = y.astype(bf16)
            for c in range(KV_HEADS):
                v = za_ref[rows, PA_V + c * LANES:PA_V + (c + 1) * LANES]
                qa_ref[rows, QA_V + c * LANES:QA_V + (c + 1) * LANES] = v.astype(bf16)
                qa_ref[rows, QA_VZ + c * LANES:QA_VZ + (c + 1) * LANES] = (
                    jnp.where(row0, 0.0, v).astype(bf16))
        return run

    return [gates_job] + [rope_job(qb) for qb in range(tb // QBLOCK)]


def _mixer_outproj_jobs(xs_ref, mix_ref, wout_ref, x1_ref):
    def job(c0, c1):
        def run():
            x1_ref[:, c0:c1] = xs_ref[:, c0:c1] + jnp.dot(mix_ref[...], wout_ref[:, c0:c1],
                                                          preferred_element_type=f32)
        return run

    return [job(c0, c1) for c0, c1 in _col_chunks(D_MODEL)]


def _mixer_window_out(xs_ref, anorm_ref, w_ref, kgs_ref, coss_ref, sins_ref, kout_ref, vout_ref):
    tb = xs_ref.shape[0]
    h = _rms(xs_ref[tb - WINDOW:, :], anorm_ref[...]).astype(bf16)
    zs = jnp.dot(h, w_ref[:, COL_KA:COL_QM], preferred_element_type=f32)
    lane_s = lax.broadcasted_iota(jnp.int32, (WINDOW, LANES), 1)
    kout_ref[...] = _headnorm_rope(zs[:, :KV_WIDTH], kgs_ref[...], coss_ref[...], sins_ref[...], lane_s)
    vout_ref[...] = zs[:, KV_WIDTH:]


def _mixer_core(first_block, last_block, fillers, prep_jobs, qa_ref, zm_ref, colf_ref, urow_ref, wrow_ref,
                mix_ref, bias_ref, mnorm_ref, kprev_ref, vprev_ref, cst_ref, cext_ref):
    tb = qa_ref.shape[0]
    nqb = tb // QBLOCK
    nch = tb // MCHUNK
    fillers = list(fillers)
    n_fill = len(fillers)
    slots = len(prep_jobs) + ATTN_PHASES + nch * MLSTM_PHASES
    progress = [0]

    def fill():
        progress[0] += 1
        while n_fill - len(fillers) < min(n_fill, -(-n_fill * progress[0] // slots)):
            fillers.pop(0)()

    low_half = lax.broadcasted_iota(jnp.int32, (QBLOCK, LANES), 1) < ATTN_HEAD_DIM
    qi = lax.broadcasted_iota(jnp.int32, (ATTN_GROUP * QBLOCK, 2 * QBLOCK), 0) & (QBLOCK - 1)
    kj = lax.broadcasted_iota(jnp.int32, (ATTN_GROUP * QBLOCK, 2 * QBLOCK), 1)
    band = (kj > qi) & (kj <= qi + QBLOCK)
    ones_slab = jnp.ones((2 * QBLOCK, LANES), bf16)

    def attn_phases(chains):
        rows_of = lambda qb: slice(qb * QBLOCK, (qb + 1) * QBLOCK)
        st = {}

        def scores():
            for qb, c in chains:
                rows = rows_of(qb)
                kcols = slice(QA_K + c * LANES, QA_K + (c + 1) * LANES)
                if qb == 0:
                    kprev, vprev = kprev_ref[c], vprev_ref[c]
                else:
                    kprev = qa_ref[rows_of(qb - 1), kcols]
                    vprev = qa_ref[rows_of(qb - 1), QA_VZ + c * LANES:QA_VZ + (c + 1) * LANES]
                kcat = jnp.concatenate([kprev, qa_ref[rows, kcols]], axis=0)
                vcat = jnp.concatenate(
                    [vprev, qa_ref[rows, QA_V + c * LANES:QA_V + (c + 1) * LANES]], axis=0)
                st['vext', qb, c] = jnp.concatenate([vcat, ones_slab], axis=1)
                q0 = QA_Q + c * ATTN_GROUP * LANES
                qst = jnp.concatenate([qa_ref[rows, q0 + g * LANES:q0 + (g + 1) * LANES]
                                       for g in range(ATTN_GROUP)], axis=0)
                st['s', qb, c] = lax.dot_general(qst, kcat, (((1,), (1,)), ((), ())),
                                                 preferred_element_type=f32)

        def softmax():
            for qb, c in chains:
                valid = band & (kj >= QBLOCK) if (first_block and qb == 0) else band
                s = jnp.where(valid, st.pop(('s', qb, c)), bias_ref[c])
                st['p', qb, c] = jnp.exp(s - jnp.max(s, axis=-1, keepdims=True)).astype(bf16)

        def values():
            for qb, c in chains:
                st['of', qb, c] = jnp.dot(st.pop(('p', qb, c)), st.pop(('vext', qb, c)),
                                          preferred_element_type=f32)

        def normalise():
            for qb, c in chains:
                of = st.pop(('of', qb, c))
                o = of[:, :LANES] / of[:, LANES:]
                for jj in range(2):
                    pair = jnp.where(low_half, o[(2 * jj) * QBLOCK:(2 * jj + 1) * QBLOCK],
                                     o[(2 * jj + 1) * QBLOCK:(2 * jj + 2) * QBLOCK])
                    col = (2 * c + jj) * LANES
                    mix_ref[rows_of(qb), col:col + LANES] = pair.astype(bf16)

        return [scores, softmax, values, normalise]

    ti = lax.broadcasted_iota(jnp.int32, (MCHUNK, MCHUNK), 0)
    si = lax.broadcasted_iota(jnp.int32, (MCHUNK, MCHUNK), 1)
    causal = si <= ti
    ones_l = jnp.ones((MCHUNK, LANES), bf16)

    cexts = [cst_ref[hd] for hd in range(MLSTM_HEADS)]

    def mlstm_phases(c):
        rows = slice(c * MCHUNK, (c + 1) * MCHUNK)
        heads = range(MLSTM_HEADS)
        hcols = lambda k, hd: slice((k * MLSTM_HEADS + hd) * MLSTM_HEAD_DIM,
                                    (k * MLSTM_HEADS + hd + 1) * MLSTM_HEAD_DIM)
        st = {}

        def scores():
            for hd in heads:
                st['q', hd] = zm_ref[rows, hcols(0, hd)].astype(bf16)
                kf = zm_ref[rows, hcols(1, hd)] * (MLSTM_HEAD_DIM ** -0.5)
                st['kt', hd] = kf.T
                st['v', hd] = jnp.concatenate([zm_ref[rows, hcols(2, hd)].astype(bf16), ones_l], axis=1)
                st['s', hd] = lax.dot_general(st['q', hd], kf.astype(bf16),
                                              (((1,), (1,)), ((), ())), preferred_element_type=f32)

        def decay():
            for hd in heads:
                g_c = colf_ref[rows, hd:hd + 1]
                u_r = urow_ref[hd:hd + 1, rows]
                dmat = jnp.exp(jnp.where(causal, u_r - g_c, NEG))
                st['s', hd] = (st['s', hd] * dmat).astype(bf16)
                st['kt', hd] = (st['kt', hd] * wrow_ref[hd:hd + 1, rows]).astype(bf16)

        def readout():
            for hd in heads:
                a_c = colf_ref[rows, SUBLANES + hd:SUBLANES + hd + 1]
                st['nd', hd] = (
                    a_c * jnp.dot(st.pop(('q', hd)), cexts[hd].astype(bf16), preferred_element_type=f32)
                    + jnp.dot(st.pop(('s', hd)), st['v', hd], preferred_element_type=f32))

        def emit():
            for hd in heads:
                nd = st.pop(('nd', hd))
                emt_c = colf_ref[rows, 2 * SUBLANES + hd:2 * SUBLANES + hd + 1]
                hraw = nd[:, :MLSTM_HEAD_DIM] / jnp.maximum(jnp.abs(nd[:, MLSTM_HEAD_DIM:]), emt_c)
                hn = _rms(hraw, mnorm_ref[hd:hd + 1, :])
                og = zm_ref[rows, hcols(3, hd)]
                mix_ref[rows, ATTN_WIDTH + hd * MLSTM_HEAD_DIM:ATTN_WIDTH + (hd + 1) * MLSTM_HEAD_DIM] = (
                    (hn * jax.nn.sigmoid(og)).astype(bf16))

        def update():
            for hd in heads:
                aend = colf_ref[c * MCHUNK:c * MCHUNK + 1, 3 * SUBLANES + hd:3 * SUBLANES + hd + 1]
                cexts[hd] = aend * cexts[hd] + jnp.dot(st.pop(('kt', hd)), st.pop(('v', hd)),
                                                       preferred_element_type=f32)

        return [scores, decay, readout, emit, update]

    gates_job, rope_jobs = prep_jobs[0], prep_jobs[1:]
    attn = attn_phases([(qb, c) for qb in range(nqb) for c in range(KV_HEADS)])
    chunks = [mlstm_phases(c) for c in range(nch)]
    mlstm = [p for scores, decay, _, _, _ in chunks for p in (scores, decay)]
    for c, (_, _, readout, emit, update) in enumerate(chunks):
        mlstm += [readout] + ([chunks[c - 1][3]] if c else []) + [update]
    mlstm.append(chunks[-1][3])
    assert len(attn) == ATTN_PHASES and len(mlstm) == nch * MLSTM_PHASES
    order = [gates_job] + rope_jobs + attn[:1]
    rest = attn[1:]
    for i in range(max(len(rest), len(mlstm))):
        order += rest[i:i + 1] + mlstm[i:i + 1]
    assert len(order) == slots
    for job in order:
        job()
        fill()
    last_rows = slice(tb - QBLOCK, tb)
    for c in range(KV_HEADS):
        kprev_ref[c] = qa_ref[last_rows, QA_K + c * LANES:QA_K + (c + 1) * LANES]
        vprev_ref[c] = qa_ref[last_rows, QA_VZ + c * LANES:QA_VZ + (c + 1) * LANES]
    for hd in range(MLSTM_HEADS):
        cst_ref[hd] = cexts[hd]
        if last_block:
            cext_ref[hd] = cexts[hd]


def _prompt_mixer_kernel(x_ref, cos_ref, sin_ref, wa_ref, w_ref, wgate_ref, wout_ref, bd_ref, bias_ref,
                         anorm_ref, qg_ref, kg_ref, kgs_ref, coss_ref, sins_ref, gbias_ref, mnorm_ref,
                         x1_ref, kout_ref, vout_ref, cext_ref, mout_ref,
                         za0, za1, zm0, zm1, gz0, gz1, mix0, mix1, xs0, xs1,
                         qa_ref, colf_ref, urow_ref, wrow_ref, kprev_ref, vprev_ref, cst_ref, mst_ref):
    step = pl.program_id(0)
    nblk = pl.num_programs(0) - 2
    za, zm, gz, mix, xs = (za0, za1), (zm0, zm1), (gz0, gz1), (mix0, mix1), (xs0, xs1)

    def run(parity, do_in, do_core, do_out, first_block=False, last_block=False):
        other = 1 - parity
        jobs = []
        if do_out:
            jobs += _mixer_outproj_jobs(xs[parity], mix[parity], wout_ref, x1_ref)
        if do_in:
            jobs += _mixer_proj_jobs(x_ref, anorm_ref, wa_ref, w_ref, wgate_ref, za[parity], zm[parity],
                                     gz[parity])
        if do_core:
            prep = _mixer_prep_jobs(za[other], gz[other], cos_ref, sin_ref, bd_ref, qg_ref, kg_ref,
                                    gbias_ref, qa_ref, colf_ref, urow_ref, wrow_ref, mst_ref, mout_ref)
            _mixer_core(first_block, last_block, jobs, prep, qa_ref, zm[other], colf_ref, urow_ref,
                        wrow_ref, mix[other], bias_ref, mnorm_ref, kprev_ref, vprev_ref, cst_ref, cext_ref)
        else:
            for job in jobs:
                job()
        if last_block:
            _mixer_window_out(xs[other], anorm_ref, w_ref, kgs_ref, coss_ref, sins_ref, kout_ref, vout_ref)
        if do_in:
            xs[parity][...] = x_ref[...]

    @pl.when(step == 0)
    def _first():
        kprev_ref[...] = jnp.zeros_like(kprev_ref)
        vprev_ref[...] = jnp.zeros_like(vprev_ref)
        cst_ref[...] = jnp.zeros_like(cst_ref)
        mst_ref[...] = jnp.zeros_like(mst_ref)
        run(0, True, False, False)

    @pl.when(step == 1)
    def _second():
        run(1, True, True, False, first_block=True)

    steady = (step >= 2) & (step < nblk)

    @pl.when(steady & (step % 2 == 0))
    def _even():
        run(0, True, True, True)

    @pl.when(steady & (step % 2 == 1))
    def _odd():
        run(1, True, True, True)

    @pl.when(step == nblk)
    def _drain_core():
        run(0, False, True, True, last_block=True)

    @pl.when(step == nblk + 1)
    def _drain_out():
        run(1, False, False, True)


def _const_spec(shape, single=False):
    nd = len(shape)
    if single:
        return pl.BlockSpec(shape, lambda i, *_: (0,) * nd, pipeline_mode=pl.Buffered(1))
    return pl.BlockSpec(shape, lambda i, *_: (0,) * nd)


def _prompt_mixer(x, cos, sin, wa, w, wgate, wout_b, bd, bias, anorm, qg, kg, kgs, coss, sins, gbias, mnorm):
    t = x.shape[0]
    tb = PROMPT_BLOCK
    nblk = t // tb
    assert nblk % 2 == 0 and nblk >= 4
    state_shape = (MLSTM_HEADS, MLSTM_HEAD_DIM, 2 * MLSTM_HEAD_DIM)
    last = nblk - 1
    lag = lambda d: (lambda i: (jnp.clip(i - d, 0, last), 0))
    return pl.pallas_call(
        _prompt_mixer_kernel,
        grid=(nblk + 2,),
        in_specs=[
            pl.BlockSpec((tb, D_MODEL), lag(0)),
            pl.BlockSpec((tb, LANES), lag(1)),
            pl.BlockSpec((tb, LANES), lag(1)),
            _const_spec((D_MODEL, PA_WIDTH), single=True),
            _const_spec((D_MODEL, IN_WIDTH), single=True),
            _const_spec((D_MODEL, LANES), single=True),
            _const_spec((MIX_WIDTH, D_MODEL), single=True),
            _const_spec((2 * LANES, 2 * LANES), single=True),
            _const_spec((KV_HEADS, ATTN_GROUP * QBLOCK, 2 * QBLOCK), single=True),
            _const_spec((1, D_MODEL)),
            _const_spec((1, LANES)),
            _const_spec((1, LANES)),
            _const_spec((1, LANES)),
            _const_spec((WINDOW, LANES)),
            _const_spec((WINDOW, LANES)),
            _const_spec((1, LANES)),
            _const_spec((MLSTM_HEADS, MLSTM_HEAD_DIM)),
        ],
        out_specs=[
            pl.BlockSpec((tb, D_MODEL), lag(2)),
            _const_spec((WINDOW, KV_WIDTH)),
            _const_spec((WINDOW, KV_WIDTH)),
            _const_spec(state_shape),
            _const_spec((SUBLANES, LANES)),
        ],
        out_shape=[
            jax.ShapeDtypeStruct((t, D_MODEL), f32),
            jax.ShapeDtypeStruct((WINDOW, KV_WIDTH), f32),
            jax.ShapeDtypeStruct((WINDOW, KV_WIDTH), f32),
            jax.ShapeDtypeStruct(state_shape, f32),
            jax.ShapeDtypeStruct((SUBLANES, LANES), f32),
        ],
        scratch_shapes=(
            [pltpu.VMEM((tb, PA_WIDTH), f32)] * 2 + [pltpu.VMEM((tb, ZM_WIDTH), f32)] * 2
            + [pltpu.VMEM((tb, LANES), f32)] * 2
            + [pltpu.VMEM((tb, MIX_WIDTH), bf16)] * 2 + [pltpu.VMEM((tb, D_MODEL), f32)] * 2
            + [pltpu.VMEM((tb, QA_WIDTH), bf16), pltpu.VMEM((tb, LANES), f32)]
            + [pltpu.VMEM((SUBLANES, tb), f32)] * 2
            + [pltpu.VMEM((KV_HEADS, WINDOW, LANES), bf16)] * 2
            + [pltpu.VMEM(state_shape, f32), pltpu.VMEM((SUBLANES, LANES), f32)]),
        compiler_params=pltpu.CompilerParams(
            dimension_semantics=("arbitrary",), vmem_limit_bytes=VMEM_LIMIT),
        name="prompt_mixer",
    )(x, cos, sin, wa, w, wgate, wout_b, bd, bias, anorm, qg, kg, kgs, coss, sins, gbias, mnorm)


def _ffn_kernel(xp_ref, xs_ref, g_ref, wg_ref, wu_ref, wd_ref, op_ref, os_ref):
    step = pl.program_id(0)
    last = pl.num_programs(0) - 1

    @pl.when(step < last)
    def _prompt_rows():
        _ffn_rows(xp_ref, g_ref, wg_ref, wu_ref, wd_ref, op_ref)

    @pl.when(step == last)
    def _sample_rows():
        _ffn_rows(xs_ref, g_ref, wg_ref, wu_ref, wd_ref, os_ref)


def _ffn_rows(x_ref, g_ref, wg_ref, wu_ref, wd_ref, o_ref):
    x = x_ref[...]
    hf = _rms(x, g_ref[...]).astype(bf16)
    acc = x
    for c in range(D_FF // FFN_CHUNK):
        cs = slice(c * FFN_CHUNK, (c + 1) * FFN_CHUNK)
        gate = jnp.dot(hf, wg_ref[:, cs], preferred_element_type=f32)
        up = jnp.dot(hf, wu_ref[:, cs], preferred_element_type=f32)
        act = (gate * jax.nn.sigmoid(gate) * up).astype(bf16)
        acc = acc + jnp.dot(act, wd_ref[cs, :], preferred_element_type=f32)
    o_ref[...] = acc


def _ffn(x_p, x_s, fnorm, wg_b, wu_b, wd_b):
    n = x_p.shape[0]
    ns = x_s.shape[0]
    tm = FFN_BLOCK
    last = n // tm - 1
    return pl.pallas_call(
        _ffn_kernel,
        grid=(n // tm + 1,),
        in_specs=[
            pl.BlockSpec((tm, D_MODEL), lambda i: (jnp.minimum(i, last), 0)),
            _const_spec((ns, D_MODEL), single=True),
            _const_spec((1, D_MODEL)),
            _const_spec((D_MODEL, D_FF), single=True),
            _const_spec((D_MODEL, D_FF), single=True),
            _const_spec((D_FF, D_MODEL), single=True),
        ],
        out_specs=[pl.BlockSpec((tm, D_MODEL), lambda i: (jnp.minimum(i, last), 0)),
                   _const_spec((ns, D_MODEL))],
        out_shape=[jax.ShapeDtypeStruct((n, D_MODEL), f32), jax.ShapeDtypeStruct((ns, D_MODEL), f32)],
        compiler_params=pltpu.CompilerParams(
            dimension_semantics=("arbitrary",), vmem_limit_bytes=VMEM_LIMIT),
        name="ffn",
    )(x_p, x_s, fnorm, wg_b, wu_b, wd_b)


def _sample_mixer_kernel(x_ref, ckt_ref, cvt_ref, c_ref, n_ref, m_ref, cos_ref, sin_ref, wqs_ref, w_ref,
                         wgate_ref, wout_ref, sink_ref, bd_ref, anorm_ref, qg_ref, kg_ref, gbias_ref,
                         mnorm_ref, x1_ref, nkt_ref, nvt_ref, cn_ref, nn_ref, mn_ref):
    bb, tpad, _ = x_ref.shape
    nrows = bb * tpad
    nreal = SAMPLE_TOKENS
    h = _rms(x_ref[...].reshape(nrows, D_MODEL), anorm_ref[...]).astype(bf16)
    z = jnp.concatenate(
        [jnp.dot(h, wqs_ref[...], preferred_element_type=f32),
         jnp.dot(h, w_ref[:, COL_KA:COL_G], preferred_element_type=f32),
         jnp.dot(h, wgate_ref[...], preferred_element_type=f32)], axis=1)
    lane = lax.broadcasted_iota(jnp.int32, (nrows, LANES), 1)
    low = lane < ATTN_HEAD_DIM
    cos = cos_ref[...]
    sin = sin_ref[...]

    def per_seq(a):
        return a.reshape(bb, tpad, a.shape[-1])

    def norm_rope(xs, gain):
        y = xs * lax.rsqrt(_group_sumsq(xs, bd_ref) * (1.0 / ATTN_HEAD_DIM) + NORM_EPS) * gain
        partner = jnp.where((lane & QUARTER) != 0, pltpu.roll(y, QUARTER, 1),
                            pltpu.roll(y, LANES - QUARTER, 1))
        return y * cos + partner * sin

    q_rows = []
    for j in range(ATTN_GROUP):
        qs = norm_rope(z[:, COL_QA + j * LANES:COL_QA + (j + 1) * LANES], qg_ref[...])
        qs = qs * (ATTN_HEAD_DIM ** -0.5)
        q_rows.append(per_seq(jnp.where(low, qs, 0.0)).astype(bf16))
        q_rows.append(per_seq(jnp.where(low, 0.0, qs)).astype(bf16))
    qbd = jnp.concatenate(q_rows, axis=1)
    knew = norm_rope(z[:, COL_KA:COL_KA + KV_WIDTH], kg_ref[...])
    vnew = z[:, COL_VA:COL_VA + KV_WIDTH]
    zpad = jnp.zeros((bb, LANES - tpad, LANES), bf16)
    knp = jnp.concatenate([per_seq(knew).astype(bf16), zpad], axis=1)
    vnp = jnp.concatenate([per_seq(vnew).astype(bf16), zpad], axis=1)
    ckt = ckt_ref[...]
    cvt = cvt_ref[...]
    s = jnp.concatenate(
        [jnp.einsum('bqd,bdw->bqw', qbd, ckt.astype(bf16), preferred_element_type=f32),
         jnp.einsum('bqd,bkd->bqk', qbd, knp, preferred_element_type=f32)], axis=2)
    tq = lax.broadcasted_iota(jnp.int32, s.shape, 1) & (tpad - 1)
    kj = lax.broadcasted_iota(jnp.int32, s.shape, 2)
    valid = ((kj < WINDOW) & (kj > tq)) | ((kj >= WINDOW) & (kj - WINDOW <= tq) & (kj - WINDOW < nreal))
    s = jnp.where(valid, s, NEG)
    sink = sink_ref[:, 0:1][None]
    mx = jnp.maximum(jnp.max(s, axis=-1, keepdims=True), sink)
    p = jnp.exp(s - mx)
    den = jnp.sum(p, axis=-1, keepdims=True) + jnp.exp(sink - mx)
    pb = p.astype(bf16)
    o = (jnp.einsum('bqw,bdw->bqd', pb[:, :, :WINDOW], cvt.astype(bf16), preferred_element_type=f32)
         + jnp.einsum('bqk,bkd->bqd', pb[:, :, WINDOW:], vnp, preferred_element_type=f32)) / den
    low3 = lax.broadcasted_iota(jnp.int32, (bb, tpad, LANES), 2) < ATTN_HEAD_DIM
    mix_parts = []
    for j in range(ATTN_GROUP):
        r0 = 2 * j * tpad
        pair = jnp.where(low3, o[:, r0:r0 + tpad, :], o[:, r0 + tpad:r0 + 2 * tpad, :])
        mix_parts.append(pair.reshape(nrows, LANES).astype(bf16))

    keep = lax.broadcasted_iota(jnp.int32, (KV_WIDTH, WINDOW), 1) < WINDOW - nreal
    knt = knew.T
    vnt = vnew.T
    for b in range(bb):
        shift = (WINDOW - nreal - b * tpad) % LANES
        nkt_ref[b] = jnp.where(keep, pltpu.roll(ckt_ref[b], WINDOW - nreal, 1), pltpu.roll(knt, shift, 1))
        nvt_ref[b] = jnp.where(keep, pltpu.roll(cvt_ref[b], WINDOW - nreal, 1), pltpu.roll(vnt, shift, 1))

    gz = per_seq(z[:, COL_G:COL_G + LANES] + gbias_ref[...])
    lgz = jax.nn.log_sigmoid(gz)
    trow = lax.broadcasted_iota(jnp.int32, (bb, tpad, 1), 1)
    real = trow < nreal
    mn_ref[...] = jnp.zeros_like(mn_ref)
    heads = range(MLSTM_HEADS)
    hcols = lambda base, hd: slice(base + hd * MLSTM_HEAD_DIM, base + (hd + 1) * MLSTM_HEAD_DIM)
    last = nreal - 1
    st = {}
    for hd in heads:
        m0 = m_ref[:, hd:hd + 1, :]
        ig_c = jnp.where(real, gz[:, :, hd:hd + 1], NEG)
        lf_c = jnp.where(real, lgz[:, :, FG_LANE + hd:FG_LANE + hd + 1], 0.0)
        b_c = jnp.zeros_like(lf_c)
        for sx in range(nreal):
            b_c = b_c + jnp.where(trow >= sx, lf_c[:, sx:sx + 1, :], 0.0)
        dlog = [jnp.where(trow >= sx, b_c - b_c[:, sx:sx + 1, :] + ig_c[:, sx:sx + 1, :], NEG)
                for sx in range(nreal)]
        inter = b_c + m0
        m_t = inter
        for sx in range(nreal):
            m_t = jnp.maximum(m_t, dlog[sx])
        m_new = m_t[:, last:last + 1, :]
        b_last = b_c[:, last:last + 1, :]
        st[hd] = dict(m_t=m_t, a=jnp.exp(inter - m_t), dexp=[jnp.exp(d - m_t) for d in dlog],
                      m_new=m_new, a_end=jnp.exp(b_last + m0 - m_new),
                      w_c=jnp.exp(b_last - b_c + ig_c - m_new))
    for hd in heads:
        s = st[hd]
        s['q'] = per_seq(z[:, hcols(COL_QM, hd)])
        s['k'] = per_seq(z[:, hcols(COL_KM, hd)]) * (MLSTM_HEAD_DIM ** -0.5)
        s['v'] = per_seq(z[:, hcols(COL_VM, hd)])
        s['qc'] = jnp.einsum('btd,bde->bte', s['q'].astype(bf16), c_ref[:, hd].astype(bf16),
                             preferred_element_type=f32)
    for hd in heads:
        s = st[hd]
        q, k, v = s['q'], s['k'], s['v']
        num = s['a'] * s.pop('qc')
        den_m = s['a'] * jnp.sum(q * n_ref[:, hd:hd + 1, :], axis=2, keepdims=True)
        for sx in range(nreal):
            sd = jnp.sum(q * k[:, sx:sx + 1, :], axis=2, keepdims=True) * s['dexp'][sx]
            num = num + sd * v[:, sx:sx + 1, :]
            den_m = den_m + sd
        hraw = num / jnp.maximum(jnp.abs(den_m), jnp.exp(-s['m_t']))
        hn = _rms(hraw, mnorm_ref[hd:hd + 1, :][None])
        og = per_seq(z[:, hcols(COL_OM, hd)])
        mix_parts.append((hn * jax.nn.sigmoid(og)).reshape(nrows, MLSTM_HEAD_DIM).astype(bf16))
    for hd in heads:
        s = st[hd]
        kw = s['k'] * s['w_c']
        cn_ref[:, hd] = s['a_end'] * c_ref[:, hd] + jnp.einsum(
            'bsd,bse->bde', kw.astype(bf16), s['v'].astype(bf16), preferred_element_type=f32)
        nn_ref[:, hd:hd + 1, :] = s['a_end'] * n_ref[:, hd:hd + 1, :] + jnp.sum(kw, axis=1, keepdims=True)
        mn_ref[:, hd:hd + 1, :] = jnp.broadcast_to(s['m_new'], (bb, 1, LANES))

    mix = jnp.concatenate(mix_parts, axis=1)
    x1 = x_ref[...].reshape(nrows, D_MODEL) + jnp.dot(mix, wout_ref[...], preferred_element_type=f32)
    x1_ref[...] = x1.reshape(bb, tpad, D_MODEL)


def _sample_mixer(x_pad, ckt, cvt, c0, n0, m0, cos, sin, wq_s, w, wgate, wout_s, sink_tile, bd, anorm, qg,
                  kg, gbias, mnorm):
    nb, tpad, _ = x_pad.shape
    bb = SAMPLE_BATCH_BLOCK
    nh = MLSTM_HEADS
    blk = lambda shape: pl.BlockSpec(shape, lambda i: (i,) + (0,) * (len(shape) - 1))
    cblk = (bb, nh, MLSTM_HEAD_DIM, MLSTM_HEAD_DIM)
    return pl.pallas_call(
        _sample_mixer_kernel,
        grid=(nb // bb,),
        in_specs=[blk((bb, tpad, D_MODEL)), blk((bb, KV_WIDTH, WINDOW)), blk((bb, KV_WIDTH, WINDOW)),
                  blk(cblk), blk((bb, nh, MLSTM_HEAD_DIM)), blk((bb, nh, 1)),
                  _const_spec((bb * tpad, LANES)), _const_spec((bb * tpad, LANES)),
                  _const_spec((D_MODEL, ATTN_WIDTH), single=True),
                  _const_spec((D_MODEL, IN_WIDTH), single=True),
                  _const_spec((D_MODEL, LANES), single=True),
                  _const_spec((MIX_WIDTH, D_MODEL), single=True),
                  _const_spec((ATTN_HEADS * tpad, LANES)), _const_spec((LANES, LANES)),
                  _const_spec((1, D_MODEL)),
                  _const_spec((1, LANES)), _const_spec((1, LANES)), _const_spec((1, LANES)),
                  _const_spec((nh, MLSTM_HEAD_DIM))],
        out_specs=[blk((bb, tpad, D_MODEL)), blk((bb, KV_WIDTH, WINDOW)), blk((bb, KV_WIDTH, WINDOW)),
                   blk(cblk), blk((bb, nh, MLSTM_HEAD_DIM)), blk((bb, tpad, LANES))],
        out_shape=[jax.ShapeDtypeStruct((nb, tpad, D_MODEL), f32),
                   jax.ShapeDtypeStruct((nb, KV_WIDTH, WINDOW), f32),
                   jax.ShapeDtypeStruct((nb, KV_WIDTH, WINDOW), f32),
                   jax.ShapeDtypeStruct((nb,) + cblk[1:], f32),
                   jax.ShapeDtypeStruct((nb, nh, MLSTM_HEAD_DIM), f32),
                   jax.ShapeDtypeStruct((nb, tpad, LANES), f32)],
        compiler_params=pltpu.CompilerParams(
            dimension_semantics=("arbitrary",), vmem_limit_bytes=VMEM_LIMIT),
        name="sample_mixer",
    )(x_pad, ckt, cvt, c0, n0, m0, cos, sin, wq_s, w, wgate, wout_s, sink_tile, bd, anorm, qg, kg, gbias,
      mnorm)


def _rope_angles(pos):
    half = ATTN_HEAD_DIM // 2
    inv = ROPE_THETA ** (-np.arange(half, dtype=np.float64) / half)
    ang = pos.astype(np.float64)[:, None] * inv[None, :]
    return np.cos(ang).astype(np.float32), np.sin(ang).astype(np.float32)


def _rope_tables(pos):
    c, s = _rope_angles(pos)
    cos = np.tile(c, (1, LANES // QUARTER))
    sin = np.tile(np.concatenate([-s, s], axis=1), (1, LANES // ATTN_HEAD_DIM))
    return cos, sin


def _rope_tables_quarters(pos):
    c, s = _rope_angles(pos)
    return np.tile(c, (1, LANES // QUARTER)), np.concatenate([-s, -s, s, s], axis=1)


def _quarters(a):
    lo, hi = a[..., :QUARTER], a[..., QUARTER:]
    return jnp.concatenate([lo, lo, hi, hi], axis=-1)


def _prompt_attn_weights(w):
    d = w.shape[0]
    wq = w[:, COL_QA:COL_KA].reshape(d, ATTN_WIDTH // LANES, 2, 2, QUARTER)
    wq = wq.transpose(0, 1, 3, 2, 4).reshape(d, ATTN_WIDTH)
    wk = _quarters(w[:, COL_KA:COL_VA].reshape(d, KV_HEADS, ATTN_HEAD_DIM)).reshape(d, KV_HEADS * LANES)
    wv = w[:, COL_VA:COL_QM].reshape(d, KV_HEADS, 1, ATTN_HEAD_DIM)
    wv = jnp.broadcast_to(wv, (d, KV_HEADS, 2, ATTN_HEAD_DIM)).reshape(d, KV_HEADS * LANES)
    return jnp.concatenate([wq, wk, wv], axis=1)


def kernel(x_prompt, x_sample, cache_k, cache_v, state_C, state_n, state_m, attn_norm, w_in, q_norm,
           k_norm, attn_sinks, b_ig, b_fg, mlstm_norm, w_out, ffn_norm, w_gate, w_up, w_down):
    assert w_in.shape[0] == 1 and x_prompt.shape[0] == 1
    tp = x_prompt.shape[1]
    nb, nt = x_sample.shape[0], x_sample.shape[1]
    assert nt == SAMPLE_TOKENS
    tpad = SUBLANES
    nh = MLSTM_HEADS

    w = w_in[0].astype(bf16)
    pad_a = jnp.zeros((D_MODEL, FG_LANE - nh), bf16)
    pad_b = jnp.zeros((D_MODEL, LANES - FG_LANE - nh), bf16)
    wgate = jnp.concatenate([w[:, COL_G:COL_G + nh], pad_a, w[:, COL_G + nh:], pad_b], axis=1)
    gbias = jnp.concatenate(
        [b_ig[0], jnp.zeros((FG_LANE - nh,), f32), b_fg[0], jnp.zeros((LANES - FG_LANE - nh,), f32)]
    ).reshape(1, LANES)
    wout_b = w_out[0].astype(bf16)
    wg_b = w_gate[0].astype(bf16)
    wu_b = w_up[0].astype(bf16)
    wd_b = w_down[0].astype(bf16)
    anorm = attn_norm[0].reshape(1, D_MODEL)
    fnorm = ffn_norm[0].reshape(1, D_MODEL)
    qg = jnp.tile(q_norm[0], LANES // ATTN_HEAD_DIM).reshape(1, LANES)
    kg = jnp.tile(k_norm[0], LANES // ATTN_HEAD_DIM).reshape(1, LANES)
    mnorm = mlstm_norm[0].reshape(nh, MLSTM_HEAD_DIM)
    sinks = attn_sinks[0]

    wa = _prompt_attn_weights(w)
    idx = np.arange(2 * LANES)
    same = (idx[:, None] // LANES == idx[None, :] // LANES) & (
        (idx[:, None] // QUARTER) % 2 == (idx[None, :] // QUARTER) % 2)
    bd = jnp.asarray(same, dtype=bf16)
    sink_rows_p = jnp.repeat(sinks.reshape(KV_HEADS, ATTN_GROUP), QBLOCK, axis=1)
    bias = jnp.where(jnp.arange(2 * QBLOCK)[None, None, :] == 0, sink_rows_p[:, :, None], NEG)
    qgq = _quarters(q_norm[0]).reshape(1, LANES)
    kgq = _quarters(k_norm[0]).reshape(1, LANES)
    pos_p = np.arange(tp, dtype=np.float32)
    cos_p, sin_p = _rope_tables_quarters(pos_p)
    cos_w, sin_w = _rope_tables(pos_p[tp - WINDOW:])
    x1_p, k_p, v_p, cext_p, m_p = _prompt_mixer(
        x_prompt[0], cos_p, sin_p, wa, w, wgate, wout_b, bd, bias, anorm, qgq, kgq, kg, cos_w, sin_w,
        gbias, mnorm)

    wq_s = w[:, COL_QA:COL_KA].reshape(D_MODEL, KV_HEADS, ATTN_GROUP, ATTN_HEAD_DIM)
    wq_s = wq_s.transpose(0, 2, 1, 3).reshape(D_MODEL, ATTN_WIDTH)
    wo_a = wout_b[:ATTN_WIDTH].reshape(KV_HEADS, ATTN_GROUP, ATTN_HEAD_DIM, D_MODEL)
    wo_a = wo_a.transpose(1, 0, 2, 3).reshape(ATTN_WIDTH, D_MODEL)
    wout_s = jnp.concatenate([wo_a, wout_b[ATTN_WIDTH:]], axis=0)
    sink_tile = jnp.broadcast_to(
        jnp.repeat(sinks.reshape(KV_HEADS, ATTN_GROUP).T.reshape(-1), tpad)[:, None],
        (ATTN_HEADS * tpad, LANES))
    lanes = np.arange(LANES)
    bd_s = jnp.asarray(lanes[:, None] // ATTN_HEAD_DIM == lanes[None, :] // ATTN_HEAD_DIM, dtype=bf16)
    cos_s, sin_s = _rope_tables(np.arange(tpad, dtype=np.float32) + np.float32(PAST_LEN))
    cos_s = np.tile(cos_s, (SAMPLE_BATCH_BLOCK, 1))
    sin_s = np.tile(sin_s, (SAMPLE_BATCH_BLOCK, 1))
    x_pad = jnp.pad(x_sample, ((0, 0), (0, tpad - nt), (0, 0)))
    ckt = cache_k[0].reshape(nb, WINDOW, KV_WIDTH).transpose(0, 2, 1)
    cvt = cache_v[0].reshape(nb, WINDOW, KV_WIDTH).transpose(0, 2, 1)
    x1_pad, nkt, nvt, c_new, n_new, m_pad = _sample_mixer(
        x_pad, ckt, cvt, state_C[0], state_n[0], state_m[0][:, :, None], cos_s, sin_s, wq_s, w, wgate,
        wout_s, sink_tile, bd_s, anorm, qg, kg, gbias, mnorm)
    y_p, y_s = _ffn(x1_p, x1_pad[:, :nt].reshape(nb * nt, D_MODEL), fnorm, wg_b, wu_b, wd_b)
    m_new = m_pad[:, :nh, 0]

    new_k_s = nkt.transpose(0, 2, 1)
    new_v_s = nvt.transpose(0, 2, 1)

    kv_shape = (1, 1, WINDOW, KV_HEADS, ATTN_HEAD_DIM)
    return (
        y_p[None],
        y_s.reshape(nb, nt, D_MODEL),
        k_p.reshape(kv_shape),
        v_p.reshape(kv_shape),
        cext_p[None, None, :, :, :MLSTM_HEAD_DIM],
        cext_p[None, None, :, :, MLSTM_HEAD_DIM],
        m_p[None, None, :nh, 0],
        new_k_s.reshape(1, nb, WINDOW, KV_HEADS, ATTN_HEAD_DIM),
        new_v_s.reshape(1, nb, WINDOW, KV_HEADS, ATTN_HEAD_DIM),
        c_new.reshape(1, nb, nh, MLSTM_HEAD_DIM, MLSTM_HEAD_DIM),
        n_new.reshape(1, nb, nh, MLSTM_HEAD_DIM),
        m_new.reshape(1, nb, nh),
    )
```

```python
import jax
import jax.numpy as jnp
import numpy as np
from jax import lax
from jax.experimental import pallas as pl
from jax.experimental.pallas import tpu as pltpu

D_MODEL = 1024
PAST_LEN = 16384
ATTN_HEADS = 8
KV_HEADS = 2
ATTN_HEAD_DIM = 64
ATTN_GROUP = ATTN_HEADS // KV_HEADS
ATTN_WIDTH = ATTN_HEADS * ATTN_HEAD_DIM
KV_WIDTH = KV_HEADS * ATTN_HEAD_DIM
WINDOW = 128
ROPE_THETA = 10000.0
MLSTM_HEADS = 4
MLSTM_HEAD_DIM = 128
MLSTM_WIDTH = MLSTM_HEADS * MLSTM_HEAD_DIM
MIX_WIDTH = ATTN_WIDTH + MLSTM_WIDTH
D_FF = 2816
NORM_EPS = 1e-6

LANES = 128
SUBLANES = 8
VMEM_LIMIT = 56 * 1024 * 1024

COL_QA = 0
COL_KA = COL_QA + ATTN_WIDTH
COL_VA = COL_KA + KV_WIDTH
COL_QM = COL_VA + KV_WIDTH
COL_KM = COL_QM + MLSTM_WIDTH
COL_VM = COL_KM + MLSTM_WIDTH
COL_OM = COL_VM + MLSTM_WIDTH
COL_G = COL_OM + MLSTM_WIDTH
IN_WIDTH = COL_G + 2 * MLSTM_HEADS
FG_LANE = SUBLANES

PROMPT_BLOCK = 256
QBLOCK = WINDOW
MCHUNK = 128
PROJ_CHUNK = 256
FFN_BLOCK = 512
FFN_CHUNK = 256
SAMPLE_BATCH_BLOCK = 16
SAMPLE_TOKENS = 4
NEG = -1e30

f32 = jnp.float32
bf16 = jnp.bfloat16


def _rms(x, gain):
    return x * lax.rsqrt(jnp.mean(x * x, axis=-1, keepdims=True) + NORM_EPS) * gain


def _segsum64(s, lane):
    for k in (1, 2, 4, 8, 16, 32):
        s = s + jnp.where((lane & k) != 0, pltpu.roll(s, k, 1), pltpu.roll(s, LANES - k, 1))
    return s


def _headnorm_rope(xs, gain, cos, sin_signed, lane):
    ss = _segsum64(xs * xs, lane)
    y = xs * lax.rsqrt(ss * (1.0 / ATTN_HEAD_DIM) + NORM_EPS) * gain
    partner = jnp.where((lane & 32) != 0, pltpu.roll(y, 32, 1), pltpu.roll(y, LANES - 32, 1))
    return y * cos + partner * sin_signed


def _group_sumsq(xs, bd_ref):
    x2 = xs * xs
    hi = x2.astype(bf16)
    lo = (x2 - hi.astype(f32)).astype(bf16)
    return (jnp.dot(hi, bd_ref[...], preferred_element_type=f32)
            + jnp.dot(lo, bd_ref[...], preferred_element_type=f32))


PA_Q = 0
PA_K = PA_Q + ATTN_WIDTH
PA_V = PA_K + KV_HEADS * LANES
PA_WIDTH = PA_V + KV_HEADS * LANES
QUARTER = ATTN_HEAD_DIM // 2
QA_Q = 0
QA_K = QA_Q + 2 * ATTN_WIDTH
QA_V = QA_K + KV_HEADS * LANES
QA_VZ = QA_V + KV_HEADS * LANES
QA_WIDTH = QA_VZ + KV_HEADS * LANES
ZM_WIDTH = 4 * MLSTM_WIDTH
ATTN_PHASES = 4
MLSTM_PHASES = 5


def _norm_rope_quarters(xs, ss, gain, cos, sin_signed):
    y = xs * lax.rsqrt(ss * (1.0 / ATTN_HEAD_DIM) + NORM_EPS) * gain
    return y * cos + pltpu.roll(y, LANES // 2, 1) * sin_signed


def _col_chunks(width):
    return [(c, min(c + PROJ_CHUNK, width)) for c in range(0, width, PROJ_CHUNK)]


def _mixer_proj_jobs(x_ref, anorm_ref, wa_ref, w_ref, wgate_ref, za_ref, zm_ref, gz_ref):
    h = _rms(x_ref[...], anorm_ref[...]).astype(bf16)

    def proj_job(w_src, wc0, z_ref, c0, c1):
        def run():
            z_ref[:, c0:c1] = jnp.dot(h, w_src[:, wc0 + c0:wc0 + c1], preferred_element_type=f32)
        return run

    return ([proj_job(wa_ref, 0, za_ref, c0, c1) for c0, c1 in _col_chunks(PA_WIDTH)]
            + [proj_job(w_ref, COL_QM, zm_ref, c0, c1) for c0, c1 in _col_chunks(ZM_WIDTH)]
            + [proj_job(wgate_ref, 0, gz_ref, 0, LANES)])


def _mixer_prep_jobs(za_ref, gz_ref, cos_ref, sin_ref, bd_ref, qg_ref, kg_ref, gbias_ref, qa_ref,
                     colf_ref, urow_ref, wrow_ref, mst_ref, mout_ref):
    tb = za_ref.shape[0]

    def gates_job():
        lane_t = lax.broadcasted_iota(jnp.int32, (tb, LANES), 1)
        gcol = gz_ref[...] + gbias_ref[...]
        acol = jnp.where(lane_t < FG_LANE, gcol, jax.nn.log_sigmoid(gcol))
        arow = acol.T
        lane8 = lax.broadcasted_iota(jnp.int32, (SUBLANES, LANES), 1)
        nsb = tb // LANES
        slabs = [slice(sb * LANES, (sb + 1) * LANES) for sb in range(nsb)]
        ig_all = jnp.concatenate([arow[0:SUBLANES, ls] for ls in slabs], axis=0)
        b_all = jnp.concatenate([arow[FG_LANE:FG_LANE + SUBLANES, ls] for ls in slabs], axis=0)
        lane_in = lax.broadcasted_iota(jnp.int32, b_all.shape, 1) & (MCHUNK - 1)

        def scan(x, combine, identity):
            k = 1
            while k < MCHUNK:
                terms = [jnp.where(lane_in >= j * k, pltpu.roll(x, j * k, 1), identity)
                         for j in (1, 2, 3) if j * k < MCHUNK]
                for t in terms:
                    x = combine(x, t)
                k *= 4
            return x

        b_all = scan(b_all, jnp.add, 0.0)
        u_all = ig_all - b_all
        cm_all = scan(u_all, jnp.maximum, NEG)
        m_prev = mst_ref[:, 0:1]
        stacks = []
        ends = {}
        for sb in range(nsb):
            sub = slice(sb * SUBLANES, (sb + 1) * SUBLANES)
            for c in range(LANES // MCHUNK):
                at_end = lane8 == c * MCHUNK + MCHUNK - 1
                ends[sb, c] = (jnp.max(jnp.where(at_end, cm_all[sub], NEG), axis=1, keepdims=True),
                               jnp.max(jnp.where(at_end, b_all[sub], NEG), axis=1, keepdims=True))
        for sb, ls in enumerate(slabs):
            sub = slice(sb * SUBLANES, (sb + 1) * SUBLANES)
            b8, u8, cm8 = b_all[sub], u_all[sub], cm_all[sub]
            g8 = jnp.zeros_like(u8)
            mp8 = jnp.zeros_like(u8)
            gl8 = jnp.zeros_like(u8)
            for c in range(LANES // MCHUNK):
                in_chunk = (lane8 // MCHUNK) == c
                gc = jnp.maximum(cm8, m_prev)
                cm_last, b_last = ends[sb, c]
                g_last = jnp.maximum(cm_last, m_prev)
                g8 = jnp.where(in_chunk, gc, g8)
                mp8 = jnp.where(in_chunk, m_prev, mp8)
                gl8 = jnp.where(in_chunk, g_last, gl8)
                m_prev = b_last + g_last
            a8 = jnp.exp(mp8 - g8)
            emt8 = jnp.exp(-(b8 + g8))
            aend8 = jnp.exp(mp8 - gl8)
            stacks.append(jnp.concatenate(
                [g8, a8, emt8, aend8, jnp.zeros((LANES - 4 * SUBLANES, LANES), f32)], axis=0))
            urow_ref[:, ls] = u8
            wrow_ref[:, ls] = jnp.exp(u8 - gl8)
        mst_ref[...] = jnp.broadcast_to(m_prev, mst_ref.shape)
        mout_ref[...] = jnp.broadcast_to(m_prev, mout_ref.shape)
        colf_ref[...] = jnp.concatenate(stacks, axis=1).T

    def rope_job(qb):
        def run():
            rows = slice(qb * QBLOCK, (qb + 1) * QBLOCK)
            lane = lax.broadcasted_iota(jnp.int32, (QBLOCK, LANES), 1)
            head_a = ((lane // QUARTER) & 1) == 0
            row0 = lax.broadcasted_iota(jnp.int32, (QBLOCK, LANES), 0) == 0
            cos = cos_ref[rows, :]
            sin = sin_ref[rows, :]
            ss = [_group_sumsq(za_ref[rows, d * 2 * LANES:(d + 1) * 2 * LANES], bd_ref)
                  for d in range(PA_V // (2 * LANES))]
            for j in range(PA_V // LANES):
                is_q = j < ATTN_WIDTH // LANES
                y = _norm_rope_quarters(za_ref[rows, j * LANES:(j + 1) * LANES],
                                        ss[j // 2][:, (j % 2) * LANES:(j % 2 + 1) * LANES],
                                        qg_ref[...] if is_q else kg_ref[...], cos, sin)
                if is_q:
                    y = y * (ATTN_HEAD_DIM ** -0.5)
                    qa_ref[rows, QA_Q + 2 * j * LANES:QA_Q + (2 * j + 1) * LANES] = (
                        jnp.where(head_a, y, 0.0).astype(bf16))
                    qa_ref[rows, QA_Q + (2 * j + 1) * LANES:QA_Q + (2 * j + 2) * LANES] = (
                        jnp.where(head_a, 0.0, y).astype(bf16))
                else:
                    c = j - ATTN_WIDTH // LANES
                    qa_ref[rows, QA_K + c * LANES:QA_K + (c + 1) * LANES] = y.astype(bf16)
            for c in range(KV_HEADS):
                v = za_ref[rows, PA_V + c * LANES:PA_V + (c + 1) * LANES]
                qa_ref[rows, QA_V + c * LANES:QA_V + (c + 1) * LANES] = v.astype(bf16)
                qa_ref[rows, QA_VZ + c * LANES:QA_VZ + (c + 1) * LANES] = (
                    jnp.where(row0, 0.0, v).astype(bf16))
        return run

    return [gates_job] + [rope_job(qb) for qb in range(tb // QBLOCK)]


def _mixer_outproj_jobs(xs_ref, mix_ref, wout_ref, x1_ref):
    def job(c0, c1):
        def run():
            x1_ref[:, c0:c1] = xs_ref[:, c0:c1] + jnp.dot(mix_ref[...], wout_ref[:, c0:c1],
                                                          preferred_element_type=f32)
        return run

    return [job(c0, c1) for c0, c1 in _col_chunks(D_MODEL)]


def _mixer_window_out(xs_ref, anorm_ref, w_ref, kgs_ref, coss_ref, sins_ref, kout_ref, vout_ref):
    tb = xs_ref.shape[0]
    h = _rms(xs_ref[tb - WINDOW:, :], anorm_ref[...]).astype(bf16)
    zs = jnp.dot(h, w_ref[:, COL_KA:COL_QM], preferred_element_type=f32)
    lane_s = lax.broadcasted_iota(jnp.int32, (WINDOW, LANES), 1)
    kout_ref[...] = _headnorm_rope(zs[:, :KV_WIDTH], kgs_ref[...], coss_ref[...], sins_ref[...], lane_s)
    vout_ref[...] = zs[:, KV_WIDTH:]


def _mixer_core(first_block, last_block, fillers, prep_jobs, qa_ref, zm_ref, colf_ref, urow_ref, wrow_ref,
                mix_ref, bias_ref, mnorm_ref, kprev_ref, vprev_ref, cst_ref, cext_ref):
    tb = qa_ref.shape[0]
    nqb = tb // QBLOCK
    nch = tb // MCHUNK
    fillers = list(fillers)
    n_fill = len(fillers)
    slots = len(prep_jobs) + ATTN_PHASES + nch * MLSTM_PHASES
    progress = [0]

    def fill():
        progress[0] += 1
        while n_fill - len(fillers) < min(n_fill, -(-n_fill * progress[0] // slots)):
            fillers.pop(0)()

    low_half = lax.broadcasted_iota(jnp.int32, (QBLOCK, LANES), 1) < ATTN_HEAD_DIM
    qi = lax.broadcasted_iota(jnp.int32, (ATTN_GROUP * QBLOCK, 2 * QBLOCK), 0) & (QBLOCK - 1)
    kj = lax.broadcasted_iota(jnp.int32, (ATTN_GROUP * QBLOCK, 2 * QBLOCK), 1)
    band = (kj > qi) & (kj <= qi + QBLOCK)
    ones_slab = jnp.ones((2 * QBLOCK, LANES), bf16)

    def attn_phases(chains):
        rows_of = lambda qb: slice(qb * QBLOCK, (qb + 1) * QBLOCK)
        st = {}

        def scores():
            for qb, c in chains:
                rows = rows_of(qb)
                kcols = slice(QA_K + c * LANES, QA_K + (c + 1) * LANES)
                if qb == 0:
                    kprev, vprev = kprev_ref[c], vprev_ref[c]
                else:
                    kprev = qa_ref[rows_of(qb - 1), kcols]
                    vprev = qa_ref[rows_of(qb - 1), QA_VZ + c * LANES:QA_VZ + (c + 1) * LANES]
                kcat = jnp.concatenate([kprev, qa_ref[rows, kcols]], axis=0)
                vcat = jnp.concatenate(
                    [vprev, qa_ref[rows, QA_V + c * LANES:QA_V + (c + 1) * LANES]], axis=0)
                st['vext', qb, c] = jnp.concatenate([vcat, ones_slab], axis=1)
                q0 = QA_Q + c * ATTN_GROUP * LANES
                qst = jnp.concatenate([qa_ref[rows, q0 + g * LANES:q0 + (g + 1) * LANES]
                                       for g in range(ATTN_GROUP)], axis=0)
                st['s', qb, c] = lax.dot_general(qst, kcat, (((1,), (1,)), ((), ())),
                                                 preferred_element_type=f32)

        def softmax():
            for qb, c in chains:
                valid = band & (kj >= QBLOCK) if (first_block and qb == 0) else band
                s = jnp.where(valid, st.pop(('s', qb, c)), bias_ref[c])
                st['p', qb, c] = jnp.exp(s - jnp.max(s, axis=-1, keepdims=True)).astype(bf16)

        def values():
            for qb, c in chains:
                st['of', qb, c] = jnp.dot(st.pop(('p', qb, c)), st.pop(('vext', qb, c)),
                                          preferred_element_type=f32)

        def normalise():
            for qb, c in chains:
                of = st.pop(('of', qb, c))
                o = of[:, :LANES] / of[:, LANES:]
                for jj in range(2):
                    pair = jnp.where(low_half, o[(2 * jj) * QBLOCK:(2 * jj + 1) * QBLOCK],
                                     o[(2 * jj + 1) * QBLOCK:(2 * jj + 2) * QBLOCK])
                    col = (2 * c + jj) * LANES
                    mix_ref[rows_of(qb), col:col + LANES] = pair.astype(bf16)

        return [scores, softmax, values, normalise]

    ti = lax.broadcasted_iota(jnp.int32, (MCHUNK, MCHUNK), 0)
    si = lax.broadcasted_iota(jnp.int32, (MCHUNK, MCHUNK), 1)
    causal = si <= ti
    ones_l = jnp.ones((MCHUNK, LANES), bf16)

    cexts = [cst_ref[hd] for hd in range(MLSTM_HEADS)]

    def mlstm_phases(c):
        rows = slice(c * MCHUNK, (c + 1) * MCHUNK)
        heads = range(MLSTM_HEADS)
        hcols = lambda k, hd: slice((k * MLSTM_HEADS + hd) * MLSTM_HEAD_DIM,
                                    (k * MLSTM_HEADS + hd + 1) * MLSTM_HEAD_DIM)
        st = {}

        def scores():
            for hd in heads:
                st['q', hd] = zm_ref[rows, hcols(0, hd)].astype(bf16)
                kf = zm_ref[rows, hcols(1, hd)] * (MLSTM_HEAD_DIM ** -0.5)
                st['kt', hd] = kf.T
                st['v', hd] = jnp.concatenate([zm_ref[rows, hcols(2, hd)].astype(bf16), ones_l], axis=1)
                st['s', hd] = lax.dot_general(st['q', hd], kf.astype(bf16),
                                              (((1,), (1,)), ((), ())), preferred_element_type=f32)

        def decay():
            for hd in heads:
                g_c = colf_ref[rows, hd:hd + 1]
                u_r = urow_ref[hd:hd + 1, rows]
                dmat = jnp.exp(jnp.where(causal, u_r - g_c, NEG))
                st['s', hd] = (st['s', hd] * dmat).astype(bf16)
                st['kt', hd] = (st['kt', hd] * wrow_ref[hd:hd + 1, rows]).astype(bf16)

        def readout():
            for hd in heads:
                a_c = colf_ref[rows, SUBLANES + hd:SUBLANES + hd + 1]
                st['nd', hd] = (
                    a_c * jnp.dot(st.pop(('q', hd)), cexts[hd].astype(bf16), preferred_element_type=f32)
                    + jnp.dot(st.pop(('s', hd)), st['v', hd], preferred_element_type=f32))

        def emit():
            for hd in heads:
                nd = st.pop(('nd', hd))
                emt_c = colf_ref[rows, 2 * SUBLANES + hd:2 * SUBLANES + hd + 1]
                hraw = nd[:, :MLSTM_HEAD_DIM] / jnp.maximum(jnp.abs(nd[:, MLSTM_HEAD_DIM:]), emt_c)
                hn = _rms(hraw, mnorm_ref[hd:hd + 1, :])
                og = zm_ref[rows, hcols(3, hd)]
                mix_ref[rows, ATTN_WIDTH + hd * MLSTM_HEAD_DIM:ATTN_WIDTH + (hd + 1) * MLSTM_HEAD_DIM] = (
                    (hn * jax.nn.sigmoid(og)).astype(bf16))

        def update():
            for hd in heads:
                aend = colf_ref[c * MCHUNK:c * MCHUNK + 1, 3 * SUBLANES + hd:3 * SUBLANES + hd + 1]
                cexts[hd] = aend * cexts[hd] + jnp.dot(st.pop(('kt', hd)), st.pop(('v', hd)),
                                                       preferred_element_type=f32)

        return [scores, decay, readout, emit, update]

    gates_job, rope_jobs = prep_jobs[0], prep_jobs[1:]
    attn = attn_phases([(qb, c) for qb in range(nqb) for c in range(KV_HEADS)])
    chunks = [mlstm_phases(c) for c in range(nch)]
    mlstm = [p for scores, decay, _, _, _ in chunks for p in (scores, decay)]
    for c, (_, _, readout, emit, update) in enumerate(chunks):
        mlstm += [readout] + ([chunks[c - 1][3]] if c else []) + [update]
    mlstm.append(chunks[-1][3])
    assert len(attn) == ATTN_PHASES and len(mlstm) == nch * MLSTM_PHASES
    order = [gates_job] + rope_jobs + attn[:1]
    rest = attn[1:]
    for i in range(max(len(rest), len(mlstm))):
        order += rest[i:i + 1] + mlstm[i:i + 1]
    assert len(order) == slots
    for job in order:
        job()
        fill()
    last_rows = slice(tb - QBLOCK, tb)
    for c in range(KV_HEADS):
        kprev_ref[c] = qa_ref[last_rows, QA_K + c * LANES:QA_K + (c + 1) * LANES]
        vprev_ref[c] = qa_ref[last_rows, QA_VZ + c * LANES:QA_VZ + (c + 1) * LANES]
    for hd in range(MLSTM_HEADS):
        cst_ref[hd] = cexts[hd]
        if last_block:
            cext_ref[hd] = cexts[hd]


def _prompt_mixer_kernel(x_ref, cos_ref, sin_ref, wa_ref, w_ref, wgate_ref, wout_ref, bd_ref, bias_ref,
                         anorm_ref, qg_ref, kg_ref, kgs_ref, coss_ref, sins_ref, gbias_ref, mnorm_ref,
                         x1_ref, kout_ref, vout_ref, cext_ref, mout_ref,
                         za0, za1, zm0, zm1, gz0, gz1, mix0, mix1, xs0, xs1,
                         qa_ref, colf_ref, urow_ref, wrow_ref, kprev_ref, vprev_ref, cst_ref, mst_ref):
    step = pl.program_id(0)
    nblk = pl.num_programs(0) - 2
    za, zm, gz, mix, xs = (za0, za1), (zm0, zm1), (gz0, gz1), (mix0, mix1), (xs0, xs1)

    def run(parity, do_in, do_core, do_out, first_block=False, last_block=False):
        other = 1 - parity
        jobs = []
        if do_out:
            jobs += _mixer_outproj_jobs(xs[parity], mix[parity], wout_ref, x1_ref)
        if do_in:
            jobs += _mixer_proj_jobs(x_ref, anorm_ref, wa_ref, w_ref, wgate_ref, za[parity], zm[parity],
                                     gz[parity])
        if do_core:
            prep = _mixer_prep_jobs(za[other], gz[other], cos_ref, sin_ref, bd_ref, qg_ref, kg_ref,
                                    gbias_ref, qa_ref, colf_ref, urow_ref, wrow_ref, mst_ref, mout_ref)
            _mixer_core(first_block, last_block, jobs, prep, qa_ref, zm[other], colf_ref, urow_ref,
                        wrow_ref, mix[other], bias_ref, mnorm_ref, kprev_ref, vprev_ref, cst_ref, cext_ref)
        else:
            for job in jobs:
                job()
        if last_block:
            _mixer_window_out(xs[other], anorm_ref, w_ref, kgs_ref, coss_ref, sins_ref, kout_ref, vout_ref)
        if do_in:
            xs[parity][...] = x_ref[...]

    @pl.when(step == 0)
    def _first():
        kprev_ref[...] = jnp.zeros_like(kprev_ref)
        vprev_ref[...] = jnp.zeros_like(vprev_ref)
        cst_ref[...] = jnp.zeros_like(cst_ref)
        mst_ref[...] = jnp.zeros_like(mst_ref)
        run(0, True, False, False)

    @pl.when(step == 1)
    def _second():
        run(1, True, True, False, first_block=True)

    steady = (step >= 2) & (step < nblk)

    @pl.when(steady & (step % 2 == 0))
    def _even():
        run(0, True, True, True)

    @pl.when(steady & (step % 2 == 1))
    def _odd():
        run(1, True, True, True)

    @pl.when(step == nblk)
    def _drain_core():
        run(0, False, True, True, last_block=True)

    @pl.when(step == nblk + 1)
    def _drain_out():
        run(1, False, False, True)


def _const_spec(shape, single=False):
    nd = len(shape)
    if single:
        return pl.BlockSpec(shape, lambda i, *_: (0,) * nd, pipeline_mode=pl.Buffered(1))
    return pl.BlockSpec(shape, lambda i, *_: (0,) * nd)


def _prompt_mixer(x, cos, sin, wa, w, wgate, wout_b, bd, bias, anorm, qg, kg, kgs, coss, sins, gbias, mnorm):
    t = x.shape[0]
    tb = PROMPT_BLOCK
    nblk = t // tb
    assert nblk % 2 == 0 and nblk >= 4
    state_shape = (MLSTM_HEADS, MLSTM_HEAD_DIM, 2 * MLSTM_HEAD_DIM)
    last = nblk - 1
    lag = lambda d: (lambda i: (jnp.clip(i - d, 0, last), 0))
    return pl.pallas_call(
        _prompt_mixer_kernel,
        grid=(nblk + 2,),
        in_specs=[
            pl.BlockSpec((tb, D_MODEL), lag(0)),
            pl.BlockSpec((tb, LANES), lag(1)),
            pl.BlockSpec((tb, LANES), lag(1)),
            _const_spec((D_MODEL, PA_WIDTH), single=True),
            _const_spec((D_MODEL, IN_WIDTH), single=True),
            _const_spec((D_MODEL, LANES), single=True),
            _const_spec((MIX_WIDTH, D_MODEL), single=True),
            _const_spec((2 * LANES, 2 * LANES), single=True),
            _const_spec((KV_HEADS, ATTN_GROUP * QBLOCK, 2 * QBLOCK), single=True),
            _const_spec((1, D_MODEL)),
            _const_spec((1, LANES)),
            _const_spec((1, LANES)),
            _const_spec((1, LANES)),
            _const_spec((WINDOW, LANES)),
            _const_spec((WINDOW, LANES)),
            _const_spec((1, LANES)),
            _const_spec((MLSTM_HEADS, MLSTM_HEAD_DIM)),
        ],
        out_specs=[
            pl.BlockSpec((tb, D_MODEL), lag(2)),
            _const_spec((WINDOW, KV_WIDTH)),
            _const_spec((WINDOW, KV_WIDTH)),
            _const_spec(state_shape),
            _const_spec((SUBLANES, LANES)),
        ],
        out_shape=[
            jax.ShapeDtypeStruct((t, D_MODEL), f32),
            jax.ShapeDtypeStruct((WINDOW, KV_WIDTH), f32),
            jax.ShapeDtypeStruct((WINDOW, KV_WIDTH), f32),
            jax.ShapeDtypeStruct(state_shape, f32),
            jax.ShapeDtypeStruct((SUBLANES, LANES), f32),
        ],
        scratch_shapes=(
            [pltpu.VMEM((tb, PA_WIDTH), f32)] * 2 + [pltpu.VMEM((tb, ZM_WIDTH), f32)] * 2
            + [pltpu.VMEM((tb, LANES), f32)] * 2
            + [pltpu.VMEM((tb, MIX_WIDTH), bf16)] * 2 + [pltpu.VMEM((tb, D_MODEL), f32)] * 2
            + [pltpu.VMEM((tb, QA_WIDTH), bf16), pltpu.VMEM((tb, LANES), f32)]
            + [pltpu.VMEM((SUBLANES, tb), f32)] * 2
            + [pltpu.VMEM((KV_HEADS, WINDOW, LANES), bf16)] * 2
            + [pltpu.VMEM(state_shape, f32), pltpu.VMEM((SUBLANES, LANES), f32)]),
        compiler_params=pltpu.CompilerParams(
            dimension_semantics=("arbitrary",), vmem_limit_bytes=VMEM_LIMIT),
        name="prompt_mixer",
    )(x, cos, sin, wa, w, wgate, wout_b, bd, bias, anorm, qg, kg, kgs, coss, sins, gbias, mnorm)


def _ffn_kernel(xp_ref, xs_ref, g_ref, wg_ref, wu_ref, wd_ref, op_ref, os_ref, act_ref):
    step = pl.program_id(0)
    last = pl.num_programs(0) - 1

    @pl.when(step < last)
    def _prompt_rows():
        _ffn_rows(xp_ref, g_ref, wg_ref, wu_ref, wd_ref, op_ref, act_ref)

    @pl.when(step == last)
    def _sample_rows():
        _ffn_rows(xs_ref, g_ref, wg_ref, wu_ref, wd_ref, os_ref, act_ref)


def _ffn_rows(x_ref, g_ref, wg_ref, wu_ref, wd_ref, o_ref, act_ref):
    hf = _rms(x_ref[...], g_ref[...]).astype(bf16)
    for c0, c1 in _col_chunks(D_FF):
        gate = jnp.dot(hf, wg_ref[:, c0:c1], preferred_element_type=f32)
        up = jnp.dot(hf, wu_ref[:, c0:c1], preferred_element_type=f32)
        act_ref[:, c0:c1] = (gate * jax.nn.sigmoid(gate) * up).astype(bf16)
    for c0, c1 in _col_chunks(D_MODEL):
        o_ref[:, c0:c1] = x_ref[:, c0:c1] + jnp.dot(act_ref[...], wd_ref[:, c0:c1],
                                                    preferred_element_type=f32)


def _ffn(x_p, x_s, fnorm, wg_b, wu_b, wd_b):
    n = x_p.shape[0]
    ns = x_s.shape[0]
    tm = FFN_BLOCK
    last = n // tm - 1
    return pl.pallas_call(
        _ffn_kernel,
        grid=(n // tm + 1,),
        in_specs=[
            pl.BlockSpec((tm, D_MODEL), lambda i: (jnp.minimum(i, last), 0)),
            _const_spec((ns, D_MODEL), single=True),
            _const_spec((1, D_MODEL)),
            _const_spec((D_MODEL, D_FF), single=True),
            _const_spec((D_MODEL, D_FF), single=True),
            _const_spec((D_FF, D_MODEL), single=True),
        ],
        out_specs=[pl.BlockSpec((tm, D_MODEL), lambda i: (jnp.minimum(i, last), 0)),
                   _const_spec((ns, D_MODEL))],
        out_shape=[jax.ShapeDtypeStruct((n, D_MODEL), f32), jax.ShapeDtypeStruct((ns, D_MODEL), f32)],
        scratch_shapes=[pltpu.VMEM((tm, D_FF), bf16)],
        compiler_params=pltpu.CompilerParams(
            dimension_semantics=("arbitrary",), vmem_limit_bytes=VMEM_LIMIT),
        name="ffn",
    )(x_p, x_s, fnorm, wg_b, wu_b, wd_b)


def _sample_mixer_kernel(x_ref, ckt_ref, cvt_ref, c_ref, n_ref, m_ref, cos_ref, sin_ref, wqs_ref, w_ref,
                         wgate_ref, wout_ref, sink_ref, bd_ref, anorm_ref, qg_ref, kg_ref, gbias_ref,
                         mnorm_ref, x1_ref, nkt_ref, nvt_ref, cn_ref, nn_ref, mn_ref):
    bb, tpad, _ = x_ref.shape
    nrows = bb * tpad
    nreal = SAMPLE_TOKENS
    h = _rms(x_ref[...].reshape(nrows, D_MODEL), anorm_ref[...]).astype(bf16)
    z = jnp.concatenate(
        [jnp.dot(h, wqs_ref[...], preferred_element_type=f32),
         jnp.dot(h, w_ref[:, COL_KA:COL_G], preferred_element_type=f32),
         jnp.dot(h, wgate_ref[...], preferred_element_type=f32)], axis=1)
    lane = lax.broadcasted_iota(jnp.int32, (nrows, LANES), 1)
    low = lane < ATTN_HEAD_DIM
    cos = cos_ref[...]
    sin = sin_ref[...]

    def per_seq(a):
        return a.reshape(bb, tpad, a.shape[-1])

    def norm_rope(xs, gain):
        y = xs * lax.rsqrt(_group_sumsq(xs, bd_ref) * (1.0 / ATTN_HEAD_DIM) + NORM_EPS) * gain
        partner = jnp.where((lane & QUARTER) != 0, pltpu.roll(y, QUARTER, 1),
                            pltpu.roll(y, LANES - QUARTER, 1))
        return y * cos + partner * sin

    q_rows = []
    for j in range(ATTN_GROUP):
        qs = norm_rope(z[:, COL_QA + j * LANES:COL_QA + (j + 1) * LANES], qg_ref[...])
        qs = qs * (ATTN_HEAD_DIM ** -0.5)
        q_rows.append(per_seq(jnp.where(low, qs, 0.0)).astype(bf16))
        q_rows.append(per_seq(jnp.where(low, 0.0, qs)).astype(bf16))
    qbd = jnp.concatenate(q_rows, axis=1)
    knew = norm_rope(z[:, COL_KA:COL_KA + KV_WIDTH], kg_ref[...])
    vnew = z[:, COL_VA:COL_VA + KV_WIDTH]
    zpad = jnp.zeros((bb, LANES - tpad, LANES), bf16)
    knp = jnp.concatenate([per_seq(knew).astype(bf16), zpad], axis=1)
    vnp = jnp.concatenate([per_seq(vnew).astype(bf16), zpad], axis=1)
    ckt = ckt_ref[...]
    cvt = cvt_ref[...]
    s = jnp.concatenate(
        [jnp.einsum('bqd,bdw->bqw', qbd, ckt.astype(bf16), preferred_element_type=f32),
         jnp.einsum('bqd,bkd->bqk', qbd, knp, preferred_element_type=f32)], axis=2)
    tq = lax.broadcasted_iota(jnp.int32, s.shape, 1) & (tpad - 1)
    kj = lax.broadcasted_iota(jnp.int32, s.shape, 2)
    valid = ((kj < WINDOW) & (kj > tq)) | ((kj >= WINDOW) & (kj - WINDOW <= tq) & (kj - WINDOW < nreal))
    s = jnp.where(valid, s, NEG)
    sink = sink_ref[:, 0:1][None]
    mx = jnp.maximum(jnp.max(s, axis=-1, keepdims=True), sink)
    p = jnp.exp(s - mx)
    den = jnp.sum(p, axis=-1, keepdims=True) + jnp.exp(sink - mx)
    pb = p.astype(bf16)
    o = (jnp.einsum('bqw,bdw->bqd', pb[:, :, :WINDOW], cvt.astype(bf16), preferred_element_type=f32)
         + jnp.einsum('bqk,bkd->bqd', pb[:, :, WINDOW:], vnp, preferred_element_type=f32)) / den
    low3 = lax.broadcasted_iota(jnp.int32, (bb, tpad, LANES), 2) < ATTN_HEAD_DIM
    mix_parts = []
    for j in range(ATTN_GROUP):
        r0 = 2 * j * tpad
        pair = jnp.where(low3, o[:, r0:r0 + tpad, :], o[:, r0 + tpad:r0 + 2 * tpad, :])
        mix_parts.append(pair.reshape(nrows, LANES).astype(bf16))

    keep = lax.broadcasted_iota(jnp.int32, (KV_WIDTH, WINDOW), 1) < WINDOW - nreal
    knt = knew.T
    vnt = vnew.T
    for b in range(bb):
        shift = (WINDOW - nreal - b * tpad) % LANES
        nkt_ref[b] = jnp.where(keep, pltpu.roll(ckt_ref[b], WINDOW - nreal, 1), pltpu.roll(knt, shift, 1))
        nvt_ref[b] = jnp.where(keep, pltpu.roll(cvt_ref[b], WINDOW - nreal, 1), pltpu.roll(vnt, shift, 1))

    gz = per_seq(z[:, COL_G:COL_G + LANES] + gbias_ref[...])
    lgz = jax.nn.log_sigmoid(gz)
    trow = lax.broadcasted_iota(jnp.int32, (bb, tpad, 1), 1)
    real = trow < nreal
    mn_ref[...] = jnp.zeros_like(mn_ref)
    heads = range(MLSTM_HEADS)
    hcols = lambda base, hd: slice(base + hd * MLSTM_HEAD_DIM, base + (hd + 1) * MLSTM_HEAD_DIM)
    last = nreal - 1
    st = {}
    for hd in heads:
        m0 = m_ref[:, hd:hd + 1, :]
        ig_c = jnp.where(real, gz[:, :, hd:hd + 1], NEG)
        lf_c = jnp.where(real, lgz[:, :, FG_LANE + hd:FG_LANE + hd + 1], 0.0)
        b_c = jnp.zeros_like(lf_c)
        for sx in range(nreal):
            b_c = b_c + jnp.where(trow >= sx, lf_c[:, sx:sx + 1, :], 0.0)
        dlog = [jnp.where(trow >= sx, b_c - b_c[:, sx:sx + 1, :] + ig_c[:, sx:sx + 1, :], NEG)
                for sx in range(nreal)]
        inter = b_c + m0
        m_t = inter
        for sx in range(nreal):
            m_t = jnp.maximum(m_t, dlog[sx])
        m_new = m_t[:, last:last + 1, :]
        b_last = b_c[:, last:last + 1, :]
        st[hd] = dict(m_t=m_t, a=jnp.exp(inter - m_t), dexp=[jnp.exp(d - m_t) for d in dlog],
                      m_new=m_new, a_end=jnp.exp(b_last + m0 - m_new),
                      w_c=jnp.exp(b_last - b_c + ig_c - m_new))
    for hd in heads:
        s = st[hd]
        s['q'] = per_seq(z[:, hcols(COL_QM, hd)])
        s['k'] = per_seq(z[:, hcols(COL_KM, hd)]) * (MLSTM_HEAD_DIM ** -0.5)
        s['v'] = per_seq(z[:, hcols(COL_VM, hd)])
        s['qc'] = jnp.einsum('btd,bde->bte', s['q'].astype(bf16), c_ref[:, hd].astype(bf16),
                             preferred_element_type=f32)
    for hd in heads:
        s = st[hd]
        q, k, v = s['q'], s['k'], s['v']
        num = s['a'] * s.pop('qc')
        den_m = s['a'] * jnp.sum(q * n_ref[:, hd:hd + 1, :], axis=2, keepdims=True)
        for sx in range(nreal):
            sd = jnp.sum(q * k[:, sx:sx + 1, :], axis=2, keepdims=True) * s['dexp'][sx]
            num = num + sd * v[:, sx:sx + 1, :]
            den_m = den_m + sd
        hraw = num / jnp.maximum(jnp.abs(den_m), jnp.exp(-s['m_t']))
        hn = _rms(hraw, mnorm_ref[hd:hd + 1, :][None])
        og = per_seq(z[:, hcols(COL_OM, hd)])
        mix_parts.append((hn * jax.nn.sigmoid(og)).reshape(nrows, MLSTM_HEAD_DIM).astype(bf16))
    for hd in heads:
        s = st[hd]
        kw = s['k'] * s['w_c']
        cn_ref[:, hd] = s['a_end'] * c_ref[:, hd] + jnp.einsum(
            'bsd,bse->bde', kw.astype(bf16), s['v'].astype(bf16), preferred_element_type=f32)
        nn_ref[:, hd:hd + 1, :] = s['a_end'] * n_ref[:, hd:hd + 1, :] + jnp.sum(kw, axis=1, keepdims=True)
        mn_ref[:, hd:hd + 1, :] = jnp.broadcast_to(s['m_new'], (bb, 1, LANES))

    mix = jnp.concatenate(mix_parts, axis=1)
    x1 = x_ref[...].reshape(nrows, D_MODEL) + jnp.dot(mix, wout_ref[...], preferred_element_type=f32)
    x1_ref[...] = x1.reshape(bb, tpad, D_MODEL)


def _sample_mixer(x_pad, ckt, cvt, c0, n0, m0, cos, sin, wq_s, w, wgate, wout_s, sink_tile, bd, anorm, qg,
                  kg, gbias, mnorm):
    nb, tpad, _ = x_pad.shape
    bb = SAMPLE_BATCH_BLOCK
    nh = MLSTM_HEADS
    blk = lambda shape: pl.BlockSpec(shape, lambda i: (i,) + (0,) * (len(shape) - 1))
    cblk = (bb, nh, MLSTM_HEAD_DIM, MLSTM_HEAD_DIM)
    return pl.pallas_call(
        _sample_mixer_kernel,
        grid=(nb // bb,),
        in_specs=[blk((bb, tpad, D_MODEL)), blk((bb, KV_WIDTH, WINDOW)), blk((bb, KV_WIDTH, WINDOW)),
                  blk(cblk), blk((bb, nh, MLSTM_HEAD_DIM)), blk((bb, nh, 1)),
                  _const_spec((bb * tpad, LANES)), _const_spec((bb * tpad, LANES)),
                  _const_spec((D_MODEL, ATTN_WIDTH), single=True),
                  _const_spec((D_MODEL, IN_WIDTH), single=True),
                  _const_spec((D_MODEL, LANES), single=True),
                  _const_spec((MIX_WIDTH, D_MODEL), single=True),
                  _const_spec((ATTN_HEADS * tpad, LANES)), _const_spec((LANES, LANES)),
                  _const_spec((1, D_MODEL)),
                  _const_spec((1, LANES)), _const_spec((1, LANES)), _const_spec((1, LANES)),
                  _const_spec((nh, MLSTM_HEAD_DIM))],
        out_specs=[blk((bb, tpad, D_MODEL)), blk((bb, KV_WIDTH, WINDOW)), blk((bb, KV_WIDTH, WINDOW)),
                   blk(cblk), blk((bb, nh, MLSTM_HEAD_DIM)), blk((bb, tpad, LANES))],
        out_shape=[jax.ShapeDtypeStruct((nb, tpad, D_MODEL), f32),
                   jax.ShapeDtypeStruct((nb, KV_WIDTH, WINDOW), f32),
                   jax.ShapeDtypeStruct((nb, KV_WIDTH, WINDOW), f32),
                   jax.ShapeDtypeStruct((nb,) + cblk[1:], f32),
                   jax.ShapeDtypeStruct((nb, nh, MLSTM_HEAD_DIM), f32),
                   jax.ShapeDtypeStruct((nb, tpad, LANES), f32)],
        compiler_params=pltpu.CompilerParams(
            dimension_semantics=("arbitrary",), vmem_limit_bytes=VMEM_LIMIT),
        name="sample_mixer",
    )(x_pad, ckt, cvt, c0, n0, m0, cos, sin, wq_s, w, wgate, wout_s, sink_tile, bd, anorm, qg, kg, gbias,
      mnorm)


def _rope_angles(pos):
    half = ATTN_HEAD_DIM // 2
    inv = ROPE_THETA ** (-np.arange(half, dtype=np.float64) / half)
    ang = pos.astype(np.float64)[:, None] * inv[None, :]
    return np.cos(ang).astype(np.float32), np.sin(ang).astype(np.float32)


def _rope_tables(pos):
    c, s = _rope_angles(pos)
    cos = np.tile(c, (1, LANES // QUARTER))
    sin = np.tile(np.concatenate([-s, s], axis=1), (1, LANES // ATTN_HEAD_DIM))
    return cos, sin


def _rope_tables_quarters(pos):
    c, s = _rope_angles(pos)
    return np.tile(c, (1, LANES // QUARTER)), np.concatenate([-s, -s, s, s], axis=1)


def _quarters(a):
    lo, hi = a[..., :QUARTER], a[..., QUARTER:]
    return jnp.concatenate([lo, lo, hi, hi], axis=-1)


def _prompt_attn_weights(w):
    d = w.shape[0]
    wq = w[:, COL_QA:COL_KA].reshape(d, ATTN_WIDTH // LANES, 2, 2, QUARTER)
    wq = wq.transpose(0, 1, 3, 2, 4).reshape(d, ATTN_WIDTH)
    wk = _quarters(w[:, COL_KA:COL_VA].reshape(d, KV_HEADS, ATTN_HEAD_DIM)).reshape(d, KV_HEADS * LANES)
    wv = w[:, COL_VA:COL_QM].reshape(d, KV_HEADS, 1, ATTN_HEAD_DIM)
    wv = jnp.broadcast_to(wv, (d, KV_HEADS, 2, ATTN_HEAD_DIM)).reshape(d, KV_HEADS * LANES)
    return jnp.concatenate([wq, wk, wv], axis=1)


def kernel(x_prompt, x_sample, cache_k, cache_v, state_C, state_n, state_m, attn_norm, w_in, q_norm,
           k_norm, attn_sinks, b_ig, b_fg, mlstm_norm, w_out, ffn_norm, w_gate, w_up, w_down):
    assert w_in.shape[0] == 1 and x_prompt.shape[0] == 1
    tp = x_prompt.shape[1]
    nb, nt = x_sample.shape[0], x_sample.shape[1]
    assert nt == SAMPLE_TOKENS
    tpad = SUBLANES
    nh = MLSTM_HEADS

    w = w_in[0].astype(bf16)
    pad_a = jnp.zeros((D_MODEL, FG_LANE - nh), bf16)
    pad_b = jnp.zeros((D_MODEL, LANES - FG_LANE - nh), bf16)
    wgate = jnp.concatenate([w[:, COL_G:COL_G + nh], pad_a, w[:, COL_G + nh:], pad_b], axis=1)
    gbias = jnp.concatenate(
        [b_ig[0], jnp.zeros((FG_LANE - nh,), f32), b_fg[0], jnp.zeros((LANES - FG_LANE - nh,), f32)]
    ).reshape(1, LANES)
    wout_b = w_out[0].astype(bf16)
    wg_b = w_gate[0].astype(bf16)
    wu_b = w_up[0].astype(bf16)
    wd_b = w_down[0].astype(bf16)
    anorm = attn_norm[0].reshape(1, D_MODEL)
    fnorm = ffn_norm[0].reshape(1, D_MODEL)
    qg = jnp.tile(q_norm[0], LANES // ATTN_HEAD_DIM).reshape(1, LANES)
    kg = jnp.tile(k_norm[0], LANES // ATTN_HEAD_DIM).reshape(1, LANES)
    mnorm = mlstm_norm[0].reshape(nh, MLSTM_HEAD_DIM)
    sinks = attn_sinks[0]

    wa = _prompt_attn_weights(w)
    idx = np.arange(2 * LANES)
    same = (idx[:, None] // LANES == idx[None, :] // LANES) & (
        (idx[:, None] // QUARTER) % 2 == (idx[None, :] // QUARTER) % 2)
    bd = jnp.asarray(same, dtype=bf16)
    sink_rows_p = jnp.repeat(sinks.reshape(KV_HEADS, ATTN_GROUP), QBLOCK, axis=1)
    bias = jnp.where(jnp.arange(2 * QBLOCK)[None, None, :] == 0, sink_rows_p[:, :, None], NEG)
    qgq = _quarters(q_norm[0]).reshape(1, LANES)
    kgq = _quarters(k_norm[0]).reshape(1, LANES)
    pos_p = np.arange(tp, dtype=np.float32)
    cos_p, sin_p = _rope_tables_quarters(pos_p)
    cos_w, sin_w = _rope_tables(pos_p[tp - WINDOW:])
    x1_p, k_p, v_p, cext_p, m_p = _prompt_mixer(
        x_prompt[0], cos_p, sin_p, wa, w, wgate, wout_b, bd, bias, anorm, qgq, kgq, kg, cos_w, sin_w,
        gbias, mnorm)

    wq_s = w[:, COL_QA:COL_KA].reshape(D_MODEL, KV_HEADS, ATTN_GROUP, ATTN_HEAD_DIM)
    wq_s = wq_s.transpose(0, 2, 1, 3).reshape(D_MODEL, ATTN_WIDTH)
    wo_a = wout_b[:ATTN_WIDTH].reshape(KV_HEADS, ATTN_GROUP, ATTN_HEAD_DIM, D_MODEL)
    wo_a = wo_a.transpose(1, 0, 2, 3).reshape(ATTN_WIDTH, D_MODEL)
    wout_s = jnp.concatenate([wo_a, wout_b[ATTN_WIDTH:]], axis=0)
    sink_tile = jnp.broadcast_to(
        jnp.repeat(sinks.reshape(KV_HEADS, ATTN_GROUP).T.reshape(-1), tpad)[:, None],
        (ATTN_HEADS * tpad, LANES))
    lanes = np.arange(LANES)
    bd_s = jnp.asarray(lanes[:, None] // ATTN_HEAD_DIM == lanes[None, :] // ATTN_HEAD_DIM, dtype=bf16)
    cos_s, sin_s = _rope_tables(np.arange(tpad, dtype=np.float32) + np.float32(PAST_LEN))
    cos_s = np.tile(cos_s, (SAMPLE_BATCH_BLOCK, 1))
    sin_s = np.tile(sin_s, (SAMPLE_BATCH_BLOCK, 1))
    x_pad = jnp.pad(x_sample, ((0, 0), (0, tpad - nt), (0, 0)))
    ckt = cache_k[0].reshape(nb, WINDOW, KV_WIDTH).transpose(0, 2, 1)
    cvt = cache_v[0].reshape(nb, WINDOW, KV_WIDTH).transpose(0, 2, 1)
    x1_pad, nkt, nvt, c_new, n_new, m_pad = _sample_mixer(
        x_pad, ckt, cvt, state_C[0], state_n[0], state_m[0][:, :, None], cos_s, sin_s, wq_s, w, wgate,
        wout_s, sink_tile, bd_s, anorm, qg, kg, gbias, mnorm)
    y_p, y_s = _ffn(x1_p, x1_pad[:, :nt].reshape(nb * nt, D_MODEL), fnorm, wg_b, wu_b, wd_b)
    m_new = m_pad[:, :nh, 0]

    new_k_s = nkt.transpose(0, 2, 1)
    new_v_s = nvt.transpose(0, 2, 1)

    kv_shape = (1, 1, WINDOW, KV_HEADS, ATTN_HEAD_DIM)
    return (
        y_p[None],
        y_s.reshape(nb, nt, D_MODEL),
        k_p.reshape(kv_shape),
        v_p.reshape(kv_shape),
        cext_p[None, None, :, :, :MLSTM_HEAD_DIM],
        cext_p[None, None, :, :, MLSTM_HEAD_DIM],
        m_p[None, None, :nh, 0],
        new_k_s.reshape(1, nb, WINDOW, KV_HEADS, ATTN_HEAD_DIM),
        new_v_s.reshape(1, nb, WINDOW, KV_HEADS, ATTN_HEAD_DIM),
        c_new.reshape(1, nb, nh, MLSTM_HEAD_DIM, MLSTM_HEAD_DIM),
        n_new.reshape(1, nb, nh, MLSTM_HEAD_DIM),
        m_new.reshape(1, nb, nh),
    )
```

```python
import jax
import jax.numpy as jnp
import numpy as np
from jax import lax
from jax.experimental import pallas as pl
from jax.experimental.pallas import tpu as pltpu

D_MODEL = 1024
PAST_LEN = 16384
ATTN_HEADS = 8
KV_HEADS = 2
ATTN_HEAD_DIM = 64
ATTN_GROUP = ATTN_HEADS // KV_HEADS
ATTN_WIDTH = ATTN_HEADS * ATTN_HEAD_DIM
KV_WIDTH = KV_HEADS * ATTN_HEAD_DIM
WINDOW = 128
ROPE_THETA = 10000.0
MLSTM_HEADS = 4
MLSTM_HEAD_DIM = 128
MLSTM_WIDTH = MLSTM_HEADS * MLSTM_HEAD_DIM
MIX_WIDTH = ATTN_WIDTH + MLSTM_WIDTH
D_FF = 2816
NORM_EPS = 1e-6

LANES = 128
SUBLANES = 8
VMEM_LIMIT = 56 * 1024 * 1024

COL_QA = 0
COL_KA = COL_QA + ATTN_WIDTH
COL_VA = COL_KA + KV_WIDTH
COL_QM = COL_VA + KV_WIDTH
COL_KM = COL_QM + MLSTM_WIDTH
COL_VM = COL_KM + MLSTM_WIDTH
COL_OM = COL_VM + MLSTM_WIDTH
COL_G = COL_OM + MLSTM_WIDTH
IN_WIDTH = COL_G + 2 * MLSTM_HEADS
FG_LANE = SUBLANES

PROMPT_BLOCK = 256
QBLOCK = WINDOW
MCHUNK = 128
PROJ_CHUNK = 256
FFN_BLOCK = 512
FFN_CHUNK = 256
SAMPLE_BATCH_BLOCK = 16
SAMPLE_TOKENS = 4
NEG = -1e30

f32 = jnp.float32
bf16 = jnp.bfloat16


def _rms(x, gain):
    return x * lax.rsqrt(jnp.mean(x * x, axis=-1, keepdims=True) + NORM_EPS) * gain


def _segsum64(s, lane):
    for k in (1, 2, 4, 8, 16, 32):
        s = s + jnp.where((lane & k) != 0, pltpu.roll(s, k, 1), pltpu.roll(s, LANES - k, 1))
    return s


def _headnorm_rope(xs, gain, cos, sin_signed, lane):
    ss = _segsum64(xs * xs, lane)
    y = xs * lax.rsqrt(ss * (1.0 / ATTN_HEAD_DIM) + NORM_EPS) * gain
    partner = jnp.where((lane & 32) != 0, pltpu.roll(y, 32, 1), pltpu.roll(y, LANES - 32, 1))
    return y * cos + partner * sin_signed


def _group_sumsq(xs, bd_ref):
    x2 = xs * xs
    hi = x2.astype(bf16)
    lo = (x2 - hi.astype(f32)).astype(bf16)
    return (jnp.dot(hi, bd_ref[...], preferred_element_type=f32)
            + jnp.dot(lo, bd_ref[...], preferred_element_type=f32))


PA_Q = 0
PA_K = PA_Q + ATTN_WIDTH
PA_V = PA_K + KV_HEADS * LANES
PA_WIDTH = PA_V + KV_HEADS * LANES
QUARTER = ATTN_HEAD_DIM // 2
QA_Q = 0
QA_K = QA_Q + 2 * ATTN_WIDTH
QA_V = QA_K + KV_HEADS * LANES
QA_VZ = QA_V + KV_HEADS * LANES
QA_WIDTH = QA_VZ + KV_HEADS * LANES
ZM_WIDTH = 4 * MLSTM_WIDTH
ATTN_PHASES = 4
MLSTM_PHASES = 5


def _norm_rope_quarters(xs, ss, gain, cos, sin_signed):
    y = xs * lax.rsqrt(ss * (1.0 / ATTN_HEAD_DIM) + NORM_EPS) * gain
    return y * cos + pltpu.roll(y, LANES // 2, 1) * sin_signed


def _col_chunks(width):
    return [(c, min(c + PROJ_CHUNK, width)) for c in range(0, width, PROJ_CHUNK)]


def _mixer_proj_jobs(x_ref, anorm_ref, wa_ref, w_ref, wgate_ref, za_ref, zm_ref, gz_ref):
    h = _rms(x_ref[...], anorm_ref[...]).astype(bf16)

    def proj_job(w_src, wc0, z_ref, c0, c1):
        def run():
            z_ref[:, c0:c1] = jnp.dot(h, w_src[:, wc0 + c0:wc0 + c1], preferred_element_type=f32)
        return run

    return ([proj_job(wa_ref, 0, za_ref, c0, c1) for c0, c1 in _col_chunks(PA_WIDTH)]
            + [proj_job(w_ref, COL_QM, zm_ref, c0, c1) for c0, c1 in _col_chunks(ZM_WIDTH)]
            + [proj_job(wgate_ref, 0, gz_ref, 0, LANES)])


def _mixer_prep_jobs(za_ref, gz_ref, cos_ref, sin_ref, bd_ref, qg_ref, kg_ref, gbias_ref, qa_ref,
                     colf_ref, urow_ref, wrow_ref, mst_ref, mout_ref):
    tb = za_ref.shape[0]

    def gates_job():
        lane_t = lax.broadcasted_iota(jnp.int32, (tb, LANES), 1)
        gcol = gz_ref[...] + gbias_ref[...]
        acol = jnp.where(lane_t < FG_LANE, gcol, jax.nn.log_sigmoid(gcol))
        arow = acol.T
        lane8 = lax.broadcasted_iota(jnp.int32, (SUBLANES, LANES), 1)
        nsb = tb // LANES
        slabs = [slice(sb * LANES, (sb + 1) * LANES) for sb in range(nsb)]
        ig_all = jnp.concatenate([arow[0:SUBLANES, ls] for ls in slabs], axis=0)
        b_all = jnp.concatenate([arow[FG_LANE:FG_LANE + SUBLANES, ls] for ls in slabs], axis=0)
        lane_in = lax.broadcasted_iota(jnp.int32, b_all.shape, 1) & (MCHUNK - 1)

        def scan(x, combine, identity):
            k = 1
            while k < MCHUNK:
                terms = [jnp.where(lane_in >= j * k, pltpu.roll(x, j * k, 1), identity)
                         for j in (1, 2, 3) if j * k < MCHUNK]
                for t in terms:
                    x = combine(x, t)
                k *= 4
            return x

        b_all = scan(b_all, jnp.add, 0.0)
        u_all = ig_all - b_all
        cm_all = scan(u_all, jnp.maximum, NEG)
        m_prev = mst_ref[:, 0:1]
        stacks = []
        ends = {}
        for sb in range(nsb):
            sub = slice(sb * SUBLANES, (sb + 1) * SUBLANES)
            for c in range(LANES // MCHUNK):
                at_end = lane8 == c * MCHUNK + MCHUNK - 1
                ends[sb, c] = (jnp.max(jnp.where(at_end, cm_all[sub], NEG), axis=1, keepdims=True),
                               jnp.max(jnp.where(at_end, b_all[sub], NEG), axis=1, keepdims=True))
        for sb, ls in enumerate(slabs):
            sub = slice(sb * SUBLANES, (sb + 1) * SUBLANES)
            b8, u8, cm8 = b_all[sub], u_all[sub], cm_all[sub]
            g8 = jnp.zeros_like(u8)
            mp8 = jnp.zeros_like(u8)
            gl8 = jnp.zeros_like(u8)
            for c in range(LANES // MCHUNK):
                in_chunk = (lane8 // MCHUNK) == c
                gc = jnp.maximum(cm8, m_prev)
                cm_last, b_last = ends[sb, c]
                g_last = jnp.maximum(cm_last, m_prev)
                g8 = jnp.where(in_chunk, gc, g8)
                mp8 = jnp.where(in_chunk, m_prev, mp8)
                gl8 = jnp.where(in_chunk, g_last, gl8)
                m_prev = b_last + g_last
            a8 = jnp.exp(mp8 - g8)
            emt8 = jnp.exp(-(b8 + g8))
            aend8 = jnp.exp(mp8 - gl8)
            stacks.append(jnp.concatenate(
                [g8, a8, emt8, aend8, jnp.zeros((LANES - 4 * SUBLANES, LANES), f32)], axis=0))
            urow_ref[:, ls] = u8
            wrow_ref[:, ls] = jnp.exp(u8 - gl8)
        mst_ref[...] = jnp.broadcast_to(m_prev, mst_ref.shape)
        mout_ref[...] = jnp.broadcast_to(m_prev, mout_ref.shape)
        colf_ref[...] = jnp.concatenate(stacks, axis=1).T

    def rope_job(qb):
        def run():
            rows = slice(qb * QBLOCK, (qb + 1) * QBLOCK)
            lane = lax.broadcasted_iota(jnp.int32, (QBLOCK, LANES), 1)
            head_a = ((lane // QUARTER) & 1) == 0
            row0 = lax.broadcasted_iota(jnp.int32, (QBLOCK, LANES), 0) == 0
            cos = cos_ref[rows, :]
            sin = sin_ref[rows, :]
            ss = [_group_sumsq(za_ref[rows, d * 2 * LANES:(d + 1) * 2 * LANES], bd_ref)
                  for d in range(PA_V // (2 * LANES))]
            for j in range(PA_V // LANES):
                is_q = j < ATTN_WIDTH // LANES
                y = _norm_rope_quarters(za_ref[rows, j * LANES:(j + 1) * LANES],
                                        ss[j // 2][:, (j % 2) * LANES:(j % 2 + 1) * LANES],
                                        qg_ref[...] if is_q else kg_ref[...], cos, sin)
                if is_q:
                    y = y * (ATTN_HEAD_DIM ** -0.5)
                    qa_ref[rows, QA_Q + 2 * j * LANES:QA_Q + (2 * j + 1) * LANES] = (
                        jnp.where(head_a, y, 0.0).astype(bf16))
                    qa_ref[rows, QA_Q + (2 * j + 1) * LANES:QA_Q + (2 * j + 2) * LANES] = (
                        jnp.where(head_a, 0.0, y).astype(bf16))
                else:
                    c = j - ATTN_WIDTH // LANES
                    qa_ref[rows, QA_K + c * LANES:QA_K + (c + 1) * LANES] = y.astype(bf16)
            for c in range(KV_HEADS):
                v = za_ref[rows, PA_V + c * LANES:PA_V + (c + 1) * LANES]
                qa_ref[rows, QA_V + c * LANES:QA_V + (c + 1) * LANES] = v.astype(bf16)
                qa_ref[rows, QA_VZ + c * LANES:QA_VZ + (c + 1) * LANES] = (
                    jnp.where(row0, 0.0, v).astype(bf16))
        return run

    return [gates_job] + [rope_job(qb) for qb in range(tb // QBLOCK)]


def _mixer_outproj_jobs(xs_ref, mix_ref, wout_ref, x1_ref):
    def job(c0, c1):
        def run():
            x1_ref[:, c0:c1] = xs_ref[:, c0:c1] + jnp.dot(mix_ref[...], wout_ref[:, c0:c1],
                                                          preferred_element_type=f32)
        return run

    return [job(c0, c1) for c0, c1 in _col_chunks(D_MODEL)]


def _mixer_window_out(xs_ref, anorm_ref, w_ref, kgs_ref, coss_ref, sins_ref, kout_ref, vout_ref):
    tb = xs_ref.shape[0]
    h = _rms(xs_ref[tb - WINDOW:, :], anorm_ref[...]).astype(bf16)
    zs = jnp.dot(h, w_ref[:, COL_KA:COL_QM], preferred_element_type=f32)
    lane_s = lax.broadcasted_iota(jnp.int32, (WINDOW, LANES), 1)
    kout_ref[...] = _headnorm_rope(zs[:, :KV_WIDTH], kgs_ref[...], coss_ref[...], sins_ref[...], lane_s)
    vout_ref[...] = zs[:, KV_WIDTH:]


def _mixer_core(first_block, last_block, fillers, prep_jobs, qa_ref, zm_ref, colf_ref, urow_ref, wrow_ref,
                mix_ref, bias_ref, mnorm_ref, kprev_ref, vprev_ref, cst_ref, cext_ref):
    tb = qa_ref.shape[0]
    nqb = tb // QBLOCK
    nch = tb // MCHUNK
    fillers = list(fillers)
    n_fill = len(fillers)
    slots = len(prep_jobs) + ATTN_PHASES + nch * MLSTM_PHASES
    progress = [0]

    def fill():
        progress[0] += 1
        while n_fill - len(fillers) < min(n_fill, -(-n_fill * progress[0] // slots)):
            fillers.pop(0)()

    low_half = lax.broadcasted_iota(jnp.int32, (QBLOCK, LANES), 1) < ATTN_HEAD_DIM
    qi = lax.broadcasted_iota(jnp.int32, (ATTN_GROUP * QBLOCK, 2 * QBLOCK), 0) & (QBLOCK - 1)
    kj = lax.broadcasted_iota(jnp.int32, (ATTN_GROUP * QBLOCK, 2 * QBLOCK), 1)
    band = (kj > qi) & (kj <= qi + QBLOCK)
    ones_slab = jnp.ones((2 * QBLOCK, LANES), bf16)

    def attn_phases(chains):
        rows_of = lambda qb: slice(qb * QBLOCK, (qb + 1) * QBLOCK)
        st = {}

        def scores():
            for qb, c in chains:
                rows = rows_of(qb)
                kcols = slice(QA_K + c * LANES, QA_K + (c + 1) * LANES)
                if qb == 0:
                    kprev, vprev = kprev_ref[c], vprev_ref[c]
                else:
                    kprev = qa_ref[rows_of(qb - 1), kcols]
                    vprev = qa_ref[rows_of(qb - 1), QA_VZ + c * LANES:QA_VZ + (c + 1) * LANES]
                kcat = jnp.concatenate([kprev, qa_ref[rows, kcols]], axis=0)
                vcat = jnp.concatenate(
                    [vprev, qa_ref[rows, QA_V + c * LANES:QA_V + (c + 1) * LANES]], axis=0)
                st['vext', qb, c] = jnp.concatenate([vcat, ones_slab], axis=1)
                q0 = QA_Q + c * ATTN_GROUP * LANES
                qst = jnp.concatenate([qa_ref[rows, q0 + g * LANES:q0 + (g + 1) * LANES]
                                       for g in range(ATTN_GROUP)], axis=0)
                st['s', qb, c] = lax.dot_general(qst, kcat, (((1,), (1,)), ((), ())),
                                                 preferred_element_type=f32)

        def softmax():
            for qb, c in chains:
                valid = band & (kj >= QBLOCK) if (first_block and qb == 0) else band
                s = jnp.where(valid, st.pop(('s', qb, c)), bias_ref[c])
                st['p', qb, c] = jnp.exp(s - jnp.max(s, axis=-1, keepdims=True)).astype(bf16)

        def values():
            for qb, c in chains:
                st['of', qb, c] = jnp.dot(st.pop(('p', qb, c)), st.pop(('vext', qb, c)),
                                          preferred_element_type=f32)

        def normalise():
            for qb, c in chains:
                of = st.pop(('of', qb, c))
                o = of[:, :LANES] / of[:, LANES:]
                for jj in range(2):
                    pair = jnp.where(low_half, o[(2 * jj) * QBLOCK:(2 * jj + 1) * QBLOCK],
                                     o[(2 * jj + 1) * QBLOCK:(2 * jj + 2) * QBLOCK])
                    col = (2 * c + jj) * LANES
                    mix_ref[rows_of(qb), col:col + LANES] = pair.astype(bf16)

        return [scores, softmax, values, normalise]

    ti = lax.broadcasted_iota(jnp.int32, (MCHUNK, MCHUNK), 0)
    si = lax.broadcasted_iota(jnp.int32, (MCHUNK, MCHUNK), 1)
    causal = si <= ti
    ones_l = jnp.ones((MCHUNK, LANES), bf16)

    cexts = [cst_ref[hd] for hd in range(MLSTM_HEADS)]

    def mlstm_phases(c):
        rows = slice(c * MCHUNK, (c + 1) * MCHUNK)
        heads = range(MLSTM_HEADS)
        hcols = lambda k, hd: slice((k * MLSTM_HEADS + hd) * MLSTM_HEAD_DIM,
                                    (k * MLSTM_HEADS + hd + 1) * MLSTM_HEAD_DIM)
        st = {}

        def scores():
            for hd in heads:
                st['q', hd] = zm_ref[rows, hcols(0, hd)].astype(bf16)
                kf = zm_ref[rows, hcols(1, hd)] * (MLSTM_HEAD_DIM ** -0.5)
                st['kt', hd] = kf.T
                st['v', hd] = jnp.concatenate([zm_ref[rows, hcols(2, hd)].astype(bf16), ones_l], axis=1)
                st['s', hd] = lax.dot_general(st['q', hd], kf.astype(bf16),
                                              (((1,), (1,)), ((), ())), preferred_element_type=f32)

        def decay():
            for hd in heads:
                g_c = colf_ref[rows, hd:hd + 1]
                u_r = urow_ref[hd:hd + 1, rows]
                dmat = jnp.exp(jnp.where(causal, u_r - g_c, NEG))
                st['s', hd] = (st['s', hd] * dmat).astype(bf16)
                st['kt', hd] = (st['kt', hd] * wrow_ref[hd:hd + 1, rows]).astype(bf16)

        def readout():
            for hd in heads:
                a_c = colf_ref[rows, SUBLANES + hd:SUBLANES + hd + 1]
                st['nd', hd] = (
                    a_c * jnp.dot(st.pop(('q', hd)), cexts[hd].astype(bf16), preferred_element_type=f32)
                    + jnp.dot(st.pop(('s', hd)), st['v', hd], preferred_element_type=f32))

        def emit():
            for hd in heads:
                nd = st.pop(('nd', hd))
                emt_c = colf_ref[rows, 2 * SUBLANES + hd:2 * SUBLANES + hd + 1]
                hraw = nd[:, :MLSTM_HEAD_DIM] / jnp.maximum(jnp.abs(nd[:, MLSTM_HEAD_DIM:]), emt_c)
                hn = _rms(hraw, mnorm_ref[hd:hd + 1, :])
                og = zm_ref[rows, hcols(3, hd)]
                mix_ref[rows, ATTN_WIDTH + hd * MLSTM_HEAD_DIM:ATTN_WIDTH + (hd + 1) * MLSTM_HEAD_DIM] = (
                    (hn * jax.nn.sigmoid(og)).astype(bf16))

        def update():
            for hd in heads:
                aend = colf_ref[c * MCHUNK:c * MCHUNK + 1, 3 * SUBLANES + hd:3 * SUBLANES + hd + 1]
                cexts[hd] = aend * cexts[hd] + jnp.dot(st.pop(('kt', hd)), st.pop(('v', hd)),
                                                       preferred_element_type=f32)

        return [scores, decay, readout, emit, update]

    gates_job, rope_jobs = prep_jobs[0], prep_jobs[1:]
    attn = attn_phases([(qb, c) for qb in range(nqb) for c in range(KV_HEADS)])
    chunks = [mlstm_phases(c) for c in range(nch)]
    mlstm = [p for scores, decay, _, _, _ in chunks for p in (scores, decay)]
    for c, (_, _, readout, emit, update) in enumerate(chunks):
        mlstm += [readout] + ([chunks[c - 1][3]] if c else []) + [update]
    mlstm.append(chunks[-1][3])
    assert len(attn) == ATTN_PHASES and len(mlstm) == nch * MLSTM_PHASES
    order = [gates_job] + rope_jobs + attn[:1]
    rest = attn[1:]
    for i in range(max(len(rest), len(mlstm))):
        order += rest[i:i + 1] + mlstm[i:i + 1]
    assert len(order) == slots
    for job in order:
        job()
        fill()
    last_rows = slice(tb - QBLOCK, tb)
    for c in range(KV_HEADS):
        kprev_ref[c] = qa_ref[last_rows, QA_K + c * LANES:QA_K + (c + 1) * LANES]
        vprev_ref[c] = qa_ref[last_rows, QA_VZ + c * LANES:QA_VZ + (c + 1) * LANES]
    for hd in range(MLSTM_HEADS):
        cst_ref[hd] = cexts[hd]
        if last_block:
            cext_ref[hd] = cexts[hd]


def _prompt_mixer_kernel(x_ref, cos_ref, sin_ref, wa_ref, w_ref, wgate_ref, wout_ref, bd_ref, bias_ref,
                         anorm_ref, qg_ref, kg_ref, kgs_ref, coss_ref, sins_ref, gbias_ref, mnorm_ref,
                         x1_ref, kout_ref, vout_ref, cext_ref, mout_ref,
                         za0, za1, zm0, zm1, gz0, gz1, mix0, mix1, xs0, xs1,
                         qa_ref, colf_ref, urow_ref, wrow_ref, kprev_ref, vprev_ref, cst_ref, mst_ref):
    step = pl.program_id(0)
    nblk = pl.num_programs(0) - 2
    za, zm, gz, mix, xs = (za0, za1), (zm0, zm1), (gz0, gz1), (mix0, mix1), (xs0, xs1)

    def run(parity, do_in, do_core, do_out, first_block=False, last_block=False):
        other = 1 - parity
        jobs = []
        if do_out:
            jobs += _mixer_outproj_jobs(xs[parity], mix[parity], wout_ref, x1_ref)
        if do_in:
            jobs += _mixer_proj_jobs(x_ref, anorm_ref, wa_ref, w_ref, wgate_ref, za[parity], zm[parity],
                                     gz[parity])
        if do_core:
            prep = _mixer_prep_jobs(za[other], gz[other], cos_ref, sin_ref, bd_ref, qg_ref, kg_ref,
                                    gbias_ref, qa_ref, colf_ref, urow_ref, wrow_ref, mst_ref, mout_ref)
            _mixer_core(first_block, last_block, jobs, prep, qa_ref, zm[other], colf_ref, urow_ref,
                        wrow_ref, mix[other], bias_ref, mnorm_ref, kprev_ref, vprev_ref, cst_ref, cext_ref)
        else:
            for job in jobs:
                job()
        if last_block:
            _mixer_window_out(xs[other], anorm_ref, w_ref, kgs_ref, coss_ref, sins_ref, kout_ref, vout_ref)
        if do_in:
            xs[parity][...] = x_ref[...]

    @pl.when(step == 0)
    def _first():
        kprev_ref[...] = jnp.zeros_like(kprev_ref)
        vprev_ref[...] = jnp.zeros_like(vprev_ref)
        cst_ref[...] = jnp.zeros_like(cst_ref)
        mst_ref[...] = jnp.zeros_like(mst_ref)
        run(0, True, False, False)

    @pl.when(step == 1)
    def _second():
        run(1, True, True, False, first_block=True)

    steady = (step >= 2) & (step < nblk)

    @pl.when(steady & (step % 2 == 0))
    def _even():
        run(0, True, True, True)

    @pl.when(steady & (step % 2 == 1))
    def _odd():
        run(1, True, True, True)

    @pl.when(step == nblk)
    def _drain_core():
        run(0, False, True, True, last_block=True)

    @pl.when(step == nblk + 1)
    def _drain_out():
        run(1, False, False, True)


def _const_spec(shape, single=False):
    nd = len(shape)
    if single:
        return pl.BlockSpec(shape, lambda i, *_: (0,) * nd, pipeline_mode=pl.Buffered(1))
    return pl.BlockSpec(shape, lambda i, *_: (0,) * nd)


def _prompt_mixer(x, cos, sin, wa, w, wgate, wout_b, bd, bias, anorm, qg, kg, kgs, coss, sins, gbias, mnorm):
    t = x.shape[0]
    tb = PROMPT_BLOCK
    nblk = t // tb
    assert nblk % 2 == 0 and nblk >= 4
    state_shape = (MLSTM_HEADS, MLSTM_HEAD_DIM, 2 * MLSTM_HEAD_DIM)
    last = nblk - 1
    lag = lambda d: (lambda i: (jnp.clip(i - d, 0, last), 0))
    return pl.pallas_call(
        _prompt_mixer_kernel,
        grid=(nblk + 2,),
        in_specs=[
            pl.BlockSpec((tb, D_MODEL), lag(0)),
            pl.BlockSpec((tb, LANES), lag(1)),
            pl.BlockSpec((tb, LANES), lag(1)),
            _const_spec((D_MODEL, PA_WIDTH), single=True),
            _const_spec((D_MODEL, IN_WIDTH), single=True),
            _const_spec((D_MODEL, LANES), single=True),
            _const_spec((MIX_WIDTH, D_MODEL), single=True),
            _const_spec((2 * LANES, 2 * LANES), single=True),
            _const_spec((KV_HEADS, ATTN_GROUP * QBLOCK, 2 * QBLOCK), single=True),
            _const_spec((1, D_MODEL)),
            _const_spec((1, LANES)),
            _const_spec((1, LANES)),
            _const_spec((1, LANES)),
            _const_spec((WINDOW, LANES)),
            _const_spec((WINDOW, LANES)),
            _const_spec((1, LANES)),
            _const_spec((MLSTM_HEADS, MLSTM_HEAD_DIM)),
        ],
        out_specs=[
            pl.BlockSpec((tb, D_MODEL), lag(2)),
            _const_spec((WINDOW, KV_WIDTH)),
            _const_spec((WINDOW, KV_WIDTH)),
            _const_spec(state_shape),
            _const_spec((SUBLANES, LANES)),
        ],
        out_shape=[
            jax.ShapeDtypeStruct((t, D_MODEL), f32),
            jax.ShapeDtypeStruct((WINDOW, KV_WIDTH), f32),
            jax.ShapeDtypeStruct((WINDOW, KV_WIDTH), f32),
            jax.ShapeDtypeStruct(state_shape, f32),
            jax.ShapeDtypeStruct((SUBLANES, LANES), f32),
        ],
        scratch_shapes=(
            [pltpu.VMEM((tb, PA_WIDTH), f32)] * 2 + [pltpu.VMEM((tb, ZM_WIDTH), f32)] * 2
            + [pltpu.VMEM((tb, LANES), f32)] * 2
            + [pltpu.VMEM((tb, MIX_WIDTH), bf16)] * 2 + [pltpu.VMEM((tb, D_MODEL), f32)] * 2
            + [pltpu.VMEM((tb, QA_WIDTH), bf16), pltpu.VMEM((tb, LANES), f32)]
            + [pltpu.VMEM((SUBLANES, tb), f32)] * 2
            + [pltpu.VMEM((KV_HEADS, WINDOW, LANES), bf16)] * 2
            + [pltpu.VMEM(state_shape, f32), pltpu.VMEM((SUBLANES, LANES), f32)]),
        compiler_params=pltpu.CompilerParams(
            dimension_semantics=("arbitrary",), vmem_limit_bytes=VMEM_LIMIT),
        name="prompt_mixer",
    )(x, cos, sin, wa, w, wgate, wout_b, bd, bias, anorm, qg, kg, kgs, coss, sins, gbias, mnorm)


def _ffn_kernel(xp_ref, xs_ref, g_ref, wg_ref, wu_ref, wd_ref, op_ref, os_ref, act_ref):
    step = pl.program_id(0)
    last = pl.num_programs(0) - 1

    @pl.when(step < last)
    def _prompt_rows():
        _ffn_rows(xp_ref, g_ref, wg_ref, wu_ref, wd_ref, op_ref, act_ref)

    @pl.when(step == last)
    def _sample_rows():
        _ffn_rows(xs_ref, g_ref, wg_ref, wu_ref, wd_ref, os_ref, act_ref)


def _ffn_rows(x_ref, g_ref, wg_ref, wu_ref, wd_ref, o_ref, act_ref):
    hf = _rms(x_ref[...], g_ref[...]).astype(bf16)
    for c0, c1 in _col_chunks(D_FF):
        gate = jnp.dot(hf, wg_ref[:, c0:c1], preferred_element_type=f32)
        up = jnp.dot(hf, wu_ref[:, c0:c1], preferred_element_type=f32)
        act_ref[:, c0:c1] = (gate * jax.nn.sigmoid(gate) * up).astype(bf16)
    for c0, c1 in _col_chunks(D_MODEL):
        o_ref[:, c0:c1] = x_ref[:, c0:c1] + jnp.dot(act_ref[...], wd_ref[:, c0:c1],
                                                    preferred_element_type=f32)


def _ffn(x_p, x_s, fnorm, wg_b, wu_b, wd_b):
    n = x_p.shape[0]
    ns = x_s.shape[0]
    tm = FFN_BLOCK
    last = n // tm - 1
    return pl.pallas_call(
        _ffn_kernel,
        grid=(n // tm + 1,),
        in_specs=[
            pl.BlockSpec((tm, D_MODEL), lambda i: (jnp.minimum(i, last), 0)),
            _const_spec((ns, D_MODEL), single=True),
            _const_spec((1, D_MODEL)),
            _const_spec((D_MODEL, D_FF), single=True),
            _const_spec((D_MODEL, D_FF), single=True),
            _const_spec((D_FF, D_MODEL), single=True),
        ],
        out_specs=[pl.BlockSpec((tm, D_MODEL), lambda i: (jnp.minimum(i, last), 0)),
                   _const_spec((ns, D_MODEL))],
        out_shape=[jax.ShapeDtypeStruct((n, D_MODEL), f32), jax.ShapeDtypeStruct((ns, D_MODEL), f32)],
        scratch_shapes=[pltpu.VMEM((tm, D_FF), bf16)],
        compiler_params=pltpu.CompilerParams(
            dimension_semantics=("arbitrary",), vmem_limit_bytes=VMEM_LIMIT),
        name="ffn",
    )(x_p, x_s, fnorm, wg_b, wu_b, wd_b)


def _sample_mixer_kernel(x_ref, ckt_ref, cvt_ref, c_ref, n_ref, m_ref, cos_ref, sin_ref, wqs_ref, w_ref,
                         wgate_ref, wout_ref, sink_ref, bd_ref, anorm_ref, qg_ref, kg_ref, gbias_ref,
                         mnorm_ref, x1_ref, nkt_ref, nvt_ref, cn_ref, nn_ref, mn_ref):
    bb, tpad, _ = x_ref.shape
    nrows = bb * tpad
    nreal = SAMPLE_TOKENS
    h = _rms(x_ref[...].reshape(nrows, D_MODEL), anorm_ref[...]).astype(bf16)
    z = jnp.concatenate(
        [jnp.dot(h, wqs_ref[...], preferred_element_type=f32),
         jnp.dot(h, w_ref[:, COL_KA:COL_G], preferred_element_type=f32),
         jnp.dot(h, wgate_ref[...], preferred_element_type=f32)], axis=1)
    lane = lax.broadcasted_iota(jnp.int32, (nrows, LANES), 1)
    low = lane < ATTN_HEAD_DIM
    cos = cos_ref[...]
    sin = sin_ref[...]

    def per_seq(a):
        return a.reshape(bb, tpad, a.shape[-1])

    def norm_rope(xs, gain):
        y = xs * lax.rsqrt(_group_sumsq(xs, bd_ref) * (1.0 / ATTN_HEAD_DIM) + NORM_EPS) * gain
        partner = jnp.where((lane & QUARTER) != 0, pltpu.roll(y, QUARTER, 1),
                            pltpu.roll(y, LANES - QUARTER, 1))
        return y * cos + partner * sin

    q_rows = []
    for j in range(ATTN_GROUP):
        qs = norm_rope(z[:, COL_QA + j * LANES:COL_QA + (j + 1) * LANES], qg_ref[...])
        qs = qs * (ATTN_HEAD_DIM ** -0.5)
        q_rows.append(per_seq(jnp.where(low, qs, 0.0)).astype(bf16))
        q_rows.append(per_seq(jnp.where(low, 0.0, qs)).astype(bf16))
    qbd = jnp.concatenate(q_rows, axis=1)
    knew = norm_rope(z[:, COL_KA:COL_KA + KV_WIDTH], kg_ref[...])
    vnew = z[:, COL_VA:COL_VA + KV_WIDTH]
    zpad = jnp.zeros((bb, LANES - tpad, LANES), bf16)
    knp = jnp.concatenate([per_seq(knew).astype(bf16), zpad], axis=1)
    vnp = jnp.concatenate([per_seq(vnew).astype(bf16), zpad], axis=1)
    ckt = ckt_ref[...]
    cvt = cvt_ref[...]
    s = jnp.concatenate(
        [jnp.einsum('bqd,bdw->bqw', qbd, ckt.astype(bf16), preferred_element_type=f32),
         jnp.einsum('bqd,bkd->bqk', qbd, knp, preferred_element_type=f32)], axis=2)
    tq = lax.broadcasted_iota(jnp.int32, s.shape, 1) & (tpad - 1)
    kj = lax.broadcasted_iota(jnp.int32, s.shape, 2)
    valid = ((kj < WINDOW) & (kj > tq)) | ((kj >= WINDOW) & (kj - WINDOW <= tq) & (kj - WINDOW < nreal))
    s = jnp.where(valid, s, NEG)
    sink = sink_ref[:, 0:1][None]
    mx = jnp.maximum(jnp.max(s, axis=-1, keepdims=True), sink)
    p = jnp.exp(s - mx)
    den = jnp.sum(p, axis=-1, keepdims=True) + jnp.exp(sink - mx)
    pb = p.astype(bf16)
    o = (jnp.einsum('bqw,bdw->bqd', pb[:, :, :WINDOW], cvt.astype(bf16), preferred_element_type=f32)
         + jnp.einsum('bqk,bkd->bqd', pb[:, :, WINDOW:], vnp, preferred_element_type=f32)) / den
    low3 = lax.broadcasted_iota(jnp.int32, (bb, tpad, LANES), 2) < ATTN_HEAD_DIM
    mix_parts = []
    for j in range(ATTN_GROUP):
        r0 = 2 * j * tpad
        pair = jnp.where(low3, o[:, r0:r0 + tpad, :], o[:, r0 + tpad:r0 + 2 * tpad, :])
        mix_parts.append(pair.reshape(nrows, LANES).astype(bf16))

    keep = lax.broadcasted_iota(jnp.int32, (KV_WIDTH, WINDOW), 1) < WINDOW - nreal
    knt = knew.T
    vnt = vnew.T
    for b in range(bb):
        shift = (WINDOW - nreal - b * tpad) % LANES
        nkt_ref[b] = jnp.where(keep, pltpu.roll(ckt_ref[b], WINDOW - nreal, 1), pltpu.roll(knt, shift, 1))
        nvt_ref[b] = jnp.where(keep, pltpu.roll(cvt_ref[b], WINDOW - nreal, 1), pltpu.roll(vnt, shift, 1))

    gz = per_seq(z[:, COL_G:COL_G + LANES] + gbias_ref[...])
    lgz = jax.nn.log_sigmoid(gz)
    trow = lax.broadcasted_iota(jnp.int32, (bb, tpad, 1), 1)
    real = trow < nreal
    mn_ref[...] = jnp.zeros_like(mn_ref)
    heads = range(MLSTM_HEADS)
    hcols = lambda base, hd: slice(base + hd * MLSTM_HEAD_DIM, base + (hd + 1) * MLSTM_HEAD_DIM)
    last = nreal - 1
    st = {}
    for hd in heads:
        m0 = m_ref[:, hd:hd + 1, :]
        ig_c = jnp.where(real, gz[:, :, hd:hd + 1], NEG)
        lf_c = jnp.where(real, lgz[:, :, FG_LANE + hd:FG_LANE + hd + 1], 0.0)
        b_c = jnp.zeros_like(lf_c)
        for sx in range(nreal):
            b_c = b_c + jnp.where(trow >= sx, lf_c[:, sx:sx + 1, :], 0.0)
        dlog = [jnp.where(trow >= sx, b_c - b_c[:, sx:sx + 1, :] + ig_c[:, sx:sx + 1, :], NEG)
                for sx in range(nreal)]
        inter = b_c + m0
        m_t = inter
        for sx in range(nreal):
            m_t = jnp.maximum(m_t, dlog[sx])
        m_new = m_t[:, last:last + 1, :]
        b_last = b_c[:, last:last + 1, :]
        st[hd] = dict(m_t=m_t, a=jnp.exp(inter - m_t), dexp=[jnp.exp(d - m_t) for d in dlog],
                      m_new=m_new, a_end=jnp.exp(b_last + m0 - m_new),
                      w_c=jnp.exp(b_last - b_c + ig_c - m_new))
    for hd in heads:
        s = st[hd]
        s['q'] = per_seq(z[:, hcols(COL_QM, hd)])
        s['k'] = per_seq(z[:, hcols(COL_KM, hd)]) * (MLSTM_HEAD_DIM ** -0.5)
        s['v'] = per_seq(z[:, hcols(COL_VM, hd)])
        s['qc'] = jnp.einsum('btd,bde->bte', s['q'].astype(bf16), c_ref[:, hd].astype(bf16),
                             preferred_element_type=f32)
    for hd in heads:
        s = st[hd]
        q, k, v = s['q'], s['k'], s['v']
        num = s['a'] * s.pop('qc')
        den_m = s['a'] * jnp.sum(q * n_ref[:, hd:hd + 1, :], axis=2, keepdims=True)
        for sx in range(nreal):
            sd = jnp.sum(q * k[:, sx:sx + 1, :], axis=2, keepdims=True) * s['dexp'][sx]
            num = num + sd * v[:, sx:sx + 1, :]
            den_m = den_m + sd
        hraw = num / jnp.maximum(jnp.abs(den_m), jnp.exp(-s['m_t']))
        hn = _rms(hraw, mnorm_ref[hd:hd + 1, :][None])
        og = per_seq(z[:, hcols(COL_OM, hd)])
        mix_parts.append((hn * jax.nn.sigmoid(og)).reshape(nrows, MLSTM_HEAD_DIM).astype(bf16))
    for hd in heads:
        s = st[hd]
        kw = s['k'] * s['w_c']
        cn_ref[:, hd] = s['a_end'] * c_ref[:, hd] + jnp.einsum(
            'bsd,bse->bde', kw.astype(bf16), s['v'].astype(bf16), preferred_element_type=f32)
        nn_ref[:, hd:hd + 1, :] = s['a_end'] * n_ref[:, hd:hd + 1, :] + jnp.sum(kw, axis=1, keepdims=True)
        mn_ref[:, hd:hd + 1, :] = jnp.broadcast_to(s['m_new'], (bb, 1, LANES))

    mix = jnp.concatenate(mix_parts, axis=1)
    x1 = x_ref[...].reshape(nrows, D_MODEL) + jnp.dot(mix, wout_ref[...], preferred_element_type=f32)
    x1_ref[...] = x1.reshape(bb, tpad, D_MODEL)


def _sample_mixer(x_pad, ckt, cvt, c0, n0, m0, cos, sin, wq_s, w, wgate, wout_s, sink_tile, bd, anorm, qg,
                  kg, gbias, mnorm):
    nb, tpad, _ = x_pad.shape
    bb = SAMPLE_BATCH_BLOCK
    nh = MLSTM_HEADS
    blk = lambda shape: pl.BlockSpec(shape, lambda i: (i,) + (0,) * (len(shape) - 1))
    cblk = (bb, nh, MLSTM_HEAD_DIM, MLSTM_HEAD_DIM)
    return pl.pallas_call(
        _sample_mixer_kernel,
        grid=(nb // bb,),
        in_specs=[blk((bb, tpad, D_MODEL)), blk((bb, KV_WIDTH, WINDOW)), blk((bb, KV_WIDTH, WINDOW)),
                  blk(cblk), blk((bb, nh, MLSTM_HEAD_DIM)), blk((bb, nh, 1)),
                  _const_spec((bb * tpad, LANES)), _const_spec((bb * tpad, LANES)),
                  _const_spec((D_MODEL, ATTN_WIDTH), single=True),
                  _const_spec((D_MODEL, IN_WIDTH), single=True),
                  _const_spec((D_MODEL, LANES), single=True),
                  _const_spec((MIX_WIDTH, D_MODEL), single=True),
                  _const_spec((ATTN_HEADS * tpad, LANES)), _const_spec((LANES, LANES)),
                  _const_spec((1, D_MODEL)),
                  _const_spec((1, LANES)), _const_spec((1, LANES)), _const_spec((1, LANES)),
                  _const_spec((nh, MLSTM_HEAD_DIM))],
        out_specs=[blk((bb, tpad, D_MODEL)), blk((bb, KV_WIDTH, WINDOW)), blk((bb, KV_WIDTH, WINDOW)),
                   blk(cblk), blk((bb, nh, MLSTM_HEAD_DIM)), blk((bb, tpad, LANES))],
        out_shape=[jax.ShapeDtypeStruct((nb, tpad, D_MODEL), f32),
                   jax.ShapeDtypeStruct((nb, KV_WIDTH, WINDOW), f32),
                   jax.ShapeDtypeStruct((nb, KV_WIDTH, WINDOW), f32),
                   jax.ShapeDtypeStruct((nb,) + cblk[1:], f32),
                   jax.ShapeDtypeStruct((nb, nh, MLSTM_HEAD_DIM), f32),
                   jax.ShapeDtypeStruct((nb, tpad, LANES), f32)],
        compiler_params=pltpu.CompilerParams(
            dimension_semantics=("arbitrary",), vmem_limit_bytes=VMEM_LIMIT),
        name="sample_mixer",
    )(x_pad, ckt, cvt, c0, n0, m0, cos, sin, wq_s, w, wgate, wout_s, sink_tile, bd, anorm, qg, kg, gbias,
      mnorm)


def _rope_angles(pos):
    half = ATTN_HEAD_DIM // 2
    inv = ROPE_THETA ** (-np.arange(half, dtype=np.float64) / half)
    ang = pos.astype(np.float64)[:, None] * inv[None, :]
    return np.cos(ang).astype(np.float32), np.sin(ang).astype(np.float32)


def _rope_tables(pos):
    c, s = _rope_angles(pos)
    cos = np.tile(c, (1, LANES // QUARTER))
    sin = np.tile(np.concatenate([-s, s], axis=1), (1, LANES // ATTN_HEAD_DIM))
    return cos, sin


def _rope_tables_quarters(pos):
    c, s = _rope_angles(pos)
    return np.tile(c, (1, LANES // QUARTER)), np.concatenate([-s, -s, s, s], axis=1)


def _quarters(a):
    lo, hi = a[..., :QUARTER], a[..., QUARTER:]
    return jnp.concatenate([lo, lo, hi, hi], axis=-1)


def _prompt_attn_weights(w):
    d = w.shape[0]
    wq = w[:, COL_QA:COL_KA].reshape(d, ATTN_WIDTH // LANES, 2, 2, QUARTER)
    wq = wq.transpose(0, 1, 3, 2, 4).reshape(d, ATTN_WIDTH)
    wk = _quarters(w[:, COL_KA:COL_VA].reshape(d, KV_HEADS, ATTN_HEAD_DIM)).reshape(d, KV_HEADS * LANES)
    wv = w[:, COL_VA:COL_QM].reshape(d, KV_HEADS, 1, ATTN_HEAD_DIM)
    wv = jnp.broadcast_to(wv, (d, KV_HEADS, 2, ATTN_HEAD_DIM)).reshape(d, KV_HEADS * LANES)
    return jnp.concatenate([wq, wk, wv], axis=1)


def kernel(x_prompt, x_sample, cache_k, cache_v, state_C, state_n, state_m, attn_norm, w_in, q_norm,
           k_norm, attn_sinks, b_ig, b_fg, mlstm_norm, w_out, ffn_norm, w_gate, w_up, w_down):
    assert w_in.shape[0] == 1 and x_prompt.shape[0] == 1
    tp = x_prompt.shape[1]
    nb, nt = x_sample.shape[0], x_sample.shape[1]
    assert nt == SAMPLE_TOKENS
    tpad = SUBLANES
    nh = MLSTM_HEADS

    w = w_in[0].astype(bf16)
    pad_a = jnp.zeros((D_MODEL, FG_LANE - nh), bf16)
    pad_b = jnp.zeros((D_MODEL, LANES - FG_LANE - nh), bf16)
    wgate = jnp.concatenate([w[:, COL_G:COL_G + nh], pad_a, w[:, COL_G + nh:], pad_b], axis=1)
    gbias = jnp.concatenate(
        [b_ig[0], jnp.zeros((FG_LANE - nh,), f32), b_fg[0], jnp.zeros((LANES - FG_LANE - nh,), f32)]
    ).reshape(1, LANES)
    wout_b = w_out[0].astype(bf16)
    wg_b = w_gate[0]
    wu_b = w_up[0]
    wd_b = w_down[0]
    anorm = attn_norm[0].reshape(1, D_MODEL)
    fnorm = ffn_norm[0].reshape(1, D_MODEL)
    qg = jnp.tile(q_norm[0], LANES // ATTN_HEAD_DIM).reshape(1, LANES)
    kg = jnp.tile(k_norm[0], LANES // ATTN_HEAD_DIM).reshape(1, LANES)
    mnorm = mlstm_norm[0].reshape(nh, MLSTM_HEAD_DIM)
    sinks = attn_sinks[0]

    wa = _prompt_attn_weights(w)
    idx = np.arange(2 * LANES)
    same = (idx[:, None] // LANES == idx[None, :] // LANES) & (
        (idx[:, None] // QUARTER) % 2 == (idx[None, :] // QUARTER) % 2)
    bd = jnp.asarray(same, dtype=bf16)
    sink_rows_p = jnp.repeat(sinks.reshape(KV_HEADS, ATTN_GROUP), QBLOCK, axis=1)
    bias = jnp.where(jnp.arange(2 * QBLOCK)[None, None, :] == 0, sink_rows_p[:, :, None], NEG)
    qgq = _quarters(q_norm[0]).reshape(1, LANES)
    kgq = _quarters(k_norm[0]).reshape(1, LANES)
    pos_p = np.arange(tp, dtype=np.float32)
    cos_p, sin_p = _rope_tables_quarters(pos_p)
    cos_w, sin_w = _rope_tables(pos_p[tp - WINDOW:])
    x1_p, k_p, v_p, cext_p, m_p = _prompt_mixer(
        x_prompt[0], cos_p, sin_p, wa, w, wgate, wout_b, bd, bias, anorm, qgq, kgq, kg, cos_w, sin_w,
        gbias, mnorm)

    wq_s = w[:, COL_QA:COL_KA].reshape(D_MODEL, KV_HEADS, ATTN_GROUP, ATTN_HEAD_DIM)
    wq_s = wq_s.transpose(0, 2, 1, 3).reshape(D_MODEL, ATTN_WIDTH)
    wo_a = wout_b[:ATTN_WIDTH].reshape(KV_HEADS, ATTN_GROUP, ATTN_HEAD_DIM, D_MODEL)
    wo_a = wo_a.transpose(1, 0, 2, 3).reshape(ATTN_WIDTH, D_MODEL)
    wout_s = jnp.concatenate([wo_a, wout_b[ATTN_WIDTH:]], axis=0)
    sink_tile = jnp.broadcast_to(
        jnp.repeat(sinks.reshape(KV_HEADS, ATTN_GROUP).T.reshape(-1), tpad)[:, None],
        (ATTN_HEADS * tpad, LANES))
    lanes = np.arange(LANES)
    bd_s = jnp.asarray(lanes[:, None] // ATTN_HEAD_DIM == lanes[None, :] // ATTN_HEAD_DIM, dtype=bf16)
    cos_s, sin_s = _rope_tables(np.arange(tpad, dtype=np.float32) + np.float32(PAST_LEN))
    cos_s = np.tile(cos_s, (SAMPLE_BATCH_BLOCK, 1))
    sin_s = np.tile(sin_s, (SAMPLE_BATCH_BLOCK, 1))
    x_pad = jnp.pad(x_sample, ((0, 0), (0, tpad - nt), (0, 0)))
    ckt = cache_k[0].reshape(nb, WINDOW, KV_WIDTH).transpose(0, 2, 1)
    cvt = cache_v[0].reshape(nb, WINDOW, KV_WIDTH).transpose(0, 2, 1)
    x1_pad, nkt, nvt, c_new, n_new, m_pad = _sample_mixer(
        x_pad, ckt, cvt, state_C[0], state_n[0], state_m[0][:, :, None], cos_s, sin_s, wq_s, w, wgate,
        wout_s, sink_tile, bd_s, anorm, qg, kg, gbias, mnorm)
    y_p, y_s = _ffn(x1_p, x1_pad[:, :nt].reshape(nb * nt, D_MODEL), fnorm, wg_b, wu_b, wd_b)
    m_new = m_pad[:, :nh, 0]

    new_k_s = nkt.transpose(0, 2, 1)
    new_v_s = nvt.transpose(0, 2, 1)

    kv_shape = (1, 1, WINDOW, KV_HEADS, ATTN_HEAD_DIM)
    return (
        y_p[None],
        y_s.reshape(nb, nt, D_MODEL),
        k_p.reshape(kv_shape),
        v_p.reshape(kv_shape),
        cext_p[None, None, :, :, :MLSTM_HEAD_DIM],
        cext_p[None, None, :, :, MLSTM_HEAD_DIM],
        m_p[None, None, :nh, 0],
        new_k_s.reshape(1, nb, WINDOW, KV_HEADS, ATTN_HEAD_DIM),
        new_v_s.reshape(1, nb, WINDOW, KV_HEADS, ATTN_HEAD_DIM),
        c_new.reshape(1, nb, nh, MLSTM_HEAD_DIM, MLSTM_HEAD_DIM),
        n_new.reshape(1, nb, nh, MLSTM_HEAD_DIM),
        m_new.reshape(1, nb, nh),
    )
```

```python
import jax
import jax.numpy as jnp
import numpy as np
from jax import lax
from jax.experimental import pallas as pl
from jax.experimental.pallas import tpu as pltpu

D_MODEL = 1024
PAST_LEN = 16384
ATTN_HEADS = 8
KV_HEADS = 2
ATTN_HEAD_DIM = 64
ATTN_GROUP = ATTN_HEADS // KV_HEADS
ATTN_WIDTH = ATTN_HEADS * ATTN_HEAD_DIM
KV_WIDTH = KV_HEADS * ATTN_HEAD_DIM
WINDOW = 128
ROPE_THETA = 10000.0
MLSTM_HEADS = 4
MLSTM_HEAD_DIM = 128
MLSTM_WIDTH = MLSTM_HEADS * MLSTM_HEAD_DIM
MIX_WIDTH = ATTN_WIDTH + MLSTM_WIDTH
D_FF = 2816
NORM_EPS = 1e-6

LANES = 128
SUBLANES = 8
VMEM_LIMIT = 56 * 1024 * 1024

COL_QA = 0
COL_KA = COL_QA + ATTN_WIDTH
COL_VA = COL_KA + KV_WIDTH
COL_QM = COL_VA + KV_WIDTH
COL_KM = COL_QM + MLSTM_WIDTH
COL_VM = COL_KM + MLSTM_WIDTH
COL_OM = COL_VM + MLSTM_WIDTH
COL_G = COL_OM + MLSTM_WIDTH
IN_WIDTH = COL_G + 2 * MLSTM_HEADS
FG_LANE = SUBLANES

PROMPT_BLOCK = 256
QBLOCK = WINDOW
MCHUNK = 128
PROJ_CHUNK = 256
FFN_BLOCK = 512
SAMPLE_BATCH_BLOCK = 16
SAMPLE_TOKENS = 4
NEG = -1e30

f32 = jnp.float32
bf16 = jnp.bfloat16


def _rms(x, gain):
    return x * lax.rsqrt(jnp.mean(x * x, axis=-1, keepdims=True) + NORM_EPS) * gain


def _segsum64(s, lane):
    for k in (1, 2, 4, 8, 16, 32):
        s = s + jnp.where((lane & k) != 0, pltpu.roll(s, k, 1), pltpu.roll(s, LANES - k, 1))
    return s


def _headnorm_rope(xs, gain, cos, sin_signed, lane):
    ss = _segsum64(xs * xs, lane)
    y = xs * lax.rsqrt(ss * (1.0 / ATTN_HEAD_DIM) + NORM_EPS) * gain
    partner = jnp.where((lane & 32) != 0, pltpu.roll(y, 32, 1), pltpu.roll(y, LANES - 32, 1))
    return y * cos + partner * sin_signed


def _group_sumsq(xs, bd_ref):
    x2 = xs * xs
    hi = x2.astype(bf16)
    lo = (x2 - hi.astype(f32)).astype(bf16)
    return (jnp.dot(hi, bd_ref[...], preferred_element_type=f32)
            + jnp.dot(lo, bd_ref[...], preferred_element_type=f32))


PA_Q = 0
PA_K = PA_Q + ATTN_WIDTH
PA_V = PA_K + KV_HEADS * LANES
PA_WIDTH = PA_V + KV_HEADS * LANES
QUARTER = ATTN_HEAD_DIM // 2
QA_Q = 0
QA_K = QA_Q + 2 * ATTN_WIDTH
QA_V = QA_K + KV_HEADS * LANES
QA_VZ = QA_V + KV_HEADS * LANES
QA_WIDTH = QA_VZ + KV_HEADS * LANES
ZM_WIDTH = 4 * MLSTM_WIDTH
ATTN_PHASES = 4
MLSTM_PHASES = 5


def _norm_rope_quarters(xs, ss, gain, cos, sin_signed):
    y = xs * lax.rsqrt(ss * (1.0 / ATTN_HEAD_DIM) + NORM_EPS) * gain
    return y * cos + pltpu.roll(y, LANES // 2, 1) * sin_signed


def _col_chunks(width):
    return [(c, min(c + PROJ_CHUNK, width)) for c in range(0, width, PROJ_CHUNK)]


def _mixer_proj_jobs(x_ref, anorm_ref, wa_ref, w_ref, wgate_ref, za_ref, zm_ref, gz_ref):
    h = _rms(x_ref[...], anorm_ref[...]).astype(bf16)

    def proj_job(w_src, wc0, z_ref, c0, c1):
        def run():
            z_ref[:, c0:c1] = jnp.dot(h, w_src[:, wc0 + c0:wc0 + c1], preferred_element_type=f32)
        return run

    return ([proj_job(wa_ref, 0, za_ref, c0, c1) for c0, c1 in _col_chunks(PA_WIDTH)]
            + [proj_job(w_ref, COL_QM, zm_ref, c0, c1) for c0, c1 in _col_chunks(ZM_WIDTH)]
            + [proj_job(wgate_ref, 0, gz_ref, 0, LANES)])


def _mixer_prep_jobs(za_ref, gz_ref, cos_ref, sin_ref, bd_ref, qg_ref, kg_ref, gbias_ref, qa_ref,
                     colf_ref, urow_ref, wrow_ref, mst_ref, mout_ref):
    tb = za_ref.shape[0]

    def gates_job():
        lane_t = lax.broadcasted_iota(jnp.int32, (tb, LANES), 1)
        gcol = gz_ref[...] + gbias_ref[...]
        acol = jnp.where(lane_t < FG_LANE, gcol, jax.nn.log_sigmoid(gcol))
        arow = acol.T
        lane8 = lax.broadcasted_iota(jnp.int32, (SUBLANES, LANES), 1)
        nsb = tb // LANES
        slabs = [slice(sb * LANES, (sb + 1) * LANES) for sb in range(nsb)]
        ig_all = jnp.concatenate([arow[0:SUBLANES, ls] for ls in slabs], axis=0)
        b_all = jnp.concatenate([arow[FG_LANE:FG_LANE + SUBLANES, ls] for ls in slabs], axis=0)
        lane_in = lax.broadcasted_iota(jnp.int32, b_all.shape, 1) & (MCHUNK - 1)

        def scan(x, combine, identity):
            k = 1
            while k < MCHUNK:
                terms = [jnp.where(lane_in >= j * k, pltpu.roll(x, j * k, 1), identity)
                         for j in (1, 2, 3) if j * k < MCHUNK]
                for t in terms:
                    x = combine(x, t)
                k *= 4
            return x

        b_all = scan(b_all, jnp.add, 0.0)
        u_all = ig_all - b_all
        cm_all = scan(u_all, jnp.maximum, NEG)
        m_prev = mst_ref[:, 0:1]
        stacks = []
        ends = {}
        for sb in range(nsb):
            sub = slice(sb * SUBLANES, (sb + 1) * SUBLANES)
            for c in range(LANES // MCHUNK):
                at_end = lane8 == c * MCHUNK + MCHUNK - 1
                ends[sb, c] = (jnp.max(jnp.where(at_end, cm_all[sub], NEG), axis=1, keepdims=True),
                               jnp.max(jnp.where(at_end, b_all[sub], NEG), axis=1, keepdims=True))
        for sb, ls in enumerate(slabs):
            sub = slice(sb * SUBLANES, (sb + 1) * SUBLANES)
            b8, u8, cm8 = b_all[sub], u_all[sub], cm_all[sub]
            g8 = jnp.zeros_like(u8)
            mp8 = jnp.zeros_like(u8)
            gl8 = jnp.zeros_like(u8)
            for c in range(LANES // MCHUNK):
                in_chunk = (lane8 // MCHUNK) == c
                gc = jnp.maximum(cm8, m_prev)
                cm_last, b_last = ends[sb, c]
                g_last = jnp.maximum(cm_last, m_prev)
                g8 = jnp.where(in_chunk, gc, g8)
                mp8 = jnp.where(in_chunk, m_prev, mp8)
                gl8 = jnp.where(in_chunk, g_last, gl8)
                m_prev = b_last + g_last
            a8 = jnp.exp(mp8 - g8)
            emt8 = jnp.exp(-(b8 + g8))
            aend8 = jnp.exp(mp8 - gl8)
            stacks.append(jnp.concatenate(
                [g8, a8, emt8, aend8, jnp.zeros((LANES - 4 * SUBLANES, LANES), f32)], axis=0))
            urow_ref[:, ls] = u8
            wrow_ref[:, ls] = jnp.exp(u8 - gl8)
        mst_ref[...] = jnp.broadcast_to(m_prev, mst_ref.shape)
        mout_ref[...] = jnp.broadcast_to(m_prev, mout_ref.shape)
        colf_ref[...] = jnp.concatenate(stacks, axis=1).T

    def rope_job(qb):
        def run():
            rows = slice(qb * QBLOCK, (qb + 1) * QBLOCK)
            lane = lax.broadcasted_iota(jnp.int32, (QBLOCK, LANES), 1)
            head_a = ((lane // QUARTER) & 1) == 0
            row0 = lax.broadcasted_iota(jnp.int32, (QBLOCK, LANES), 0) == 0
            cos = cos_ref[rows, :]
            sin = sin_ref[rows, :]
            ss = [_group_sumsq(za_ref[rows, d * 2 * LANES:(d + 1) * 2 * LANES], bd_ref)
                  for d in range(PA_V // (2 * LANES))]
            for j in range(PA_V // LANES):
                is_q = j < ATTN_WIDTH // LANES
                y = _norm_rope_quarters(za_ref[rows, j * LANES:(j + 1) * LANES],
                                        ss[j // 2][:, (j % 2) * LANES:(j % 2 + 1) * LANES],
                                        qg_ref[...] if is_q else kg_ref[...], cos, sin)
                if is_q:
                    y = y * (ATTN_HEAD_DIM ** -0.5)
                    qa_ref[rows, QA_Q + 2 * j * LANES:QA_Q + (2 * j + 1) * LANES] = (
                        jnp.where(head_a, y, 0.0).astype(bf16))
                    qa_ref[rows, QA_Q + (2 * j + 1) * LANES:QA_Q + (2 * j + 2) * LANES] = (
                        jnp.where(head_a, 0.0, y).astype(bf16))
                else:
                    c = j - ATTN_WIDTH // LANES
                    qa_ref[rows, QA_K + c * LANES:QA_K + (c + 1) * LANES] = y.astype(bf16)
            for c in range(KV_HEADS):
                v = za_ref[rows, PA_V + c * LANES:PA_V + (c + 1) * LANES]
                qa_ref[rows, QA_V + c * LANES:QA_V + (c + 1) * LANES] = v.astype(bf16)
                qa_ref[rows, QA_VZ + c * LANES:QA_VZ + (c + 1) * LANES] = (
                    jnp.where(row0, 0.0, v).astype(bf16))
        return run

    return [gates_job] + [rope_job(qb) for qb in range(tb // QBLOCK)]


def _mixer_outproj_jobs(xs_ref, mix_ref, wout_ref, x1_ref):
    def job(c0, c1):
        def run():
            x1_ref[:, c0:c1] = xs_ref[:, c0:c1] + jnp.dot(mix_ref[...], wout_ref[:, c0:c1],
                                                          preferred_element_type=f32)
        return run

    return [job(c0, c1) for c0, c1 in _col_chunks(D_MODEL)]


def _mixer_window_out(xs_ref, anorm_ref, w_ref, kgs_ref, coss_ref, sins_ref, kout_ref, vout_ref):
    tb = xs_ref.shape[0]
    h = _rms(xs_ref[tb - WINDOW:, :], anorm_ref[...]).astype(bf16)
    zs = jnp.dot(h, w_ref[:, COL_KA:COL_QM], preferred_element_type=f32)
    lane_s = lax.broadcasted_iota(jnp.int32, (WINDOW, LANES), 1)
    kout_ref[...] = _headnorm_rope(zs[:, :KV_WIDTH], kgs_ref[...], coss_ref[...], sins_ref[...], lane_s)
    vout_ref[...] = zs[:, KV_WIDTH:]


def _mixer_core(first_block, last_block, fillers, prep_jobs, qa_ref, zm_ref, colf_ref, urow_ref, wrow_ref,
                mix_ref, bias_ref, mnorm_ref, kprev_ref, vprev_ref, cst_ref, cext_ref):
    tb = qa_ref.shape[0]
    nqb = tb // QBLOCK
    nch = tb // MCHUNK
    fillers = list(fillers)
    n_fill = len(fillers)
    slots = len(prep_jobs) + ATTN_PHASES + nch * MLSTM_PHASES
    progress = [0]

    def fill():
        progress[0] += 1
        while n_fill - len(fillers) < min(n_fill, -(-n_fill * progress[0] // slots)):
            fillers.pop(0)()

    low_half = lax.broadcasted_iota(jnp.int32, (QBLOCK, LANES), 1) < ATTN_HEAD_DIM
    qi = lax.broadcasted_iota(jnp.int32, (ATTN_GROUP * QBLOCK, 2 * QBLOCK), 0) & (QBLOCK - 1)
    kj = lax.broadcasted_iota(jnp.int32, (ATTN_GROUP * QBLOCK, 2 * QBLOCK), 1)
    band = (kj > qi) & (kj <= qi + QBLOCK)
    ones_slab = jnp.ones((2 * QBLOCK, LANES), bf16)

    def attn_phases(chains):
        rows_of = lambda qb: slice(qb * QBLOCK, (qb + 1) * QBLOCK)
        st = {}

        def scores():
            for qb, c in chains:
                rows = rows_of(qb)
                kcols = slice(QA_K + c * LANES, QA_K + (c + 1) * LANES)
                if qb == 0:
                    kprev, vprev = kprev_ref[c], vprev_ref[c]
                else:
                    kprev = qa_ref[rows_of(qb - 1), kcols]
                    vprev = qa_ref[rows_of(qb - 1), QA_VZ + c * LANES:QA_VZ + (c + 1) * LANES]
                kcat = jnp.concatenate([kprev, qa_ref[rows, kcols]], axis=0)
                vcat = jnp.concatenate(
                    [vprev, qa_ref[rows, QA_V + c * LANES:QA_V + (c + 1) * LANES]], axis=0)
                st['vext', qb, c] = jnp.concatenate([vcat, ones_slab], axis=1)
                q0 = QA_Q + c * ATTN_GROUP * LANES
                qst = jnp.concatenate([qa_ref[rows, q0 + g * LANES:q0 + (g + 1) * LANES]
                                       for g in range(ATTN_GROUP)], axis=0)
                st['s', qb, c] = lax.dot_general(qst, kcat, (((1,), (1,)), ((), ())),
                                                 preferred_element_type=f32)

        def softmax():
            for qb, c in chains:
                valid = band & (kj >= QBLOCK) if (first_block and qb == 0) else band
                s = jnp.where(valid, st.pop(('s', qb, c)), bias_ref[c])
                st['p', qb, c] = jnp.exp(s - jnp.max(s, axis=-1, keepdims=True)).astype(bf16)

        def values():
            for qb, c in chains:
                st['of', qb, c] = jnp.dot(st.pop(('p', qb, c)), st.pop(('vext', qb, c)),
                                          preferred_element_type=f32)

        def normalise():
            for qb, c in chains:
                of = st.pop(('of', qb, c))
                o = of[:, :LANES] / of[:, LANES:]
                for jj in range(2):
                    pair = jnp.where(low_half, o[(2 * jj) * QBLOCK:(2 * jj + 1) * QBLOCK],
                                     o[(2 * jj + 1) * QBLOCK:(2 * jj + 2) * QBLOCK])
                    col = (2 * c + jj) * LANES
                    mix_ref[rows_of(qb), col:col + LANES] = pair.astype(bf16)

        return [scores, softmax, values, normalise]

    ti = lax.broadcasted_iota(jnp.int32, (MCHUNK, MCHUNK), 0)
    si = lax.broadcasted_iota(jnp.int32, (MCHUNK, MCHUNK), 1)
    causal = si <= ti
    ones_l = jnp.ones((MCHUNK, LANES), bf16)

    cexts = [cst_ref[hd] for hd in range(MLSTM_HEADS)]

    def mlstm_phases(c):
        rows = slice(c * MCHUNK, (c + 1) * MCHUNK)
        heads = range(MLSTM_HEADS)
        hcols = lambda k, hd: slice((k * MLSTM_HEADS + hd) * MLSTM_HEAD_DIM,
                                    (k * MLSTM_HEADS + hd + 1) * MLSTM_HEAD_DIM)
        st = {}

        def scores():
            for hd in heads:
                st['q', hd] = zm_ref[rows, hcols(0, hd)].astype(bf16)
                kf = zm_ref[rows, hcols(1, hd)] * (MLSTM_HEAD_DIM ** -0.5)
                st['kt', hd] = kf.T
                st['v', hd] = jnp.concatenate([zm_ref[rows, hcols(2, hd)].astype(bf16), ones_l], axis=1)
                st['s', hd] = lax.dot_general(st['q', hd], kf.astype(bf16),
                                              (((1,), (1,)), ((), ())), preferred_element_type=f32)

        def decay():
            for hd in heads:
                g_c = colf_ref[rows, hd:hd + 1]
                u_r = urow_ref[hd:hd + 1, rows]
                dmat = jnp.exp(jnp.where(causal, u_r - g_c, NEG))
                st['s', hd] = (st['s', hd] * dmat).astype(bf16)
                st['kt', hd] = (st['kt', hd] * wrow_ref[hd:hd + 1, rows]).astype(bf16)

        def readout():
            for hd in heads:
                a_c = colf_ref[rows, SUBLANES + hd:SUBLANES + hd + 1]
                st['nd', hd] = (
                    a_c * jnp.dot(st.pop(('q', hd)), cexts[hd].astype(bf16), preferred_element_type=f32)
                    + jnp.dot(st.pop(('s', hd)), st['v', hd], preferred_element_type=f32))

        def emit():
            for hd in heads:
                nd = st.pop(('nd', hd))
                emt_c = colf_ref[rows, 2 * SUBLANES + hd:2 * SUBLANES + hd + 1]
                hraw = nd[:, :MLSTM_HEAD_DIM] / jnp.maximum(jnp.abs(nd[:, MLSTM_HEAD_DIM:]), emt_c)
                hn = _rms(hraw, mnorm_ref[hd:hd + 1, :])
                og = zm_ref[rows, hcols(3, hd)]
                mix_ref[rows, ATTN_WIDTH + hd * MLSTM_HEAD_DIM:ATTN_WIDTH + (hd + 1) * MLSTM_HEAD_DIM] = (
                    (hn * jax.nn.sigmoid(og)).astype(bf16))

        def update():
            for hd in heads:
                aend = colf_ref[c * MCHUNK:c * MCHUNK + 1, 3 * SUBLANES + hd:3 * SUBLANES + hd + 1]
                cexts[hd] = aend * cexts[hd] + jnp.dot(st.pop(('kt', hd)), st.pop(('v', hd)),
                                                       preferred_element_type=f32)

        return [scores, decay, readout, emit, update]

    gates_job, rope_jobs = prep_jobs[0], prep_jobs[1:]
    attn = attn_phases([(qb, c) for qb in range(nqb) for c in range(KV_HEADS)])
    chunks = [mlstm_phases(c) for c in range(nch)]
    mlstm = [p for scores, decay, _, _, _ in chunks for p in (scores, decay)]
    for c, (_, _, readout, emit, update) in enumerate(chunks):
        mlstm += [readout] + ([chunks[c - 1][3]] if c else []) + [update]
    mlstm.append(chunks[-1][3])
    assert len(attn) == ATTN_PHASES and len(mlstm) == nch * MLSTM_PHASES
    order = [gates_job] + rope_jobs + attn[:1]
    rest = attn[1:]
    for i in range(max(len(rest), len(mlstm))):
        order += rest[i:i + 1] + mlstm[i:i + 1]
    assert len(order) == slots
    for job in order:
        job()
        fill()
    last_rows = slice(tb - QBLOCK, tb)
    for c in range(KV_HEADS):
        kprev_ref[c] = qa_ref[last_rows, QA_K + c * LANES:QA_K + (c + 1) * LANES]
        vprev_ref[c] = qa_ref[last_rows, QA_VZ + c * LANES:QA_VZ + (c + 1) * LANES]
    for hd in range(MLSTM_HEADS):
        cst_ref[hd] = cexts[hd]
        if last_block:
            cext_ref[hd] = cexts[hd]


def _prompt_mixer_kernel(x_ref, cos_ref, sin_ref, wa_ref, w_ref, wgate_ref, wout_ref, bd_ref, bias_ref,
                         anorm_ref, qg_ref, kg_ref, kgs_ref, coss_ref, sins_ref, gbias_ref, mnorm_ref,
                         x1_ref, kout_ref, vout_ref, cext_ref, mout_ref,
                         za0, za1, zm0, zm1, gz0, gz1, mix0, mix1, xs0, xs1,
                         qa_ref, colf_ref, urow_ref, wrow_ref, kprev_ref, vprev_ref, cst_ref, mst_ref):
    step = pl.program_id(0)
    nblk = pl.num_programs(0) - 2
    za, zm, gz, mix, xs = (za0, za1), (zm0, zm1), (gz0, gz1), (mix0, mix1), (xs0, xs1)

    def run(parity, do_in, do_core, do_out, first_block=False, last_block=False):
        other = 1 - parity
        jobs = []
        if do_out:
            jobs += _mixer_outproj_jobs(xs[parity], mix[parity], wout_ref, x1_ref)
        if do_in:
            jobs += _mixer_proj_jobs(x_ref, anorm_ref, wa_ref, w_ref, wgate_ref, za[parity], zm[parity],
                                     gz[parity])
        if do_core:
            prep = _mixer_prep_jobs(za[other], gz[other], cos_ref, sin_ref, bd_ref, qg_ref, kg_ref,
                                    gbias_ref, qa_ref, colf_ref, urow_ref, wrow_ref, mst_ref, mout_ref)
            _mixer_core(first_block, last_block, jobs, prep, qa_ref, zm[other], colf_ref, urow_ref,
                        wrow_ref, mix[other], bias_ref, mnorm_ref, kprev_ref, vprev_ref, cst_ref, cext_ref)
        else:
            for job in jobs:
                job()
        if last_block:
            _mixer_window_out(xs[other], anorm_ref, w_ref, kgs_ref, coss_ref, sins_ref, kout_ref, vout_ref)
        if do_in:
            xs[parity][...] = x_ref[...]

    @pl.when(step == 0)
    def _first():
        kprev_ref[...] = jnp.zeros_like(kprev_ref)
        vprev_ref[...] = jnp.zeros_like(vprev_ref)
        cst_ref[...] = jnp.zeros_like(cst_ref)
        mst_ref[...] = jnp.zeros_like(mst_ref)
        run(0, True, False, False)

    @pl.when(step == 1)
    def _second():
        run(1, True, True, False, first_block=True)

    steady = (step >= 2) & (step < nblk)

    @pl.when(steady & (step % 2 == 0))
    def _even():
        run(0, True, True, True)

    @pl.when(steady & (step % 2 == 1))
    def _odd():
        run(1, True, True, True)

    @pl.when(step == nblk)
    def _drain_core():
        run(0, False, True, True, last_block=True)

    @pl.when(step == nblk + 1)
    def _drain_out():
        run(1, False, False, True)


def _const_spec(shape, single=False):
    nd = len(shape)
    if single:
        return pl.BlockSpec(shape, lambda i, *_: (0,) * nd, pipeline_mode=pl.Buffered(1))
    return pl.BlockSpec(shape, lambda i, *_: (0,) * nd)


def _prompt_mixer(x, cos, sin, wa, w, wgate, wout_b, bd, bias, anorm, qg, kg, kgs, coss, sins, gbias, mnorm):
    t = x.shape[0]
    tb = PROMPT_BLOCK
    nblk = t // tb
    assert nblk % 2 == 0 and nblk >= 4
    state_shape = (MLSTM_HEADS, MLSTM_HEAD_DIM, 2 * MLSTM_HEAD_DIM)
    last = nblk - 1
    lag = lambda d: (lambda i: (jnp.clip(i - d, 0, last), 0))
    return pl.pallas_call(
        _prompt_mixer_kernel,
        grid=(nblk + 2,),
        in_specs=[
            pl.BlockSpec((tb, D_MODEL), lag(0)),
            pl.BlockSpec((tb, LANES), lag(1)),
            pl.BlockSpec((tb, LANES), lag(1)),
            _const_spec((D_MODEL, PA_WIDTH), single=True),
            _const_spec((D_MODEL, IN_WIDTH), single=True),
            _const_spec((D_MODEL, LANES), single=True),
            _const_spec((MIX_WIDTH, D_MODEL), single=True),
            _const_spec((2 * LANES, 2 * LANES), single=True),
            _const_spec((KV_HEADS, ATTN_GROUP * QBLOCK, 2 * QBLOCK), single=True),
            _const_spec((1, D_MODEL)),
            _const_spec((1, LANES)),
            _const_spec((1, LANES)),
            _const_spec((1, LANES)),
            _const_spec((WINDOW, LANES)),
            _const_spec((WINDOW, LANES)),
            _const_spec((1, LANES)),
            _const_spec((MLSTM_HEADS, MLSTM_HEAD_DIM)),
        ],
        out_specs=[
            pl.BlockSpec((tb, D_MODEL), lag(2)),
            _const_spec((WINDOW, KV_WIDTH)),
            _const_spec((WINDOW, KV_WIDTH)),
            _const_spec(state_shape),
            _const_spec((SUBLANES, LANES)),
        ],
        out_shape=[
            jax.ShapeDtypeStruct((t, D_MODEL), f32),
            jax.ShapeDtypeStruct((WINDOW, KV_WIDTH), f32),
            jax.ShapeDtypeStruct((WINDOW, KV_WIDTH), f32),
            jax.ShapeDtypeStruct(state_shape, f32),
            jax.ShapeDtypeStruct((SUBLANES, LANES), f32),
        ],
        scratch_shapes=(
            [pltpu.VMEM((tb, PA_WIDTH), f32)] * 2 + [pltpu.VMEM((tb, ZM_WIDTH), f32)] * 2
            + [pltpu.VMEM((tb, LANES), f32)] * 2
            + [pltpu.VMEM((tb, MIX_WIDTH), bf16)] * 2 + [pltpu.VMEM((tb, D_MODEL), f32)] * 2
            + [pltpu.VMEM((tb, QA_WIDTH), bf16), pltpu.VMEM((tb, LANES), f32)]
            + [pltpu.VMEM((SUBLANES, tb), f32)] * 2
            + [pltpu.VMEM((KV_HEADS, WINDOW, LANES), bf16)] * 2
            + [pltpu.VMEM(state_shape, f32), pltpu.VMEM((SUBLANES, LANES), f32)]),
        compiler_params=pltpu.CompilerParams(
            dimension_semantics=("arbitrary",), vmem_limit_bytes=VMEM_LIMIT),
        name="prompt_mixer",
    )(x, cos, sin, wa, w, wgate, wout_b, bd, bias, anorm, qg, kg, kgs, coss, sins, gbias, mnorm)


def _ffn_kernel(xp_ref, xs_ref, g_ref, wg_ref, wu_ref, wd_ref, op_ref, os_ref, act_ref):
    step = pl.program_id(0)
    last = pl.num_programs(0) - 1

    @pl.when(step < last)
    def _prompt_rows():
        _ffn_rows(xp_ref, g_ref, wg_ref, wu_ref, wd_ref, op_ref, act_ref)

    @pl.when(step == last)
    def _sample_rows():
        _ffn_rows(xs_ref, g_ref, wg_ref, wu_ref, wd_ref, os_ref, act_ref)


def _ffn_rows(x_ref, g_ref, wg_ref, wu_ref, wd_ref, o_ref, act_ref):
    hf = _rms(x_ref[...], g_ref[...]).astype(bf16)
    for c0, c1 in _col_chunks(D_FF):
        gate = jnp.dot(hf, wg_ref[:, c0:c1], preferred_element_type=f32)
        up = jnp.dot(hf, wu_ref[:, c0:c1], preferred_element_type=f32)
        act_ref[:, c0:c1] = (gate * jax.nn.sigmoid(gate) * up).astype(bf16)
    for c0, c1 in _col_chunks(D_MODEL):
        o_ref[:, c0:c1] = x_ref[:, c0:c1] + jnp.dot(act_ref[...], wd_ref[:, c0:c1],
                                                    preferred_element_type=f32)


def _ffn(x_p, x_s, fnorm, w_gate, w_up, w_down):
    n = x_p.shape[0]
    ns = x_s.shape[0]
    tm = FFN_BLOCK
    last = n // tm - 1
    return pl.pallas_call(
        _ffn_kernel,
        grid=(n // tm + 1,),
        in_specs=[
            pl.BlockSpec((tm, D_MODEL), lambda i: (jnp.minimum(i, last), 0)),
            _const_spec((ns, D_MODEL), single=True),
            _const_spec((1, D_MODEL)),
            _const_spec((D_MODEL, D_FF), single=True),
            _const_spec((D_MODEL, D_FF), single=True),
            _const_spec((D_FF, D_MODEL), single=True),
        ],
        out_specs=[pl.BlockSpec((tm, D_MODEL), lambda i: (jnp.minimum(i, last), 0)),
                   _const_spec((ns, D_MODEL))],
        out_shape=[jax.ShapeDtypeStruct((n, D_MODEL), f32), jax.ShapeDtypeStruct((ns, D_MODEL), f32)],
        scratch_shapes=[pltpu.VMEM((tm, D_FF), bf16)],
        compiler_params=pltpu.CompilerParams(
            dimension_semantics=("arbitrary",), vmem_limit_bytes=VMEM_LIMIT),
        name="ffn",
    )(x_p, x_s, fnorm, w_gate, w_up, w_down)


def _sample_mixer_kernel(x_ref, ckt_ref, cvt_ref, c_ref, n_ref, m_ref, cos_ref, sin_ref, wqs_ref, w_ref,
                         wgate_ref, wout_ref, sink_ref, bd_ref, anorm_ref, qg_ref, kg_ref, gbias_ref,
                         mnorm_ref, x1_ref, nkt_ref, nvt_ref, cn_ref, nn_ref, mn_ref):
    bb, tpad, _ = x_ref.shape
    nrows = bb * tpad
    nreal = SAMPLE_TOKENS
    h = _rms(x_ref[...].reshape(nrows, D_MODEL), anorm_ref[...]).astype(bf16)
    z = jnp.concatenate(
        [jnp.dot(h, wqs_ref[...], preferred_element_type=f32),
         jnp.dot(h, w_ref[:, COL_KA:COL_G], preferred_element_type=f32),
         jnp.dot(h, wgate_ref[...], preferred_element_type=f32)], axis=1)
    lane = lax.broadcasted_iota(jnp.int32, (nrows, LANES), 1)
    low = lane < ATTN_HEAD_DIM
    cos = cos_ref[...]
    sin = sin_ref[...]

    def per_seq(a):
        return a.reshape(bb, tpad, a.shape[-1])

    def norm_rope(xs, gain):
        y = xs * lax.rsqrt(_group_sumsq(xs, bd_ref) * (1.0 / ATTN_HEAD_DIM) + NORM_EPS) * gain
        partner = jnp.where((lane & QUARTER) != 0, pltpu.roll(y, QUARTER, 1),
                            pltpu.roll(y, LANES - QUARTER, 1))
        return y * cos + partner * sin

    q_rows = []
    for j in range(ATTN_GROUP):
        qs = norm_rope(z[:, COL_QA + j * LANES:COL_QA + (j + 1) * LANES], qg_ref[...])
        qs = qs * (ATTN_HEAD_DIM ** -0.5)
        q_rows.append(per_seq(jnp.where(low, qs, 0.0)).astype(bf16))
        q_rows.append(per_seq(jnp.where(low, 0.0, qs)).astype(bf16))
    qbd = jnp.concatenate(q_rows, axis=1)
    knew = norm_rope(z[:, COL_KA:COL_KA + KV_WIDTH], kg_ref[...])
    vnew = z[:, COL_VA:COL_VA + KV_WIDTH]
    zpad = jnp.zeros((bb, LANES - tpad, LANES), bf16)
    knp = jnp.concatenate([per_seq(knew).astype(bf16), zpad], axis=1)
    vnp = jnp.concatenate([per_seq(vnew).astype(bf16), zpad], axis=1)
    ckt = ckt_ref[...]
    cvt = cvt_ref[...]
    s = jnp.concatenate(
        [jnp.einsum('bqd,bdw->bqw', qbd, ckt.astype(bf16), preferred_element_type=f32),
         jnp.einsum('bqd,bkd->bqk', qbd, knp, preferred_element_type=f32)], axis=2)
    tq = lax.broadcasted_iota(jnp.int32, s.shape, 1) & (tpad - 1)
    kj = lax.broadcasted_iota(jnp.int32, s.shape, 2)
    valid = ((kj < WINDOW) & (kj > tq)) | ((kj >= WINDOW) & (kj - WINDOW <= tq) & (kj - WINDOW < nreal))
    s = jnp.where(valid, s, NEG)
    sink = sink_ref[:, 0:1][None]
    mx = jnp.maximum(jnp.max(s, axis=-1, keepdims=True), sink)
    p = jnp.exp(s - mx)
    den = jnp.sum(p, axis=-1, keepdims=True) + jnp.exp(sink - mx)
    pb = p.astype(bf16)
    o = (jnp.einsum('bqw,bdw->bqd', pb[:, :, :WINDOW], cvt.astype(bf16), preferred_element_type=f32)
         + jnp.einsum('bqk,bkd->bqd', pb[:, :, WINDOW:], vnp, preferred_element_type=f32)) / den
    low3 = lax.broadcasted_iota(jnp.int32, (bb, tpad, LANES), 2) < ATTN_HEAD_DIM
    mix_parts = []
    for j in range(ATTN_GROUP):
        r0 = 2 * j * tpad
        pair = jnp.where(low3, o[:, r0:r0 + tpad, :], o[:, r0 + tpad:r0 + 2 * tpad, :])
        mix_parts.append(pair.reshape(nrows, LANES).astype(bf16))

    keep = lax.broadcasted_iota(jnp.int32, (KV_WIDTH, WINDOW), 1) < WINDOW - nreal
    knt = knew.T
    vnt = vnew.T
    for b in range(bb):
        shift = (WINDOW - nreal - b * tpad) % LANES
        nkt_ref[b] = jnp.where(keep, pltpu.roll(ckt_ref[b], WINDOW - nreal, 1), pltpu.roll(knt, shift, 1))
        nvt_ref[b] = jnp.where(keep, pltpu.roll(cvt_ref[b], WINDOW - nreal, 1), pltpu.roll(vnt, shift, 1))

    gz = per_seq(z[:, COL_G:COL_G + LANES] + gbias_ref[...])
    lgz = jax.nn.log_sigmoid(gz)
    trow = lax.broadcasted_iota(jnp.int32, (bb, tpad, 1), 1)
    real = trow < nreal
    mn_ref[...] = jnp.zeros_like(mn_ref)
    heads = range(MLSTM_HEADS)
    hcols = lambda base, hd: slice(base + hd * MLSTM_HEAD_DIM, base + (hd + 1) * MLSTM_HEAD_DIM)
    last = nreal - 1
    st = {}
    for hd in heads:
        m0 = m_ref[:, hd:hd + 1, :]
        ig_c = jnp.where(real, gz[:, :, hd:hd + 1], NEG)
        lf_c = jnp.where(real, lgz[:, :, FG_LANE + hd:FG_LANE + hd + 1], 0.0)
        b_c = jnp.zeros_like(lf_c)
        for sx in range(nreal):
            b_c = b_c + jnp.where(trow >= sx, lf_c[:, sx:sx + 1, :], 0.0)
        dlog = [jnp.where(trow >= sx, b_c - b_c[:, sx:sx + 1, :] + ig_c[:, sx:sx + 1, :], NEG)
                for sx in range(nreal)]
        inter = b_c + m0
        m_t = inter
        for sx in range(nreal):
            m_t = jnp.maximum(m_t, dlog[sx])
        m_new = m_t[:, last:last + 1, :]
        b_last = b_c[:, last:last + 1, :]
        st[hd] = dict(m_t=m_t, a=jnp.exp(inter - m_t), dexp=[jnp.exp(d - m_t) for d in dlog],
                      m_new=m_new, a_end=jnp.exp(b_last + m0 - m_new),
                      w_c=jnp.exp(b_last - b_c + ig_c - m_new))
    for hd in heads:
        s = st[hd]
        s['q'] = per_seq(z[:, hcols(COL_QM, hd)])
        s['k'] = per_seq(z[:, hcols(COL_KM, hd)]) * (MLSTM_HEAD_DIM ** -0.5)
        s['v'] = per_seq(z[:, hcols(COL_VM, hd)])
        s['qc'] = jnp.einsum('btd,bde->bte', s['q'].astype(bf16), c_ref[:, hd].astype(bf16),
                             preferred_element_type=f32)
    for hd in heads:
        s = st[hd]
        q, k, v = s['q'], s['k'], s['v']
        num = s['a'] * s.pop('qc')
        den_m = s['a'] * jnp.sum(q * n_ref[:, hd:hd + 1, :], axis=2, keepdims=True)
        for sx in range(nreal):
            sd = jnp.sum(q * k[:, sx:sx + 1, :], axis=2, keepdims=True) * s['dexp'][sx]
            num = num + sd * v[:, sx:sx + 1, :]
            den_m = den_m + sd
        hraw = num / jnp.maximum(jnp.abs(den_m), jnp.exp(-s['m_t']))
        hn = _rms(hraw, mnorm_ref[hd:hd + 1, :][None])
        og = per_seq(z[:, hcols(COL_OM, hd)])
        mix_parts.append((hn * jax.nn.sigmoid(og)).reshape(nrows, MLSTM_HEAD_DIM).astype(bf16))
    for hd in heads:
        s = st[hd]
        kw = s['k'] * s['w_c']
        cn_ref[:, hd] = s['a_end'] * c_ref[:, hd] + jnp.einsum(
            'bsd,bse->bde', kw.astype(bf16), s['v'].astype(bf16), preferred_element_type=f32)
        nn_ref[:, hd:hd + 1, :] = s['a_end'] * n_ref[:, hd:hd + 1, :] + jnp.sum(kw, axis=1, keepdims=True)
        mn_ref[:, hd:hd + 1, :] = jnp.broadcast_to(s['m_new'], (bb, 1, LANES))

    mix = jnp.concatenate(mix_parts, axis=1)
    x1 = x_ref[...].reshape(nrows, D_MODEL) + jnp.dot(mix, wout_ref[...], preferred_element_type=f32)
    x1_ref[...] = x1.reshape(bb, tpad, D_MODEL)


def _sample_mixer(x_pad, ckt, cvt, c0, n0, m0, cos, sin, wq_s, w, wgate, wout_s, sink_tile, bd, anorm, qg,
                  kg, gbias, mnorm):
    nb, tpad, _ = x_pad.shape
    bb = SAMPLE_BATCH_BLOCK
    nh = MLSTM_HEADS
    blk = lambda shape: pl.BlockSpec(shape, lambda i: (i,) + (0,) * (len(shape) - 1))
    cblk = (bb, nh, MLSTM_HEAD_DIM, MLSTM_HEAD_DIM)
    return pl.pallas_call(
        _sample_mixer_kernel,
        grid=(nb // bb,),
        in_specs=[blk((bb, tpad, D_MODEL)), blk((bb, KV_WIDTH, WINDOW)), blk((bb, KV_WIDTH, WINDOW)),
                  blk(cblk), blk((bb, nh, MLSTM_HEAD_DIM)), blk((bb, nh, 1)),
                  _const_spec((bb * tpad, LANES)), _const_spec((bb * tpad, LANES)),
                  _const_spec((D_MODEL, ATTN_WIDTH), single=True),
                  _const_spec((D_MODEL, IN_WIDTH), single=True),
                  _const_spec((D_MODEL, LANES), single=True),
                  _const_spec((MIX_WIDTH, D_MODEL), single=True),
                  _const_spec((ATTN_HEADS * tpad, LANES)), _const_spec((LANES, LANES)),
                  _const_spec((1, D_MODEL)),
                  _const_spec((1, LANES)), _const_spec((1, LANES)), _const_spec((1, LANES)),
                  _const_spec((nh, MLSTM_HEAD_DIM))],
        out_specs=[blk((bb, tpad, D_MODEL)), blk((bb, KV_WIDTH, WINDOW)), blk((bb, KV_WIDTH, WINDOW)),
                   blk(cblk), blk((bb, nh, MLSTM_HEAD_DIM)), blk((bb, tpad, LANES))],
        out_shape=[jax.ShapeDtypeStruct((nb, tpad, D_MODEL), f32),
                   jax.ShapeDtypeStruct((nb, KV_WIDTH, WINDOW), f32),
                   jax.ShapeDtypeStruct((nb, KV_WIDTH, WINDOW), f32),
                   jax.ShapeDtypeStruct((nb,) + cblk[1:], f32),
                   jax.ShapeDtypeStruct((nb, nh, MLSTM_HEAD_DIM), f32),
                   jax.ShapeDtypeStruct((nb, tpad, LANES), f32)],
        compiler_params=pltpu.CompilerParams(
            dimension_semantics=("arbitrary",), vmem_limit_bytes=VMEM_LIMIT),
        name="sample_mixer",
    )(x_pad, ckt, cvt, c0, n0, m0, cos, sin, wq_s, w, wgate, wout_s, sink_tile, bd, anorm, qg, kg, gbias,
      mnorm)


def _rope_angles(pos):
    half = ATTN_HEAD_DIM // 2
    inv = ROPE_THETA ** (-np.arange(half, dtype=np.float64) / half)
    ang = pos.astype(np.float64)[:, None] * inv[None, :]
    return np.cos(ang).astype(np.float32), np.sin(ang).astype(np.float32)


def _rope_tables(pos):
    c, s = _rope_angles(pos)
    cos = np.tile(c, (1, LANES // QUARTER))
    sin = np.tile(np.concatenate([-s, s], axis=1), (1, LANES // ATTN_HEAD_DIM))
    return cos, sin


def _rope_tables_quarters(pos):
    c, s = _rope_angles(pos)
    return np.tile(c, (1, LANES // QUARTER)), np.concatenate([-s, -s, s, s], axis=1)


def _quarters(a):
    lo, hi = a[..., :QUARTER], a[..., QUARTER:]
    return jnp.concatenate([lo, lo, hi, hi], axis=-1)


def _prompt_attn_weights(w):
    d = w.shape[0]
    wq = w[:, COL_QA:COL_KA].reshape(d, ATTN_WIDTH // LANES, 2, 2, QUARTER)
    wq = wq.transpose(0, 1, 3, 2, 4).reshape(d, ATTN_WIDTH)
    wk = _quarters(w[:, COL_KA:COL_VA].reshape(d, KV_HEADS, ATTN_HEAD_DIM)).reshape(d, KV_HEADS * LANES)
    wv = w[:, COL_VA:COL_QM].reshape(d, KV_HEADS, 1, ATTN_HEAD_DIM)
    wv = jnp.broadcast_to(wv, (d, KV_HEADS, 2, ATTN_HEAD_DIM)).reshape(d, KV_HEADS * LANES)
    return jnp.concatenate([wq, wk, wv], axis=1)


def kernel(x_prompt, x_sample, cache_k, cache_v, state_C, state_n, state_m, attn_norm, w_in, q_norm,
           k_norm, attn_sinks, b_ig, b_fg, mlstm_norm, w_out, ffn_norm, w_gate, w_up, w_down):
    assert w_in.shape[0] == 1 and x_prompt.shape[0] == 1
    tp = x_prompt.shape[1]
    nb, nt = x_sample.shape[0], x_sample.shape[1]
    assert nt == SAMPLE_TOKENS
    tpad = SUBLANES
    nh = MLSTM_HEADS

    w = w_in[0].astype(bf16)
    pad_a = jnp.zeros((D_MODEL, FG_LANE - nh), bf16)
    pad_b = jnp.zeros((D_MODEL, LANES - FG_LANE - nh), bf16)
    wgate = jnp.concatenate([w[:, COL_G:COL_G + nh], pad_a, w[:, COL_G + nh:], pad_b], axis=1)
    gbias = jnp.concatenate(
        [b_ig[0], jnp.zeros((FG_LANE - nh,), f32), b_fg[0], jnp.zeros((LANES - FG_LANE - nh,), f32)]
    ).reshape(1, LANES)
    wout_b = w_out[0].astype(bf16)
    anorm = attn_norm[0].reshape(1, D_MODEL)
    fnorm = ffn_norm[0].reshape(1, D_MODEL)
    qg = jnp.tile(q_norm[0], LANES // ATTN_HEAD_DIM).reshape(1, LANES)
    kg = jnp.tile(k_norm[0], LANES // ATTN_HEAD_DIM).reshape(1, LANES)
    mnorm = mlstm_norm[0].reshape(nh, MLSTM_HEAD_DIM)
    sinks = attn_sinks[0]

    wa = _prompt_attn_weights(w)
    idx = np.arange(2 * LANES)
    same = (idx[:, None] // LANES == idx[None, :] // LANES) & (
        (idx[:, None] // QUARTER) % 2 == (idx[None, :] // QUARTER) % 2)
    bd = jnp.asarray(same, dtype=bf16)
    sink_rows_p = jnp.repeat(sinks.reshape(KV_HEADS, ATTN_GROUP), QBLOCK, axis=1)
    bias = jnp.where(jnp.arange(2 * QBLOCK)[None, None, :] == 0, sink_rows_p[:, :, None], NEG)
    qgq = _quarters(q_norm[0]).reshape(1, LANES)
    kgq = _quarters(k_norm[0]).reshape(1, LANES)
    pos_p = np.arange(tp, dtype=np.float32)
    cos_p, sin_p = _rope_tables_quarters(pos_p)
    cos_w, sin_w = _rope_tables(pos_p[tp - WINDOW:])
    x1_p, k_p, v_p, cext_p, m_p = _prompt_mixer(
        x_prompt[0], cos_p, sin_p, wa, w, wgate, wout_b, bd, bias, anorm, qgq, kgq, kg, cos_w, sin_w,
        gbias, mnorm)

    wq_s = w[:, COL_QA:COL_KA].reshape(D_MODEL, KV_HEADS, ATTN_GROUP, ATTN_HEAD_DIM)
    wq_s = wq_s.transpose(0, 2, 1, 3).reshape(D_MODEL, ATTN_WIDTH)
    wo_a = wout_b[:ATTN_WIDTH].reshape(KV_HEADS, ATTN_GROUP, ATTN_HEAD_DIM, D_MODEL)
    wo_a = wo_a.transpose(1, 0, 2, 3).reshape(ATTN_WIDTH, D_MODEL)
    wout_s = jnp.concatenate([wo_a, wout_b[ATTN_WIDTH:]], axis=0)
    sink_tile = jnp.broadcast_to(
        jnp.repeat(sinks.reshape(KV_HEADS, ATTN_GROUP).T.reshape(-1), tpad)[:, None],
        (ATTN_HEADS * tpad, LANES))
    lanes = np.arange(LANES)
    bd_s = jnp.asarray(lanes[:, None] // ATTN_HEAD_DIM == lanes[None, :] // ATTN_HEAD_DIM, dtype=bf16)
    cos_s, sin_s = _rope_tables(np.arange(tpad, dtype=np.float32) + np.float32(PAST_LEN))
    cos_s = np.tile(cos_s, (SAMPLE_BATCH_BLOCK, 1))
    sin_s = np.tile(sin_s, (SAMPLE_BATCH_BLOCK, 1))
    x_pad = jnp.pad(x_sample, ((0, 0), (0, tpad - nt), (0, 0)))
    ckt = cache_k[0].reshape(nb, WINDOW, KV_WIDTH).transpose(0, 2, 1)
    cvt = cache_v[0].reshape(nb, WINDOW, KV_WIDTH).transpose(0, 2, 1)
    x1_pad, nkt, nvt, c_new, n_new, m_pad = _sample_mixer(
        x_pad, ckt, cvt, state_C[0], state_n[0], state_m[0][:, :, None], cos_s, sin_s, wq_s, w, wgate,
        wout_s, sink_tile, bd_s, anorm, qg, kg, gbias, mnorm)
    y_p, y_s = _ffn(x1_p, x1_pad[:, :nt].reshape(nb * nt, D_MODEL), fnorm, w_gate[0], w_up[0], w_down[0])
    m_new = m_pad[:, :nh, 0]

    new_k_s = nkt.transpose(0, 2, 1)
    new_v_s = nvt.transpose(0, 2, 1)

    kv_shape = (1, 1, WINDOW, KV_HEADS, ATTN_HEAD_DIM)
    return (
        y_p[None],
        y_s.reshape(nb, nt, D_MODEL),
        k_p.reshape(kv_shape),
        v_p.reshape(kv_shape),
        cext_p[None, None, :, :, :MLSTM_HEAD_DIM],
        cext_p[None, None, :, :, MLSTM_HEAD_DIM],
        m_p[None, None, :nh, 0],
        new_k_s.reshape(1, nb, WINDOW, KV_HEADS, ATTN_HEAD_DIM),
        new_v_s.reshape(1, nb, WINDOW, KV_HEADS, ATTN_HEAD_DIM),
        c_new.reshape(1, nb, nh, MLSTM_HEAD_DIM, MLSTM_HEAD_DIM),
        n_new.reshape(1, nb, nh, MLSTM_HEAD_DIM),
        m_new.reshape(1, nb, nh),
    )
```

```python
import jax
import jax.numpy as jnp
import numpy as np
from jax import lax
from jax.experimental import pallas as pl
from jax.experimental.pallas import tpu as pltpu

D_MODEL = 1024
PAST_LEN = 16384
ATTN_HEADS = 8
KV_HEADS = 2
ATTN_HEAD_DIM = 64
ATTN_GROUP = ATTN_HEADS // KV_HEADS
ATTN_WIDTH = ATTN_HEADS * ATTN_HEAD_DIM
KV_WIDTH = KV_HEADS * ATTN_HEAD_DIM
WINDOW = 128
ROPE_THETA = 10000.0
MLSTM_HEADS = 4
MLSTM_HEAD_DIM = 128
MLSTM_WIDTH = MLSTM_HEADS * MLSTM_HEAD_DIM
MIX_WIDTH = ATTN_WIDTH + MLSTM_WIDTH
D_FF = 2816
NORM_EPS = 1e-6

LANES = 128
SUBLANES = 8
VMEM_LIMIT = 56 * 1024 * 1024

COL_QA = 0
COL_KA = COL_QA + ATTN_WIDTH
COL_VA = COL_KA + KV_WIDTH
COL_QM = COL_VA + KV_WIDTH
COL_KM = COL_QM + MLSTM_WIDTH
COL_VM = COL_KM + MLSTM_WIDTH
COL_OM = COL_VM + MLSTM_WIDTH
COL_G = COL_OM + MLSTM_WIDTH
IN_WIDTH = COL_G + 2 * MLSTM_HEADS
FG_LANE = SUBLANES

PROMPT_BLOCK = 256
QBLOCK = WINDOW
MCHUNK = 128
PROJ_CHUNK = 256
FFN_BLOCK = 512
SAMPLE_BATCH_BLOCK = 16
SAMPLE_TOKENS = 4
NEG = -1e30

f32 = jnp.float32
bf16 = jnp.bfloat16


def _rms(x, gain):
    return x * lax.rsqrt(jnp.mean(x * x, axis=-1, keepdims=True) + NORM_EPS) * gain


def _segsum64(s, lane):
    for k in (1, 2, 4, 8, 16, 32):
        s = s + jnp.where((lane & k) != 0, pltpu.roll(s, k, 1), pltpu.roll(s, LANES - k, 1))
    return s


def _headnorm_rope(xs, gain, cos, sin_signed, lane):
    ss = _segsum64(xs * xs, lane)
    y = xs * lax.rsqrt(ss * (1.0 / ATTN_HEAD_DIM) + NORM_EPS) * gain
    partner = jnp.where((lane & 32) != 0, pltpu.roll(y, 32, 1), pltpu.roll(y, LANES - 32, 1))
    return y * cos + partner * sin_signed


def _group_sumsq(xs, bd_ref):
    x2 = xs * xs
    hi = x2.astype(bf16)
    lo = (x2 - hi.astype(f32)).astype(bf16)
    return (jnp.dot(hi, bd_ref[...], preferred_element_type=f32)
            + jnp.dot(lo, bd_ref[...], preferred_element_type=f32))


PA_Q = 0
PA_K = PA_Q + ATTN_WIDTH
PA_V = PA_K + KV_HEADS * LANES
PA_WIDTH = PA_V + KV_HEADS * LANES
QUARTER = ATTN_HEAD_DIM // 2
QA_Q = 0
QA_K = QA_Q + 2 * ATTN_WIDTH
QA_V = QA_K + KV_HEADS * LANES
QA_VZ = QA_V + KV_HEADS * LANES
QA_WIDTH = QA_VZ + KV_HEADS * LANES
ZM_WIDTH = 4 * MLSTM_WIDTH
ATTN_PHASES = 4
MLSTM_PHASES = 5


def _norm_rope_quarters(xs, ss, gain, cos, sin_signed):
    y = xs * lax.rsqrt(ss * (1.0 / ATTN_HEAD_DIM) + NORM_EPS) * gain
    return y * cos + pltpu.roll(y, LANES // 2, 1) * sin_signed


def _col_chunks(width):
    return [(c, min(c + PROJ_CHUNK, width)) for c in range(0, width, PROJ_CHUNK)]


def _mixer_proj_jobs(x_ref, anorm_ref, wa_ref, w_ref, wgate_ref, za_ref, zm_ref, gz_ref):
    h = _rms(x_ref[...], anorm_ref[...]).astype(bf16)

    def proj_job(w_src, wc0, z_ref, c0, c1):
        def run():
            z_ref[:, c0:c1] = jnp.dot(h, w_src[:, wc0 + c0:wc0 + c1], preferred_element_type=f32)
        return run

    return ([proj_job(wa_ref, 0, za_ref, c0, c1) for c0, c1 in _col_chunks(PA_WIDTH)]
            + [proj_job(w_ref, COL_QM, zm_ref, c0, c1) for c0, c1 in _col_chunks(ZM_WIDTH)]
            + [proj_job(wgate_ref, 0, gz_ref, 0, LANES)])


def _mixer_prep_jobs(za_ref, gz_ref, cos_ref, sin_ref, bd_ref, qg_ref, kg_ref, gbias_ref, qa_ref,
                     colf_ref, urow_ref, wrow_ref, mst_ref, mout_ref):
    tb = za_ref.shape[0]

    def gates_job():
        lane_t = lax.broadcasted_iota(jnp.int32, (tb, LANES), 1)
        gcol = gz_ref[...] + gbias_ref[...]
        acol = jnp.where(lane_t < FG_LANE, gcol, jax.nn.log_sigmoid(gcol))
        arow = acol.T
        lane8 = lax.broadcasted_iota(jnp.int32, (SUBLANES, LANES), 1)
        nsb = tb // LANES
        slabs = [slice(sb * LANES, (sb + 1) * LANES) for sb in range(nsb)]
        ig_all = jnp.concatenate([arow[0:SUBLANES, ls] for ls in slabs], axis=0)
        b_all = jnp.concatenate([arow[FG_LANE:FG_LANE + SUBLANES, ls] for ls in slabs], axis=0)
        lane_in = lax.broadcasted_iota(jnp.int32, b_all.shape, 1) & (MCHUNK - 1)

        def scan(x, combine, identity):
            k = 1
            while k < MCHUNK:
                terms = [jnp.where(lane_in >= j * k, pltpu.roll(x, j * k, 1), identity)
                         for j in (1, 2, 3) if j * k < MCHUNK]
                for t in terms:
                    x = combine(x, t)
                k *= 4
            return x

        b_all = scan(b_all, jnp.add, 0.0)
        u_all = ig_all - b_all
        cm_all = scan(u_all, jnp.maximum, NEG)
        m_prev = mst_ref[:, 0:1]
        stacks = []
        ends = {}
        for sb in range(nsb):
            sub = slice(sb * SUBLANES, (sb + 1) * SUBLANES)
            for c in range(LANES // MCHUNK):
                at_end = lane8 == c * MCHUNK + MCHUNK - 1
                ends[sb, c] = (jnp.max(jnp.where(at_end, cm_all[sub], NEG), axis=1, keepdims=True),
                               jnp.max(jnp.where(at_end, b_all[sub], NEG), axis=1, keepdims=True))
        for sb, ls in enumerate(slabs):
            sub = slice(sb * SUBLANES, (sb + 1) * SUBLANES)
            b8, u8, cm8 = b_all[sub], u_all[sub], cm_all[sub]
            g8 = jnp.zeros_like(u8)
            mp8 = jnp.zeros_like(u8)
            gl8 = jnp.zeros_like(u8)
            for c in range(LANES // MCHUNK):
                in_chunk = (lane8 // MCHUNK) == c
                gc = jnp.maximum(cm8, m_prev)
                cm_last, b_last = ends[sb, c]
                g_last = jnp.maximum(cm_last, m_prev)
                g8 = jnp.where(in_chunk, gc, g8)
                mp8 = jnp.where(in_chunk, m_prev, mp8)
                gl8 = jnp.where(in_chunk, g_last, gl8)
                m_prev = b_last + g_last
            a8 = jnp.exp(mp8 - g8)
            emt8 = jnp.exp(-(b8 + g8))
            aend8 = jnp.exp(mp8 - gl8)
            stacks.append(jnp.concatenate(
                [g8, a8, emt8, aend8, jnp.zeros((LANES - 4 * SUBLANES, LANES), f32)], axis=0))
            urow_ref[:, ls] = u8
            wrow_ref[:, ls] = jnp.exp(u8 - gl8)
        mst_ref[...] = jnp.broadcast_to(m_prev, mst_ref.shape)
        mout_ref[...] = jnp.broadcast_to(m_prev, mout_ref.shape)
        colf_ref[...] = jnp.concatenate(stacks, axis=1).T

    def rope_job(qb):
        def run():
            rows = slice(qb * QBLOCK, (qb + 1) * QBLOCK)
            lane = lax.broadcasted_iota(jnp.int32, (QBLOCK, LANES), 1)
            head_a = ((lane // QUARTER) & 1) == 0
            row0 = lax.broadcasted_iota(jnp.int32, (QBLOCK, LANES), 0) == 0
            cos = cos_ref[rows, :]
            sin = sin_ref[rows, :]
            ss = [_group_sumsq(za_ref[rows, d * 2 * LANES:(d + 1) * 2 * LANES], bd_ref)
                  for d in range(PA_V // (2 * LANES))]
            for j in range(PA_V // LANES):
                is_q = j < ATTN_WIDTH // LANES
                y = _norm_rope_quarters(za_ref[rows, j * LANES:(j + 1) * LANES],
                                        ss[j // 2][:, (j % 2) * LANES:(j % 2 + 1) * LANES],
                                        qg_ref[...] if is_q else kg_ref[...], cos, sin)
                if is_q:
                    y = y * (ATTN_HEAD_DIM ** -0.5)
                    qa_ref[rows, QA_Q + 2 * j * LANES:QA_Q + (2 * j + 1) * LANES] = (
                        jnp.where(head_a, y, 0.0).astype(bf16))
                    qa_ref[rows, QA_Q + (2 * j + 1) * LANES:QA_Q + (2 * j + 2) * LANES] = (
                        jnp.where(head_a, 0.0, y).astype(bf16))
                else:
                    c = j - ATTN_WIDTH // LANES
                    qa_ref[rows, QA_K + c * LANES:QA_K + (c + 1) * LANES] = y.astype(bf16)
            for c in range(KV_HEADS):
                v = za_ref[rows, PA_V + c * LANES:PA_V + (c + 1) * LANES]
                qa_ref[rows, QA_V + c * LANES:QA_V + (c + 1) * LANES] = v.astype(bf16)
                qa_ref[rows, QA_VZ + c * LANES:QA_VZ + (c + 1) * LANES] = (
                    jnp.where(row0, 0.0, v).astype(bf16))
        return run

    return [gates_job] + [rope_job(qb) for qb in range(tb // QBLOCK)]


def _mixer_outproj_jobs(xs_ref, mix_ref, wout_ref, x1_ref):
    def job(c0, c1):
        def run():
            x1_ref[:, c0:c1] = xs_ref[:, c0:c1] + jnp.dot(mix_ref[...], wout_ref[:, c0:c1],
                                                          preferred_element_type=f32)
        return run

    return [job(c0, c1) for c0, c1 in _col_chunks(D_MODEL)]


def _mixer_window_out(xs_ref, anorm_ref, w_ref, kgs_ref, coss_ref, sins_ref, kout_ref, vout_ref):
    tb = xs_ref.shape[0]
    h = _rms(xs_ref[tb - WINDOW:, :], anorm_ref[...]).astype(bf16)
    zs = jnp.dot(h, w_ref[:, COL_KA:COL_QM], preferred_element_type=f32)
    lane_s = lax.broadcasted_iota(jnp.int32, (WINDOW, LANES), 1)
    kout_ref[...] = _headnorm_rope(zs[:, :KV_WIDTH], kgs_ref[...], coss_ref[...], sins_ref[...], lane_s)
    vout_ref[...] = zs[:, KV_WIDTH:]


def _mixer_core(first_block, last_block, fillers, prep_jobs, qa_ref, zm_ref, colf_ref, urow_ref, wrow_ref,
                mix_ref, bias_ref, mnorm_ref, kprev_ref, vprev_ref, cst_ref, cext_ref):
    tb = qa_ref.shape[0]
    nqb = tb // QBLOCK
    nch = tb // MCHUNK
    fillers = list(fillers)
    n_fill = len(fillers)
    slots = len(prep_jobs) + ATTN_PHASES + nch * MLSTM_PHASES
    progress = [0]

    def fill():
        progress[0] += 1
        while n_fill - len(fillers) < min(n_fill, -(-n_fill * progress[0] // slots)):
            fillers.pop(0)()

    low_half = lax.broadcasted_iota(jnp.int32, (QBLOCK, LANES), 1) < ATTN_HEAD_DIM
    qi = lax.broadcasted_iota(jnp.int32, (ATTN_GROUP * QBLOCK, 2 * QBLOCK), 0) & (QBLOCK - 1)
    kj = lax.broadcasted_iota(jnp.int32, (ATTN_GROUP * QBLOCK, 2 * QBLOCK), 1)
    band = (kj > qi) & (kj <= qi + QBLOCK)
    ones_slab = jnp.ones((2 * QBLOCK, LANES), bf16)

    def attn_phases(chains):
        rows_of = lambda qb: slice(qb * QBLOCK, (qb + 1) * QBLOCK)
        st = {}

        def scores():
            for qb, c in chains:
                rows = rows_of(qb)
                kcols = slice(QA_K + c * LANES, QA_K + (c + 1) * LANES)
                if qb == 0:
                    kprev, vprev = kprev_ref[c], vprev_ref[c]
                else:
                    kprev = qa_ref[rows_of(qb - 1), kcols]
                    vprev = qa_ref[rows_of(qb - 1), QA_VZ + c * LANES:QA_VZ + (c + 1) * LANES]
                kcat = jnp.concatenate([kprev, qa_ref[rows, kcols]], axis=0)
                vcat = jnp.concatenate(
                    [vprev, qa_ref[rows, QA_V + c * LANES:QA_V + (c + 1) * LANES]], axis=0)
                st['vext', qb, c] = jnp.concatenate([vcat, ones_slab], axis=1)
                q0 = QA_Q + c * ATTN_GROUP * LANES
                qst = jnp.concatenate([qa_ref[rows, q0 + g * LANES:q0 + (g + 1) * LANES]
                                       for g in range(ATTN_GROUP)], axis=0)
                st['s', qb, c] = lax.dot_general(qst, kcat, (((1,), (1,)), ((), ())),
                                                 preferred_element_type=f32)

        def softmax():
            for qb, c in chains:
                valid = band & (kj >= QBLOCK) if (first_block and qb == 0) else band
                s = jnp.where(valid, st.pop(('s', qb, c)), bias_ref[c])
                st['p', qb, c] = jnp.exp(s - jnp.max(s, axis=-1, keepdims=True)).astype(bf16)

        def values():
            for qb, c in chains:
                st['of', qb, c] = jnp.dot(st.pop(('p', qb, c)), st.pop(('vext', qb, c)),
                                          preferred_element_type=f32)

        def normalise():
            for qb, c in chains:
                of = st.pop(('of', qb, c))
                o = of[:, :LANES] / of[:, LANES:]
                for jj in range(2):
                    pair = jnp.where(low_half, o[(2 * jj) * QBLOCK:(2 * jj + 1) * QBLOCK],
                                     o[(2 * jj + 1) * QBLOCK:(2 * jj + 2) * QBLOCK])
                    col = (2 * c + jj) * LANES
                    mix_ref[rows_of(qb), col:col + LANES] = pair.astype(bf16)

        return [scores, softmax, values, normalise]

    ti = lax.broadcasted_iota(jnp.int32, (MCHUNK, MCHUNK), 0)
    si = lax.broadcasted_iota(jnp.int32, (MCHUNK, MCHUNK), 1)
    causal = si <= ti
    ones_l = jnp.ones((MCHUNK, LANES), bf16)

    cexts = [cst_ref[hd] for hd in range(MLSTM_HEADS)]

    def mlstm_phases(c):
        rows = slice(c * MCHUNK, (c + 1) * MCHUNK)
        heads = range(MLSTM_HEADS)
        hcols = lambda k, hd: slice((k * MLSTM_HEADS + hd) * MLSTM_HEAD_DIM,
                                    (k * MLSTM_HEADS + hd + 1) * MLSTM_HEAD_DIM)
        st = {}

        def scores():
            for hd in heads:
                st['q', hd] = zm_ref[rows, hcols(0, hd)].astype(bf16)
                kf = zm_ref[rows, hcols(1, hd)] * (MLSTM_HEAD_DIM ** -0.5)
                st['kt', hd] = kf.T
                st['v', hd] = jnp.concatenate([zm_ref[rows, hcols(2, hd)].astype(bf16), ones_l], axis=1)
                st['s', hd] = lax.dot_general(st['q', hd], kf.astype(bf16),
                                              (((1,), (1,)), ((), ())), preferred_element_type=f32)

        def decay():
            for hd in heads:
                g_c = colf_ref[rows, hd:hd + 1]
                u_r = urow_ref[hd:hd + 1, rows]
                dmat = jnp.exp(jnp.where(causal, u_r - g_c, NEG))
                st['s', hd] = (st['s', hd] * dmat).astype(bf16)
                st['kt', hd] = (st['kt', hd] * wrow_ref[hd:hd + 1, rows]).astype(bf16)

        def readout():
            for hd in heads:
                a_c = colf_ref[rows, SUBLANES + hd:SUBLANES + hd + 1]
                st['nd', hd] = (
                    a_c * jnp.dot(st.pop(('q', hd)), cexts[hd].astype(bf16), preferred_element_type=f32)
                    + jnp.dot(st.pop(('s', hd)), st['v', hd], preferred_element_type=f32))

        def emit():
            for hd in heads:
                nd = st.pop(('nd', hd))
                emt_c = colf_ref[rows, 2 * SUBLANES + hd:2 * SUBLANES + hd + 1]
                hraw = nd[:, :MLSTM_HEAD_DIM] / jnp.maximum(jnp.abs(nd[:, MLSTM_HEAD_DIM:]), emt_c)
                hn = _rms(hraw, mnorm_ref[hd:hd + 1, :])
                og = zm_ref[rows, hcols(3, hd)]
                mix_ref[rows, ATTN_WIDTH + hd * MLSTM_HEAD_DIM:ATTN_WIDTH + (hd + 1) * MLSTM_HEAD_DIM] = (
                    (hn * jax.nn.sigmoid(og)).astype(bf16))

        def update():
            for hd in heads:
                aend = colf_ref[c * MCHUNK:c * MCHUNK + 1, 3 * SUBLANES + hd:3 * SUBLANES + hd + 1]
                cexts[hd] = aend * cexts[hd] + jnp.dot(st.pop(('kt', hd)), st.pop(('v', hd)),
                                                       preferred_element_type=f32)

        return [scores, decay, readout, emit, update]

    gates_job, rope_jobs = prep_jobs[0], prep_jobs[1:]
    attn = attn_phases([(qb, c) for qb in range(nqb) for c in range(KV_HEADS)])
    chunks = [mlstm_phases(c) for c in range(nch)]
    mlstm = [p for scores, decay, _, _, _ in chunks for p in (scores, decay)]
    for c, (_, _, readout, emit, update) in enumerate(chunks):
        mlstm += [readout] + ([chunks[c - 1][3]] if c else []) + [update]
    mlstm.append(chunks[-1][3])
    assert len(attn) == ATTN_PHASES and len(mlstm) == nch * MLSTM_PHASES
    order = [gates_job] + rope_jobs + attn[:1]
    rest = attn[1:]
    for i in range(max(len(rest), len(mlstm))):
        order += rest[i:i + 1] + mlstm[i:i + 1]
    assert len(order) == slots
    for job in order:
        job()
        fill()
    last_rows = slice(tb - QBLOCK, tb)
    for c in range(KV_HEADS):
        kprev_ref[c] = qa_ref[last_rows, QA_K + c * LANES:QA_K + (c + 1) * LANES]
        vprev_ref[c] = qa_ref[last_rows, QA_VZ + c * LANES:QA_VZ + (c + 1) * LANES]
    for hd in range(MLSTM_HEADS):
        cst_ref[hd] = cexts[hd]
        if last_block:
            cext_ref[hd] = cexts[hd]


def _prompt_mixer_kernel(x_ref, cos_ref, sin_ref, wa_ref, w_ref, wgate_ref, wout_ref, bd_ref, bias_ref,
                         anorm_ref, qg_ref, kg_ref, kgs_ref, coss_ref, sins_ref, gbias_ref, mnorm_ref,
                         x1_ref, kout_ref, vout_ref, cext_ref, mout_ref,
                         za0, za1, zm0, zm1, gz0, gz1, mix0, mix1, xs0, xs1,
                         qa_ref, colf_ref, urow_ref, wrow_ref, kprev_ref, vprev_ref, cst_ref, mst_ref):
    step = pl.program_id(0)
    nblk = pl.num_programs(0) - 2
    za, zm, gz, mix, xs = (za0, za1), (zm0, zm1), (gz0, gz1), (mix0, mix1), (xs0, xs1)

    def run(parity, do_in, do_core, do_out, first_block=False, last_block=False):
        other = 1 - parity
        jobs = []
        if do_out:
            jobs += _mixer_outproj_jobs(xs[parity], mix[parity], wout_ref, x1_ref)
        if do_in:
            jobs += _mixer_proj_jobs(x_ref, anorm_ref, wa_ref, w_ref, wgate_ref, za[parity], zm[parity],
                                     gz[parity])
        if do_core:
            prep = _mixer_prep_jobs(za[other], gz[other], cos_ref, sin_ref, bd_ref, qg_ref, kg_ref,
                                    gbias_ref, qa_ref, colf_ref, urow_ref, wrow_ref, mst_ref, mout_ref)
            _mixer_core(first_block, last_block, jobs, prep, qa_ref, zm[other], colf_ref, urow_ref,
                        wrow_ref, mix[other], bias_ref, mnorm_ref, kprev_ref, vprev_ref, cst_ref, cext_ref)
        else:
            for job in jobs:
                job()
        if last_block:
            _mixer_window_out(xs[other], anorm_ref, w_ref, kgs_ref, coss_ref, sins_ref, kout_ref, vout_ref)
        if do_in:
            xs[parity][...] = x_ref[...]

    @pl.when(step == 0)
    def _first():
        kprev_ref[...] = jnp.zeros_like(kprev_ref)
        vprev_ref[...] = jnp.zeros_like(vprev_ref)
        cst_ref[...] = jnp.zeros_like(cst_ref)
        mst_ref[...] = jnp.zeros_like(mst_ref)
        run(0, True, False, False)

    @pl.when(step == 1)
    def _second():
        run(1, True, True, False, first_block=True)

    steady = (step >= 2) & (step < nblk)

    @pl.when(steady & (step % 2 == 0))
    def _even():
        run(0, True, True, True)

    @pl.when(steady & (step % 2 == 1))
    def _odd():
        run(1, True, True, True)

    @pl.when(step == nblk)
    def _drain_core():
        run(0, False, True, True, last_block=True)

    @pl.when(step == nblk + 1)
    def _drain_out():
        run(1, False, False, True)


def _const_spec(shape, single=False):
    nd = len(shape)
    if single:
        return pl.BlockSpec(shape, lambda i, *_: (0,) * nd, pipeline_mode=pl.Buffered(1))
    return pl.BlockSpec(shape, lambda i, *_: (0,) * nd)


def _prompt_mixer(x, cos, sin, wa, w, wgate, wout_b, bd, bias, anorm, qg, kg, kgs, coss, sins, gbias, mnorm):
    t = x.shape[0]
    tb = PROMPT_BLOCK
    nblk = t // tb
    assert nblk % 2 == 0 and nblk >= 4
    state_shape = (MLSTM_HEADS, MLSTM_HEAD_DIM, 2 * MLSTM_HEAD_DIM)
    last = nblk - 1
    lag = lambda d: (lambda i: (jnp.clip(i - d, 0, last), 0))
    return pl.pallas_call(
        _prompt_mixer_kernel,
        grid=(nblk + 2,),
        in_specs=[
            pl.BlockSpec((tb, D_MODEL), lag(0)),
            pl.BlockSpec((tb, LANES), lag(1)),
            pl.BlockSpec((tb, LANES), lag(1)),
            _const_spec((D_MODEL, PA_WIDTH), single=True),
            _const_spec((D_MODEL, IN_WIDTH), single=True),
            _const_spec((D_MODEL, LANES), single=True),
            _const_spec((MIX_WIDTH, D_MODEL), single=True),
            _const_spec((2 * LANES, 2 * LANES), single=True),
            _const_spec((KV_HEADS, ATTN_GROUP * QBLOCK, 2 * QBLOCK), single=True),
            _const_spec((1, D_MODEL)),
            _const_spec((1, LANES)),
            _const_spec((1, LANES)),
            _const_spec((1, LANES)),
            _const_spec((WINDOW, LANES)),
            _const_spec((WINDOW, LANES)),
            _const_spec((1, LANES)),
            _const_spec((MLSTM_HEADS, MLSTM_HEAD_DIM)),
        ],
        out_specs=[
            pl.BlockSpec((tb, D_MODEL), lag(2)),
            _const_spec((WINDOW, KV_WIDTH)),
            _const_spec((WINDOW, KV_WIDTH)),
            _const_spec(state_shape),
            _const_spec((SUBLANES, LANES)),
        ],
        out_shape=[
            jax.ShapeDtypeStruct((t, D_MODEL), f32),
            jax.ShapeDtypeStruct((WINDOW, KV_WIDTH), f32),
            jax.ShapeDtypeStruct((WINDOW, KV_WIDTH), f32),
            jax.ShapeDtypeStruct(state_shape, f32),
            jax.ShapeDtypeStruct((SUBLANES, LANES), f32),
        ],
        scratch_shapes=(
            [pltpu.VMEM((tb, PA_WIDTH), f32)] * 2 + [pltpu.VMEM((tb, ZM_WIDTH), f32)] * 2
            + [pltpu.VMEM((tb, LANES), f32)] * 2
            + [pltpu.VMEM((tb, MIX_WIDTH), bf16)] * 2 + [pltpu.VMEM((tb, D_MODEL), f32)] * 2
            + [pltpu.VMEM((tb, QA_WIDTH), bf16), pltpu.VMEM((tb, LANES), f32)]
            + [pltpu.VMEM((SUBLANES, tb), f32)] * 2
            + [pltpu.VMEM((KV_HEADS, WINDOW, LANES), bf16)] * 2
            + [pltpu.VMEM(state_shape, f32), pltpu.VMEM((SUBLANES, LANES), f32)]),
        compiler_params=pltpu.CompilerParams(
            dimension_semantics=("arbitrary",), vmem_limit_bytes=VMEM_LIMIT),
        name="prompt_mixer",
    )(x, cos, sin, wa, w, wgate, wout_b, bd, bias, anorm, qg, kg, kgs, coss, sins, gbias, mnorm)


def _ffn_kernel(xp_ref, xs_ref, g_ref, wg_ref, wu_ref, wd_ref, op_ref, os_ref, act_ref):
    step = pl.program_id(0)
    last = pl.num_programs(0) - 1

    @pl.when(step < last)
    def _prompt_rows():
        _ffn_rows(xp_ref, g_ref, wg_ref, wu_ref, wd_ref, op_ref, act_ref)

    @pl.when(step == last)
    def _sample_rows():
        _ffn_rows(xs_ref, g_ref, wg_ref, wu_ref, wd_ref, os_ref, act_ref)


def _ffn_rows(x_ref, g_ref, wg_ref, wu_ref, wd_ref, o_ref, act_ref):
    hf = _rms(x_ref[...], g_ref[...]).astype(bf16)
    for c0, c1 in _col_chunks(D_FF):
        gate = jnp.dot(hf, wg_ref[:, c0:c1], preferred_element_type=f32)
        up = jnp.dot(hf, wu_ref[:, c0:c1], preferred_element_type=f32)
        act_ref[:, c0:c1] = (gate * jax.nn.sigmoid(gate) * up).astype(bf16)
    for c0, c1 in _col_chunks(D_MODEL):
        o_ref[:, c0:c1] = x_ref[:, c0:c1] + jnp.dot(act_ref[...], wd_ref[:, c0:c1],
                                                    preferred_element_type=f32)


def _ffn(x_p, x_s, fnorm, w_gate, w_up, w_down):
    n = x_p.shape[0]
    ns = x_s.shape[0]
    tm = FFN_BLOCK
    last = n // tm - 1
    return pl.pallas_call(
        _ffn_kernel,
        grid=(n // tm + 1,),
        in_specs=[
            pl.BlockSpec((tm, D_MODEL), lambda i: (jnp.minimum(i, last), 0)),
            _const_spec((ns, D_MODEL), single=True),
            _const_spec((1, D_MODEL)),
            _const_spec((D_MODEL, D_FF), single=True),
            _const_spec((D_MODEL, D_FF), single=True),
            _const_spec((D_FF, D_MODEL), single=True),
        ],
        out_specs=[pl.BlockSpec((tm, D_MODEL), lambda i: (jnp.minimum(i, last), 0)),
                   _const_spec((ns, D_MODEL))],
        out_shape=[jax.ShapeDtypeStruct((n, D_MODEL), f32), jax.ShapeDtypeStruct((ns, D_MODEL), f32)],
        scratch_shapes=[pltpu.VMEM((tm, D_FF), bf16)],
        compiler_params=pltpu.CompilerParams(
            dimension_semantics=("arbitrary",), vmem_limit_bytes=VMEM_LIMIT),
        name="ffn",
    )(x_p, x_s, fnorm, w_gate, w_up, w_down)


def _sample_mixer_kernel(x_ref, ckt_ref, cvt_ref, c_ref, n_ref, m_ref, cos_ref, sin_ref, wqs_ref, w_ref,
                         wgate_ref, wout_ref, sink_ref, bd_ref, anorm_ref, qg_ref, kg_ref, gbias_ref,
                         mnorm_ref, x1_ref, nkt_ref, nvt_ref, cn_ref, nn_ref, mn_ref):
    bb, tpad, _ = x_ref.shape
    nrows = bb * tpad
    nreal = SAMPLE_TOKENS
    h = _rms(x_ref[...].reshape(nrows, D_MODEL), anorm_ref[...]).astype(bf16)
    z = jnp.concatenate(
        [jnp.dot(h, wqs_ref[...], preferred_element_type=f32),
         jnp.dot(h, w_ref[:, COL_KA:COL_G], preferred_element_type=f32),
         jnp.dot(h, wgate_ref[...], preferred_element_type=f32)], axis=1)
    lane = lax.broadcasted_iota(jnp.int32, (nrows, LANES), 1)
    low = lane < ATTN_HEAD_DIM
    cos = cos_ref[...]
    sin = sin_ref[...]

    def per_seq(a):
        return a.reshape(bb, tpad, a.shape[-1])

    def norm_rope(xs, gain):
        y = xs * lax.rsqrt(_group_sumsq(xs, bd_ref) * (1.0 / ATTN_HEAD_DIM) + NORM_EPS) * gain
        partner = jnp.where((lane & QUARTER) != 0, pltpu.roll(y, QUARTER, 1),
                            pltpu.roll(y, LANES - QUARTER, 1))
        return y * cos + partner * sin

    q_rows = []
    for j in range(ATTN_GROUP):
        qs = norm_rope(z[:, COL_QA + j * LANES:COL_QA + (j + 1) * LANES], qg_ref[...])
        qs = qs * (ATTN_HEAD_DIM ** -0.5)
        q_rows.append(per_seq(jnp.where(low, qs, 0.0)).astype(bf16))
        q_rows.append(per_seq(jnp.where(low, 0.0, qs)).astype(bf16))
    qbd = jnp.concatenate(q_rows, axis=1)
    knew = norm_rope(z[:, COL_KA:COL_KA + KV_WIDTH], kg_ref[...])
    vnew = z[:, COL_VA:COL_VA + KV_WIDTH]
    zpad = jnp.zeros((bb, LANES - tpad, LANES), bf16)
    knp = jnp.concatenate([per_seq(knew).astype(bf16), zpad], axis=1)
    vnp = jnp.concatenate([per_seq(vnew).astype(bf16), zpad], axis=1)
    ckt = ckt_ref[...]
    cvt = cvt_ref[...]
    s = jnp.concatenate(
        [jnp.einsum('bqd,bdw->bqw', qbd, ckt.astype(bf16), preferred_element_type=f32),
         jnp.einsum('bqd,bkd->bqk', qbd, knp, preferred_element_type=f32)], axis=2)
    tq = lax.broadcasted_iota(jnp.int32, s.shape, 1) & (tpad - 1)
    kj = lax.broadcasted_iota(jnp.int32, s.shape, 2)
    valid = ((kj < WINDOW) & (kj > tq)) | ((kj >= WINDOW) & (kj - WINDOW <= tq) & (kj - WINDOW < nreal))
    s = jnp.where(valid, s, NEG)
    sink = sink_ref[:, 0:1][None]
    mx = jnp.maximum(jnp.max(s, axis=-1, keepdims=True), sink)
    p = jnp.exp(s - mx)
    den = jnp.sum(p, axis=-1, keepdims=True) + jnp.exp(sink - mx)
    pb = p.astype(bf16)
    o = (jnp.einsum('bqw,bdw->bqd', pb[:, :, :WINDOW], cvt.astype(bf16), preferred_element_type=f32)
         + jnp.einsum('bqk,bkd->bqd', pb[:, :, WINDOW:], vnp, preferred_element_type=f32)) / den
    low3 = lax.broadcasted_iota(jnp.int32, (bb, tpad, LANES), 2) < ATTN_HEAD_DIM
    mix_parts = []
    for j in range(ATTN_GROUP):
        r0 = 2 * j * tpad
        pair = jnp.where(low3, o[:, r0:r0 + tpad, :], o[:, r0 + tpad:r0 + 2 * tpad, :])
        mix_parts.append(pair.reshape(nrows, LANES).astype(bf16))

    keep = lax.broadcasted_iota(jnp.int32, (KV_WIDTH, WINDOW), 1) < WINDOW - nreal
    knt = knew.T
    vnt = vnew.T
    for b in range(bb):
        shift = (WINDOW - nreal - b * tpad) % LANES
        nkt_ref[b] = jnp.where(keep, pltpu.roll(ckt_ref[b], WINDOW - nreal, 1), pltpu.roll(knt, shift, 1))
        nvt_ref[b] = jnp.where(keep, pltpu.roll(cvt_ref[b], WINDOW - nreal, 1), pltpu.roll(vnt, shift, 1))

    gz = per_seq(z[:, COL_G:COL_G + LANES] + gbias_ref[...])
    lgz = jax.nn.log_sigmoid(gz)
    trow = lax.broadcasted_iota(jnp.int32, (bb, tpad, 1), 1)
    real = trow < nreal
    mn_ref[...] = jnp.zeros_like(mn_ref)
    heads = range(MLSTM_HEADS)
    hcols = lambda base, hd: slice(base + hd * MLSTM_HEAD_DIM, base + (hd + 1) * MLSTM_HEAD_DIM)
    last = nreal - 1
    st = {}
    for hd in heads:
        m0 = m_ref[:, hd:hd + 1, :]
        ig_c = jnp.where(real, gz[:, :, hd:hd + 1], NEG)
        lf_c = jnp.where(real, lgz[:, :, FG_LANE + hd:FG_LANE + hd + 1], 0.0)
        b_c = jnp.zeros_like(lf_c)
        for sx in range(nreal):
            b_c = b_c + jnp.where(trow >= sx, lf_c[:, sx:sx + 1, :], 0.0)
        dlog = [jnp.where(trow >= sx, b_c - b_c[:, sx:sx + 1, :] + ig_c[:, sx:sx + 1, :], NEG)
                for sx in range(nreal)]
        inter = b_c + m0
        m_t = inter
        for sx in range(nreal):
            m_t = jnp.maximum(m_t, dlog[sx])
        m_new = m_t[:, last:last + 1, :]
        b_last = b_c[:, last:last + 1, :]
        st[hd] = dict(m_t=m_t, a=jnp.exp(inter - m_t), dexp=[jnp.exp(d - m_t) for d in dlog],
                      m_new=m_new, a_end=jnp.exp(b_last + m0 - m_new),
                      w_c=jnp.exp(b_last - b_c + ig_c - m_new))
    for hd in heads:
        s = st[hd]
        s['q'] = per_seq(z[:, hcols(COL_QM, hd)])
        s['k'] = per_seq(z[:, hcols(COL_KM, hd)]) * (MLSTM_HEAD_DIM ** -0.5)
        s['v'] = per_seq(z[:, hcols(COL_VM, hd)])
        s['qc'] = jnp.einsum('btd,bde->bte', s['q'].astype(bf16), c_ref[:, hd].astype(bf16),
                             preferred_element_type=f32)
    for hd in heads:
        s = st[hd]
        q, k, v = s['q'], s['k'], s['v']
        num = s['a'] * s.pop('qc')
        den_m = s['a'] * jnp.sum(q * n_ref[:, hd:hd + 1, :], axis=2, keepdims=True)
        for sx in range(nreal):
            sd = jnp.sum(q * k[:, sx:sx + 1, :], axis=2, keepdims=True) * s['dexp'][sx]
            num = num + sd * v[:, sx:sx + 1, :]
            den_m = den_m + sd
        hraw = num / jnp.maximum(jnp.abs(den_m), jnp.exp(-s['m_t']))
        hn = _rms(hraw, mnorm_ref[hd:hd + 1, :][None])
        og = per_seq(z[:, hcols(COL_OM, hd)])
        mix_parts.append((hn * jax.nn.sigmoid(og)).reshape(nrows, MLSTM_HEAD_DIM).astype(bf16))
    for hd in heads:
        s = st[hd]
        kw = s['k'] * s['w_c']
        cn_ref[:, hd] = s['a_end'] * c_ref[:, hd] + jnp.einsum(
            'bsd,bse->bde', kw.astype(bf16), s['v'].astype(bf16), preferred_element_type=f32)
        nn_ref[:, hd:hd + 1, :] = s['a_end'] * n_ref[:, hd:hd + 1, :] + jnp.sum(kw, axis=1, keepdims=True)
        mn_ref[:, hd:hd + 1, :] = jnp.broadcast_to(s['m_new'], (bb, 1, LANES))

    mix = jnp.concatenate(mix_parts, axis=1)
    x1 = x_ref[...].reshape(nrows, D_MODEL) + jnp.dot(mix, wout_ref[...], preferred_element_type=f32)
    x1_ref[...] = x1.reshape(bb, tpad, D_MODEL)


def _sample_mixer(x_pad, ckt, cvt, c0, n0, m0, cos, sin, wq_s, w, wgate, wout_s, sink_tile, bd, anorm, qg,
                  kg, gbias, mnorm):
    nb, tpad, _ = x_pad.shape
    bb = SAMPLE_BATCH_BLOCK
    assert bb * tpad == LANES and nb % bb == 0
    nh = MLSTM_HEADS
    blk = lambda shape: pl.BlockSpec(shape, lambda i: (i,) + (0,) * (len(shape) - 1))
    cblk = (bb, nh, MLSTM_HEAD_DIM, MLSTM_HEAD_DIM)
    return pl.pallas_call(
        _sample_mixer_kernel,
        grid=(nb // bb,),
        in_specs=[blk((bb, tpad, D_MODEL)), blk((bb, KV_WIDTH, WINDOW)), blk((bb, KV_WIDTH, WINDOW)),
                  blk(cblk), blk((bb, nh, MLSTM_HEAD_DIM)), blk((bb, nh, 1)),
                  _const_spec((bb * tpad, LANES)), _const_spec((bb * tpad, LANES)),
                  _const_spec((D_MODEL, ATTN_WIDTH), single=True),
                  _const_spec((D_MODEL, IN_WIDTH), single=True),
                  _const_spec((D_MODEL, LANES), single=True),
                  _const_spec((MIX_WIDTH, D_MODEL), single=True),
                  _const_spec((ATTN_HEADS * tpad, LANES)), _const_spec((LANES, LANES)),
                  _const_spec((1, D_MODEL)),
                  _const_spec((1, LANES)), _const_spec((1, LANES)), _const_spec((1, LANES)),
                  _const_spec((nh, MLSTM_HEAD_DIM))],
        out_specs=[blk((bb, tpad, D_MODEL)), blk((bb, KV_WIDTH, WINDOW)), blk((bb, KV_WIDTH, WINDOW)),
                   blk(cblk), blk((bb, nh, MLSTM_HEAD_DIM)), blk((bb, tpad, LANES))],
        out_shape=[jax.ShapeDtypeStruct((nb, tpad, D_MODEL), f32),
                   jax.ShapeDtypeStruct((nb, KV_WIDTH, WINDOW), f32),
                   jax.ShapeDtypeStruct((nb, KV_WIDTH, WINDOW), f32),
                   jax.ShapeDtypeStruct((nb,) + cblk[1:], f32),
                   jax.ShapeDtypeStruct((nb, nh, MLSTM_HEAD_DIM), f32),
                   jax.ShapeDtypeStruct((nb, tpad, LANES), f32)],
        compiler_params=pltpu.CompilerParams(
            dimension_semantics=("arbitrary",), vmem_limit_bytes=VMEM_LIMIT),
        name="sample_mixer",
    )(x_pad, ckt, cvt, c0, n0, m0, cos, sin, wq_s, w, wgate, wout_s, sink_tile, bd, anorm, qg, kg, gbias,
      mnorm)


def _rope_angles(pos):
    half = ATTN_HEAD_DIM // 2
    inv = ROPE_THETA ** (-np.arange(half, dtype=np.float64) / half)
    ang = pos.astype(np.float64)[:, None] * inv[None, :]
    return np.cos(ang).astype(np.float32), np.sin(ang).astype(np.float32)


def _rope_tables(pos):
    c, s = _rope_angles(pos)
    cos = np.tile(c, (1, LANES // QUARTER))
    sin = np.tile(np.concatenate([-s, s], axis=1), (1, LANES // ATTN_HEAD_DIM))
    return cos, sin


def _rope_tables_quarters(pos):
    c, s = _rope_angles(pos)
    return np.tile(c, (1, LANES // QUARTER)), np.concatenate([-s, -s, s, s], axis=1)


def _quarters(a):
    lo, hi = a[..., :QUARTER], a[..., QUARTER:]
    return jnp.concatenate([lo, lo, hi, hi], axis=-1)


def _prompt_attn_weights(w):
    d = w.shape[0]
    wq = w[:, COL_QA:COL_KA].reshape(d, ATTN_WIDTH // LANES, 2, 2, QUARTER)
    wq = wq.transpose(0, 1, 3, 2, 4).reshape(d, ATTN_WIDTH)
    wk = _quarters(w[:, COL_KA:COL_VA].reshape(d, KV_HEADS, ATTN_HEAD_DIM)).reshape(d, KV_HEADS * LANES)
    wv = w[:, COL_VA:COL_QM].reshape(d, KV_HEADS, 1, ATTN_HEAD_DIM)
    wv = jnp.broadcast_to(wv, (d, KV_HEADS, 2, ATTN_HEAD_DIM)).reshape(d, KV_HEADS * LANES)
    return jnp.concatenate([wq, wk, wv], axis=1)


def kernel(x_prompt, x_sample, cache_k, cache_v, state_C, state_n, state_m, attn_norm, w_in, q_norm,
           k_norm, attn_sinks, b_ig, b_fg, mlstm_norm, w_out, ffn_norm, w_gate, w_up, w_down):
    assert w_in.shape[0] == 1 and x_prompt.shape[0] == 1
    tp = x_prompt.shape[1]
    nb, nt = x_sample.shape[0], x_sample.shape[1]
    assert nt == SAMPLE_TOKENS
    tpad = SUBLANES
    nh = MLSTM_HEADS

    w = w_in[0].astype(bf16)
    pad_a = jnp.zeros((D_MODEL, FG_LANE - nh), bf16)
    pad_b = jnp.zeros((D_MODEL, LANES - FG_LANE - nh), bf16)
    wgate = jnp.concatenate([w[:, COL_G:COL_G + nh], pad_a, w[:, COL_G + nh:], pad_b], axis=1)
    gbias = jnp.concatenate(
        [b_ig[0], jnp.zeros((FG_LANE - nh,), f32), b_fg[0], jnp.zeros((LANES - FG_LANE - nh,), f32)]
    ).reshape(1, LANES)
    wout_b = w_out[0].astype(bf16)
    anorm = attn_norm[0].reshape(1, D_MODEL)
    fnorm = ffn_norm[0].reshape(1, D_MODEL)
    qg = jnp.tile(q_norm[0], LANES // ATTN_HEAD_DIM).reshape(1, LANES)
    kg = jnp.tile(k_norm[0], LANES // ATTN_HEAD_DIM).reshape(1, LANES)
    mnorm = mlstm_norm[0].reshape(nh, MLSTM_HEAD_DIM)
    sinks = attn_sinks[0]

    wa = _prompt_attn_weights(w)
    idx = np.arange(2 * LANES)
    same = (idx[:, None] // LANES == idx[None, :] // LANES) & (
        (idx[:, None] // QUARTER) % 2 == (idx[None, :] // QUARTER) % 2)
    bd = jnp.asarray(same, dtype=bf16)
    sink_rows_p = jnp.repeat(sinks.reshape(KV_HEADS, ATTN_GROUP), QBLOCK, axis=1)
    bias = jnp.where(jnp.arange(2 * QBLOCK)[None, None, :] == 0, sink_rows_p[:, :, None], NEG)
    qgq = _quarters(q_norm[0]).reshape(1, LANES)
    kgq = _quarters(k_norm[0]).reshape(1, LANES)
    pos_p = np.arange(tp, dtype=np.float32)
    cos_p, sin_p = _rope_tables_quarters(pos_p)
    cos_w, sin_w = _rope_tables(pos_p[tp - WINDOW:])
    x1_p, k_p, v_p, cext_p, m_p = _prompt_mixer(
        x_prompt[0], cos_p, sin_p, wa, w, wgate, wout_b, bd, bias, anorm, qgq, kgq, kg, cos_w, sin_w,
        gbias, mnorm)

    wq_s = w[:, COL_QA:COL_KA].reshape(D_MODEL, KV_HEADS, ATTN_GROUP, ATTN_HEAD_DIM)
    wq_s = wq_s.transpose(0, 2, 1, 3).reshape(D_MODEL, ATTN_WIDTH)
    wo_a = wout_b[:ATTN_WIDTH].reshape(KV_HEADS, ATTN_GROUP, ATTN_HEAD_DIM, D_MODEL)
    wo_a = wo_a.transpose(1, 0, 2, 3).reshape(ATTN_WIDTH, D_MODEL)
    wout_s = jnp.concatenate([wo_a, wout_b[ATTN_WIDTH:]], axis=0)
    sink_tile = jnp.broadcast_to(
        jnp.repeat(sinks.reshape(KV_HEADS, ATTN_GROUP).T.reshape(-1), tpad)[:, None],
        (ATTN_HEADS * tpad, LANES))
    lanes = np.arange(LANES)
    bd_s = jnp.asarray(lanes[:, None] // ATTN_HEAD_DIM == lanes[None, :] // ATTN_HEAD_DIM, dtype=bf16)
    cos_s, sin_s = _rope_tables(np.arange(tpad, dtype=np.float32) + np.float32(PAST_LEN))
    cos_s = np.tile(cos_s, (SAMPLE_BATCH_BLOCK, 1))
    sin_s = np.tile(sin_s, (SAMPLE_BATCH_BLOCK, 1))
    x_pad = jnp.pad(x_sample, ((0, 0), (0, tpad - nt), (0, 0)))
    ckt = cache_k[0].reshape(nb, WINDOW, KV_WIDTH).transpose(0, 2, 1)
    cvt = cache_v[0].reshape(nb, WINDOW, KV_WIDTH).transpose(0, 2, 1)
    x1_pad, nkt, nvt, c_new, n_new, m_pad = _sample_mixer(
        x_pad, ckt, cvt, state_C[0], state_n[0], state_m[0][:, :, None], cos_s, sin_s, wq_s, w, wgate,
        wout_s, sink_tile, bd_s, anorm, qg, kg, gbias, mnorm)
    y_p, y_s = _ffn(x1_p, x1_pad[:, :nt].reshape(nb * nt, D_MODEL), fnorm, w_gate[0], w_up[0], w_down[0])
    m_new = m_pad[:, :nh, 0]

    new_k_s = nkt.transpose(0, 2, 1)
    new_v_s = nvt.transpose(0, 2, 1)

    kv_shape = (1, 1, WINDOW, KV_HEADS, ATTN_HEAD_DIM)
    return (
        y_p[None],
        y_s.reshape(nb, nt, D_MODEL),
        k_p.reshape(kv_shape),
        v_p.reshape(kv_shape),
        cext_p[None, None, :, :, :MLSTM_HEAD_DIM],
        cext_p[None, None, :, :, MLSTM_HEAD_DIM],
        m_p[None, None, :nh, 0],
        new_k_s.reshape(1, nb, WINDOW, KV_HEADS, ATTN_HEAD_DIM),
        new_v_s.reshape(1, nb, WINDOW, KV_HEADS, ATTN_HEAD_DIM),
        c_new.reshape(1, nb, nh, MLSTM_HEAD_DIM, MLSTM_HEAD_DIM),
        n_new.reshape(1, nb, nh, MLSTM_HEAD_DIM),
        m_new.reshape(1, nb, nh),
    )
```
